```python
import jax, jax.numpy as jnp
from jax import lax
import numpy as np

D_MODEL = 1024
BATCH = 1
SEQ = 16384
DEPTH = 2
DEC_BATCH = 128
DEC_SEQ = 4
PAST_LEN = 16384
PAGE_SIZE = 128

PLE_DIM = 256
N_CONV_LAYERS = (DEPTH + 1) // 2
N_ATTN_LAYERS = DEPTH // 2
CONV_WIDTH = 31
CONV_STATE = CONV_WIDTH - 1
N_HEADS = 16
N_KV_HEADS = 4
HEAD_DIM = D_MODEL // N_HEADS
GROUP = N_HEADS // N_KV_HEADS
WINDOW = 128
QKV_DIM = (N_HEADS + 2 * N_KV_HEADS) * HEAD_DIM
N_EXPERT_GROUPS = 4
EXPERTS_PER_GROUP = 4
N_EXPERTS = N_EXPERT_GROUPS * EXPERTS_PER_GROUP
TOP_K = 2
D_EXPERT = 256
EPS = 1e-6
NEG_INF = -1e30

kernel_name = 'hybrid_conformer_swa_hmoe_decoder_step'


def rms_norm(x, g):
    xf = x.astype(jnp.float32)
    xf = xf * lax.rsqrt(jnp.mean(xf * xf, axis=-1, keepdims=True) + EPS)
    return xf.astype(x.dtype) * g


def layer_norm(x, g, b):
    xf = x.astype(jnp.float32)
    mu = jnp.mean(xf, axis=-1, keepdims=True)
    xc = xf - mu
    xf = xc * lax.rsqrt(jnp.mean(xc * xc, axis=-1, keepdims=True) + EPS)
    return xf.astype(x.dtype) * g + b


def alibi_slopes():
    return jnp.exp2(-8.0 * jnp.arange(1, N_HEADS + 1, dtype=jnp.float32) / N_HEADS)


def conv_mixer(h, u_prev, w_in, b_in, w_dw, b_dw, ln_g, ln_b, w_out, b_out):
    a, g = jnp.split(h @ w_in + b_in, 2, axis=-1)
    u = a * jax.nn.sigmoid(g)
    u_ext = jnp.concatenate([u_prev.astype(u.dtype), u], axis=1)
    c = lax.conv_general_dilated(u_ext, w_dw[:, None, :].astype(u.dtype), window_strides=(1,),
                                 padding='VALID', dimension_numbers=('NWC', 'WIO', 'NWC'),
                                 feature_group_count=D_MODEL) + b_dw
    c = layer_norm(c, ln_g, ln_b)
    y = jax.nn.silu(c) @ w_out + b_out
    return y, u_ext[:, -CONV_STATE:]


def qkv_project(h, w_qkv, q_gain, k_gain):
    B, T, _ = h.shape
    qkv = h @ w_qkv
    q, k, v = jnp.split(qkv, [N_HEADS * HEAD_DIM, (N_HEADS + N_KV_HEADS) * HEAD_DIM], axis=-1)
    q = rms_norm(q.reshape(B, T, N_HEADS, HEAD_DIM), q_gain)
    k = rms_norm(k.reshape(B, T, N_KV_HEADS, HEAD_DIM), k_gain)
    v = v.reshape(B, T, N_KV_HEADS, HEAD_DIM)
    return q, k, v


def banded_attention(q, k, v, key_valid, sinks):
    B, N, Lq = q.shape[:3]
    Lk = k.shape[2]
    qf = q.astype(jnp.float32).reshape(B, N, Lq, N_KV_HEADS, GROUP, HEAD_DIM)
    scores = jnp.einsum('bnqkgd,bnskd->bnkgqs', qf, k.astype(jnp.float32)) * (HEAD_DIM ** -0.5)
    dist = WINDOW + jnp.arange(Lq)[:, None] - jnp.arange(Lk)[None, :]
    slopes = alibi_slopes().reshape(N_KV_HEADS, GROUP)
    scores = scores - slopes[:, :, None, None] * dist.astype(jnp.float32)
    allowed = ((dist >= 0) & (dist <= WINDOW))[None] & key_valid[:, None, :]
    scores = jnp.where(allowed[None, :, None, None], scores, NEG_INF)
    sink = sinks.astype(jnp.float32).reshape(N_KV_HEADS, GROUP)[None, None, :, :, None]
    m = jnp.maximum(jnp.max(scores, axis=-1), sink)
    p = jnp.exp(scores - m[..., None])
    denom = jnp.sum(p, axis=-1) + jnp.exp(sink - m)
    out = jnp.einsum('bnkgqs,bnskd->bnqkgd', p, v.astype(jnp.float32))
    out = out / jnp.transpose(denom, (0, 1, 4, 2, 3))[..., None]
    return out.reshape(B, N * Lq, N_HEADS * HEAD_DIM)


def attn_prompt(h, w_qkv, q_gain, k_gain, sinks, w_o):
    B, T, _ = h.shape
    nb = T // WINDOW
    q, k, v = qkv_project(h, w_qkv, q_gain, k_gain)

    def band(a):
        a_pad = jnp.concatenate([jnp.zeros_like(a[:, :WINDOW]), a], axis=1)
        blocks = a_pad.reshape(B, nb + 1, WINDOW, N_KV_HEADS, HEAD_DIM)
        return jnp.concatenate([blocks[:, :-1], blocks[:, 1:]], axis=2)

    key_pos = (jnp.arange(nb)[:, None] - 1) * WINDOW + jnp.arange(2 * WINDOW)[None, :]
    o = banded_attention(q.reshape(B, nb, WINDOW, N_HEADS, HEAD_DIM), band(k), band(v), key_pos >= 0, sinks)
    y = o.astype(h.dtype) @ w_o
    return y, k[:, -WINDOW:], v[:, -WINDOW:]


def attn_sample(h, k_cache, v_cache, w_qkv, q_gain, k_gain, sinks, w_o):
    B, T, _ = h.shape
    q, k, v = qkv_project(h, w_qkv, q_gain, k_gain)
    k_all = jnp.concatenate([k_cache.astype(k.dtype), k], axis=1)
    v_all = jnp.concatenate([v_cache.astype(v.dtype), v], axis=1)
    key_valid = jnp.ones((1, WINDOW + T), dtype=bool)
    o = banded_attention(q[:, None], k_all[:, None], v_all[:, None], key_valid, sinks)
    y = o.astype(h.dtype) @ w_o
    return y, k_all[:, -WINDOW:], v_all[:, -WINDOW:]


def hier_moe(h, w_rg, b_rg, w_re, b_re, w_gate, w_up, w_down):
    B, T, D = h.shape
    xt = h.reshape(B * T, D)
    g_prob = jax.nn.softmax((xt @ w_rg + b_rg).astype(jnp.float32), axis=-1)
    g_w, g_idx = lax.top_k(g_prob, 1)
    e_logits = (xt @ w_re + b_re).astype(jnp.float32).reshape(-1, N_EXPERT_GROUPS, EXPERTS_PER_GROUP)
    e_logits = jnp.take_along_axis(e_logits, g_idx[:, :, None], axis=1)[:, 0]
    e_val, e_idx = lax.top_k(e_logits, TOP_K)
    e_w = jax.nn.softmax(e_val, axis=-1) * g_w
    expert = g_idx * EXPERTS_PER_GROUP + e_idx
    gate = jnp.sum(jax.nn.one_hot(expert, N_EXPERTS, dtype=jnp.float32) * e_w[..., None], axis=1)
    hid = jax.nn.silu(jnp.einsum('nd,edf->nef', xt, w_gate)) * jnp.einsum('nd,edf->nef', xt, w_up)
    y = jnp.einsum('nef,efd->nd', hid * gate[:, :, None].astype(hid.dtype), w_down)
    return y.reshape(B, T, D)


def per_layer_input(x, p, g_norm, w_gate, w_proj):
    return jax.nn.sigmoid(rms_norm(x, g_norm) @ w_gate) * (p @ w_proj)


def setup_inputs(seed: int = 0) -> dict:
    key = jax.random.key(seed)
    ks = iter(jax.random.split(key, 64))
    f32 = jnp.float32
    D = D_MODEL
    NC, NA = N_CONV_LAYERS, N_ATTN_LAYERS

    def nrm(shape, scale=1.0):
        return jax.random.normal(next(ks), shape, f32) * scale

    def gain(shape):
        return 1.0 + nrm(shape, 0.02)

    return {
        'x_prompt': nrm((BATCH, SEQ, D)),
        'x_sample': nrm((DEC_BATCH, DEC_SEQ, D)),
        'state_conv': nrm((NC, DEC_BATCH, CONV_STATE, D), 0.5),
        'cache_k': nrm((NA, DEC_BATCH, WINDOW, N_KV_HEADS, HEAD_DIM)),
        'cache_v': nrm((NA, DEC_BATCH, WINDOW, N_KV_HEADS, HEAD_DIM)),
        'p_prompt': nrm((DEPTH, BATCH, SEQ, PLE_DIM)),
        'p_sample': nrm((DEPTH, DEC_BATCH, DEC_SEQ, PLE_DIM)),
        'norm_mix': gain((DEPTH, D)),
        'norm_ffn': gain((DEPTH, D)),
        'norm_ple': gain((DEPTH, D)),
        'conv_w_in': nrm((NC, D, 2 * D), D ** -0.5),
        'conv_b_in': nrm((NC, 2 * D), 0.02),
        'conv_w_dw': nrm((NC, CONV_WIDTH, D), CONV_WIDTH ** -0.5),
        'conv_b_dw': nrm((NC, D), 0.02),
        'conv_ln_g': gain((NC, D)),
        'conv_ln_b': nrm((NC, D), 0.02),
        'conv_w_out': nrm((NC, D, D), D ** -0.5),
        'conv_b_out': nrm((NC, D), 0.02),
        'attn_w_qkv': nrm((NA, D, QKV_DIM), D ** -0.5),
        'attn_q_norm': gain((NA, HEAD_DIM)),
        'attn_k_norm': gain((NA, HEAD_DIM)),
        'attn_sinks': nrm((NA, N_HEADS), 0.5),
        'attn_w_o': nrm((NA, N_HEADS * HEAD_DIM, D), (N_HEADS * HEAD_DIM) ** -0.5),
        'moe_w_rg': nrm((DEPTH, D, N_EXPERT_GROUPS), D ** -0.5),
        'moe_b_rg': nrm((DEPTH, N_EXPERT_GROUPS), 0.01),
        'moe_w_re': nrm((DEPTH, D, N_EXPERTS), D ** -0.5),
        'moe_b_re': nrm((DEPTH, N_EXPERTS), 0.01),
        'moe_w_gate': nrm((DEPTH, N_EXPERTS, D, D_EXPERT), D ** -0.5),
        'moe_w_up': nrm((DEPTH, N_EXPERTS, D, D_EXPERT), D ** -0.5),
        'moe_w_down': nrm((DEPTH, N_EXPERTS, D_EXPERT, D), D_EXPERT ** -0.5),
        'ple_w_gate': nrm((DEPTH, D, D), D ** -0.5),
        'ple_w_proj': nrm((DEPTH, PLE_DIM, D), PLE_DIM ** -0.5),
    }


def reference(x_prompt, x_sample, state_conv, cache_k, cache_v, p_prompt, p_sample,
              norm_mix, norm_ffn, norm_ple,
              conv_w_in, conv_b_in, conv_w_dw, conv_b_dw, conv_ln_g, conv_ln_b, conv_w_out, conv_b_out,
              attn_w_qkv, attn_q_norm, attn_k_norm, attn_sinks, attn_w_o,
              moe_w_rg, moe_b_rg, moe_w_re, moe_b_re, moe_w_gate, moe_w_up, moe_w_down,
              ple_w_gate, ple_w_proj):
    y_p, y_s = x_prompt, x_sample
    conv_p, conv_s, k_p, v_p, k_s, v_s = [], [], [], [], [], []
    for i in range(DEPTH):
        j = i // 2
        hp = rms_norm(y_p, norm_mix[i])
        hs = rms_norm(y_s, norm_mix[i])
        if i % 2 == 0:
            cw = (conv_w_in[j], conv_b_in[j], conv_w_dw[j], conv_b_dw[j],
                  conv_ln_g[j], conv_ln_b[j], conv_w_out[j], conv_b_out[j])
            zeros = jnp.zeros((hp.shape[0], CONV_STATE, D_MODEL), hp.dtype)
            mp, sp = conv_mixer(hp, zeros, *cw)
            ms, ss = conv_mixer(hs, state_conv[j], *cw)
            conv_p.append(sp)
            conv_s.append(ss)
        else:
            aw = (attn_w_qkv[j], attn_q_norm[j], attn_k_norm[j], attn_sinks[j], attn_w_o[j])
            mp, kp_new, vp_new = attn_prompt(hp, *aw)
            ms, ks_new, vs_new = attn_sample(hs, cache_k[j], cache_v[j], *aw)
            k_p.append(kp_new)
            v_p.append(vp_new)
            k_s.append(ks_new)
            v_s.append(vs_new)
        y_p = y_p + mp
        y_s = y_s + ms
        mw = (moe_w_rg[i], moe_b_rg[i], moe_w_re[i], moe_b_re[i], moe_w_gate[i], moe_w_up[i], moe_w_down[i])
        y_p = y_p + hier_moe(rms_norm(y_p, norm_ffn[i]), *mw)
        y_s = y_s + hier_moe(rms_norm(y_s, norm_ffn[i]), *mw)
        y_p = y_p + per_layer_input(y_p, p_prompt[i], norm_ple[i], ple_w_gate[i], ple_w_proj[i])
        y_s = y_s + per_layer_input(y_s, p_sample[i], norm_ple[i], ple_w_gate[i], ple_w_proj[i])
    new_conv_p = jnp.stack(conv_p)
    new_conv_s = jnp.stack(conv_s)
    new_k_p = jnp.stack(k_p)
    new_v_p = jnp.stack(v_p)
    new_k_s = jnp.stack(k_s)
    new_v_s = jnp.stack(v_s)
    return (y_p, y_s, new_conv_p, new_conv_s, new_k_p, new_v_p, new_k_s, new_v_s)
```

```python
import functools

import jax
import jax.numpy as jnp
from jax import lax
from jax.experimental import pallas as pl
from jax.experimental.pallas import tpu as pltpu

F32 = jnp.float32
BF16 = jnp.bfloat16

D_MODEL = 1024
PLE_DIM = 256
CONV_WIDTH = 31
CONV_STATE = CONV_WIDTH - 1
N_HEADS = 16
N_KV_HEADS = 4
HEAD_DIM = 64
GROUP = N_HEADS // N_KV_HEADS
WINDOW = 128
KV_DIM = N_KV_HEADS * HEAD_DIM
N_EXPERT_GROUPS = 4
EXPERTS_PER_GROUP = 4
N_EXPERTS = 16
D_EXPERT = 256
EPS = 1e-6
NEG_INF = -1e30

LANES = 128
ROUTER_LANES = LANES
EXPERT_LANE0 = N_EXPERT_GROUPS
HALO = 32
CONV_ROWS = 32
CONV_LANES = 256
SAMPLE_SEQS = 8
VMEM_LIMIT = 48 * 1024 * 1024


def _row_tile(n):
    return 512 if n % 512 == 0 else n


def _params(*sem):
    return pltpu.CompilerParams(dimension_semantics=sem, vmem_limit_bytes=VMEM_LIMIT)


def _full(shape):
    nd = len(shape)
    return pl.BlockSpec(shape, lambda *_: (0,) * nd)


def _rms(x, g):
    ms = jnp.mean(x * x, axis=-1, keepdims=True)
    return x * lax.rsqrt(ms + EPS) * g


def _sigmoid(x):
    return 1.0 / (1.0 + jnp.exp(-x))


def _mm(a, b):
    return jnp.dot(a, b, preferred_element_type=F32)


def _conv_in_kernel(x_ref, g_ref, w_ref, b_ref, u_ref):
    h = _rms(x_ref[...], g_ref[...]).astype(BF16)
    z = _mm(h, w_ref[...]) + b_ref[...]
    u_ref[...] = z[:, :D_MODEL] * _sigmoid(z[:, D_MODEL:])


def _conv_in(x, g, w, b):
    n = x.shape[0]
    t = _row_tile(n)
    return pl.pallas_call(
        _conv_in_kernel,
        grid=(n // t,),
        in_specs=[pl.BlockSpec((t, D_MODEL), lambda i: (i, 0)), _full(g.shape), _full(w.shape), _full(b.shape)],
        out_specs=pl.BlockSpec((t, D_MODEL), lambda i: (i, 0)),
        out_shape=jax.ShapeDtypeStruct((n, D_MODEL), F32),
        compiler_params=_params("arbitrary"),
        name="conv_in",
    )(x, g, w, b)


def _ln_silu(c, g, b):
    mu = jnp.mean(c, axis=-1, keepdims=True)
    xc = c - mu
    var = jnp.mean(xc * xc, axis=-1, keepdims=True)
    cn = xc * lax.rsqrt(var + EPS) * g + b
    return cn * _sigmoid(cn)


def _conv_out_p_kernel(u_ref, halo_ref, x_ref, wdw_ref, bdw_ref, lng_ref, lnb_ref, wout_ref, bout_ref, y_ref,
                       ubuf, hbuf):
    t = u_ref.shape[0]
    i = pl.program_id(0)
    ubuf[0:HALO, :] = jnp.where(i > 0, halo_ref[...], 0.0)
    ubuf[HALO:, :] = u_ref[...]

    def chunk(r, carry):
        r0 = pl.multiple_of(r * CONV_ROWS, CONV_ROWS)
        pieces = []
        for lc in range(D_MODEL // CONV_LANES):
            lanes = slice(lc * CONV_LANES, (lc + 1) * CONV_LANES)
            blk = ubuf[pl.ds(r0, CONV_ROWS + HALO), lanes]
            acc = jnp.broadcast_to(bdw_ref[:, lanes], (CONV_ROWS, CONV_LANES))
            for k in range(CONV_WIDTH):
                off = HALO - CONV_STATE + k
                acc = acc + wdw_ref[k:k + 1, lanes] * blk[off:off + CONV_ROWS, :]
            pieces.append(acc)
        c = jnp.concatenate(pieces, axis=1)
        hbuf[pl.ds(r0, CONV_ROWS), :] = _ln_silu(c, lng_ref[...], lnb_ref[...]).astype(BF16)
        return carry

    lax.fori_loop(0, t // CONV_ROWS, chunk, 0)
    y_ref[...] = x_ref[...] + _mm(hbuf[...], wout_ref[...]) + bout_ref[...]


def _conv_out_p(u, x, wdw, bdw, lng, lnb, wout, bout):
    n = u.shape[0]
    t = _row_tile(n)
    hb = t // HALO
    row = lambda i: (i, 0)
    return pl.pallas_call(
        _conv_out_p_kernel,
        grid=(n // t,),
        in_specs=[pl.BlockSpec((t, D_MODEL), row),
                  pl.BlockSpec((HALO, D_MODEL), lambda i: (jnp.maximum(i * hb - 1, 0), 0)),
                  pl.BlockSpec((t, D_MODEL), row),
                  _full(wdw.shape), _full(bdw.shape), _full(lng.shape), _full(lnb.shape),
                  _full(wout.shape), _full(bout.shape)],
        out_specs=pl.BlockSpec((t, D_MODEL), row),
        out_shape=jax.ShapeDtypeStruct((n, D_MODEL), F32),
        scratch_shapes=[pltpu.VMEM((t + HALO, D_MODEL), F32), pltpu.VMEM((t, D_MODEL), BF16)],
        compiler_params=_params("arbitrary"),
        name="conv_out_prompt",
    )(u, u, x, wdw, bdw, lng, lnb, wout, bout)


def _conv_out_s_kernel(ue_ref, x_ref, wdw_ref, bdw_ref, lng_ref, lnb_ref, wout_ref, bout_ref, y_ref):
    nt, bb, _ = x_ref.shape
    for t in range(nt):
        acc = jnp.broadcast_to(bdw_ref[...], (bb, D_MODEL))
        for k in range(CONV_WIDTH):
            acc = acc + wdw_ref[k:k + 1, :] * ue_ref[t + k]
        h = _ln_silu(acc, lng_ref[...], lnb_ref[...]).astype(BF16)
        y_ref[t] = x_ref[t] + _mm(h, wout_ref[...]) + bout_ref[...]


def _conv_out_s(ue, x, wdw, bdw, lng, lnb, wout, bout):
    nt, b, _ = x.shape
    bb = 32 if b % 32 == 0 else b
    return pl.pallas_call(
        _conv_out_s_kernel,
        grid=(b // bb,),
        in_specs=[pl.BlockSpec((ue.shape[0], bb, D_MODEL), lambda i: (0, i, 0)),
                  pl.BlockSpec((nt, bb, D_MODEL), lambda i: (0, i, 0)),
                  _full(wdw.shape), _full(bdw.shape), _full(lng.shape), _full(lnb.shape),
                  _full(wout.shape), _full(bout.shape)],
        out_specs=pl.BlockSpec((nt, bb, D_MODEL), lambda i: (0, i, 0)),
        out_shape=jax.ShapeDtypeStruct((nt, b, D_MODEL), F32),
        compiler_params=_params("arbitrary"),
        name="conv_out_sample",
    )(ue, x, wdw, bdw, lng, lnb, wout, bout)


def _route(lg):
    big = 3.0e38
    lane = lax.broadcasted_iota(jnp.int32, lg.shape, 1)
    lanef = lane.astype(F32)
    is_g = lane < N_EXPERT_GROUPS
    gl = jnp.where(is_g, lg, -big)
    gmax = jnp.max(gl, axis=1, keepdims=True)
    gsum = jnp.sum(jnp.where(is_g, jnp.exp(gl - gmax), 0.0), axis=1, keepdims=True)
    g_w = 1.0 / gsum
    g_idx = jnp.min(jnp.where(gl == gmax, lanef, big), axis=1, keepdims=True)
    rel = lanef - float(EXPERT_LANE0) - g_idx * float(EXPERTS_PER_GROUP)
    in_grp = jnp.where(rel >= 0.0, jnp.where(rel < float(EXPERTS_PER_GROUP), 1.0, 0.0), 0.0) > 0.5
    el = jnp.where(in_grp, lg, -big)
    e1 = jnp.max(el, axis=1, keepdims=True)
    i1 = jnp.min(jnp.where(el == e1, lanef, big), axis=1, keepdims=True)
    el2 = jnp.where(lanef == i1, -big, el)
    e2 = jnp.max(el2, axis=1, keepdims=True)
    i2 = jnp.min(jnp.where(el2 == e2, lanef, big), axis=1, keepdims=True)
    tt = jnp.exp(e2 - e1)
    w1 = g_w / (1.0 + tt)
    w2 = g_w * tt / (1.0 + tt)
    return jnp.where(lanef == i1, w1, 0.0) + jnp.where(lanef == i2, w2, 0.0)


def _moe_ple_kernel(y_ref, gffn_ref, wr_ref, br_ref, wg_ref, wu_ref, wd_ref, p_ref, gple_ref, pwg_ref, pwp_ref,
                    o_ref, xn_s, gate_s, acc_s):
    e = pl.program_id(1)

    @pl.when(e == 0)
    def _():
        xf = _rms(y_ref[...], gffn_ref[...])
        xn_s[...] = xf.astype(BF16)
        logits = jnp.dot(xf, wr_ref[...], precision=lax.Precision.HIGHEST, preferred_element_type=F32) + br_ref[...]
        gate_s[...] = _route(logits)
        acc_s[...] = jnp.zeros_like(acc_s)

    xn = xn_s[...]
    hg = _mm(xn, wg_ref[0])
    hu = _mm(xn, wu_ref[0])
    lane = lax.broadcasted_iota(jnp.int32, gate_s.shape, 1)
    ge = jnp.sum(jnp.where(lane == e + EXPERT_LANE0, gate_s[...], 0.0), axis=1, keepdims=True)
    hid = (hg * _sigmoid(hg)) * hu * ge
    acc_s[...] += _mm(hid.astype(BF16), wd_ref[0])

    @pl.when(e == N_EXPERTS - 1)
    def _():
        y2 = y_ref[...] + acc_s[...]
        hn = _rms(y2, gple_ref[...]).astype(BF16)
        gt = _sigmoid(_mm(hn, pwg_ref[...]))
        pr = _mm(p_ref[...].astype(BF16), pwp_ref[...])
        o_ref[...] = y2 + gt * pr


def _moe_ple(y, gffn, wr, br, wg, wu, wd, p, gple, pwg, pwp):
    n = y.shape[0]
    t = 1024 if n % 1024 == 0 else _row_tile(n)
    row = lambda i, e: (i, 0)
    exp = lambda i, e: (e, 0, 0)
    c2 = lambda i, e: (0, 0)
    return pl.pallas_call(
        _moe_ple_kernel,
        grid=(n // t, N_EXPERTS),
        in_specs=[pl.BlockSpec((t, D_MODEL), row),
                  pl.BlockSpec(gffn.shape, c2), pl.BlockSpec(wr.shape, c2), pl.BlockSpec(br.shape, c2),
                  pl.BlockSpec((1, D_MODEL, D_EXPERT), exp), pl.BlockSpec((1, D_MODEL, D_EXPERT), exp),
                  pl.BlockSpec((1, D_EXPERT, D_MODEL), exp),
                  pl.BlockSpec((t, PLE_DIM), row),
                  pl.BlockSpec(gple.shape, c2), pl.BlockSpec(pwg.shape, c2), pl.BlockSpec(pwp.shape, c2)],
        out_specs=pl.BlockSpec((t, D_MODEL), row),
        out_shape=jax.ShapeDtypeStruct((n, D_MODEL), F32),
        scratch_shapes=[pltpu.VMEM((t, D_MODEL), BF16), pltpu.VMEM((t, ROUTER_LANES), F32),
                        pltpu.VMEM((t, D_MODEL), F32)],
        compiler_params=_params("arbitrary", "arbitrary"),
        name="moe_ple",
    )(y, gffn, wr, br, wg, wu, wd, p, gple, pwg, pwp)


def _split_bf16(a):
    hi = a.astype(BF16)
    lo = (a - hi.astype(F32)).astype(BF16)
    return hi, lo


def _head_norm(a, ind_ref, indt_ref, gain):
    hi, lo = _split_bf16(a * a)
    ss = _mm(hi, ind_ref[...]) + _mm(lo, ind_ref[...])
    inv = lax.rsqrt(ss * (1.0 / HEAD_DIM) + EPS)
    ihi, ilo = _split_bf16(inv)
    invb = _mm(ihi, indt_ref[...]) + _mm(ilo, indt_ref[...])
    return a * invb * gain


def _qkv_kernel(x_ref, g_ref, w_ref, iq_ref, iqt_ref, ik_ref, ikt_ref, qg_ref, kg_ref, q_ref, k_ref, v_ref):
    h = _rms(x_ref[...], g_ref[...]).astype(BF16)
    qkv = _mm(h, w_ref[...])
    nq = N_HEADS * HEAD_DIM
    q = _head_norm(qkv[:, :nq], iq_ref, iqt_ref, qg_ref[...])
    k = _head_norm(qkv[:, nq:nq + KV_DIM], ik_ref, ikt_ref, kg_ref[...])
    q_ref[...] = (q * (HEAD_DIM ** -0.5)).astype(BF16)
    k_ref[...] = k
    v_ref[...] = qkv[:, nq + KV_DIM:]


def _qkv(x, g, w, iq, iqt, ik, ikt, qg, kg):
    n = x.shape[0]
    t = _row_tile(n)
    row = lambda i: (i, 0)
    return pl.pallas_call(
        _qkv_kernel,
        grid=(n // t,),
        in_specs=[pl.BlockSpec((t, D_MODEL), row)] + [_full(a.shape) for a in (g, w, iq, iqt, ik, ikt, qg, kg)],
        out_specs=[pl.BlockSpec((t, N_HEADS * HEAD_DIM), row), pl.BlockSpec((t, KV_DIM), row),
                   pl.BlockSpec((t, KV_DIM), row)],
        out_shape=[jax.ShapeDtypeStruct((n, N_HEADS * HEAD_DIM), BF16), jax.ShapeDtypeStruct((n, KV_DIM), F32),
                   jax.ShapeDtypeStruct((n, KV_DIM), F32)],
        compiler_params=_params("arbitrary"),
        name="qkv",
    )(x, g, w, iq, iqt, ik, ikt, qg, kg)


def _dup_heads(a):
    out = []
    for s in range(KV_DIM // LANES):
        sl = a[:, s * LANES:(s + 1) * LANES]
        sw = pltpu.roll(sl, HEAD_DIM, axis=1)
        low = lax.broadcasted_iota(jnp.int32, sl.shape, 1) < HEAD_DIM
        out.append(jnp.where(low, sl, sw))
        out.append(jnp.where(low, sw, sl))
    return jnp.concatenate(out, axis=1).astype(BF16)


def _attend(q_rows, k2, v2, bias_of, sink_of, extra_mask):
    m_rows = q_rows.shape[0]
    low_q = lax.broadcasted_iota(jnp.int32, (m_rows, LANES), 1) < HEAD_DIM
    low_k = lax.broadcasted_iota(jnp.int32, (k2.shape[0], LANES), 1) < HEAD_DIM
    zero_q = jnp.zeros((m_rows, LANES), BF16)
    zero_k = jnp.zeros((k2.shape[0], LANES), BF16)
    slabs = []
    for g in range(N_KV_HEADS):
        kg = k2[:, g * LANES:(g + 1) * LANES]
        vg = v2[:, g * LANES:(g + 1) * LANES]
        v_lo = jnp.where(low_k, vg, zero_k)
        v_hi = jnp.where(low_k, zero_k, vg)
        lhs = []
        for a in range(GROUP):
            h = g * GROUP + a
            qs = q_rows[:, (h // 2) * LANES:(h // 2 + 1) * LANES]
            lhs.append(jnp.where(low_q, qs, zero_q) if h % 2 == 0 else jnp.where(low_q, zero_q, qs))
        s = lax.dot_general(jnp.concatenate(lhs, axis=0), kg, (((1,), (1,)), ((), ())), preferred_element_type=F32)
        probs, rinv = [], []
        for a in range(GROUP):
            h = g * GROUP + a
            sa = s[a * m_rows:(a + 1) * m_rows] + bias_of(h)
            if extra_mask is not None:
                sa = jnp.where(extra_mask, NEG_INF, sa)
            sink = sink_of(h)
            m = jnp.maximum(jnp.max(sa, axis=1, keepdims=True), sink)
            p = jnp.exp(sa - m)
            den = jnp.sum(p, axis=1, keepdims=True) + jnp.exp(sink - m)
            probs.append(p.astype(BF16))
            rinv.append(1.0 / den)
        for sp in range(GROUP // 2):
            o = _mm(probs[2 * sp], v_lo) + _mm(probs[2 * sp + 1], v_hi)
            slabs.append(o * jnp.where(low_q, rinv[2 * sp], rinv[2 * sp + 1]))
    return jnp.concatenate(slabs, axis=1)


def _attn_p_kernel(sink_ref, q_ref, k_ref, v_ref, x_ref, bias_ref, wo_ref, y_ref, kbuf, vbuf, obuf):
    t = q_ref.shape[0]
    i = pl.program_id(0)

    @pl.when(i == 0)
    def _():
        kbuf[0:WINDOW, :] = jnp.zeros((WINDOW, 2 * KV_DIM), BF16)
        vbuf[0:WINDOW, :] = jnp.zeros((WINDOW, 2 * KV_DIM), BF16)

    @pl.when(i > 0)
    def _():
        kbuf[0:WINDOW, :] = kbuf[t:t + WINDOW, :]
        vbuf[0:WINDOW, :] = vbuf[t:t + WINDOW, :]

    kbuf[WINDOW:, :] = _dup_heads(k_ref[...])
    vbuf[WINDOW:, :] = _dup_heads(v_ref[...])
    col = lax.broadcasted_iota(jnp.int32, (WINDOW, 2 * WINDOW), 1)
    for j in range(t // WINDOW):
        rows = slice(j * WINDOW, (j + 1) * WINDOW)
        keys = slice(j * WINDOW, (j + 2) * WINDOW)
        extra = jnp.logical_and(i == 0, col < WINDOW) if j == 0 else None
        o = _attend(q_ref[rows, :], kbuf[keys, :], vbuf[keys, :], lambda h: bias_ref[h], lambda h: sink_ref[h], extra)
        obuf[rows, :] = o.astype(BF16)
    y_ref[...] = x_ref[...] + _mm(obuf[...], wo_ref[...])


def _attn_p(sinks, q, k, v, x, bias, wo):
    n = q.shape[0]
    t = _row_tile(n)
    row = lambda i: (i, 0)
    return pl.pallas_call(
        _attn_p_kernel,
        grid=(n // t,),
        in_specs=[pl.BlockSpec(memory_space=pltpu.SMEM),
                  pl.BlockSpec((t, N_HEADS * HEAD_DIM), row), pl.BlockSpec((t, KV_DIM), row),
                  pl.BlockSpec((t, KV_DIM), row), pl.BlockSpec((t, D_MODEL), row),
                  _full(bias.shape), _full(wo.shape)],
        out_specs=pl.BlockSpec((t, D_MODEL), row),
        out_shape=jax.ShapeDtypeStruct((n, D_MODEL), F32),
        scratch_shapes=[pltpu.VMEM((t + WINDOW, 2 * KV_DIM), BF16), pltpu.VMEM((t + WINDOW, 2 * KV_DIM), BF16),
                        pltpu.VMEM((t, N_HEADS * HEAD_DIM), BF16)],
        compiler_params=_params("arbitrary"),
        name="attn_prompt",
    )(sinks, q, k, v, x, bias, wo)


def _attn_s_kernel(sink_ref, q_ref, k_ref, v_ref, ck_ref, cv_ref, bias_ref, o_ref):
    pad = bias_ref.shape[2] - ck_ref.shape[0] - k_ref.shape[0]
    zpad = jnp.zeros((pad, KV_DIM), F32)
    k2 = _dup_heads(jnp.concatenate([ck_ref[...], k_ref[...], zpad], axis=0))
    v2 = _dup_heads(jnp.concatenate([cv_ref[...], v_ref[...], zpad], axis=0))
    o = _attend(q_ref[...], k2, v2, lambda h: bias_ref[h], lambda h: sink_ref[h], None)
    o_ref[...] = o.astype(BF16)


def _attn_s(sinks, q, k, v, ck, cv, bias, t_new):
    n = q.shape[0]
    rows = SAMPLE_SEQS * t_new
    crow = SAMPLE_SEQS * WINDOW
    row = lambda i: (i, 0)
    return pl.pallas_call(
        _attn_s_kernel,
        grid=(n // rows,),
        in_specs=[pl.BlockSpec(memory_space=pltpu.SMEM),
                  pl.BlockSpec((rows, N_HEADS * HEAD_DIM), row), pl.BlockSpec((rows, KV_DIM), row),
                  pl.BlockSpec((rows, KV_DIM), row), pl.BlockSpec((crow, KV_DIM), row),
                  pl.BlockSpec((crow, KV_DIM), row), _full(bias.shape)],
        out_specs=pl.BlockSpec((rows, N_HEADS * HEAD_DIM), row),
        out_shape=jax.ShapeDtypeStruct((n, N_HEADS * HEAD_DIM), BF16),
        compiler_params=_params("arbitrary"),
        name="attn_sample",
    )(sinks, q, k, v, ck, cv, bias)


def _proj_res_kernel(o_ref, x_ref, w_ref, y_ref):
    y_ref[...] = x_ref[...] + _mm(o_ref[...], w_ref[...])


def _proj_res(o, x, w):
    n = o.shape[0]
    t = _row_tile(n)
    row = lambda i: (i, 0)
    return pl.pallas_call(
        _proj_res_kernel,
        grid=(n // t,),
        in_specs=[pl.BlockSpec((t, o.shape[1]), row), pl.BlockSpec((t, D_MODEL), row), _full(w.shape)],
        out_specs=pl.BlockSpec((t, D_MODEL), row),
        out_shape=jax.ShapeDtypeStruct((n, D_MODEL), F32),
        compiler_params=_params("arbitrary"),
        name="proj_res",
    )(o, x, w)


def _alibi_slopes():
    return jnp.exp2(-8.0 * jnp.arange(1, N_HEADS + 1, dtype=F32) / N_HEADS)


def _prompt_bias():
    dist = WINDOW + jnp.arange(WINDOW)[:, None] - jnp.arange(2 * WINDOW)[None, :]
    allowed = (dist >= 0) & (dist <= WINDOW)
    b = -(_alibi_slopes()[:, None, None] * dist.astype(F32)[None])
    return jnp.where(allowed[None], b, NEG_INF)


def _sample_bias(t_new, n_cols):
    c = jnp.arange(n_cols)
    n_cache = SAMPLE_SEQS * WINDOW
    n_new = SAMPLE_SEQS * t_new
    is_cache = c < n_cache
    is_new = (c >= n_cache) & (c < n_cache + n_new)
    seq_c = jnp.where(is_cache, c // WINDOW, (c - n_cache) // t_new)
    pos_c = jnp.where(is_cache, c % WINDOW, WINDOW + (c - n_cache) % t_new)
    r = jnp.arange(n_new)
    seq_r, tok_r = r // t_new, r % t_new
    dist = WINDOW + tok_r[:, None] - pos_c[None, :]
    allowed = (seq_r[:, None] == seq_c[None, :]) & (is_cache | is_new)[None, :] & (dist >= 0) & (dist <= WINDOW)
    b = -(_alibi_slopes()[:, None, None] * dist.astype(F32)[None])
    return jnp.where(allowed[None], b, NEG_INF)


def _head_indicator(n_heads):
    ch = jnp.arange(n_heads * HEAD_DIM) // HEAD_DIM
    ind = (ch[:, None] == jnp.arange(LANES)[None, :]).astype(BF16)
    return ind, ind.T


def kernel(x_prompt, x_sample, state_conv, cache_k, cache_v, p_prompt, p_sample, norm_mix, norm_ffn, norm_ple,
           conv_w_in, conv_b_in, conv_w_dw, conv_b_dw, conv_ln_g, conv_ln_b, conv_w_out, conv_b_out, attn_w_qkv,
           attn_q_norm, attn_k_norm, attn_sinks, attn_w_o, moe_w_rg, moe_b_rg, moe_w_re, moe_b_re, moe_w_gate,
           moe_w_up, moe_w_down, ple_w_gate, ple_w_proj):
    bp, seq, d = x_prompt.shape
    bs, t_new, _ = x_sample.shape
    assert bp == 1 and d == D_MODEL and seq % WINDOW == 0 and bs % SAMPLE_SEQS == 0
    depth = norm_mix.shape[0]
    row2 = lambda a: a.reshape(1, -1)

    y_p = x_prompt.reshape(seq, d)
    y_s = x_sample.reshape(bs * t_new, d)
    conv_p, conv_s, k_p, v_p, k_s, v_s = [], [], [], [], [], []

    for i in range(depth):
        j = i // 2
        g_mix = row2(norm_mix[i])
        if i % 2 == 0:
            w_in = conv_w_in[j].astype(BF16)
            b_in = row2(conv_b_in[j])
            tail = (conv_w_dw[j], row2(conv_b_dw[j]), row2(conv_ln_g[j]), row2(conv_ln_b[j]),
                    conv_w_out[j].astype(BF16), row2(conv_b_out[j]))
            u_p = _conv_in(y_p, g_mix, w_in, b_in)
            u_s = _conv_in(y_s, g_mix, w_in, b_in)
            conv_p.append(u_p[seq - CONV_STATE:].reshape(1, CONV_STATE, d))
            ue = jnp.concatenate([state_conv[j], u_s.reshape(bs, t_new, d)], axis=1)
            conv_s.append(ue[:, t_new:])
            y_p = _conv_out_p(u_p, y_p, *tail)
            ys_t = _conv_out_s(ue.transpose(1, 0, 2), y_s.reshape(bs, t_new, d).transpose(1, 0, 2), *tail)
            y_s = ys_t.transpose(1, 0, 2).reshape(bs * t_new, d)
        else:
            w_qkv = attn_w_qkv[j].astype(BF16)
            w_o = attn_w_o[j].astype(BF16)
            iq, iqt = _head_indicator(N_HEADS)
            ik, ikt = _head_indicator(N_KV_HEADS)
            qg = row2(jnp.tile(attn_q_norm[j], N_HEADS))
            kg = row2(jnp.tile(attn_k_norm[j], N_KV_HEADS))
            sinks = attn_sinks[j]
            q1, k1, v1 = _qkv(y_p, g_mix, w_qkv, iq, iqt, ik, ikt, qg, kg)
            q2, k2, v2 = _qkv(y_s, g_mix, w_qkv, iq, iqt, ik, ikt, qg, kg)
            k_p.append(k1[seq - WINDOW:].reshape(1, WINDOW, N_KV_HEADS, HEAD_DIM))
            v_p.append(v1[seq - WINDOW:].reshape(1, WINDOW, N_KV_HEADS, HEAD_DIM))
            shp = (bs, t_new, N_KV_HEADS, HEAD_DIM)
            k_s.append(jnp.concatenate([cache_k[j][:, t_new:], k2.reshape(shp)], axis=1))
            v_s.append(jnp.concatenate([cache_v[j][:, t_new:], v2.reshape(shp)], axis=1))
            y_p = _attn_p(sinks, q1, k1, v1, y_p, _prompt_bias(), w_o)
            n_cols = -(-(SAMPLE_SEQS * (WINDOW + t_new)) // LANES) * LANES
            o_s = _attn_s(sinks, q2, k2, v2, cache_k[j].reshape(bs * WINDOW, KV_DIM),
                          cache_v[j].reshape(bs * WINDOW, KV_DIM), _sample_bias(t_new, n_cols), t_new)
            y_s = _proj_res(o_s, y_s, w_o)

        w_r = jnp.zeros((d, ROUTER_LANES), F32)
        w_r = w_r.at[:, :N_EXPERT_GROUPS].set(moe_w_rg[i]).at[:, EXPERT_LANE0:EXPERT_LANE0 + N_EXPERTS].set(moe_w_re[i])
        b_r = jnp.zeros((1, ROUTER_LANES), F32)
        b_r = b_r.at[0, :N_EXPERT_GROUPS].set(moe_b_rg[i]).at[0, EXPERT_LANE0:EXPERT_LANE0 + N_EXPERTS].set(moe_b_re[i])
        moe = (row2(norm_ffn[i]), w_r, b_r, moe_w_gate[i].astype(BF16), moe_w_up[i].astype(BF16),
               moe_w_down[i].astype(BF16))
        ple = (row2(norm_ple[i]), ple_w_gate[i].astype(BF16), ple_w_proj[i].astype(BF16))
        y_p = _moe_ple(y_p, *moe, p_prompt[i].reshape(seq, PLE_DIM), *ple)
        y_s = _moe_ple(y_s, *moe, p_sample[i].reshape(bs * t_new, PLE_DIM), *ple)

    return (y_p.reshape(1, seq, d), y_s.reshape(bs, t_new, d), jnp.stack(conv_p), jnp.stack(conv_s),
            jnp.stack(k_p), jnp.stack(v_p), jnp.stack(k_s), jnp.stack(v_s))
```

```python
import functools

import jax
import jax.numpy as jnp
from jax import lax
from jax.experimental import pallas as pl
from jax.experimental.pallas import tpu as pltpu

F32 = jnp.float32
BF16 = jnp.bfloat16

D_MODEL = 1024
PLE_DIM = 256
CONV_WIDTH = 31
CONV_STATE = CONV_WIDTH - 1
N_HEADS = 16
N_KV_HEADS = 4
HEAD_DIM = 64
GROUP = N_HEADS // N_KV_HEADS
WINDOW = 128
KV_DIM = N_KV_HEADS * HEAD_DIM
N_EXPERT_GROUPS = 4
EXPERTS_PER_GROUP = 4
N_EXPERTS = 16
D_EXPERT = 256
EPS = 1e-6
NEG_INF = -1e30

LANES = 128
ROUTER_LANES = LANES
EXPERT_LANE0 = N_EXPERT_GROUPS
HALO = 32
CONV_ROWS = 32
NORM_ROWS = 128
CONV_PITCH = D_MODEL // LANES + 1
SAMPLE_SEQS = 8
VMEM_LIMIT = 48 * 1024 * 1024


def _row_tile(n):
    return 512 if n % 512 == 0 else n


def _params(*sem):
    return pltpu.CompilerParams(dimension_semantics=sem, vmem_limit_bytes=VMEM_LIMIT)


def _full(shape):
    nd = len(shape)
    return pl.BlockSpec(shape, lambda *_: (0,) * nd)


def _rms(x, g):
    ms = jnp.mean(x * x, axis=-1, keepdims=True)
    return x * lax.rsqrt(ms + EPS) * g


def _sigmoid(x):
    return 1.0 / (1.0 + jnp.exp(-x))


def _mm(a, b):
    return jnp.dot(a, b, preferred_element_type=F32)


def _conv_in_kernel(x_ref, g_ref, w_ref, b_ref, u_ref):
    h = _rms(x_ref[...], g_ref[...]).astype(BF16)
    z = _mm(h, w_ref[...]) + b_ref[...]
    u_ref[...] = z[:, :D_MODEL] * _sigmoid(z[:, D_MODEL:])


def _conv_in(x, g, w, b):
    n = x.shape[0]
    t = _row_tile(n)
    return pl.pallas_call(
        _conv_in_kernel,
        grid=(n // t,),
        in_specs=[pl.BlockSpec((t, D_MODEL), lambda i: (i, 0)), _full(g.shape), _full(w.shape), _full(b.shape)],
        out_specs=pl.BlockSpec((t, D_MODEL), lambda i: (i, 0)),
        out_shape=jax.ShapeDtypeStruct((n, D_MODEL), F32),
        compiler_params=_params("arbitrary"),
        name="conv_in",
    )(x, g, w, b)


def _ln_silu(c, g, b):
    mu = jnp.mean(c, axis=-1, keepdims=True)
    xc = c - mu
    var = jnp.mean(xc * xc, axis=-1, keepdims=True)
    cn = xc * lax.rsqrt(var + EPS) * g + b
    return cn * _sigmoid(cn)


def _conv_out_p_kernel(u_ref, halo_ref, x_ref, wdw_ref, bdw_ref, lng_ref, lnb_ref, wout_ref, bout_ref, y_ref,
                       ubuf, cbuf, hbuf):
    t = u_ref.shape[0]
    i = pl.program_id(0)
    nj = D_MODEL // LANES

    def put(r, j, val):
        ubuf[pl.ds(r * CONV_PITCH + j, 8, stride=CONV_PITCH), :] = val

    for r in range(0, HALO, 8):
        for j in range(nj):
            put(r, j, jnp.where(i > 0, halo_ref[r:r + 8, j * LANES:(j + 1) * LANES], 0.0))

    def fill(rr, carry):
        r = pl.multiple_of(rr * 8, 8)
        for j in range(nj):
            put(r + HALO, j, u_ref[pl.ds(r, 8), j * LANES:(j + 1) * LANES])
        return carry

    lax.fori_loop(0, t // 8, fill, 0)

    def conv_chunk(rr, carry):
        r0 = pl.multiple_of(rr * CONV_ROWS, CONV_ROWS)
        for j in range(nj):
            lanes = slice(j * LANES, (j + 1) * LANES)
            accs = [None] * (CONV_ROWS // 8)
            for k in range(CONV_WIDTH):
                wk = wdw_ref[k:k + 1, lanes]
                for q in range(CONV_ROWS // 8):
                    r = r0 + (HALO - CONV_STATE + k + 8 * q)
                    term = wk * ubuf[pl.ds(r * CONV_PITCH + j, 8, stride=CONV_PITCH), :]
                    accs[q] = term if k == 0 else accs[q] + term
            for q in range(CONV_ROWS // 8):
                cbuf[pl.ds(r0 + 8 * q, 8), lanes] = accs[q]
        return carry

    lax.fori_loop(0, t // CONV_ROWS, conv_chunk, 0)

    def norm_chunk(rr, carry):
        r0 = pl.multiple_of(rr * NORM_ROWS, NORM_ROWS)
        c = cbuf[pl.ds(r0, NORM_ROWS), :] + bdw_ref[...]
        hbuf[pl.ds(r0, NORM_ROWS), :] = _ln_silu(c, lng_ref[...], lnb_ref[...]).astype(BF16)
        return carry

    lax.fori_loop(0, t // NORM_ROWS, norm_chunk, 0)
    y_ref[...] = x_ref[...] + _mm(hbuf[...], wout_ref[...]) + bout_ref[...]


def _conv_out_p(u, x, wdw, bdw, lng, lnb, wout, bout):
    n = u.shape[0]
    t = _row_tile(n)
    hb = t // HALO
    row = lambda i: (i, 0)
    return pl.pallas_call(
        _conv_out_p_kernel,
        grid=(n // t,),
        in_specs=[pl.BlockSpec((t, D_MODEL), row),
                  pl.BlockSpec((HALO, D_MODEL), lambda i: (jnp.maximum(i * hb - 1, 0), 0)),
                  pl.BlockSpec((t, D_MODEL), row),
                  _full(wdw.shape), _full(bdw.shape), _full(lng.shape), _full(lnb.shape),
                  _full(wout.shape), _full(bout.shape)],
        out_specs=pl.BlockSpec((t, D_MODEL), row),
        out_shape=jax.ShapeDtypeStruct((n, D_MODEL), F32),
        scratch_shapes=[pltpu.VMEM(((t + HALO) * CONV_PITCH, LANES), F32), pltpu.VMEM((t, D_MODEL), F32),
                        pltpu.VMEM((t, D_MODEL), BF16)],
        compiler_params=_params("arbitrary"),
        name="conv_out_prompt",
    )(u, u, x, wdw, bdw, lng, lnb, wout, bout)


def _conv_out_s_kernel(ue_ref, x_ref, wdw_ref, bdw_ref, lng_ref, lnb_ref, wout_ref, bout_ref, y_ref):
    nt, bb, _ = x_ref.shape
    for t in range(nt):
        acc = jnp.broadcast_to(bdw_ref[...], (bb, D_MODEL))
        for k in range(CONV_WIDTH):
            acc = acc + wdw_ref[k:k + 1, :] * ue_ref[t + k]
        h = _ln_silu(acc, lng_ref[...], lnb_ref[...]).astype(BF16)
        y_ref[t] = x_ref[t] + _mm(h, wout_ref[...]) + bout_ref[...]


def _conv_out_s(ue, x, wdw, bdw, lng, lnb, wout, bout):
    nt, b, _ = x.shape
    bb = 32 if b % 32 == 0 else b
    return pl.pallas_call(
        _conv_out_s_kernel,
        grid=(b // bb,),
        in_specs=[pl.BlockSpec((ue.shape[0], bb, D_MODEL), lambda i: (0, i, 0)),
                  pl.BlockSpec((nt, bb, D_MODEL), lambda i: (0, i, 0)),
                  _full(wdw.shape), _full(bdw.shape), _full(lng.shape), _full(lnb.shape),
                  _full(wout.shape), _full(bout.shape)],
        out_specs=pl.BlockSpec((nt, bb, D_MODEL), lambda i: (0, i, 0)),
        out_shape=jax.ShapeDtypeStruct((nt, b, D_MODEL), F32),
        compiler_params=_params("arbitrary"),
        name="conv_out_sample",
    )(ue, x, wdw, bdw, lng, lnb, wout, bout)


def _route(lg):
    big = 3.0e38
    lane = lax.broadcasted_iota(jnp.int32, lg.shape, 1)
    lanef = lane.astype(F32)
    is_g = lane < N_EXPERT_GROUPS
    gl = jnp.where(is_g, lg, -big)
    gmax = jnp.max(gl, axis=1, keepdims=True)
    gsum = jnp.sum(jnp.where(is_g, jnp.exp(gl - gmax), 0.0), axis=1, keepdims=True)
    g_w = 1.0 / gsum
    g_idx = jnp.min(jnp.where(gl == gmax, lanef, big), axis=1, keepdims=True)
    rel = lanef - float(EXPERT_LANE0) - g_idx * float(EXPERTS_PER_GROUP)
    in_grp = jnp.where(rel >= 0.0, jnp.where(rel < float(EXPERTS_PER_GROUP), 1.0, 0.0), 0.0) > 0.5
    el = jnp.where(in_grp, lg, -big)
    e1 = jnp.max(el, axis=1, keepdims=True)
    i1 = jnp.min(jnp.where(el == e1, lanef, big), axis=1, keepdims=True)
    el2 = jnp.where(lanef == i1, -big, el)
    e2 = jnp.max(el2, axis=1, keepdims=True)
    i2 = jnp.min(jnp.where(el2 == e2, lanef, big), axis=1, keepdims=True)
    tt = jnp.exp(e2 - e1)
    w1 = g_w / (1.0 + tt)
    w2 = g_w * tt / (1.0 + tt)
    return jnp.where(lanef == i1, w1, 0.0) + jnp.where(lanef == i2, w2, 0.0)


def _moe_ple_kernel(y_ref, gffn_ref, wr_ref, br_ref, wg_ref, wu_ref, wd_ref, p_ref, gple_ref, pwg_ref, pwp_ref,
                    o_ref, xn_s, gate_s, acc_s):
    e = pl.program_id(1)

    @pl.when(e == 0)
    def _():
        xf = _rms(y_ref[...], gffn_ref[...])
        xn_s[...] = xf.astype(BF16)
        logits = jnp.dot(xf, wr_ref[...], precision=lax.Precision.HIGHEST, preferred_element_type=F32) + br_ref[...]
        gate_s[...] = _route(logits)
        acc_s[...] = jnp.zeros_like(acc_s)

    xn = xn_s[...]
    hg = _mm(xn, wg_ref[0])
    hu = _mm(xn, wu_ref[0])
    lane = lax.broadcasted_iota(jnp.int32, gate_s.shape, 1)
    ge = jnp.sum(jnp.where(lane == e + EXPERT_LANE0, gate_s[...], 0.0), axis=1, keepdims=True)
    hid = (hg * _sigmoid(hg)) * hu * ge
    acc_s[...] += _mm(hid.astype(BF16), wd_ref[0])

    @pl.when(e == N_EXPERTS - 1)
    def _():
        y2 = y_ref[...] + acc_s[...]
        hn = _rms(y2, gple_ref[...]).astype(BF16)
        gt = _sigmoid(_mm(hn, pwg_ref[...]))
        pr = _mm(p_ref[...].astype(BF16), pwp_ref[...])
        o_ref[...] = y2 + gt * pr


def _moe_ple(y, gffn, wr, br, wg, wu, wd, p, gple, pwg, pwp):
    n = y.shape[0]
    t = 1024 if n % 1024 == 0 else _row_tile(n)
    row = lambda i, e: (i, 0)
    exp = lambda i, e: (e, 0, 0)
    c2 = lambda i, e: (0, 0)
    return pl.pallas_call(
        _moe_ple_kernel,
        grid=(n // t, N_EXPERTS),
        in_specs=[pl.BlockSpec((t, D_MODEL), row),
                  pl.BlockSpec(gffn.shape, c2), pl.BlockSpec(wr.shape, c2), pl.BlockSpec(br.shape, c2),
                  pl.BlockSpec((1, D_MODEL, D_EXPERT), exp), pl.BlockSpec((1, D_MODEL, D_EXPERT), exp),
                  pl.BlockSpec((1, D_EXPERT, D_MODEL), exp),
                  pl.BlockSpec((t, PLE_DIM), row),
                  pl.BlockSpec(gple.shape, c2), pl.BlockSpec(pwg.shape, c2), pl.BlockSpec(pwp.shape, c2)],
        out_specs=pl.BlockSpec((t, D_MODEL), row),
        out_shape=jax.ShapeDtypeStruct((n, D_MODEL), F32),
        scratch_shapes=[pltpu.VMEM((t, D_MODEL), BF16), pltpu.VMEM((t, ROUTER_LANES), F32),
                        pltpu.VMEM((t, D_MODEL), F32)],
        compiler_params=_params("arbitrary", "arbitrary"),
        name="moe_ple",
    )(y, gffn, wr, br, wg, wu, wd, p, gple, pwg, pwp)


def _split_bf16(a):
    hi = a.astype(BF16)
    lo = (a - hi.astype(F32)).astype(BF16)
    return hi, lo


def _head_norm(a, ind_ref, indt_ref, gain):
    hi, lo = _split_bf16(a * a)
    ss = _mm(hi, ind_ref[...]) + _mm(lo, ind_ref[...])
    inv = lax.rsqrt(ss * (1.0 / HEAD_DIM) + EPS)
    ihi, ilo = _split_bf16(inv)
    invb = _mm(ihi, indt_ref[...]) + _mm(ilo, indt_ref[...])
    return a * invb * gain


def _qkv_kernel(x_ref, g_ref, w_ref, iq_ref, iqt_ref, ik_ref, ikt_ref, qg_ref, kg_ref, q_ref, k_ref, v_ref):
    h = _rms(x_ref[...], g_ref[...]).astype(BF16)
    qkv = _mm(h, w_ref[...])
    nq = N_HEADS * HEAD_DIM
    q = _head_norm(qkv[:, :nq], iq_ref, iqt_ref, qg_ref[...])
    k = _head_norm(qkv[:, nq:nq + KV_DIM], ik_ref, ikt_ref, kg_ref[...])
    q_ref[...] = (q * (HEAD_DIM ** -0.5)).astype(BF16)
    k_ref[...] = k
    v_ref[...] = qkv[:, nq + KV_DIM:]


def _qkv(x, g, w, iq, iqt, ik, ikt, qg, kg):
    n = x.shape[0]
    t = _row_tile(n)
    row = lambda i: (i, 0)
    return pl.pallas_call(
        _qkv_kernel,
        grid=(n // t,),
        in_specs=[pl.BlockSpec((t, D_MODEL), row)] + [_full(a.shape) for a in (g, w, iq, iqt, ik, ikt, qg, kg)],
        out_specs=[pl.BlockSpec((t, N_HEADS * HEAD_DIM), row), pl.BlockSpec((t, KV_DIM), row),
                   pl.BlockSpec((t, KV_DIM), row)],
        out_shape=[jax.ShapeDtypeStruct((n, N_HEADS * HEAD_DIM), BF16), jax.ShapeDtypeStruct((n, KV_DIM), F32),
                   jax.ShapeDtypeStruct((n, KV_DIM), F32)],
        compiler_params=_params("arbitrary"),
        name="qkv",
    )(x, g, w, iq, iqt, ik, ikt, qg, kg)


def _dup_heads(a):
    out = []
    for s in range(KV_DIM // LANES):
        sl = a[:, s * LANES:(s + 1) * LANES]
        sw = pltpu.roll(sl, HEAD_DIM, axis=1)
        low = lax.broadcasted_iota(jnp.int32, sl.shape, 1) < HEAD_DIM
        out.append(jnp.where(low, sl, sw))
        out.append(jnp.where(low, sw, sl))
    return jnp.concatenate(out, axis=1).astype(BF16)


def _attend(q_rows, k2, v2, bias_of, sink_of, extra_mask):
    m_rows = q_rows.shape[0]
    low_q = lax.broadcasted_iota(jnp.int32, (m_rows, LANES), 1) < HEAD_DIM
    low_k = lax.broadcasted_iota(jnp.int32, (k2.shape[0], LANES), 1) < HEAD_DIM
    zero_q = jnp.zeros((m_rows, LANES), BF16)
    zero_k = jnp.zeros((k2.shape[0], LANES), BF16)
    slabs = []
    for g in range(N_KV_HEADS):
        kg = k2[:, g * LANES:(g + 1) * LANES]
        vg = v2[:, g * LANES:(g + 1) * LANES]
        v_lo = jnp.where(low_k, vg, zero_k)
        v_hi = jnp.where(low_k, zero_k, vg)
        lhs = []
        for a in range(GROUP):
            h = g * GROUP + a
            qs = q_rows[:, (h // 2) * LANES:(h // 2 + 1) * LANES]
            lhs.append(jnp.where(low_q, qs, zero_q) if h % 2 == 0 else jnp.where(low_q, zero_q, qs))
        s = lax.dot_general(jnp.concatenate(lhs, axis=0), kg, (((1,), (1,)), ((), ())), preferred_element_type=F32)
        probs, rinv = [], []
        for a in range(GROUP):
            h = g * GROUP + a
            sa = s[a * m_rows:(a + 1) * m_rows] + bias_of(h)
            if extra_mask is not None:
                sa = jnp.where(extra_mask, NEG_INF, sa)
            sink = sink_of(h)
            m = jnp.maximum(jnp.max(sa, axis=1, keepdims=True), sink)
            p = jnp.exp(sa - m)
            den = jnp.sum(p, axis=1, keepdims=True) + jnp.exp(sink - m)
            probs.append(p.astype(BF16))
            rinv.append(1.0 / den)
        for sp in range(GROUP // 2):
            o = _mm(probs[2 * sp], v_lo) + _mm(probs[2 * sp + 1], v_hi)
            slabs.append(o * jnp.where(low_q, rinv[2 * sp], rinv[2 * sp + 1]))
    return jnp.concatenate(slabs, axis=1)


def _attn_p_kernel(sink_ref, q_ref, k_ref, v_ref, x_ref, bias_ref, wo_ref, y_ref, kbuf, vbuf, obuf):
    t = q_ref.shape[0]
    i = pl.program_id(0)

    @pl.when(i == 0)
    def _():
        kbuf[0:WINDOW, :] = jnp.zeros((WINDOW, 2 * KV_DIM), BF16)
        vbuf[0:WINDOW, :] = jnp.zeros((WINDOW, 2 * KV_DIM), BF16)

    @pl.when(i > 0)
    def _():
        kbuf[0:WINDOW, :] = kbuf[t:t + WINDOW, :]
        vbuf[0:WINDOW, :] = vbuf[t:t + WINDOW, :]

    kbuf[WINDOW:, :] = _dup_heads(k_ref[...])
    vbuf[WINDOW:, :] = _dup_heads(v_ref[...])
    col = lax.broadcasted_iota(jnp.int32, (WINDOW, 2 * WINDOW), 1)
    for j in range(t // WINDOW):
        rows = slice(j * WINDOW, (j + 1) * WINDOW)
        keys = slice(j * WINDOW, (j + 2) * WINDOW)
        extra = jnp.logical_and(i == 0, col < WINDOW) if j == 0 else None
        o = _attend(q_ref[rows, :], kbuf[keys, :], vbuf[keys, :], lambda h: bias_ref[h], lambda h: sink_ref[h], extra)
        obuf[rows, :] = o.astype(BF16)
    y_ref[...] = x_ref[...] + _mm(obuf[...], wo_ref[...])


def _attn_p(sinks, q, k, v, x, bias, wo):
    n = q.shape[0]
    t = _row_tile(n)
    row = lambda i: (i, 0)
    return pl.pallas_call(
        _attn_p_kernel,
        grid=(n // t,),
        in_specs=[pl.BlockSpec(memory_space=pltpu.SMEM),
                  pl.BlockSpec((t, N_HEADS * HEAD_DIM), row), pl.BlockSpec((t, KV_DIM), row),
                  pl.BlockSpec((t, KV_DIM), row), pl.BlockSpec((t, D_MODEL), row),
                  _full(bias.shape), _full(wo.shape)],
        out_specs=pl.BlockSpec((t, D_MODEL), row),
        out_shape=jax.ShapeDtypeStruct((n, D_MODEL), F32),
        scratch_shapes=[pltpu.VMEM((t + WINDOW, 2 * KV_DIM), BF16), pltpu.VMEM((t + WINDOW, 2 * KV_DIM), BF16),
                        pltpu.VMEM((t, N_HEADS * HEAD_DIM), BF16)],
        compiler_params=_params("arbitrary"),
        name="attn_prompt",
    )(sinks, q, k, v, x, bias, wo)


def _attn_s_kernel(sink_ref, q_ref, k_ref, v_ref, ck_ref, cv_ref, bias_ref, o_ref):
    pad = bias_ref.shape[2] - ck_ref.shape[0] - k_ref.shape[0]
    zpad = jnp.zeros((pad, KV_DIM), F32)
    k2 = _dup_heads(jnp.concatenate([ck_ref[...], k_ref[...], zpad], axis=0))
    v2 = _dup_heads(jnp.concatenate([cv_ref[...], v_ref[...], zpad], axis=0))
    o = _attend(q_ref[...], k2, v2, lambda h: bias_ref[h], lambda h: sink_ref[h], None)
    o_ref[...] = o.astype(BF16)


def _attn_s(sinks, q, k, v, ck, cv, bias, t_new):
    n = q.shape[0]
    rows = SAMPLE_SEQS * t_new
    crow = SAMPLE_SEQS * WINDOW
    row = lambda i: (i, 0)
    return pl.pallas_call(
        _attn_s_kernel,
        grid=(n // rows,),
        in_specs=[pl.BlockSpec(memory_space=pltpu.SMEM),
                  pl.BlockSpec((rows, N_HEADS * HEAD_DIM), row), pl.BlockSpec((rows, KV_DIM), row),
                  pl.BlockSpec((rows, KV_DIM), row), pl.BlockSpec((crow, KV_DIM), row),
                  pl.BlockSpec((crow, KV_DIM), row), _full(bias.shape)],
        out_specs=pl.BlockSpec((rows, N_HEADS * HEAD_DIM), row),
        out_shape=jax.ShapeDtypeStruct((n, N_HEADS * HEAD_DIM), BF16),
        compiler_params=_params("arbitrary"),
        name="attn_sample",
    )(sinks, q, k, v, ck, cv, bias)


def _proj_res_kernel(o_ref, x_ref, w_ref, y_ref):
    y_ref[...] = x_ref[...] + _mm(o_ref[...], w_ref[...])


def _proj_res(o, x, w):
    n = o.shape[0]
    t = _row_tile(n)
    row = lambda i: (i, 0)
    return pl.pallas_call(
        _proj_res_kernel,
        grid=(n // t,),
        in_specs=[pl.BlockSpec((t, o.shape[1]), row), pl.BlockSpec((t, D_MODEL), row), _full(w.shape)],
        out_specs=pl.BlockSpec((t, D_MODEL), row),
        out_shape=jax.ShapeDtypeStruct((n, D_MODEL), F32),
        compiler_params=_params("arbitrary"),
        name="proj_res",
    )(o, x, w)


def _alibi_slopes():
    return jnp.exp2(-8.0 * jnp.arange(1, N_HEADS + 1, dtype=F32) / N_HEADS)


def _prompt_bias():
    dist = WINDOW + jnp.arange(WINDOW)[:, None] - jnp.arange(2 * WINDOW)[None, :]
    allowed = (dist >= 0) & (dist <= WINDOW)
    b = -(_alibi_slopes()[:, None, None] * dist.astype(F32)[None])
    return jnp.where(allowed[None], b, NEG_INF)


def _sample_bias(t_new, n_cols):
    c = jnp.arange(n_cols)
    n_cache = SAMPLE_SEQS * WINDOW
    n_new = SAMPLE_SEQS * t_new
    is_cache = c < n_cache
    is_new = (c >= n_cache) & (c < n_cache + n_new)
    seq_c = jnp.where(is_cache, c // WINDOW, (c - n_cache) // t_new)
    pos_c = jnp.where(is_cache, c % WINDOW, WINDOW + (c - n_cache) % t_new)
    r = jnp.arange(n_new)
    seq_r, tok_r = r // t_new, r % t_new
    dist = WINDOW + tok_r[:, None] - pos_c[None, :]
    allowed = (seq_r[:, None] == seq_c[None, :]) & (is_cache | is_new)[None, :] & (dist >= 0) & (dist <= WINDOW)
    b = -(_alibi_slopes()[:, None, None] * dist.astype(F32)[None])
    return jnp.where(allowed[None], b, NEG_INF)


def _head_indicator(n_heads):
    ch = jnp.arange(n_heads * HEAD_DIM) // HEAD_DIM
    ind = (ch[:, None] == jnp.arange(LANES)[None, :]).astype(BF16)
    return ind, ind.T


def kernel(x_prompt, x_sample, state_conv, cache_k, cache_v, p_prompt, p_sample, norm_mix, norm_ffn, norm_ple,
           conv_w_in, conv_b_in, conv_w_dw, conv_b_dw, conv_ln_g, conv_ln_b, conv_w_out, conv_b_out, attn_w_qkv,
           attn_q_norm, attn_k_norm, attn_sinks, attn_w_o, moe_w_rg, moe_b_rg, moe_w_re, moe_b_re, moe_w_gate,
           moe_w_up, moe_w_down, ple_w_gate, ple_w_proj):
    bp, seq, d = x_prompt.shape
    bs, t_new, _ = x_sample.shape
    assert bp == 1 and d == D_MODEL and seq % WINDOW == 0 and bs % SAMPLE_SEQS == 0
    depth = norm_mix.shape[0]
    row2 = lambda a: a.reshape(1, -1)

    y_p = x_prompt.reshape(seq, d)
    y_s = x_sample.reshape(bs * t_new, d)
    conv_p, conv_s, k_p, v_p, k_s, v_s = [], [], [], [], [], []

    for i in range(depth):
        j = i // 2
        g_mix = row2(norm_mix[i])
        if i % 2 == 0:
            w_in = conv_w_in[j].astype(BF16)
            b_in = row2(conv_b_in[j])
            tail = (conv_w_dw[j], row2(conv_b_dw[j]), row2(conv_ln_g[j]), row2(conv_ln_b[j]),
                    conv_w_out[j].astype(BF16), row2(conv_b_out[j]))
            u_p = _conv_in(y_p, g_mix, w_in, b_in)
            u_s = _conv_in(y_s, g_mix, w_in, b_in)
            conv_p.append(u_p[seq - CONV_STATE:].reshape(1, CONV_STATE, d))
            ue = jnp.concatenate([state_conv[j], u_s.reshape(bs, t_new, d)], axis=1)
            conv_s.append(ue[:, t_new:])
            y_p = _conv_out_p(u_p, y_p, *tail)
            ys_t = _conv_out_s(ue.transpose(1, 0, 2), y_s.reshape(bs, t_new, d).transpose(1, 0, 2), *tail)
            y_s = ys_t.transpose(1, 0, 2).reshape(bs * t_new, d)
        else:
            w_qkv = attn_w_qkv[j].astype(BF16)
            w_o = attn_w_o[j].astype(BF16)
            iq, iqt = _head_indicator(N_HEADS)
            ik, ikt = _head_indicator(N_KV_HEADS)
            qg = row2(jnp.tile(attn_q_norm[j], N_HEADS))
            kg = row2(jnp.tile(attn_k_norm[j], N_KV_HEADS))
            sinks = attn_sinks[j]
            q1, k1, v1 = _qkv(y_p, g_mix, w_qkv, iq, iqt, ik, ikt, qg, kg)
            q2, k2, v2 = _qkv(y_s, g_mix, w_qkv, iq, iqt, ik, ikt, qg, kg)
            k_p.append(k1[seq - WINDOW:].reshape(1, WINDOW, N_KV_HEADS, HEAD_DIM))
            v_p.append(v1[seq - WINDOW:].reshape(1, WINDOW, N_KV_HEADS, HEAD_DIM))
            shp = (bs, t_new, N_KV_HEADS, HEAD_DIM)
            k_s.append(jnp.concatenate([cache_k[j][:, t_new:], k2.reshape(shp)], axis=1))
            v_s.append(jnp.concatenate([cache_v[j][:, t_new:], v2.reshape(shp)], axis=1))
            y_p = _attn_p(sinks, q1, k1, v1, y_p, _prompt_bias(), w_o)
            n_cols = -(-(SAMPLE_SEQS * (WINDOW + t_new)) // LANES) * LANES
            o_s = _attn_s(sinks, q2, k2, v2, cache_k[j].reshape(bs * WINDOW, KV_DIM),
                          cache_v[j].reshape(bs * WINDOW, KV_DIM), _sample_bias(t_new, n_cols), t_new)
            y_s = _proj_res(o_s, y_s, w_o)

        w_r = jnp.zeros((d, ROUTER_LANES), F32)
        w_r = w_r.at[:, :N_EXPERT_GROUPS].set(moe_w_rg[i]).at[:, EXPERT_LANE0:EXPERT_LANE0 + N_EXPERTS].set(moe_w_re[i])
        b_r = jnp.zeros((1, ROUTER_LANES), F32)
        b_r = b_r.at[0, :N_EXPERT_GROUPS].set(moe_b_rg[i]).at[0, EXPERT_LANE0:EXPERT_LANE0 + N_EXPERTS].set(moe_b_re[i])
        moe = (row2(norm_ffn[i]), w_r, b_r, moe_w_gate[i].astype(BF16), moe_w_up[i].astype(BF16),
               moe_w_down[i].astype(BF16))
        ple = (row2(norm_ple[i]), ple_w_gate[i].astype(BF16), ple_w_proj[i].astype(BF16))
        y_p = _moe_ple(y_p, *moe, p_prompt[i].reshape(seq, PLE_DIM), *ple)
        y_s = _moe_ple(y_s, *moe, p_sample[i].reshape(bs * t_new, PLE_DIM), *ple)

    return (y_p.reshape(1, seq, d), y_s.reshape(bs, t_new, d), jnp.stack(conv_p), jnp.stack(conv_s),
            jnp.stack(k_p), jnp.stack(v_p), jnp.stack(k_s), jnp.stack(v_s))
```

```python
import functools

import numpy as np

import jax
import jax.numpy as jnp
from jax import lax
from jax.experimental import pallas as pl
from jax.experimental.pallas import tpu as pltpu

F32 = jnp.float32
BF16 = jnp.bfloat16

D_MODEL = 1024
PLE_DIM = 256
CONV_WIDTH = 31
CONV_STATE = CONV_WIDTH - 1
N_HEADS = 16
N_KV_HEADS = 4
HEAD_DIM = 64
GROUP = N_HEADS // N_KV_HEADS
WINDOW = 128
KV_DIM = N_KV_HEADS * HEAD_DIM
N_EXPERT_GROUPS = 4
EXPERTS_PER_GROUP = 4
N_EXPERTS = 16
D_EXPERT = 256
EPS = 1e-6
NEG_INF = -1e30

LANES = 128
ROUTER_LANES = LANES
EXPERT_LANE0 = N_EXPERT_GROUPS
HALO = 32
CONV_ROWS = 32
NORM_ROWS = 128
CONV_PITCH = D_MODEL // LANES + 1
SAMPLE_SEQS = 8
PAIRS_PER_GROUP = EXPERTS_PER_GROUP * (EXPERTS_PER_GROUP - 1) // 2
N_BUCKETS = N_EXPERT_GROUPS * PAIRS_PER_GROUP
TM = 256
PAY_WIDTH = D_MODEL + LANES
DMA_CHUNK = 512
VMEM_LIMIT = 48 * 1024 * 1024


def _row_tile(n):
    return 512 if n % 512 == 0 else n


def _params(*sem):
    return pltpu.CompilerParams(dimension_semantics=sem, vmem_limit_bytes=VMEM_LIMIT)


def _full(shape):
    nd = len(shape)
    return pl.BlockSpec(shape, lambda *_: (0,) * nd)


def _rms(x, g):
    ms = jnp.mean(x * x, axis=-1, keepdims=True)
    return x * lax.rsqrt(ms + EPS) * g


def _sigmoid(x):
    return 1.0 / (1.0 + jnp.exp(-x))


def _mm(a, b):
    return jnp.dot(a, b, preferred_element_type=F32)


def _conv_in_kernel(x_ref, g_ref, w_ref, b_ref, u_ref):
    h = _rms(x_ref[...], g_ref[...]).astype(BF16)
    z = _mm(h, w_ref[...]) + b_ref[...]
    u_ref[...] = z[:, :D_MODEL] * _sigmoid(z[:, D_MODEL:])


def _conv_in(x, g, w, b):
    n = x.shape[0]
    t = _row_tile(n)
    return pl.pallas_call(
        _conv_in_kernel,
        grid=(n // t,),
        in_specs=[pl.BlockSpec((t, D_MODEL), lambda i: (i, 0)), _full(g.shape), _full(w.shape), _full(b.shape)],
        out_specs=pl.BlockSpec((t, D_MODEL), lambda i: (i, 0)),
        out_shape=jax.ShapeDtypeStruct((n, D_MODEL), F32),
        compiler_params=_params("arbitrary"),
        name="conv_in",
    )(x, g, w, b)


def _ln_silu(c, g, b):
    mu = jnp.mean(c, axis=-1, keepdims=True)
    xc = c - mu
    var = jnp.mean(xc * xc, axis=-1, keepdims=True)
    cn = xc * lax.rsqrt(var + EPS) * g + b
    return cn * _sigmoid(cn)


def _conv_out_p_kernel(u_ref, halo_ref, x_ref, wdw_ref, bdw_ref, lng_ref, lnb_ref, wout_ref, bout_ref, y_ref,
                       ubuf, cbuf, hbuf):
    t = u_ref.shape[0]
    i = pl.program_id(0)
    nj = D_MODEL // LANES

    def put(r, j, val):
        ubuf[pl.ds(r * CONV_PITCH + j, 8, stride=CONV_PITCH), :] = val

    for r in range(0, HALO, 8):
        for j in range(nj):
            put(r, j, jnp.where(i > 0, halo_ref[r:r + 8, j * LANES:(j + 1) * LANES], 0.0))

    def fill(rr, carry):
        r = pl.multiple_of(rr * 8, 8)
        for j in range(nj):
            put(r + HALO, j, u_ref[pl.ds(r, 8), j * LANES:(j + 1) * LANES])
        return carry

    lax.fori_loop(0, t // 8, fill, 0)

    def conv_chunk(rr, carry):
        r0 = pl.multiple_of(rr * CONV_ROWS, CONV_ROWS)
        for j in range(nj):
            lanes = slice(j * LANES, (j + 1) * LANES)
            accs = [None] * (CONV_ROWS // 8)
            for k in range(CONV_WIDTH):
                wk = wdw_ref[k:k + 1, lanes]
                for q in range(CONV_ROWS // 8):
                    r = r0 + (HALO - CONV_STATE + k + 8 * q)
                    term = wk * ubuf[pl.ds(r * CONV_PITCH + j, 8, stride=CONV_PITCH), :]
                    accs[q] = term if k == 0 else accs[q] + term
            for q in range(CONV_ROWS // 8):
                cbuf[pl.ds(r0 + 8 * q, 8), lanes] = accs[q]
        return carry

    lax.fori_loop(0, t // CONV_ROWS, conv_chunk, 0)

    def norm_chunk(rr, carry):
        r0 = pl.multiple_of(rr * NORM_ROWS, NORM_ROWS)
        c = cbuf[pl.ds(r0, NORM_ROWS), :] + bdw_ref[...]
        hbuf[pl.ds(r0, NORM_ROWS), :] = _ln_silu(c, lng_ref[...], lnb_ref[...]).astype(BF16)
        return carry

    lax.fori_loop(0, t // NORM_ROWS, norm_chunk, 0)
    y_ref[...] = x_ref[...] + _mm(hbuf[...], wout_ref[...]) + bout_ref[...]


def _conv_out_p(u, x, wdw, bdw, lng, lnb, wout, bout):
    n = u.shape[0]
    t = _row_tile(n)
    hb = t // HALO
    row = lambda i: (i, 0)
    return pl.pallas_call(
        _conv_out_p_kernel,
        grid=(n // t,),
        in_specs=[pl.BlockSpec((t, D_MODEL), row),
                  pl.BlockSpec((HALO, D_MODEL), lambda i: (jnp.maximum(i * hb - 1, 0), 0)),
                  pl.BlockSpec((t, D_MODEL), row),
                  _full(wdw.shape), _full(bdw.shape), _full(lng.shape), _full(lnb.shape),
                  _full(wout.shape), _full(bout.shape)],
        out_specs=pl.BlockSpec((t, D_MODEL), row),
        out_shape=jax.ShapeDtypeStruct((n, D_MODEL), F32),
        scratch_shapes=[pltpu.VMEM(((t + HALO) * CONV_PITCH, LANES), F32), pltpu.VMEM((t, D_MODEL), F32),
                        pltpu.VMEM((t, D_MODEL), BF16)],
        compiler_params=_params("arbitrary"),
        name="conv_out_prompt",
    )(u, u, x, wdw, bdw, lng, lnb, wout, bout)


def _conv_out_s_kernel(ue_ref, x_ref, wdw_ref, bdw_ref, lng_ref, lnb_ref, wout_ref, bout_ref, y_ref):
    nt, bb, _ = x_ref.shape
    for t in range(nt):
        acc = jnp.broadcast_to(bdw_ref[...], (bb, D_MODEL))
        for k in range(CONV_WIDTH):
            acc = acc + wdw_ref[k:k + 1, :] * ue_ref[t + k]
        h = _ln_silu(acc, lng_ref[...], lnb_ref[...]).astype(BF16)
        y_ref[t] = x_ref[t] + _mm(h, wout_ref[...]) + bout_ref[...]


def _conv_out_s(ue, x, wdw, bdw, lng, lnb, wout, bout):
    nt, b, _ = x.shape
    bb = 32 if b % 32 == 0 else b
    return pl.pallas_call(
        _conv_out_s_kernel,
        grid=(b // bb,),
        in_specs=[pl.BlockSpec((ue.shape[0], bb, D_MODEL), lambda i: (0, i, 0)),
                  pl.BlockSpec((nt, bb, D_MODEL), lambda i: (0, i, 0)),
                  _full(wdw.shape), _full(bdw.shape), _full(lng.shape), _full(lnb.shape),
                  _full(wout.shape), _full(bout.shape)],
        out_specs=pl.BlockSpec((nt, bb, D_MODEL), lambda i: (0, i, 0)),
        out_shape=jax.ShapeDtypeStruct((nt, b, D_MODEL), F32),
        compiler_params=_params("arbitrary"),
        name="conv_out_sample",
    )(ue, x, wdw, bdw, lng, lnb, wout, bout)


def _route(lg):
    big = 3.0e38
    lane = lax.broadcasted_iota(jnp.int32, lg.shape, 1)
    lanef = lane.astype(F32)
    is_g = lane < N_EXPERT_GROUPS
    gl = jnp.where(is_g, lg, -big)
    gmax = jnp.max(gl, axis=1, keepdims=True)
    gsum = jnp.sum(jnp.where(is_g, jnp.exp(gl - gmax), 0.0), axis=1, keepdims=True)
    g_w = 1.0 / gsum
    g_idx = jnp.min(jnp.where(gl == gmax, lanef, big), axis=1, keepdims=True)
    rel = lanef - float(EXPERT_LANE0) - g_idx * float(EXPERTS_PER_GROUP)
    in_grp = jnp.where(rel >= 0.0, jnp.where(rel < float(EXPERTS_PER_GROUP), 1.0, 0.0), 0.0) > 0.5
    el = jnp.where(in_grp, lg, -big)
    e1 = jnp.max(el, axis=1, keepdims=True)
    i1 = jnp.min(jnp.where(el == e1, lanef, big), axis=1, keepdims=True)
    el2 = jnp.where(lanef == i1, -big, el)
    e2 = jnp.max(el2, axis=1, keepdims=True)
    i2 = jnp.min(jnp.where(el2 == e2, lanef, big), axis=1, keepdims=True)
    tt = jnp.exp(e2 - e1)
    w1 = g_w / (1.0 + tt)
    w2 = g_w * tt / (1.0 + tt)
    base = float(EXPERT_LANE0) + g_idx * float(EXPERTS_PER_GROUP)
    a = jnp.minimum(i1, i2) - base
    b = jnp.maximum(i1, i2) - base
    pair = a * (7.0 - a) * 0.5 + (b - a - 1.0)
    first_is_lo = i1 < i2
    return (g_idx * float(PAIRS_PER_GROUP) + pair, jnp.where(first_is_lo, w1, w2), jnp.where(first_is_lo, w2, w1))


def _pick(i, n_prompt_tiles, prompt_ref, sample_ref):
    return jnp.where(i < n_prompt_tiles, prompt_ref[...], sample_ref[...])


def _route_kernel(n_prompt_tiles, yp_ref, ys_ref, g_ref, wr_ref, br_ref, tri_ref, pay_ref, meta_ref, cnt_ref, carry):
    i = pl.program_id(0)

    @pl.when(i == 0)
    def _():
        carry[...] = jnp.zeros_like(carry)

    xf = _rms(_pick(i, n_prompt_tiles, yp_ref, ys_ref), g_ref[...])
    logits = jnp.dot(xf, wr_ref[...], precision=lax.Precision.HIGHEST, preferred_element_type=F32) + br_ref[...]
    bucket, w_lo, w_hi = _route(logits)
    lane = lax.broadcasted_iota(jnp.int32, logits.shape, 1)
    onehot = jnp.where(lane.astype(F32) == bucket, 1.0, 0.0)
    before = _mm(tri_ref[...], onehot.astype(BF16)) + carry[...]
    rank = jnp.sum(onehot * before, axis=1, keepdims=True)
    carry[...] += jnp.sum(onehot, axis=0, keepdims=True)
    cnt_ref[...] = carry[...]
    meta_ref[...] = jnp.where(lane == 0, bucket, jnp.where(lane == 1, rank, 0.0))

    pay_ref[:, :D_MODEL] = xf
    pay_ref[:, D_MODEL:] = jnp.where(lane == 0, w_lo, jnp.where(lane == 1, w_hi, 0.0))


def _route_call(y_p, y_s, g, wr, br, tri):
    t = y_s.shape[0]
    npt = y_p.shape[0] // t
    n = y_p.shape[0] + t
    pidx = lambda i: (jnp.minimum(i, npt - 1), 0)
    return pl.pallas_call(
        functools.partial(_route_kernel, npt),
        grid=(npt + 1,),
        in_specs=[pl.BlockSpec((t, D_MODEL), pidx), _full(y_s.shape), _full(g.shape), _full(wr.shape),
                  _full(br.shape), _full(tri.shape)],
        out_specs=[pl.BlockSpec((t, PAY_WIDTH), lambda i: (i, 0)),
                   pl.BlockSpec((t, ROUTER_LANES), lambda i: (i, 0)), _full((1, ROUTER_LANES))],
        out_shape=[jax.ShapeDtypeStruct((n, PAY_WIDTH), F32),
                   jax.ShapeDtypeStruct((n, ROUTER_LANES), F32), jax.ShapeDtypeStruct((1, ROUTER_LANES), F32)],
        scratch_shapes=[pltpu.VMEM((1, ROUTER_LANES), F32)],
        compiler_params=_params("arbitrary"),
        name="route",
    )(y_p, y_s, g, wr, br, tri)


def _row_copies(n_rows, src_row, dst_row, src_ref, dst_ref, sems):
    n_chunks = n_rows // DMA_CHUNK

    def issue(c, slot):
        def body(jj, carry):
            t = c * DMA_CHUNK + jj
            pltpu.make_async_copy(src_ref.at[pl.ds(src_row(t), 1)], dst_ref.at[pl.ds(dst_row(t), 1)], sems.at[slot]).start()
            return carry
        lax.fori_loop(0, DMA_CHUNK, body, 0, unroll=8)

    def drain(slot):
        pltpu.make_async_copy(src_ref.at[pl.ds(0, DMA_CHUNK)], dst_ref.at[pl.ds(0, DMA_CHUNK)], sems.at[slot]).wait()

    issue(0, 0)

    def outer(c, carry):
        slot = lax.rem(c, 2)
        issue(c, slot)
        drain(1 - slot)
        return carry

    lax.fori_loop(1, n_chunks, outer, 0)
    drain((n_chunks - 1) % 2)


def _permute_kernel(pos_ref, zstart_ref, nz_ref, used_ref, src_ref, dst_ref, zbuf, zsem, sems):
    zbuf[...] = jnp.zeros_like(zbuf)

    def zero_tile(start):
        return pltpu.make_async_copy(zbuf, dst_ref.at[pl.ds(pl.multiple_of(start, TM), TM)], zsem)

    for b in range(N_BUCKETS):
        @pl.when(nz_ref[b] > 0)
        def _():
            zero_tile(zstart_ref[b]).start()
    n_tiles = dst_ref.shape[0] // TM

    def start_unused(i, carry):
        zero_tile(i * TM).start()
        return carry

    lax.fori_loop(used_ref[0], n_tiles, start_unused, 0)
    for b in range(N_BUCKETS):
        @pl.when(nz_ref[b] > 0)
        def _():
            zero_tile(0).wait()

    def wait_unused(i, carry):
        zero_tile(0).wait()
        return carry

    lax.fori_loop(used_ref[0], n_tiles, wait_unused, 0)
    _row_copies(src_ref.shape[0], lambda t: t, lambda t: pos_ref[t], src_ref, dst_ref, sems)


def _permute(pos, zstart, nz, used, payload, n_rows_out):
    return pl.pallas_call(
        _permute_kernel,
        grid_spec=pltpu.PrefetchScalarGridSpec(
            num_scalar_prefetch=4, grid=(1,),
            in_specs=[pl.BlockSpec(memory_space=pl.ANY)],
            out_specs=pl.BlockSpec(memory_space=pl.ANY),
            scratch_shapes=[pltpu.VMEM((TM, PAY_WIDTH), payload.dtype), pltpu.SemaphoreType.DMA(()),
                            pltpu.SemaphoreType.DMA((2,))]),
        out_shape=jax.ShapeDtypeStruct((n_rows_out, PAY_WIDTH), payload.dtype),
        compiler_params=_params("arbitrary"),
        name="permute",
    )(pos, zstart, nz, used, payload)


def _unpermute_kernel(pos_ref, src_ref, dst_ref, sems):
    _row_copies(dst_ref.shape[0], lambda t: pos_ref[t], lambda t: t, src_ref, dst_ref, sems)


def _unpermute(pos, ys, n_tokens):
    return pl.pallas_call(
        _unpermute_kernel,
        grid_spec=pltpu.PrefetchScalarGridSpec(
            num_scalar_prefetch=1, grid=(1,),
            in_specs=[pl.BlockSpec(memory_space=pl.ANY)],
            out_specs=pl.BlockSpec(memory_space=pl.ANY),
            scratch_shapes=[pltpu.SemaphoreType.DMA((2,))]),
        out_shape=jax.ShapeDtypeStruct((n_tokens, D_MODEL), ys.dtype),
        compiler_params=_params("arbitrary"),
        name="unpermute",
    )(pos, ys)


def _silu(x):
    return x * _sigmoid(x)


def _experts_kernel(tix_ref, lo_ref, hi_ref, valid_ref, fresh_ref, xs_ref, wgl_ref, wgh_ref, wul_ref, wuh_ref,
                    wdl_ref, wdh_ref, ys_ref, wup_s, wdn_s):
    i = pl.program_id(0)

    @pl.when(fresh_ref[i] > 0)
    def _():
        for s, ref in enumerate((wgl_ref, wgh_ref, wul_ref, wuh_ref)):
            wup_s[s] = ref[0].astype(BF16)
        for s, ref in enumerate((wdl_ref, wdh_ref)):
            wdn_s[s] = ref[0].astype(BF16)

    @pl.when(valid_ref[i] == 0)
    def _():
        ys_ref[...] = jnp.zeros_like(ys_ref)

    @pl.when(valid_ref[i] > 0)
    def _():
        x = xs_ref[:, :D_MODEL].astype(BF16)
        gates = xs_ref[:, D_MODEL:]
        h_lo = _silu(_mm(x, wup_s[0])) * _mm(x, wup_s[2]) * gates[:, 0:1]
        h_hi = _silu(_mm(x, wup_s[1])) * _mm(x, wup_s[3]) * gates[:, 1:2]
        ys_ref[...] = _mm(h_lo.astype(BF16), wdn_s[0]) + _mm(h_hi.astype(BF16), wdn_s[1])


def _experts(tix, lo, hi, valid, fresh, xs, wg, wu, wd):
    n_tiles = xs.shape[0] // TM
    row = lambda i, tix, lo, hi, valid, fresh: (tix[i], 0)
    e_lo = lambda i, tix, lo, hi, valid, fresh: (lo[i], 0, 0)
    e_hi = lambda i, tix, lo, hi, valid, fresh: (hi[i], 0, 0)
    up = pl.BlockSpec((1, D_MODEL, D_EXPERT), e_lo), pl.BlockSpec((1, D_MODEL, D_EXPERT), e_hi)
    down = pl.BlockSpec((1, D_EXPERT, D_MODEL), e_lo), pl.BlockSpec((1, D_EXPERT, D_MODEL), e_hi)
    return pl.pallas_call(
        _experts_kernel,
        grid_spec=pltpu.PrefetchScalarGridSpec(
            num_scalar_prefetch=5, grid=(n_tiles,),
            in_specs=[pl.BlockSpec((TM, PAY_WIDTH), row), *up, *up, *down],
            out_specs=pl.BlockSpec((TM, D_MODEL), row),
            scratch_shapes=[pltpu.VMEM((4, D_MODEL, D_EXPERT), BF16), pltpu.VMEM((2, D_EXPERT, D_MODEL), BF16)]),
        out_shape=jax.ShapeDtypeStruct((xs.shape[0], D_MODEL), F32),
        compiler_params=_params("arbitrary"),
        name="experts",
    )(tix, lo, hi, valid, fresh, xs, wg, wg, wu, wu, wd, wd)


def _ple_kernel(n_prompt_tiles, yp_ref, ys_ref, mo_ref, pp_ref, ps_ref, g_ref, wg_ref, wp_ref, op_ref, os_ref):
    i = pl.program_id(0)
    y2 = _pick(i, n_prompt_tiles, yp_ref, ys_ref) + mo_ref[...]
    hn = _rms(y2, g_ref[...]).astype(BF16)
    gt = _sigmoid(_mm(hn, wg_ref[...]))
    pr = _mm(_pick(i, n_prompt_tiles, pp_ref, ps_ref).astype(BF16), wp_ref[...])
    y3 = y2 + gt * pr

    @pl.when(i < n_prompt_tiles)
    def _():
        op_ref[...] = y3

    @pl.when(i >= n_prompt_tiles)
    def _():
        os_ref[...] = y3


def _ple(y_p, y_s, mo, p_p, p_s, g, wg, wp):
    t = y_s.shape[0]
    npt = y_p.shape[0] // t
    pidx = lambda i: (jnp.minimum(i, npt - 1), 0)
    return pl.pallas_call(
        functools.partial(_ple_kernel, npt),
        grid=(npt + 1,),
        in_specs=[pl.BlockSpec((t, D_MODEL), pidx), _full(y_s.shape),
                  pl.BlockSpec((t, D_MODEL), lambda i: (i, 0)),
                  pl.BlockSpec((t, PLE_DIM), pidx), _full(p_s.shape), _full(g.shape), _full(wg.shape), _full(wp.shape)],
        out_specs=[pl.BlockSpec((t, D_MODEL), pidx), _full(y_s.shape)],
        out_shape=[jax.ShapeDtypeStruct(y_p.shape, F32), jax.ShapeDtypeStruct(y_s.shape, F32)],
        compiler_params=_params("arbitrary"),
        name="ple",
    )(y_p, y_s, mo, p_p, p_s, g, wg, wp)


def _bucket_experts():
    pairs = [(a, b) for a in range(EXPERTS_PER_GROUP) for b in range(a + 1, EXPERTS_PER_GROUP)]
    lo = [g * EXPERTS_PER_GROUP + a for g in range(N_EXPERT_GROUPS) for a, _ in pairs]
    hi = [g * EXPERTS_PER_GROUP + b for g in range(N_EXPERT_GROUPS) for _, b in pairs]
    return jnp.array(lo, jnp.int32), jnp.array(hi, jnp.int32)


def _moe_ple(y_p, y_s, gffn, wr, br, wg, wu, wd, p_p, p_s, gple, pwg, pwp):
    t = y_s.shape[0]
    n = y_p.shape[0] + t
    tri = jnp.asarray(np.tri(t, t, -1, dtype=np.float32), BF16)
    payload, meta, counts = _route_call(y_p, y_s, gffn, wr, br, tri)

    bucket = meta[:, 0].astype(jnp.int32)
    rank = meta[:, 1].astype(jnp.int32)
    cnt = counts[0, :N_BUCKETS].astype(jnp.int32)
    padded = (cnt + TM - 1) // TM * TM
    ends = jnp.cumsum(padded)
    pos = jnp.take(ends - padded, bucket) + rank
    n_rows = n + N_BUCKETS * TM
    n_tiles = n_rows // TM
    tile_start = jnp.arange(n_tiles, dtype=jnp.int32) * TM
    valid = tile_start < ends[-1]
    used = (ends[-1] // TM).astype(jnp.int32)
    tix = jnp.arange(n_tiles, dtype=jnp.int32)
    tile_bucket = jnp.sum((ends[None, :] <= (jnp.minimum(tix, used - 1) * TM)[:, None]).astype(jnp.int32), axis=1)
    lo_tab, hi_tab = _bucket_experts()
    lo = jnp.take(lo_tab, tile_bucket)
    hi = jnp.take(hi_tab, tile_bucket)
    prev_bucket = jnp.concatenate([jnp.full((1,), -1, jnp.int32), tile_bucket[:-1]])
    fresh = valid & (tile_bucket != prev_bucket)

    xs = _permute(pos, (ends - TM).astype(jnp.int32), (cnt > 0).astype(jnp.int32), used.reshape(1), payload, n_rows)
    ys = _experts(tix, lo, hi, valid.astype(jnp.int32), fresh.astype(jnp.int32), xs, wg, wu, wd)
    mo = _unpermute(pos, ys, n)
    return _ple(y_p, y_s, mo, p_p, p_s, gple, pwg, pwp)


def _split_bf16(a):
    hi = a.astype(BF16)
    lo = (a - hi.astype(F32)).astype(BF16)
    return hi, lo


def _head_norm(a, ind_ref, indt_ref, gain):
    hi, lo = _split_bf16(a * a)
    ss = _mm(hi, ind_ref[...]) + _mm(lo, ind_ref[...])
    inv = lax.rsqrt(ss * (1.0 / HEAD_DIM) + EPS)
    ihi, ilo = _split_bf16(inv)
    invb = _mm(ihi, indt_ref[...]) + _mm(ilo, indt_ref[...])
    return a * invb * gain


def _qkv_kernel(x_ref, g_ref, w_ref, iq_ref, iqt_ref, ik_ref, ikt_ref, qg_ref, kg_ref, q_ref, k_ref, v_ref):
    h = _rms(x_ref[...], g_ref[...]).astype(BF16)
    qkv = _mm(h, w_ref[...])
    nq = N_HEADS * HEAD_DIM
    q = _head_norm(qkv[:, :nq], iq_ref, iqt_ref, qg_ref[...])
    k = _head_norm(qkv[:, nq:nq + KV_DIM], ik_ref, ikt_ref, kg_ref[...])
    q_ref[...] = (q * (HEAD_DIM ** -0.5)).astype(BF16)
    k_ref[...] = k
    v_ref[...] = qkv[:, nq + KV_DIM:]


def _qkv(x, g, w, iq, iqt, ik, ikt, qg, kg):
    n = x.shape[0]
    t = _row_tile(n)
    row = lambda i: (i, 0)
    return pl.pallas_call(
        _qkv_kernel,
        grid=(n // t,),
        in_specs=[pl.BlockSpec((t, D_MODEL), row)] + [_full(a.shape) for a in (g, w, iq, iqt, ik, ikt, qg, kg)],
        out_specs=[pl.BlockSpec((t, N_HEADS * HEAD_DIM), row), pl.BlockSpec((t, KV_DIM), row),
                   pl.BlockSpec((t, KV_DIM), row)],
        out_shape=[jax.ShapeDtypeStruct((n, N_HEADS * HEAD_DIM), BF16), jax.ShapeDtypeStruct((n, KV_DIM), F32),
                   jax.ShapeDtypeStruct((n, KV_DIM), F32)],
        compiler_params=_params("arbitrary"),
        name="qkv",
    )(x, g, w, iq, iqt, ik, ikt, qg, kg)


def _dup_heads(a):
    out = []
    for s in range(KV_DIM // LANES):
        sl = a[:, s * LANES:(s + 1) * LANES]
        sw = pltpu.roll(sl, HEAD_DIM, axis=1)
        low = lax.broadcasted_iota(jnp.int32, sl.shape, 1) < HEAD_DIM
        out.append(jnp.where(low, sl, sw))
        out.append(jnp.where(low, sw, sl))
    return jnp.concatenate(out, axis=1).astype(BF16)


def _attend(q_rows, k2, v2, bias_of, sink_of, extra_mask):
    m_rows = q_rows.shape[0]
    low_q = lax.broadcasted_iota(jnp.int32, (m_rows, LANES), 1) < HEAD_DIM
    low_k = lax.broadcasted_iota(jnp.int32, (k2.shape[0], LANES), 1) < HEAD_DIM
    zero_q = jnp.zeros((m_rows, LANES), BF16)
    zero_k = jnp.zeros((k2.shape[0], LANES), BF16)
    slabs = []
    for g in range(N_KV_HEADS):
        kg = k2[:, g * LANES:(g + 1) * LANES]
        vg = v2[:, g * LANES:(g + 1) * LANES]
        v_lo = jnp.where(low_k, vg, zero_k)
        v_hi = jnp.where(low_k, zero_k, vg)
        lhs = []
        for a in range(GROUP):
            h = g * GROUP + a
            qs = q_rows[:, (h // 2) * LANES:(h // 2 + 1) * LANES]
            lhs.append(jnp.where(low_q, qs, zero_q) if h % 2 == 0 else jnp.where(low_q, zero_q, qs))
        s = lax.dot_general(jnp.concatenate(lhs, axis=0), kg, (((1,), (1,)), ((), ())), preferred_element_type=F32)
        probs, rinv = [], []
        for a in range(GROUP):
            h = g * GROUP + a
            sa = s[a * m_rows:(a + 1) * m_rows] + bias_of(h)
            if extra_mask is not None:
                sa = jnp.where(extra_mask, NEG_INF, sa)
            sink = sink_of(h)
            m = jnp.maximum(jnp.max(sa, axis=1, keepdims=True), sink)
            p = jnp.exp(sa - m)
            den = jnp.sum(p, axis=1, keepdims=True) + jnp.exp(sink - m)
            probs.append(p.astype(BF16))
            rinv.append(1.0 / den)
        for sp in range(GROUP // 2):
            o = _mm(probs[2 * sp], v_lo) + _mm(probs[2 * sp + 1], v_hi)
            slabs.append(o * jnp.where(low_q, rinv[2 * sp], rinv[2 * sp + 1]))
    return jnp.concatenate(slabs, axis=1)


def _attn_p_kernel(sink_ref, q_ref, k_ref, v_ref, x_ref, bias_ref, wo_ref, y_ref, kbuf, vbuf, obuf):
    t = q_ref.shape[0]
    i = pl.program_id(0)

    @pl.when(i == 0)
    def _():
        kbuf[0:WINDOW, :] = jnp.zeros((WINDOW, 2 * KV_DIM), BF16)
        vbuf[0:WINDOW, :] = jnp.zeros((WINDOW, 2 * KV_DIM), BF16)

    @pl.when(i > 0)
    def _():
        kbuf[0:WINDOW, :] = kbuf[t:t + WINDOW, :]
        vbuf[0:WINDOW, :] = vbuf[t:t + WINDOW, :]

    kbuf[WINDOW:, :] = _dup_heads(k_ref[...])
    vbuf[WINDOW:, :] = _dup_heads(v_ref[...])
    col = lax.broadcasted_iota(jnp.int32, (WINDOW, 2 * WINDOW), 1)
    for j in range(t // WINDOW):
        rows = slice(j * WINDOW, (j + 1) * WINDOW)
        keys = slice(j * WINDOW, (j + 2) * WINDOW)
        extra = jnp.logical_and(i == 0, col < WINDOW) if j == 0 else None
        o = _attend(q_ref[rows, :], kbuf[keys, :], vbuf[keys, :], lambda h: bias_ref[h], lambda h: sink_ref[h], extra)
        obuf[rows, :] = o.astype(BF16)
    y_ref[...] = x_ref[...] + _mm(obuf[...], wo_ref[...])


def _attn_p(sinks, q, k, v, x, bias, wo):
    n = q.shape[0]
    t = _row_tile(n)
    row = lambda i: (i, 0)
    return pl.pallas_call(
        _attn_p_kernel,
        grid=(n // t,),
        in_specs=[pl.BlockSpec(memory_space=pltpu.SMEM),
                  pl.BlockSpec((t, N_HEADS * HEAD_DIM), row), pl.BlockSpec((t, KV_DIM), row),
                  pl.BlockSpec((t, KV_DIM), row), pl.BlockSpec((t, D_MODEL), row),
                  _full(bias.shape), _full(wo.shape)],
        out_specs=pl.BlockSpec((t, D_MODEL), row),
        out_shape=jax.ShapeDtypeStruct((n, D_MODEL), F32),
        scratch_shapes=[pltpu.VMEM((t + WINDOW, 2 * KV_DIM), BF16), pltpu.VMEM((t + WINDOW, 2 * KV_DIM), BF16),
                        pltpu.VMEM((t, N_HEADS * HEAD_DIM), BF16)],
        compiler_params=_params("arbitrary"),
        name="attn_prompt",
    )(sinks, q, k, v, x, bias, wo)


def _attn_s_kernel(sink_ref, q_ref, k_ref, v_ref, ck_ref, cv_ref, bias_ref, o_ref):
    pad = bias_ref.shape[2] - ck_ref.shape[0] - k_ref.shape[0]
    zpad = jnp.zeros((pad, KV_DIM), F32)
    k2 = _dup_heads(jnp.concatenate([ck_ref[...], k_ref[...], zpad], axis=0))
    v2 = _dup_heads(jnp.concatenate([cv_ref[...], v_ref[...], zpad], axis=0))
    o = _attend(q_ref[...], k2, v2, lambda h: bias_ref[h], lambda h: sink_ref[h], None)
    o_ref[...] = o.astype(BF16)


def _attn_s(sinks, q, k, v, ck, cv, bias, t_new):
    n = q.shape[0]
    rows = SAMPLE_SEQS * t_new
    crow = SAMPLE_SEQS * WINDOW
    row = lambda i: (i, 0)
    return pl.pallas_call(
        _attn_s_kernel,
        grid=(n // rows,),
        in_specs=[pl.BlockSpec(memory_space=pltpu.SMEM),
                  pl.BlockSpec((rows, N_HEADS * HEAD_DIM), row), pl.BlockSpec((rows, KV_DIM), row),
                  pl.BlockSpec((rows, KV_DIM), row), pl.BlockSpec((crow, KV_DIM), row),
                  pl.BlockSpec((crow, KV_DIM), row), _full(bias.shape)],
        out_specs=pl.BlockSpec((rows, N_HEADS * HEAD_DIM), row),
        out_shape=jax.ShapeDtypeStruct((n, N_HEADS * HEAD_DIM), BF16),
        compiler_params=_params("arbitrary"),
        name="attn_sample",
    )(sinks, q, k, v, ck, cv, bias)


def _proj_res_kernel(o_ref, x_ref, w_ref, y_ref):
    y_ref[...] = x_ref[...] + _mm(o_ref[...], w_ref[...])


def _proj_res(o, x, w):
    n = o.shape[0]
    t = _row_tile(n)
    row = lambda i: (i, 0)
    return pl.pallas_call(
        _proj_res_kernel,
        grid=(n // t,),
        in_specs=[pl.BlockSpec((t, o.shape[1]), row), pl.BlockSpec((t, D_MODEL), row), _full(w.shape)],
        out_specs=pl.BlockSpec((t, D_MODEL), row),
        out_shape=jax.ShapeDtypeStruct((n, D_MODEL), F32),
        compiler_params=_params("arbitrary"),
        name="proj_res",
    )(o, x, w)


def _alibi_slopes():
    return np.exp2(-8.0 * np.arange(1, N_HEADS + 1, dtype=np.float64) / N_HEADS).astype(np.float32)


def _band_bias(dist, allowed):
    b = -(_alibi_slopes()[:, None, None] * dist.astype(np.float32)[None])
    return jnp.asarray(np.where(allowed[None], b, np.float32(NEG_INF)).astype(np.float32))


def _prompt_bias():
    dist = WINDOW + np.arange(WINDOW)[:, None] - np.arange(2 * WINDOW)[None, :]
    return _band_bias(dist, (dist >= 0) & (dist <= WINDOW))


def _sample_bias(t_new, n_cols):
    c = np.arange(n_cols)
    n_cache = SAMPLE_SEQS * WINDOW
    n_new = SAMPLE_SEQS * t_new
    is_cache = c < n_cache
    is_new = (c >= n_cache) & (c < n_cache + n_new)
    seq_c = np.where(is_cache, c // WINDOW, (c - n_cache) // t_new)
    pos_c = np.where(is_cache, c % WINDOW, WINDOW + (c - n_cache) % t_new)
    r = np.arange(n_new)
    seq_r, tok_r = r // t_new, r % t_new
    dist = WINDOW + tok_r[:, None] - pos_c[None, :]
    allowed = (seq_r[:, None] == seq_c[None, :]) & (is_cache | is_new)[None, :] & (dist >= 0) & (dist <= WINDOW)
    return _band_bias(dist, allowed)


def _head_indicator(n_heads):
    ch = np.arange(n_heads * HEAD_DIM) // HEAD_DIM
    ind = (ch[:, None] == np.arange(LANES)[None, :]).astype(np.float32)
    return jnp.asarray(ind, BF16), jnp.asarray(ind.T, BF16)


def kernel(x_prompt, x_sample, state_conv, cache_k, cache_v, p_prompt, p_sample, norm_mix, norm_ffn, norm_ple,
           conv_w_in, conv_b_in, conv_w_dw, conv_b_dw, conv_ln_g, conv_ln_b, conv_w_out, conv_b_out, attn_w_qkv,
           attn_q_norm, attn_k_norm, attn_sinks, attn_w_o, moe_w_rg, moe_b_rg, moe_w_re, moe_b_re, moe_w_gate,
           moe_w_up, moe_w_down, ple_w_gate, ple_w_proj):
    bp, seq, d = x_prompt.shape
    bs, t_new, _ = x_sample.shape
    assert bp == 1 and d == D_MODEL and seq % WINDOW == 0 and bs % SAMPLE_SEQS == 0
    assert seq % (bs * t_new) == 0 and (bs * t_new) % DMA_CHUNK == 0
    depth = norm_mix.shape[0]
    row2 = lambda a: a.reshape(1, -1)

    y_p = x_prompt.reshape(seq, d)
    y_s = x_sample.reshape(bs * t_new, d)
    conv_p, conv_s, k_p, v_p, k_s, v_s = [], [], [], [], [], []

    for i in range(depth):
        j = i // 2
        g_mix = row2(norm_mix[i])
        if i % 2 == 0:
            w_in = conv_w_in[j].astype(BF16)
            b_in = row2(conv_b_in[j])
            tail = (conv_w_dw[j], row2(conv_b_dw[j]), row2(conv_ln_g[j]), row2(conv_ln_b[j]),
                    conv_w_out[j].astype(BF16), row2(conv_b_out[j]))
            u_p = _conv_in(y_p, g_mix, w_in, b_in)
            u_s = _conv_in(y_s, g_mix, w_in, b_in)
            conv_p.append(u_p[seq - CONV_STATE:].reshape(1, CONV_STATE, d))
            ue = jnp.concatenate([state_conv[j], u_s.reshape(bs, t_new, d)], axis=1)
            conv_s.append(ue[:, t_new:])
            y_p = _conv_out_p(u_p, y_p, *tail)
            ys_t = _conv_out_s(ue.transpose(1, 0, 2), y_s.reshape(bs, t_new, d).transpose(1, 0, 2), *tail)
            y_s = ys_t.transpose(1, 0, 2).reshape(bs * t_new, d)
        else:
            w_qkv = attn_w_qkv[j].astype(BF16)
            w_o = attn_w_o[j].astype(BF16)
            iq, iqt = _head_indicator(N_HEADS)
            ik, ikt = _head_indicator(N_KV_HEADS)
            qg = row2(jnp.tile(attn_q_norm[j], N_HEADS))
            kg = row2(jnp.tile(attn_k_norm[j], N_KV_HEADS))
            sinks = attn_sinks[j]
            q1, k1, v1 = _qkv(y_p, g_mix, w_qkv, iq, iqt, ik, ikt, qg, kg)
            q2, k2, v2 = _qkv(y_s, g_mix, w_qkv, iq, iqt, ik, ikt, qg, kg)
            k_p.append(k1[seq - WINDOW:].reshape(1, WINDOW, N_KV_HEADS, HEAD_DIM))
            v_p.append(v1[seq - WINDOW:].reshape(1, WINDOW, N_KV_HEADS, HEAD_DIM))
            shp = (bs, t_new, N_KV_HEADS, HEAD_DIM)
            k_s.append(jnp.concatenate([cache_k[j][:, t_new:], k2.reshape(shp)], axis=1))
            v_s.append(jnp.concatenate([cache_v[j][:, t_new:], v2.reshape(shp)], axis=1))
            y_p = _attn_p(sinks, q1, k1, v1, y_p, _prompt_bias(), w_o)
            n_cols = -(-(SAMPLE_SEQS * (WINDOW + t_new)) // LANES) * LANES
            o_s = _attn_s(sinks, q2, k2, v2, cache_k[j].reshape(bs * WINDOW, KV_DIM),
                          cache_v[j].reshape(bs * WINDOW, KV_DIM), _sample_bias(t_new, n_cols), t_new)
            y_s = _proj_res(o_s, y_s, w_o)

        w_r = jnp.zeros((d, ROUTER_LANES), F32)
        w_r = w_r.at[:, :N_EXPERT_GROUPS].set(moe_w_rg[i]).at[:, EXPERT_LANE0:EXPERT_LANE0 + N_EXPERTS].set(moe_w_re[i])
        b_r = jnp.zeros((1, ROUTER_LANES), F32)
        b_r = b_r.at[0, :N_EXPERT_GROUPS].set(moe_b_rg[i]).at[0, EXPERT_LANE0:EXPERT_LANE0 + N_EXPERTS].set(moe_b_re[i])
        moe = (row2(norm_ffn[i]), w_r, b_r, moe_w_gate[i], moe_w_up[i], moe_w_down[i])
        ple = (row2(norm_ple[i]), ple_w_gate[i].astype(BF16), ple_w_proj[i].astype(BF16))
        y_p, y_s = _moe_ple(y_p, y_s, *moe, p_prompt[i].reshape(seq, PLE_DIM), p_sample[i].reshape(bs * t_new, PLE_DIM),
                            *ple)

    return (y_p.reshape(1, seq, d), y_s.reshape(bs, t_new, d), jnp.stack(conv_p), jnp.stack(conv_s),
            jnp.stack(k_p), jnp.stack(v_p), jnp.stack(k_s), jnp.stack(v_s))
```

```python
import functools

import numpy as np

import jax
import jax.numpy as jnp
from jax import lax
from jax.experimental import pallas as pl
from jax.experimental.pallas import tpu as pltpu

F32 = jnp.float32
BF16 = jnp.bfloat16

D_MODEL = 1024
PLE_DIM = 256
CONV_WIDTH = 31
CONV_STATE = CONV_WIDTH - 1
N_HEADS = 16
N_KV_HEADS = 4
HEAD_DIM = 64
GROUP = N_HEADS // N_KV_HEADS
WINDOW = 128
KV_DIM = N_KV_HEADS * HEAD_DIM
N_EXPERT_GROUPS = 4
EXPERTS_PER_GROUP = 4
N_EXPERTS = 16
D_EXPERT = 256
EPS = 1e-6
NEG_INF = -1e30

LANES = 128
ROUTER_LANES = LANES
EXPERT_LANE0 = N_EXPERT_GROUPS
HALO = 32
CONV_ROWS = 32
NORM_ROWS = 128
CONV_PITCH = D_MODEL // LANES + 1
SAMPLE_SEQS = 8
PAIRS_PER_GROUP = EXPERTS_PER_GROUP * (EXPERTS_PER_GROUP - 1) // 2
N_BUCKETS = N_EXPERT_GROUPS * PAIRS_PER_GROUP
TM = 256
PAY_WIDTH = D_MODEL + LANES
ROW_PITCH = PAY_WIDTH // LANES
DMA_CHUNK = 512
VMEM_LIMIT = 48 * 1024 * 1024


def _row_tile(n):
    return 512 if n % 512 == 0 else n


def _params(*sem):
    return pltpu.CompilerParams(dimension_semantics=sem, vmem_limit_bytes=VMEM_LIMIT)


def _full(shape):
    nd = len(shape)
    return pl.BlockSpec(shape, lambda *_: (0,) * nd)


def _rms(x, g):
    ms = jnp.mean(x * x, axis=-1, keepdims=True)
    return x * lax.rsqrt(ms + EPS) * g


def _sigmoid(x):
    return 1.0 / (1.0 + jnp.exp(-x))


def _mm(a, b):
    return jnp.dot(a, b, preferred_element_type=F32)


def _conv_in_kernel(x_ref, g_ref, w_ref, b_ref, u_ref):
    h = _rms(x_ref[...], g_ref[...]).astype(BF16)
    z = _mm(h, w_ref[...]) + b_ref[...]
    u_ref[...] = z[:, :D_MODEL] * _sigmoid(z[:, D_MODEL:])


def _conv_in(x, g, w, b):
    n = x.shape[0]
    t = _row_tile(n)
    return pl.pallas_call(
        _conv_in_kernel,
        grid=(n // t,),
        in_specs=[pl.BlockSpec((t, D_MODEL), lambda i: (i, 0)), _full(g.shape), _full(w.shape), _full(b.shape)],
        out_specs=pl.BlockSpec((t, D_MODEL), lambda i: (i, 0)),
        out_shape=jax.ShapeDtypeStruct((n, D_MODEL), F32),
        compiler_params=_params("arbitrary"),
        name="conv_in",
    )(x, g, w, b)


def _ln_silu(c, g, b):
    mu = jnp.mean(c, axis=-1, keepdims=True)
    xc = c - mu
    var = jnp.mean(xc * xc, axis=-1, keepdims=True)
    cn = xc * lax.rsqrt(var + EPS) * g + b
    return cn * _sigmoid(cn)


def _conv_out_p_kernel(u_ref, halo_ref, x_ref, wdw_ref, bdw_ref, lng_ref, lnb_ref, wout_ref, bout_ref, y_ref,
                       ubuf, cbuf, hbuf):
    t = u_ref.shape[0]
    i = pl.program_id(0)
    nj = D_MODEL // LANES

    def put(r, j, val):
        ubuf[pl.ds(r * CONV_PITCH + j, 8, stride=CONV_PITCH), :] = val

    for r in range(0, HALO, 8):
        for j in range(nj):
            put(r, j, jnp.where(i > 0, halo_ref[r:r + 8, j * LANES:(j + 1) * LANES], 0.0))

    def fill(rr, carry):
        r = pl.multiple_of(rr * 8, 8)
        for j in range(nj):
            put(r + HALO, j, u_ref[pl.ds(r, 8), j * LANES:(j + 1) * LANES])
        return carry

    lax.fori_loop(0, t // 8, fill, 0)

    def conv_chunk(rr, carry):
        r0 = pl.multiple_of(rr * CONV_ROWS, CONV_ROWS)
        for j in range(nj):
            lanes = slice(j * LANES, (j + 1) * LANES)
            accs = [None] * (CONV_ROWS // 8)
            for k in range(CONV_WIDTH):
                wk = wdw_ref[k:k + 1, lanes]
                for q in range(CONV_ROWS // 8):
                    r = r0 + (HALO - CONV_STATE + k + 8 * q)
                    term = wk * ubuf[pl.ds(r * CONV_PITCH + j, 8, stride=CONV_PITCH), :]
                    accs[q] = term if k == 0 else accs[q] + term
            for q in range(CONV_ROWS // 8):
                cbuf[pl.ds(r0 + 8 * q, 8), lanes] = accs[q]
        return carry

    lax.fori_loop(0, t // CONV_ROWS, conv_chunk, 0)

    def norm_chunk(rr, carry):
        r0 = pl.multiple_of(rr * NORM_ROWS, NORM_ROWS)
        c = cbuf[pl.ds(r0, NORM_ROWS), :] + bdw_ref[...]
        hbuf[pl.ds(r0, NORM_ROWS), :] = _ln_silu(c, lng_ref[...], lnb_ref[...]).astype(BF16)
        return carry

    lax.fori_loop(0, t // NORM_ROWS, norm_chunk, 0)
    y_ref[...] = x_ref[...] + _mm(hbuf[...], wout_ref[...]) + bout_ref[...]


def _conv_out_p(u, x, wdw, bdw, lng, lnb, wout, bout):
    n = u.shape[0]
    t = _row_tile(n)
    hb = t // HALO
    row = lambda i: (i, 0)
    return pl.pallas_call(
        _conv_out_p_kernel,
        grid=(n // t,),
        in_specs=[pl.BlockSpec((t, D_MODEL), row),
                  pl.BlockSpec((HALO, D_MODEL), lambda i: (jnp.maximum(i * hb - 1, 0), 0)),
                  pl.BlockSpec((t, D_MODEL), row),
                  _full(wdw.shape), _full(bdw.shape), _full(lng.shape), _full(lnb.shape),
                  _full(wout.shape), _full(bout.shape)],
        out_specs=pl.BlockSpec((t, D_MODEL), row),
        out_shape=jax.ShapeDtypeStruct((n, D_MODEL), F32),
        scratch_shapes=[pltpu.VMEM(((t + HALO) * CONV_PITCH, LANES), F32), pltpu.VMEM((t, D_MODEL), F32),
                        pltpu.VMEM((t, D_MODEL), BF16)],
        compiler_params=_params("arbitrary"),
        name="conv_out_prompt",
    )(u, u, x, wdw, bdw, lng, lnb, wout, bout)


def _conv_out_s_kernel(ue_ref, x_ref, wdw_ref, bdw_ref, lng_ref, lnb_ref, wout_ref, bout_ref, y_ref):
    nt, bb, _ = x_ref.shape
    for t in range(nt):
        acc = jnp.broadcast_to(bdw_ref[...], (bb, D_MODEL))
        for k in range(CONV_WIDTH):
            acc = acc + wdw_ref[k:k + 1, :] * ue_ref[t + k]
        h = _ln_silu(acc, lng_ref[...], lnb_ref[...]).astype(BF16)
        y_ref[t] = x_ref[t] + _mm(h, wout_ref[...]) + bout_ref[...]


def _conv_out_s(ue, x, wdw, bdw, lng, lnb, wout, bout):
    nt, b, _ = x.shape
    bb = 32 if b % 32 == 0 else b
    return pl.pallas_call(
        _conv_out_s_kernel,
        grid=(b // bb,),
        in_specs=[pl.BlockSpec((ue.shape[0], bb, D_MODEL), lambda i: (0, i, 0)),
                  pl.BlockSpec((nt, bb, D_MODEL), lambda i: (0, i, 0)),
                  _full(wdw.shape), _full(bdw.shape), _full(lng.shape), _full(lnb.shape),
                  _full(wout.shape), _full(bout.shape)],
        out_specs=pl.BlockSpec((nt, bb, D_MODEL), lambda i: (0, i, 0)),
        out_shape=jax.ShapeDtypeStruct((nt, b, D_MODEL), F32),
        compiler_params=_params("arbitrary"),
        name="conv_out_sample",
    )(ue, x, wdw, bdw, lng, lnb, wout, bout)


def _route(lg):
    big = 3.0e38
    lane = lax.broadcasted_iota(jnp.int32, lg.shape, 1)
    lanef = lane.astype(F32)
    is_g = lane < N_EXPERT_GROUPS
    gl = jnp.where(is_g, lg, -big)
    gmax = jnp.max(gl, axis=1, keepdims=True)
    gsum = jnp.sum(jnp.where(is_g, jnp.exp(gl - gmax), 0.0), axis=1, keepdims=True)
    g_w = 1.0 / gsum
    g_idx = jnp.min(jnp.where(gl == gmax, lanef, big), axis=1, keepdims=True)
    rel = lanef - float(EXPERT_LANE0) - g_idx * float(EXPERTS_PER_GROUP)
    in_grp = jnp.where(rel >= 0.0, jnp.where(rel < float(EXPERTS_PER_GROUP), 1.0, 0.0), 0.0) > 0.5
    el = jnp.where(in_grp, lg, -big)
    e1 = jnp.max(el, axis=1, keepdims=True)
    i1 = jnp.min(jnp.where(el == e1, lanef, big), axis=1, keepdims=True)
    el2 = jnp.where(lanef == i1, -big, el)
    e2 = jnp.max(el2, axis=1, keepdims=True)
    i2 = jnp.min(jnp.where(el2 == e2, lanef, big), axis=1, keepdims=True)
    tt = jnp.exp(e2 - e1)
    w1 = g_w / (1.0 + tt)
    w2 = g_w * tt / (1.0 + tt)
    base = float(EXPERT_LANE0) + g_idx * float(EXPERTS_PER_GROUP)
    a = jnp.minimum(i1, i2) - base
    b = jnp.maximum(i1, i2) - base
    pair = a * (7.0 - a) * 0.5 + (b - a - 1.0)
    first_is_lo = i1 < i2
    return (g_idx * float(PAIRS_PER_GROUP) + pair, jnp.where(first_is_lo, w1, w2), jnp.where(first_is_lo, w2, w1))


def _to_row_linear(dst_ref, src_ref, n_tiles):
    def body(g, carry):
        r = pl.multiple_of(g * 8, 8)
        for j in range(n_tiles):
            dst_ref[pl.ds(r * ROW_PITCH + j, 8, stride=ROW_PITCH), :] = src_ref[pl.ds(r, 8), j * LANES:(j + 1) * LANES]
        return carry

    lax.fori_loop(0, src_ref.shape[0] // 8, body, 0)


def _from_row_linear(dst_ref, src_ref, n_tiles, pitch):
    def body(g, carry):
        r = pl.multiple_of(g * 8, 8)
        for j in range(n_tiles):
            dst_ref[pl.ds(r, 8), j * LANES:(j + 1) * LANES] = src_ref[pl.ds(r * pitch + j, 8, stride=pitch), :]
        return carry

    lax.fori_loop(0, dst_ref.shape[0] // 8, body, 0)


def _pick(i, n_prompt_tiles, prompt_ref, sample_ref):
    return jnp.where(i < n_prompt_tiles, prompt_ref[...], sample_ref[...])


def _route_kernel(n_prompt_tiles, yp_ref, ys_ref, g_ref, wr_ref, br_ref, tri_ref, pay_ref, meta_ref, cnt_ref, carry,
                  rowbuf):
    i = pl.program_id(0)

    @pl.when(i == 0)
    def _():
        carry[...] = jnp.zeros_like(carry)

    xf = _rms(_pick(i, n_prompt_tiles, yp_ref, ys_ref), g_ref[...])
    x_hi, x_lo = _split_bf16(xf)
    logits = _mm(x_hi, wr_ref[0]) + (_mm(x_lo, wr_ref[0]) + _mm(x_hi, wr_ref[1])) + br_ref[...]
    bucket, w_lo, w_hi = _route(logits)
    lane = lax.broadcasted_iota(jnp.int32, logits.shape, 1)
    onehot = jnp.where(lane.astype(F32) == bucket, 1.0, 0.0)
    before = _mm(tri_ref[...], onehot.astype(BF16)) + carry[...]
    rank = jnp.sum(onehot * before, axis=1, keepdims=True)
    carry[...] += jnp.sum(onehot, axis=0, keepdims=True)
    cnt_ref[...] = carry[...]
    meta_ref[...] = jnp.where(lane == 0, bucket, jnp.where(lane == 1, rank, 0.0))

    rowbuf[:, :D_MODEL] = xf
    rowbuf[:, D_MODEL:] = jnp.where(lane == 0, w_lo, jnp.where(lane == 1, w_hi, 0.0))
    _to_row_linear(pay_ref, rowbuf, ROW_PITCH)


def _route_call(y_p, y_s, g, wr, br, tri):
    t = y_s.shape[0]
    npt = y_p.shape[0] // t
    n = y_p.shape[0] + t
    pidx = lambda i: (jnp.minimum(i, npt - 1), 0)
    return pl.pallas_call(
        functools.partial(_route_kernel, npt),
        grid=(npt + 1,),
        in_specs=[pl.BlockSpec((t, D_MODEL), pidx), _full(y_s.shape), _full(g.shape), _full(wr.shape),
                  _full(br.shape), _full(tri.shape)],
        out_specs=[pl.BlockSpec((t * ROW_PITCH, LANES), lambda i: (i, 0)),
                   pl.BlockSpec((t, ROUTER_LANES), lambda i: (i, 0)), _full((1, ROUTER_LANES))],
        out_shape=[jax.ShapeDtypeStruct((n * ROW_PITCH, LANES), F32),
                   jax.ShapeDtypeStruct((n, ROUTER_LANES), F32), jax.ShapeDtypeStruct((1, ROUTER_LANES), F32)],
        scratch_shapes=[pltpu.VMEM((1, ROUTER_LANES), F32), pltpu.VMEM((t, PAY_WIDTH), F32)],
        compiler_params=_params("arbitrary"),
        name="route",
    )(y_p, y_s, g, wr, br, tri)


def _row_copies(n_rows, src_row, dst_row, src_pitch, dst_pitch, width, src_ref, dst_ref, sems):
    n_chunks = n_rows // DMA_CHUNK

    def issue(c, slot):
        def body(jj, carry):
            t = c * DMA_CHUNK + jj
            pltpu.make_async_copy(src_ref.at[pl.ds(src_row(t) * src_pitch, width)],
                                  dst_ref.at[pl.ds(dst_row(t) * dst_pitch, width)], sems.at[slot]).start()
            return carry
        lax.fori_loop(0, DMA_CHUNK, body, 0, unroll=8)

    def drain(slot):
        pltpu.make_async_copy(src_ref.at[pl.ds(0, DMA_CHUNK * width)], dst_ref.at[pl.ds(0, DMA_CHUNK * width)],
                              sems.at[slot]).wait()

    issue(0, 0)

    def outer(c, carry):
        slot = lax.rem(c, 2)
        issue(c, slot)
        drain(1 - slot)
        return carry

    lax.fori_loop(1, n_chunks, outer, 0)
    drain((n_chunks - 1) % 2)


def _permute_kernel(pos_ref, zstart_ref, nz_ref, used_ref, src_ref, dst_ref, zbuf, zsem, sems):
    zbuf[...] = jnp.zeros_like(zbuf)

    def zero_tile(start):
        return pltpu.make_async_copy(zbuf, dst_ref.at[pl.ds(pl.multiple_of(start * ROW_PITCH, 8), TM * ROW_PITCH)], zsem)

    for b in range(N_BUCKETS):
        @pl.when(nz_ref[b] > 0)
        def _():
            zero_tile(zstart_ref[b]).start()
    n_tiles = dst_ref.shape[0] // (TM * ROW_PITCH)

    def start_unused(i, carry):
        zero_tile(i * TM).start()
        return carry

    lax.fori_loop(used_ref[0], n_tiles, start_unused, 0)
    for b in range(N_BUCKETS):
        @pl.when(nz_ref[b] > 0)
        def _():
            zero_tile(0).wait()

    def wait_unused(i, carry):
        zero_tile(0).wait()
        return carry

    lax.fori_loop(used_ref[0], n_tiles, wait_unused, 0)
    _row_copies(src_ref.shape[0] // ROW_PITCH, lambda t: t, lambda t: pos_ref[t], ROW_PITCH, ROW_PITCH, ROW_PITCH,
                src_ref, dst_ref, sems)


def _permute(pos, zstart, nz, used, payload, n_rows_out):
    return pl.pallas_call(
        _permute_kernel,
        grid_spec=pltpu.PrefetchScalarGridSpec(
            num_scalar_prefetch=4, grid=(1,),
            in_specs=[pl.BlockSpec(memory_space=pl.ANY)],
            out_specs=pl.BlockSpec(memory_space=pl.ANY),
            scratch_shapes=[pltpu.VMEM((TM * ROW_PITCH, LANES), payload.dtype), pltpu.SemaphoreType.DMA(()),
                            pltpu.SemaphoreType.DMA((2,))]),
        out_shape=jax.ShapeDtypeStruct((n_rows_out * ROW_PITCH, LANES), payload.dtype),
        compiler_params=_params("arbitrary"),
        name="permute",
    )(pos, zstart, nz, used, payload)


def _unpermute_kernel(pos_ref, src_ref, dst_ref, sems):
    w = D_MODEL // LANES
    _row_copies(dst_ref.shape[0] // w, lambda t: pos_ref[t], lambda t: t, ROW_PITCH, w, w, src_ref, dst_ref, sems)


def _unpermute(pos, ys, n_tokens):
    return pl.pallas_call(
        _unpermute_kernel,
        grid_spec=pltpu.PrefetchScalarGridSpec(
            num_scalar_prefetch=1, grid=(1,),
            in_specs=[pl.BlockSpec(memory_space=pl.ANY)],
            out_specs=pl.BlockSpec(memory_space=pl.ANY),
            scratch_shapes=[pltpu.SemaphoreType.DMA((2,))]),
        out_shape=jax.ShapeDtypeStruct((n_tokens * (D_MODEL // LANES), LANES), ys.dtype),
        compiler_params=_params("arbitrary"),
        name="unpermute",
    )(pos, ys)


def _silu(x):
    return x * _sigmoid(x)


def _experts_kernel(tix_ref, lo_ref, hi_ref, valid_ref, fresh_ref, xs_ref, wgl_ref, wgh_ref, wul_ref, wuh_ref,
                    wdl_ref, wdh_ref, ys_ref, wup_s, wdn_s, xbuf, ybuf):
    i = pl.program_id(0)

    @pl.when(fresh_ref[i] > 0)
    def _():
        for s, ref in enumerate((wgl_ref, wgh_ref, wul_ref, wuh_ref)):
            wup_s[s] = ref[0, 0].astype(BF16)
        for s, ref in enumerate((wdl_ref, wdh_ref)):
            wdn_s[s] = ref[0, 0].astype(BF16)

    @pl.when(valid_ref[i] == 0)
    def _():
        ys_ref[...] = jnp.zeros_like(ys_ref)

    @pl.when(valid_ref[i] > 0)
    def _():
        _from_row_linear(xbuf, xs_ref, ROW_PITCH, ROW_PITCH)
        x = xbuf[:, :D_MODEL].astype(BF16)
        gates = xbuf[:, D_MODEL:]
        h_lo = _silu(_mm(x, wup_s[0])) * _mm(x, wup_s[2]) * gates[:, 0:1]
        h_hi = _silu(_mm(x, wup_s[1])) * _mm(x, wup_s[3]) * gates[:, 1:2]
        ybuf[:, :D_MODEL] = _mm(h_lo.astype(BF16), wdn_s[0]) + _mm(h_hi.astype(BF16), wdn_s[1])
        ybuf[:, D_MODEL:] = jnp.zeros((TM, LANES), F32)
        _to_row_linear(ys_ref, ybuf, ROW_PITCH)


def _experts(layer, tix, lo, hi, valid, fresh, xs, wg, wu, wd):
    n_tiles = xs.shape[0] // (TM * ROW_PITCH)
    row = lambda i, tix, lo, hi, valid, fresh: (tix[i], 0)
    e_lo = lambda i, tix, lo, hi, valid, fresh: (layer, lo[i], 0, 0)
    e_hi = lambda i, tix, lo, hi, valid, fresh: (layer, hi[i], 0, 0)
    up = pl.BlockSpec((1, 1, D_MODEL, D_EXPERT), e_lo), pl.BlockSpec((1, 1, D_MODEL, D_EXPERT), e_hi)
    down = pl.BlockSpec((1, 1, D_EXPERT, D_MODEL), e_lo), pl.BlockSpec((1, 1, D_EXPERT, D_MODEL), e_hi)
    return pl.pallas_call(
        _experts_kernel,
        grid_spec=pltpu.PrefetchScalarGridSpec(
            num_scalar_prefetch=5, grid=(n_tiles,),
            in_specs=[pl.BlockSpec((TM * ROW_PITCH, LANES), row), *up, *up, *down],
            out_specs=pl.BlockSpec((TM * ROW_PITCH, LANES), row),
            scratch_shapes=[pltpu.VMEM((4, D_MODEL, D_EXPERT), BF16), pltpu.VMEM((2, D_EXPERT, D_MODEL), BF16),
                            pltpu.VMEM((TM, PAY_WIDTH), F32), pltpu.VMEM((TM, PAY_WIDTH), F32)]),
        out_shape=jax.ShapeDtypeStruct(xs.shape, F32),
        compiler_params=_params("arbitrary"),
        name="experts",
    )(tix, lo, hi, valid, fresh, xs, wg, wg, wu, wu, wd, wd)


def _ple_kernel(n_prompt_tiles, yp_ref, ys_ref, mo_ref, pp_ref, ps_ref, g_ref, wg_ref, wp_ref, op_ref, os_ref, mbuf):
    i = pl.program_id(0)
    _from_row_linear(mbuf, mo_ref, D_MODEL // LANES, D_MODEL // LANES)
    y2 = _pick(i, n_prompt_tiles, yp_ref, ys_ref) + mbuf[...]
    hn = _rms(y2, g_ref[...]).astype(BF16)
    gt = _sigmoid(_mm(hn, wg_ref[...]))
    pr = _mm(jnp.where(i < n_prompt_tiles, pp_ref[0], ps_ref[...]).astype(BF16), wp_ref[...])
    y3 = y2 + gt * pr

    @pl.when(i < n_prompt_tiles)
    def _():
        op_ref[...] = y3

    @pl.when(i >= n_prompt_tiles)
    def _():
        os_ref[...] = y3


def _ple(layer, y_p, y_s, mo, p_p, p_s, g, wg, wp):
    t = y_s.shape[0]
    npt = y_p.shape[0] // t
    pidx = lambda i: (jnp.minimum(i, npt - 1), 0)
    return pl.pallas_call(
        functools.partial(_ple_kernel, npt),
        grid=(npt + 1,),
        in_specs=[pl.BlockSpec((t, D_MODEL), pidx), _full(y_s.shape),
                  pl.BlockSpec((t * (D_MODEL // LANES), LANES), lambda i: (i, 0)),
                  pl.BlockSpec((1, t, PLE_DIM), lambda i: (layer, jnp.minimum(i, npt - 1), 0)), _full(p_s.shape),
                  _full(g.shape), _full(wg.shape), _full(wp.shape)],
        out_specs=[pl.BlockSpec((t, D_MODEL), pidx), _full(y_s.shape)],
        out_shape=[jax.ShapeDtypeStruct(y_p.shape, F32), jax.ShapeDtypeStruct(y_s.shape, F32)],
        scratch_shapes=[pltpu.VMEM((t, D_MODEL), F32)],
        compiler_params=_params("arbitrary"),
        name="ple",
    )(y_p, y_s, mo, p_p, p_s, g, wg, wp)


def _bucket_experts():
    pairs = [(a, b) for a in range(EXPERTS_PER_GROUP) for b in range(a + 1, EXPERTS_PER_GROUP)]
    lo = [g * EXPERTS_PER_GROUP + a for g in range(N_EXPERT_GROUPS) for a, _ in pairs]
    hi = [g * EXPERTS_PER_GROUP + b for g in range(N_EXPERT_GROUPS) for _, b in pairs]
    return jnp.array(lo, jnp.int32), jnp.array(hi, jnp.int32)


def _moe_ple(layer, y_p, y_s, gffn, wr, br, wg, wu, wd, p_p, p_s, gple, pwg, pwp):
    t = y_s.shape[0]
    n = y_p.shape[0] + t
    tri = jnp.asarray(np.tri(t, t, -1, dtype=np.float32), BF16)
    payload, meta, counts = _route_call(y_p, y_s, gffn, wr, br, tri)

    bucket = meta[:, 0].astype(jnp.int32)
    rank = meta[:, 1].astype(jnp.int32)
    cnt = counts[0, :N_BUCKETS].astype(jnp.int32)
    padded = (cnt + TM - 1) // TM * TM
    ends = jnp.cumsum(padded)
    pos = jnp.take(ends - padded, bucket) + rank
    n_rows = n + N_BUCKETS * TM
    n_tiles = n_rows // TM
    tile_start = jnp.arange(n_tiles, dtype=jnp.int32) * TM
    valid = tile_start < ends[-1]
    used = (ends[-1] // TM).astype(jnp.int32)
    tix = jnp.arange(n_tiles, dtype=jnp.int32)
    tile_bucket = jnp.sum((ends[None, :] <= (jnp.minimum(tix, used - 1) * TM)[:, None]).astype(jnp.int32), axis=1)
    lo_tab, hi_tab = _bucket_experts()
    lo = jnp.take(lo_tab, tile_bucket)
    hi = jnp.take(hi_tab, tile_bucket)
    prev_bucket = jnp.concatenate([jnp.full((1,), -1, jnp.int32), tile_bucket[:-1]])
    fresh = valid & (tile_bucket != prev_bucket)

    xs = _permute(pos, (ends - TM).astype(jnp.int32), (cnt > 0).astype(jnp.int32), used.reshape(1), payload, n_rows)
    ys = _experts(layer, tix, lo, hi, valid.astype(jnp.int32), fresh.astype(jnp.int32), xs, wg, wu, wd)
    mo = _unpermute(pos, ys, n)
    return _ple(layer, y_p, y_s, mo, p_p, p_s, gple, pwg, pwp)


def _split_bf16(a):
    hi = a.astype(BF16)
    lo = (a - hi.astype(F32)).astype(BF16)
    return hi, lo


def _head_norm(a, ind_ref, indt_ref, gain):
    hi, lo = _split_bf16(a * a)
    ss = _mm(hi, ind_ref[...]) + _mm(lo, ind_ref[...])
    inv = lax.rsqrt(ss * (1.0 / HEAD_DIM) + EPS)
    ihi, ilo = _split_bf16(inv)
    invb = _mm(ihi, indt_ref[...]) + _mm(ilo, indt_ref[...])
    return a * invb * gain


def _qkv_kernel(x_ref, g_ref, w_ref, iq_ref, iqt_ref, ik_ref, ikt_ref, qg_ref, kg_ref, q_ref, k_ref, v_ref):
    h = _rms(x_ref[...], g_ref[...]).astype(BF16)
    qkv = _mm(h, w_ref[...])
    nq = N_HEADS * HEAD_DIM
    q = _head_norm(qkv[:, :nq], iq_ref, iqt_ref, qg_ref[...])
    k = _head_norm(qkv[:, nq:nq + KV_DIM], ik_ref, ikt_ref, kg_ref[...])
    q_ref[...] = (q * (HEAD_DIM ** -0.5)).astype(BF16)
    k_ref[...] = k
    v_ref[...] = qkv[:, nq + KV_DIM:]


def _qkv(x, g, w, iq, iqt, ik, ikt, qg, kg):
    n = x.shape[0]
    t = _row_tile(n)
    row = lambda i: (i, 0)
    return pl.pallas_call(
        _qkv_kernel,
        grid=(n // t,),
        in_specs=[pl.BlockSpec((t, D_MODEL), row)] + [_full(a.shape) for a in (g, w, iq, iqt, ik, ikt, qg, kg)],
        out_specs=[pl.BlockSpec((t, N_HEADS * HEAD_DIM), row), pl.BlockSpec((t, KV_DIM), row),
                   pl.BlockSpec((t, KV_DIM), row)],
        out_shape=[jax.ShapeDtypeStruct((n, N_HEADS * HEAD_DIM), BF16), jax.ShapeDtypeStruct((n, KV_DIM), F32),
                   jax.ShapeDtypeStruct((n, KV_DIM), F32)],
        compiler_params=_params("arbitrary"),
        name="qkv",
    )(x, g, w, iq, iqt, ik, ikt, qg, kg)


def _dup_heads(a):
    out = []
    for s in range(KV_DIM // LANES):
        sl = a[:, s * LANES:(s + 1) * LANES]
        sw = pltpu.roll(sl, HEAD_DIM, axis=1)
        low = lax.broadcasted_iota(jnp.int32, sl.shape, 1) < HEAD_DIM
        out.append(jnp.where(low, sl, sw))
        out.append(jnp.where(low, sw, sl))
    return jnp.concatenate(out, axis=1).astype(BF16)


def _attend(q_rows, k2, v2, bias_of, sink_of, extra_mask):
    m_rows = q_rows.shape[0]
    low_q = lax.broadcasted_iota(jnp.int32, (m_rows, LANES), 1) < HEAD_DIM
    low_k = lax.broadcasted_iota(jnp.int32, (k2.shape[0], LANES), 1) < HEAD_DIM
    zero_q = jnp.zeros((m_rows, LANES), BF16)
    zero_k = jnp.zeros((k2.shape[0], LANES), BF16)
    slabs = []
    for g in range(N_KV_HEADS):
        kg = k2[:, g * LANES:(g + 1) * LANES]
        vg = v2[:, g * LANES:(g + 1) * LANES]
        v_lo = jnp.where(low_k, vg, zero_k)
        v_hi = jnp.where(low_k, zero_k, vg)
        lhs = []
        for a in range(GROUP):
            h = g * GROUP + a
            qs = q_rows[:, (h // 2) * LANES:(h // 2 + 1) * LANES]
            lhs.append(jnp.where(low_q, qs, zero_q) if h % 2 == 0 else jnp.where(low_q, zero_q, qs))
        s = lax.dot_general(jnp.concatenate(lhs, axis=0), kg, (((1,), (1,)), ((), ())), preferred_element_type=F32)
        probs, rinv = [], []
        for a in range(GROUP):
            h = g * GROUP + a
            sa = s[a * m_rows:(a + 1) * m_rows] + bias_of(h)
            if extra_mask is not None:
                sa = jnp.where(extra_mask, NEG_INF, sa)
            sink = sink_of(h)
            m = jnp.maximum(jnp.max(sa, axis=1, keepdims=True), sink)
            p = jnp.exp(sa - m)
            den = jnp.sum(p, axis=1, keepdims=True) + jnp.exp(sink - m)
            probs.append(p.astype(BF16))
            rinv.append(1.0 / den)
        for sp in range(GROUP // 2):
            o = _mm(probs[2 * sp], v_lo) + _mm(probs[2 * sp + 1], v_hi)
            slabs.append(o * jnp.where(low_q, rinv[2 * sp], rinv[2 * sp + 1]))
    return jnp.concatenate(slabs, axis=1)


def _attn_p_kernel(sink_ref, q_ref, k_ref, v_ref, x_ref, bias_ref, wo_ref, y_ref, kbuf, vbuf, obuf):
    t = q_ref.shape[0]
    i = pl.program_id(0)

    @pl.when(i == 0)
    def _():
        kbuf[0:WINDOW, :] = jnp.zeros((WINDOW, 2 * KV_DIM), BF16)
        vbuf[0:WINDOW, :] = jnp.zeros((WINDOW, 2 * KV_DIM), BF16)

    @pl.when(i > 0)
    def _():
        kbuf[0:WINDOW, :] = kbuf[t:t + WINDOW, :]
        vbuf[0:WINDOW, :] = vbuf[t:t + WINDOW, :]

    kbuf[WINDOW:, :] = _dup_heads(k_ref[...])
    vbuf[WINDOW:, :] = _dup_heads(v_ref[...])
    col = lax.broadcasted_iota(jnp.int32, (WINDOW, 2 * WINDOW), 1)
    for j in range(t // WINDOW):
        rows = slice(j * WINDOW, (j + 1) * WINDOW)
        keys = slice(j * WINDOW, (j + 2) * WINDOW)
        extra = jnp.logical_and(i == 0, col < WINDOW) if j == 0 else None
        o = _attend(q_ref[rows, :], kbuf[keys, :], vbuf[keys, :], lambda h: bias_ref[h], lambda h: sink_ref[h], extra)
        obuf[rows, :] = o.astype(BF16)
    y_ref[...] = x_ref[...] + _mm(obuf[...], wo_ref[...])


def _attn_p(sinks, q, k, v, x, bias, wo):
    n = q.shape[0]
    t = _row_tile(n)
    row = lambda i: (i, 0)
    return pl.pallas_call(
        _attn_p_kernel,
        grid=(n // t,),
        in_specs=[pl.BlockSpec(memory_space=pltpu.SMEM),
                  pl.BlockSpec((t, N_HEADS * HEAD_DIM), row), pl.BlockSpec((t, KV_DIM), row),
                  pl.BlockSpec((t, KV_DIM), row), pl.BlockSpec((t, D_MODEL), row),
                  _full(bias.shape), _full(wo.shape)],
        out_specs=pl.BlockSpec((t, D_MODEL), row),
        out_shape=jax.ShapeDtypeStruct((n, D_MODEL), F32),
        scratch_shapes=[pltpu.VMEM((t + WINDOW, 2 * KV_DIM), BF16), pltpu.VMEM((t + WINDOW, 2 * KV_DIM), BF16),
                        pltpu.VMEM((t, N_HEADS * HEAD_DIM), BF16)],
        compiler_params=_params("arbitrary"),
        name="attn_prompt",
    )(sinks, q, k, v, x, bias, wo)


def _attn_s_kernel(sink_ref, q_ref, k_ref, v_ref, ck_ref, cv_ref, bias_ref, o_ref):
    pad = bias_ref.shape[2] - ck_ref.shape[0] - k_ref.shape[0]
    zpad = jnp.zeros((pad, KV_DIM), F32)
    k2 = _dup_heads(jnp.concatenate([ck_ref[...], k_ref[...], zpad], axis=0))
    v2 = _dup_heads(jnp.concatenate([cv_ref[...], v_ref[...], zpad], axis=0))
    o = _attend(q_ref[...], k2, v2, lambda h: bias_ref[h], lambda h: sink_ref[h], None)
    o_ref[...] = o.astype(BF16)


def _attn_s(sinks, q, k, v, ck, cv, bias, t_new):
    n = q.shape[0]
    rows = SAMPLE_SEQS * t_new
    crow = SAMPLE_SEQS * WINDOW
    row = lambda i: (i, 0)
    return pl.pallas_call(
        _attn_s_kernel,
        grid=(n // rows,),
        in_specs=[pl.BlockSpec(memory_space=pltpu.SMEM),
                  pl.BlockSpec((rows, N_HEADS * HEAD_DIM), row), pl.BlockSpec((rows, KV_DIM), row),
                  pl.BlockSpec((rows, KV_DIM), row), pl.BlockSpec((crow, KV_DIM), row),
                  pl.BlockSpec((crow, KV_DIM), row), _full(bias.shape)],
        out_specs=pl.BlockSpec((rows, N_HEADS * HEAD_DIM), row),
        out_shape=jax.ShapeDtypeStruct((n, N_HEADS * HEAD_DIM), BF16),
        compiler_params=_params("arbitrary"),
        name="attn_sample",
    )(sinks, q, k, v, ck, cv, bias)


def _proj_res_kernel(o_ref, x_ref, w_ref, y_ref):
    y_ref[...] = x_ref[...] + _mm(o_ref[...], w_ref[...])


def _proj_res(o, x, w):
    n = o.shape[0]
    t = _row_tile(n)
    row = lambda i: (i, 0)
    return pl.pallas_call(
        _proj_res_kernel,
        grid=(n // t,),
        in_specs=[pl.BlockSpec((t, o.shape[1]), row), pl.BlockSpec((t, D_MODEL), row), _full(w.shape)],
        out_specs=pl.BlockSpec((t, D_MODEL), row),
        out_shape=jax.ShapeDtypeStruct((n, D_MODEL), F32),
        compiler_params=_params("arbitrary"),
        name="proj_res",
    )(o, x, w)


def _alibi_slopes():
    return np.exp2(-8.0 * np.arange(1, N_HEADS + 1, dtype=np.float64) / N_HEADS).astype(np.float32)


def _band_bias(dist, allowed):
    b = -(_alibi_slopes()[:, None, None] * dist.astype(np.float32)[None])
    return jnp.asarray(np.where(allowed[None], b, np.float32(NEG_INF)).astype(np.float32))


def _prompt_bias():
    dist = WINDOW + np.arange(WINDOW)[:, None] - np.arange(2 * WINDOW)[None, :]
    return _band_bias(dist, (dist >= 0) & (dist <= WINDOW))


def _sample_bias(t_new, n_cols):
    c = np.arange(n_cols)
    n_cache = SAMPLE_SEQS * WINDOW
    n_new = SAMPLE_SEQS * t_new
    is_cache = c < n_cache
    is_new = (c >= n_cache) & (c < n_cache + n_new)
    seq_c = np.where(is_cache, c // WINDOW, (c - n_cache) // t_new)
    pos_c = np.where(is_cache, c % WINDOW, WINDOW + (c - n_cache) % t_new)
    r = np.arange(n_new)
    seq_r, tok_r = r // t_new, r % t_new
    dist = WINDOW + tok_r[:, None] - pos_c[None, :]
    allowed = (seq_r[:, None] == seq_c[None, :]) & (is_cache | is_new)[None, :] & (dist >= 0) & (dist <= WINDOW)
    return _band_bias(dist, allowed)


def _head_indicator(n_heads):
    ch = np.arange(n_heads * HEAD_DIM) // HEAD_DIM
    ind = (ch[:, None] == np.arange(LANES)[None, :]).astype(np.float32)
    return jnp.asarray(ind, BF16), jnp.asarray(ind.T, BF16)


def kernel(x_prompt, x_sample, state_conv, cache_k, cache_v, p_prompt, p_sample, norm_mix, norm_ffn, norm_ple,
           conv_w_in, conv_b_in, conv_w_dw, conv_b_dw, conv_ln_g, conv_ln_b, conv_w_out, conv_b_out, attn_w_qkv,
           attn_q_norm, attn_k_norm, attn_sinks, attn_w_o, moe_w_rg, moe_b_rg, moe_w_re, moe_b_re, moe_w_gate,
           moe_w_up, moe_w_down, ple_w_gate, ple_w_proj):
    bp, seq, d = x_prompt.shape
    bs, t_new, _ = x_sample.shape
    assert bp == 1 and d == D_MODEL and seq % WINDOW == 0 and bs % SAMPLE_SEQS == 0
    assert seq % (bs * t_new) == 0 and (bs * t_new) % DMA_CHUNK == 0
    depth = norm_mix.shape[0]
    row2 = lambda a: a.reshape(1, -1)

    y_p = x_prompt.reshape(seq, d)
    y_s = x_sample.reshape(bs * t_new, d)
    conv_p, conv_s, k_p, v_p, k_s, v_s = [], [], [], [], [], []

    for i in range(depth):
        j = i // 2
        g_mix = row2(norm_mix[i])
        if i % 2 == 0:
            w_in = conv_w_in[j].astype(BF16)
            b_in = row2(conv_b_in[j])
            tail = (conv_w_dw[j], row2(conv_b_dw[j]), row2(conv_ln_g[j]), row2(conv_ln_b[j]),
                    conv_w_out[j].astype(BF16), row2(conv_b_out[j]))
            u_p = _conv_in(y_p, g_mix, w_in, b_in)
            u_s = _conv_in(y_s, g_mix, w_in, b_in)
            conv_p.append(u_p[seq - CONV_STATE:].reshape(1, CONV_STATE, d))
            ue = jnp.concatenate([state_conv[j], u_s.reshape(bs, t_new, d)], axis=1)
            conv_s.append(ue[:, t_new:])
            y_p = _conv_out_p(u_p, y_p, *tail)
            ys_t = _conv_out_s(ue.transpose(1, 0, 2), y_s.reshape(bs, t_new, d).transpose(1, 0, 2), *tail)
            y_s = ys_t.transpose(1, 0, 2).reshape(bs * t_new, d)
        else:
            w_qkv = attn_w_qkv[j].astype(BF16)
            w_o = attn_w_o[j].astype(BF16)
            iq, iqt = _head_indicator(N_HEADS)
            ik, ikt = _head_indicator(N_KV_HEADS)
            qg = row2(jnp.tile(attn_q_norm[j], N_HEADS))
            kg = row2(jnp.tile(attn_k_norm[j], N_KV_HEADS))
            sinks = attn_sinks[j]
            q1, k1, v1 = _qkv(y_p, g_mix, w_qkv, iq, iqt, ik, ikt, qg, kg)
            q2, k2, v2 = _qkv(y_s, g_mix, w_qkv, iq, iqt, ik, ikt, qg, kg)
            k_p.append(k1[seq - WINDOW:].reshape(1, WINDOW, N_KV_HEADS, HEAD_DIM))
            v_p.append(v1[seq - WINDOW:].reshape(1, WINDOW, N_KV_HEADS, HEAD_DIM))
            shp = (bs, t_new, N_KV_HEADS, HEAD_DIM)
            k_s.append(jnp.concatenate([cache_k[j][:, t_new:], k2.reshape(shp)], axis=1))
            v_s.append(jnp.concatenate([cache_v[j][:, t_new:], v2.reshape(shp)], axis=1))
            y_p = _attn_p(sinks, q1, k1, v1, y_p, _prompt_bias(), w_o)
            n_cols = -(-(SAMPLE_SEQS * (WINDOW + t_new)) // LANES) * LANES
            o_s = _attn_s(sinks, q2, k2, v2, cache_k[j].reshape(bs * WINDOW, KV_DIM),
                          cache_v[j].reshape(bs * WINDOW, KV_DIM), _sample_bias(t_new, n_cols), t_new)
            y_s = _proj_res(o_s, y_s, w_o)

        w_r = jnp.zeros((d, ROUTER_LANES), F32)
        w_r = w_r.at[:, :N_EXPERT_GROUPS].set(moe_w_rg[i]).at[:, EXPERT_LANE0:EXPERT_LANE0 + N_EXPERTS].set(moe_w_re[i])
        b_r = jnp.zeros((1, ROUTER_LANES), F32)
        b_r = b_r.at[0, :N_EXPERT_GROUPS].set(moe_b_rg[i]).at[0, EXPERT_LANE0:EXPERT_LANE0 + N_EXPERTS].set(moe_b_re[i])
        moe = (row2(norm_ffn[i]), jnp.stack(_split_bf16(w_r)), b_r, moe_w_gate, moe_w_up, moe_w_down)
        ple = (row2(norm_ple[i]), ple_w_gate[i].astype(BF16), ple_w_proj[i].astype(BF16))
        y_p, y_s = _moe_ple(i, y_p, y_s, *moe, p_prompt.reshape(depth, seq, PLE_DIM),
                            p_sample[i].reshape(bs * t_new, PLE_DIM), *ple)

    return (y_p.reshape(1, seq, d), y_s.reshape(bs, t_new, d), jnp.stack(conv_p), jnp.stack(conv_s),
            jnp.stack(k_p), jnp.stack(v_p), jnp.stack(k_s), jnp.stack(v_s))
```

```python
import functools

import numpy as np

import jax
import jax.numpy as jnp
from jax import lax
from jax.experimental import pallas as pl
from jax.experimental.pallas import tpu as pltpu

F32 = jnp.float32
BF16 = jnp.bfloat16

D_MODEL = 1024
PLE_DIM = 256
CONV_WIDTH = 31
CONV_STATE = CONV_WIDTH - 1
N_HEADS = 16
N_KV_HEADS = 4
HEAD_DIM = 64
GROUP = N_HEADS // N_KV_HEADS
WINDOW = 128
KV_DIM = N_KV_HEADS * HEAD_DIM
N_EXPERT_GROUPS = 4
EXPERTS_PER_GROUP = 4
N_EXPERTS = 16
D_EXPERT = 256
EPS = 1e-6
NEG_INF = -1e30

LANES = 128
ROUTER_LANES = LANES
EXPERT_LANE0 = N_EXPERT_GROUPS
HALO = 32
CONV_ROWS = 32
NORM_ROWS = 128
CONV_PITCH = D_MODEL // LANES + 1
SAMPLE_SEQS = 8
PAIRS_PER_GROUP = EXPERTS_PER_GROUP * (EXPERTS_PER_GROUP - 1) // 2
N_BUCKETS = N_EXPERT_GROUPS * PAIRS_PER_GROUP
TM = 256
PAY_WIDTH = D_MODEL + LANES
ROW_PITCH = PAY_WIDTH // LANES
DMA_CHUNK = 512
RING = 3
VMEM_LIMIT = 48 * 1024 * 1024


def _row_tile(n):
    return 512 if n % 512 == 0 else n


def _params(*sem):
    return pltpu.CompilerParams(dimension_semantics=sem, vmem_limit_bytes=VMEM_LIMIT)


def _full(shape):
    nd = len(shape)
    return pl.BlockSpec(shape, lambda *_: (0,) * nd)


def _rms(x, g):
    ms = jnp.mean(x * x, axis=-1, keepdims=True)
    return x * lax.rsqrt(ms + EPS) * g


def _sigmoid(x):
    return 1.0 / (1.0 + jnp.exp(-x))


def _mm(a, b):
    return jnp.dot(a, b, preferred_element_type=F32)


def _conv_in_kernel(x_ref, g_ref, w_ref, b_ref, u_ref):
    h = _rms(x_ref[...], g_ref[...]).astype(BF16)
    z = _mm(h, w_ref[...]) + b_ref[...]
    u_ref[...] = z[:, :D_MODEL] * _sigmoid(z[:, D_MODEL:])


def _conv_in(x, g, w, b):
    n = x.shape[0]
    t = _row_tile(n)
    return pl.pallas_call(
        _conv_in_kernel,
        grid=(n // t,),
        in_specs=[pl.BlockSpec((t, D_MODEL), lambda i: (i, 0)), _full(g.shape), _full(w.shape), _full(b.shape)],
        out_specs=pl.BlockSpec((t, D_MODEL), lambda i: (i, 0)),
        out_shape=jax.ShapeDtypeStruct((n, D_MODEL), F32),
        compiler_params=_params("arbitrary"),
        name="conv_in",
    )(x, g, w, b)


def _ln_silu(c, g, b):
    mu = jnp.mean(c, axis=-1, keepdims=True)
    xc = c - mu
    var = jnp.mean(xc * xc, axis=-1, keepdims=True)
    cn = xc * lax.rsqrt(var + EPS) * g + b
    return cn * _sigmoid(cn)


def _conv_out_p_kernel(u_ref, halo_ref, x_ref, wdw_ref, bdw_ref, lng_ref, lnb_ref, wout_ref, bout_ref, y_ref,
                       ubuf, cbuf, hbuf):
    t = u_ref.shape[0]
    i = pl.program_id(0)
    nj = D_MODEL // LANES

    def put(r, j, val):
        ubuf[pl.ds(r * CONV_PITCH + j, 8, stride=CONV_PITCH), :] = val

    for r in range(0, HALO, 8):
        for j in range(nj):
            put(r, j, jnp.where(i > 0, halo_ref[r:r + 8, j * LANES:(j + 1) * LANES], 0.0))

    def fill(rr, carry):
        r = pl.multiple_of(rr * 8, 8)
        for j in range(nj):
            put(r + HALO, j, u_ref[pl.ds(r, 8), j * LANES:(j + 1) * LANES])
        return carry

    lax.fori_loop(0, t // 8, fill, 0)

    def conv_chunk(rr, carry):
        r0 = pl.multiple_of(rr * CONV_ROWS, CONV_ROWS)
        for j in range(nj):
            lanes = slice(j * LANES, (j + 1) * LANES)
            accs = [None] * (CONV_ROWS // 8)
            for k in range(CONV_WIDTH):
                wk = wdw_ref[k:k + 1, lanes]
                for q in range(CONV_ROWS // 8):
                    r = r0 + (HALO - CONV_STATE + k + 8 * q)
                    term = wk * ubuf[pl.ds(r * CONV_PITCH + j, 8, stride=CONV_PITCH), :]
                    accs[q] = term if k == 0 else accs[q] + term
            for q in range(CONV_ROWS // 8):
                cbuf[pl.ds(r0 + 8 * q, 8), lanes] = accs[q]
        return carry

    lax.fori_loop(0, t // CONV_ROWS, conv_chunk, 0)

    def norm_chunk(rr, carry):
        r0 = pl.multiple_of(rr * NORM_ROWS, NORM_ROWS)
        c = cbuf[pl.ds(r0, NORM_ROWS), :] + bdw_ref[...]
        hbuf[pl.ds(r0, NORM_ROWS), :] = _ln_silu(c, lng_ref[...], lnb_ref[...]).astype(BF16)
        return carry

    lax.fori_loop(0, t // NORM_ROWS, norm_chunk, 0)
    y_ref[...] = x_ref[...] + _mm(hbuf[...], wout_ref[...]) + bout_ref[...]


def _conv_out_p(u, x, wdw, bdw, lng, lnb, wout, bout):
    n = u.shape[0]
    t = _row_tile(n)
    hb = t // HALO
    row = lambda i: (i, 0)
    return pl.pallas_call(
        _conv_out_p_kernel,
        grid=(n // t,),
        in_specs=[pl.BlockSpec((t, D_MODEL), row),
                  pl.BlockSpec((HALO, D_MODEL), lambda i: (jnp.maximum(i * hb - 1, 0), 0)),
                  pl.BlockSpec((t, D_MODEL), row),
                  _full(wdw.shape), _full(bdw.shape), _full(lng.shape), _full(lnb.shape),
                  _full(wout.shape), _full(bout.shape)],
        out_specs=pl.BlockSpec((t, D_MODEL), row),
        out_shape=jax.ShapeDtypeStruct((n, D_MODEL), F32),
        scratch_shapes=[pltpu.VMEM(((t + HALO) * CONV_PITCH, LANES), F32), pltpu.VMEM((t, D_MODEL), F32),
                        pltpu.VMEM((t, D_MODEL), BF16)],
        compiler_params=_params("arbitrary"),
        name="conv_out_prompt",
    )(u, u, x, wdw, bdw, lng, lnb, wout, bout)


def _conv_out_s_kernel(ue_ref, x_ref, wdw_ref, bdw_ref, lng_ref, lnb_ref, wout_ref, bout_ref, y_ref):
    nt, bb, _ = x_ref.shape
    for t in range(nt):
        acc = jnp.broadcast_to(bdw_ref[...], (bb, D_MODEL))
        for k in range(CONV_WIDTH):
            acc = acc + wdw_ref[k:k + 1, :] * ue_ref[t + k]
        h = _ln_silu(acc, lng_ref[...], lnb_ref[...]).astype(BF16)
        y_ref[t] = x_ref[t] + _mm(h, wout_ref[...]) + bout_ref[...]


def _conv_out_s(ue, x, wdw, bdw, lng, lnb, wout, bout):
    nt, b, _ = x.shape
    bb = 32 if b % 32 == 0 else b
    return pl.pallas_call(
        _conv_out_s_kernel,
        grid=(b // bb,),
        in_specs=[pl.BlockSpec((ue.shape[0], bb, D_MODEL), lambda i: (0, i, 0)),
                  pl.BlockSpec((nt, bb, D_MODEL), lambda i: (0, i, 0)),
                  _full(wdw.shape), _full(bdw.shape), _full(lng.shape), _full(lnb.shape),
                  _full(wout.shape), _full(bout.shape)],
        out_specs=pl.BlockSpec((nt, bb, D_MODEL), lambda i: (0, i, 0)),
        out_shape=jax.ShapeDtypeStruct((nt, b, D_MODEL), F32),
        compiler_params=_params("arbitrary"),
        name="conv_out_sample",
    )(ue, x, wdw, bdw, lng, lnb, wout, bout)


def _route(lg):
    big = 3.0e38
    lane = lax.broadcasted_iota(jnp.int32, lg.shape, 1)
    lanef = lane.astype(F32)
    is_g = lane < N_EXPERT_GROUPS
    gl = jnp.where(is_g, lg, -big)
    gmax = jnp.max(gl, axis=1, keepdims=True)
    gsum = jnp.sum(jnp.where(is_g, jnp.exp(gl - gmax), 0.0), axis=1, keepdims=True)
    g_w = 1.0 / gsum
    g_idx = jnp.min(jnp.where(gl == gmax, lanef, big), axis=1, keepdims=True)
    rel = lanef - float(EXPERT_LANE0) - g_idx * float(EXPERTS_PER_GROUP)
    in_grp = jnp.where(rel >= 0.0, jnp.where(rel < float(EXPERTS_PER_GROUP), 1.0, 0.0), 0.0) > 0.5
    el = jnp.where(in_grp, lg, -big)
    e1 = jnp.max(el, axis=1, keepdims=True)
    i1 = jnp.min(jnp.where(el == e1, lanef, big), axis=1, keepdims=True)
    el2 = jnp.where(lanef == i1, -big, el)
    e2 = jnp.max(el2, axis=1, keepdims=True)
    i2 = jnp.min(jnp.where(el2 == e2, lanef, big), axis=1, keepdims=True)
    tt = jnp.exp(e2 - e1)
    w1 = g_w / (1.0 + tt)
    w2 = g_w * tt / (1.0 + tt)
    base = float(EXPERT_LANE0) + g_idx * float(EXPERTS_PER_GROUP)
    a = jnp.minimum(i1, i2) - base
    b = jnp.maximum(i1, i2) - base
    pair = a * (7.0 - a) * 0.5 + (b - a - 1.0)
    first_is_lo = i1 < i2
    return (g_idx * float(PAIRS_PER_GROUP) + pair, jnp.where(first_is_lo, w1, w2), jnp.where(first_is_lo, w2, w1))


def _to_row_linear(dst_ref, src_ref, n_tiles):
    def body(g, carry):
        r = pl.multiple_of(g * 8, 8)
        for j in range(n_tiles):
            dst_ref[pl.ds(r * ROW_PITCH + j, 8, stride=ROW_PITCH), :] = src_ref[pl.ds(r, 8), j * LANES:(j + 1) * LANES]
        return carry

    lax.fori_loop(0, src_ref.shape[0] // 8, body, 0)


def _from_row_linear(dst_ref, src_ref, n_tiles, pitch):
    def body(g, carry):
        r = pl.multiple_of(g * 8, 8)
        for j in range(n_tiles):
            dst_ref[pl.ds(r, 8), j * LANES:(j + 1) * LANES] = src_ref[pl.ds(r * pitch + j, 8, stride=pitch), :]
        return carry

    lax.fori_loop(0, dst_ref.shape[0] // 8, body, 0)


def _pick(i, n_prompt_tiles, prompt_ref, sample_ref):
    return jnp.where(i < n_prompt_tiles, prompt_ref[...], sample_ref[...])


def _route_kernel(n_prompt_tiles, yp_ref, ys_ref, g_ref, wr_ref, br_ref, tri_ref, pay_ref, meta_ref, cnt_ref, carry,
                  rowbuf):
    i = pl.program_id(0)

    @pl.when(i == 0)
    def _():
        carry[...] = jnp.zeros_like(carry)

    xf = _rms(_pick(i, n_prompt_tiles, yp_ref, ys_ref), g_ref[...])
    x_hi, x_lo = _split_bf16(xf)
    logits = _mm(x_hi, wr_ref[0]) + (_mm(x_lo, wr_ref[0]) + _mm(x_hi, wr_ref[1])) + br_ref[...]
    bucket, w_lo, w_hi = _route(logits)
    lane = lax.broadcasted_iota(jnp.int32, logits.shape, 1)
    onehot = jnp.where(lane.astype(F32) == bucket, 1.0, 0.0)
    before = _mm(tri_ref[...], onehot.astype(BF16)) + carry[...]
    rank = jnp.sum(onehot * before, axis=1, keepdims=True)
    carry[...] += jnp.sum(onehot, axis=0, keepdims=True)
    cnt_ref[...] = carry[...]
    meta_ref[...] = jnp.where(lane == 0, bucket, jnp.where(lane == 1, rank, 0.0))

    rowbuf[:, :D_MODEL] = xf
    rowbuf[:, D_MODEL:] = jnp.where(lane == 0, w_lo, jnp.where(lane == 1, w_hi, 0.0))
    _to_row_linear(pay_ref, rowbuf, ROW_PITCH)


def _route_call(y_p, y_s, g, wr, br, tri):
    t = y_s.shape[0]
    npt = y_p.shape[0] // t
    n = y_p.shape[0] + t
    pidx = lambda i: (jnp.minimum(i, npt - 1), 0)
    return pl.pallas_call(
        functools.partial(_route_kernel, npt),
        grid=(npt + 1,),
        in_specs=[pl.BlockSpec((t, D_MODEL), pidx), _full(y_s.shape), _full(g.shape), _full(wr.shape),
                  _full(br.shape), _full(tri.shape)],
        out_specs=[pl.BlockSpec((t * ROW_PITCH, LANES), lambda i: (i, 0)),
                   pl.BlockSpec((t, ROUTER_LANES), lambda i: (i, 0)), _full((1, ROUTER_LANES))],
        out_shape=[jax.ShapeDtypeStruct((n * ROW_PITCH, LANES), F32),
                   jax.ShapeDtypeStruct((n, ROUTER_LANES), F32), jax.ShapeDtypeStruct((1, ROUTER_LANES), F32)],
        scratch_shapes=[pltpu.VMEM((1, ROUTER_LANES), F32), pltpu.VMEM((t, PAY_WIDTH), F32)],
        compiler_params=_params("arbitrary"),
        name="route",
    )(y_p, y_s, g, wr, br, tri)


def _permute_kernel(pos_ref, zstart_ref, nz_ref, used_ref, src_ref, dst_ref, zbuf, ring, zsem, in_sems, out_sems):
    zbuf[...] = jnp.zeros_like(zbuf)

    def zero_tile(start):
        return pltpu.make_async_copy(zbuf, dst_ref.at[pl.ds(pl.multiple_of(start * ROW_PITCH, 8), TM * ROW_PITCH)], zsem)

    for b in range(N_BUCKETS):
        @pl.when(nz_ref[b] > 0)
        def _():
            zero_tile(zstart_ref[b]).start()
    n_tiles = dst_ref.shape[0] // (TM * ROW_PITCH)

    def start_unused(i, carry):
        zero_tile(i * TM).start()
        return carry

    lax.fori_loop(used_ref[0], n_tiles, start_unused, 0)
    for b in range(N_BUCKETS):
        @pl.when(nz_ref[b] > 0)
        def _():
            zero_tile(0).wait()

    def wait_unused(i, carry):
        zero_tile(0).wait()
        return carry

    lax.fori_loop(used_ref[0], n_tiles, wait_unused, 0)

    chunk_rows = DMA_CHUNK * ROW_PITCH
    n_chunks = src_ref.shape[0] // chunk_rows

    def fetch(c, slot):
        return pltpu.make_async_copy(src_ref.at[pl.ds(pl.multiple_of(c * chunk_rows, 8), chunk_rows)], ring.at[slot],
                                     in_sems.at[slot])

    def drain(slot):
        pltpu.make_async_copy(ring.at[slot], dst_ref.at[pl.ds(0, chunk_rows)], out_sems.at[slot]).wait()

    fetch(0, 0).start()

    def step(c, carry):
        slot = lax.rem(c, RING)
        nxt = lax.rem(c + 1, RING)
        fetch(c, slot).wait()

        @pl.when(c >= RING - 1)
        def _():
            drain(nxt)

        @pl.when(c + 1 < n_chunks)
        def _():
            fetch(c + 1, nxt).start()

        def send(jj, carry2):
            row = pos_ref[c * DMA_CHUNK + jj] * ROW_PITCH
            pltpu.make_async_copy(ring.at[slot, pl.ds(jj * ROW_PITCH, ROW_PITCH)], dst_ref.at[pl.ds(row, ROW_PITCH)],
                                  out_sems.at[slot]).start()
            return carry2

        lax.fori_loop(0, DMA_CHUNK, send, 0, unroll=8)
        return carry

    lax.fori_loop(0, n_chunks, step, 0)
    for c in range(n_chunks - (RING - 1), n_chunks):
        drain(c % RING)


def _permute(pos, zstart, nz, used, payload, n_rows_out):
    assert payload.shape[0] // (DMA_CHUNK * ROW_PITCH) >= RING
    return pl.pallas_call(
        _permute_kernel,
        grid_spec=pltpu.PrefetchScalarGridSpec(
            num_scalar_prefetch=4, grid=(1,),
            in_specs=[pl.BlockSpec(memory_space=pl.ANY)],
            out_specs=pl.BlockSpec(memory_space=pl.ANY),
            scratch_shapes=[pltpu.VMEM((TM * ROW_PITCH, LANES), payload.dtype),
                            pltpu.VMEM((RING, DMA_CHUNK * ROW_PITCH, LANES), payload.dtype),
                            pltpu.SemaphoreType.DMA(()), pltpu.SemaphoreType.DMA((RING,)),
                            pltpu.SemaphoreType.DMA((RING,))]),
        out_shape=jax.ShapeDtypeStruct((n_rows_out * ROW_PITCH, LANES), payload.dtype),
        compiler_params=_params("arbitrary"),
        name="permute",
    )(pos, zstart, nz, used, payload)


def _silu(x):
    return x * _sigmoid(x)


def _experts_kernel(tix_ref, lo_ref, hi_ref, valid_ref, fresh_ref, xs_ref, wgl_ref, wgh_ref, wul_ref, wuh_ref,
                    wdl_ref, wdh_ref, ys_ref, wup_s, wdn_s, xbuf, ybuf):
    i = pl.program_id(0)

    @pl.when(fresh_ref[i] > 0)
    def _():
        for s, ref in enumerate((wgl_ref, wgh_ref, wul_ref, wuh_ref)):
            wup_s[s] = ref[0, 0].astype(BF16)
        for s, ref in enumerate((wdl_ref, wdh_ref)):
            wdn_s[s] = ref[0, 0].astype(BF16)

    @pl.when(valid_ref[i] == 0)
    def _():
        ys_ref[...] = jnp.zeros_like(ys_ref)

    @pl.when(valid_ref[i] > 0)
    def _():
        _from_row_linear(xbuf, xs_ref, ROW_PITCH, ROW_PITCH)
        x = xbuf[:, :D_MODEL].astype(BF16)
        gates = xbuf[:, D_MODEL:]
        h_lo = _silu(_mm(x, wup_s[0])) * _mm(x, wup_s[2]) * gates[:, 0:1]
        h_hi = _silu(_mm(x, wup_s[1])) * _mm(x, wup_s[3]) * gates[:, 1:2]
        ybuf[:, :D_MODEL] = _mm(h_lo.astype(BF16), wdn_s[0]) + _mm(h_hi.astype(BF16), wdn_s[1])
        ybuf[:, D_MODEL:] = jnp.zeros((TM, LANES), F32)
        _to_row_linear(ys_ref, ybuf, ROW_PITCH)


def _experts(layer, tix, lo, hi, valid, fresh, xs, wg, wu, wd):
    n_tiles = xs.shape[0] // (TM * ROW_PITCH)
    row = lambda i, tix, lo, hi, valid, fresh: (tix[i], 0)
    e_lo = lambda i, tix, lo, hi, valid, fresh: (layer, lo[i], 0, 0)
    e_hi = lambda i, tix, lo, hi, valid, fresh: (layer, hi[i], 0, 0)
    up = pl.BlockSpec((1, 1, D_MODEL, D_EXPERT), e_lo), pl.BlockSpec((1, 1, D_MODEL, D_EXPERT), e_hi)
    down = pl.BlockSpec((1, 1, D_EXPERT, D_MODEL), e_lo), pl.BlockSpec((1, 1, D_EXPERT, D_MODEL), e_hi)
    return pl.pallas_call(
        _experts_kernel,
        grid_spec=pltpu.PrefetchScalarGridSpec(
            num_scalar_prefetch=5, grid=(n_tiles,),
            in_specs=[pl.BlockSpec((TM * ROW_PITCH, LANES), row), *up, *up, *down],
            out_specs=pl.BlockSpec((TM * ROW_PITCH, LANES), row),
            scratch_shapes=[pltpu.VMEM((4, D_MODEL, D_EXPERT), BF16), pltpu.VMEM((2, D_EXPERT, D_MODEL), BF16),
                            pltpu.VMEM((TM, PAY_WIDTH), F32), pltpu.VMEM((TM, PAY_WIDTH), F32)]),
        out_shape=jax.ShapeDtypeStruct(xs.shape, F32),
        compiler_params=_params("arbitrary"),
        name="experts",
    )(tix, lo, hi, valid, fresh, xs, wg, wg, wu, wu, wd, wd)


def _ple_kernel(n_prompt_tiles, pos_ref, yp_ref, ys_ref, moe_ref, pp_ref, ps_ref, g_ref, wg_ref, wp_ref, op_ref, os_ref,
                mring, mbuf, sems):
    i = pl.program_id(0)
    t = mbuf.shape[0]
    w = D_MODEL // LANES

    def gather(tile, slot):
        def body(jj, carry):
            row = pos_ref[tile * t + jj] * ROW_PITCH
            pltpu.make_async_copy(moe_ref.at[pl.ds(row, w)], mring.at[slot, pl.ds(jj * w, w)], sems.at[slot]).start()
            return carry

        lax.fori_loop(0, t, body, 0, unroll=8)

    @pl.when(i == 0)
    def _():
        gather(0, 0)

    @pl.when(i < n_prompt_tiles)
    def _():
        gather(i + 1, lax.rem(i + 1, 2))

    slot = lax.rem(i, 2)
    pltpu.make_async_copy(moe_ref.at[pl.ds(0, t * w)], mring.at[slot], sems.at[slot]).wait()
    _from_row_linear(mbuf, mring.at[slot], w, w)
    y2 = _pick(i, n_prompt_tiles, yp_ref, ys_ref) + mbuf[...]
    hn = _rms(y2, g_ref[...]).astype(BF16)
    gt = _sigmoid(_mm(hn, wg_ref[...]))
    pr = _mm(jnp.where(i < n_prompt_tiles, pp_ref[0], ps_ref[...]).astype(BF16), wp_ref[...])
    y3 = y2 + gt * pr

    @pl.when(i < n_prompt_tiles)
    def _():
        op_ref[...] = y3

    @pl.when(i >= n_prompt_tiles)
    def _():
        os_ref[...] = y3


def _ple(layer, pos, y_p, y_s, ys_sorted, p_p, p_s, g, wg, wp):
    t = y_s.shape[0]
    npt = y_p.shape[0] // t
    w = D_MODEL // LANES
    pidx = lambda i, pos: (jnp.minimum(i, npt - 1), 0)
    full = lambda shape: pl.BlockSpec(shape, lambda i, pos: (0,) * len(shape))
    return pl.pallas_call(
        functools.partial(_ple_kernel, npt),
        grid_spec=pltpu.PrefetchScalarGridSpec(
            num_scalar_prefetch=1, grid=(npt + 1,),
            in_specs=[pl.BlockSpec((t, D_MODEL), pidx), full(y_s.shape), pl.BlockSpec(memory_space=pl.ANY),
                      pl.BlockSpec((1, t, PLE_DIM), lambda i, pos: (layer, jnp.minimum(i, npt - 1), 0)),
                      full(p_s.shape), full(g.shape), full(wg.shape), full(wp.shape)],
            out_specs=[pl.BlockSpec((t, D_MODEL), pidx), full(y_s.shape)],
            scratch_shapes=[pltpu.VMEM((2, t * w, LANES), F32), pltpu.VMEM((t, D_MODEL), F32),
                            pltpu.SemaphoreType.DMA((2,))]),
        out_shape=[jax.ShapeDtypeStruct(y_p.shape, F32), jax.ShapeDtypeStruct(y_s.shape, F32)],
        compiler_params=_params("arbitrary"),
        name="ple",
    )(pos, y_p, y_s, ys_sorted, p_p, p_s, g, wg, wp)


def _bucket_experts():
    pairs = [(a, b) for a in range(EXPERTS_PER_GROUP) for b in range(a + 1, EXPERTS_PER_GROUP)]
    lo = [g * EXPERTS_PER_GROUP + a for g in range(N_EXPERT_GROUPS) for a, _ in pairs]
    hi = [g * EXPERTS_PER_GROUP + b for g in range(N_EXPERT_GROUPS) for _, b in pairs]
    return jnp.array(lo, jnp.int32), jnp.array(hi, jnp.int32)


def _moe_ple(layer, y_p, y_s, gffn, wr, br, wg, wu, wd, p_p, p_s, gple, pwg, pwp):
    t = y_s.shape[0]
    n = y_p.shape[0] + t
    tri = jnp.asarray(np.tri(t, t, -1, dtype=np.float32), BF16)
    payload, meta, counts = _route_call(y_p, y_s, gffn, wr, br, tri)

    bucket = meta[:, 0].astype(jnp.int32)
    rank = meta[:, 1].astype(jnp.int32)
    cnt = counts[0, :N_BUCKETS].astype(jnp.int32)
    padded = (cnt + TM - 1) // TM * TM
    ends = jnp.cumsum(padded)
    pos = jnp.take(ends - padded, bucket) + rank
    n_rows = n + N_BUCKETS * TM
    n_tiles = n_rows // TM
    tile_start = jnp.arange(n_tiles, dtype=jnp.int32) * TM
    valid = tile_start < ends[-1]
    used = (ends[-1] // TM).astype(jnp.int32)
    tix = jnp.arange(n_tiles, dtype=jnp.int32)
    tile_bucket = jnp.sum((ends[None, :] <= (jnp.minimum(tix, used - 1) * TM)[:, None]).astype(jnp.int32), axis=1)
    lo_tab, hi_tab = _bucket_experts()
    lo = jnp.take(lo_tab, tile_bucket)
    hi = jnp.take(hi_tab, tile_bucket)
    prev_bucket = jnp.concatenate([jnp.full((1,), -1, jnp.int32), tile_bucket[:-1]])
    fresh = valid & (tile_bucket != prev_bucket)

    xs = _permute(pos, (ends - TM).astype(jnp.int32), (cnt > 0).astype(jnp.int32), used.reshape(1), payload, n_rows)
    ys = _experts(layer, tix, lo, hi, valid.astype(jnp.int32), fresh.astype(jnp.int32), xs, wg, wu, wd)
    return _ple(layer, pos, y_p, y_s, ys, p_p, p_s, gple, pwg, pwp)


def _split_bf16(a):
    hi = a.astype(BF16)
    lo = (a - hi.astype(F32)).astype(BF16)
    return hi, lo


def _head_norm(a, ind_ref, indt_ref, gain):
    hi, lo = _split_bf16(a * a)
    ss = _mm(hi, ind_ref[...]) + _mm(lo, ind_ref[...])
    inv = lax.rsqrt(ss * (1.0 / HEAD_DIM) + EPS)
    ihi, ilo = _split_bf16(inv)
    invb = _mm(ihi, indt_ref[...]) + _mm(ilo, indt_ref[...])
    return a * invb * gain


def _qkv_kernel(x_ref, g_ref, w_ref, iq_ref, iqt_ref, ik_ref, ikt_ref, qg_ref, kg_ref, q_ref, k_ref, v_ref):
    h = _rms(x_ref[...], g_ref[...]).astype(BF16)
    qkv = _mm(h, w_ref[...])
    nq = N_HEADS * HEAD_DIM
    q = _head_norm(qkv[:, :nq], iq_ref, iqt_ref, qg_ref[...])
    k = _head_norm(qkv[:, nq:nq + KV_DIM], ik_ref, ikt_ref, kg_ref[...])
    q_ref[...] = (q * (HEAD_DIM ** -0.5)).astype(BF16)
    k_ref[...] = k
    v_ref[...] = qkv[:, nq + KV_DIM:]


def _qkv(x, g, w, iq, iqt, ik, ikt, qg, kg):
    n = x.shape[0]
    t = _row_tile(n)
    row = lambda i: (i, 0)
    return pl.pallas_call(
        _qkv_kernel,
        grid=(n // t,),
        in_specs=[pl.BlockSpec((t, D_MODEL), row)] + [_full(a.shape) for a in (g, w, iq, iqt, ik, ikt, qg, kg)],
        out_specs=[pl.BlockSpec((t, N_HEADS * HEAD_DIM), row), pl.BlockSpec((t, KV_DIM), row),
                   pl.BlockSpec((t, KV_DIM), row)],
        out_shape=[jax.ShapeDtypeStruct((n, N_HEADS * HEAD_DIM), BF16), jax.ShapeDtypeStruct((n, KV_DIM), F32),
                   jax.ShapeDtypeStruct((n, KV_DIM), F32)],
        compiler_params=_params("arbitrary"),
        name="qkv",
    )(x, g, w, iq, iqt, ik, ikt, qg, kg)


def _dup_heads(a):
    out = []
    for s in range(KV_DIM // LANES):
        sl = a[:, s * LANES:(s + 1) * LANES]
        sw = pltpu.roll(sl, HEAD_DIM, axis=1)
        low = lax.broadcasted_iota(jnp.int32, sl.shape, 1) < HEAD_DIM
        out.append(jnp.where(low, sl, sw))
        out.append(jnp.where(low, sw, sl))
    return jnp.concatenate(out, axis=1).astype(BF16)


def _attend(q_rows, k2, v2, bias_of, sink_of, extra_mask):
    m_rows = q_rows.shape[0]
    low_q = lax.broadcasted_iota(jnp.int32, (m_rows, LANES), 1) < HEAD_DIM
    low_k = lax.broadcasted_iota(jnp.int32, (k2.shape[0], LANES), 1) < HEAD_DIM
    zero_q = jnp.zeros((m_rows, LANES), BF16)
    zero_k = jnp.zeros((k2.shape[0], LANES), BF16)
    slabs = []
    for g in range(N_KV_HEADS):
        kg = k2[:, g * LANES:(g + 1) * LANES]
        vg = v2[:, g * LANES:(g + 1) * LANES]
        v_lo = jnp.where(low_k, vg, zero_k)
        v_hi = jnp.where(low_k, zero_k, vg)
        lhs = []
        for a in range(GROUP):
            h = g * GROUP + a
            qs = q_rows[:, (h // 2) * LANES:(h // 2 + 1) * LANES]
            lhs.append(jnp.where(low_q, qs, zero_q) if h % 2 == 0 else jnp.where(low_q, zero_q, qs))
        s = lax.dot_general(jnp.concatenate(lhs, axis=0), kg, (((1,), (1,)), ((), ())), preferred_element_type=F32)
        probs, rinv = [], []
        for a in range(GROUP):
            h = g * GROUP + a
            sa = s[a * m_rows:(a + 1) * m_rows] + bias_of(h)
            if extra_mask is not None:
                sa = jnp.where(extra_mask, NEG_INF, sa)
            sink = sink_of(h)
            m = jnp.maximum(jnp.max(sa, axis=1, keepdims=True), sink)
            p = jnp.exp(sa - m)
            den = jnp.sum(p, axis=1, keepdims=True) + jnp.exp(sink - m)
            probs.append(p.astype(BF16))
            rinv.append(1.0 / den)
        for sp in range(GROUP // 2):
            o = _mm(probs[2 * sp], v_lo) + _mm(probs[2 * sp + 1], v_hi)
            slabs.append(o * jnp.where(low_q, rinv[2 * sp], rinv[2 * sp + 1]))
    return jnp.concatenate(slabs, axis=1)


def _attn_p_kernel(sink_ref, q_ref, k_ref, v_ref, x_ref, bias_ref, wo_ref, y_ref, kbuf, vbuf, obuf):
    t = q_ref.shape[0]
    i = pl.program_id(0)

    @pl.when(i == 0)
    def _():
        kbuf[0:WINDOW, :] = jnp.zeros((WINDOW, 2 * KV_DIM), BF16)
        vbuf[0:WINDOW, :] = jnp.zeros((WINDOW, 2 * KV_DIM), BF16)

    @pl.when(i > 0)
    def _():
        kbuf[0:WINDOW, :] = kbuf[t:t + WINDOW, :]
        vbuf[0:WINDOW, :] = vbuf[t:t + WINDOW, :]

    kbuf[WINDOW:, :] = _dup_heads(k_ref[...])
    vbuf[WINDOW:, :] = _dup_heads(v_ref[...])
    col = lax.broadcasted_iota(jnp.int32, (WINDOW, 2 * WINDOW), 1)
    for j in range(t // WINDOW):
        rows = slice(j * WINDOW, (j + 1) * WINDOW)
        keys = slice(j * WINDOW, (j + 2) * WINDOW)
        extra = jnp.logical_and(i == 0, col < WINDOW) if j == 0 else None
        o = _attend(q_ref[rows, :], kbuf[keys, :], vbuf[keys, :], lambda h: bias_ref[h], lambda h: sink_ref[h], extra)
        obuf[rows, :] = o.astype(BF16)
    y_ref[...] = x_ref[...] + _mm(obuf[...], wo_ref[...])


def _attn_p(sinks, q, k, v, x, bias, wo):
    n = q.shape[0]
    t = _row_tile(n)
    row = lambda i: (i, 0)
    return pl.pallas_call(
        _attn_p_kernel,
        grid=(n // t,),
        in_specs=[pl.BlockSpec(memory_space=pltpu.SMEM),
                  pl.BlockSpec((t, N_HEADS * HEAD_DIM), row), pl.BlockSpec((t, KV_DIM), row),
                  pl.BlockSpec((t, KV_DIM), row), pl.BlockSpec((t, D_MODEL), row),
                  _full(bias.shape), _full(wo.shape)],
        out_specs=pl.BlockSpec((t, D_MODEL), row),
        out_shape=jax.ShapeDtypeStruct((n, D_MODEL), F32),
        scratch_shapes=[pltpu.VMEM((t + WINDOW, 2 * KV_DIM), BF16), pltpu.VMEM((t + WINDOW, 2 * KV_DIM), BF16),
                        pltpu.VMEM((t, N_HEADS * HEAD_DIM), BF16)],
        compiler_params=_params("arbitrary"),
        name="attn_prompt",
    )(sinks, q, k, v, x, bias, wo)


def _attn_s_kernel(sink_ref, q_ref, k_ref, v_ref, ck_ref, cv_ref, bias_ref, o_ref):
    pad = bias_ref.shape[2] - ck_ref.shape[0] - k_ref.shape[0]
    zpad = jnp.zeros((pad, KV_DIM), F32)
    k2 = _dup_heads(jnp.concatenate([ck_ref[...], k_ref[...], zpad], axis=0))
    v2 = _dup_heads(jnp.concatenate([cv_ref[...], v_ref[...], zpad], axis=0))
    o = _attend(q_ref[...], k2, v2, lambda h: bias_ref[h], lambda h: sink_ref[h], None)
    o_ref[...] = o.astype(BF16)


def _attn_s(sinks, q, k, v, ck, cv, bias, t_new):
    n = q.shape[0]
    rows = SAMPLE_SEQS * t_new
    crow = SAMPLE_SEQS * WINDOW
    row = lambda i: (i, 0)
    return pl.pallas_call(
        _attn_s_kernel,
        grid=(n // rows,),
        in_specs=[pl.BlockSpec(memory_space=pltpu.SMEM),
                  pl.BlockSpec((rows, N_HEADS * HEAD_DIM), row), pl.BlockSpec((rows, KV_DIM), row),
                  pl.BlockSpec((rows, KV_DIM), row), pl.BlockSpec((crow, KV_DIM), row),
                  pl.BlockSpec((crow, KV_DIM), row), _full(bias.shape)],
        out_specs=pl.BlockSpec((rows, N_HEADS * HEAD_DIM), row),
        out_shape=jax.ShapeDtypeStruct((n, N_HEADS * HEAD_DIM), BF16),
        compiler_params=_params("arbitrary"),
        name="attn_sample",
    )(sinks, q, k, v, ck, cv, bias)


def _proj_res_kernel(o_ref, x_ref, w_ref, y_ref):
    y_ref[...] = x_ref[...] + _mm(o_ref[...], w_ref[...])


def _proj_res(o, x, w):
    n = o.shape[0]
    t = _row_tile(n)
    row = lambda i: (i, 0)
    return pl.pallas_call(
        _proj_res_kernel,
        grid=(n // t,),
        in_specs=[pl.BlockSpec((t, o.shape[1]), row), pl.BlockSpec((t, D_MODEL), row), _full(w.shape)],
        out_specs=pl.BlockSpec((t, D_MODEL), row),
        out_shape=jax.ShapeDtypeStruct((n, D_MODEL), F32),
        compiler_params=_params("arbitrary"),
        name="proj_res",
    )(o, x, w)


def _alibi_slopes():
    return np.exp2(-8.0 * np.arange(1, N_HEADS + 1, dtype=np.float64) / N_HEADS).astype(np.float32)


def _band_bias(dist, allowed):
    b = -(_alibi_slopes()[:, None, None] * dist.astype(np.float32)[None])
    return jnp.asarray(np.where(allowed[None], b, np.float32(NEG_INF)).astype(np.float32))


def _prompt_bias():
    dist = WINDOW + np.arange(WINDOW)[:, None] - np.arange(2 * WINDOW)[None, :]
    return _band_bias(dist, (dist >= 0) & (dist <= WINDOW))


def _sample_bias(t_new, n_cols):
    c = np.arange(n_cols)
    n_cache = SAMPLE_SEQS * WINDOW
    n_new = SAMPLE_SEQS * t_new
    is_cache = c < n_cache
    is_new = (c >= n_cache) & (c < n_cache + n_new)
    seq_c = np.where(is_cache, c // WINDOW, (c - n_cache) // t_new)
    pos_c = np.where(is_cache, c % WINDOW, WINDOW + (c - n_cache) % t_new)
    r = np.arange(n_new)
    seq_r, tok_r = r // t_new, r % t_new
    dist = WINDOW + tok_r[:, None] - pos_c[None, :]
    allowed = (seq_r[:, None] == seq_c[None, :]) & (is_cache | is_new)[None, :] & (dist >= 0) & (dist <= WINDOW)
    return _band_bias(dist, allowed)


def _head_indicator(n_heads):
    ch = np.arange(n_heads * HEAD_DIM) // HEAD_DIM
    ind = (ch[:, None] == np.arange(LANES)[None, :]).astype(np.float32)
    return jnp.asarray(ind, BF16), jnp.asarray(ind.T, BF16)


def kernel(x_prompt, x_sample, state_conv, cache_k, cache_v, p_prompt, p_sample, norm_mix, norm_ffn, norm_ple,
           conv_w_in, conv_b_in, conv_w_dw, conv_b_dw, conv_ln_g, conv_ln_b, conv_w_out, conv_b_out, attn_w_qkv,
           attn_q_norm, attn_k_norm, attn_sinks, attn_w_o, moe_w_rg, moe_b_rg, moe_w_re, moe_b_re, moe_w_gate,
           moe_w_up, moe_w_down, ple_w_gate, ple_w_proj):
    bp, seq, d = x_prompt.shape
    bs, t_new, _ = x_sample.shape
    assert bp == 1 and d == D_MODEL and seq % WINDOW == 0 and bs % SAMPLE_SEQS == 0
    assert seq % (bs * t_new) == 0 and (bs * t_new) % DMA_CHUNK == 0
    depth = norm_mix.shape[0]
    row2 = lambda a: a.reshape(1, -1)

    y_p = x_prompt.reshape(seq, d)
    y_s = x_sample.reshape(bs * t_new, d)
    conv_p, conv_s, k_p, v_p, k_s, v_s = [], [], [], [], [], []

    for i in range(depth):
        j = i // 2
        g_mix = row2(norm_mix[i])
        if i % 2 == 0:
            w_in = conv_w_in[j].astype(BF16)
            b_in = row2(conv_b_in[j])
            tail = (conv_w_dw[j], row2(conv_b_dw[j]), row2(conv_ln_g[j]), row2(conv_ln_b[j]),
                    conv_w_out[j].astype(BF16), row2(conv_b_out[j]))
            u_p = _conv_in(y_p, g_mix, w_in, b_in)
            u_s = _conv_in(y_s, g_mix, w_in, b_in)
            conv_p.append(u_p[seq - CONV_STATE:].reshape(1, CONV_STATE, d))
            ue = jnp.concatenate([state_conv[j], u_s.reshape(bs, t_new, d)], axis=1)
            conv_s.append(ue[:, t_new:])
            y_p = _conv_out_p(u_p, y_p, *tail)
            ys_t = _conv_out_s(ue.transpose(1, 0, 2), y_s.reshape(bs, t_new, d).transpose(1, 0, 2), *tail)
            y_s = ys_t.transpose(1, 0, 2).reshape(bs * t_new, d)
        else:
            w_qkv = attn_w_qkv[j].astype(BF16)
            w_o = attn_w_o[j].astype(BF16)
            iq, iqt = _head_indicator(N_HEADS)
            ik, ikt = _head_indicator(N_KV_HEADS)
            qg = row2(jnp.tile(attn_q_norm[j], N_HEADS))
            kg = row2(jnp.tile(attn_k_norm[j], N_KV_HEADS))
            sinks = attn_sinks[j]
            q1, k1, v1 = _qkv(y_p, g_mix, w_qkv, iq, iqt, ik, ikt, qg, kg)
            q2, k2, v2 = _qkv(y_s, g_mix, w_qkv, iq, iqt, ik, ikt, qg, kg)
            k_p.append(k1[seq - WINDOW:].reshape(1, WINDOW, N_KV_HEADS, HEAD_DIM))
            v_p.append(v1[seq - WINDOW:].reshape(1, WINDOW, N_KV_HEADS, HEAD_DIM))
            shp = (bs, t_new, N_KV_HEADS, HEAD_DIM)
            k_s.append(jnp.concatenate([cache_k[j][:, t_new:], k2.reshape(shp)], axis=1))
            v_s.append(jnp.concatenate([cache_v[j][:, t_new:], v2.reshape(shp)], axis=1))
            y_p = _attn_p(sinks, q1, k1, v1, y_p, _prompt_bias(), w_o)
            n_cols = -(-(SAMPLE_SEQS * (WINDOW + t_new)) // LANES) * LANES
            o_s = _attn_s(sinks, q2, k2, v2, cache_k[j].reshape(bs * WINDOW, KV_DIM),
                          cache_v[j].reshape(bs * WINDOW, KV_DIM), _sample_bias(t_new, n_cols), t_new)
            y_s = _proj_res(o_s, y_s, w_o)

        w_r = jnp.zeros((d, ROUTER_LANES), F32)
        w_r = w_r.at[:, :N_EXPERT_GROUPS].set(moe_w_rg[i]).at[:, EXPERT_LANE0:EXPERT_LANE0 + N_EXPERTS].set(moe_w_re[i])
        b_r = jnp.zeros((1, ROUTER_LANES), F32)
        b_r = b_r.at[0, :N_EXPERT_GROUPS].set(moe_b_rg[i]).at[0, EXPERT_LANE0:EXPERT_LANE0 + N_EXPERTS].set(moe_b_re[i])
        moe = (row2(norm_ffn[i]), jnp.stack(_split_bf16(w_r)), b_r, moe_w_gate, moe_w_up, moe_w_down)
        ple = (row2(norm_ple[i]), ple_w_gate[i].astype(BF16), ple_w_proj[i].astype(BF16))
        y_p, y_s = _moe_ple(i, y_p, y_s, *moe, p_prompt.reshape(depth, seq, PLE_DIM),
                            p_sample[i].reshape(bs * t_new, PLE_DIM), *ple)

    return (y_p.reshape(1, seq, d), y_s.reshape(bs, t_new, d), jnp.stack(conv_p), jnp.stack(conv_s),
            jnp.stack(k_p), jnp.stack(v_p), jnp.stack(k_s), jnp.stack(v_s))
```

```python
import functools

import numpy as np

import jax
import jax.numpy as jnp
from jax import lax
from jax.experimental import pallas as pl
from jax.experimental.pallas import tpu as pltpu

F32 = jnp.float32
BF16 = jnp.bfloat16

D_MODEL = 1024
PLE_DIM = 256
CONV_WIDTH = 31
CONV_STATE = CONV_WIDTH - 1
N_HEADS = 16
N_KV_HEADS = 4
HEAD_DIM = 64
GROUP = N_HEADS // N_KV_HEADS
WINDOW = 128
KV_DIM = N_KV_HEADS * HEAD_DIM
N_EXPERT_GROUPS = 4
EXPERTS_PER_GROUP = 4
N_EXPERTS = 16
D_EXPERT = 256
EPS = 1e-6
NEG_INF = -1e30

LANES = 128
ROUTER_LANES = LANES
EXPERT_LANE0 = N_EXPERT_GROUPS
HALO = 32
CONV_ROWS = 32
NORM_ROWS = 128
CONV_PITCH = D_MODEL // LANES + 1
SAMPLE_SEQS = 8
PAIRS_PER_GROUP = EXPERTS_PER_GROUP * (EXPERTS_PER_GROUP - 1) // 2
N_BUCKETS = N_EXPERT_GROUPS * PAIRS_PER_GROUP
TM = 256
PAY_WIDTH = D_MODEL + LANES
ROW_PITCH = PAY_WIDTH // LANES
DMA_CHUNK = 512
RING = 3
VMEM_LIMIT = 48 * 1024 * 1024


def _row_tile(n):
    return 512 if n % 512 == 0 else n


def _params(*sem):
    return pltpu.CompilerParams(dimension_semantics=sem, vmem_limit_bytes=VMEM_LIMIT)


def _full(shape):
    nd = len(shape)
    return pl.BlockSpec(shape, lambda *_: (0,) * nd)


def _rms(x, g):
    ms = jnp.mean(x * x, axis=-1, keepdims=True)
    return x * lax.rsqrt(ms + EPS) * g


def _sigmoid(x):
    return 1.0 / (1.0 + jnp.exp(-x))


def _mm(a, b):
    return jnp.dot(a, b, preferred_element_type=F32)


def _conv_in_kernel(x_ref, g_ref, w_ref, b_ref, u_ref):
    h = _rms(x_ref[...], g_ref[...]).astype(BF16)
    z = _mm(h, w_ref[...]) + b_ref[...]
    u_ref[...] = z[:, :D_MODEL] * _sigmoid(z[:, D_MODEL:])


def _conv_in(x, g, w, b):
    n = x.shape[0]
    t = _row_tile(n)
    return pl.pallas_call(
        _conv_in_kernel,
        grid=(n // t,),
        in_specs=[pl.BlockSpec((t, D_MODEL), lambda i: (i, 0)), _full(g.shape), _full(w.shape), _full(b.shape)],
        out_specs=pl.BlockSpec((t, D_MODEL), lambda i: (i, 0)),
        out_shape=jax.ShapeDtypeStruct((n, D_MODEL), F32),
        compiler_params=_params("arbitrary"),
        name="conv_in",
    )(x, g, w, b)


def _ln_silu(c, g, b):
    mu = jnp.mean(c, axis=-1, keepdims=True)
    xc = c - mu
    var = jnp.mean(xc * xc, axis=-1, keepdims=True)
    cn = xc * lax.rsqrt(var + EPS) * g + b
    return cn * _sigmoid(cn)


def _conv_out_p_kernel(u_ref, halo_ref, x_ref, wdw_ref, bdw_ref, lng_ref, lnb_ref, wout_ref, bout_ref, y_ref,
                       ubuf, cbuf, hbuf):
    t = u_ref.shape[0]
    i = pl.program_id(0)
    nj = D_MODEL // LANES

    def put(r, j, val):
        ubuf[pl.ds(r * CONV_PITCH + j, 8, stride=CONV_PITCH), :] = val

    for r in range(0, HALO, 8):
        for j in range(nj):
            put(r, j, jnp.where(i > 0, halo_ref[r:r + 8, j * LANES:(j + 1) * LANES], 0.0))

    def fill(rr, carry):
        r = pl.multiple_of(rr * 8, 8)
        for j in range(nj):
            put(r + HALO, j, u_ref[pl.ds(r, 8), j * LANES:(j + 1) * LANES])
        return carry

    lax.fori_loop(0, t // 8, fill, 0)

    def conv_chunk(rr, carry):
        r0 = pl.multiple_of(rr * CONV_ROWS, CONV_ROWS)
        for j in range(nj):
            lanes = slice(j * LANES, (j + 1) * LANES)
            accs = [None] * (CONV_ROWS // 8)
            for k in range(CONV_WIDTH):
                wk = wdw_ref[k:k + 1, lanes]
                for q in range(CONV_ROWS // 8):
                    r = r0 + (HALO - CONV_STATE + k + 8 * q)
                    term = wk * ubuf[pl.ds(r * CONV_PITCH + j, 8, stride=CONV_PITCH), :]
                    accs[q] = term if k == 0 else accs[q] + term
            for q in range(CONV_ROWS // 8):
                cbuf[pl.ds(r0 + 8 * q, 8), lanes] = accs[q]
        return carry

    lax.fori_loop(0, t // CONV_ROWS, conv_chunk, 0)

    def norm_chunk(rr, carry):
        r0 = pl.multiple_of(rr * NORM_ROWS, NORM_ROWS)
        c = cbuf[pl.ds(r0, NORM_ROWS), :] + bdw_ref[...]
        hbuf[pl.ds(r0, NORM_ROWS), :] = _ln_silu(c, lng_ref[...], lnb_ref[...]).astype(BF16)
        return carry

    lax.fori_loop(0, t // NORM_ROWS, norm_chunk, 0)
    y_ref[...] = x_ref[...] + _mm(hbuf[...], wout_ref[...]) + bout_ref[...]


def _conv_out_p(u, x, wdw, bdw, lng, lnb, wout, bout):
    n = u.shape[0]
    t = _row_tile(n)
    hb = t // HALO
    row = lambda i: (i, 0)
    return pl.pallas_call(
        _conv_out_p_kernel,
        grid=(n // t,),
        in_specs=[pl.BlockSpec((t, D_MODEL), row),
                  pl.BlockSpec((HALO, D_MODEL), lambda i: (jnp.maximum(i * hb - 1, 0), 0)),
                  pl.BlockSpec((t, D_MODEL), row),
                  _full(wdw.shape), _full(bdw.shape), _full(lng.shape), _full(lnb.shape),
                  _full(wout.shape), _full(bout.shape)],
        out_specs=pl.BlockSpec((t, D_MODEL), row),
        out_shape=jax.ShapeDtypeStruct((n, D_MODEL), F32),
        scratch_shapes=[pltpu.VMEM(((t + HALO) * CONV_PITCH, LANES), F32), pltpu.VMEM((t, D_MODEL), F32),
                        pltpu.VMEM((t, D_MODEL), BF16)],
        compiler_params=_params("arbitrary"),
        name="conv_out_prompt",
    )(u, u, x, wdw, bdw, lng, lnb, wout, bout)


def _conv_out_s_kernel(st_ref, u_ref, x_ref, wdw_ref, bdw_ref, lng_ref, lnb_ref, wout_ref, bout_ref, y_ref, ns_ref,
                       wbuf, cbuf):
    bb, nt, _ = u_ref.shape
    win = CONV_STATE + nt
    nj = D_MODEL // LANES
    seq_pitch = win * CONV_PITCH

    def put(b, r0, rows, j, val):
        wbuf[pl.ds((b * win + r0) * CONV_PITCH + j, rows, stride=CONV_PITCH), :] = val

    for b in range(bb):
        for j in range(nj):
            lanes = slice(j * LANES, (j + 1) * LANES)
            for r0 in range(0, CONV_STATE, 8):
                rows = min(8, CONV_STATE - r0)
                put(b, r0, rows, j, st_ref[b, r0:r0 + rows, lanes])
            put(b, CONV_STATE, nt, j, u_ref[b, :, lanes])

    for b in range(bb):
        for r0 in range(0, CONV_STATE, 8):
            rows = min(8, CONV_STATE - r0)
            for j in range(nj):
                src = (b * win + nt + r0) * CONV_PITCH + j
                ns_ref[b, r0:r0 + rows, j * LANES:(j + 1) * LANES] = wbuf[pl.ds(src, rows, stride=CONV_PITCH), :]

    for t in range(nt):
        for b0 in range(0, bb, 8):
            for j in range(nj):
                lanes = slice(j * LANES, (j + 1) * LANES)
                acc = None
                for k in range(CONV_WIDTH):
                    start = (b0 * win + t + k) * CONV_PITCH + j
                    term = wdw_ref[k:k + 1, lanes] * wbuf[pl.ds(start, 8, stride=seq_pitch), :]
                    acc = term if acc is None else acc + term
                cbuf[t * bb + b0:t * bb + b0 + 8, lanes] = acc
    c = cbuf[...] + bdw_ref[...]
    h = _ln_silu(c, lng_ref[...], lnb_ref[...]).astype(BF16)
    y = _mm(h, wout_ref[...]) + bout_ref[...]
    for t in range(nt):
        y_ref[t] = x_ref[t] + y[t * bb:(t + 1) * bb]


def _conv_out_s(state, u, x, wdw, bdw, lng, lnb, wout, bout):
    b, nt, _ = u.shape
    bb = 16 if b % 16 == 0 else b
    assert bb % 8 == 0
    win = CONV_STATE + nt
    seq = lambda i: (i, 0, 0)
    tm = lambda i: (0, i, 0)
    return pl.pallas_call(
        _conv_out_s_kernel,
        grid=(b // bb,),
        in_specs=[pl.BlockSpec((bb, CONV_STATE, D_MODEL), seq), pl.BlockSpec((bb, nt, D_MODEL), seq),
                  pl.BlockSpec((nt, bb, D_MODEL), tm),
                  _full(wdw.shape), _full(bdw.shape), _full(lng.shape), _full(lnb.shape),
                  _full(wout.shape), _full(bout.shape)],
        out_specs=[pl.BlockSpec((nt, bb, D_MODEL), tm), pl.BlockSpec((bb, CONV_STATE, D_MODEL), seq)],
        out_shape=[jax.ShapeDtypeStruct((nt, b, D_MODEL), F32), jax.ShapeDtypeStruct((b, CONV_STATE, D_MODEL), F32)],
        scratch_shapes=[pltpu.VMEM((bb * win * CONV_PITCH, LANES), F32), pltpu.VMEM((nt * bb, D_MODEL), F32)],
        compiler_params=_params("arbitrary"),
        name="conv_out_sample",
    )(state, u, x, wdw, bdw, lng, lnb, wout, bout)


def _route(lg):
    big = 3.0e38
    lane = lax.broadcasted_iota(jnp.int32, lg.shape, 1)
    lanef = lane.astype(F32)
    is_g = lane < N_EXPERT_GROUPS
    gl = jnp.where(is_g, lg, -big)
    gmax = jnp.max(gl, axis=1, keepdims=True)
    gsum = jnp.sum(jnp.where(is_g, jnp.exp(gl - gmax), 0.0), axis=1, keepdims=True)
    g_w = 1.0 / gsum
    g_idx = jnp.min(jnp.where(gl == gmax, lanef, big), axis=1, keepdims=True)
    rel = lanef - float(EXPERT_LANE0) - g_idx * float(EXPERTS_PER_GROUP)
    in_grp = jnp.where(rel >= 0.0, jnp.where(rel < float(EXPERTS_PER_GROUP), 1.0, 0.0), 0.0) > 0.5
    el = jnp.where(in_grp, lg, -big)
    e1 = jnp.max(el, axis=1, keepdims=True)
    i1 = jnp.min(jnp.where(el == e1, lanef, big), axis=1, keepdims=True)
    el2 = jnp.where(lanef == i1, -big, el)
    e2 = jnp.max(el2, axis=1, keepdims=True)
    i2 = jnp.min(jnp.where(el2 == e2, lanef, big), axis=1, keepdims=True)
    tt = jnp.exp(e2 - e1)
    w1 = g_w / (1.0 + tt)
    w2 = g_w * tt / (1.0 + tt)
    base = float(EXPERT_LANE0) + g_idx * float(EXPERTS_PER_GROUP)
    a = jnp.minimum(i1, i2) - base
    b = jnp.maximum(i1, i2) - base
    pair = a * (7.0 - a) * 0.5 + (b - a - 1.0)
    first_is_lo = i1 < i2
    return (g_idx * float(PAIRS_PER_GROUP) + pair, jnp.where(first_is_lo, w1, w2), jnp.where(first_is_lo, w2, w1))


def _to_row_linear(dst_ref, src_ref, n_tiles):
    def body(g, carry):
        r = pl.multiple_of(g * 8, 8)
        for j in range(n_tiles):
            dst_ref[pl.ds(r * ROW_PITCH + j, 8, stride=ROW_PITCH), :] = src_ref[pl.ds(r, 8), j * LANES:(j + 1) * LANES]
        return carry

    lax.fori_loop(0, src_ref.shape[0] // 8, body, 0)


def _from_row_linear(dst_ref, src_ref, n_tiles, pitch):
    def body(g, carry):
        r = pl.multiple_of(g * 8, 8)
        for j in range(n_tiles):
            dst_ref[pl.ds(r, 8), j * LANES:(j + 1) * LANES] = src_ref[pl.ds(r * pitch + j, 8, stride=pitch), :]
        return carry

    lax.fori_loop(0, dst_ref.shape[0] // 8, body, 0)


def _pick(i, n_prompt_tiles, prompt_ref, sample_ref):
    return jnp.where(i < n_prompt_tiles, prompt_ref[...], sample_ref[...])


def _route_kernel(n_prompt_tiles, yp_ref, ys_ref, g_ref, wr_ref, br_ref, tri_ref, pay_ref, meta_ref, cnt_ref, carry,
                  rowbuf):
    i = pl.program_id(0)

    @pl.when(i == 0)
    def _():
        carry[...] = jnp.zeros_like(carry)

    xf = _rms(_pick(i, n_prompt_tiles, yp_ref, ys_ref), g_ref[...])
    x_hi, x_lo = _split_bf16(xf)
    logits = _mm(x_hi, wr_ref[0]) + (_mm(x_lo, wr_ref[0]) + _mm(x_hi, wr_ref[1])) + br_ref[...]
    bucket, w_lo, w_hi = _route(logits)
    lane = lax.broadcasted_iota(jnp.int32, logits.shape, 1)
    onehot = jnp.where(lane.astype(F32) == bucket, 1.0, 0.0)
    before = _mm(tri_ref[...], onehot.astype(BF16)) + carry[...]
    rank = jnp.sum(onehot * before, axis=1, keepdims=True)
    carry[...] += jnp.sum(onehot, axis=0, keepdims=True)
    cnt_ref[...] = carry[...]
    meta = jnp.where(lane == 0, bucket, jnp.where(lane == 1, rank, 0.0))
    meta_ref[...] = jnp.transpose(meta)[:8, :].astype(jnp.int32)

    rowbuf[:, :D_MODEL] = xf
    rowbuf[:, D_MODEL:] = jnp.where(lane == 0, w_lo, jnp.where(lane == 1, w_hi, 0.0))
    _to_row_linear(pay_ref, rowbuf, ROW_PITCH)


def _route_call(y_p, y_s, g, wr, br, tri):
    t = y_s.shape[0]
    npt = y_p.shape[0] // t
    n = y_p.shape[0] + t
    pidx = lambda i: (jnp.minimum(i, npt - 1), 0)
    return pl.pallas_call(
        functools.partial(_route_kernel, npt),
        grid=(npt + 1,),
        in_specs=[pl.BlockSpec((t, D_MODEL), pidx), _full(y_s.shape), _full(g.shape), _full(wr.shape),
                  _full(br.shape), _full(tri.shape)],
        out_specs=[pl.BlockSpec((t * ROW_PITCH, LANES), lambda i: (i, 0)),
                   pl.BlockSpec((8, t), lambda i: (0, i)), _full((1, ROUTER_LANES))],
        out_shape=[jax.ShapeDtypeStruct((n * ROW_PITCH, LANES), F32),
                   jax.ShapeDtypeStruct((8, n), jnp.int32), jax.ShapeDtypeStruct((1, ROUTER_LANES), F32)],
        scratch_shapes=[pltpu.VMEM((1, ROUTER_LANES), F32), pltpu.VMEM((t, PAY_WIDTH), F32)],
        compiler_params=_params("arbitrary"),
        name="route",
    )(y_p, y_s, g, wr, br, tri)


def _permute_kernel(pos_ref, zstart_ref, nz_ref, used_ref, src_ref, dst_ref, zbuf, ring, zsem, in_sems, out_sems):
    zbuf[...] = jnp.zeros_like(zbuf)

    def zero_tile(start):
        return pltpu.make_async_copy(zbuf, dst_ref.at[pl.ds(pl.multiple_of(start * ROW_PITCH, 8), TM * ROW_PITCH)], zsem)

    for b in range(N_BUCKETS):
        @pl.when(nz_ref[b] > 0)
        def _():
            zero_tile(zstart_ref[b]).start()
    n_tiles = dst_ref.shape[0] // (TM * ROW_PITCH)

    def start_unused(i, carry):
        zero_tile(i * TM).start()
        return carry

    lax.fori_loop(used_ref[0], n_tiles, start_unused, 0)
    for b in range(N_BUCKETS):
        @pl.when(nz_ref[b] > 0)
        def _():
            zero_tile(0).wait()

    def wait_unused(i, carry):
        zero_tile(0).wait()
        return carry

    lax.fori_loop(used_ref[0], n_tiles, wait_unused, 0)

    chunk_rows = DMA_CHUNK * ROW_PITCH
    n_chunks = src_ref.shape[0] // chunk_rows

    def fetch(c, slot):
        return pltpu.make_async_copy(src_ref.at[pl.ds(pl.multiple_of(c * chunk_rows, 8), chunk_rows)], ring.at[slot],
                                     in_sems.at[slot])

    def drain(slot):
        pltpu.make_async_copy(ring.at[slot], dst_ref.at[pl.ds(0, chunk_rows)], out_sems.at[slot]).wait()

    fetch(0, 0).start()

    def step(c, carry):
        slot = lax.rem(c, RING)
        nxt = lax.rem(c + 1, RING)
        fetch(c, slot).wait()

        @pl.when(c >= RING - 1)
        def _():
            drain(nxt)

        @pl.when(c + 1 < n_chunks)
        def _():
            fetch(c + 1, nxt).start()

        def send(jj, carry2):
            row = pos_ref[c * DMA_CHUNK + jj] * ROW_PITCH
            pltpu.make_async_copy(ring.at[slot, pl.ds(jj * ROW_PITCH, ROW_PITCH)], dst_ref.at[pl.ds(row, ROW_PITCH)],
                                  out_sems.at[slot]).start()
            return carry2

        lax.fori_loop(0, DMA_CHUNK, send, 0, unroll=8)
        return carry

    lax.fori_loop(0, n_chunks, step, 0)
    for c in range(n_chunks - (RING - 1), n_chunks):
        drain(c % RING)


def _permute(pos, zstart, nz, used, payload, n_rows_out):
    assert payload.shape[0] // (DMA_CHUNK * ROW_PITCH) >= RING
    return pl.pallas_call(
        _permute_kernel,
        grid_spec=pltpu.PrefetchScalarGridSpec(
            num_scalar_prefetch=4, grid=(1,),
            in_specs=[pl.BlockSpec(memory_space=pl.ANY)],
            out_specs=pl.BlockSpec(memory_space=pl.ANY),
            scratch_shapes=[pltpu.VMEM((TM * ROW_PITCH, LANES), payload.dtype),
                            pltpu.VMEM((RING, DMA_CHUNK * ROW_PITCH, LANES), payload.dtype),
                            pltpu.SemaphoreType.DMA(()), pltpu.SemaphoreType.DMA((RING,)),
                            pltpu.SemaphoreType.DMA((RING,))]),
        out_shape=jax.ShapeDtypeStruct((n_rows_out * ROW_PITCH, LANES), payload.dtype),
        compiler_params=_params("arbitrary"),
        name="permute",
    )(pos, zstart, nz, used, payload)


def _silu(x):
    return x * _sigmoid(x)


def _experts_kernel(tix_ref, lo_ref, hi_ref, valid_ref, fresh_ref, xs_ref, wgl_ref, wgh_ref, wul_ref, wuh_ref,
                    wdl_ref, wdh_ref, ys_ref, wup_s, wdn_s, xbuf, ybuf):
    i = pl.program_id(0)

    @pl.when(fresh_ref[i] > 0)
    def _():
        for s, ref in enumerate((wgl_ref, wgh_ref, wul_ref, wuh_ref)):
            wup_s[s] = ref[0, 0].astype(BF16)
        for s, ref in enumerate((wdl_ref, wdh_ref)):
            wdn_s[s] = ref[0, 0].astype(BF16)

    @pl.when(valid_ref[i] == 0)
    def _():
        ys_ref[...] = jnp.zeros_like(ys_ref)

    @pl.when(valid_ref[i] > 0)
    def _():
        _from_row_linear(xbuf, xs_ref, ROW_PITCH, ROW_PITCH)
        x = xbuf[:, :D_MODEL].astype(BF16)
        gates = xbuf[:, D_MODEL:]
        h_lo = _silu(_mm(x, wup_s[0])) * _mm(x, wup_s[2]) * gates[:, 0:1]
        h_hi = _silu(_mm(x, wup_s[1])) * _mm(x, wup_s[3]) * gates[:, 1:2]
        ybuf[:, :D_MODEL] = _mm(h_lo.astype(BF16), wdn_s[0]) + _mm(h_hi.astype(BF16), wdn_s[1])
        ybuf[:, D_MODEL:] = jnp.zeros((TM, LANES), F32)
        _to_row_linear(ys_ref, ybuf, ROW_PITCH)


def _experts(layer, tix, lo, hi, valid, fresh, xs, wg, wu, wd):
    n_tiles = xs.shape[0] // (TM * ROW_PITCH)
    row = lambda i, tix, lo, hi, valid, fresh: (tix[i], 0)
    e_lo = lambda i, tix, lo, hi, valid, fresh: (layer, lo[i], 0, 0)
    e_hi = lambda i, tix, lo, hi, valid, fresh: (layer, hi[i], 0, 0)
    up = pl.BlockSpec((1, 1, D_MODEL, D_EXPERT), e_lo), pl.BlockSpec((1, 1, D_MODEL, D_EXPERT), e_hi)
    down = pl.BlockSpec((1, 1, D_EXPERT, D_MODEL), e_lo), pl.BlockSpec((1, 1, D_EXPERT, D_MODEL), e_hi)
    return pl.pallas_call(
        _experts_kernel,
        grid_spec=pltpu.PrefetchScalarGridSpec(
            num_scalar_prefetch=5, grid=(n_tiles,),
            in_specs=[pl.BlockSpec((TM * ROW_PITCH, LANES), row), *up, *up, *down],
            out_specs=pl.BlockSpec((TM * ROW_PITCH, LANES), row),
            scratch_shapes=[pltpu.VMEM((4, D_MODEL, D_EXPERT), BF16), pltpu.VMEM((2, D_EXPERT, D_MODEL), BF16),
                            pltpu.VMEM((TM, PAY_WIDTH), F32), pltpu.VMEM((TM, PAY_WIDTH), F32)]),
        out_shape=jax.ShapeDtypeStruct(xs.shape, F32),
        compiler_params=_params("arbitrary"),
        name="experts",
    )(tix, lo, hi, valid, fresh, xs, wg, wg, wu, wu, wd, wd)


def _ple_kernel(n_prompt_tiles, pos_ref, yp_ref, ys_ref, moe_ref, pp_ref, ps_ref, g_ref, wg_ref, wp_ref, op_ref, os_ref,
                mring, mbuf, sems):
    i = pl.program_id(0)
    t = mbuf.shape[0]
    w = D_MODEL // LANES

    def gather(tile, slot):
        def body(jj, carry):
            row = pos_ref[tile * t + jj] * ROW_PITCH
            pltpu.make_async_copy(moe_ref.at[pl.ds(row, w)], mring.at[slot, pl.ds(jj * w, w)], sems.at[slot]).start()
            return carry

        lax.fori_loop(0, t, body, 0, unroll=8)

    @pl.when(i == 0)
    def _():
        gather(0, 0)

    @pl.when(i < n_prompt_tiles)
    def _():
        gather(i + 1, lax.rem(i + 1, 2))

    slot = lax.rem(i, 2)
    pltpu.make_async_copy(moe_ref.at[pl.ds(0, t * w)], mring.at[slot], sems.at[slot]).wait()
    _from_row_linear(mbuf, mring.at[slot], w, w)
    y2 = _pick(i, n_prompt_tiles, yp_ref, ys_ref) + mbuf[...]
    hn = _rms(y2, g_ref[...]).astype(BF16)
    gt = _sigmoid(_mm(hn, wg_ref[...]))
    pr = _mm(jnp.where(i < n_prompt_tiles, pp_ref[0], ps_ref[...]).astype(BF16), wp_ref[...])
    y3 = y2 + gt * pr

    @pl.when(i < n_prompt_tiles)
    def _():
        op_ref[...] = y3

    @pl.when(i >= n_prompt_tiles)
    def _():
        os_ref[...] = y3


def _ple(layer, pos, y_p, y_s, ys_sorted, p_p, p_s, g, wg, wp):
    t = y_s.shape[0]
    npt = y_p.shape[0] // t
    w = D_MODEL // LANES
    pidx = lambda i, pos: (jnp.minimum(i, npt - 1), 0)
    full = lambda shape: pl.BlockSpec(shape, lambda i, pos: (0,) * len(shape))
    return pl.pallas_call(
        functools.partial(_ple_kernel, npt),
        grid_spec=pltpu.PrefetchScalarGridSpec(
            num_scalar_prefetch=1, grid=(npt + 1,),
            in_specs=[pl.BlockSpec((t, D_MODEL), pidx), full(y_s.shape), pl.BlockSpec(memory_space=pl.ANY),
                      pl.BlockSpec((1, t, PLE_DIM), lambda i, pos: (layer, jnp.minimum(i, npt - 1), 0)),
                      full(p_s.shape), full(g.shape), full(wg.shape), full(wp.shape)],
            out_specs=[pl.BlockSpec((t, D_MODEL), pidx), full(y_s.shape)],
            scratch_shapes=[pltpu.VMEM((2, t * w, LANES), F32), pltpu.VMEM((t, D_MODEL), F32),
                            pltpu.SemaphoreType.DMA((2,))]),
        out_shape=[jax.ShapeDtypeStruct(y_p.shape, F32), jax.ShapeDtypeStruct(y_s.shape, F32)],
        compiler_params=_params("arbitrary"),
        name="ple",
    )(pos, y_p, y_s, ys_sorted, p_p, p_s, g, wg, wp)


def _bucket_experts():
    pairs = [(a, b) for a in range(EXPERTS_PER_GROUP) for b in range(a + 1, EXPERTS_PER_GROUP)]
    lo = [g * EXPERTS_PER_GROUP + a for g in range(N_EXPERT_GROUPS) for a, _ in pairs]
    hi = [g * EXPERTS_PER_GROUP + b for g in range(N_EXPERT_GROUPS) for _, b in pairs]
    return jnp.array(lo, jnp.int32), jnp.array(hi, jnp.int32)


def _moe_ple(layer, y_p, y_s, gffn, wr, br, wg, wu, wd, p_p, p_s, gple, pwg, pwp):
    t = y_s.shape[0]
    n = y_p.shape[0] + t
    tri = jnp.asarray(np.tri(t, t, -1, dtype=np.float32), BF16)
    payload, meta, counts = _route_call(y_p, y_s, gffn, wr, br, tri)

    bucket, rank = meta[0], meta[1]
    cnt = counts[0, :N_BUCKETS].astype(jnp.int32)
    padded = (cnt + TM - 1) // TM * TM
    ends = jnp.cumsum(padded)
    starts = ends - padded
    in_bucket = bucket[:, None] == jnp.arange(N_BUCKETS, dtype=jnp.int32)[None, :]
    pos = rank + jnp.sum(jnp.where(in_bucket, starts[None, :], 0), axis=1)
    n_rows = n + N_BUCKETS * TM
    n_tiles = n_rows // TM
    tile_start = jnp.arange(n_tiles, dtype=jnp.int32) * TM
    valid = tile_start < ends[-1]
    used = (ends[-1] // TM).astype(jnp.int32)
    tix = jnp.arange(n_tiles, dtype=jnp.int32)
    tile_bucket = jnp.sum((ends[None, :] <= (jnp.minimum(tix, used - 1) * TM)[:, None]).astype(jnp.int32), axis=1)
    lo_tab, hi_tab = _bucket_experts()
    lo = jnp.take(lo_tab, tile_bucket)
    hi = jnp.take(hi_tab, tile_bucket)
    prev_bucket = jnp.concatenate([jnp.full((1,), -1, jnp.int32), tile_bucket[:-1]])
    fresh = valid & (tile_bucket != prev_bucket)

    xs = _permute(pos, (ends - TM).astype(jnp.int32), (cnt > 0).astype(jnp.int32), used.reshape(1), payload, n_rows)
    ys = _experts(layer, tix, lo, hi, valid.astype(jnp.int32), fresh.astype(jnp.int32), xs, wg, wu, wd)
    return _ple(layer, pos, y_p, y_s, ys, p_p, p_s, gple, pwg, pwp)


def _split_bf16(a):
    hi = a.astype(BF16)
    lo = (a - hi.astype(F32)).astype(BF16)
    return hi, lo


def _head_norm(a, ind_ref, indt_ref, gain):
    hi, lo = _split_bf16(a * a)
    ss = _mm(hi, ind_ref[...]) + _mm(lo, ind_ref[...])
    inv = lax.rsqrt(ss * (1.0 / HEAD_DIM) + EPS)
    ihi, ilo = _split_bf16(inv)
    invb = _mm(ihi, indt_ref[...]) + _mm(ilo, indt_ref[...])
    return a * invb * gain


def _qkv_kernel(x_ref, g_ref, w_ref, iq_ref, iqt_ref, ik_ref, ikt_ref, qg_ref, kg_ref, q_ref, k_ref, v_ref):
    h = _rms(x_ref[...], g_ref[...]).astype(BF16)
    qkv = _mm(h, w_ref[...])
    nq = N_HEADS * HEAD_DIM
    q = _head_norm(qkv[:, :nq], iq_ref, iqt_ref, qg_ref[...])
    k = _head_norm(qkv[:, nq:nq + KV_DIM], ik_ref, ikt_ref, kg_ref[...])
    q_ref[...] = (q * (HEAD_DIM ** -0.5)).astype(BF16)
    k_ref[...] = k
    v_ref[...] = qkv[:, nq + KV_DIM:]


def _qkv(x, g, w, iq, iqt, ik, ikt, qg, kg):
    n = x.shape[0]
    t = _row_tile(n)
    row = lambda i: (i, 0)
    return pl.pallas_call(
        _qkv_kernel,
        grid=(n // t,),
        in_specs=[pl.BlockSpec((t, D_MODEL), row)] + [_full(a.shape) for a in (g, w, iq, iqt, ik, ikt, qg, kg)],
        out_specs=[pl.BlockSpec((t, N_HEADS * HEAD_DIM), row), pl.BlockSpec((t, KV_DIM), row),
                   pl.BlockSpec((t, KV_DIM), row)],
        out_shape=[jax.ShapeDtypeStruct((n, N_HEADS * HEAD_DIM), BF16), jax.ShapeDtypeStruct((n, KV_DIM), F32),
                   jax.ShapeDtypeStruct((n, KV_DIM), F32)],
        compiler_params=_params("arbitrary"),
        name="qkv",
    )(x, g, w, iq, iqt, ik, ikt, qg, kg)


def _dup_heads(a):
    out = []
    for s in range(KV_DIM // LANES):
        sl = a[:, s * LANES:(s + 1) * LANES]
        sw = pltpu.roll(sl, HEAD_DIM, axis=1)
        low = lax.broadcasted_iota(jnp.int32, sl.shape, 1) < HEAD_DIM
        out.append(jnp.where(low, sl, sw))
        out.append(jnp.where(low, sw, sl))
    return jnp.concatenate(out, axis=1).astype(BF16)


def _attend(q_rows, k2, v2, bias_of, sink_of, extra_mask):
    m_rows = q_rows.shape[0]
    low_q = lax.broadcasted_iota(jnp.int32, (m_rows, LANES), 1) < HEAD_DIM
    low_k = lax.broadcasted_iota(jnp.int32, (k2.shape[0], LANES), 1) < HEAD_DIM
    zero_q = jnp.zeros((m_rows, LANES), BF16)
    zero_k = jnp.zeros((k2.shape[0], LANES), BF16)
    slabs = []
    for g in range(N_KV_HEADS):
        kg = k2[:, g * LANES:(g + 1) * LANES]
        vg = v2[:, g * LANES:(g + 1) * LANES]
        v_lo = jnp.where(low_k, vg, zero_k)
        v_hi = jnp.where(low_k, zero_k, vg)
        lhs = []
        for a in range(GROUP):
            h = g * GROUP + a
            qs = q_rows[:, (h // 2) * LANES:(h // 2 + 1) * LANES]
            lhs.append(jnp.where(low_q, qs, zero_q) if h % 2 == 0 else jnp.where(low_q, zero_q, qs))
        s = lax.dot_general(jnp.concatenate(lhs, axis=0), kg, (((1,), (1,)), ((), ())), preferred_element_type=F32)
        probs, rinv = [], []
        for a in range(GROUP):
            h = g * GROUP + a
            sa = s[a * m_rows:(a + 1) * m_rows] + bias_of(h)
            if extra_mask is not None:
                sa = jnp.where(extra_mask, NEG_INF, sa)
            sink = sink_of(h)
            m = jnp.maximum(jnp.max(sa, axis=1, keepdims=True), sink)
            p = jnp.exp(sa - m)
            den = jnp.sum(p, axis=1, keepdims=True) + jnp.exp(sink - m)
            probs.append(p.astype(BF16))
            rinv.append(1.0 / den)
        for sp in range(GROUP // 2):
            o = _mm(probs[2 * sp], v_lo) + _mm(probs[2 * sp + 1], v_hi)
            slabs.append(o * jnp.where(low_q, rinv[2 * sp], rinv[2 * sp + 1]))
    return jnp.concatenate(slabs, axis=1)


def _attn_p_kernel(sink_ref, q_ref, k_ref, v_ref, x_ref, bias_ref, wo_ref, y_ref, kbuf, vbuf, obuf):
    t = q_ref.shape[0]
    i = pl.program_id(0)

    @pl.when(i == 0)
    def _():
        kbuf[0:WINDOW, :] = jnp.zeros((WINDOW, 2 * KV_DIM), BF16)
        vbuf[0:WINDOW, :] = jnp.zeros((WINDOW, 2 * KV_DIM), BF16)

    @pl.when(i > 0)
    def _():
        kbuf[0:WINDOW, :] = kbuf[t:t + WINDOW, :]
        vbuf[0:WINDOW, :] = vbuf[t:t + WINDOW, :]

    kbuf[WINDOW:, :] = _dup_heads(k_ref[...])
    vbuf[WINDOW:, :] = _dup_heads(v_ref[...])
    col = lax.broadcasted_iota(jnp.int32, (WINDOW, 2 * WINDOW), 1)
    for j in range(t // WINDOW):
        rows = slice(j * WINDOW, (j + 1) * WINDOW)
        keys = slice(j * WINDOW, (j + 2) * WINDOW)
        extra = jnp.logical_and(i == 0, col < WINDOW) if j == 0 else None
        o = _attend(q_ref[rows, :], kbuf[keys, :], vbuf[keys, :], lambda h: bias_ref[h], lambda h: sink_ref[h], extra)
        obuf[rows, :] = o.astype(BF16)
    y_ref[...] = x_ref[...] + _mm(obuf[...], wo_ref[...])


def _attn_p(sinks, q, k, v, x, bias, wo):
    n = q.shape[0]
    t = _row_tile(n)
    row = lambda i: (i, 0)
    return pl.pallas_call(
        _attn_p_kernel,
        grid=(n // t,),
        in_specs=[pl.BlockSpec(memory_space=pltpu.SMEM),
                  pl.BlockSpec((t, N_HEADS * HEAD_DIM), row), pl.BlockSpec((t, KV_DIM), row),
                  pl.BlockSpec((t, KV_DIM), row), pl.BlockSpec((t, D_MODEL), row),
                  _full(bias.shape), _full(wo.shape)],
        out_specs=pl.BlockSpec((t, D_MODEL), row),
        out_shape=jax.ShapeDtypeStruct((n, D_MODEL), F32),
        scratch_shapes=[pltpu.VMEM((t + WINDOW, 2 * KV_DIM), BF16), pltpu.VMEM((t + WINDOW, 2 * KV_DIM), BF16),
                        pltpu.VMEM((t, N_HEADS * HEAD_DIM), BF16)],
        compiler_params=_params("arbitrary"),
        name="attn_prompt",
    )(sinks, q, k, v, x, bias, wo)


def _attn_s_kernel(sink_ref, q_ref, k_ref, v_ref, ck_ref, cv_ref, bias_ref, o_ref, nk_ref, nv_ref):
    pad = bias_ref.shape[2] - ck_ref.shape[0] - k_ref.shape[0]
    zpad = jnp.zeros((pad, KV_DIM), F32)
    k2 = _dup_heads(jnp.concatenate([ck_ref[...], k_ref[...], zpad], axis=0))
    v2 = _dup_heads(jnp.concatenate([cv_ref[...], v_ref[...], zpad], axis=0))
    o = _attend(q_ref[...], k2, v2, lambda h: bias_ref[h], lambda h: sink_ref[h], None)
    o_ref[...] = o.astype(BF16)
    t_new = k_ref.shape[0] // SAMPLE_SEQS
    for new_ref, old_ref, add_ref in ((nk_ref, ck_ref, k_ref), (nv_ref, cv_ref, v_ref)):
        for b in range(SAMPLE_SEQS):
            r = b * WINDOW
            new_ref[r:r + WINDOW - t_new, :] = old_ref[r + t_new:r + WINDOW, :]
            new_ref[r + WINDOW - t_new:r + WINDOW, :] = add_ref[b * t_new:(b + 1) * t_new, :]


def _attn_s(sinks, q, k, v, ck, cv, bias, t_new):
    n = q.shape[0]
    rows = SAMPLE_SEQS * t_new
    crow = SAMPLE_SEQS * WINDOW
    row = lambda i: (i, 0)
    return pl.pallas_call(
        _attn_s_kernel,
        grid=(n // rows,),
        in_specs=[pl.BlockSpec(memory_space=pltpu.SMEM),
                  pl.BlockSpec((rows, N_HEADS * HEAD_DIM), row), pl.BlockSpec((rows, KV_DIM), row),
                  pl.BlockSpec((rows, KV_DIM), row), pl.BlockSpec((crow, KV_DIM), row),
                  pl.BlockSpec((crow, KV_DIM), row), _full(bias.shape)],
        out_specs=[pl.BlockSpec((rows, N_HEADS * HEAD_DIM), row), pl.BlockSpec((crow, KV_DIM), row),
                   pl.BlockSpec((crow, KV_DIM), row)],
        out_shape=[jax.ShapeDtypeStruct((n, N_HEADS * HEAD_DIM), BF16), jax.ShapeDtypeStruct(ck.shape, F32),
                   jax.ShapeDtypeStruct(cv.shape, F32)],
        compiler_params=_params("arbitrary"),
        name="attn_sample",
    )(sinks, q, k, v, ck, cv, bias)


def _proj_res_kernel(o_ref, x_ref, w_ref, y_ref):
    y_ref[...] = x_ref[...] + _mm(o_ref[...], w_ref[...])


def _proj_res(o, x, w):
    n = o.shape[0]
    t = _row_tile(n)
    row = lambda i: (i, 0)
    return pl.pallas_call(
        _proj_res_kernel,
        grid=(n // t,),
        in_specs=[pl.BlockSpec((t, o.shape[1]), row), pl.BlockSpec((t, D_MODEL), row), _full(w.shape)],
        out_specs=pl.BlockSpec((t, D_MODEL), row),
        out_shape=jax.ShapeDtypeStruct((n, D_MODEL), F32),
        compiler_params=_params("arbitrary"),
        name="proj_res",
    )(o, x, w)


def _alibi_slopes():
    return np.exp2(-8.0 * np.arange(1, N_HEADS + 1, dtype=np.float64) / N_HEADS).astype(np.float32)


def _band_bias(dist, allowed):
    b = -(_alibi_slopes()[:, None, None] * dist.astype(np.float32)[None])
    return jnp.asarray(np.where(allowed[None], b, np.float32(NEG_INF)).astype(np.float32))


def _prompt_bias():
    dist = WINDOW + np.arange(WINDOW)[:, None] - np.arange(2 * WINDOW)[None, :]
    return _band_bias(dist, (dist >= 0) & (dist <= WINDOW))


def _sample_bias(t_new, n_cols):
    c = np.arange(n_cols)
    n_cache = SAMPLE_SEQS * WINDOW
    n_new = SAMPLE_SEQS * t_new
    is_cache = c < n_cache
    is_new = (c >= n_cache) & (c < n_cache + n_new)
    seq_c = np.where(is_cache, c // WINDOW, (c - n_cache) // t_new)
    pos_c = np.where(is_cache, c % WINDOW, WINDOW + (c - n_cache) % t_new)
    r = np.arange(n_new)
    seq_r, tok_r = r // t_new, r % t_new
    dist = WINDOW + tok_r[:, None] - pos_c[None, :]
    allowed = (seq_r[:, None] == seq_c[None, :]) & (is_cache | is_new)[None, :] & (dist >= 0) & (dist <= WINDOW)
    return _band_bias(dist, allowed)


def _head_indicator(n_heads):
    ch = np.arange(n_heads * HEAD_DIM) // HEAD_DIM
    ind = (ch[:, None] == np.arange(LANES)[None, :]).astype(np.float32)
    return jnp.asarray(ind, BF16), jnp.asarray(ind.T, BF16)


def kernel(x_prompt, x_sample, state_conv, cache_k, cache_v, p_prompt, p_sample, norm_mix, norm_ffn, norm_ple,
           conv_w_in, conv_b_in, conv_w_dw, conv_b_dw, conv_ln_g, conv_ln_b, conv_w_out, conv_b_out, attn_w_qkv,
           attn_q_norm, attn_k_norm, attn_sinks, attn_w_o, moe_w_rg, moe_b_rg, moe_w_re, moe_b_re, moe_w_gate,
           moe_w_up, moe_w_down, ple_w_gate, ple_w_proj):
    bp, seq, d = x_prompt.shape
    bs, t_new, _ = x_sample.shape
    assert bp == 1 and d == D_MODEL and seq % WINDOW == 0 and bs % SAMPLE_SEQS == 0
    assert seq % (bs * t_new) == 0 and (bs * t_new) % DMA_CHUNK == 0
    depth = norm_mix.shape[0]
    row2 = lambda a: a.reshape(1, -1)

    y_p = x_prompt.reshape(seq, d)
    y_s = x_sample.reshape(bs * t_new, d)
    conv_p, conv_s, k_p, v_p, k_s, v_s = [], [], [], [], [], []

    for i in range(depth):
        j = i // 2
        g_mix = row2(norm_mix[i])
        if i % 2 == 0:
            w_in = conv_w_in[j].astype(BF16)
            b_in = row2(conv_b_in[j])
            tail = (conv_w_dw[j], row2(conv_b_dw[j]), row2(conv_ln_g[j]), row2(conv_ln_b[j]),
                    conv_w_out[j].astype(BF16), row2(conv_b_out[j]))
            u_p = _conv_in(y_p, g_mix, w_in, b_in)
            u_s = _conv_in(y_s, g_mix, w_in, b_in)
            conv_p.append(u_p[seq - CONV_STATE:].reshape(1, CONV_STATE, d))
            y_p = _conv_out_p(u_p, y_p, *tail)
            ys_t, new_state = _conv_out_s(state_conv[j], u_s.reshape(bs, t_new, d),
                                          y_s.reshape(bs, t_new, d).transpose(1, 0, 2), *tail)
            conv_s.append(new_state)
            y_s = ys_t.transpose(1, 0, 2).reshape(bs * t_new, d)
        else:
            w_qkv = attn_w_qkv[j].astype(BF16)
            w_o = attn_w_o[j].astype(BF16)
            iq, iqt = _head_indicator(N_HEADS)
            ik, ikt = _head_indicator(N_KV_HEADS)
            qg = row2(jnp.tile(attn_q_norm[j], N_HEADS))
            kg = row2(jnp.tile(attn_k_norm[j], N_KV_HEADS))
            sinks = attn_sinks[j]
            q1, k1, v1 = _qkv(y_p, g_mix, w_qkv, iq, iqt, ik, ikt, qg, kg)
            q2, k2, v2 = _qkv(y_s, g_mix, w_qkv, iq, iqt, ik, ikt, qg, kg)
            k_p.append(k1[seq - WINDOW:].reshape(1, WINDOW, N_KV_HEADS, HEAD_DIM))
            v_p.append(v1[seq - WINDOW:].reshape(1, WINDOW, N_KV_HEADS, HEAD_DIM))
            y_p = _attn_p(sinks, q1, k1, v1, y_p, _prompt_bias(), w_o)
            n_cols = -(-(SAMPLE_SEQS * (WINDOW + t_new)) // LANES) * LANES
            o_s, nk, nv = _attn_s(sinks, q2, k2, v2, cache_k[j].reshape(bs * WINDOW, KV_DIM),
                                  cache_v[j].reshape(bs * WINDOW, KV_DIM), _sample_bias(t_new, n_cols), t_new)
            k_s.append(nk.reshape(bs, WINDOW, N_KV_HEADS, HEAD_DIM))
            v_s.append(nv.reshape(bs, WINDOW, N_KV_HEADS, HEAD_DIM))
            y_s = _proj_res(o_s, y_s, w_o)

        w_r = jnp.zeros((d, ROUTER_LANES), F32)
        w_r = w_r.at[:, :N_EXPERT_GROUPS].set(moe_w_rg[i]).at[:, EXPERT_LANE0:EXPERT_LANE0 + N_EXPERTS].set(moe_w_re[i])
        b_r = jnp.zeros((1, ROUTER_LANES), F32)
        b_r = b_r.at[0, :N_EXPERT_GROUPS].set(moe_b_rg[i]).at[0, EXPERT_LANE0:EXPERT_LANE0 + N_EXPERTS].set(moe_b_re[i])
        moe = (row2(norm_ffn[i]), jnp.stack(_split_bf16(w_r)), b_r, moe_w_gate, moe_w_up, moe_w_down)
        ple = (row2(norm_ple[i]), ple_w_gate[i].astype(BF16), ple_w_proj[i].astype(BF16))
        y_p, y_s = _moe_ple(i, y_p, y_s, *moe, p_prompt.reshape(depth, seq, PLE_DIM),
                            p_sample[i].reshape(bs * t_new, PLE_DIM), *ple)

    return (y_p.reshape(1, seq, d), y_s.reshape(bs, t_new, d), jnp.stack(conv_p), jnp.stack(conv_s),
            jnp.stack(k_p), jnp.stack(v_p), jnp.stack(k_s), jnp.stack(v_s))
```

```python
import functools

import numpy as np

import jax
import jax.numpy as jnp
from jax import lax
from jax.experimental import pallas as pl
from jax.experimental.pallas import tpu as pltpu

F32 = jnp.float32
BF16 = jnp.bfloat16

D_MODEL = 1024
PLE_DIM = 256
CONV_WIDTH = 31
CONV_STATE = CONV_WIDTH - 1
N_HEADS = 16
N_KV_HEADS = 4
HEAD_DIM = 64
GROUP = N_HEADS // N_KV_HEADS
WINDOW = 128
KV_DIM = N_KV_HEADS * HEAD_DIM
N_EXPERT_GROUPS = 4
EXPERTS_PER_GROUP = 4
N_EXPERTS = 16
D_EXPERT = 256
EPS = 1e-6
NEG_INF = -1e30

LANES = 128
ROUTER_LANES = LANES
EXPERT_LANE0 = N_EXPERT_GROUPS
HALO = 32
CONV_COLS = 8
GLU_ROWS = 32
NORM_ROWS = 128
CONV_PITCH = D_MODEL // LANES + 1
SAMPLE_SEQS = 8
PAIRS_PER_GROUP = EXPERTS_PER_GROUP * (EXPERTS_PER_GROUP - 1) // 2
N_BUCKETS = N_EXPERT_GROUPS * PAIRS_PER_GROUP
TM = 256
PAY_WIDTH = D_MODEL + LANES
ROW_PITCH = PAY_WIDTH // LANES
DMA_CHUNK = 512
RING = 3
VMEM_LIMIT = 48 * 1024 * 1024


def _row_tile(n):
    return 512 if n % 512 == 0 else n


def _params(*sem):
    return pltpu.CompilerParams(dimension_semantics=sem, vmem_limit_bytes=VMEM_LIMIT)


def _full(shape):
    nd = len(shape)
    return pl.BlockSpec(shape, lambda *_: (0,) * nd)


def _rms(x, g):
    ms = jnp.mean(x * x, axis=-1, keepdims=True)
    return x * lax.rsqrt(ms + EPS) * g


def _sigmoid(x):
    return 1.0 / (1.0 + jnp.exp(-x))


def _mm(a, b):
    return jnp.dot(a, b, preferred_element_type=F32)


def _conv_in_kernel(x_ref, g_ref, w_ref, b_ref, u_ref):
    h = _rms(x_ref[...], g_ref[...]).astype(BF16)
    z = _mm(h, w_ref[...]) + b_ref[...]
    u_ref[...] = z[:, :D_MODEL] * _sigmoid(z[:, D_MODEL:])


def _conv_in(x, g, w, b):
    n = x.shape[0]
    t = _row_tile(n)
    return pl.pallas_call(
        _conv_in_kernel,
        grid=(n // t,),
        in_specs=[pl.BlockSpec((t, D_MODEL), lambda i: (i, 0)), _full(g.shape), _full(w.shape), _full(b.shape)],
        out_specs=pl.BlockSpec((t, D_MODEL), lambda i: (i, 0)),
        out_shape=jax.ShapeDtypeStruct((n, D_MODEL), F32),
        compiler_params=_params("arbitrary"),
        name="conv_in",
    )(x, g, w, b)


def _ln_silu(c, g, b):
    mu = jnp.mean(c, axis=-1, keepdims=True)
    xc = c - mu
    var = jnp.mean(xc * xc, axis=-1, keepdims=True)
    cn = xc * lax.rsqrt(var + EPS) * g + b
    return cn * _sigmoid(cn)


def _conv_p_kernel(n_tiles, xc_ref, xp_ref, g_ref, win_ref, bin_ref, wdw_ref, bdw_ref, lng_ref, lnb_ref, wout_ref,
                   bout_ref, y_ref, tail_ref, hn, zbuf, ubuf, cbuf, hbuf):
    t = xc_ref.shape[0]
    i = pl.program_id(0)
    nj = D_MODEL // LANES
    cur = lax.rem(i, 2)
    prv = 1 - cur
    n_col = win_ref.shape[0]
    col_w = win_ref.shape[2]
    rows_per = t // n_col

    @pl.when(i == 0)
    def _():
        ubuf[...] = jnp.zeros_like(ubuf)

    hn[...] = _rms(xc_ref[...], g_ref[...]).astype(BF16)

    def project_and_conv(c, carry):
        zbuf[c] = _mm(hn[...], win_ref[c])
        r0 = pl.multiple_of(c * rows_per, rows_per)
        for j in range(nj):
            lanes = slice(j * LANES, (j + 1) * LANES)
            accs = [None] * (rows_per // 8)
            for k in range(CONV_WIDTH):
                wk = wdw_ref[k:k + 1, lanes]
                for q in range(rows_per // 8):
                    r = r0 + (HALO - CONV_STATE + k + 8 * q)
                    term = wk * ubuf[prv, pl.ds(r * CONV_PITCH + j, 8, stride=CONV_PITCH), :]
                    accs[q] = term if k == 0 else accs[q] + term
            for q in range(rows_per // 8):
                cbuf[pl.ds(r0 + 8 * q, 8), lanes] = accs[q]
        return carry

    lax.fori_loop(0, n_col, project_and_conv, 0)

    ubuf[cur, 0:HALO * CONV_PITCH, :] = ubuf[prv, t * CONV_PITCH:(t + HALO) * CONV_PITCH, :]
    half = n_col // 2

    def glu(rr, carry):
        r = pl.multiple_of(rr * GLU_ROWS, GLU_ROWS)
        for c in range(half):
            a = zbuf[c, pl.ds(r, GLU_ROWS), :] + bin_ref[:, c * col_w:(c + 1) * col_w]
            gate = zbuf[c + half, pl.ds(r, GLU_ROWS), :] + bin_ref[:, D_MODEL + c * col_w:D_MODEL + (c + 1) * col_w]
            u = a * _sigmoid(gate)
            for q in range(GLU_ROWS // 8):
                for jj in range(col_w // LANES):
                    j = c * (col_w // LANES) + jj
                    dst = (r + HALO + 8 * q) * CONV_PITCH + j
                    ubuf[cur, pl.ds(dst, 8, stride=CONV_PITCH), :] = u[8 * q:8 * q + 8, jj * LANES:(jj + 1) * LANES]
        return carry

    lax.fori_loop(0, t // GLU_ROWS, glu, 0)

    @pl.when(i == n_tiles - 1)
    def _():
        for r in range(0, HALO, 8):
            for j in range(nj):
                src = (t + r) * CONV_PITCH + j
                tail_ref[r:r + 8, j * LANES:(j + 1) * LANES] = ubuf[cur, pl.ds(src, 8, stride=CONV_PITCH), :]

    def norm_chunk(rr, carry):
        r0 = pl.multiple_of(rr * NORM_ROWS, NORM_ROWS)
        c = cbuf[pl.ds(r0, NORM_ROWS), :] + bdw_ref[...]
        hbuf[pl.ds(r0, NORM_ROWS), :] = _ln_silu(c, lng_ref[...], lnb_ref[...]).astype(BF16)
        return carry

    lax.fori_loop(0, t // NORM_ROWS, norm_chunk, 0)

    @pl.when(i >= 1)
    def _():
        y_ref[...] = xp_ref[...] + _mm(hbuf[...], wout_ref[...]) + bout_ref[...]


def _conv_p(x, g, w_in, b_in, wdw, bdw, lng, lnb, wout, bout):
    n = x.shape[0]
    t = _row_tile(n)
    nt = n // t
    cur = lambda i: (jnp.minimum(i, nt - 1), 0)
    prev = lambda i: (jnp.maximum(i - 1, 0), 0)
    return pl.pallas_call(
        functools.partial(_conv_p_kernel, nt),
        grid=(nt + 1,),
        in_specs=[pl.BlockSpec((t, D_MODEL), cur), pl.BlockSpec((t, D_MODEL), prev)]
        + [_full(a.shape) for a in (g, w_in, b_in, wdw, bdw, lng, lnb, wout, bout)],
        out_specs=[pl.BlockSpec((t, D_MODEL), prev), _full((HALO, D_MODEL))],
        out_shape=[jax.ShapeDtypeStruct((n, D_MODEL), F32), jax.ShapeDtypeStruct((HALO, D_MODEL), F32)],
        scratch_shapes=[pltpu.VMEM((t, D_MODEL), BF16), pltpu.VMEM((w_in.shape[0], t, w_in.shape[2]), F32),
                        pltpu.VMEM((2, (t + HALO) * CONV_PITCH, LANES), F32), pltpu.VMEM((t, D_MODEL), F32),
                        pltpu.VMEM((t, D_MODEL), BF16)],
        compiler_params=_params("arbitrary"),
        name="conv_prompt",
    )(x, x, g, w_in, b_in, wdw, bdw, lng, lnb, wout, bout)


def _conv_out_s_kernel(st_ref, u_ref, x_ref, wdw_ref, bdw_ref, lng_ref, lnb_ref, wout_ref, bout_ref, y_ref, ns_ref,
                       wbuf, cbuf):
    bb, nt, _ = u_ref.shape
    win = CONV_STATE + nt
    nj = D_MODEL // LANES
    seq_pitch = win * CONV_PITCH

    def put(b, r0, rows, j, val):
        wbuf[pl.ds((b * win + r0) * CONV_PITCH + j, rows, stride=CONV_PITCH), :] = val

    for b in range(bb):
        for j in range(nj):
            lanes = slice(j * LANES, (j + 1) * LANES)
            for r0 in range(0, CONV_STATE, 8):
                rows = min(8, CONV_STATE - r0)
                put(b, r0, rows, j, st_ref[b, r0:r0 + rows, lanes])
            put(b, CONV_STATE, nt, j, u_ref[b, :, lanes])

    for b in range(bb):
        for r0 in range(0, CONV_STATE, 8):
            rows = min(8, CONV_STATE - r0)
            for j in range(nj):
                src = (b * win + nt + r0) * CONV_PITCH + j
                ns_ref[b, r0:r0 + rows, j * LANES:(j + 1) * LANES] = wbuf[pl.ds(src, rows, stride=CONV_PITCH), :]

    for t in range(nt):
        for b0 in range(0, bb, 8):
            for j in range(nj):
                lanes = slice(j * LANES, (j + 1) * LANES)
                acc = None
                for k in range(CONV_WIDTH):
                    start = (b0 * win + t + k) * CONV_PITCH + j
                    term = wdw_ref[k:k + 1, lanes] * wbuf[pl.ds(start, 8, stride=seq_pitch), :]
                    acc = term if acc is None else acc + term
                cbuf[t * bb + b0:t * bb + b0 + 8, lanes] = acc
    c = cbuf[...] + bdw_ref[...]
    h = _ln_silu(c, lng_ref[...], lnb_ref[...]).astype(BF16)
    y = _mm(h, wout_ref[...]) + bout_ref[...]
    for t in range(nt):
        y_ref[t] = x_ref[t] + y[t * bb:(t + 1) * bb]


def _conv_out_s(state, u, x, wdw, bdw, lng, lnb, wout, bout):
    b, nt, _ = u.shape
    bb = 16 if b % 16 == 0 else b
    assert bb % 8 == 0
    win = CONV_STATE + nt
    seq = lambda i: (i, 0, 0)
    tm = lambda i: (0, i, 0)
    return pl.pallas_call(
        _conv_out_s_kernel,
        grid=(b // bb,),
        in_specs=[pl.BlockSpec((bb, CONV_STATE, D_MODEL), seq), pl.BlockSpec((bb, nt, D_MODEL), seq),
                  pl.BlockSpec((nt, bb, D_MODEL), tm),
                  _full(wdw.shape), _full(bdw.shape), _full(lng.shape), _full(lnb.shape),
                  _full(wout.shape), _full(bout.shape)],
        out_specs=[pl.BlockSpec((nt, bb, D_MODEL), tm), pl.BlockSpec((bb, CONV_STATE, D_MODEL), seq)],
        out_shape=[jax.ShapeDtypeStruct((nt, b, D_MODEL), F32), jax.ShapeDtypeStruct((b, CONV_STATE, D_MODEL), F32)],
        scratch_shapes=[pltpu.VMEM((bb * win * CONV_PITCH, LANES), F32), pltpu.VMEM((nt * bb, D_MODEL), F32)],
        compiler_params=_params("arbitrary"),
        name="conv_out_sample",
    )(state, u, x, wdw, bdw, lng, lnb, wout, bout)


def _route(lg):
    big = 3.0e38
    lane = lax.broadcasted_iota(jnp.int32, lg.shape, 1)
    lanef = lane.astype(F32)
    is_g = lane < N_EXPERT_GROUPS
    gl = jnp.where(is_g, lg, -big)
    gmax = jnp.max(gl, axis=1, keepdims=True)
    gsum = jnp.sum(jnp.where(is_g, jnp.exp(gl - gmax), 0.0), axis=1, keepdims=True)
    g_w = 1.0 / gsum
    g_idx = jnp.min(jnp.where(gl == gmax, lanef, big), axis=1, keepdims=True)
    rel = lanef - float(EXPERT_LANE0) - g_idx * float(EXPERTS_PER_GROUP)
    in_grp = jnp.where(rel >= 0.0, jnp.where(rel < float(EXPERTS_PER_GROUP), 1.0, 0.0), 0.0) > 0.5
    el = jnp.where(in_grp, lg, -big)
    e1 = jnp.max(el, axis=1, keepdims=True)
    i1 = jnp.min(jnp.where(el == e1, lanef, big), axis=1, keepdims=True)
    el2 = jnp.where(lanef == i1, -big, el)
    e2 = jnp.max(el2, axis=1, keepdims=True)
    i2 = jnp.min(jnp.where(el2 == e2, lanef, big), axis=1, keepdims=True)
    tt = jnp.exp(e2 - e1)
    w1 = g_w / (1.0 + tt)
    w2 = g_w * tt / (1.0 + tt)
    base = float(EXPERT_LANE0) + g_idx * float(EXPERTS_PER_GROUP)
    a = jnp.minimum(i1, i2) - base
    b = jnp.maximum(i1, i2) - base
    pair = a * (7.0 - a) * 0.5 + (b - a - 1.0)
    first_is_lo = i1 < i2
    return (g_idx * float(PAIRS_PER_GROUP) + pair, jnp.where(first_is_lo, w1, w2), jnp.where(first_is_lo, w2, w1))


def _to_row_linear(dst_ref, src_ref, n_tiles):
    def body(g, carry):
        r = pl.multiple_of(g * 8, 8)
        for j in range(n_tiles):
            dst_ref[pl.ds(r * ROW_PITCH + j, 8, stride=ROW_PITCH), :] = src_ref[pl.ds(r, 8), j * LANES:(j + 1) * LANES]
        return carry

    lax.fori_loop(0, src_ref.shape[0] // 8, body, 0)


def _from_row_linear(dst_ref, src_ref, n_tiles, pitch):
    def body(g, carry):
        r = pl.multiple_of(g * 8, 8)
        for j in range(n_tiles):
            dst_ref[pl.ds(r, 8), j * LANES:(j + 1) * LANES] = src_ref[pl.ds(r * pitch + j, 8, stride=pitch), :]
        return carry

    lax.fori_loop(0, dst_ref.shape[0] // 8, body, 0)


def _pick(i, n_prompt_tiles, prompt_ref, sample_ref):
    return jnp.where(i < n_prompt_tiles, prompt_ref[...], sample_ref[...])


def _route_kernel(n_prompt_tiles, yp_ref, ys_ref, g_ref, wr_ref, br_ref, tri_ref, pay_ref, meta_ref, cnt_ref, carry,
                  rowbuf):
    i = pl.program_id(0)

    @pl.when(i == 0)
    def _():
        carry[...] = jnp.zeros_like(carry)

    xf = _rms(_pick(i, n_prompt_tiles, yp_ref, ys_ref), g_ref[...])
    x_hi, x_lo = _split_bf16(xf)
    logits = _mm(x_hi, wr_ref[0]) + (_mm(x_lo, wr_ref[0]) + _mm(x_hi, wr_ref[1])) + br_ref[...]
    bucket, w_lo, w_hi = _route(logits)
    lane = lax.broadcasted_iota(jnp.int32, logits.shape, 1)
    onehot = jnp.where(lane.astype(F32) == bucket, 1.0, 0.0)
    before = _mm(tri_ref[...], onehot.astype(BF16)) + carry[...]
    rank = jnp.sum(onehot * before, axis=1, keepdims=True)
    carry[...] += jnp.sum(onehot, axis=0, keepdims=True)
    cnt_ref[...] = carry[...]
    meta = jnp.where(lane == 0, bucket, jnp.where(lane == 1, rank, 0.0))
    meta_ref[...] = jnp.transpose(meta)[:8, :].astype(jnp.int32)

    rowbuf[:, :D_MODEL] = xf
    rowbuf[:, D_MODEL:] = jnp.where(lane == 0, w_lo, jnp.where(lane == 1, w_hi, 0.0))
    _to_row_linear(pay_ref, rowbuf, ROW_PITCH)


def _route_call(y_p, y_s, g, wr, br, tri):
    t = y_s.shape[0]
    npt = y_p.shape[0] // t
    n = y_p.shape[0] + t
    pidx = lambda i: (jnp.minimum(i, npt - 1), 0)
    return pl.pallas_call(
        functools.partial(_route_kernel, npt),
        grid=(npt + 1,),
        in_specs=[pl.BlockSpec((t, D_MODEL), pidx), _full(y_s.shape), _full(g.shape), _full(wr.shape),
                  _full(br.shape), _full(tri.shape)],
        out_specs=[pl.BlockSpec((t * ROW_PITCH, LANES), lambda i: (i, 0)),
                   pl.BlockSpec((8, t), lambda i: (0, i)), _full((1, ROUTER_LANES))],
        out_shape=[jax.ShapeDtypeStruct((n * ROW_PITCH, LANES), F32),
                   jax.ShapeDtypeStruct((8, n), jnp.int32), jax.ShapeDtypeStruct((1, ROUTER_LANES), F32)],
        scratch_shapes=[pltpu.VMEM((1, ROUTER_LANES), F32), pltpu.VMEM((t, PAY_WIDTH), F32)],
        compiler_params=_params("arbitrary"),
        name="route",
    )(y_p, y_s, g, wr, br, tri)


def _permute_kernel(pos_ref, zstart_ref, nz_ref, used_ref, src_ref, dst_ref, zbuf, ring, zsem, in_sems, out_sems):
    zbuf[...] = jnp.zeros_like(zbuf)

    def zero_tile(start):
        return pltpu.make_async_copy(zbuf, dst_ref.at[pl.ds(pl.multiple_of(start * ROW_PITCH, 8), TM * ROW_PITCH)], zsem)

    for b in range(N_BUCKETS):
        @pl.when(nz_ref[b] > 0)
        def _():
            zero_tile(zstart_ref[b]).start()
    n_tiles = dst_ref.shape[0] // (TM * ROW_PITCH)

    def start_unused(i, carry):
        zero_tile(i * TM).start()
        return carry

    lax.fori_loop(used_ref[0], n_tiles, start_unused, 0)
    for b in range(N_BUCKETS):
        @pl.when(nz_ref[b] > 0)
        def _():
            zero_tile(0).wait()

    def wait_unused(i, carry):
        zero_tile(0).wait()
        return carry

    lax.fori_loop(used_ref[0], n_tiles, wait_unused, 0)

    chunk_rows = DMA_CHUNK * ROW_PITCH
    n_chunks = src_ref.shape[0] // chunk_rows

    def fetch(c, slot):
        return pltpu.make_async_copy(src_ref.at[pl.ds(pl.multiple_of(c * chunk_rows, 8), chunk_rows)], ring.at[slot],
                                     in_sems.at[slot])

    def drain(slot):
        pltpu.make_async_copy(ring.at[slot], dst_ref.at[pl.ds(0, chunk_rows)], out_sems.at[slot]).wait()

    fetch(0, 0).start()

    def step(c, carry):
        slot = lax.rem(c, RING)
        nxt = lax.rem(c + 1, RING)
        fetch(c, slot).wait()

        @pl.when(c >= RING - 1)
        def _():
            drain(nxt)

        @pl.when(c + 1 < n_chunks)
        def _():
            fetch(c + 1, nxt).start()

        def send(jj, carry2):
            row = pos_ref[c * DMA_CHUNK + jj] * ROW_PITCH
            pltpu.make_async_copy(ring.at[slot, pl.ds(jj * ROW_PITCH, ROW_PITCH)], dst_ref.at[pl.ds(row, ROW_PITCH)],
                                  out_sems.at[slot]).start()
            return carry2

        lax.fori_loop(0, DMA_CHUNK, send, 0, unroll=8)
        return carry

    lax.fori_loop(0, n_chunks, step, 0)
    for c in range(n_chunks - (RING - 1), n_chunks):
        drain(c % RING)


def _permute(pos, zstart, nz, used, payload, n_rows_out):
    assert payload.shape[0] // (DMA_CHUNK * ROW_PITCH) >= RING
    return pl.pallas_call(
        _permute_kernel,
        grid_spec=pltpu.PrefetchScalarGridSpec(
            num_scalar_prefetch=4, grid=(1,),
            in_specs=[pl.BlockSpec(memory_space=pl.ANY)],
            out_specs=pl.BlockSpec(memory_space=pl.ANY),
            scratch_shapes=[pltpu.VMEM((TM * ROW_PITCH, LANES), payload.dtype),
                            pltpu.VMEM((RING, DMA_CHUNK * ROW_PITCH, LANES), payload.dtype),
                            pltpu.SemaphoreType.DMA(()), pltpu.SemaphoreType.DMA((RING,)),
                            pltpu.SemaphoreType.DMA((RING,))]),
        out_shape=jax.ShapeDtypeStruct((n_rows_out * ROW_PITCH, LANES), payload.dtype),
        compiler_params=_params("arbitrary"),
        name="permute",
    )(pos, zstart, nz, used, payload)


def _silu(x):
    return x * _sigmoid(x)


def _experts_kernel(tix_ref, lo_ref, hi_ref, valid_ref, fresh_ref, xs_ref, wgl_ref, wgh_ref, wul_ref, wuh_ref,
                    wdl_ref, wdh_ref, ys_ref, wup_s, wdn_s, xbuf, ybuf):
    i = pl.program_id(0)

    @pl.when(fresh_ref[i] > 0)
    def _():
        for s, ref in enumerate((wgl_ref, wgh_ref, wul_ref, wuh_ref)):
            wup_s[s] = ref[0, 0].astype(BF16)
        for s, ref in enumerate((wdl_ref, wdh_ref)):
            wdn_s[s] = ref[0, 0].astype(BF16)

    @pl.when(valid_ref[i] == 0)
    def _():
        ys_ref[...] = jnp.zeros_like(ys_ref)

    @pl.when(valid_ref[i] > 0)
    def _():
        _from_row_linear(xbuf, xs_ref, ROW_PITCH, ROW_PITCH)
        x = xbuf[:, :D_MODEL].astype(BF16)
        gates = xbuf[:, D_MODEL:]
        h_lo = _silu(_mm(x, wup_s[0])) * _mm(x, wup_s[2]) * gates[:, 0:1]
        h_hi = _silu(_mm(x, wup_s[1])) * _mm(x, wup_s[3]) * gates[:, 1:2]
        ybuf[:, :D_MODEL] = _mm(h_lo.astype(BF16), wdn_s[0]) + _mm(h_hi.astype(BF16), wdn_s[1])
        ybuf[:, D_MODEL:] = jnp.zeros((TM, LANES), F32)
        _to_row_linear(ys_ref, ybuf, ROW_PITCH)


def _experts(layer, tix, lo, hi, valid, fresh, xs, wg, wu, wd):
    n_tiles = xs.shape[0] // (TM * ROW_PITCH)
    row = lambda i, tix, lo, hi, valid, fresh: (tix[i], 0)
    e_lo = lambda i, tix, lo, hi, valid, fresh: (layer, lo[i], 0, 0)
    e_hi = lambda i, tix, lo, hi, valid, fresh: (layer, hi[i], 0, 0)
    up = pl.BlockSpec((1, 1, D_MODEL, D_EXPERT), e_lo), pl.BlockSpec((1, 1, D_MODEL, D_EXPERT), e_hi)
    down = pl.BlockSpec((1, 1, D_EXPERT, D_MODEL), e_lo), pl.BlockSpec((1, 1, D_EXPERT, D_MODEL), e_hi)
    return pl.pallas_call(
        _experts_kernel,
        grid_spec=pltpu.PrefetchScalarGridSpec(
            num_scalar_prefetch=5, grid=(n_tiles,),
            in_specs=[pl.BlockSpec((TM * ROW_PITCH, LANES), row), *up, *up, *down],
            out_specs=pl.BlockSpec((TM * ROW_PITCH, LANES), row),
            scratch_shapes=[pltpu.VMEM((4, D_MODEL, D_EXPERT), BF16), pltpu.VMEM((2, D_EXPERT, D_MODEL), BF16),
                            pltpu.VMEM((TM, PAY_WIDTH), F32), pltpu.VMEM((TM, PAY_WIDTH), F32)]),
        out_shape=jax.ShapeDtypeStruct(xs.shape, F32),
        compiler_params=_params("arbitrary"),
        name="experts",
    )(tix, lo, hi, valid, fresh, xs, wg, wg, wu, wu, wd, wd)


def _ple_kernel(n_prompt_tiles, pos_ref, yp_ref, ys_ref, moe_ref, pp_ref, ps_ref, g_ref, wg_ref, wp_ref, op_ref, os_ref,
                mring, mbuf, sems):
    i = pl.program_id(0)
    t = mbuf.shape[0]
    w = D_MODEL // LANES

    def gather(tile, slot):
        def body(jj, carry):
            row = pos_ref[tile * t + jj] * ROW_PITCH
            pltpu.make_async_copy(moe_ref.at[pl.ds(row, w)], mring.at[slot, pl.ds(jj * w, w)], sems.at[slot]).start()
            return carry

        lax.fori_loop(0, t, body, 0, unroll=8)

    @pl.when(i == 0)
    def _():
        gather(0, 0)

    @pl.when(i < n_prompt_tiles)
    def _():
        gather(i + 1, lax.rem(i + 1, 2))

    slot = lax.rem(i, 2)
    pltpu.make_async_copy(moe_ref.at[pl.ds(0, t * w)], mring.at[slot], sems.at[slot]).wait()
    _from_row_linear(mbuf, mring.at[slot], w, w)
    y2 = _pick(i, n_prompt_tiles, yp_ref, ys_ref) + mbuf[...]
    hn = _rms(y2, g_ref[...]).astype(BF16)
    gt = _sigmoid(_mm(hn, wg_ref[...]))
    pr = _mm(jnp.where(i < n_prompt_tiles, pp_ref[0], ps_ref[...]).astype(BF16), wp_ref[...])
    y3 = y2 + gt * pr

    @pl.when(i < n_prompt_tiles)
    def _():
        op_ref[...] = y3

    @pl.when(i >= n_prompt_tiles)
    def _():
        os_ref[...] = y3


def _ple(layer, pos, y_p, y_s, ys_sorted, p_p, p_s, g, wg, wp):
    t = y_s.shape[0]
    npt = y_p.shape[0] // t
    w = D_MODEL // LANES
    pidx = lambda i, pos: (jnp.minimum(i, npt - 1), 0)
    full = lambda shape: pl.BlockSpec(shape, lambda i, pos: (0,) * len(shape))
    return pl.pallas_call(
        functools.partial(_ple_kernel, npt),
        grid_spec=pltpu.PrefetchScalarGridSpec(
            num_scalar_prefetch=1, grid=(npt + 1,),
            in_specs=[pl.BlockSpec((t, D_MODEL), pidx), full(y_s.shape), pl.BlockSpec(memory_space=pl.ANY),
                      pl.BlockSpec((1, t, PLE_DIM), lambda i, pos: (layer, jnp.minimum(i, npt - 1), 0)),
                      full(p_s.shape), full(g.shape), full(wg.shape), full(wp.shape)],
            out_specs=[pl.BlockSpec((t, D_MODEL), pidx), full(y_s.shape)],
            scratch_shapes=[pltpu.VMEM((2, t * w, LANES), F32), pltpu.VMEM((t, D_MODEL), F32),
                            pltpu.SemaphoreType.DMA((2,))]),
        out_shape=[jax.ShapeDtypeStruct(y_p.shape, F32), jax.ShapeDtypeStruct(y_s.shape, F32)],
        compiler_params=_params("arbitrary"),
        name="ple",
    )(pos, y_p, y_s, ys_sorted, p_p, p_s, g, wg, wp)


def _bucket_experts():
    pairs = [(a, b) for a in range(EXPERTS_PER_GROUP) for b in range(a + 1, EXPERTS_PER_GROUP)]
    lo = [g * EXPERTS_PER_GROUP + a for g in range(N_EXPERT_GROUPS) for a, _ in pairs]
    hi = [g * EXPERTS_PER_GROUP + b for g in range(N_EXPERT_GROUPS) for _, b in pairs]
    return jnp.array(lo, jnp.int32), jnp.array(hi, jnp.int32)


def _moe_ple(layer, y_p, y_s, gffn, wr, br, wg, wu, wd, p_p, p_s, gple, pwg, pwp):
    t = y_s.shape[0]
    n = y_p.shape[0] + t
    tri = jnp.asarray(np.tri(t, t, -1, dtype=np.float32), BF16)
    payload, meta, counts = _route_call(y_p, y_s, gffn, wr, br, tri)

    bucket, rank = meta[0], meta[1]
    cnt = counts[0, :N_BUCKETS].astype(jnp.int32)
    padded = (cnt + TM - 1) // TM * TM
    ends = jnp.cumsum(padded)
    starts = ends - padded
    in_bucket = bucket[:, None] == jnp.arange(N_BUCKETS, dtype=jnp.int32)[None, :]
    pos = rank + jnp.sum(jnp.where(in_bucket, starts[None, :], 0), axis=1)
    n_rows = n + N_BUCKETS * TM
    n_tiles = n_rows // TM
    tile_start = jnp.arange(n_tiles, dtype=jnp.int32) * TM
    valid = tile_start < ends[-1]
    used = (ends[-1] // TM).astype(jnp.int32)
    tix = jnp.arange(n_tiles, dtype=jnp.int32)
    tile_bucket = jnp.sum((ends[None, :] <= (jnp.minimum(tix, used - 1) * TM)[:, None]).astype(jnp.int32), axis=1)
    lo_tab, hi_tab = _bucket_experts()
    lo = jnp.take(lo_tab, tile_bucket)
    hi = jnp.take(hi_tab, tile_bucket)
    prev_bucket = jnp.concatenate([jnp.full((1,), -1, jnp.int32), tile_bucket[:-1]])
    fresh = valid & (tile_bucket != prev_bucket)

    xs = _permute(pos, (ends - TM).astype(jnp.int32), (cnt > 0).astype(jnp.int32), used.reshape(1), payload, n_rows)
    ys = _experts(layer, tix, lo, hi, valid.astype(jnp.int32), fresh.astype(jnp.int32), xs, wg, wu, wd)
    return _ple(layer, pos, y_p, y_s, ys, p_p, p_s, gple, pwg, pwp)


def _split_bf16(a):
    hi = a.astype(BF16)
    lo = (a - hi.astype(F32)).astype(BF16)
    return hi, lo


def _head_norm(a, ind_ref, indt_ref, gain):
    hi, lo = _split_bf16(a * a)
    ss = _mm(hi, ind_ref[...]) + _mm(lo, ind_ref[...])
    inv = lax.rsqrt(ss * (1.0 / HEAD_DIM) + EPS)
    ihi, ilo = _split_bf16(inv)
    invb = _mm(ihi, indt_ref[...]) + _mm(ilo, indt_ref[...])
    return a * invb * gain


def _qkv_kernel(x_ref, g_ref, w_ref, iq_ref, iqt_ref, ik_ref, ikt_ref, qg_ref, kg_ref, q_ref, k_ref, v_ref):
    h = _rms(x_ref[...], g_ref[...]).astype(BF16)
    qkv = _mm(h, w_ref[...])
    nq = N_HEADS * HEAD_DIM
    q = _head_norm(qkv[:, :nq], iq_ref, iqt_ref, qg_ref[...])
    k = _head_norm(qkv[:, nq:nq + KV_DIM], ik_ref, ikt_ref, kg_ref[...])
    q_ref[...] = (q * (HEAD_DIM ** -0.5)).astype(BF16)
    k_ref[...] = k
    v_ref[...] = qkv[:, nq + KV_DIM:]


def _qkv(x, g, w, iq, iqt, ik, ikt, qg, kg):
    n = x.shape[0]
    t = _row_tile(n)
    row = lambda i: (i, 0)
    return pl.pallas_call(
        _qkv_kernel,
        grid=(n // t,),
        in_specs=[pl.BlockSpec((t, D_MODEL), row)] + [_full(a.shape) for a in (g, w, iq, iqt, ik, ikt, qg, kg)],
        out_specs=[pl.BlockSpec((t, N_HEADS * HEAD_DIM), row), pl.BlockSpec((t, KV_DIM), row),
                   pl.BlockSpec((t, KV_DIM), row)],
        out_shape=[jax.ShapeDtypeStruct((n, N_HEADS * HEAD_DIM), BF16), jax.ShapeDtypeStruct((n, KV_DIM), F32),
                   jax.ShapeDtypeStruct((n, KV_DIM), F32)],
        compiler_params=_params("arbitrary"),
        name="qkv",
    )(x, g, w, iq, iqt, ik, ikt, qg, kg)


def _dup_heads(a):
    out = []
    for s in range(KV_DIM // LANES):
        sl = a[:, s * LANES:(s + 1) * LANES]
        sw = pltpu.roll(sl, HEAD_DIM, axis=1)
        low = lax.broadcasted_iota(jnp.int32, sl.shape, 1) < HEAD_DIM
        out.append(jnp.where(low, sl, sw))
        out.append(jnp.where(low, sw, sl))
    return jnp.concatenate(out, axis=1).astype(BF16)


def _attend(q_rows, k2, v2, bias_of, sink_of, extra_mask):
    m_rows = q_rows.shape[0]
    low_q = lax.broadcasted_iota(jnp.int32, (m_rows, LANES), 1) < HEAD_DIM
    low_k = lax.broadcasted_iota(jnp.int32, (k2.shape[0], LANES), 1) < HEAD_DIM
    zero_q = jnp.zeros((m_rows, LANES), BF16)
    zero_k = jnp.zeros((k2.shape[0], LANES), BF16)
    slabs = []
    for g in range(N_KV_HEADS):
        kg = k2[:, g * LANES:(g + 1) * LANES]
        vg = v2[:, g * LANES:(g + 1) * LANES]
        v_lo = jnp.where(low_k, vg, zero_k)
        v_hi = jnp.where(low_k, zero_k, vg)
        lhs = []
        for a in range(GROUP):
            h = g * GROUP + a
            qs = q_rows[:, (h // 2) * LANES:(h // 2 + 1) * LANES]
            lhs.append(jnp.where(low_q, qs, zero_q) if h % 2 == 0 else jnp.where(low_q, zero_q, qs))
        s = lax.dot_general(jnp.concatenate(lhs, axis=0), kg, (((1,), (1,)), ((), ())), preferred_element_type=F32)
        probs, rinv = [], []
        for a in range(GROUP):
            h = g * GROUP + a
            sa = s[a * m_rows:(a + 1) * m_rows] + bias_of(h)
            if extra_mask is not None:
                sa = jnp.where(extra_mask, NEG_INF, sa)
            sink = sink_of(h)
            m = jnp.maximum(jnp.max(sa, axis=1, keepdims=True), sink)
            p = jnp.exp(sa - m)
            den = jnp.sum(p, axis=1, keepdims=True) + jnp.exp(sink - m)
            probs.append(p.astype(BF16))
            rinv.append(1.0 / den)
        for sp in range(GROUP // 2):
            o = _mm(probs[2 * sp], v_lo) + _mm(probs[2 * sp + 1], v_hi)
            slabs.append(o * jnp.where(low_q, rinv[2 * sp], rinv[2 * sp + 1]))
    return jnp.concatenate(slabs, axis=1)


def _attn_p_kernel(sink_ref, q_ref, k_ref, v_ref, x_ref, bias_ref, wo_ref, y_ref, kbuf, vbuf, obuf):
    t = q_ref.shape[0]
    i = pl.program_id(0)

    @pl.when(i == 0)
    def _():
        kbuf[0:WINDOW, :] = jnp.zeros((WINDOW, 2 * KV_DIM), BF16)
        vbuf[0:WINDOW, :] = jnp.zeros((WINDOW, 2 * KV_DIM), BF16)

    @pl.when(i > 0)
    def _():
        kbuf[0:WINDOW, :] = kbuf[t:t + WINDOW, :]
        vbuf[0:WINDOW, :] = vbuf[t:t + WINDOW, :]

    kbuf[WINDOW:, :] = _dup_heads(k_ref[...])
    vbuf[WINDOW:, :] = _dup_heads(v_ref[...])
    col = lax.broadcasted_iota(jnp.int32, (WINDOW, 2 * WINDOW), 1)
    for j in range(t // WINDOW):
        rows = slice(j * WINDOW, (j + 1) * WINDOW)
        keys = slice(j * WINDOW, (j + 2) * WINDOW)
        extra = jnp.logical_and(i == 0, col < WINDOW) if j == 0 else None
        o = _attend(q_ref[rows, :], kbuf[keys, :], vbuf[keys, :], lambda h: bias_ref[h], lambda h: sink_ref[h], extra)
        obuf[rows, :] = o.astype(BF16)
    y_ref[...] = x_ref[...] + _mm(obuf[...], wo_ref[...])


def _attn_p(sinks, q, k, v, x, bias, wo):
    n = q.shape[0]
    t = _row_tile(n)
    row = lambda i: (i, 0)
    return pl.pallas_call(
        _attn_p_kernel,
        grid=(n // t,),
        in_specs=[pl.BlockSpec(memory_space=pltpu.SMEM),
                  pl.BlockSpec((t, N_HEADS * HEAD_DIM), row), pl.BlockSpec((t, KV_DIM), row),
                  pl.BlockSpec((t, KV_DIM), row), pl.BlockSpec((t, D_MODEL), row),
                  _full(bias.shape), _full(wo.shape)],
        out_specs=pl.BlockSpec((t, D_MODEL), row),
        out_shape=jax.ShapeDtypeStruct((n, D_MODEL), F32),
        scratch_shapes=[pltpu.VMEM((t + WINDOW, 2 * KV_DIM), BF16), pltpu.VMEM((t + WINDOW, 2 * KV_DIM), BF16),
                        pltpu.VMEM((t, N_HEADS * HEAD_DIM), BF16)],
        compiler_params=_params("arbitrary"),
        name="attn_prompt",
    )(sinks, q, k, v, x, bias, wo)


def _attn_s_kernel(sink_ref, q_ref, k_ref, v_ref, ck_ref, cv_ref, bias_ref, o_ref, nk_ref, nv_ref):
    pad = bias_ref.shape[2] - ck_ref.shape[0] - k_ref.shape[0]
    zpad = jnp.zeros((pad, KV_DIM), F32)
    k2 = _dup_heads(jnp.concatenate([ck_ref[...], k_ref[...], zpad], axis=0))
    v2 = _dup_heads(jnp.concatenate([cv_ref[...], v_ref[...], zpad], axis=0))
    o = _attend(q_ref[...], k2, v2, lambda h: bias_ref[h], lambda h: sink_ref[h], None)
    o_ref[...] = o.astype(BF16)
    t_new = k_ref.shape[0] // SAMPLE_SEQS
    for new_ref, old_ref, add_ref in ((nk_ref, ck_ref, k_ref), (nv_ref, cv_ref, v_ref)):
        for b in range(SAMPLE_SEQS):
            r = b * WINDOW
            new_ref[r:r + WINDOW - t_new, :] = old_ref[r + t_new:r + WINDOW, :]
            new_ref[r + WINDOW - t_new:r + WINDOW, :] = add_ref[b * t_new:(b + 1) * t_new, :]


def _attn_s(sinks, q, k, v, ck, cv, bias, t_new):
    n = q.shape[0]
    rows = SAMPLE_SEQS * t_new
    crow = SAMPLE_SEQS * WINDOW
    row = lambda i: (i, 0)
    return pl.pallas_call(
        _attn_s_kernel,
        grid=(n // rows,),
        in_specs=[pl.BlockSpec(memory_space=pltpu.SMEM),
                  pl.BlockSpec((rows, N_HEADS * HEAD_DIM), row), pl.BlockSpec((rows, KV_DIM), row),
                  pl.BlockSpec((rows, KV_DIM), row), pl.BlockSpec((crow, KV_DIM), row),
                  pl.BlockSpec((crow, KV_DIM), row), _full(bias.shape)],
        out_specs=[pl.BlockSpec((rows, N_HEADS * HEAD_DIM), row), pl.BlockSpec((crow, KV_DIM), row),
                   pl.BlockSpec((crow, KV_DIM), row)],
        out_shape=[jax.ShapeDtypeStruct((n, N_HEADS * HEAD_DIM), BF16), jax.ShapeDtypeStruct(ck.shape, F32),
                   jax.ShapeDtypeStruct(cv.shape, F32)],
        compiler_params=_params("arbitrary"),
        name="attn_sample",
    )(sinks, q, k, v, ck, cv, bias)


def _proj_res_kernel(o_ref, x_ref, w_ref, y_ref):
    y_ref[...] = x_ref[...] + _mm(o_ref[...], w_ref[...])


def _proj_res(o, x, w):
    n = o.shape[0]
    t = _row_tile(n)
    row = lambda i: (i, 0)
    return pl.pallas_call(
        _proj_res_kernel,
        grid=(n // t,),
        in_specs=[pl.BlockSpec((t, o.shape[1]), row), pl.BlockSpec((t, D_MODEL), row), _full(w.shape)],
        out_specs=pl.BlockSpec((t, D_MODEL), row),
        out_shape=jax.ShapeDtypeStruct((n, D_MODEL), F32),
        compiler_params=_params("arbitrary"),
        name="proj_res",
    )(o, x, w)


def _alibi_slopes():
    return np.exp2(-8.0 * np.arange(1, N_HEADS + 1, dtype=np.float64) / N_HEADS).astype(np.float32)


def _band_bias(dist, allowed):
    b = -(_alibi_slopes()[:, None, None] * dist.astype(np.float32)[None])
    return jnp.asarray(np.where(allowed[None], b, np.float32(NEG_INF)).astype(np.float32))


def _prompt_bias():
    dist = WINDOW + np.arange(WINDOW)[:, None] - np.arange(2 * WINDOW)[None, :]
    return _band_bias(dist, (dist >= 0) & (dist <= WINDOW))


def _sample_bias(t_new, n_cols):
    c = np.arange(n_cols)
    n_cache = SAMPLE_SEQS * WINDOW
    n_new = SAMPLE_SEQS * t_new
    is_cache = c < n_cache
    is_new = (c >= n_cache) & (c < n_cache + n_new)
    seq_c = np.where(is_cache, c // WINDOW, (c - n_cache) // t_new)
    pos_c = np.where(is_cache, c % WINDOW, WINDOW + (c - n_cache) % t_new)
    r = np.arange(n_new)
    seq_r, tok_r = r // t_new, r % t_new
    dist = WINDOW + tok_r[:, None] - pos_c[None, :]
    allowed = (seq_r[:, None] == seq_c[None, :]) & (is_cache | is_new)[None, :] & (dist >= 0) & (dist <= WINDOW)
    return _band_bias(dist, allowed)


def _head_indicator(n_heads):
    ch = np.arange(n_heads * HEAD_DIM) // HEAD_DIM
    ind = (ch[:, None] == np.arange(LANES)[None, :]).astype(np.float32)
    return jnp.asarray(ind, BF16), jnp.asarray(ind.T, BF16)


def kernel(x_prompt, x_sample, state_conv, cache_k, cache_v, p_prompt, p_sample, norm_mix, norm_ffn, norm_ple,
           conv_w_in, conv_b_in, conv_w_dw, conv_b_dw, conv_ln_g, conv_ln_b, conv_w_out, conv_b_out, attn_w_qkv,
           attn_q_norm, attn_k_norm, attn_sinks, attn_w_o, moe_w_rg, moe_b_rg, moe_w_re, moe_b_re, moe_w_gate,
           moe_w_up, moe_w_down, ple_w_gate, ple_w_proj):
    bp, seq, d = x_prompt.shape
    bs, t_new, _ = x_sample.shape
    assert bp == 1 and d == D_MODEL and seq % WINDOW == 0 and bs % SAMPLE_SEQS == 0
    assert seq % (bs * t_new) == 0 and (bs * t_new) % DMA_CHUNK == 0
    depth = norm_mix.shape[0]
    row2 = lambda a: a.reshape(1, -1)

    y_p = x_prompt.reshape(seq, d)
    y_s = x_sample.reshape(bs * t_new, d)
    conv_p, conv_s, k_p, v_p, k_s, v_s = [], [], [], [], [], []

    for i in range(depth):
        j = i // 2
        g_mix = row2(norm_mix[i])
        if i % 2 == 0:
            w_in = conv_w_in[j].astype(BF16)
            b_in = row2(conv_b_in[j])
            tail = (conv_w_dw[j], row2(conv_b_dw[j]), row2(conv_ln_g[j]), row2(conv_ln_b[j]),
                    conv_w_out[j].astype(BF16), row2(conv_b_out[j]))
            w_in_cols = w_in.reshape(d, CONV_COLS, 2 * d // CONV_COLS).transpose(1, 0, 2)
            u_s = _conv_in(y_s, g_mix, w_in, b_in)
            y_p, u_tail = _conv_p(y_p, g_mix, w_in_cols, b_in, *tail)
            conv_p.append(u_tail[HALO - CONV_STATE:].reshape(1, CONV_STATE, d))
            ys_t, new_state = _conv_out_s(state_conv[j], u_s.reshape(bs, t_new, d),
                                          y_s.reshape(bs, t_new, d).transpose(1, 0, 2), *tail)
            conv_s.append(new_state)
            y_s = ys_t.transpose(1, 0, 2).reshape(bs * t_new, d)
        else:
            w_qkv = attn_w_qkv[j].astype(BF16)
            w_o = attn_w_o[j].astype(BF16)
            iq, iqt = _head_indicator(N_HEADS)
            ik, ikt = _head_indicator(N_KV_HEADS)
            qg = row2(jnp.tile(attn_q_norm[j], N_HEADS))
            kg = row2(jnp.tile(attn_k_norm[j], N_KV_HEADS))
            sinks = attn_sinks[j]
            q1, k1, v1 = _qkv(y_p, g_mix, w_qkv, iq, iqt, ik, ikt, qg, kg)
            q2, k2, v2 = _qkv(y_s, g_mix, w_qkv, iq, iqt, ik, ikt, qg, kg)
            k_p.append(k1[seq - WINDOW:].reshape(1, WINDOW, N_KV_HEADS, HEAD_DIM))
            v_p.append(v1[seq - WINDOW:].reshape(1, WINDOW, N_KV_HEADS, HEAD_DIM))
            y_p = _attn_p(sinks, q1, k1, v1, y_p, _prompt_bias(), w_o)
            n_cols = -(-(SAMPLE_SEQS * (WINDOW + t_new)) // LANES) * LANES
            o_s, nk, nv = _attn_s(sinks, q2, k2, v2, cache_k[j].reshape(bs * WINDOW, KV_DIM),
                                  cache_v[j].reshape(bs * WINDOW, KV_DIM), _sample_bias(t_new, n_cols), t_new)
            k_s.append(nk.reshape(bs, WINDOW, N_KV_HEADS, HEAD_DIM))
            v_s.append(nv.reshape(bs, WINDOW, N_KV_HEADS, HEAD_DIM))
            y_s = _proj_res(o_s, y_s, w_o)

        w_r = jnp.zeros((d, ROUTER_LANES), F32)
        w_r = w_r.at[:, :N_EXPERT_GROUPS].set(moe_w_rg[i]).at[:, EXPERT_LANE0:EXPERT_LANE0 + N_EXPERTS].set(moe_w_re[i])
        b_r = jnp.zeros((1, ROUTER_LANES), F32)
        b_r = b_r.at[0, :N_EXPERT_GROUPS].set(moe_b_rg[i]).at[0, EXPERT_LANE0:EXPERT_LANE0 + N_EXPERTS].set(moe_b_re[i])
        moe = (row2(norm_ffn[i]), jnp.stack(_split_bf16(w_r)), b_r, moe_w_gate, moe_w_up, moe_w_down)
        ple = (row2(norm_ple[i]), ple_w_gate[i].astype(BF16), ple_w_proj[i].astype(BF16))
        y_p, y_s = _moe_ple(i, y_p, y_s, *moe, p_prompt.reshape(depth, seq, PLE_DIM),
                            p_sample[i].reshape(bs * t_new, PLE_DIM), *ple)

    return (y_p.reshape(1, seq, d), y_s.reshape(bs, t_new, d), jnp.stack(conv_p), jnp.stack(conv_s),
            jnp.stack(k_p), jnp.stack(v_p), jnp.stack(k_s), jnp.stack(v_s))
```

```python
import functools

import numpy as np

import jax
import jax.numpy as jnp
from jax import lax
from jax.experimental import pallas as pl
from jax.experimental.pallas import tpu as pltpu

F32 = jnp.float32
BF16 = jnp.bfloat16

D_MODEL = 1024
PLE_DIM = 256
CONV_WIDTH = 31
CONV_STATE = CONV_WIDTH - 1
N_HEADS = 16
N_KV_HEADS = 4
HEAD_DIM = 64
GROUP = N_HEADS // N_KV_HEADS
WINDOW = 128
KV_DIM = N_KV_HEADS * HEAD_DIM
N_EXPERT_GROUPS = 4
EXPERTS_PER_GROUP = 4
N_EXPERTS = 16
D_EXPERT = 256
EPS = 1e-6
NEG_INF = -1e30

LANES = 128
ROUTER_LANES = LANES
EXPERT_LANE0 = N_EXPERT_GROUPS
HALO = 32
CONV_ROWS = 32
NORM_ROWS = 128
CONV_PITCH = D_MODEL // LANES + 1
SAMPLE_SEQS = 8
PAIRS_PER_GROUP = EXPERTS_PER_GROUP * (EXPERTS_PER_GROUP - 1) // 2
N_BUCKETS = N_EXPERT_GROUPS * PAIRS_PER_GROUP
TM = 256
PAY_WIDTH = D_MODEL + LANES
ROW_PITCH = PAY_WIDTH // LANES
DMA_CHUNK = 512
RING = 3
VMEM_LIMIT = 48 * 1024 * 1024


def _row_tile(n):
    return 512 if n % 512 == 0 else n


def _params(*sem):
    return pltpu.CompilerParams(dimension_semantics=sem, vmem_limit_bytes=VMEM_LIMIT)


def _full(shape):
    nd = len(shape)
    return pl.BlockSpec(shape, lambda *_: (0,) * nd)


def _rms(x, g):
    ms = jnp.mean(x * x, axis=-1, keepdims=True)
    return x * lax.rsqrt(ms + EPS) * g


def _sigmoid(x):
    return 1.0 / (1.0 + jnp.exp(-x))


def _mm(a, b):
    return jnp.dot(a, b, preferred_element_type=F32)


def _conv_in_kernel(x_ref, g_ref, w_ref, b_ref, u_ref):
    h = _rms(x_ref[...], g_ref[...]).astype(BF16)
    z = _mm(h, w_ref[...]) + b_ref[...]
    u_ref[...] = z[:, :D_MODEL] * _sigmoid(z[:, D_MODEL:])


def _conv_in(x, g, w, b):
    n = x.shape[0]
    t = _row_tile(n)
    return pl.pallas_call(
        _conv_in_kernel,
        grid=(n // t,),
        in_specs=[pl.BlockSpec((t, D_MODEL), lambda i: (i, 0)), _full(g.shape), _full(w.shape), _full(b.shape)],
        out_specs=pl.BlockSpec((t, D_MODEL), lambda i: (i, 0)),
        out_shape=jax.ShapeDtypeStruct((n, D_MODEL), F32),
        compiler_params=_params("arbitrary"),
        name="conv_in",
    )(x, g, w, b)


def _ln_silu(c, g, b):
    mu = jnp.mean(c, axis=-1, keepdims=True)
    xc = c - mu
    var = jnp.mean(xc * xc, axis=-1, keepdims=True)
    cn = xc * lax.rsqrt(var + EPS) * g + b
    return cn * _sigmoid(cn)


def _conv_out_p_kernel(u_ref, halo_ref, x_ref, wdw_ref, bdw_ref, lng_ref, lnb_ref, wout_ref, bout_ref, y_ref,
                       ubuf, cbuf, hbuf):
    t = u_ref.shape[0]
    i = pl.program_id(0)
    nj = D_MODEL // LANES

    def put(r, j, val):
        ubuf[pl.ds(r * CONV_PITCH + j, 8, stride=CONV_PITCH), :] = val

    for r in range(0, HALO, 8):
        for j in range(nj):
            put(r, j, jnp.where(i > 0, halo_ref[r:r + 8, j * LANES:(j + 1) * LANES], 0.0))

    def fill(rr, carry):
        r = pl.multiple_of(rr * 8, 8)
        for j in range(nj):
            put(r + HALO, j, u_ref[pl.ds(r, 8), j * LANES:(j + 1) * LANES])
        return carry

    lax.fori_loop(0, t // 8, fill, 0)

    def conv_chunk(rr, carry):
        r0 = pl.multiple_of(rr * CONV_ROWS, CONV_ROWS)
        for j in range(nj):
            lanes = slice(j * LANES, (j + 1) * LANES)
            accs = [None] * (CONV_ROWS // 8)
            for k in range(CONV_WIDTH):
                wk = wdw_ref[k:k + 1, lanes]
                for q in range(CONV_ROWS // 8):
                    r = r0 + (HALO - CONV_STATE + k + 8 * q)
                    term = wk * ubuf[pl.ds(r * CONV_PITCH + j, 8, stride=CONV_PITCH), :]
                    accs[q] = term if k == 0 else accs[q] + term
            for q in range(CONV_ROWS // 8):
                cbuf[pl.ds(r0 + 8 * q, 8), lanes] = accs[q]
        return carry

    lax.fori_loop(0, t // CONV_ROWS, conv_chunk, 0)

    def norm_chunk(rr, carry):
        r0 = pl.multiple_of(rr * NORM_ROWS, NORM_ROWS)
        c = cbuf[pl.ds(r0, NORM_ROWS), :] + bdw_ref[...]
        hbuf[pl.ds(r0, NORM_ROWS), :] = _ln_silu(c, lng_ref[...], lnb_ref[...]).astype(BF16)
        return carry

    lax.fori_loop(0, t // NORM_ROWS, norm_chunk, 0)
    y_ref[...] = x_ref[...] + _mm(hbuf[...], wout_ref[...]) + bout_ref[...]


def _conv_out_p(u, x, wdw, bdw, lng, lnb, wout, bout):
    n = u.shape[0]
    t = _row_tile(n)
    hb = t // HALO
    row = lambda i: (i, 0)
    return pl.pallas_call(
        _conv_out_p_kernel,
        grid=(n // t,),
        in_specs=[pl.BlockSpec((t, D_MODEL), row),
                  pl.BlockSpec((HALO, D_MODEL), lambda i: (jnp.maximum(i * hb - 1, 0), 0)),
                  pl.BlockSpec((t, D_MODEL), row),
                  _full(wdw.shape), _full(bdw.shape), _full(lng.shape), _full(lnb.shape),
                  _full(wout.shape), _full(bout.shape)],
        out_specs=pl.BlockSpec((t, D_MODEL), row),
        out_shape=jax.ShapeDtypeStruct((n, D_MODEL), F32),
        scratch_shapes=[pltpu.VMEM(((t + HALO) * CONV_PITCH, LANES), F32), pltpu.VMEM((t, D_MODEL), F32),
                        pltpu.VMEM((t, D_MODEL), BF16)],
        compiler_params=_params("arbitrary"),
        name="conv_out_prompt",
    )(u, u, x, wdw, bdw, lng, lnb, wout, bout)


def _conv_out_s_kernel(st_ref, u_ref, x_ref, wdw_ref, bdw_ref, lng_ref, lnb_ref, wout_ref, bout_ref, y_ref, ns_ref,
                       wbuf, cbuf):
    bb, nt, _ = u_ref.shape
    win = CONV_STATE + nt
    nj = D_MODEL // LANES
    seq_pitch = win * CONV_PITCH

    def put(b, r0, rows, j, val):
        wbuf[pl.ds((b * win + r0) * CONV_PITCH + j, rows, stride=CONV_PITCH), :] = val

    for b in range(bb):
        for j in range(nj):
            lanes = slice(j * LANES, (j + 1) * LANES)
            for r0 in range(0, CONV_STATE, 8):
                rows = min(8, CONV_STATE - r0)
                put(b, r0, rows, j, st_ref[b, r0:r0 + rows, lanes])
            put(b, CONV_STATE, nt, j, u_ref[b, :, lanes])

    for b in range(bb):
        for r0 in range(0, CONV_STATE, 8):
            rows = min(8, CONV_STATE - r0)
            for j in range(nj):
                src = (b * win + nt + r0) * CONV_PITCH + j
                ns_ref[b, r0:r0 + rows, j * LANES:(j + 1) * LANES] = wbuf[pl.ds(src, rows, stride=CONV_PITCH), :]

    for t in range(nt):
        for b0 in range(0, bb, 8):
            for j in range(nj):
                lanes = slice(j * LANES, (j + 1) * LANES)
                acc = None
                for k in range(CONV_WIDTH):
                    start = (b0 * win + t + k) * CONV_PITCH + j
                    term = wdw_ref[k:k + 1, lanes] * wbuf[pl.ds(start, 8, stride=seq_pitch), :]
                    acc = term if acc is None else acc + term
                cbuf[t * bb + b0:t * bb + b0 + 8, lanes] = acc
    c = cbuf[...] + bdw_ref[...]
    h = _ln_silu(c, lng_ref[...], lnb_ref[...]).astype(BF16)
    y = _mm(h, wout_ref[...]) + bout_ref[...]
    for t in range(nt):
        y_ref[t] = x_ref[t] + y[t * bb:(t + 1) * bb]


def _conv_out_s(state, u, x, wdw, bdw, lng, lnb, wout, bout):
    b, nt, _ = u.shape
    bb = 16 if b % 16 == 0 else b
    assert bb % 8 == 0
    win = CONV_STATE + nt
    seq = lambda i: (i, 0, 0)
    tm = lambda i: (0, i, 0)
    return pl.pallas_call(
        _conv_out_s_kernel,
        grid=(b // bb,),
        in_specs=[pl.BlockSpec((bb, CONV_STATE, D_MODEL), seq), pl.BlockSpec((bb, nt, D_MODEL), seq),
                  pl.BlockSpec((nt, bb, D_MODEL), tm),
                  _full(wdw.shape), _full(bdw.shape), _full(lng.shape), _full(lnb.shape),
                  _full(wout.shape), _full(bout.shape)],
        out_specs=[pl.BlockSpec((nt, bb, D_MODEL), tm), pl.BlockSpec((bb, CONV_STATE, D_MODEL), seq)],
        out_shape=[jax.ShapeDtypeStruct((nt, b, D_MODEL), F32), jax.ShapeDtypeStruct((b, CONV_STATE, D_MODEL), F32)],
        scratch_shapes=[pltpu.VMEM((bb * win * CONV_PITCH, LANES), F32), pltpu.VMEM((nt * bb, D_MODEL), F32)],
        compiler_params=_params("arbitrary"),
        name="conv_out_sample",
    )(state, u, x, wdw, bdw, lng, lnb, wout, bout)


def _route(lg):
    big = 3.0e38
    lane = lax.broadcasted_iota(jnp.int32, lg.shape, 1)
    lanef = lane.astype(F32)
    is_g = lane < N_EXPERT_GROUPS
    gl = jnp.where(is_g, lg, -big)
    gmax = jnp.max(gl, axis=1, keepdims=True)
    gsum = jnp.sum(jnp.where(is_g, jnp.exp(gl - gmax), 0.0), axis=1, keepdims=True)
    g_w = 1.0 / gsum
    g_idx = jnp.min(jnp.where(gl == gmax, lanef, big), axis=1, keepdims=True)
    rel = lanef - float(EXPERT_LANE0) - g_idx * float(EXPERTS_PER_GROUP)
    in_grp = jnp.where(rel >= 0.0, jnp.where(rel < float(EXPERTS_PER_GROUP), 1.0, 0.0), 0.0) > 0.5
    el = jnp.where(in_grp, lg, -big)
    e1 = jnp.max(el, axis=1, keepdims=True)
    i1 = jnp.min(jnp.where(el == e1, lanef, big), axis=1, keepdims=True)
    el2 = jnp.where(lanef == i1, -big, el)
    e2 = jnp.max(el2, axis=1, keepdims=True)
    i2 = jnp.min(jnp.where(el2 == e2, lanef, big), axis=1, keepdims=True)
    tt = jnp.exp(e2 - e1)
    w1 = g_w / (1.0 + tt)
    w2 = g_w * tt / (1.0 + tt)
    base = float(EXPERT_LANE0) + g_idx * float(EXPERTS_PER_GROUP)
    a = jnp.minimum(i1, i2) - base
    b = jnp.maximum(i1, i2) - base
    pair = a * (7.0 - a) * 0.5 + (b - a - 1.0)
    first_is_lo = i1 < i2
    return (g_idx * float(PAIRS_PER_GROUP) + pair, jnp.where(first_is_lo, w1, w2), jnp.where(first_is_lo, w2, w1))


def _to_row_linear(dst_ref, src_ref, n_tiles):
    def body(g, carry):
        r = pl.multiple_of(g * 8, 8)
        for j in range(n_tiles):
            dst_ref[pl.ds(r * ROW_PITCH + j, 8, stride=ROW_PITCH), :] = src_ref[pl.ds(r, 8), j * LANES:(j + 1) * LANES]
        return carry

    lax.fori_loop(0, src_ref.shape[0] // 8, body, 0)


def _from_row_linear(dst_ref, src_ref, n_tiles, pitch):
    def body(g, carry):
        r = pl.multiple_of(g * 8, 8)
        for j in range(n_tiles):
            dst_ref[pl.ds(r, 8), j * LANES:(j + 1) * LANES] = src_ref[pl.ds(r * pitch + j, 8, stride=pitch), :]
        return carry

    lax.fori_loop(0, dst_ref.shape[0] // 8, body, 0)


def _pick(i, n_prompt_tiles, prompt_ref, sample_ref):
    return jnp.where(i < n_prompt_tiles, prompt_ref[...], sample_ref[...])


def _route_kernel(n_prompt_tiles, yp_ref, ys_ref, g_ref, wr_ref, br_ref, tri_ref, pay_ref, meta_ref, cnt_ref, carry,
                  rowbuf):
    i = pl.program_id(0)

    @pl.when(i == 0)
    def _():
        carry[...] = jnp.zeros_like(carry)

    xf = _rms(_pick(i, n_prompt_tiles, yp_ref, ys_ref), g_ref[...])
    x_hi, x_lo = _split_bf16(xf)
    logits = _mm(x_hi, wr_ref[0]) + (_mm(x_lo, wr_ref[0]) + _mm(x_hi, wr_ref[1])) + br_ref[...]
    bucket, w_lo, w_hi = _route(logits)
    lane = lax.broadcasted_iota(jnp.int32, logits.shape, 1)
    onehot = jnp.where(lane.astype(F32) == bucket, 1.0, 0.0)
    before = _mm(tri_ref[...], onehot.astype(BF16)) + carry[...]
    rank = jnp.sum(onehot * before, axis=1, keepdims=True)
    carry[...] += jnp.sum(onehot, axis=0, keepdims=True)
    cnt_ref[...] = carry[...]
    meta = jnp.where(lane == 0, bucket, jnp.where(lane == 1, rank, 0.0))
    meta_ref[...] = jnp.transpose(meta)[:8, :].astype(jnp.int32)

    rowbuf[:, :D_MODEL] = xf
    rowbuf[:, D_MODEL:] = jnp.where(lane == 0, w_lo, jnp.where(lane == 1, w_hi, 0.0))
    _to_row_linear(pay_ref, rowbuf, ROW_PITCH)


def _route_call(y_p, y_s, g, wr, br, tri):
    t = y_s.shape[0]
    npt = y_p.shape[0] // t
    n = y_p.shape[0] + t
    pidx = lambda i: (jnp.minimum(i, npt - 1), 0)
    return pl.pallas_call(
        functools.partial(_route_kernel, npt),
        grid=(npt + 1,),
        in_specs=[pl.BlockSpec((t, D_MODEL), pidx), _full(y_s.shape), _full(g.shape), _full(wr.shape),
                  _full(br.shape), _full(tri.shape)],
        out_specs=[pl.BlockSpec((t * ROW_PITCH, LANES), lambda i: (i, 0)),
                   pl.BlockSpec((8, t), lambda i: (0, i)), _full((1, ROUTER_LANES))],
        out_shape=[jax.ShapeDtypeStruct((n * ROW_PITCH, LANES), F32),
                   jax.ShapeDtypeStruct((8, n), jnp.int32), jax.ShapeDtypeStruct((1, ROUTER_LANES), F32)],
        scratch_shapes=[pltpu.VMEM((1, ROUTER_LANES), F32), pltpu.VMEM((t, PAY_WIDTH), F32)],
        compiler_params=_params("arbitrary"),
        name="route",
    )(y_p, y_s, g, wr, br, tri)


def _permute_kernel(pos_ref, zstart_ref, nz_ref, used_ref, src_ref, dst_ref, zbuf, ring, zsem, in_sems, out_sems):
    zbuf[...] = jnp.zeros_like(zbuf)

    def zero_tile(start):
        return pltpu.make_async_copy(zbuf, dst_ref.at[pl.ds(pl.multiple_of(start * ROW_PITCH, 8), TM * ROW_PITCH)], zsem)

    for b in range(N_BUCKETS):
        @pl.when(nz_ref[b] > 0)
        def _():
            zero_tile(zstart_ref[b]).start()
    n_tiles = dst_ref.shape[0] // (TM * ROW_PITCH)

    def start_unused(i, carry):
        zero_tile(i * TM).start()
        return carry

    lax.fori_loop(used_ref[0], n_tiles, start_unused, 0)
    for b in range(N_BUCKETS):
        @pl.when(nz_ref[b] > 0)
        def _():
            zero_tile(0).wait()

    def wait_unused(i, carry):
        zero_tile(0).wait()
        return carry

    lax.fori_loop(used_ref[0], n_tiles, wait_unused, 0)

    chunk_rows = DMA_CHUNK * ROW_PITCH
    n_chunks = src_ref.shape[0] // chunk_rows

    def fetch(c, slot):
        return pltpu.make_async_copy(src_ref.at[pl.ds(pl.multiple_of(c * chunk_rows, 8), chunk_rows)], ring.at[slot],
                                     in_sems.at[slot])

    def drain(slot):
        pltpu.make_async_copy(ring.at[slot], dst_ref.at[pl.ds(0, chunk_rows)], out_sems.at[slot]).wait()

    fetch(0, 0).start()

    def step(c, carry):
        slot = lax.rem(c, RING)
        nxt = lax.rem(c + 1, RING)
        fetch(c, slot).wait()

        @pl.when(c >= RING - 1)
        def _():
            drain(nxt)

        @pl.when(c + 1 < n_chunks)
        def _():
            fetch(c + 1, nxt).start()

        def send(jj, carry2):
            row = pos_ref[c * DMA_CHUNK + jj] * ROW_PITCH
            pltpu.make_async_copy(ring.at[slot, pl.ds(jj * ROW_PITCH, ROW_PITCH)], dst_ref.at[pl.ds(row, ROW_PITCH)],
                                  out_sems.at[slot]).start()
            return carry2

        lax.fori_loop(0, DMA_CHUNK, send, 0, unroll=8)
        return carry

    lax.fori_loop(0, n_chunks, step, 0)
    for c in range(n_chunks - (RING - 1), n_chunks):
        drain(c % RING)


def _permute(pos, zstart, nz, used, payload, n_rows_out):
    assert payload.shape[0] // (DMA_CHUNK * ROW_PITCH) >= RING
    return pl.pallas_call(
        _permute_kernel,
        grid_spec=pltpu.PrefetchScalarGridSpec(
            num_scalar_prefetch=4, grid=(1,),
            in_specs=[pl.BlockSpec(memory_space=pl.ANY)],
            out_specs=pl.BlockSpec(memory_space=pl.ANY),
            scratch_shapes=[pltpu.VMEM((TM * ROW_PITCH, LANES), payload.dtype),
                            pltpu.VMEM((RING, DMA_CHUNK * ROW_PITCH, LANES), payload.dtype),
                            pltpu.SemaphoreType.DMA(()), pltpu.SemaphoreType.DMA((RING,)),
                            pltpu.SemaphoreType.DMA((RING,))]),
        out_shape=jax.ShapeDtypeStruct((n_rows_out * ROW_PITCH, LANES), payload.dtype),
        compiler_params=_params("arbitrary"),
        name="permute",
    )(pos, zstart, nz, used, payload)


def _silu(x):
    return x * _sigmoid(x)


def _experts_kernel(tix_ref, lo_ref, hi_ref, valid_ref, fresh_ref, xs_ref, wgl_ref, wgh_ref, wul_ref, wuh_ref,
                    wdl_ref, wdh_ref, ys_ref, wup_s, wdn_s, xbuf, ybuf):
    i = pl.program_id(0)

    @pl.when(fresh_ref[i] > 0)
    def _():
        for s, ref in enumerate((wgl_ref, wgh_ref, wul_ref, wuh_ref)):
            wup_s[s] = ref[0, 0].astype(BF16)
        for s, ref in enumerate((wdl_ref, wdh_ref)):
            wdn_s[s] = ref[0, 0].astype(BF16)

    @pl.when(valid_ref[i] == 0)
    def _():
        ys_ref[...] = jnp.zeros_like(ys_ref)

    @pl.when(valid_ref[i] > 0)
    def _():
        _from_row_linear(xbuf, xs_ref, ROW_PITCH, ROW_PITCH)
        x = xbuf[:, :D_MODEL].astype(BF16)
        gates = xbuf[:, D_MODEL:]
        h_lo = _silu(_mm(x, wup_s[0])) * _mm(x, wup_s[2]) * gates[:, 0:1]
        h_hi = _silu(_mm(x, wup_s[1])) * _mm(x, wup_s[3]) * gates[:, 1:2]
        ybuf[:, :D_MODEL] = _mm(h_lo.astype(BF16), wdn_s[0]) + _mm(h_hi.astype(BF16), wdn_s[1])
        ybuf[:, D_MODEL:] = jnp.zeros((TM, LANES), F32)
        _to_row_linear(ys_ref, ybuf, ROW_PITCH)


def _experts(layer, tix, lo, hi, valid, fresh, xs, wg, wu, wd):
    n_tiles = xs.shape[0] // (TM * ROW_PITCH)
    row = lambda i, tix, lo, hi, valid, fresh: (tix[i], 0)
    row_out = lambda i, tix, lo, hi, valid, fresh: (i, 0)
    e_lo = lambda i, tix, lo, hi, valid, fresh: (layer, lo[i], 0, 0)
    e_hi = lambda i, tix, lo, hi, valid, fresh: (layer, hi[i], 0, 0)
    up = pl.BlockSpec((1, 1, D_MODEL, D_EXPERT), e_lo), pl.BlockSpec((1, 1, D_MODEL, D_EXPERT), e_hi)
    down = pl.BlockSpec((1, 1, D_EXPERT, D_MODEL), e_lo), pl.BlockSpec((1, 1, D_EXPERT, D_MODEL), e_hi)
    return pl.pallas_call(
        _experts_kernel,
        grid_spec=pltpu.PrefetchScalarGridSpec(
            num_scalar_prefetch=5, grid=(n_tiles,),
            in_specs=[pl.BlockSpec((TM * ROW_PITCH, LANES), row), *up, *up, *down],
            out_specs=pl.BlockSpec((TM * ROW_PITCH, LANES), row_out),
            scratch_shapes=[pltpu.VMEM((4, D_MODEL, D_EXPERT), BF16), pltpu.VMEM((2, D_EXPERT, D_MODEL), BF16),
                            pltpu.VMEM((TM, PAY_WIDTH), F32), pltpu.VMEM((TM, PAY_WIDTH), F32)]),
        out_shape=jax.ShapeDtypeStruct(xs.shape, F32),
        compiler_params=_params("arbitrary"),
        name="experts",
    )(tix, lo, hi, valid, fresh, xs, wg, wg, wu, wu, wd, wd)


def _ple_kernel(n_prompt_tiles, pos_ref, yp_ref, ys_ref, moe_ref, pp_ref, ps_ref, g_ref, wg_ref, wp_ref, op_ref, os_ref,
                mring, mbuf, sems):
    i = pl.program_id(0)
    t = mbuf.shape[0]
    w = D_MODEL // LANES

    def gather(tile, slot):
        def body(jj, carry):
            row = pos_ref[tile * t + jj] * ROW_PITCH
            pltpu.make_async_copy(moe_ref.at[pl.ds(row, w)], mring.at[slot, pl.ds(jj * w, w)], sems.at[slot]).start()
            return carry

        lax.fori_loop(0, t, body, 0, unroll=8)

    @pl.when(i == 0)
    def _():
        gather(0, 0)

    @pl.when(i < n_prompt_tiles)
    def _():
        gather(i + 1, lax.rem(i + 1, 2))

    slot = lax.rem(i, 2)
    pltpu.make_async_copy(moe_ref.at[pl.ds(0, t * w)], mring.at[slot], sems.at[slot]).wait()
    _from_row_linear(mbuf, mring.at[slot], w, w)
    y2 = _pick(i, n_prompt_tiles, yp_ref, ys_ref) + mbuf[...]
    hn = _rms(y2, g_ref[...]).astype(BF16)
    gt = _sigmoid(_mm(hn, wg_ref[...]))
    pr = _mm(jnp.where(i < n_prompt_tiles, pp_ref[0], ps_ref[...]).astype(BF16), wp_ref[...])
    y3 = y2 + gt * pr

    @pl.when(i < n_prompt_tiles)
    def _():
        op_ref[...] = y3

    @pl.when(i >= n_prompt_tiles)
    def _():
        os_ref[...] = y3


def _ple(layer, pos, y_p, y_s, ys_sorted, p_p, p_s, g, wg, wp):
    t = y_s.shape[0]
    npt = y_p.shape[0] // t
    w = D_MODEL // LANES
    pidx = lambda i, pos: (jnp.minimum(i, npt - 1), 0)
    full = lambda shape: pl.BlockSpec(shape, lambda i, pos: (0,) * len(shape))
    return pl.pallas_call(
        functools.partial(_ple_kernel, npt),
        grid_spec=pltpu.PrefetchScalarGridSpec(
            num_scalar_prefetch=1, grid=(npt + 1,),
            in_specs=[pl.BlockSpec((t, D_MODEL), pidx), full(y_s.shape), pl.BlockSpec(memory_space=pl.ANY),
                      pl.BlockSpec((1, t, PLE_DIM), lambda i, pos: (layer, jnp.minimum(i, npt - 1), 0)),
                      full(p_s.shape), full(g.shape), full(wg.shape), full(wp.shape)],
            out_specs=[pl.BlockSpec((t, D_MODEL), pidx), full(y_s.shape)],
            scratch_shapes=[pltpu.VMEM((2, t * w, LANES), F32), pltpu.VMEM((t, D_MODEL), F32),
                            pltpu.SemaphoreType.DMA((2,))]),
        out_shape=[jax.ShapeDtypeStruct(y_p.shape, F32), jax.ShapeDtypeStruct(y_s.shape, F32)],
        compiler_params=_params("arbitrary"),
        name="ple",
    )(pos, y_p, y_s, ys_sorted, p_p, p_s, g, wg, wp)


def _bucket_experts():
    pairs = [(a, b) for a in range(EXPERTS_PER_GROUP) for b in range(a + 1, EXPERTS_PER_GROUP)]
    lo = [g * EXPERTS_PER_GROUP + a for g in range(N_EXPERT_GROUPS) for a, _ in pairs]
    hi = [g * EXPERTS_PER_GROUP + b for g in range(N_EXPERT_GROUPS) for _, b in pairs]
    return jnp.array(lo, jnp.int32), jnp.array(hi, jnp.int32)


def _moe_ple(layer, y_p, y_s, gffn, wr, br, wg, wu, wd, p_p, p_s, gple, pwg, pwp):
    t = y_s.shape[0]
    n = y_p.shape[0] + t
    tri = jnp.asarray(np.tri(t, t, -1, dtype=np.float32), BF16)
    payload, meta, counts = _route_call(y_p, y_s, gffn, wr, br, tri)

    bucket, rank = meta[0], meta[1]
    cnt = counts[0, :N_BUCKETS].astype(jnp.int32)
    padded = (cnt + TM - 1) // TM * TM
    ends = jnp.cumsum(padded)
    starts = ends - padded
    in_bucket = bucket[:, None] == jnp.arange(N_BUCKETS, dtype=jnp.int32)[None, :]
    pos = rank + jnp.sum(jnp.where(in_bucket, starts[None, :], 0), axis=1)
    n_rows = n + N_BUCKETS * TM
    n_tiles = n_rows // TM
    tile_start = jnp.arange(n_tiles, dtype=jnp.int32) * TM
    valid = tile_start < ends[-1]
    used = (ends[-1] // TM).astype(jnp.int32)
    tix = jnp.arange(n_tiles, dtype=jnp.int32)
    tile_bucket = jnp.sum((ends[None, :] <= (jnp.minimum(tix, used - 1) * TM)[:, None]).astype(jnp.int32), axis=1)
    lo_tab, hi_tab = _bucket_experts()
    lo = jnp.take(lo_tab, tile_bucket)
    hi = jnp.take(hi_tab, tile_bucket)
    prev_bucket = jnp.concatenate([jnp.full((1,), -1, jnp.int32), tile_bucket[:-1]])
    fresh = valid & (tile_bucket != prev_bucket)

    xs = _permute(pos, (ends - TM).astype(jnp.int32), (cnt > 0).astype(jnp.int32), used.reshape(1), payload, n_rows)
    ys = _experts(layer, jnp.minimum(tix, used - 1), lo, hi, valid.astype(jnp.int32), fresh.astype(jnp.int32), xs, wg,
                  wu, wd)
    return _ple(layer, pos, y_p, y_s, ys, p_p, p_s, gple, pwg, pwp)


def _split_bf16(a):
    hi = a.astype(BF16)
    lo = (a - hi.astype(F32)).astype(BF16)
    return hi, lo


def _head_norm(a, ind_ref, indt_ref, gain):
    hi, lo = _split_bf16(a * a)
    ss = _mm(hi, ind_ref[...]) + _mm(lo, ind_ref[...])
    inv = lax.rsqrt(ss * (1.0 / HEAD_DIM) + EPS)
    ihi, ilo = _split_bf16(inv)
    invb = _mm(ihi, indt_ref[...]) + _mm(ilo, indt_ref[...])
    return a * invb * gain


def _qkv_kernel(x_ref, g_ref, w_ref, iq_ref, iqt_ref, ik_ref, ikt_ref, qg_ref, kg_ref, q_ref, k_ref, v_ref):
    h = _rms(x_ref[...], g_ref[...]).astype(BF16)
    qkv = _mm(h, w_ref[...])
    nq = N_HEADS * HEAD_DIM
    q = _head_norm(qkv[:, :nq], iq_ref, iqt_ref, qg_ref[...])
    k = _head_norm(qkv[:, nq:nq + KV_DIM], ik_ref, ikt_ref, kg_ref[...])
    q_ref[...] = (q * (HEAD_DIM ** -0.5)).astype(BF16)
    k_ref[...] = k
    v_ref[...] = qkv[:, nq + KV_DIM:]


def _qkv(x, g, w, iq, iqt, ik, ikt, qg, kg):
    n = x.shape[0]
    t = _row_tile(n)
    row = lambda i: (i, 0)
    return pl.pallas_call(
        _qkv_kernel,
        grid=(n // t,),
        in_specs=[pl.BlockSpec((t, D_MODEL), row)] + [_full(a.shape) for a in (g, w, iq, iqt, ik, ikt, qg, kg)],
        out_specs=[pl.BlockSpec((t, N_HEADS * HEAD_DIM), row), pl.BlockSpec((t, KV_DIM), row),
                   pl.BlockSpec((t, KV_DIM), row)],
        out_shape=[jax.ShapeDtypeStruct((n, N_HEADS * HEAD_DIM), BF16), jax.ShapeDtypeStruct((n, KV_DIM), F32),
                   jax.ShapeDtypeStruct((n, KV_DIM), F32)],
        compiler_params=_params("arbitrary"),
        name="qkv",
    )(x, g, w, iq, iqt, ik, ikt, qg, kg)


def _dup_heads(a):
    out = []
    for s in range(KV_DIM // LANES):
        sl = a[:, s * LANES:(s + 1) * LANES]
        sw = pltpu.roll(sl, HEAD_DIM, axis=1)
        low = lax.broadcasted_iota(jnp.int32, sl.shape, 1) < HEAD_DIM
        out.append(jnp.where(low, sl, sw))
        out.append(jnp.where(low, sw, sl))
    return jnp.concatenate(out, axis=1).astype(BF16)


def _attend(q_rows, k2, v2, bias_of, sink_of, extra_mask):
    m_rows = q_rows.shape[0]
    low_q = lax.broadcasted_iota(jnp.int32, (m_rows, LANES), 1) < HEAD_DIM
    low_k = lax.broadcasted_iota(jnp.int32, (k2.shape[0], LANES), 1) < HEAD_DIM
    zero_q = jnp.zeros((m_rows, LANES), BF16)
    zero_k = jnp.zeros((k2.shape[0], LANES), BF16)
    slabs = []
    for g in range(N_KV_HEADS):
        kg = k2[:, g * LANES:(g + 1) * LANES]
        vg = v2[:, g * LANES:(g + 1) * LANES]
        v_lo = jnp.where(low_k, vg, zero_k)
        v_hi = jnp.where(low_k, zero_k, vg)
        lhs = []
        for a in range(GROUP):
            h = g * GROUP + a
            qs = q_rows[:, (h // 2) * LANES:(h // 2 + 1) * LANES]
            lhs.append(jnp.where(low_q, qs, zero_q) if h % 2 == 0 else jnp.where(low_q, zero_q, qs))
        s = lax.dot_general(jnp.concatenate(lhs, axis=0), kg, (((1,), (1,)), ((), ())), preferred_element_type=F32)
        probs, rinv = [], []
        for a in range(GROUP):
            h = g * GROUP + a
            sa = s[a * m_rows:(a + 1) * m_rows] + bias_of(h)
            if extra_mask is not None:
                sa = jnp.where(extra_mask, NEG_INF, sa)
            sink = sink_of(h)
            m = jnp.maximum(jnp.max(sa, axis=1, keepdims=True), sink)
            p = jnp.exp(sa - m)
            den = jnp.sum(p, axis=1, keepdims=True) + jnp.exp(sink - m)
            probs.append(p.astype(BF16))
            rinv.append(1.0 / den)
        for sp in range(GROUP // 2):
            o = _mm(probs[2 * sp], v_lo) + _mm(probs[2 * sp + 1], v_hi)
            slabs.append(o * jnp.where(low_q, rinv[2 * sp], rinv[2 * sp + 1]))
    return jnp.concatenate(slabs, axis=1)


def _attn_p_kernel(sink_ref, q_ref, k_ref, v_ref, x_ref, bias_ref, wo_ref, y_ref, kbuf, vbuf, obuf):
    t = q_ref.shape[0]
    i = pl.program_id(0)

    @pl.when(i == 0)
    def _():
        kbuf[0:WINDOW, :] = jnp.zeros((WINDOW, 2 * KV_DIM), BF16)
        vbuf[0:WINDOW, :] = jnp.zeros((WINDOW, 2 * KV_DIM), BF16)

    @pl.when(i > 0)
    def _():
        kbuf[0:WINDOW, :] = kbuf[t:t + WINDOW, :]
        vbuf[0:WINDOW, :] = vbuf[t:t + WINDOW, :]

    kbuf[WINDOW:, :] = _dup_heads(k_ref[...])
    vbuf[WINDOW:, :] = _dup_heads(v_ref[...])
    col = lax.broadcasted_iota(jnp.int32, (WINDOW, 2 * WINDOW), 1)
    for j in range(t // WINDOW):
        rows = slice(j * WINDOW, (j + 1) * WINDOW)
        keys = slice(j * WINDOW, (j + 2) * WINDOW)
        extra = jnp.logical_and(i == 0, col < WINDOW) if j == 0 else None
        o = _attend(q_ref[rows, :], kbuf[keys, :], vbuf[keys, :], lambda h: bias_ref[h], lambda h: sink_ref[h], extra)
        obuf[rows, :] = o.astype(BF16)
    y_ref[...] = x_ref[...] + _mm(obuf[...], wo_ref[...])


def _attn_p(sinks, q, k, v, x, bias, wo):
    n = q.shape[0]
    t = _row_tile(n)
    row = lambda i: (i, 0)
    return pl.pallas_call(
        _attn_p_kernel,
        grid=(n // t,),
        in_specs=[pl.BlockSpec(memory_space=pltpu.SMEM),
                  pl.BlockSpec((t, N_HEADS * HEAD_DIM), row), pl.BlockSpec((t, KV_DIM), row),
                  pl.BlockSpec((t, KV_DIM), row), pl.BlockSpec((t, D_MODEL), row),
                  _full(bias.shape), _full(wo.shape)],
        out_specs=pl.BlockSpec((t, D_MODEL), row),
        out_shape=jax.ShapeDtypeStruct((n, D_MODEL), F32),
        scratch_shapes=[pltpu.VMEM((t + WINDOW, 2 * KV_DIM), BF16), pltpu.VMEM((t + WINDOW, 2 * KV_DIM), BF16),
                        pltpu.VMEM((t, N_HEADS * HEAD_DIM), BF16)],
        compiler_params=_params("arbitrary"),
        name="attn_prompt",
    )(sinks, q, k, v, x, bias, wo)


def _attn_s_kernel(sink_ref, q_ref, k_ref, v_ref, ck_ref, cv_ref, bias_ref, o_ref, nk_ref, nv_ref, shift_buf):
    pad = bias_ref.shape[2] - ck_ref.shape[0] - k_ref.shape[0]
    zpad = jnp.zeros((pad, KV_DIM), F32)
    k2 = _dup_heads(jnp.concatenate([ck_ref[...], k_ref[...], zpad], axis=0))
    v2 = _dup_heads(jnp.concatenate([cv_ref[...], v_ref[...], zpad], axis=0))
    o = _attend(q_ref[...], k2, v2, lambda h: bias_ref[h], lambda h: sink_ref[h], None)
    o_ref[...] = o.astype(BF16)
    t_new = k_ref.shape[0] // SAMPLE_SEQS
    for new_ref, old_ref, add_ref in ((nk_ref, ck_ref, k_ref), (nv_ref, cv_ref, v_ref)):
        for b in range(SAMPLE_SEQS):
            r = b * WINDOW
            shift_buf[0:WINDOW - t_new, :] = old_ref[r + t_new:r + WINDOW, :]
            shift_buf[WINDOW - t_new:WINDOW, :] = add_ref[b * t_new:(b + 1) * t_new, :]
            new_ref[b] = jnp.transpose(shift_buf[...])


def _attn_s(sinks, q, k, v, ck, cv, bias, t_new):
    n = q.shape[0]
    rows = SAMPLE_SEQS * t_new
    crow = SAMPLE_SEQS * WINDOW
    row = lambda i: (i, 0)
    return pl.pallas_call(
        _attn_s_kernel,
        grid=(n // rows,),
        in_specs=[pl.BlockSpec(memory_space=pltpu.SMEM),
                  pl.BlockSpec((rows, N_HEADS * HEAD_DIM), row), pl.BlockSpec((rows, KV_DIM), row),
                  pl.BlockSpec((rows, KV_DIM), row), pl.BlockSpec((crow, KV_DIM), row),
                  pl.BlockSpec((crow, KV_DIM), row), _full(bias.shape)],
        out_specs=[pl.BlockSpec((rows, N_HEADS * HEAD_DIM), row),
                   pl.BlockSpec((SAMPLE_SEQS, KV_DIM, WINDOW), lambda i: (i, 0, 0)),
                   pl.BlockSpec((SAMPLE_SEQS, KV_DIM, WINDOW), lambda i: (i, 0, 0))],
        out_shape=[jax.ShapeDtypeStruct((n, N_HEADS * HEAD_DIM), BF16),
                   jax.ShapeDtypeStruct((ck.shape[0] // WINDOW, KV_DIM, WINDOW), F32),
                   jax.ShapeDtypeStruct((cv.shape[0] // WINDOW, KV_DIM, WINDOW), F32)],
        scratch_shapes=[pltpu.VMEM((WINDOW, KV_DIM), F32)],
        compiler_params=_params("arbitrary"),
        name="attn_sample",
    )(sinks, q, k, v, ck, cv, bias)


def _proj_res_kernel(o_ref, x_ref, w_ref, y_ref):
    y_ref[...] = x_ref[...] + _mm(o_ref[...], w_ref[...])


def _proj_res(o, x, w):
    n = o.shape[0]
    t = _row_tile(n)
    row = lambda i: (i, 0)
    return pl.pallas_call(
        _proj_res_kernel,
        grid=(n // t,),
        in_specs=[pl.BlockSpec((t, o.shape[1]), row), pl.BlockSpec((t, D_MODEL), row), _full(w.shape)],
        out_specs=pl.BlockSpec((t, D_MODEL), row),
        out_shape=jax.ShapeDtypeStruct((n, D_MODEL), F32),
        compiler_params=_params("arbitrary"),
        name="proj_res",
    )(o, x, w)


def _alibi_slopes():
    return np.exp2(-8.0 * np.arange(1, N_HEADS + 1, dtype=np.float64) / N_HEADS).astype(np.float32)


def _band_bias(dist, allowed):
    b = -(_alibi_slopes()[:, None, None] * dist.astype(np.float32)[None])
    return jnp.asarray(np.where(allowed[None], b, np.float32(NEG_INF)).astype(np.float32))


def _prompt_bias():
    dist = WINDOW + np.arange(WINDOW)[:, None] - np.arange(2 * WINDOW)[None, :]
    return _band_bias(dist, (dist >= 0) & (dist <= WINDOW))


def _sample_bias(t_new, n_cols):
    c = np.arange(n_cols)
    n_cache = SAMPLE_SEQS * WINDOW
    n_new = SAMPLE_SEQS * t_new
    is_cache = c < n_cache
    is_new = (c >= n_cache) & (c < n_cache + n_new)
    seq_c = np.where(is_cache, c // WINDOW, (c - n_cache) // t_new)
    pos_c = np.where(is_cache, c % WINDOW, WINDOW + (c - n_cache) % t_new)
    r = np.arange(n_new)
    seq_r, tok_r = r // t_new, r % t_new
    dist = WINDOW + tok_r[:, None] - pos_c[None, :]
    allowed = (seq_r[:, None] == seq_c[None, :]) & (is_cache | is_new)[None, :] & (dist >= 0) & (dist <= WINDOW)
    return _band_bias(dist, allowed)


def _head_indicator(n_heads):
    ch = np.arange(n_heads * HEAD_DIM) // HEAD_DIM
    ind = (ch[:, None] == np.arange(LANES)[None, :]).astype(np.float32)
    return jnp.asarray(ind, BF16), jnp.asarray(ind.T, BF16)


def kernel(x_prompt, x_sample, state_conv, cache_k, cache_v, p_prompt, p_sample, norm_mix, norm_ffn, norm_ple,
           conv_w_in, conv_b_in, conv_w_dw, conv_b_dw, conv_ln_g, conv_ln_b, conv_w_out, conv_b_out, attn_w_qkv,
           attn_q_norm, attn_k_norm, attn_sinks, attn_w_o, moe_w_rg, moe_b_rg, moe_w_re, moe_b_re, moe_w_gate,
           moe_w_up, moe_w_down, ple_w_gate, ple_w_proj):
    bp, seq, d = x_prompt.shape
    bs, t_new, _ = x_sample.shape
    assert bp == 1 and d == D_MODEL and seq % WINDOW == 0 and bs % SAMPLE_SEQS == 0
    assert seq % (bs * t_new) == 0 and (bs * t_new) % DMA_CHUNK == 0
    depth = norm_mix.shape[0]
    row2 = lambda a: a.reshape(1, -1)

    y_p = x_prompt.reshape(seq, d)
    y_s = x_sample.reshape(bs * t_new, d)
    conv_p, conv_s, k_p, v_p, k_s, v_s = [], [], [], [], [], []

    for i in range(depth):
        j = i // 2
        g_mix = row2(norm_mix[i])
        if i % 2 == 0:
            w_in = conv_w_in[j].astype(BF16)
            b_in = row2(conv_b_in[j])
            tail = (conv_w_dw[j], row2(conv_b_dw[j]), row2(conv_ln_g[j]), row2(conv_ln_b[j]),
                    conv_w_out[j].astype(BF16), row2(conv_b_out[j]))
            u_p = _conv_in(y_p, g_mix, w_in, b_in)
            u_s = _conv_in(y_s, g_mix, w_in, b_in)
            conv_p.append(u_p[seq - CONV_STATE:].reshape(1, CONV_STATE, d))
            y_p = _conv_out_p(u_p, y_p, *tail)
            ys_t, new_state = _conv_out_s(state_conv[j], u_s.reshape(bs, t_new, d),
                                          y_s.reshape(bs, t_new, d).transpose(1, 0, 2), *tail)
            conv_s.append(new_state)
            y_s = ys_t.transpose(1, 0, 2).reshape(bs * t_new, d)
        else:
            w_qkv = attn_w_qkv[j].astype(BF16)
            w_o = attn_w_o[j].astype(BF16)
            iq, iqt = _head_indicator(N_HEADS)
            ik, ikt = _head_indicator(N_KV_HEADS)
            qg = row2(jnp.tile(attn_q_norm[j], N_HEADS))
            kg = row2(jnp.tile(attn_k_norm[j], N_KV_HEADS))
            sinks = attn_sinks[j]
            q1, k1, v1 = _qkv(y_p, g_mix, w_qkv, iq, iqt, ik, ikt, qg, kg)
            q2, k2, v2 = _qkv(y_s, g_mix, w_qkv, iq, iqt, ik, ikt, qg, kg)
            k_p.append(k1[seq - WINDOW:].reshape(1, WINDOW, N_KV_HEADS, HEAD_DIM))
            v_p.append(v1[seq - WINDOW:].reshape(1, WINDOW, N_KV_HEADS, HEAD_DIM))
            y_p = _attn_p(sinks, q1, k1, v1, y_p, _prompt_bias(), w_o)
            n_cols = -(-(SAMPLE_SEQS * (WINDOW + t_new)) // LANES) * LANES
            o_s, nk, nv = _attn_s(sinks, q2, k2, v2, cache_k[j].reshape(bs * WINDOW, KV_DIM),
                                  cache_v[j].reshape(bs * WINDOW, KV_DIM), _sample_bias(t_new, n_cols), t_new)
            k_s.append(nk.reshape(bs, N_KV_HEADS, HEAD_DIM, WINDOW).transpose(0, 3, 1, 2))
            v_s.append(nv.reshape(bs, N_KV_HEADS, HEAD_DIM, WINDOW).transpose(0, 3, 1, 2))
            y_s = _proj_res(o_s, y_s, w_o)

        w_r = jnp.zeros((d, ROUTER_LANES), F32)
        w_r = w_r.at[:, :N_EXPERT_GROUPS].set(moe_w_rg[i]).at[:, EXPERT_LANE0:EXPERT_LANE0 + N_EXPERTS].set(moe_w_re[i])
        b_r = jnp.zeros((1, ROUTER_LANES), F32)
        b_r = b_r.at[0, :N_EXPERT_GROUPS].set(moe_b_rg[i]).at[0, EXPERT_LANE0:EXPERT_LANE0 + N_EXPERTS].set(moe_b_re[i])
        moe = (row2(norm_ffn[i]), jnp.stack(_split_bf16(w_r)), b_r, moe_w_gate, moe_w_up, moe_w_down)
        ple = (row2(norm_ple[i]), ple_w_gate[i].astype(BF16), ple_w_proj[i].astype(BF16))
        y_p, y_s = _moe_ple(i, y_p, y_s, *moe, p_prompt.reshape(depth, seq, PLE_DIM),
                            p_sample[i].reshape(bs * t_new, PLE_DIM), *ple)

    return (y_p.reshape(1, seq, d), y_s.reshape(bs, t_new, d), jnp.stack(conv_p), jnp.stack(conv_s),
            jnp.stack(k_p), jnp.stack(v_p), jnp.stack(k_s), jnp.stack(v_s))
```

```python
import functools

import numpy as np

import jax
import jax.numpy as jnp
from jax import lax
from jax.experimental import pallas as pl
from jax.experimental.pallas import tpu as pltpu

F32 = jnp.float32
BF16 = jnp.bfloat16

D_MODEL = 1024
PLE_DIM = 256
CONV_WIDTH = 31
CONV_STATE = CONV_WIDTH - 1
N_HEADS = 16
N_KV_HEADS = 4
HEAD_DIM = 64
GROUP = N_HEADS // N_KV_HEADS
WINDOW = 128
KV_DIM = N_KV_HEADS * HEAD_DIM
N_EXPERT_GROUPS = 4
EXPERTS_PER_GROUP = 4
N_EXPERTS = 16
D_EXPERT = 256
EPS = 1e-6
NEG_INF = -1e30

LANES = 128
ROUTER_LANES = LANES
EXPERT_LANE0 = N_EXPERT_GROUPS
HALO = 32
CONV_ROWS = 32
NORM_ROWS = 128
CONV_PITCH = D_MODEL // LANES + 1
SAMPLE_SEQS = 8
PAIRS_PER_GROUP = EXPERTS_PER_GROUP * (EXPERTS_PER_GROUP - 1) // 2
N_BUCKETS = N_EXPERT_GROUPS * PAIRS_PER_GROUP
TM = 256
PAY_WIDTH = D_MODEL + LANES
ROW_PITCH = PAY_WIDTH // LANES
DMA_CHUNK = 512
RING = 3
VMEM_LIMIT = 48 * 1024 * 1024


def _row_tile(n):
    return 512 if n % 512 == 0 else n


def _params(*sem):
    return pltpu.CompilerParams(dimension_semantics=sem, vmem_limit_bytes=VMEM_LIMIT)


def _full(shape):
    nd = len(shape)
    return pl.BlockSpec(shape, lambda *_: (0,) * nd)


def _rms(x, g):
    ms = jnp.mean(x * x, axis=-1, keepdims=True)
    return x * lax.rsqrt(ms + EPS) * g


def _sigmoid(x):
    return 1.0 / (1.0 + jnp.exp(-x))


def _mm(a, b):
    return jnp.dot(a, b, preferred_element_type=F32)


def _conv_in_kernel(x_ref, g_ref, w_ref, b_ref, u_ref):
    h = _rms(x_ref[...], g_ref[...]).astype(BF16)
    z = _mm(h, w_ref[...]) + b_ref[...]
    u_ref[...] = z[:, :D_MODEL] * _sigmoid(z[:, D_MODEL:])


def _conv_in(x, g, w, b):
    n = x.shape[0]
    t = _row_tile(n)
    return pl.pallas_call(
        _conv_in_kernel,
        grid=(n // t,),
        in_specs=[pl.BlockSpec((t, D_MODEL), lambda i: (i, 0)), _full(g.shape), _full(w.shape), _full(b.shape)],
        out_specs=pl.BlockSpec((t, D_MODEL), lambda i: (i, 0)),
        out_shape=jax.ShapeDtypeStruct((n, D_MODEL), F32),
        compiler_params=_params("arbitrary"),
        name="conv_in",
    )(x, g, w, b)


def _ln_silu(c, g, b):
    mu = jnp.mean(c, axis=-1, keepdims=True)
    xc = c - mu
    var = jnp.mean(xc * xc, axis=-1, keepdims=True)
    cn = xc * lax.rsqrt(var + EPS) * g + b
    return cn * _sigmoid(cn)


def _conv_out_p_kernel(u_ref, halo_ref, x_ref, wdw_ref, bdw_ref, lng_ref, lnb_ref, wout_ref, bout_ref, y_ref,
                       ubuf, cbuf, hbuf):
    t = u_ref.shape[0]
    i = pl.program_id(0)
    nj = D_MODEL // LANES

    def put(r, j, val):
        ubuf[pl.ds(r * CONV_PITCH + j, 8, stride=CONV_PITCH), :] = val

    for r in range(0, HALO, 8):
        for j in range(nj):
            put(r, j, jnp.where(i > 0, halo_ref[r:r + 8, j * LANES:(j + 1) * LANES], 0.0))

    def fill(rr, carry):
        r = pl.multiple_of(rr * 8, 8)
        for j in range(nj):
            put(r + HALO, j, u_ref[pl.ds(r, 8), j * LANES:(j + 1) * LANES])
        return carry

    lax.fori_loop(0, t // 8, fill, 0)

    def conv_chunk(rr, carry):
        r0 = pl.multiple_of(rr * CONV_ROWS, CONV_ROWS)
        for j in range(nj):
            lanes = slice(j * LANES, (j + 1) * LANES)
            accs = [None] * (CONV_ROWS // 8)
            for k in range(CONV_WIDTH):
                wk = wdw_ref[k:k + 1, lanes]
                for q in range(CONV_ROWS // 8):
                    r = r0 + (HALO - CONV_STATE + k + 8 * q)
                    term = wk * ubuf[pl.ds(r * CONV_PITCH + j, 8, stride=CONV_PITCH), :]
                    accs[q] = term if k == 0 else accs[q] + term
            for q in range(CONV_ROWS // 8):
                cbuf[pl.ds(r0 + 8 * q, 8), lanes] = accs[q]
        return carry

    lax.fori_loop(0, t // CONV_ROWS, conv_chunk, 0)

    def norm_chunk(rr, carry):
        r0 = pl.multiple_of(rr * NORM_ROWS, NORM_ROWS)
        c = cbuf[pl.ds(r0, NORM_ROWS), :] + bdw_ref[...]
        hbuf[pl.ds(r0, NORM_ROWS), :] = _ln_silu(c, lng_ref[...], lnb_ref[...]).astype(BF16)
        return carry

    lax.fori_loop(0, t // NORM_ROWS, norm_chunk, 0)
    y_ref[...] = x_ref[...] + _mm(hbuf[...], wout_ref[...]) + bout_ref[...]


def _conv_out_p(u, x, wdw, bdw, lng, lnb, wout, bout):
    n = u.shape[0]
    t = _row_tile(n)
    hb = t // HALO
    row = lambda i: (i, 0)
    return pl.pallas_call(
        _conv_out_p_kernel,
        grid=(n // t,),
        in_specs=[pl.BlockSpec((t, D_MODEL), row),
                  pl.BlockSpec((HALO, D_MODEL), lambda i: (jnp.maximum(i * hb - 1, 0), 0)),
                  pl.BlockSpec((t, D_MODEL), row),
                  _full(wdw.shape), _full(bdw.shape), _full(lng.shape), _full(lnb.shape),
                  _full(wout.shape), _full(bout.shape)],
        out_specs=pl.BlockSpec((t, D_MODEL), row),
        out_shape=jax.ShapeDtypeStruct((n, D_MODEL), F32),
        scratch_shapes=[pltpu.VMEM(((t + HALO) * CONV_PITCH, LANES), F32), pltpu.VMEM((t, D_MODEL), F32),
                        pltpu.VMEM((t, D_MODEL), BF16)],
        compiler_params=_params("arbitrary"),
        name="conv_out_prompt",
    )(u, u, x, wdw, bdw, lng, lnb, wout, bout)


def _conv_out_s_kernel(st_ref, u_ref, x_ref, wdw_ref, bdw_ref, lng_ref, lnb_ref, wout_ref, bout_ref, y_ref, ns_ref,
                       wbuf, cbuf):
    bb, nt, _ = u_ref.shape
    win = CONV_STATE + nt
    nj = D_MODEL // LANES
    seq_pitch = win * CONV_PITCH

    def put(b, r0, rows, j, val):
        wbuf[pl.ds((b * win + r0) * CONV_PITCH + j, rows, stride=CONV_PITCH), :] = val

    for b in range(bb):
        for j in range(nj):
            lanes = slice(j * LANES, (j + 1) * LANES)
            for r0 in range(0, CONV_STATE, 8):
                rows = min(8, CONV_STATE - r0)
                put(b, r0, rows, j, st_ref[b, r0:r0 + rows, lanes])
            put(b, CONV_STATE, nt, j, u_ref[b, :, lanes])

    for b in range(bb):
        for r0 in range(0, CONV_STATE, 8):
            rows = min(8, CONV_STATE - r0)
            for j in range(nj):
                src = (b * win + nt + r0) * CONV_PITCH + j
                ns_ref[b, r0:r0 + rows, j * LANES:(j + 1) * LANES] = wbuf[pl.ds(src, rows, stride=CONV_PITCH), :]

    for t in range(nt):
        for b0 in range(0, bb, 8):
            for j in range(nj):
                lanes = slice(j * LANES, (j + 1) * LANES)
                acc = None
                for k in range(CONV_WIDTH):
                    start = (b0 * win + t + k) * CONV_PITCH + j
                    term = wdw_ref[k:k + 1, lanes] * wbuf[pl.ds(start, 8, stride=seq_pitch), :]
                    acc = term if acc is None else acc + term
                cbuf[t * bb + b0:t * bb + b0 + 8, lanes] = acc
    c = cbuf[...] + bdw_ref[...]
    h = _ln_silu(c, lng_ref[...], lnb_ref[...]).astype(BF16)
    y = _mm(h, wout_ref[...]) + bout_ref[...]
    for t in range(nt):
        y_ref[t] = x_ref[t] + y[t * bb:(t + 1) * bb]


def _conv_out_s(state, u, x, wdw, bdw, lng, lnb, wout, bout):
    b, nt, _ = u.shape
    bb = 16 if b % 16 == 0 else b
    assert bb % 8 == 0
    win = CONV_STATE + nt
    seq = lambda i: (i, 0, 0)
    tm = lambda i: (0, i, 0)
    return pl.pallas_call(
        _conv_out_s_kernel,
        grid=(b // bb,),
        in_specs=[pl.BlockSpec((bb, CONV_STATE, D_MODEL), seq), pl.BlockSpec((bb, nt, D_MODEL), seq),
                  pl.BlockSpec((nt, bb, D_MODEL), tm),
                  _full(wdw.shape), _full(bdw.shape), _full(lng.shape), _full(lnb.shape),
                  _full(wout.shape), _full(bout.shape)],
        out_specs=[pl.BlockSpec((nt, bb, D_MODEL), tm), pl.BlockSpec((bb, CONV_STATE, D_MODEL), seq)],
        out_shape=[jax.ShapeDtypeStruct((nt, b, D_MODEL), F32), jax.ShapeDtypeStruct((b, CONV_STATE, D_MODEL), F32)],
        scratch_shapes=[pltpu.VMEM((bb * win * CONV_PITCH, LANES), F32), pltpu.VMEM((nt * bb, D_MODEL), F32)],
        compiler_params=_params("arbitrary"),
        name="conv_out_sample",
    )(state, u, x, wdw, bdw, lng, lnb, wout, bout)


def _route(lg):
    big = 3.0e38
    lane = lax.broadcasted_iota(jnp.int32, lg.shape, 1)
    lanef = lane.astype(F32)
    is_g = lane < N_EXPERT_GROUPS
    gl = jnp.where(is_g, lg, -big)
    gmax = jnp.max(gl, axis=1, keepdims=True)
    gsum = jnp.sum(jnp.where(is_g, jnp.exp(gl - gmax), 0.0), axis=1, keepdims=True)
    g_w = 1.0 / gsum
    g_idx = jnp.min(jnp.where(gl == gmax, lanef, big), axis=1, keepdims=True)
    rel = lanef - float(EXPERT_LANE0) - g_idx * float(EXPERTS_PER_GROUP)
    in_grp = jnp.where(rel >= 0.0, jnp.where(rel < float(EXPERTS_PER_GROUP), 1.0, 0.0), 0.0) > 0.5
    el = jnp.where(in_grp, lg, -big)
    e1 = jnp.max(el, axis=1, keepdims=True)
    i1 = jnp.min(jnp.where(el == e1, lanef, big), axis=1, keepdims=True)
    el2 = jnp.where(lanef == i1, -big, el)
    e2 = jnp.max(el2, axis=1, keepdims=True)
    i2 = jnp.min(jnp.where(el2 == e2, lanef, big), axis=1, keepdims=True)
    tt = jnp.exp(e2 - e1)
    w1 = g_w / (1.0 + tt)
    w2 = g_w * tt / (1.0 + tt)
    base = float(EXPERT_LANE0) + g_idx * float(EXPERTS_PER_GROUP)
    a = jnp.minimum(i1, i2) - base
    b = jnp.maximum(i1, i2) - base
    pair = a * (7.0 - a) * 0.5 + (b - a - 1.0)
    first_is_lo = i1 < i2
    return (g_idx * float(PAIRS_PER_GROUP) + pair, jnp.where(first_is_lo, w1, w2), jnp.where(first_is_lo, w2, w1))


def _to_row_linear(dst_ref, src_ref, n_tiles):
    def body(g, carry):
        r = pl.multiple_of(g * 8, 8)
        for j in range(n_tiles):
            dst_ref[pl.ds(r * ROW_PITCH + j, 8, stride=ROW_PITCH), :] = src_ref[pl.ds(r, 8), j * LANES:(j + 1) * LANES]
        return carry

    lax.fori_loop(0, src_ref.shape[0] // 8, body, 0)


def _from_row_linear(dst_ref, src_ref, n_tiles, pitch):
    def body(g, carry):
        r = pl.multiple_of(g * 8, 8)
        for j in range(n_tiles):
            dst_ref[pl.ds(r, 8), j * LANES:(j + 1) * LANES] = src_ref[pl.ds(r * pitch + j, 8, stride=pitch), :]
        return carry

    lax.fori_loop(0, dst_ref.shape[0] // 8, body, 0)


def _pick(i, n_prompt_tiles, prompt_ref, sample_ref):
    return jnp.where(i < n_prompt_tiles, prompt_ref[...], sample_ref[...])


def _route_kernel(n_prompt_tiles, yp_ref, ys_ref, g_ref, wr_ref, br_ref, tri_ref, pay_ref, meta_ref, cnt_ref, carry,
                  rowbuf):
    i = pl.program_id(0)

    @pl.when(i == 0)
    def _():
        carry[...] = jnp.zeros_like(carry)

    xf = _rms(_pick(i, n_prompt_tiles, yp_ref, ys_ref), g_ref[...])
    x_hi, x_lo = _split_bf16(xf)
    logits = _mm(x_hi, wr_ref[0]) + (_mm(x_lo, wr_ref[0]) + _mm(x_hi, wr_ref[1])) + br_ref[...]
    bucket, w_lo, w_hi = _route(logits)
    lane = lax.broadcasted_iota(jnp.int32, logits.shape, 1)
    onehot = jnp.where(lane.astype(F32) == bucket, 1.0, 0.0)
    before = _mm(tri_ref[...], onehot.astype(BF16)) + carry[...]
    rank = jnp.sum(onehot * before, axis=1, keepdims=True)
    carry[...] += jnp.sum(onehot, axis=0, keepdims=True)
    cnt_ref[...] = carry[...]
    meta = jnp.where(lane == 0, bucket, jnp.where(lane == 1, rank, 0.0))
    meta_ref[...] = jnp.transpose(meta)[:8, :].astype(jnp.int32)

    rowbuf[:, :D_MODEL] = xf
    rowbuf[:, D_MODEL:] = jnp.where(lane == 0, w_lo, jnp.where(lane == 1, w_hi, 0.0))
    _to_row_linear(pay_ref, rowbuf, ROW_PITCH)


def _route_call(y_p, y_s, g, wr, br, tri):
    t = y_s.shape[0]
    npt = y_p.shape[0] // t
    n = y_p.shape[0] + t
    pidx = lambda i: (jnp.minimum(i, npt - 1), 0)
    return pl.pallas_call(
        functools.partial(_route_kernel, npt),
        grid=(npt + 1,),
        in_specs=[pl.BlockSpec((t, D_MODEL), pidx), _full(y_s.shape), _full(g.shape), _full(wr.shape),
                  _full(br.shape), _full(tri.shape)],
        out_specs=[pl.BlockSpec((t * ROW_PITCH, LANES), lambda i: (i, 0)),
                   pl.BlockSpec((8, t), lambda i: (0, i)), _full((1, ROUTER_LANES))],
        out_shape=[jax.ShapeDtypeStruct((n * ROW_PITCH, LANES), F32),
                   jax.ShapeDtypeStruct((8, n), jnp.int32), jax.ShapeDtypeStruct((1, ROUTER_LANES), F32)],
        scratch_shapes=[pltpu.VMEM((1, ROUTER_LANES), F32), pltpu.VMEM((t, PAY_WIDTH), F32)],
        compiler_params=_params("arbitrary"),
        name="route",
    )(y_p, y_s, g, wr, br, tri)


def _permute_kernel(pos_ref, zstart_ref, nz_ref, used_ref, src_ref, dst_ref, zbuf, ring, zsem, in_sems, out_sems):
    zbuf[...] = jnp.zeros_like(zbuf)

    def zero_tile(start):
        return pltpu.make_async_copy(zbuf, dst_ref.at[pl.ds(pl.multiple_of(start * ROW_PITCH, 8), TM * ROW_PITCH)], zsem)

    for b in range(N_BUCKETS):
        @pl.when(nz_ref[b] > 0)
        def _():
            zero_tile(zstart_ref[b]).start()
    n_tiles = dst_ref.shape[0] // (TM * ROW_PITCH)

    def start_unused(i, carry):
        zero_tile(i * TM).start()
        return carry

    lax.fori_loop(used_ref[0], n_tiles, start_unused, 0)
    for b in range(N_BUCKETS):
        @pl.when(nz_ref[b] > 0)
        def _():
            zero_tile(0).wait()

    def wait_unused(i, carry):
        zero_tile(0).wait()
        return carry

    lax.fori_loop(used_ref[0], n_tiles, wait_unused, 0)

    chunk_rows = DMA_CHUNK * ROW_PITCH
    n_chunks = src_ref.shape[0] // chunk_rows

    def fetch(c, slot):
        return pltpu.make_async_copy(src_ref.at[pl.ds(pl.multiple_of(c * chunk_rows, 8), chunk_rows)], ring.at[slot],
                                     in_sems.at[slot])

    def drain(slot):
        pltpu.make_async_copy(ring.at[slot], dst_ref.at[pl.ds(0, chunk_rows)], out_sems.at[slot]).wait()

    fetch(0, 0).start()

    def step(c, carry):
        slot = lax.rem(c, RING)
        nxt = lax.rem(c + 1, RING)
        fetch(c, slot).wait()

        @pl.when(c >= RING - 1)
        def _():
            drain(nxt)

        @pl.when(c + 1 < n_chunks)
        def _():
            fetch(c + 1, nxt).start()

        def send(jj, carry2):
            row = pos_ref[c * DMA_CHUNK + jj] * ROW_PITCH
            pltpu.make_async_copy(ring.at[slot, pl.ds(jj * ROW_PITCH, ROW_PITCH)], dst_ref.at[pl.ds(row, ROW_PITCH)],
                                  out_sems.at[slot]).start()
            return carry2

        lax.fori_loop(0, DMA_CHUNK, send, 0, unroll=8)
        return carry

    lax.fori_loop(0, n_chunks, step, 0)
    for c in range(n_chunks - (RING - 1), n_chunks):
        drain(c % RING)


def _permute(pos, zstart, nz, used, payload, n_rows_out):
    assert payload.shape[0] // (DMA_CHUNK * ROW_PITCH) >= RING
    return pl.pallas_call(
        _permute_kernel,
        grid_spec=pltpu.PrefetchScalarGridSpec(
            num_scalar_prefetch=4, grid=(1,),
            in_specs=[pl.BlockSpec(memory_space=pl.ANY)],
            out_specs=pl.BlockSpec(memory_space=pl.ANY),
            scratch_shapes=[pltpu.VMEM((TM * ROW_PITCH, LANES), payload.dtype),
                            pltpu.VMEM((RING, DMA_CHUNK * ROW_PITCH, LANES), payload.dtype),
                            pltpu.SemaphoreType.DMA(()), pltpu.SemaphoreType.DMA((RING,)),
                            pltpu.SemaphoreType.DMA((RING,))]),
        out_shape=jax.ShapeDtypeStruct((n_rows_out * ROW_PITCH, LANES), payload.dtype),
        compiler_params=_params("arbitrary"),
        name="permute",
    )(pos, zstart, nz, used, payload)


def _silu(x):
    return x * _sigmoid(x)


def _experts_kernel(tix_ref, lo_ref, hi_ref, valid_ref, fresh_ref, xs_ref, wgl_ref, wgh_ref, wul_ref, wuh_ref,
                    wdl_ref, wdh_ref, ys_ref, wup_s, wdn_s, xbuf, ybuf):
    i = pl.program_id(0)

    @pl.when(fresh_ref[i] > 0)
    def _():
        for s, ref in enumerate((wgl_ref, wgh_ref, wul_ref, wuh_ref)):
            wup_s[s] = ref[0, 0].astype(BF16)
        for s, ref in enumerate((wdl_ref, wdh_ref)):
            wdn_s[s] = ref[0, 0].astype(BF16)

    @pl.when(valid_ref[i] == 0)
    def _():
        ys_ref[...] = jnp.zeros_like(ys_ref)

    @pl.when(valid_ref[i] > 0)
    def _():
        _from_row_linear(xbuf, xs_ref, ROW_PITCH, ROW_PITCH)
        x = xbuf[:, :D_MODEL].astype(BF16)
        gates = xbuf[:, D_MODEL:]
        h_lo = _silu(_mm(x, wup_s[0])) * _mm(x, wup_s[2]) * gates[:, 0:1]
        h_hi = _silu(_mm(x, wup_s[1])) * _mm(x, wup_s[3]) * gates[:, 1:2]
        ybuf[:, :D_MODEL] = _mm(h_lo.astype(BF16), wdn_s[0]) + _mm(h_hi.astype(BF16), wdn_s[1])
        ybuf[:, D_MODEL:] = jnp.zeros((TM, LANES), F32)
        _to_row_linear(ys_ref, ybuf, ROW_PITCH)


def _experts(layer, tix, lo, hi, valid, fresh, xs, wg, wu, wd):
    n_tiles = xs.shape[0] // (TM * ROW_PITCH)
    row = lambda i, tix, lo, hi, valid, fresh: (tix[i], 0)
    row_out = lambda i, tix, lo, hi, valid, fresh: (i, 0)
    e_lo = lambda i, tix, lo, hi, valid, fresh: (layer, lo[i], 0, 0)
    e_hi = lambda i, tix, lo, hi, valid, fresh: (layer, hi[i], 0, 0)
    up = pl.BlockSpec((1, 1, D_MODEL, D_EXPERT), e_lo), pl.BlockSpec((1, 1, D_MODEL, D_EXPERT), e_hi)
    down = pl.BlockSpec((1, 1, D_EXPERT, D_MODEL), e_lo), pl.BlockSpec((1, 1, D_EXPERT, D_MODEL), e_hi)
    return pl.pallas_call(
        _experts_kernel,
        grid_spec=pltpu.PrefetchScalarGridSpec(
            num_scalar_prefetch=5, grid=(n_tiles,),
            in_specs=[pl.BlockSpec((TM * ROW_PITCH, LANES), row), *up, *up, *down],
            out_specs=pl.BlockSpec((TM * ROW_PITCH, LANES), row_out),
            scratch_shapes=[pltpu.VMEM((4, D_MODEL, D_EXPERT), BF16), pltpu.VMEM((2, D_EXPERT, D_MODEL), BF16),
                            pltpu.VMEM((TM, PAY_WIDTH), F32), pltpu.VMEM((TM, PAY_WIDTH), F32)]),
        out_shape=jax.ShapeDtypeStruct(xs.shape, F32),
        compiler_params=_params("arbitrary"),
        name="experts",
    )(tix, lo, hi, valid, fresh, xs, wg, wg, wu, wu, wd, wd)


def _ple_kernel(n_prompt_tiles, pos_ref, yp_ref, ys_ref, moe_ref, pp_ref, ps_ref, g_ref, wg_ref, wp_ref, op_ref, os_ref,
                mring, mbuf, sems):
    i = pl.program_id(0)
    t = mbuf.shape[0]
    w = D_MODEL // LANES

    def gather(tile, slot):
        def body(jj, carry):
            row = pos_ref[tile * t + jj] * ROW_PITCH
            pltpu.make_async_copy(moe_ref.at[pl.ds(row, w)], mring.at[slot, pl.ds(jj * w, w)], sems.at[slot]).start()
            return carry

        lax.fori_loop(0, t, body, 0, unroll=8)

    def wait_tile(slot):
        pltpu.make_async_copy(moe_ref.at[pl.ds(0, t * w)], mring.at[slot], sems.at[slot]).wait()

    @pl.when(i == 0)
    def _():
        gather(0, 0)

    slot = lax.rem(i, 2)
    nslot = 1 - slot
    wait_tile(slot)
    _from_row_linear(mbuf, mring.at[slot], w, w)
    y2 = _pick(i, n_prompt_tiles, yp_ref, ys_ref) + mbuf[...]
    hn = _rms(y2, g_ref[...]).astype(BF16)
    gt = _sigmoid(_mm(hn, wg_ref[...]))
    pr = _mm(jnp.where(i < n_prompt_tiles, pp_ref[0], ps_ref[...]).astype(BF16), wp_ref[...])
    y3 = y2 + gt * pr
    nxt = jnp.minimum(i + 1, n_prompt_tiles) * t
    for jj in range(t):
        row = pos_ref[nxt + jj] * ROW_PITCH
        pltpu.make_async_copy(moe_ref.at[pl.ds(row, w)], mring.at[nslot, pl.ds(jj * w, w)], sems.at[nslot]).start()

    @pl.when(i < n_prompt_tiles)
    def _():
        op_ref[...] = y3

    @pl.when(i >= n_prompt_tiles)
    def _():
        os_ref[...] = y3
        wait_tile(nslot)


def _ple(layer, pos, y_p, y_s, ys_sorted, p_p, p_s, g, wg, wp):
    t = y_s.shape[0]
    npt = y_p.shape[0] // t
    w = D_MODEL // LANES
    pidx = lambda i, pos: (jnp.minimum(i, npt - 1), 0)
    full = lambda shape: pl.BlockSpec(shape, lambda i, pos: (0,) * len(shape))
    return pl.pallas_call(
        functools.partial(_ple_kernel, npt),
        grid_spec=pltpu.PrefetchScalarGridSpec(
            num_scalar_prefetch=1, grid=(npt + 1,),
            in_specs=[pl.BlockSpec((t, D_MODEL), pidx), full(y_s.shape), pl.BlockSpec(memory_space=pl.ANY),
                      pl.BlockSpec((1, t, PLE_DIM), lambda i, pos: (layer, jnp.minimum(i, npt - 1), 0)),
                      full(p_s.shape), full(g.shape), full(wg.shape), full(wp.shape)],
            out_specs=[pl.BlockSpec((t, D_MODEL), pidx), full(y_s.shape)],
            scratch_shapes=[pltpu.VMEM((2, t * w, LANES), F32), pltpu.VMEM((t, D_MODEL), F32),
                            pltpu.SemaphoreType.DMA((2,))]),
        out_shape=[jax.ShapeDtypeStruct(y_p.shape, F32), jax.ShapeDtypeStruct(y_s.shape, F32)],
        compiler_params=_params("arbitrary"),
        name="ple",
    )(pos, y_p, y_s, ys_sorted, p_p, p_s, g, wg, wp)


def _bucket_experts():
    pairs = [(a, b) for a in range(EXPERTS_PER_GROUP) for b in range(a + 1, EXPERTS_PER_GROUP)]
    lo = [g * EXPERTS_PER_GROUP + a for g in range(N_EXPERT_GROUPS) for a, _ in pairs]
    hi = [g * EXPERTS_PER_GROUP + b for g in range(N_EXPERT_GROUPS) for _, b in pairs]
    return jnp.array(lo, jnp.int32), jnp.array(hi, jnp.int32)


def _moe_ple(layer, y_p, y_s, gffn, wr, br, wg, wu, wd, p_p, p_s, gple, pwg, pwp):
    t = y_s.shape[0]
    n = y_p.shape[0] + t
    tri = jnp.asarray(np.tri(t, t, -1, dtype=np.float32), BF16)
    payload, meta, counts = _route_call(y_p, y_s, gffn, wr, br, tri)

    bucket, rank = meta[0], meta[1]
    cnt = counts[0, :N_BUCKETS].astype(jnp.int32)
    padded = (cnt + TM - 1) // TM * TM
    ends = jnp.cumsum(padded)
    starts = ends - padded
    in_bucket = bucket[:, None] == jnp.arange(N_BUCKETS, dtype=jnp.int32)[None, :]
    pos = rank + jnp.sum(jnp.where(in_bucket, starts[None, :], 0), axis=1)
    n_rows = n + N_BUCKETS * TM
    n_tiles = n_rows // TM
    tile_start = jnp.arange(n_tiles, dtype=jnp.int32) * TM
    valid = tile_start < ends[-1]
    used = (ends[-1] // TM).astype(jnp.int32)
    tix = jnp.arange(n_tiles, dtype=jnp.int32)
    tile_bucket = jnp.sum((ends[None, :] <= (jnp.minimum(tix, used - 1) * TM)[:, None]).astype(jnp.int32), axis=1)
    lo_tab, hi_tab = _bucket_experts()
    lo = jnp.take(lo_tab, tile_bucket)
    hi = jnp.take(hi_tab, tile_bucket)
    prev_bucket = jnp.concatenate([jnp.full((1,), -1, jnp.int32), tile_bucket[:-1]])
    fresh = valid & (tile_bucket != prev_bucket)

    xs = _permute(pos, (ends - TM).astype(jnp.int32), (cnt > 0).astype(jnp.int32), used.reshape(1), payload, n_rows)
    ys = _experts(layer, jnp.minimum(tix, used - 1), lo, hi, valid.astype(jnp.int32), fresh.astype(jnp.int32), xs, wg,
                  wu, wd)
    return _ple(layer, pos, y_p, y_s, ys, p_p, p_s, gple, pwg, pwp)


def _split_bf16(a):
    hi = a.astype(BF16)
    lo = (a - hi.astype(F32)).astype(BF16)
    return hi, lo


def _head_norm(a, ind_ref, indt_ref, gain):
    hi, lo = _split_bf16(a * a)
    ss = _mm(hi, ind_ref[...]) + _mm(lo, ind_ref[...])
    inv = lax.rsqrt(ss * (1.0 / HEAD_DIM) + EPS)
    ihi, ilo = _split_bf16(inv)
    invb = _mm(ihi, indt_ref[...]) + _mm(ilo, indt_ref[...])
    return a * invb * gain


def _qkv_kernel(x_ref, g_ref, w_ref, iq_ref, iqt_ref, ik_ref, ikt_ref, qg_ref, kg_ref, q_ref, k_ref, v_ref):
    h = _rms(x_ref[...], g_ref[...]).astype(BF16)
    qkv = _mm(h, w_ref[...])
    nq = N_HEADS * HEAD_DIM
    q = _head_norm(qkv[:, :nq], iq_ref, iqt_ref, qg_ref[...])
    k = _head_norm(qkv[:, nq:nq + KV_DIM], ik_ref, ikt_ref, kg_ref[...])
    q_ref[...] = (q * (HEAD_DIM ** -0.5)).astype(BF16)
    k_ref[...] = k
    v_ref[...] = qkv[:, nq + KV_DIM:]


def _qkv(x, g, w, iq, iqt, ik, ikt, qg, kg):
    n = x.shape[0]
    t = _row_tile(n)
    row = lambda i: (i, 0)
    return pl.pallas_call(
        _qkv_kernel,
        grid=(n // t,),
        in_specs=[pl.BlockSpec((t, D_MODEL), row)] + [_full(a.shape) for a in (g, w, iq, iqt, ik, ikt, qg, kg)],
        out_specs=[pl.BlockSpec((t, N_HEADS * HEAD_DIM), row), pl.BlockSpec((t, KV_DIM), row),
                   pl.BlockSpec((t, KV_DIM), row)],
        out_shape=[jax.ShapeDtypeStruct((n, N_HEADS * HEAD_DIM), BF16), jax.ShapeDtypeStruct((n, KV_DIM), F32),
                   jax.ShapeDtypeStruct((n, KV_DIM), F32)],
        compiler_params=_params("arbitrary"),
        name="qkv",
    )(x, g, w, iq, iqt, ik, ikt, qg, kg)


def _dup_heads(a):
    out = []
    for s in range(KV_DIM // LANES):
        sl = a[:, s * LANES:(s + 1) * LANES]
        sw = pltpu.roll(sl, HEAD_DIM, axis=1)
        low = lax.broadcasted_iota(jnp.int32, sl.shape, 1) < HEAD_DIM
        out.append(jnp.where(low, sl, sw))
        out.append(jnp.where(low, sw, sl))
    return jnp.concatenate(out, axis=1).astype(BF16)


def _attend(q_rows, k2, v2, bias_of, sink_of, extra_mask):
    m_rows = q_rows.shape[0]
    low_q = lax.broadcasted_iota(jnp.int32, (m_rows, LANES), 1) < HEAD_DIM
    low_k = lax.broadcasted_iota(jnp.int32, (k2.shape[0], LANES), 1) < HEAD_DIM
    zero_q = jnp.zeros((m_rows, LANES), BF16)
    zero_k = jnp.zeros((k2.shape[0], LANES), BF16)
    slabs = []
    for g in range(N_KV_HEADS):
        kg = k2[:, g * LANES:(g + 1) * LANES]
        vg = v2[:, g * LANES:(g + 1) * LANES]
        v_lo = jnp.where(low_k, vg, zero_k)
        v_hi = jnp.where(low_k, zero_k, vg)
        lhs = []
        for a in range(GROUP):
            h = g * GROUP + a
            qs = q_rows[:, (h // 2) * LANES:(h // 2 + 1) * LANES]
            lhs.append(jnp.where(low_q, qs, zero_q) if h % 2 == 0 else jnp.where(low_q, zero_q, qs))
        s = lax.dot_general(jnp.concatenate(lhs, axis=0), kg, (((1,), (1,)), ((), ())), preferred_element_type=F32)
        probs, rinv = [], []
        for a in range(GROUP):
            h = g * GROUP + a
            sa = s[a * m_rows:(a + 1) * m_rows] + bias_of(h)
            if extra_mask is not None:
                sa = jnp.where(extra_mask, NEG_INF, sa)
            sink = sink_of(h)
            m = jnp.maximum(jnp.max(sa, axis=1, keepdims=True), sink)
            p = jnp.exp(sa - m)
            den = jnp.sum(p, axis=1, keepdims=True) + jnp.exp(sink - m)
            probs.append(p.astype(BF16))
            rinv.append(1.0 / den)
        for sp in range(GROUP // 2):
            o = _mm(probs[2 * sp], v_lo) + _mm(probs[2 * sp + 1], v_hi)
            slabs.append(o * jnp.where(low_q, rinv[2 * sp], rinv[2 * sp + 1]))
    return jnp.concatenate(slabs, axis=1)


def _attn_p_kernel(sink_ref, q_ref, k_ref, v_ref, x_ref, bias_ref, wo_ref, y_ref, kbuf, vbuf, obuf):
    t = q_ref.shape[0]
    i = pl.program_id(0)

    @pl.when(i == 0)
    def _():
        kbuf[0:WINDOW, :] = jnp.zeros((WINDOW, 2 * KV_DIM), BF16)
        vbuf[0:WINDOW, :] = jnp.zeros((WINDOW, 2 * KV_DIM), BF16)

    @pl.when(i > 0)
    def _():
        kbuf[0:WINDOW, :] = kbuf[t:t + WINDOW, :]
        vbuf[0:WINDOW, :] = vbuf[t:t + WINDOW, :]

    kbuf[WINDOW:, :] = _dup_heads(k_ref[...])
    vbuf[WINDOW:, :] = _dup_heads(v_ref[...])
    col = lax.broadcasted_iota(jnp.int32, (WINDOW, 2 * WINDOW), 1)
    for j in range(t // WINDOW):
        rows = slice(j * WINDOW, (j + 1) * WINDOW)
        keys = slice(j * WINDOW, (j + 2) * WINDOW)
        extra = jnp.logical_and(i == 0, col < WINDOW) if j == 0 else None
        o = _attend(q_ref[rows, :], kbuf[keys, :], vbuf[keys, :], lambda h: bias_ref[h], lambda h: sink_ref[h], extra)
        obuf[rows, :] = o.astype(BF16)
    y_ref[...] = x_ref[...] + _mm(obuf[...], wo_ref[...])


def _attn_p(sinks, q, k, v, x, bias, wo):
    n = q.shape[0]
    t = _row_tile(n)
    row = lambda i: (i, 0)
    return pl.pallas_call(
        _attn_p_kernel,
        grid=(n // t,),
        in_specs=[pl.BlockSpec(memory_space=pltpu.SMEM),
                  pl.BlockSpec((t, N_HEADS * HEAD_DIM), row), pl.BlockSpec((t, KV_DIM), row),
                  pl.BlockSpec((t, KV_DIM), row), pl.BlockSpec((t, D_MODEL), row),
                  _full(bias.shape), _full(wo.shape)],
        out_specs=pl.BlockSpec((t, D_MODEL), row),
        out_shape=jax.ShapeDtypeStruct((n, D_MODEL), F32),
        scratch_shapes=[pltpu.VMEM((t + WINDOW, 2 * KV_DIM), BF16), pltpu.VMEM((t + WINDOW, 2 * KV_DIM), BF16),
                        pltpu.VMEM((t, N_HEADS * HEAD_DIM), BF16)],
        compiler_params=_params("arbitrary"),
        name="attn_prompt",
    )(sinks, q, k, v, x, bias, wo)


def _attn_s_kernel(sink_ref, q_ref, k_ref, v_ref, ck_ref, cv_ref, bias_ref, o_ref, nk_ref, nv_ref, shift_buf):
    pad = bias_ref.shape[2] - ck_ref.shape[0] - k_ref.shape[0]
    zpad = jnp.zeros((pad, KV_DIM), F32)
    k2 = _dup_heads(jnp.concatenate([ck_ref[...], k_ref[...], zpad], axis=0))
    v2 = _dup_heads(jnp.concatenate([cv_ref[...], v_ref[...], zpad], axis=0))
    o = _attend(q_ref[...], k2, v2, lambda h: bias_ref[h], lambda h: sink_ref[h], None)
    o_ref[...] = o.astype(BF16)
    t_new = k_ref.shape[0] // SAMPLE_SEQS
    for new_ref, old_ref, add_ref in ((nk_ref, ck_ref, k_ref), (nv_ref, cv_ref, v_ref)):
        for b in range(SAMPLE_SEQS):
            r = b * WINDOW
            shift_buf[0:WINDOW - t_new, :] = old_ref[r + t_new:r + WINDOW, :]
            shift_buf[WINDOW - t_new:WINDOW, :] = add_ref[b * t_new:(b + 1) * t_new, :]
            new_ref[b] = jnp.transpose(shift_buf[...])


def _attn_s(sinks, q, k, v, ck, cv, bias, t_new):
    n = q.shape[0]
    rows = SAMPLE_SEQS * t_new
    crow = SAMPLE_SEQS * WINDOW
    row = lambda i: (i, 0)
    return pl.pallas_call(
        _attn_s_kernel,
        grid=(n // rows,),
        in_specs=[pl.BlockSpec(memory_space=pltpu.SMEM),
                  pl.BlockSpec((rows, N_HEADS * HEAD_DIM), row), pl.BlockSpec((rows, KV_DIM), row),
                  pl.BlockSpec((rows, KV_DIM), row), pl.BlockSpec((crow, KV_DIM), row),
                  pl.BlockSpec((crow, KV_DIM), row), _full(bias.shape)],
        out_specs=[pl.BlockSpec((rows, N_HEADS * HEAD_DIM), row),
                   pl.BlockSpec((SAMPLE_SEQS, KV_DIM, WINDOW), lambda i: (i, 0, 0)),
                   pl.BlockSpec((SAMPLE_SEQS, KV_DIM, WINDOW), lambda i: (i, 0, 0))],
        out_shape=[jax.ShapeDtypeStruct((n, N_HEADS * HEAD_DIM), BF16),
                   jax.ShapeDtypeStruct((ck.shape[0] // WINDOW, KV_DIM, WINDOW), F32),
                   jax.ShapeDtypeStruct((cv.shape[0] // WINDOW, KV_DIM, WINDOW), F32)],
        scratch_shapes=[pltpu.VMEM((WINDOW, KV_DIM), F32)],
        compiler_params=_params("arbitrary"),
        name="attn_sample",
    )(sinks, q, k, v, ck, cv, bias)


def _proj_res_kernel(o_ref, x_ref, w_ref, y_ref):
    y_ref[...] = x_ref[...] + _mm(o_ref[...], w_ref[...])


def _proj_res(o, x, w):
    n = o.shape[0]
    t = _row_tile(n)
    row = lambda i: (i, 0)
    return pl.pallas_call(
        _proj_res_kernel,
        grid=(n // t,),
        in_specs=[pl.BlockSpec((t, o.shape[1]), row), pl.BlockSpec((t, D_MODEL), row), _full(w.shape)],
        out_specs=pl.BlockSpec((t, D_MODEL), row),
        out_shape=jax.ShapeDtypeStruct((n, D_MODEL), F32),
        compiler_params=_params("arbitrary"),
        name="proj_res",
    )(o, x, w)


def _alibi_slopes():
    return np.exp2(-8.0 * np.arange(1, N_HEADS + 1, dtype=np.float64) / N_HEADS).astype(np.float32)


def _band_bias(dist, allowed):
    b = -(_alibi_slopes()[:, None, None] * dist.astype(np.float32)[None])
    return jnp.asarray(np.where(allowed[None], b, np.float32(NEG_INF)).astype(np.float32))


def _prompt_bias():
    dist = WINDOW + np.arange(WINDOW)[:, None] - np.arange(2 * WINDOW)[None, :]
    return _band_bias(dist, (dist >= 0) & (dist <= WINDOW))


def _sample_bias(t_new, n_cols):
    c = np.arange(n_cols)
    n_cache = SAMPLE_SEQS * WINDOW
    n_new = SAMPLE_SEQS * t_new
    is_cache = c < n_cache
    is_new = (c >= n_cache) & (c < n_cache + n_new)
    seq_c = np.where(is_cache, c // WINDOW, (c - n_cache) // t_new)
    pos_c = np.where(is_cache, c % WINDOW, WINDOW + (c - n_cache) % t_new)
    r = np.arange(n_new)
    seq_r, tok_r = r // t_new, r % t_new
    dist = WINDOW + tok_r[:, None] - pos_c[None, :]
    allowed = (seq_r[:, None] == seq_c[None, :]) & (is_cache | is_new)[None, :] & (dist >= 0) & (dist <= WINDOW)
    return _band_bias(dist, allowed)


def _head_indicator(n_heads):
    ch = np.arange(n_heads * HEAD_DIM) // HEAD_DIM
    ind = (ch[:, None] == np.arange(LANES)[None, :]).astype(np.float32)
    return jnp.asarray(ind, BF16), jnp.asarray(ind.T, BF16)


def kernel(x_prompt, x_sample, state_conv, cache_k, cache_v, p_prompt, p_sample, norm_mix, norm_ffn, norm_ple,
           conv_w_in, conv_b_in, conv_w_dw, conv_b_dw, conv_ln_g, conv_ln_b, conv_w_out, conv_b_out, attn_w_qkv,
           attn_q_norm, attn_k_norm, attn_sinks, attn_w_o, moe_w_rg, moe_b_rg, moe_w_re, moe_b_re, moe_w_gate,
           moe_w_up, moe_w_down, ple_w_gate, ple_w_proj):
    bp, seq, d = x_prompt.shape
    bs, t_new, _ = x_sample.shape
    assert bp == 1 and d == D_MODEL and seq % WINDOW == 0 and bs % SAMPLE_SEQS == 0
    assert seq % (bs * t_new) == 0 and (bs * t_new) % DMA_CHUNK == 0
    depth = norm_mix.shape[0]
    row2 = lambda a: a.reshape(1, -1)

    y_p = x_prompt.reshape(seq, d)
    y_s = x_sample.reshape(bs * t_new, d)
    conv_p, conv_s, k_p, v_p, k_s, v_s = [], [], [], [], [], []

    for i in range(depth):
        j = i // 2
        g_mix = row2(norm_mix[i])
        if i % 2 == 0:
            w_in = conv_w_in[j].astype(BF16)
            b_in = row2(conv_b_in[j])
            tail = (conv_w_dw[j], row2(conv_b_dw[j]), row2(conv_ln_g[j]), row2(conv_ln_b[j]),
                    conv_w_out[j].astype(BF16), row2(conv_b_out[j]))
            u_p = _conv_in(y_p, g_mix, w_in, b_in)
            u_s = _conv_in(y_s, g_mix, w_in, b_in)
            conv_p.append(u_p[seq - CONV_STATE:].reshape(1, CONV_STATE, d))
            y_p = _conv_out_p(u_p, y_p, *tail)
            ys_t, new_state = _conv_out_s(state_conv[j], u_s.reshape(bs, t_new, d),
                                          y_s.reshape(bs, t_new, d).transpose(1, 0, 2), *tail)
            conv_s.append(new_state)
            y_s = ys_t.transpose(1, 0, 2).reshape(bs * t_new, d)
        else:
            w_qkv = attn_w_qkv[j].astype(BF16)
            w_o = attn_w_o[j].astype(BF16)
            iq, iqt = _head_indicator(N_HEADS)
            ik, ikt = _head_indicator(N_KV_HEADS)
            qg = row2(jnp.tile(attn_q_norm[j], N_HEADS))
            kg = row2(jnp.tile(attn_k_norm[j], N_KV_HEADS))
            sinks = attn_sinks[j]
            q1, k1, v1 = _qkv(y_p, g_mix, w_qkv, iq, iqt, ik, ikt, qg, kg)
            q2, k2, v2 = _qkv(y_s, g_mix, w_qkv, iq, iqt, ik, ikt, qg, kg)
            k_p.append(k1[seq - WINDOW:].reshape(1, WINDOW, N_KV_HEADS, HEAD_DIM))
            v_p.append(v1[seq - WINDOW:].reshape(1, WINDOW, N_KV_HEADS, HEAD_DIM))
            y_p = _attn_p(sinks, q1, k1, v1, y_p, _prompt_bias(), w_o)
            n_cols = -(-(SAMPLE_SEQS * (WINDOW + t_new)) // LANES) * LANES
            o_s, nk, nv = _attn_s(sinks, q2, k2, v2, cache_k[j].reshape(bs * WINDOW, KV_DIM),
                                  cache_v[j].reshape(bs * WINDOW, KV_DIM), _sample_bias(t_new, n_cols), t_new)
            k_s.append(nk.reshape(bs, N_KV_HEADS, HEAD_DIM, WINDOW).transpose(0, 3, 1, 2))
            v_s.append(nv.reshape(bs, N_KV_HEADS, HEAD_DIM, WINDOW).transpose(0, 3, 1, 2))
            y_s = _proj_res(o_s, y_s, w_o)

        w_r = jnp.zeros((d, ROUTER_LANES), F32)
        w_r = w_r.at[:, :N_EXPERT_GROUPS].set(moe_w_rg[i]).at[:, EXPERT_LANE0:EXPERT_LANE0 + N_EXPERTS].set(moe_w_re[i])
        b_r = jnp.zeros((1, ROUTER_LANES), F32)
        b_r = b_r.at[0, :N_EXPERT_GROUPS].set(moe_b_rg[i]).at[0, EXPERT_LANE0:EXPERT_LANE0 + N_EXPERTS].set(moe_b_re[i])
        moe = (row2(norm_ffn[i]), jnp.stack(_split_bf16(w_r)), b_r, moe_w_gate, moe_w_up, moe_w_down)
        ple = (row2(norm_ple[i]), ple_w_gate[i].astype(BF16), ple_w_proj[i].astype(BF16))
        y_p, y_s = _moe_ple(i, y_p, y_s, *moe, p_prompt.reshape(depth, seq, PLE_DIM),
                            p_sample[i].reshape(bs * t_new, PLE_DIM), *ple)

    return (y_p.reshape(1, seq, d), y_s.reshape(bs, t_new, d), jnp.stack(conv_p), jnp.stack(conv_s),
            jnp.stack(k_p), jnp.stack(v_p), jnp.stack(k_s), jnp.stack(v_s))
```

```python
import functools

import numpy as np

import jax
import jax.numpy as jnp
from jax import lax
from jax.experimental import pallas as pl
from jax.experimental.pallas import tpu as pltpu

F32 = jnp.float32
BF16 = jnp.bfloat16

D_MODEL = 1024
PLE_DIM = 256
CONV_WIDTH = 31
CONV_STATE = CONV_WIDTH - 1
N_HEADS = 16
N_KV_HEADS = 4
HEAD_DIM = 64
GROUP = N_HEADS // N_KV_HEADS
WINDOW = 128
KV_DIM = N_KV_HEADS * HEAD_DIM
N_EXPERT_GROUPS = 4
EXPERTS_PER_GROUP = 4
N_EXPERTS = 16
D_EXPERT = 256
EPS = 1e-6
NEG_INF = -1e30

LANES = 128
ROUTER_LANES = LANES
EXPERT_LANE0 = N_EXPERT_GROUPS
HALO = 32
CONV_ROWS = 32
NORM_ROWS = 128
CONV_PITCH = D_MODEL // LANES + 1
SAMPLE_SEQS = 8
PAIRS_PER_GROUP = EXPERTS_PER_GROUP * (EXPERTS_PER_GROUP - 1) // 2
N_BUCKETS = N_EXPERT_GROUPS * PAIRS_PER_GROUP
TM = 256
PAY_WIDTH = D_MODEL + LANES
ROW_PITCH = PAY_WIDTH // LANES
DMA_CHUNK = 512
RING = 3
PLE_RING = 3
VMEM_LIMIT = 48 * 1024 * 1024


def _row_tile(n):
    return 512 if n % 512 == 0 else n


def _params(*sem):
    return pltpu.CompilerParams(dimension_semantics=sem, vmem_limit_bytes=VMEM_LIMIT)


def _full(shape):
    nd = len(shape)
    return pl.BlockSpec(shape, lambda *_: (0,) * nd)


def _rms(x, g):
    ms = jnp.mean(x * x, axis=-1, keepdims=True)
    return x * lax.rsqrt(ms + EPS) * g


def _sigmoid(x):
    return 1.0 / (1.0 + jnp.exp(-x))


def _mm(a, b):
    return jnp.dot(a, b, preferred_element_type=F32)


def _conv_in_kernel(x_ref, g_ref, w_ref, b_ref, u_ref):
    h = _rms(x_ref[...], g_ref[...]).astype(BF16)
    z = _mm(h, w_ref[...]) + b_ref[...]
    u_ref[...] = z[:, :D_MODEL] * _sigmoid(z[:, D_MODEL:])


def _conv_in(x, g, w, b):
    n = x.shape[0]
    t = _row_tile(n)
    return pl.pallas_call(
        _conv_in_kernel,
        grid=(n // t,),
        in_specs=[pl.BlockSpec((t, D_MODEL), lambda i: (i, 0)), _full(g.shape), _full(w.shape), _full(b.shape)],
        out_specs=pl.BlockSpec((t, D_MODEL), lambda i: (i, 0)),
        out_shape=jax.ShapeDtypeStruct((n, D_MODEL), F32),
        compiler_params=_params("arbitrary"),
        name="conv_in",
    )(x, g, w, b)


def _ln_silu(c, g, b):
    mu = jnp.mean(c, axis=-1, keepdims=True)
    xc = c - mu
    var = jnp.mean(xc * xc, axis=-1, keepdims=True)
    cn = xc * lax.rsqrt(var + EPS) * g + b
    return cn * _sigmoid(cn)


def _conv_out_p_kernel(u_ref, halo_ref, x_ref, wdw_ref, bdw_ref, lng_ref, lnb_ref, wout_ref, bout_ref, y_ref,
                       ubuf, cbuf, hbuf):
    t = u_ref.shape[0]
    i = pl.program_id(0)
    nj = D_MODEL // LANES

    def put(r, j, val):
        ubuf[pl.ds(r * CONV_PITCH + j, 8, stride=CONV_PITCH), :] = val

    for r in range(0, HALO, 8):
        for j in range(nj):
            put(r, j, jnp.where(i > 0, halo_ref[r:r + 8, j * LANES:(j + 1) * LANES], 0.0))

    def fill(rr, carry):
        r = pl.multiple_of(rr * 8, 8)
        for j in range(nj):
            put(r + HALO, j, u_ref[pl.ds(r, 8), j * LANES:(j + 1) * LANES])
        return carry

    lax.fori_loop(0, t // 8, fill, 0)

    def conv_chunk(rr, carry):
        r0 = pl.multiple_of(rr * CONV_ROWS, CONV_ROWS)
        for j in range(nj):
            lanes = slice(j * LANES, (j + 1) * LANES)
            accs = [None] * (CONV_ROWS // 8)
            for k in range(CONV_WIDTH):
                wk = wdw_ref[k:k + 1, lanes]
                for q in range(CONV_ROWS // 8):
                    r = r0 + (HALO - CONV_STATE + k + 8 * q)
                    term = wk * ubuf[pl.ds(r * CONV_PITCH + j, 8, stride=CONV_PITCH), :]
                    accs[q] = term if k == 0 else accs[q] + term
            for q in range(CONV_ROWS // 8):
                cbuf[pl.ds(r0 + 8 * q, 8), lanes] = accs[q]
        return carry

    lax.fori_loop(0, t // CONV_ROWS, conv_chunk, 0)

    def norm_chunk(rr, carry):
        r0 = pl.multiple_of(rr * NORM_ROWS, NORM_ROWS)
        c = cbuf[pl.ds(r0, NORM_ROWS), :] + bdw_ref[...]
        hbuf[pl.ds(r0, NORM_ROWS), :] = _ln_silu(c, lng_ref[...], lnb_ref[...]).astype(BF16)
        return carry

    lax.fori_loop(0, t // NORM_ROWS, norm_chunk, 0)
    y_ref[...] = x_ref[...] + _mm(hbuf[...], wout_ref[...]) + bout_ref[...]


def _conv_out_p(u, x, wdw, bdw, lng, lnb, wout, bout):
    n = u.shape[0]
    t = _row_tile(n)
    hb = t // HALO
    row = lambda i: (i, 0)
    return pl.pallas_call(
        _conv_out_p_kernel,
        grid=(n // t,),
        in_specs=[pl.BlockSpec((t, D_MODEL), row),
                  pl.BlockSpec((HALO, D_MODEL), lambda i: (jnp.maximum(i * hb - 1, 0), 0)),
                  pl.BlockSpec((t, D_MODEL), row),
                  _full(wdw.shape), _full(bdw.shape), _full(lng.shape), _full(lnb.shape),
                  _full(wout.shape), _full(bout.shape)],
        out_specs=pl.BlockSpec((t, D_MODEL), row),
        out_shape=jax.ShapeDtypeStruct((n, D_MODEL), F32),
        scratch_shapes=[pltpu.VMEM(((t + HALO) * CONV_PITCH, LANES), F32), pltpu.VMEM((t, D_MODEL), F32),
                        pltpu.VMEM((t, D_MODEL), BF16)],
        compiler_params=_params("arbitrary"),
        name="conv_out_prompt",
    )(u, u, x, wdw, bdw, lng, lnb, wout, bout)


def _conv_out_s_kernel(st_ref, u_ref, x_ref, wdw_ref, bdw_ref, lng_ref, lnb_ref, wout_ref, bout_ref, y_ref, ns_ref,
                       wbuf, cbuf):
    bb, nt, _ = u_ref.shape
    win = CONV_STATE + nt
    nj = D_MODEL // LANES
    seq_pitch = win * CONV_PITCH

    def put(b, r0, rows, j, val):
        wbuf[pl.ds((b * win + r0) * CONV_PITCH + j, rows, stride=CONV_PITCH), :] = val

    for b in range(bb):
        for j in range(nj):
            lanes = slice(j * LANES, (j + 1) * LANES)
            for r0 in range(0, CONV_STATE, 8):
                rows = min(8, CONV_STATE - r0)
                put(b, r0, rows, j, st_ref[b, r0:r0 + rows, lanes])
            put(b, CONV_STATE, nt, j, u_ref[b, :, lanes])

    for b in range(bb):
        for r0 in range(0, CONV_STATE, 8):
            rows = min(8, CONV_STATE - r0)
            for j in range(nj):
                src = (b * win + nt + r0) * CONV_PITCH + j
                ns_ref[b, r0:r0 + rows, j * LANES:(j + 1) * LANES] = wbuf[pl.ds(src, rows, stride=CONV_PITCH), :]

    for t in range(nt):
        for b0 in range(0, bb, 8):
            for j in range(nj):
                lanes = slice(j * LANES, (j + 1) * LANES)
                acc = None
                for k in range(CONV_WIDTH):
                    start = (b0 * win + t + k) * CONV_PITCH + j
                    term = wdw_ref[k:k + 1, lanes] * wbuf[pl.ds(start, 8, stride=seq_pitch), :]
                    acc = term if acc is None else acc + term
                cbuf[t * bb + b0:t * bb + b0 + 8, lanes] = acc
    c = cbuf[...] + bdw_ref[...]
    h = _ln_silu(c, lng_ref[...], lnb_ref[...]).astype(BF16)
    y = _mm(h, wout_ref[...]) + bout_ref[...]
    for t in range(nt):
        y_ref[t] = x_ref[t] + y[t * bb:(t + 1) * bb]


def _conv_out_s(state, u, x, wdw, bdw, lng, lnb, wout, bout):
    b, nt, _ = u.shape
    bb = 16 if b % 16 == 0 else b
    assert bb % 8 == 0
    win = CONV_STATE + nt
    seq = lambda i: (i, 0, 0)
    tm = lambda i: (0, i, 0)
    return pl.pallas_call(
        _conv_out_s_kernel,
        grid=(b // bb,),
        in_specs=[pl.BlockSpec((bb, CONV_STATE, D_MODEL), seq), pl.BlockSpec((bb, nt, D_MODEL), seq),
                  pl.BlockSpec((nt, bb, D_MODEL), tm),
                  _full(wdw.shape), _full(bdw.shape), _full(lng.shape), _full(lnb.shape),
                  _full(wout.shape), _full(bout.shape)],
        out_specs=[pl.BlockSpec((nt, bb, D_MODEL), tm), pl.BlockSpec((bb, CONV_STATE, D_MODEL), seq)],
        out_shape=[jax.ShapeDtypeStruct((nt, b, D_MODEL), F32), jax.ShapeDtypeStruct((b, CONV_STATE, D_MODEL), F32)],
        scratch_shapes=[pltpu.VMEM((bb * win * CONV_PITCH, LANES), F32), pltpu.VMEM((nt * bb, D_MODEL), F32)],
        compiler_params=_params("arbitrary"),
        name="conv_out_sample",
    )(state, u, x, wdw, bdw, lng, lnb, wout, bout)


def _route(lg):
    big = 3.0e38
    lane = lax.broadcasted_iota(jnp.int32, lg.shape, 1)
    lanef = lane.astype(F32)
    is_g = lane < N_EXPERT_GROUPS
    gl = jnp.where(is_g, lg, -big)
    gmax = jnp.max(gl, axis=1, keepdims=True)
    gsum = jnp.sum(jnp.where(is_g, jnp.exp(gl - gmax), 0.0), axis=1, keepdims=True)
    g_w = 1.0 / gsum
    g_idx = jnp.min(jnp.where(gl == gmax, lanef, big), axis=1, keepdims=True)
    rel = lanef - float(EXPERT_LANE0) - g_idx * float(EXPERTS_PER_GROUP)
    in_grp = jnp.where(rel >= 0.0, jnp.where(rel < float(EXPERTS_PER_GROUP), 1.0, 0.0), 0.0) > 0.5
    el = jnp.where(in_grp, lg, -big)
    e1 = jnp.max(el, axis=1, keepdims=True)
    i1 = jnp.min(jnp.where(el == e1, lanef, big), axis=1, keepdims=True)
    el2 = jnp.where(lanef == i1, -big, el)
    e2 = jnp.max(el2, axis=1, keepdims=True)
    i2 = jnp.min(jnp.where(el2 == e2, lanef, big), axis=1, keepdims=True)
    tt = jnp.exp(e2 - e1)
    w1 = g_w / (1.0 + tt)
    w2 = g_w * tt / (1.0 + tt)
    base = float(EXPERT_LANE0) + g_idx * float(EXPERTS_PER_GROUP)
    a = jnp.minimum(i1, i2) - base
    b = jnp.maximum(i1, i2) - base
    pair = a * (7.0 - a) * 0.5 + (b - a - 1.0)
    first_is_lo = i1 < i2
    return (g_idx * float(PAIRS_PER_GROUP) + pair, jnp.where(first_is_lo, w1, w2), jnp.where(first_is_lo, w2, w1))


def _to_row_linear(dst_ref, src_ref, n_tiles):
    def body(g, carry):
        r = pl.multiple_of(g * 8, 8)
        for j in range(n_tiles):
            dst_ref[pl.ds(r * ROW_PITCH + j, 8, stride=ROW_PITCH), :] = src_ref[pl.ds(r, 8), j * LANES:(j + 1) * LANES]
        return carry

    lax.fori_loop(0, src_ref.shape[0] // 8, body, 0)


def _from_row_linear(dst_ref, src_ref, n_tiles, pitch):
    def body(g, carry):
        r = pl.multiple_of(g * 8, 8)
        for j in range(n_tiles):
            dst_ref[pl.ds(r, 8), j * LANES:(j + 1) * LANES] = src_ref[pl.ds(r * pitch + j, 8, stride=pitch), :]
        return carry

    lax.fori_loop(0, dst_ref.shape[0] // 8, body, 0)


def _pick(i, n_prompt_tiles, prompt_ref, sample_ref):
    return jnp.where(i < n_prompt_tiles, prompt_ref[...], sample_ref[...])


def _route_kernel(n_prompt_tiles, yp_ref, ys_ref, g_ref, wr_ref, br_ref, tri_ref, pay_ref, meta_ref, cnt_ref, carry,
                  rowbuf):
    i = pl.program_id(0)

    @pl.when(i == 0)
    def _():
        carry[...] = jnp.zeros_like(carry)

    xf = _rms(_pick(i, n_prompt_tiles, yp_ref, ys_ref), g_ref[...])
    x_hi, x_lo = _split_bf16(xf)
    logits = _mm(x_hi, wr_ref[0]) + (_mm(x_lo, wr_ref[0]) + _mm(x_hi, wr_ref[1])) + br_ref[...]
    bucket, w_lo, w_hi = _route(logits)
    lane = lax.broadcasted_iota(jnp.int32, logits.shape, 1)
    onehot = jnp.where(lane.astype(F32) == bucket, 1.0, 0.0)
    before = _mm(tri_ref[...], onehot.astype(BF16)) + carry[...]
    rank = jnp.sum(onehot * before, axis=1, keepdims=True)
    carry[...] += jnp.sum(onehot, axis=0, keepdims=True)
    cnt_ref[...] = carry[...]
    meta = jnp.where(lane == 0, bucket, jnp.where(lane == 1, rank, 0.0))
    meta_ref[...] = jnp.transpose(meta)[:8, :].astype(jnp.int32)

    rowbuf[:, :D_MODEL] = xf
    rowbuf[:, D_MODEL:] = jnp.where(lane == 0, w_lo, jnp.where(lane == 1, w_hi, 0.0))
    _to_row_linear(pay_ref, rowbuf, ROW_PITCH)


def _route_call(y_p, y_s, g, wr, br, tri):
    t = y_s.shape[0]
    npt = y_p.shape[0] // t
    n = y_p.shape[0] + t
    pidx = lambda i: (jnp.minimum(i, npt - 1), 0)
    return pl.pallas_call(
        functools.partial(_route_kernel, npt),
        grid=(npt + 1,),
        in_specs=[pl.BlockSpec((t, D_MODEL), pidx), _full(y_s.shape), _full(g.shape), _full(wr.shape),
                  _full(br.shape), _full(tri.shape)],
        out_specs=[pl.BlockSpec((t * ROW_PITCH, LANES), lambda i: (i, 0)),
                   pl.BlockSpec((8, t), lambda i: (0, i)), _full((1, ROUTER_LANES))],
        out_shape=[jax.ShapeDtypeStruct((n * ROW_PITCH, LANES), F32),
                   jax.ShapeDtypeStruct((8, n), jnp.int32), jax.ShapeDtypeStruct((1, ROUTER_LANES), F32)],
        scratch_shapes=[pltpu.VMEM((1, ROUTER_LANES), F32), pltpu.VMEM((t, PAY_WIDTH), F32)],
        compiler_params=_params("arbitrary"),
        name="route",
    )(y_p, y_s, g, wr, br, tri)


def _permute_kernel(pos_ref, zstart_ref, nz_ref, used_ref, src_ref, dst_ref, zbuf, ring, zsem, in_sems, out_sems):
    zbuf[...] = jnp.zeros_like(zbuf)

    def zero_tile(start):
        return pltpu.make_async_copy(zbuf, dst_ref.at[pl.ds(pl.multiple_of(start * ROW_PITCH, 8), TM * ROW_PITCH)], zsem)

    for b in range(N_BUCKETS):
        @pl.when(nz_ref[b] > 0)
        def _():
            zero_tile(zstart_ref[b]).start()
    n_tiles = dst_ref.shape[0] // (TM * ROW_PITCH)

    def start_unused(i, carry):
        zero_tile(i * TM).start()
        return carry

    lax.fori_loop(used_ref[0], n_tiles, start_unused, 0)
    for b in range(N_BUCKETS):
        @pl.when(nz_ref[b] > 0)
        def _():
            zero_tile(0).wait()

    def wait_unused(i, carry):
        zero_tile(0).wait()
        return carry

    lax.fori_loop(used_ref[0], n_tiles, wait_unused, 0)

    chunk_rows = DMA_CHUNK * ROW_PITCH
    n_chunks = src_ref.shape[0] // chunk_rows

    def fetch(c, slot):
        return pltpu.make_async_copy(src_ref.at[pl.ds(pl.multiple_of(c * chunk_rows, 8), chunk_rows)], ring.at[slot],
                                     in_sems.at[slot])

    def drain(slot):
        pltpu.make_async_copy(ring.at[slot], dst_ref.at[pl.ds(0, chunk_rows)], out_sems.at[slot]).wait()

    fetch(0, 0).start()

    def step(c, carry):
        slot = lax.rem(c, RING)
        nxt = lax.rem(c + 1, RING)
        fetch(c, slot).wait()

        @pl.when(c >= RING - 1)
        def _():
            drain(nxt)

        @pl.when(c + 1 < n_chunks)
        def _():
            fetch(c + 1, nxt).start()

        def send(jj, carry2):
            row = pos_ref[c * DMA_CHUNK + jj] * ROW_PITCH
            pltpu.make_async_copy(ring.at[slot, pl.ds(jj * ROW_PITCH, ROW_PITCH)], dst_ref.at[pl.ds(row, ROW_PITCH)],
                                  out_sems.at[slot]).start()
            return carry2

        lax.fori_loop(0, DMA_CHUNK, send, 0, unroll=8)
        return carry

    lax.fori_loop(0, n_chunks, step, 0)
    for c in range(n_chunks - (RING - 1), n_chunks):
        drain(c % RING)


def _permute(pos, zstart, nz, used, payload, n_rows_out):
    assert payload.shape[0] // (DMA_CHUNK * ROW_PITCH) >= RING
    return pl.pallas_call(
        _permute_kernel,
        grid_spec=pltpu.PrefetchScalarGridSpec(
            num_scalar_prefetch=4, grid=(1,),
            in_specs=[pl.BlockSpec(memory_space=pl.ANY)],
            out_specs=pl.BlockSpec(memory_space=pl.ANY),
            scratch_shapes=[pltpu.VMEM((TM * ROW_PITCH, LANES), payload.dtype),
                            pltpu.VMEM((RING, DMA_CHUNK * ROW_PITCH, LANES), payload.dtype),
                            pltpu.SemaphoreType.DMA(()), pltpu.SemaphoreType.DMA((RING,)),
                            pltpu.SemaphoreType.DMA((RING,))]),
        out_shape=jax.ShapeDtypeStruct((n_rows_out * ROW_PITCH, LANES), payload.dtype),
        compiler_params=_params("arbitrary"),
        name="permute",
    )(pos, zstart, nz, used, payload)


def _silu(x):
    return x * _sigmoid(x)


def _experts_kernel(tix_ref, lo_ref, hi_ref, valid_ref, fresh_ref, xs_ref, wgl_ref, wgh_ref, wul_ref, wuh_ref,
                    wdl_ref, wdh_ref, ys_ref, wup_s, wdn_s, xbuf, ybuf):
    i = pl.program_id(0)

    @pl.when(fresh_ref[i] > 0)
    def _():
        for s, ref in enumerate((wgl_ref, wgh_ref, wul_ref, wuh_ref)):
            wup_s[s] = ref[0, 0].astype(BF16)
        for s, ref in enumerate((wdl_ref, wdh_ref)):
            wdn_s[s] = ref[0, 0].astype(BF16)

    @pl.when(valid_ref[i] == 0)
    def _():
        ys_ref[...] = jnp.zeros_like(ys_ref)

    @pl.when(valid_ref[i] > 0)
    def _():
        _from_row_linear(xbuf, xs_ref, ROW_PITCH, ROW_PITCH)
        x = xbuf[:, :D_MODEL].astype(BF16)
        gates = xbuf[:, D_MODEL:]
        h_lo = _silu(_mm(x, wup_s[0])) * _mm(x, wup_s[2]) * gates[:, 0:1]
        h_hi = _silu(_mm(x, wup_s[1])) * _mm(x, wup_s[3]) * gates[:, 1:2]
        ybuf[:, :D_MODEL] = _mm(h_lo.astype(BF16), wdn_s[0]) + _mm(h_hi.astype(BF16), wdn_s[1])
        ybuf[:, D_MODEL:] = jnp.zeros((TM, LANES), F32)
        _to_row_linear(ys_ref, ybuf, ROW_PITCH)


def _experts(layer, tix, lo, hi, valid, fresh, xs, wg, wu, wd):
    n_tiles = xs.shape[0] // (TM * ROW_PITCH)
    row = lambda i, tix, lo, hi, valid, fresh: (tix[i], 0)
    row_out = lambda i, tix, lo, hi, valid, fresh: (i, 0)
    e_lo = lambda i, tix, lo, hi, valid, fresh: (layer, lo[i], 0, 0)
    e_hi = lambda i, tix, lo, hi, valid, fresh: (layer, hi[i], 0, 0)
    up = pl.BlockSpec((1, 1, D_MODEL, D_EXPERT), e_lo), pl.BlockSpec((1, 1, D_MODEL, D_EXPERT), e_hi)
    down = pl.BlockSpec((1, 1, D_EXPERT, D_MODEL), e_lo), pl.BlockSpec((1, 1, D_EXPERT, D_MODEL), e_hi)
    return pl.pallas_call(
        _experts_kernel,
        grid_spec=pltpu.PrefetchScalarGridSpec(
            num_scalar_prefetch=5, grid=(n_tiles,),
            in_specs=[pl.BlockSpec((TM * ROW_PITCH, LANES), row), *up, *up, *down],
            out_specs=pl.BlockSpec((TM * ROW_PITCH, LANES), row_out),
            scratch_shapes=[pltpu.VMEM((4, D_MODEL, D_EXPERT), BF16), pltpu.VMEM((2, D_EXPERT, D_MODEL), BF16),
                            pltpu.VMEM((TM, PAY_WIDTH), F32), pltpu.VMEM((TM, PAY_WIDTH), F32)]),
        out_shape=jax.ShapeDtypeStruct(xs.shape, F32),
        compiler_params=_params("arbitrary"),
        name="experts",
    )(tix, lo, hi, valid, fresh, xs, wg, wg, wu, wu, wd, wd)


def _ple_kernel(n_prompt_tiles, pos_ref, yp_ref, ys_ref, moe_ref, pp_ref, ps_ref, g_ref, wg_ref, wp_ref, op_ref, os_ref,
                mring, mbuf, sems):
    i = pl.program_id(0)
    t = mbuf.shape[0]
    w = D_MODEL // LANES

    def gather(tile, slot):
        def body(jj, carry):
            row = pos_ref[tile * t + jj] * ROW_PITCH
            pltpu.make_async_copy(moe_ref.at[pl.ds(row, w)], mring.at[slot, pl.ds(jj * w, w)], sems.at[slot]).start()
            return carry

        lax.fori_loop(0, t, body, 0, unroll=8)

    def wait_tile(slot):
        pltpu.make_async_copy(moe_ref.at[pl.ds(0, t * w)], mring.at[slot], sems.at[slot]).wait()

    @pl.when(i == 0)
    def _():
        gather(0, 0)
        gather(1, 1)

    slot = lax.rem(i, PLE_RING)
    nslot = lax.rem(i + 2, PLE_RING)
    wait_tile(slot)
    _from_row_linear(mbuf, mring.at[slot], w, w)
    y2 = _pick(i, n_prompt_tiles, yp_ref, ys_ref) + mbuf[...]
    hn = _rms(y2, g_ref[...]).astype(BF16)
    gt = _sigmoid(_mm(hn, wg_ref[...]))
    pr = _mm(jnp.where(i < n_prompt_tiles, pp_ref[0], ps_ref[...]).astype(BF16), wp_ref[...])
    y3 = y2 + gt * pr
    nxt = jnp.minimum(i + 2, n_prompt_tiles) * t
    for jj in range(t):
        row = pos_ref[nxt + jj] * ROW_PITCH
        pltpu.make_async_copy(moe_ref.at[pl.ds(row, w)], mring.at[nslot, pl.ds(jj * w, w)], sems.at[nslot]).start()

    @pl.when(i < n_prompt_tiles)
    def _():
        op_ref[...] = y3

    @pl.when(i >= n_prompt_tiles)
    def _():
        os_ref[...] = y3
        wait_tile(lax.rem(i + 1, PLE_RING))
        wait_tile(nslot)


def _ple(layer, pos, y_p, y_s, ys_sorted, p_p, p_s, g, wg, wp):
    t = y_s.shape[0]
    npt = y_p.shape[0] // t
    w = D_MODEL // LANES
    pidx = lambda i, pos: (jnp.minimum(i, npt - 1), 0)
    full = lambda shape: pl.BlockSpec(shape, lambda i, pos: (0,) * len(shape))
    return pl.pallas_call(
        functools.partial(_ple_kernel, npt),
        grid_spec=pltpu.PrefetchScalarGridSpec(
            num_scalar_prefetch=1, grid=(npt + 1,),
            in_specs=[pl.BlockSpec((t, D_MODEL), pidx), full(y_s.shape), pl.BlockSpec(memory_space=pl.ANY),
                      pl.BlockSpec((1, t, PLE_DIM), lambda i, pos: (layer, jnp.minimum(i, npt - 1), 0)),
                      full(p_s.shape), full(g.shape), full(wg.shape), full(wp.shape)],
            out_specs=[pl.BlockSpec((t, D_MODEL), pidx), full(y_s.shape)],
            scratch_shapes=[pltpu.VMEM((PLE_RING, t * w, LANES), F32), pltpu.VMEM((t, D_MODEL), F32),
                            pltpu.SemaphoreType.DMA((PLE_RING,))]),
        out_shape=[jax.ShapeDtypeStruct(y_p.shape, F32), jax.ShapeDtypeStruct(y_s.shape, F32)],
        compiler_params=_params("arbitrary"),
        name="ple",
    )(pos, y_p, y_s, ys_sorted, p_p, p_s, g, wg, wp)


def _bucket_experts():
    pairs = [(a, b) for a in range(EXPERTS_PER_GROUP) for b in range(a + 1, EXPERTS_PER_GROUP)]
    lo = [g * EXPERTS_PER_GROUP + a for g in range(N_EXPERT_GROUPS) for a, _ in pairs]
    hi = [g * EXPERTS_PER_GROUP + b for g in range(N_EXPERT_GROUPS) for _, b in pairs]
    return jnp.array(lo, jnp.int32), jnp.array(hi, jnp.int32)


def _moe_ple(layer, y_p, y_s, gffn, wr, br, wg, wu, wd, p_p, p_s, gple, pwg, pwp):
    t = y_s.shape[0]
    n = y_p.shape[0] + t
    tri = jnp.asarray(np.tri(t, t, -1, dtype=np.float32), BF16)
    payload, meta, counts = _route_call(y_p, y_s, gffn, wr, br, tri)

    bucket, rank = meta[0], meta[1]
    cnt = counts[0, :N_BUCKETS].astype(jnp.int32)
    padded = (cnt + TM - 1) // TM * TM
    ends = jnp.cumsum(padded)
    starts = ends - padded
    in_bucket = bucket[:, None] == jnp.arange(N_BUCKETS, dtype=jnp.int32)[None, :]
    pos = rank + jnp.sum(jnp.where(in_bucket, starts[None, :], 0), axis=1)
    n_rows = n + N_BUCKETS * TM
    n_tiles = n_rows // TM
    tile_start = jnp.arange(n_tiles, dtype=jnp.int32) * TM
    valid = tile_start < ends[-1]
    used = (ends[-1] // TM).astype(jnp.int32)
    tix = jnp.arange(n_tiles, dtype=jnp.int32)
    tile_bucket = jnp.sum((ends[None, :] <= (jnp.minimum(tix, used - 1) * TM)[:, None]).astype(jnp.int32), axis=1)
    lo_tab, hi_tab = _bucket_experts()
    lo = jnp.take(lo_tab, tile_bucket)
    hi = jnp.take(hi_tab, tile_bucket)
    prev_bucket = jnp.concatenate([jnp.full((1,), -1, jnp.int32), tile_bucket[:-1]])
    fresh = valid & (tile_bucket != prev_bucket)

    xs = _permute(pos, (ends - TM).astype(jnp.int32), (cnt > 0).astype(jnp.int32), used.reshape(1), payload, n_rows)
    ys = _experts(layer, jnp.minimum(tix, used - 1), lo, hi, valid.astype(jnp.int32), fresh.astype(jnp.int32), xs, wg,
                  wu, wd)
    return _ple(layer, pos, y_p, y_s, ys, p_p, p_s, gple, pwg, pwp)


def _split_bf16(a):
    hi = a.astype(BF16)
    lo = (a - hi.astype(F32)).astype(BF16)
    return hi, lo


def _head_norm(a, ind_ref, indt_ref, gain):
    hi, lo = _split_bf16(a * a)
    ss = _mm(hi, ind_ref[...]) + _mm(lo, ind_ref[...])
    inv = lax.rsqrt(ss * (1.0 / HEAD_DIM) + EPS)
    ihi, ilo = _split_bf16(inv)
    invb = _mm(ihi, indt_ref[...]) + _mm(ilo, indt_ref[...])
    return a * invb * gain


def _qkv_kernel(x_ref, g_ref, w_ref, iq_ref, iqt_ref, ik_ref, ikt_ref, qg_ref, kg_ref, q_ref, k_ref, v_ref):
    h = _rms(x_ref[...], g_ref[...]).astype(BF16)
    qkv = _mm(h, w_ref[...])
    nq = N_HEADS * HEAD_DIM
    q = _head_norm(qkv[:, :nq], iq_ref, iqt_ref, qg_ref[...])
    k = _head_norm(qkv[:, nq:nq + KV_DIM], ik_ref, ikt_ref, kg_ref[...])
    q_ref[...] = (q * (HEAD_DIM ** -0.5)).astype(BF16)
    k_ref[...] = k
    v_ref[...] = qkv[:, nq + KV_DIM:]


def _qkv(x, g, w, iq, iqt, ik, ikt, qg, kg):
    n = x.shape[0]
    t = _row_tile(n)
    row = lambda i: (i, 0)
    return pl.pallas_call(
        _qkv_kernel,
        grid=(n // t,),
        in_specs=[pl.BlockSpec((t, D_MODEL), row)] + [_full(a.shape) for a in (g, w, iq, iqt, ik, ikt, qg, kg)],
        out_specs=[pl.BlockSpec((t, N_HEADS * HEAD_DIM), row), pl.BlockSpec((t, KV_DIM), row),
                   pl.BlockSpec((t, KV_DIM), row)],
        out_shape=[jax.ShapeDtypeStruct((n, N_HEADS * HEAD_DIM), BF16), jax.ShapeDtypeStruct((n, KV_DIM), F32),
                   jax.ShapeDtypeStruct((n, KV_DIM), F32)],
        compiler_params=_params("arbitrary"),
        name="qkv",
    )(x, g, w, iq, iqt, ik, ikt, qg, kg)


def _dup_heads(a):
    out = []
    for s in range(KV_DIM // LANES):
        sl = a[:, s * LANES:(s + 1) * LANES]
        sw = pltpu.roll(sl, HEAD_DIM, axis=1)
        low = lax.broadcasted_iota(jnp.int32, sl.shape, 1) < HEAD_DIM
        out.append(jnp.where(low, sl, sw))
        out.append(jnp.where(low, sw, sl))
    return jnp.concatenate(out, axis=1).astype(BF16)


def _attend(q_rows, k2, v2, bias_of, sink_of, extra_mask):
    m_rows = q_rows.shape[0]
    low_q = lax.broadcasted_iota(jnp.int32, (m_rows, LANES), 1) < HEAD_DIM
    low_k = lax.broadcasted_iota(jnp.int32, (k2.shape[0], LANES), 1) < HEAD_DIM
    zero_q = jnp.zeros((m_rows, LANES), BF16)
    zero_k = jnp.zeros((k2.shape[0], LANES), BF16)
    slabs = []
    for g in range(N_KV_HEADS):
        kg = k2[:, g * LANES:(g + 1) * LANES]
        vg = v2[:, g * LANES:(g + 1) * LANES]
        v_lo = jnp.where(low_k, vg, zero_k)
        v_hi = jnp.where(low_k, zero_k, vg)
        lhs = []
        for a in range(GROUP):
            h = g * GROUP + a
            qs = q_rows[:, (h // 2) * LANES:(h // 2 + 1) * LANES]
            lhs.append(jnp.where(low_q, qs, zero_q) if h % 2 == 0 else jnp.where(low_q, zero_q, qs))
        s = lax.dot_general(jnp.concatenate(lhs, axis=0), kg, (((1,), (1,)), ((), ())), preferred_element_type=F32)
        probs, rinv = [], []
        for a in range(GROUP):
            h = g * GROUP + a
            sa = s[a * m_rows:(a + 1) * m_rows] + bias_of(h)
            if extra_mask is not None:
                sa = jnp.where(extra_mask, NEG_INF, sa)
            sink = sink_of(h)
            m = jnp.maximum(jnp.max(sa, axis=1, keepdims=True), sink)
            p = jnp.exp(sa - m)
            den = jnp.sum(p, axis=1, keepdims=True) + jnp.exp(sink - m)
            probs.append(p.astype(BF16))
            rinv.append(1.0 / den)
        for sp in range(GROUP // 2):
            o = _mm(probs[2 * sp], v_lo) + _mm(probs[2 * sp + 1], v_hi)
            slabs.append(o * jnp.where(low_q, rinv[2 * sp], rinv[2 * sp + 1]))
    return jnp.concatenate(slabs, axis=1)


def _attn_p_kernel(sink_ref, q_ref, k_ref, v_ref, x_ref, bias_ref, wo_ref, y_ref, kbuf, vbuf, obuf):
    t = q_ref.shape[0]
    i = pl.program_id(0)

    @pl.when(i == 0)
    def _():
        kbuf[0:WINDOW, :] = jnp.zeros((WINDOW, 2 * KV_DIM), BF16)
        vbuf[0:WINDOW, :] = jnp.zeros((WINDOW, 2 * KV_DIM), BF16)

    @pl.when(i > 0)
    def _():
        kbuf[0:WINDOW, :] = kbuf[t:t + WINDOW, :]
        vbuf[0:WINDOW, :] = vbuf[t:t + WINDOW, :]

    kbuf[WINDOW:, :] = _dup_heads(k_ref[...])
    vbuf[WINDOW:, :] = _dup_heads(v_ref[...])
    col = lax.broadcasted_iota(jnp.int32, (WINDOW, 2 * WINDOW), 1)
    for j in range(t // WINDOW):
        rows = slice(j * WINDOW, (j + 1) * WINDOW)
        keys = slice(j * WINDOW, (j + 2) * WINDOW)
        extra = jnp.logical_and(i == 0, col < WINDOW) if j == 0 else None
        o = _attend(q_ref[rows, :], kbuf[keys, :], vbuf[keys, :], lambda h: bias_ref[h], lambda h: sink_ref[h], extra)
        obuf[rows, :] = o.astype(BF16)
    y_ref[...] = x_ref[...] + _mm(obuf[...], wo_ref[...])


def _attn_p(sinks, q, k, v, x, bias, wo):
    n = q.shape[0]
    t = _row_tile(n)
    row = lambda i: (i, 0)
    return pl.pallas_call(
        _attn_p_kernel,
        grid=(n // t,),
        in_specs=[pl.BlockSpec(memory_space=pltpu.SMEM),
                  pl.BlockSpec((t, N_HEADS * HEAD_DIM), row), pl.BlockSpec((t, KV_DIM), row),
                  pl.BlockSpec((t, KV_DIM), row), pl.BlockSpec((t, D_MODEL), row),
                  _full(bias.shape), _full(wo.shape)],
        out_specs=pl.BlockSpec((t, D_MODEL), row),
        out_shape=jax.ShapeDtypeStruct((n, D_MODEL), F32),
        scratch_shapes=[pltpu.VMEM((t + WINDOW, 2 * KV_DIM), BF16), pltpu.VMEM((t + WINDOW, 2 * KV_DIM), BF16),
                        pltpu.VMEM((t, N_HEADS * HEAD_DIM), BF16)],
        compiler_params=_params("arbitrary"),
        name="attn_prompt",
    )(sinks, q, k, v, x, bias, wo)


def _attn_s_kernel(sink_ref, q_ref, k_ref, v_ref, ck_ref, cv_ref, bias_ref, o_ref, nk_ref, nv_ref, shift_buf):
    pad = bias_ref.shape[2] - ck_ref.shape[0] - k_ref.shape[0]
    zpad = jnp.zeros((pad, KV_DIM), F32)
    k2 = _dup_heads(jnp.concatenate([ck_ref[...], k_ref[...], zpad], axis=0))
    v2 = _dup_heads(jnp.concatenate([cv_ref[...], v_ref[...], zpad], axis=0))
    o = _attend(q_ref[...], k2, v2, lambda h: bias_ref[h], lambda h: sink_ref[h], None)
    o_ref[...] = o.astype(BF16)
    t_new = k_ref.shape[0] // SAMPLE_SEQS
    for new_ref, old_ref, add_ref in ((nk_ref, ck_ref, k_ref), (nv_ref, cv_ref, v_ref)):
        for b in range(SAMPLE_SEQS):
            r = b * WINDOW
            shift_buf[0:WINDOW - t_new, :] = old_ref[r + t_new:r + WINDOW, :]
            shift_buf[WINDOW - t_new:WINDOW, :] = add_ref[b * t_new:(b + 1) * t_new, :]
            new_ref[b] = jnp.transpose(shift_buf[...])


def _attn_s(sinks, q, k, v, ck, cv, bias, t_new):
    n = q.shape[0]
    rows = SAMPLE_SEQS * t_new
    crow = SAMPLE_SEQS * WINDOW
    row = lambda i: (i, 0)
    return pl.pallas_call(
        _attn_s_kernel,
        grid=(n // rows,),
        in_specs=[pl.BlockSpec(memory_space=pltpu.SMEM),
                  pl.BlockSpec((rows, N_HEADS * HEAD_DIM), row), pl.BlockSpec((rows, KV_DIM), row),
                  pl.BlockSpec((rows, KV_DIM), row), pl.BlockSpec((crow, KV_DIM), row),
                  pl.BlockSpec((crow, KV_DIM), row), _full(bias.shape)],
        out_specs=[pl.BlockSpec((rows, N_HEADS * HEAD_DIM), row),
                   pl.BlockSpec((SAMPLE_SEQS, KV_DIM, WINDOW), lambda i: (i, 0, 0)),
                   pl.BlockSpec((SAMPLE_SEQS, KV_DIM, WINDOW), lambda i: (i, 0, 0))],
        out_shape=[jax.ShapeDtypeStruct((n, N_HEADS * HEAD_DIM), BF16),
                   jax.ShapeDtypeStruct((ck.shape[0] // WINDOW, KV_DIM, WINDOW), F32),
                   jax.ShapeDtypeStruct((cv.shape[0] // WINDOW, KV_DIM, WINDOW), F32)],
        scratch_shapes=[pltpu.VMEM((WINDOW, KV_DIM), F32)],
        compiler_params=_params("arbitrary"),
        name="attn_sample",
    )(sinks, q, k, v, ck, cv, bias)


def _proj_res_kernel(o_ref, x_ref, w_ref, y_ref):
    y_ref[...] = x_ref[...] + _mm(o_ref[...], w_ref[...])


def _proj_res(o, x, w):
    n = o.shape[0]
    t = _row_tile(n)
    row = lambda i: (i, 0)
    return pl.pallas_call(
        _proj_res_kernel,
        grid=(n // t,),
        in_specs=[pl.BlockSpec((t, o.shape[1]), row), pl.BlockSpec((t, D_MODEL), row), _full(w.shape)],
        out_specs=pl.BlockSpec((t, D_MODEL), row),
        out_shape=jax.ShapeDtypeStruct((n, D_MODEL), F32),
        compiler_params=_params("arbitrary"),
        name="proj_res",
    )(o, x, w)


def _alibi_slopes():
    return np.exp2(-8.0 * np.arange(1, N_HEADS + 1, dtype=np.float64) / N_HEADS).astype(np.float32)


def _band_bias(dist, allowed):
    b = -(_alibi_slopes()[:, None, None] * dist.astype(np.float32)[None])
    return jnp.asarray(np.where(allowed[None], b, np.float32(NEG_INF)).astype(np.float32))


def _prompt_bias():
    dist = WINDOW + np.arange(WINDOW)[:, None] - np.arange(2 * WINDOW)[None, :]
    return _band_bias(dist, (dist >= 0) & (dist <= WINDOW))


def _sample_bias(t_new, n_cols):
    c = np.arange(n_cols)
    n_cache = SAMPLE_SEQS * WINDOW
    n_new = SAMPLE_SEQS * t_new
    is_cache = c < n_cache
    is_new = (c >= n_cache) & (c < n_cache + n_new)
    seq_c = np.where(is_cache, c // WINDOW, (c - n_cache) // t_new)
    pos_c = np.where(is_cache, c % WINDOW, WINDOW + (c - n_cache) % t_new)
    r = np.arange(n_new)
    seq_r, tok_r = r // t_new, r % t_new
    dist = WINDOW + tok_r[:, None] - pos_c[None, :]
    allowed = (seq_r[:, None] == seq_c[None, :]) & (is_cache | is_new)[None, :] & (dist >= 0) & (dist <= WINDOW)
    return _band_bias(dist, allowed)


def _head_indicator(n_heads):
    ch = np.arange(n_heads * HEAD_DIM) // HEAD_DIM
    ind = (ch[:, None] == np.arange(LANES)[None, :]).astype(np.float32)
    return jnp.asarray(ind, BF16), jnp.asarray(ind.T, BF16)


def kernel(x_prompt, x_sample, state_conv, cache_k, cache_v, p_prompt, p_sample, norm_mix, norm_ffn, norm_ple,
           conv_w_in, conv_b_in, conv_w_dw, conv_b_dw, conv_ln_g, conv_ln_b, conv_w_out, conv_b_out, attn_w_qkv,
           attn_q_norm, attn_k_norm, attn_sinks, attn_w_o, moe_w_rg, moe_b_rg, moe_w_re, moe_b_re, moe_w_gate,
           moe_w_up, moe_w_down, ple_w_gate, ple_w_proj):
    bp, seq, d = x_prompt.shape
    bs, t_new, _ = x_sample.shape
    assert bp == 1 and d == D_MODEL and seq % WINDOW == 0 and bs % SAMPLE_SEQS == 0
    assert seq % (bs * t_new) == 0 and (bs * t_new) % DMA_CHUNK == 0
    depth = norm_mix.shape[0]
    row2 = lambda a: a.reshape(1, -1)

    y_p = x_prompt.reshape(seq, d)
    y_s = x_sample.reshape(bs * t_new, d)
    conv_p, conv_s, k_p, v_p, k_s, v_s = [], [], [], [], [], []

    for i in range(depth):
        j = i // 2
        g_mix = row2(norm_mix[i])
        if i % 2 == 0:
            w_in = conv_w_in[j].astype(BF16)
            b_in = row2(conv_b_in[j])
            tail = (conv_w_dw[j], row2(conv_b_dw[j]), row2(conv_ln_g[j]), row2(conv_ln_b[j]),
                    conv_w_out[j].astype(BF16), row2(conv_b_out[j]))
            u_p = _conv_in(y_p, g_mix, w_in, b_in)
            u_s = _conv_in(y_s, g_mix, w_in, b_in)
            conv_p.append(u_p[seq - CONV_STATE:].reshape(1, CONV_STATE, d))
            y_p = _conv_out_p(u_p, y_p, *tail)
            ys_t, new_state = _conv_out_s(state_conv[j], u_s.reshape(bs, t_new, d),
                                          y_s.reshape(bs, t_new, d).transpose(1, 0, 2), *tail)
            conv_s.append(new_state)
            y_s = ys_t.transpose(1, 0, 2).reshape(bs * t_new, d)
        else:
            w_qkv = attn_w_qkv[j].astype(BF16)
            w_o = attn_w_o[j].astype(BF16)
            iq, iqt = _head_indicator(N_HEADS)
            ik, ikt = _head_indicator(N_KV_HEADS)
            qg = row2(jnp.tile(attn_q_norm[j], N_HEADS))
            kg = row2(jnp.tile(attn_k_norm[j], N_KV_HEADS))
            sinks = attn_sinks[j]
            q1, k1, v1 = _qkv(y_p, g_mix, w_qkv, iq, iqt, ik, ikt, qg, kg)
            q2, k2, v2 = _qkv(y_s, g_mix, w_qkv, iq, iqt, ik, ikt, qg, kg)
            k_p.append(k1[seq - WINDOW:].reshape(1, WINDOW, N_KV_HEADS, HEAD_DIM))
            v_p.append(v1[seq - WINDOW:].reshape(1, WINDOW, N_KV_HEADS, HEAD_DIM))
            y_p = _attn_p(sinks, q1, k1, v1, y_p, _prompt_bias(), w_o)
            n_cols = -(-(SAMPLE_SEQS * (WINDOW + t_new)) // LANES) * LANES
            o_s, nk, nv = _attn_s(sinks, q2, k2, v2, cache_k[j].reshape(bs * WINDOW, KV_DIM),
                                  cache_v[j].reshape(bs * WINDOW, KV_DIM), _sample_bias(t_new, n_cols), t_new)
            k_s.append(nk.reshape(bs, N_KV_HEADS, HEAD_DIM, WINDOW).transpose(0, 3, 1, 2))
            v_s.append(nv.reshape(bs, N_KV_HEADS, HEAD_DIM, WINDOW).transpose(0, 3, 1, 2))
            y_s = _proj_res(o_s, y_s, w_o)

        w_r = jnp.zeros((d, ROUTER_LANES), F32)
        w_r = w_r.at[:, :N_EXPERT_GROUPS].set(moe_w_rg[i]).at[:, EXPERT_LANE0:EXPERT_LANE0 + N_EXPERTS].set(moe_w_re[i])
        b_r = jnp.zeros((1, ROUTER_LANES), F32)
        b_r = b_r.at[0, :N_EXPERT_GROUPS].set(moe_b_rg[i]).at[0, EXPERT_LANE0:EXPERT_LANE0 + N_EXPERTS].set(moe_b_re[i])
        moe = (row2(norm_ffn[i]), jnp.stack(_split_bf16(w_r)), b_r, moe_w_gate, moe_w_up, moe_w_down)
        ple = (row2(norm_ple[i]), ple_w_gate[i].astype(BF16), ple_w_proj[i].astype(BF16))
        y_p, y_s = _moe_ple(i, y_p, y_s, *moe, p_prompt.reshape(depth, seq, PLE_DIM),
                            p_sample[i].reshape(bs * t_new, PLE_DIM), *ple)

    return (y_p.reshape(1, seq, d), y_s.reshape(bs, t_new, d), jnp.stack(conv_p), jnp.stack(conv_s),
            jnp.stack(k_p), jnp.stack(v_p), jnp.stack(k_s), jnp.stack(v_s))
```

```python
import functools

import numpy as np

import jax
import jax.numpy as jnp
from jax import lax
from jax.experimental import pallas as pl
from jax.experimental.pallas import tpu as pltpu

F32 = jnp.float32
BF16 = jnp.bfloat16

D_MODEL = 1024
PLE_DIM = 256
CONV_WIDTH = 31
CONV_STATE = CONV_WIDTH - 1
N_HEADS = 16
N_KV_HEADS = 4
HEAD_DIM = 64
GROUP = N_HEADS // N_KV_HEADS
WINDOW = 128
KV_DIM = N_KV_HEADS * HEAD_DIM
N_EXPERT_GROUPS = 4
EXPERTS_PER_GROUP = 4
N_EXPERTS = 16
D_EXPERT = 256
EPS = 1e-6
NEG_INF = -1e30

LANES = 128
ROUTER_LANES = LANES
EXPERT_LANE0 = N_EXPERT_GROUPS
HALO = 32
CONV_ROWS = 32
NORM_ROWS = 128
CONV_PITCH = D_MODEL // LANES + 1
SAMPLE_SEQS = 8
PAIRS_PER_GROUP = EXPERTS_PER_GROUP * (EXPERTS_PER_GROUP - 1) // 2
N_BUCKETS = N_EXPERT_GROUPS * PAIRS_PER_GROUP
TM = 256
PAY_WIDTH = D_MODEL + LANES
ROW_PITCH = PAY_WIDTH // LANES
DMA_CHUNK = 512
RING = 3
SEND_GROUP = 64
PLE_RING = 3
VMEM_LIMIT = 48 * 1024 * 1024


def _row_tile(n):
    return 512 if n % 512 == 0 else n


def _params(*sem):
    return pltpu.CompilerParams(dimension_semantics=sem, vmem_limit_bytes=VMEM_LIMIT)


def _full(shape):
    nd = len(shape)
    return pl.BlockSpec(shape, lambda *_: (0,) * nd)


def _rms(x, g):
    ms = jnp.mean(x * x, axis=-1, keepdims=True)
    return x * lax.rsqrt(ms + EPS) * g


def _sigmoid(x):
    return 1.0 / (1.0 + jnp.exp(-x))


def _mm(a, b):
    return jnp.dot(a, b, preferred_element_type=F32)


def _conv_in_kernel(x_ref, g_ref, w_ref, b_ref, u_ref):
    h = _rms(x_ref[...], g_ref[...]).astype(BF16)
    z = _mm(h, w_ref[...]) + b_ref[...]
    u_ref[...] = z[:, :D_MODEL] * _sigmoid(z[:, D_MODEL:])


def _conv_in(x, g, w, b):
    n = x.shape[0]
    t = _row_tile(n)
    return pl.pallas_call(
        _conv_in_kernel,
        grid=(n // t,),
        in_specs=[pl.BlockSpec((t, D_MODEL), lambda i: (i, 0)), _full(g.shape), _full(w.shape), _full(b.shape)],
        out_specs=pl.BlockSpec((t, D_MODEL), lambda i: (i, 0)),
        out_shape=jax.ShapeDtypeStruct((n, D_MODEL), F32),
        compiler_params=_params("arbitrary"),
        name="conv_in",
    )(x, g, w, b)


def _ln_silu(c, g, b):
    mu = jnp.mean(c, axis=-1, keepdims=True)
    xc = c - mu
    var = jnp.mean(xc * xc, axis=-1, keepdims=True)
    cn = xc * lax.rsqrt(var + EPS) * g + b
    return cn * _sigmoid(cn)


def _conv_out_p_kernel(u_ref, halo_ref, x_ref, wdw_ref, bdw_ref, lng_ref, lnb_ref, wout_ref, bout_ref, y_ref,
                       ubuf, cbuf, hbuf):
    t = u_ref.shape[0]
    i = pl.program_id(0)
    nj = D_MODEL // LANES

    def put(r, j, val):
        ubuf[pl.ds(r * CONV_PITCH + j, 8, stride=CONV_PITCH), :] = val

    for r in range(0, HALO, 8):
        for j in range(nj):
            put(r, j, jnp.where(i > 0, halo_ref[r:r + 8, j * LANES:(j + 1) * LANES], 0.0))

    def fill(rr, carry):
        r = pl.multiple_of(rr * 8, 8)
        for j in range(nj):
            put(r + HALO, j, u_ref[pl.ds(r, 8), j * LANES:(j + 1) * LANES])
        return carry

    lax.fori_loop(0, t // 8, fill, 0)

    def conv_chunk(rr, carry):
        r0 = pl.multiple_of(rr * CONV_ROWS, CONV_ROWS)
        for j in range(nj):
            lanes = slice(j * LANES, (j + 1) * LANES)
            accs = [None] * (CONV_ROWS // 8)
            for k in range(CONV_WIDTH):
                wk = wdw_ref[k:k + 1, lanes]
                for q in range(CONV_ROWS // 8):
                    r = r0 + (HALO - CONV_STATE + k + 8 * q)
                    term = wk * ubuf[pl.ds(r * CONV_PITCH + j, 8, stride=CONV_PITCH), :]
                    accs[q] = term if k == 0 else accs[q] + term
            for q in range(CONV_ROWS // 8):
                cbuf[pl.ds(r0 + 8 * q, 8), lanes] = accs[q]
        return carry

    lax.fori_loop(0, t // CONV_ROWS, conv_chunk, 0)

    def norm_chunk(rr, carry):
        r0 = pl.multiple_of(rr * NORM_ROWS, NORM_ROWS)
        c = cbuf[pl.ds(r0, NORM_ROWS), :] + bdw_ref[...]
        hbuf[pl.ds(r0, NORM_ROWS), :] = _ln_silu(c, lng_ref[...], lnb_ref[...]).astype(BF16)
        return carry

    lax.fori_loop(0, t // NORM_ROWS, norm_chunk, 0)
    y_ref[...] = x_ref[...] + _mm(hbuf[...], wout_ref[...]) + bout_ref[...]


def _conv_out_p(u, x, wdw, bdw, lng, lnb, wout, bout):
    n = u.shape[0]
    t = _row_tile(n)
    hb = t // HALO
    row = lambda i: (i, 0)
    return pl.pallas_call(
        _conv_out_p_kernel,
        grid=(n // t,),
        in_specs=[pl.BlockSpec((t, D_MODEL), row),
                  pl.BlockSpec((HALO, D_MODEL), lambda i: (jnp.maximum(i * hb - 1, 0), 0)),
                  pl.BlockSpec((t, D_MODEL), row),
                  _full(wdw.shape), _full(bdw.shape), _full(lng.shape), _full(lnb.shape),
                  _full(wout.shape), _full(bout.shape)],
        out_specs=pl.BlockSpec((t, D_MODEL), row),
        out_shape=jax.ShapeDtypeStruct((n, D_MODEL), F32),
        scratch_shapes=[pltpu.VMEM(((t + HALO) * CONV_PITCH, LANES), F32), pltpu.VMEM((t, D_MODEL), F32),
                        pltpu.VMEM((t, D_MODEL), BF16)],
        compiler_params=_params("arbitrary"),
        name="conv_out_prompt",
    )(u, u, x, wdw, bdw, lng, lnb, wout, bout)


def _conv_out_s_kernel(st_ref, u_ref, x_ref, wdw_ref, bdw_ref, lng_ref, lnb_ref, wout_ref, bout_ref, y_ref, ns_ref,
                       wbuf, cbuf):
    bb, nt, _ = u_ref.shape
    win = CONV_STATE + nt
    nj = D_MODEL // LANES
    seq_pitch = win * CONV_PITCH

    def put(b, r0, rows, j, val):
        wbuf[pl.ds((b * win + r0) * CONV_PITCH + j, rows, stride=CONV_PITCH), :] = val

    for b in range(bb):
        for j in range(nj):
            lanes = slice(j * LANES, (j + 1) * LANES)
            for r0 in range(0, CONV_STATE, 8):
                rows = min(8, CONV_STATE - r0)
                put(b, r0, rows, j, st_ref[b, r0:r0 + rows, lanes])
            put(b, CONV_STATE, nt, j, u_ref[b, :, lanes])

    for b in range(bb):
        for r0 in range(0, CONV_STATE, 8):
            rows = min(8, CONV_STATE - r0)
            for j in range(nj):
                src = (b * win + nt + r0) * CONV_PITCH + j
                ns_ref[b, r0:r0 + rows, j * LANES:(j + 1) * LANES] = wbuf[pl.ds(src, rows, stride=CONV_PITCH), :]

    for t in range(nt):
        for b0 in range(0, bb, 8):
            for j in range(nj):
                lanes = slice(j * LANES, (j + 1) * LANES)
                acc = None
                for k in range(CONV_WIDTH):
                    start = (b0 * win + t + k) * CONV_PITCH + j
                    term = wdw_ref[k:k + 1, lanes] * wbuf[pl.ds(start, 8, stride=seq_pitch), :]
                    acc = term if acc is None else acc + term
                cbuf[t * bb + b0:t * bb + b0 + 8, lanes] = acc
    c = cbuf[...] + bdw_ref[...]
    h = _ln_silu(c, lng_ref[...], lnb_ref[...]).astype(BF16)
    y = _mm(h, wout_ref[...]) + bout_ref[...]
    for t in range(nt):
        y_ref[t] = x_ref[t] + y[t * bb:(t + 1) * bb]


def _conv_out_s(state, u, x, wdw, bdw, lng, lnb, wout, bout):
    b, nt, _ = u.shape
    bb = 16 if b % 16 == 0 else b
    assert bb % 8 == 0
    win = CONV_STATE + nt
    seq = lambda i: (i, 0, 0)
    tm = lambda i: (0, i, 0)
    return pl.pallas_call(
        _conv_out_s_kernel,
        grid=(b // bb,),
        in_specs=[pl.BlockSpec((bb, CONV_STATE, D_MODEL), seq), pl.BlockSpec((bb, nt, D_MODEL), seq),
                  pl.BlockSpec((nt, bb, D_MODEL), tm),
                  _full(wdw.shape), _full(bdw.shape), _full(lng.shape), _full(lnb.shape),
                  _full(wout.shape), _full(bout.shape)],
        out_specs=[pl.BlockSpec((nt, bb, D_MODEL), tm), pl.BlockSpec((bb, CONV_STATE, D_MODEL), seq)],
        out_shape=[jax.ShapeDtypeStruct((nt, b, D_MODEL), F32), jax.ShapeDtypeStruct((b, CONV_STATE, D_MODEL), F32)],
        scratch_shapes=[pltpu.VMEM((bb * win * CONV_PITCH, LANES), F32), pltpu.VMEM((nt * bb, D_MODEL), F32)],
        compiler_params=_params("arbitrary"),
        name="conv_out_sample",
    )(state, u, x, wdw, bdw, lng, lnb, wout, bout)


def _route(lg):
    big = 3.0e38
    lane = lax.broadcasted_iota(jnp.int32, lg.shape, 1)
    lanef = lane.astype(F32)
    is_g = lane < N_EXPERT_GROUPS
    gl = jnp.where(is_g, lg, -big)
    gmax = jnp.max(gl, axis=1, keepdims=True)
    gsum = jnp.sum(jnp.where(is_g, jnp.exp(gl - gmax), 0.0), axis=1, keepdims=True)
    g_w = 1.0 / gsum
    g_idx = jnp.min(jnp.where(gl == gmax, lanef, big), axis=1, keepdims=True)
    rel = lanef - float(EXPERT_LANE0) - g_idx * float(EXPERTS_PER_GROUP)
    in_grp = jnp.where(rel >= 0.0, jnp.where(rel < float(EXPERTS_PER_GROUP), 1.0, 0.0), 0.0) > 0.5
    el = jnp.where(in_grp, lg, -big)
    e1 = jnp.max(el, axis=1, keepdims=True)
    i1 = jnp.min(jnp.where(el == e1, lanef, big), axis=1, keepdims=True)
    el2 = jnp.where(lanef == i1, -big, el)
    e2 = jnp.max(el2, axis=1, keepdims=True)
    i2 = jnp.min(jnp.where(el2 == e2, lanef, big), axis=1, keepdims=True)
    tt = jnp.exp(e2 - e1)
    w1 = g_w / (1.0 + tt)
    w2 = g_w * tt / (1.0 + tt)
    base = float(EXPERT_LANE0) + g_idx * float(EXPERTS_PER_GROUP)
    a = jnp.minimum(i1, i2) - base
    b = jnp.maximum(i1, i2) - base
    pair = a * (7.0 - a) * 0.5 + (b - a - 1.0)
    first_is_lo = i1 < i2
    return (g_idx * float(PAIRS_PER_GROUP) + pair, jnp.where(first_is_lo, w1, w2), jnp.where(first_is_lo, w2, w1))


def _to_row_linear(dst_ref, src_ref, n_tiles):
    def body(g, carry):
        r = pl.multiple_of(g * 8, 8)
        for j in range(n_tiles):
            dst_ref[pl.ds(r * ROW_PITCH + j, 8, stride=ROW_PITCH), :] = src_ref[pl.ds(r, 8), j * LANES:(j + 1) * LANES]
        return carry

    lax.fori_loop(0, src_ref.shape[0] // 8, body, 0)


def _from_row_linear(dst_ref, src_ref, n_tiles, pitch):
    def body(g, carry):
        r = pl.multiple_of(g * 8, 8)
        for j in range(n_tiles):
            dst_ref[pl.ds(r, 8), j * LANES:(j + 1) * LANES] = src_ref[pl.ds(r * pitch + j, 8, stride=pitch), :]
        return carry

    lax.fori_loop(0, dst_ref.shape[0] // 8, body, 0)


def _pick(i, n_prompt_tiles, prompt_ref, sample_ref):
    return jnp.where(i < n_prompt_tiles, prompt_ref[...], sample_ref[...])


def _route_kernel(n_prompt_tiles, yp_ref, ys_ref, g_ref, wr_ref, br_ref, tri_ref, pay_ref, meta_ref, cnt_ref, carry,
                  rowbuf):
    i = pl.program_id(0)

    @pl.when(i == 0)
    def _():
        carry[...] = jnp.zeros_like(carry)

    xf = _rms(_pick(i, n_prompt_tiles, yp_ref, ys_ref), g_ref[...])
    x_hi, x_lo = _split_bf16(xf)
    logits = _mm(x_hi, wr_ref[0]) + (_mm(x_lo, wr_ref[0]) + _mm(x_hi, wr_ref[1])) + br_ref[...]
    bucket, w_lo, w_hi = _route(logits)
    lane = lax.broadcasted_iota(jnp.int32, logits.shape, 1)
    onehot = jnp.where(lane.astype(F32) == bucket, 1.0, 0.0)
    before = _mm(tri_ref[...], onehot.astype(BF16)) + carry[...]
    rank = jnp.sum(onehot * before, axis=1, keepdims=True)
    carry[...] += jnp.sum(onehot, axis=0, keepdims=True)
    cnt_ref[...] = carry[...]
    meta = jnp.where(lane == 0, bucket, jnp.where(lane == 1, rank, 0.0))
    meta_ref[...] = jnp.transpose(meta)[:8, :].astype(jnp.int32)

    rowbuf[:, :D_MODEL] = xf
    rowbuf[:, D_MODEL:] = jnp.where(lane == 0, w_lo, jnp.where(lane == 1, w_hi, 0.0))
    _to_row_linear(pay_ref, rowbuf, ROW_PITCH)


def _route_call(y_p, y_s, g, wr, br, tri):
    t = y_s.shape[0]
    npt = y_p.shape[0] // t
    n = y_p.shape[0] + t
    pidx = lambda i: (jnp.minimum(i, npt - 1), 0)
    return pl.pallas_call(
        functools.partial(_route_kernel, npt),
        grid=(npt + 1,),
        in_specs=[pl.BlockSpec((t, D_MODEL), pidx), _full(y_s.shape), _full(g.shape), _full(wr.shape),
                  _full(br.shape), _full(tri.shape)],
        out_specs=[pl.BlockSpec((t * ROW_PITCH, LANES), lambda i: (i, 0)),
                   pl.BlockSpec((8, t), lambda i: (0, i)), _full((1, ROUTER_LANES))],
        out_shape=[jax.ShapeDtypeStruct((n * ROW_PITCH, LANES), F32),
                   jax.ShapeDtypeStruct((8, n), jnp.int32), jax.ShapeDtypeStruct((1, ROUTER_LANES), F32)],
        scratch_shapes=[pltpu.VMEM((1, ROUTER_LANES), F32), pltpu.VMEM((t, PAY_WIDTH), F32)],
        compiler_params=_params("arbitrary"),
        name="route",
    )(y_p, y_s, g, wr, br, tri)


def _permute_kernel(pos_ref, zstart_ref, nz_ref, used_ref, src_ref, dst_ref, zbuf, ring, zsem, in_sems, out_sems):
    zbuf[...] = jnp.zeros_like(zbuf)

    def zero_tile(start):
        return pltpu.make_async_copy(zbuf, dst_ref.at[pl.ds(pl.multiple_of(start * ROW_PITCH, 8), TM * ROW_PITCH)], zsem)

    for b in range(N_BUCKETS):
        @pl.when(nz_ref[b] > 0)
        def _():
            zero_tile(zstart_ref[b]).start()
    n_tiles = dst_ref.shape[0] // (TM * ROW_PITCH)

    def start_unused(i, carry):
        zero_tile(i * TM).start()
        return carry

    lax.fori_loop(used_ref[0], n_tiles, start_unused, 0)
    for b in range(N_BUCKETS):
        @pl.when(nz_ref[b] > 0)
        def _():
            zero_tile(0).wait()

    def wait_unused(i, carry):
        zero_tile(0).wait()
        return carry

    lax.fori_loop(used_ref[0], n_tiles, wait_unused, 0)

    chunk_rows = DMA_CHUNK * ROW_PITCH
    n_chunks = src_ref.shape[0] // chunk_rows

    def fetch(c, slot):
        return pltpu.make_async_copy(src_ref.at[pl.ds(pl.multiple_of(c * chunk_rows, 8), chunk_rows)], ring.at[slot],
                                     in_sems.at[slot])

    def drain(slot):
        pltpu.make_async_copy(ring.at[slot], dst_ref.at[pl.ds(0, chunk_rows)], out_sems.at[slot]).wait()

    fetch(0, 0).start()

    def step(c, carry):
        slot = lax.rem(c, RING)
        nxt = lax.rem(c + 1, RING)
        fetch(c, slot).wait()

        @pl.when(c >= RING - 1)
        def _():
            drain(nxt)

        @pl.when(c + 1 < n_chunks)
        def _():
            fetch(c + 1, nxt).start()

        def send(g, carry2):
            for u in range(SEND_GROUP):
                jj = g * SEND_GROUP + u
                row = pos_ref[c * DMA_CHUNK + jj] * ROW_PITCH
                pltpu.make_async_copy(ring.at[slot, pl.ds(jj * ROW_PITCH, ROW_PITCH)],
                                      dst_ref.at[pl.ds(row, ROW_PITCH)], out_sems.at[slot]).start(priority=u % 2)
            return carry2

        lax.fori_loop(0, DMA_CHUNK // SEND_GROUP, send, 0)
        return carry

    lax.fori_loop(0, n_chunks, step, 0)
    for c in range(n_chunks - (RING - 1), n_chunks):
        drain(c % RING)


def _permute(pos, zstart, nz, used, payload, n_rows_out):
    assert payload.shape[0] // (DMA_CHUNK * ROW_PITCH) >= RING
    return pl.pallas_call(
        _permute_kernel,
        grid_spec=pltpu.PrefetchScalarGridSpec(
            num_scalar_prefetch=4, grid=(1,),
            in_specs=[pl.BlockSpec(memory_space=pl.ANY)],
            out_specs=pl.BlockSpec(memory_space=pl.ANY),
            scratch_shapes=[pltpu.VMEM((TM * ROW_PITCH, LANES), payload.dtype),
                            pltpu.VMEM((RING, DMA_CHUNK * ROW_PITCH, LANES), payload.dtype),
                            pltpu.SemaphoreType.DMA(()), pltpu.SemaphoreType.DMA((RING,)),
                            pltpu.SemaphoreType.DMA((RING,))]),
        out_shape=jax.ShapeDtypeStruct((n_rows_out * ROW_PITCH, LANES), payload.dtype),
        compiler_params=_params("arbitrary"),
        name="permute",
    )(pos, zstart, nz, used, payload)


def _silu(x):
    return x * _sigmoid(x)


def _experts_kernel(tix_ref, lo_ref, hi_ref, valid_ref, fresh_ref, xs_ref, wgl_ref, wgh_ref, wul_ref, wuh_ref,
                    wdl_ref, wdh_ref, ys_ref, wup_s, wdn_s, xbuf, ybuf):
    i = pl.program_id(0)

    @pl.when(fresh_ref[i] > 0)
    def _():
        for s, ref in enumerate((wgl_ref, wgh_ref, wul_ref, wuh_ref)):
            wup_s[s] = ref[0, 0].astype(BF16)
        for s, ref in enumerate((wdl_ref, wdh_ref)):
            wdn_s[s] = ref[0, 0].astype(BF16)

    @pl.when(valid_ref[i] == 0)
    def _():
        ys_ref[...] = jnp.zeros_like(ys_ref)

    @pl.when(valid_ref[i] > 0)
    def _():
        _from_row_linear(xbuf, xs_ref, ROW_PITCH, ROW_PITCH)
        x = xbuf[:, :D_MODEL].astype(BF16)
        gates = xbuf[:, D_MODEL:]
        h_lo = _silu(_mm(x, wup_s[0])) * _mm(x, wup_s[2]) * gates[:, 0:1]
        h_hi = _silu(_mm(x, wup_s[1])) * _mm(x, wup_s[3]) * gates[:, 1:2]
        ybuf[:, :D_MODEL] = _mm(h_lo.astype(BF16), wdn_s[0]) + _mm(h_hi.astype(BF16), wdn_s[1])
        ybuf[:, D_MODEL:] = jnp.zeros((TM, LANES), F32)
        _to_row_linear(ys_ref, ybuf, ROW_PITCH)


def _experts(layer, tix, lo, hi, valid, fresh, xs, wg, wu, wd):
    n_tiles = xs.shape[0] // (TM * ROW_PITCH)
    row = lambda i, tix, lo, hi, valid, fresh: (tix[i], 0)
    row_out = lambda i, tix, lo, hi, valid, fresh: (i, 0)
    e_lo = lambda i, tix, lo, hi, valid, fresh: (layer, lo[i], 0, 0)
    e_hi = lambda i, tix, lo, hi, valid, fresh: (layer, hi[i], 0, 0)
    up = pl.BlockSpec((1, 1, D_MODEL, D_EXPERT), e_lo), pl.BlockSpec((1, 1, D_MODEL, D_EXPERT), e_hi)
    down = pl.BlockSpec((1, 1, D_EXPERT, D_MODEL), e_lo), pl.BlockSpec((1, 1, D_EXPERT, D_MODEL), e_hi)
    return pl.pallas_call(
        _experts_kernel,
        grid_spec=pltpu.PrefetchScalarGridSpec(
            num_scalar_prefetch=5, grid=(n_tiles,),
            in_specs=[pl.BlockSpec((TM * ROW_PITCH, LANES), row), *up, *up, *down],
            out_specs=pl.BlockSpec((TM * ROW_PITCH, LANES), row_out),
            scratch_shapes=[pltpu.VMEM((4, D_MODEL, D_EXPERT), BF16), pltpu.VMEM((2, D_EXPERT, D_MODEL), BF16),
                            pltpu.VMEM((TM, PAY_WIDTH), F32), pltpu.VMEM((TM, PAY_WIDTH), F32)]),
        out_shape=jax.ShapeDtypeStruct(xs.shape, F32),
        compiler_params=_params("arbitrary"),
        name="experts",
    )(tix, lo, hi, valid, fresh, xs, wg, wg, wu, wu, wd, wd)


def _ple_kernel(n_prompt_tiles, pos_ref, yp_ref, ys_ref, moe_ref, pp_ref, ps_ref, g_ref, wg_ref, wp_ref, op_ref, os_ref,
                mring, mbuf, sems):
    i = pl.program_id(0)
    t = mbuf.shape[0]
    w = D_MODEL // LANES

    def gather(tile, slot):
        def body(jj, carry):
            row = pos_ref[tile * t + jj] * ROW_PITCH
            pltpu.make_async_copy(moe_ref.at[pl.ds(row, w)], mring.at[slot, pl.ds(jj * w, w)], sems.at[slot]).start()
            return carry

        lax.fori_loop(0, t, body, 0, unroll=8)

    def wait_tile(slot):
        pltpu.make_async_copy(moe_ref.at[pl.ds(0, t * w)], mring.at[slot], sems.at[slot]).wait()

    @pl.when(i == 0)
    def _():
        gather(0, 0)
        gather(1, 1)

    slot = lax.rem(i, PLE_RING)
    nslot = lax.rem(i + 2, PLE_RING)
    wait_tile(slot)
    _from_row_linear(mbuf, mring.at[slot], w, w)
    y2 = _pick(i, n_prompt_tiles, yp_ref, ys_ref) + mbuf[...]
    hn = _rms(y2, g_ref[...]).astype(BF16)
    gt = _sigmoid(_mm(hn, wg_ref[...]))
    pr = _mm(jnp.where(i < n_prompt_tiles, pp_ref[0], ps_ref[...]).astype(BF16), wp_ref[...])
    y3 = y2 + gt * pr
    nxt = jnp.minimum(i + 2, n_prompt_tiles) * t
    for jj in range(t):
        row = pos_ref[nxt + jj] * ROW_PITCH
        pltpu.make_async_copy(moe_ref.at[pl.ds(row, w)], mring.at[nslot, pl.ds(jj * w, w)], sems.at[nslot]).start()

    @pl.when(i < n_prompt_tiles)
    def _():
        op_ref[...] = y3

    @pl.when(i >= n_prompt_tiles)
    def _():
        os_ref[...] = y3
        wait_tile(lax.rem(i + 1, PLE_RING))
        wait_tile(nslot)


def _ple(layer, pos, y_p, y_s, ys_sorted, p_p, p_s, g, wg, wp):
    t = y_s.shape[0]
    npt = y_p.shape[0] // t
    w = D_MODEL // LANES
    pidx = lambda i, pos: (jnp.minimum(i, npt - 1), 0)
    full = lambda shape: pl.BlockSpec(shape, lambda i, pos: (0,) * len(shape))
    return pl.pallas_call(
        functools.partial(_ple_kernel, npt),
        grid_spec=pltpu.PrefetchScalarGridSpec(
            num_scalar_prefetch=1, grid=(npt + 1,),
            in_specs=[pl.BlockSpec((t, D_MODEL), pidx), full(y_s.shape), pl.BlockSpec(memory_space=pl.ANY),
                      pl.BlockSpec((1, t, PLE_DIM), lambda i, pos: (layer, jnp.minimum(i, npt - 1), 0)),
                      full(p_s.shape), full(g.shape), full(wg.shape), full(wp.shape)],
            out_specs=[pl.BlockSpec((t, D_MODEL), pidx), full(y_s.shape)],
            scratch_shapes=[pltpu.VMEM((PLE_RING, t * w, LANES), F32), pltpu.VMEM((t, D_MODEL), F32),
                            pltpu.SemaphoreType.DMA((PLE_RING,))]),
        out_shape=[jax.ShapeDtypeStruct(y_p.shape, F32), jax.ShapeDtypeStruct(y_s.shape, F32)],
        compiler_params=_params("arbitrary"),
        name="ple",
    )(pos, y_p, y_s, ys_sorted, p_p, p_s, g, wg, wp)


def _bucket_experts():
    pairs = [(a, b) for a in range(EXPERTS_PER_GROUP) for b in range(a + 1, EXPERTS_PER_GROUP)]
    lo = [g * EXPERTS_PER_GROUP + a for g in range(N_EXPERT_GROUPS) for a, _ in pairs]
    hi = [g * EXPERTS_PER_GROUP + b for g in range(N_EXPERT_GROUPS) for _, b in pairs]
    return jnp.array(lo, jnp.int32), jnp.array(hi, jnp.int32)


def _moe_ple(layer, y_p, y_s, gffn, wr, br, wg, wu, wd, p_p, p_s, gple, pwg, pwp):
    t = y_s.shape[0]
    n = y_p.shape[0] + t
    tri = jnp.asarray(np.tri(t, t, -1, dtype=np.float32), BF16)
    payload, meta, counts = _route_call(y_p, y_s, gffn, wr, br, tri)

    bucket, rank = meta[0], meta[1]
    cnt = counts[0, :N_BUCKETS].astype(jnp.int32)
    padded = (cnt + TM - 1) // TM * TM
    ends = jnp.cumsum(padded)
    starts = ends - padded
    in_bucket = bucket[:, None] == jnp.arange(N_BUCKETS, dtype=jnp.int32)[None, :]
    pos = rank + jnp.sum(jnp.where(in_bucket, starts[None, :], 0), axis=1)
    n_rows = n + N_BUCKETS * TM
    n_tiles = n_rows // TM
    tile_start = jnp.arange(n_tiles, dtype=jnp.int32) * TM
    valid = tile_start < ends[-1]
    used = (ends[-1] // TM).astype(jnp.int32)
    tix = jnp.arange(n_tiles, dtype=jnp.int32)
    tile_bucket = jnp.sum((ends[None, :] <= (jnp.minimum(tix, used - 1) * TM)[:, None]).astype(jnp.int32), axis=1)
    lo_tab, hi_tab = _bucket_experts()
    lo = jnp.take(lo_tab, tile_bucket)
    hi = jnp.take(hi_tab, tile_bucket)
    prev_bucket = jnp.concatenate([jnp.full((1,), -1, jnp.int32), tile_bucket[:-1]])
    fresh = valid & (tile_bucket != prev_bucket)

    xs = _permute(pos, (ends - TM).astype(jnp.int32), (cnt > 0).astype(jnp.int32), used.reshape(1), payload, n_rows)
    ys = _experts(layer, jnp.minimum(tix, used - 1), lo, hi, valid.astype(jnp.int32), fresh.astype(jnp.int32), xs, wg,
                  wu, wd)
    return _ple(layer, pos, y_p, y_s, ys, p_p, p_s, gple, pwg, pwp)


def _split_bf16(a):
    hi = a.astype(BF16)
    lo = (a - hi.astype(F32)).astype(BF16)
    return hi, lo


def _head_norm(a, ind_ref, indt_ref, gain):
    hi, lo = _split_bf16(a * a)
    ss = _mm(hi, ind_ref[...]) + _mm(lo, ind_ref[...])
    inv = lax.rsqrt(ss * (1.0 / HEAD_DIM) + EPS)
    ihi, ilo = _split_bf16(inv)
    invb = _mm(ihi, indt_ref[...]) + _mm(ilo, indt_ref[...])
    return a * invb * gain


def _qkv_kernel(x_ref, g_ref, w_ref, iq_ref, iqt_ref, ik_ref, ikt_ref, qg_ref, kg_ref, q_ref, k_ref, v_ref):
    h = _rms(x_ref[...], g_ref[...]).astype(BF16)
    qkv = _mm(h, w_ref[...])
    nq = N_HEADS * HEAD_DIM
    q = _head_norm(qkv[:, :nq], iq_ref, iqt_ref, qg_ref[...])
    k = _head_norm(qkv[:, nq:nq + KV_DIM], ik_ref, ikt_ref, kg_ref[...])
    q_ref[...] = (q * (HEAD_DIM ** -0.5)).astype(BF16)
    k_ref[...] = k
    v_ref[...] = qkv[:, nq + KV_DIM:]


def _qkv(x, g, w, iq, iqt, ik, ikt, qg, kg):
    n = x.shape[0]
    t = _row_tile(n)
    row = lambda i: (i, 0)
    return pl.pallas_call(
        _qkv_kernel,
        grid=(n // t,),
        in_specs=[pl.BlockSpec((t, D_MODEL), row)] + [_full(a.shape) for a in (g, w, iq, iqt, ik, ikt, qg, kg)],
        out_specs=[pl.BlockSpec((t, N_HEADS * HEAD_DIM), row), pl.BlockSpec((t, KV_DIM), row),
                   pl.BlockSpec((t, KV_DIM), row)],
        out_shape=[jax.ShapeDtypeStruct((n, N_HEADS * HEAD_DIM), BF16), jax.ShapeDtypeStruct((n, KV_DIM), F32),
                   jax.ShapeDtypeStruct((n, KV_DIM), F32)],
        compiler_params=_params("arbitrary"),
        name="qkv",
    )(x, g, w, iq, iqt, ik, ikt, qg, kg)


def _dup_heads(a):
    out = []
    for s in range(KV_DIM // LANES):
        sl = a[:, s * LANES:(s + 1) * LANES]
        sw = pltpu.roll(sl, HEAD_DIM, axis=1)
        low = lax.broadcasted_iota(jnp.int32, sl.shape, 1) < HEAD_DIM
        out.append(jnp.where(low, sl, sw))
        out.append(jnp.where(low, sw, sl))
    return jnp.concatenate(out, axis=1).astype(BF16)


def _attend(q_rows, k2, v2, bias_of, sink_of, extra_mask):
    m_rows = q_rows.shape[0]
    low_q = lax.broadcasted_iota(jnp.int32, (m_rows, LANES), 1) < HEAD_DIM
    low_k = lax.broadcasted_iota(jnp.int32, (k2.shape[0], LANES), 1) < HEAD_DIM
    zero_q = jnp.zeros((m_rows, LANES), BF16)
    zero_k = jnp.zeros((k2.shape[0], LANES), BF16)
    slabs = []
    for g in range(N_KV_HEADS):
        kg = k2[:, g * LANES:(g + 1) * LANES]
        vg = v2[:, g * LANES:(g + 1) * LANES]
        v_lo = jnp.where(low_k, vg, zero_k)
        v_hi = jnp.where(low_k, zero_k, vg)
        lhs = []
        for a in range(GROUP):
            h = g * GROUP + a
            qs = q_rows[:, (h // 2) * LANES:(h // 2 + 1) * LANES]
            lhs.append(jnp.where(low_q, qs, zero_q) if h % 2 == 0 else jnp.where(low_q, zero_q, qs))
        s = lax.dot_general(jnp.concatenate(lhs, axis=0), kg, (((1,), (1,)), ((), ())), preferred_element_type=F32)
        probs, rinv = [], []
        for a in range(GROUP):
            h = g * GROUP + a
            sa = s[a * m_rows:(a + 1) * m_rows] + bias_of(h)
            if extra_mask is not None:
                sa = jnp.where(extra_mask, NEG_INF, sa)
            sink = sink_of(h)
            m = jnp.maximum(jnp.max(sa, axis=1, keepdims=True), sink)
            p = jnp.exp(sa - m)
            den = jnp.sum(p, axis=1, keepdims=True) + jnp.exp(sink - m)
            probs.append(p.astype(BF16))
            rinv.append(1.0 / den)
        for sp in range(GROUP // 2):
            o = _mm(probs[2 * sp], v_lo) + _mm(probs[2 * sp + 1], v_hi)
            slabs.append(o * jnp.where(low_q, rinv[2 * sp], rinv[2 * sp + 1]))
    return jnp.concatenate(slabs, axis=1)


def _attn_p_kernel(sink_ref, q_ref, k_ref, v_ref, x_ref, bias_ref, wo_ref, y_ref, kbuf, vbuf, obuf):
    t = q_ref.shape[0]
    i = pl.program_id(0)

    @pl.when(i == 0)
    def _():
        kbuf[0:WINDOW, :] = jnp.zeros((WINDOW, 2 * KV_DIM), BF16)
        vbuf[0:WINDOW, :] = jnp.zeros((WINDOW, 2 * KV_DIM), BF16)

    @pl.when(i > 0)
    def _():
        kbuf[0:WINDOW, :] = kbuf[t:t + WINDOW, :]
        vbuf[0:WINDOW, :] = vbuf[t:t + WINDOW, :]

    kbuf[WINDOW:, :] = _dup_heads(k_ref[...])
    vbuf[WINDOW:, :] = _dup_heads(v_ref[...])
    col = lax.broadcasted_iota(jnp.int32, (WINDOW, 2 * WINDOW), 1)
    for j in range(t // WINDOW):
        rows = slice(j * WINDOW, (j + 1) * WINDOW)
        keys = slice(j * WINDOW, (j + 2) * WINDOW)
        extra = jnp.logical_and(i == 0, col < WINDOW) if j == 0 else None
        o = _attend(q_ref[rows, :], kbuf[keys, :], vbuf[keys, :], lambda h: bias_ref[h], lambda h: sink_ref[h], extra)
        obuf[rows, :] = o.astype(BF16)
    y_ref[...] = x_ref[...] + _mm(obuf[...], wo_ref[...])


def _attn_p(sinks, q, k, v, x, bias, wo):
    n = q.shape[0]
    t = _row_tile(n)
    row = lambda i: (i, 0)
    return pl.pallas_call(
        _attn_p_kernel,
        grid=(n // t,),
        in_specs=[pl.BlockSpec(memory_space=pltpu.SMEM),
                  pl.BlockSpec((t, N_HEADS * HEAD_DIM), row), pl.BlockSpec((t, KV_DIM), row),
                  pl.BlockSpec((t, KV_DIM), row), pl.BlockSpec((t, D_MODEL), row),
                  _full(bias.shape), _full(wo.shape)],
        out_specs=pl.BlockSpec((t, D_MODEL), row),
        out_shape=jax.ShapeDtypeStruct((n, D_MODEL), F32),
        scratch_shapes=[pltpu.VMEM((t + WINDOW, 2 * KV_DIM), BF16), pltpu.VMEM((t + WINDOW, 2 * KV_DIM), BF16),
                        pltpu.VMEM((t, N_HEADS * HEAD_DIM), BF16)],
        compiler_params=_params("arbitrary"),
        name="attn_prompt",
    )(sinks, q, k, v, x, bias, wo)


def _attn_s_kernel(sink_ref, q_ref, k_ref, v_ref, ck_ref, cv_ref, bias_ref, o_ref, nk_ref, nv_ref, shift_buf):
    pad = bias_ref.shape[2] - ck_ref.shape[0] - k_ref.shape[0]
    zpad = jnp.zeros((pad, KV_DIM), F32)
    k2 = _dup_heads(jnp.concatenate([ck_ref[...], k_ref[...], zpad], axis=0))
    v2 = _dup_heads(jnp.concatenate([cv_ref[...], v_ref[...], zpad], axis=0))
    o = _attend(q_ref[...], k2, v2, lambda h: bias_ref[h], lambda h: sink_ref[h], None)
    o_ref[...] = o.astype(BF16)
    t_new = k_ref.shape[0] // SAMPLE_SEQS
    for new_ref, old_ref, add_ref in ((nk_ref, ck_ref, k_ref), (nv_ref, cv_ref, v_ref)):
        for b in range(SAMPLE_SEQS):
            r = b * WINDOW
            shift_buf[0:WINDOW - t_new, :] = old_ref[r + t_new:r + WINDOW, :]
            shift_buf[WINDOW - t_new:WINDOW, :] = add_ref[b * t_new:(b + 1) * t_new, :]
            new_ref[b] = jnp.transpose(shift_buf[...])


def _attn_s(sinks, q, k, v, ck, cv, bias, t_new):
    n = q.shape[0]
    rows = SAMPLE_SEQS * t_new
    crow = SAMPLE_SEQS * WINDOW
    row = lambda i: (i, 0)
    return pl.pallas_call(
        _attn_s_kernel,
        grid=(n // rows,),
        in_specs=[pl.BlockSpec(memory_space=pltpu.SMEM),
                  pl.BlockSpec((rows, N_HEADS * HEAD_DIM), row), pl.BlockSpec((rows, KV_DIM), row),
                  pl.BlockSpec((rows, KV_DIM), row), pl.BlockSpec((crow, KV_DIM), row),
                  pl.BlockSpec((crow, KV_DIM), row), _full(bias.shape)],
        out_specs=[pl.BlockSpec((rows, N_HEADS * HEAD_DIM), row),
                   pl.BlockSpec((SAMPLE_SEQS, KV_DIM, WINDOW), lambda i: (i, 0, 0)),
                   pl.BlockSpec((SAMPLE_SEQS, KV_DIM, WINDOW), lambda i: (i, 0, 0))],
        out_shape=[jax.ShapeDtypeStruct((n, N_HEADS * HEAD_DIM), BF16),
                   jax.ShapeDtypeStruct((ck.shape[0] // WINDOW, KV_DIM, WINDOW), F32),
                   jax.ShapeDtypeStruct((cv.shape[0] // WINDOW, KV_DIM, WINDOW), F32)],
        scratch_shapes=[pltpu.VMEM((WINDOW, KV_DIM), F32)],
        compiler_params=_params("arbitrary"),
        name="attn_sample",
    )(sinks, q, k, v, ck, cv, bias)


def _proj_res_kernel(o_ref, x_ref, w_ref, y_ref):
    y_ref[...] = x_ref[...] + _mm(o_ref[...], w_ref[...])


def _proj_res(o, x, w):
    n = o.shape[0]
    t = _row_tile(n)
    row = lambda i: (i, 0)
    return pl.pallas_call(
        _proj_res_kernel,
        grid=(n // t,),
        in_specs=[pl.BlockSpec((t, o.shape[1]), row), pl.BlockSpec((t, D_MODEL), row), _full(w.shape)],
        out_specs=pl.BlockSpec((t, D_MODEL), row),
        out_shape=jax.ShapeDtypeStruct((n, D_MODEL), F32),
        compiler_params=_params("arbitrary"),
        name="proj_res",
    )(o, x, w)


def _alibi_slopes():
    return np.exp2(-8.0 * np.arange(1, N_HEADS + 1, dtype=np.float64) / N_HEADS).astype(np.float32)


def _band_bias(dist, allowed):
    b = -(_alibi_slopes()[:, None, None] * dist.astype(np.float32)[None])
    return jnp.asarray(np.where(allowed[None], b, np.float32(NEG_INF)).astype(np.float32))


def _prompt_bias():
    dist = WINDOW + np.arange(WINDOW)[:, None] - np.arange(2 * WINDOW)[None, :]
    return _band_bias(dist, (dist >= 0) & (dist <= WINDOW))


def _sample_bias(t_new, n_cols):
    c = np.arange(n_cols)
    n_cache = SAMPLE_SEQS * WINDOW
    n_new = SAMPLE_SEQS * t_new
    is_cache = c < n_cache
    is_new = (c >= n_cache) & (c < n_cache + n_new)
    seq_c = np.where(is_cache, c // WINDOW, (c - n_cache) // t_new)
    pos_c = np.where(is_cache, c % WINDOW, WINDOW + (c - n_cache) % t_new)
    r = np.arange(n_new)
    seq_r, tok_r = r // t_new, r % t_new
    dist = WINDOW + tok_r[:, None] - pos_c[None, :]
    allowed = (seq_r[:, None] == seq_c[None, :]) & (is_cache | is_new)[None, :] & (dist >= 0) & (dist <= WINDOW)
    return _band_bias(dist, allowed)


def _head_indicator(n_heads):
    ch = np.arange(n_heads * HEAD_DIM) // HEAD_DIM
    ind = (ch[:, None] == np.arange(LANES)[None, :]).astype(np.float32)
    return jnp.asarray(ind, BF16), jnp.asarray(ind.T, BF16)


def kernel(x_prompt, x_sample, state_conv, cache_k, cache_v, p_prompt, p_sample, norm_mix, norm_ffn, norm_ple,
           conv_w_in, conv_b_in, conv_w_dw, conv_b_dw, conv_ln_g, conv_ln_b, conv_w_out, conv_b_out, attn_w_qkv,
           attn_q_norm, attn_k_norm, attn_sinks, attn_w_o, moe_w_rg, moe_b_rg, moe_w_re, moe_b_re, moe_w_gate,
           moe_w_up, moe_w_down, ple_w_gate, ple_w_proj):
    bp, seq, d = x_prompt.shape
    bs, t_new, _ = x_sample.shape
    assert bp == 1 and d == D_MODEL and seq % WINDOW == 0 and bs % SAMPLE_SEQS == 0
    assert seq % (bs * t_new) == 0 and (bs * t_new) % DMA_CHUNK == 0
    depth = norm_mix.shape[0]
    row2 = lambda a: a.reshape(1, -1)

    y_p = x_prompt.reshape(seq, d)
    y_s = x_sample.reshape(bs * t_new, d)
    conv_p, conv_s, k_p, v_p, k_s, v_s = [], [], [], [], [], []

    for i in range(depth):
        j = i // 2
        g_mix = row2(norm_mix[i])
        if i % 2 == 0:
            w_in = conv_w_in[j].astype(BF16)
            b_in = row2(conv_b_in[j])
            tail = (conv_w_dw[j], row2(conv_b_dw[j]), row2(conv_ln_g[j]), row2(conv_ln_b[j]),
                    conv_w_out[j].astype(BF16), row2(conv_b_out[j]))
            u_p = _conv_in(y_p, g_mix, w_in, b_in)
            u_s = _conv_in(y_s, g_mix, w_in, b_in)
            conv_p.append(u_p[seq - CONV_STATE:].reshape(1, CONV_STATE, d))
            y_p = _conv_out_p(u_p, y_p, *tail)
            ys_t, new_state = _conv_out_s(state_conv[j], u_s.reshape(bs, t_new, d),
                                          y_s.reshape(bs, t_new, d).transpose(1, 0, 2), *tail)
            conv_s.append(new_state)
            y_s = ys_t.transpose(1, 0, 2).reshape(bs * t_new, d)
        else:
            w_qkv = attn_w_qkv[j].astype(BF16)
            w_o = attn_w_o[j].astype(BF16)
            iq, iqt = _head_indicator(N_HEADS)
            ik, ikt = _head_indicator(N_KV_HEADS)
            qg = row2(jnp.tile(attn_q_norm[j], N_HEADS))
            kg = row2(jnp.tile(attn_k_norm[j], N_KV_HEADS))
            sinks = attn_sinks[j]
            q1, k1, v1 = _qkv(y_p, g_mix, w_qkv, iq, iqt, ik, ikt, qg, kg)
            q2, k2, v2 = _qkv(y_s, g_mix, w_qkv, iq, iqt, ik, ikt, qg, kg)
            k_p.append(k1[seq - WINDOW:].reshape(1, WINDOW, N_KV_HEADS, HEAD_DIM))
            v_p.append(v1[seq - WINDOW:].reshape(1, WINDOW, N_KV_HEADS, HEAD_DIM))
            y_p = _attn_p(sinks, q1, k1, v1, y_p, _prompt_bias(), w_o)
            n_cols = -(-(SAMPLE_SEQS * (WINDOW + t_new)) // LANES) * LANES
            o_s, nk, nv = _attn_s(sinks, q2, k2, v2, cache_k[j].reshape(bs * WINDOW, KV_DIM),
                                  cache_v[j].reshape(bs * WINDOW, KV_DIM), _sample_bias(t_new, n_cols), t_new)
            k_s.append(nk.reshape(bs, N_KV_HEADS, HEAD_DIM, WINDOW).transpose(0, 3, 1, 2))
            v_s.append(nv.reshape(bs, N_KV_HEADS, HEAD_DIM, WINDOW).transpose(0, 3, 1, 2))
            y_s = _proj_res(o_s, y_s, w_o)

        w_r = jnp.zeros((d, ROUTER_LANES), F32)
        w_r = w_r.at[:, :N_EXPERT_GROUPS].set(moe_w_rg[i]).at[:, EXPERT_LANE0:EXPERT_LANE0 + N_EXPERTS].set(moe_w_re[i])
        b_r = jnp.zeros((1, ROUTER_LANES), F32)
        b_r = b_r.at[0, :N_EXPERT_GROUPS].set(moe_b_rg[i]).at[0, EXPERT_LANE0:EXPERT_LANE0 + N_EXPERTS].set(moe_b_re[i])
        moe = (row2(norm_ffn[i]), jnp.stack(_split_bf16(w_r)), b_r, moe_w_gate, moe_w_up, moe_w_down)
        ple = (row2(norm_ple[i]), ple_w_gate[i].astype(BF16), ple_w_proj[i].astype(BF16))
        y_p, y_s = _moe_ple(i, y_p, y_s, *moe, p_prompt.reshape(depth, seq, PLE_DIM),
                            p_sample[i].reshape(bs * t_new, PLE_DIM), *ple)

    return (y_p.reshape(1, seq, d), y_s.reshape(bs, t_new, d), jnp.stack(conv_p), jnp.stack(conv_s),
            jnp.stack(k_p), jnp.stack(v_p), jnp.stack(k_s), jnp.stack(v_s))
```

```python
import functools

import numpy as np

import jax
import jax.numpy as jnp
from jax import lax
from jax.experimental import pallas as pl
from jax.experimental.pallas import tpu as pltpu

F32 = jnp.float32
BF16 = jnp.bfloat16

D_MODEL = 1024
PLE_DIM = 256
CONV_WIDTH = 31
CONV_STATE = CONV_WIDTH - 1
N_HEADS = 16
N_KV_HEADS = 4
HEAD_DIM = 64
GROUP = N_HEADS // N_KV_HEADS
WINDOW = 128
KV_DIM = N_KV_HEADS * HEAD_DIM
N_EXPERT_GROUPS = 4
EXPERTS_PER_GROUP = 4
N_EXPERTS = 16
D_EXPERT = 256
EPS = 1e-6
NEG_INF = -1e30

LANES = 128
ROUTER_LANES = LANES
EXPERT_LANE0 = N_EXPERT_GROUPS
HALO = 32
CONV_ROWS = 32
NORM_ROWS = 128
CONV_PITCH = D_MODEL // LANES + 1
SAMPLE_SEQS = 8
PAIRS_PER_GROUP = EXPERTS_PER_GROUP * (EXPERTS_PER_GROUP - 1) // 2
N_BUCKETS = N_EXPERT_GROUPS * PAIRS_PER_GROUP
TM = 256
PAY_WIDTH = D_MODEL + LANES
ROW_PITCH = PAY_WIDTH // LANES
DMA_CHUNK = 512
RING = 3
PLE_RING = 3
VMEM_LIMIT = 48 * 1024 * 1024


def _row_tile(n):
    return 512 if n % 512 == 0 else n


def _params(*sem):
    return pltpu.CompilerParams(dimension_semantics=sem, vmem_limit_bytes=VMEM_LIMIT)


def _full(shape):
    nd = len(shape)
    return pl.BlockSpec(shape, lambda *_: (0,) * nd)


def _rms(x, g):
    ms = jnp.mean(x * x, axis=-1, keepdims=True)
    return x * lax.rsqrt(ms + EPS) * g


def _sigmoid(x):
    return 1.0 / (1.0 + jnp.exp(-x))


def _mm(a, b):
    return jnp.dot(a, b, preferred_element_type=F32)


def _conv_in_kernel(x_ref, g_ref, w_ref, b_ref, u_ref):
    h = _rms(x_ref[...], g_ref[...]).astype(BF16)
    z = _mm(h, w_ref[...]) + b_ref[...]
    u_ref[...] = z[:, :D_MODEL] * _sigmoid(z[:, D_MODEL:])


def _conv_in(x, g, w, b):
    n = x.shape[0]
    t = _row_tile(n)
    return pl.pallas_call(
        _conv_in_kernel,
        grid=(n // t,),
        in_specs=[pl.BlockSpec((t, D_MODEL), lambda i: (i, 0)), _full(g.shape), _full(w.shape), _full(b.shape)],
        out_specs=pl.BlockSpec((t, D_MODEL), lambda i: (i, 0)),
        out_shape=jax.ShapeDtypeStruct((n, D_MODEL), F32),
        compiler_params=_params("arbitrary"),
        name="conv_in",
    )(x, g, w, b)


def _ln_silu(c, g, b):
    mu = jnp.mean(c, axis=-1, keepdims=True)
    xc = c - mu
    var = jnp.mean(xc * xc, axis=-1, keepdims=True)
    cn = xc * lax.rsqrt(var + EPS) * g + b
    return cn * _sigmoid(cn)


def _conv_out_p_kernel(u_ref, halo_ref, x_ref, wdw_ref, bdw_ref, lng_ref, lnb_ref, wout_ref, bout_ref, y_ref,
                       ubuf, cbuf, hbuf):
    t = u_ref.shape[0]
    i = pl.program_id(0)
    nj = D_MODEL // LANES

    def put(r, j, val):
        ubuf[pl.ds(r * CONV_PITCH + j, 8, stride=CONV_PITCH), :] = val

    for r in range(0, HALO, 8):
        for j in range(nj):
            put(r, j, jnp.where(i > 0, halo_ref[r:r + 8, j * LANES:(j + 1) * LANES], 0.0))

    def fill(rr, carry):
        r = pl.multiple_of(rr * 8, 8)
        for j in range(nj):
            put(r + HALO, j, u_ref[pl.ds(r, 8), j * LANES:(j + 1) * LANES])
        return carry

    lax.fori_loop(0, t // 8, fill, 0)

    def conv_chunk(rr, carry):
        r0 = pl.multiple_of(rr * CONV_ROWS, CONV_ROWS)
        for j in range(nj):
            lanes = slice(j * LANES, (j + 1) * LANES)
            accs = [None] * (CONV_ROWS // 8)
            for k in range(CONV_WIDTH):
                wk = wdw_ref[k:k + 1, lanes]
                for q in range(CONV_ROWS // 8):
                    r = r0 + (HALO - CONV_STATE + k + 8 * q)
                    term = wk * ubuf[pl.ds(r * CONV_PITCH + j, 8, stride=CONV_PITCH), :]
                    accs[q] = term if k == 0 else accs[q] + term
            for q in range(CONV_ROWS // 8):
                cbuf[pl.ds(r0 + 8 * q, 8), lanes] = accs[q]
        return carry

    lax.fori_loop(0, t // CONV_ROWS, conv_chunk, 0)

    def norm_chunk(rr, carry):
        r0 = pl.multiple_of(rr * NORM_ROWS, NORM_ROWS)
        c = cbuf[pl.ds(r0, NORM_ROWS), :] + bdw_ref[...]
        hbuf[pl.ds(r0, NORM_ROWS), :] = _ln_silu(c, lng_ref[...], lnb_ref[...]).astype(BF16)
        return carry

    lax.fori_loop(0, t // NORM_ROWS, norm_chunk, 0)
    y_ref[...] = x_ref[...] + _mm(hbuf[...], wout_ref[...]) + bout_ref[...]


def _conv_out_p(u, x, wdw, bdw, lng, lnb, wout, bout):
    n = u.shape[0]
    t = _row_tile(n)
    hb = t // HALO
    row = lambda i: (i, 0)
    return pl.pallas_call(
        _conv_out_p_kernel,
        grid=(n // t,),
        in_specs=[pl.BlockSpec((t, D_MODEL), row),
                  pl.BlockSpec((HALO, D_MODEL), lambda i: (jnp.maximum(i * hb - 1, 0), 0)),
                  pl.BlockSpec((t, D_MODEL), row),
                  _full(wdw.shape), _full(bdw.shape), _full(lng.shape), _full(lnb.shape),
                  _full(wout.shape), _full(bout.shape)],
        out_specs=pl.BlockSpec((t, D_MODEL), row),
        out_shape=jax.ShapeDtypeStruct((n, D_MODEL), F32),
        scratch_shapes=[pltpu.VMEM(((t + HALO) * CONV_PITCH, LANES), F32), pltpu.VMEM((t, D_MODEL), F32),
                        pltpu.VMEM((t, D_MODEL), BF16)],
        compiler_params=_params("arbitrary"),
        name="conv_out_prompt",
    )(u, u, x, wdw, bdw, lng, lnb, wout, bout)


def _conv_out_s_kernel(st_ref, u_ref, x_ref, wdw_ref, bdw_ref, lng_ref, lnb_ref, wout_ref, bout_ref, y_ref, ns_ref,
                       wbuf, cbuf):
    bb, nt, _ = u_ref.shape
    win = CONV_STATE + nt
    nj = D_MODEL // LANES
    seq_pitch = win * CONV_PITCH

    def put(b, r0, rows, j, val):
        wbuf[pl.ds((b * win + r0) * CONV_PITCH + j, rows, stride=CONV_PITCH), :] = val

    for b in range(bb):
        for j in range(nj):
            lanes = slice(j * LANES, (j + 1) * LANES)
            for r0 in range(0, CONV_STATE, 8):
                rows = min(8, CONV_STATE - r0)
                put(b, r0, rows, j, st_ref[b, r0:r0 + rows, lanes])
            put(b, CONV_STATE, nt, j, u_ref[b, :, lanes])

    for b in range(bb):
        for r0 in range(0, CONV_STATE, 8):
            rows = min(8, CONV_STATE - r0)
            for j in range(nj):
                src = (b * win + nt + r0) * CONV_PITCH + j
                ns_ref[b, r0:r0 + rows, j * LANES:(j + 1) * LANES] = wbuf[pl.ds(src, rows, stride=CONV_PITCH), :]

    for t in range(nt):
        for b0 in range(0, bb, 8):
            for j in range(nj):
                lanes = slice(j * LANES, (j + 1) * LANES)
                acc = None
                for k in range(CONV_WIDTH):
                    start = (b0 * win + t + k) * CONV_PITCH + j
                    term = wdw_ref[k:k + 1, lanes] * wbuf[pl.ds(start, 8, stride=seq_pitch), :]
                    acc = term if acc is None else acc + term
                cbuf[t * bb + b0:t * bb + b0 + 8, lanes] = acc
    c = cbuf[...] + bdw_ref[...]
    h = _ln_silu(c, lng_ref[...], lnb_ref[...]).astype(BF16)
    y = _mm(h, wout_ref[...]) + bout_ref[...]
    for t in range(nt):
        y_ref[t] = x_ref[t] + y[t * bb:(t + 1) * bb]


def _conv_out_s(state, u, x, wdw, bdw, lng, lnb, wout, bout):
    b, nt, _ = u.shape
    bb = 16 if b % 16 == 0 else b
    assert bb % 8 == 0
    win = CONV_STATE + nt
    seq = lambda i: (i, 0, 0)
    tm = lambda i: (0, i, 0)
    return pl.pallas_call(
        _conv_out_s_kernel,
        grid=(b // bb,),
        in_specs=[pl.BlockSpec((bb, CONV_STATE, D_MODEL), seq), pl.BlockSpec((bb, nt, D_MODEL), seq),
                  pl.BlockSpec((nt, bb, D_MODEL), tm),
                  _full(wdw.shape), _full(bdw.shape), _full(lng.shape), _full(lnb.shape),
                  _full(wout.shape), _full(bout.shape)],
        out_specs=[pl.BlockSpec((nt, bb, D_MODEL), tm), pl.BlockSpec((bb, CONV_STATE, D_MODEL), seq)],
        out_shape=[jax.ShapeDtypeStruct((nt, b, D_MODEL), F32), jax.ShapeDtypeStruct((b, CONV_STATE, D_MODEL), F32)],
        scratch_shapes=[pltpu.VMEM((bb * win * CONV_PITCH, LANES), F32), pltpu.VMEM((nt * bb, D_MODEL), F32)],
        compiler_params=_params("arbitrary"),
        name="conv_out_sample",
    )(state, u, x, wdw, bdw, lng, lnb, wout, bout)


def _route(lg):
    big = 3.0e38
    lane = lax.broadcasted_iota(jnp.int32, lg.shape, 1)
    lanef = lane.astype(F32)
    is_g = lane < N_EXPERT_GROUPS
    gl = jnp.where(is_g, lg, -big)
    gmax = jnp.max(gl, axis=1, keepdims=True)
    gsum = jnp.sum(jnp.where(is_g, jnp.exp(gl - gmax), 0.0), axis=1, keepdims=True)
    g_w = 1.0 / gsum
    g_idx = jnp.min(jnp.where(gl == gmax, lanef, big), axis=1, keepdims=True)
    rel = lanef - float(EXPERT_LANE0) - g_idx * float(EXPERTS_PER_GROUP)
    in_grp = jnp.where(rel >= 0.0, jnp.where(rel < float(EXPERTS_PER_GROUP), 1.0, 0.0), 0.0) > 0.5
    el = jnp.where(in_grp, lg, -big)
    e1 = jnp.max(el, axis=1, keepdims=True)
    i1 = jnp.min(jnp.where(el == e1, lanef, big), axis=1, keepdims=True)
    el2 = jnp.where(lanef == i1, -big, el)
    e2 = jnp.max(el2, axis=1, keepdims=True)
    i2 = jnp.min(jnp.where(el2 == e2, lanef, big), axis=1, keepdims=True)
    tt = jnp.exp(e2 - e1)
    w1 = g_w / (1.0 + tt)
    w2 = g_w * tt / (1.0 + tt)
    base = float(EXPERT_LANE0) + g_idx * float(EXPERTS_PER_GROUP)
    a = jnp.minimum(i1, i2) - base
    b = jnp.maximum(i1, i2) - base
    pair = a * (7.0 - a) * 0.5 + (b - a - 1.0)
    first_is_lo = i1 < i2
    return (g_idx * float(PAIRS_PER_GROUP) + pair, jnp.where(first_is_lo, w1, w2), jnp.where(first_is_lo, w2, w1))


def _to_row_linear(dst_ref, src_ref, n_tiles):
    def body(g, carry):
        r = pl.multiple_of(g * 8, 8)
        for j in range(n_tiles):
            dst_ref[pl.ds(r * ROW_PITCH + j, 8, stride=ROW_PITCH), :] = src_ref[pl.ds(r, 8), j * LANES:(j + 1) * LANES]
        return carry

    lax.fori_loop(0, src_ref.shape[0] // 8, body, 0)


def _from_row_linear(dst_ref, src_ref, n_tiles, pitch):
    def body(g, carry):
        r = pl.multiple_of(g * 8, 8)
        for j in range(n_tiles):
            dst_ref[pl.ds(r, 8), j * LANES:(j + 1) * LANES] = src_ref[pl.ds(r * pitch + j, 8, stride=pitch), :]
        return carry

    lax.fori_loop(0, dst_ref.shape[0] // 8, body, 0)


def _pick(i, n_prompt_tiles, prompt_ref, sample_ref):
    return jnp.where(i < n_prompt_tiles, prompt_ref[...], sample_ref[...])


def _route_kernel(n_prompt_tiles, yp_ref, ys_ref, g_ref, wr_ref, br_ref, tri_ref, pay_ref, meta_ref, cnt_ref, carry,
                  rowbuf):
    i = pl.program_id(0)

    @pl.when(i == 0)
    def _():
        carry[...] = jnp.zeros_like(carry)

    xf = _rms(_pick(i, n_prompt_tiles, yp_ref, ys_ref), g_ref[...])
    x_hi, x_lo = _split_bf16(xf)
    logits = _mm(x_hi, wr_ref[0]) + (_mm(x_lo, wr_ref[0]) + _mm(x_hi, wr_ref[1])) + br_ref[...]
    bucket, w_lo, w_hi = _route(logits)
    lane = lax.broadcasted_iota(jnp.int32, logits.shape, 1)
    onehot = jnp.where(lane.astype(F32) == bucket, 1.0, 0.0)
    before = _mm(tri_ref[...], onehot.astype(BF16)) + carry[...]
    rank = jnp.sum(onehot * before, axis=1, keepdims=True)
    carry[...] += jnp.sum(onehot, axis=0, keepdims=True)
    cnt_ref[...] = carry[...]
    meta = jnp.where(lane == 0, bucket, jnp.where(lane == 1, rank, 0.0))
    meta_ref[...] = jnp.transpose(meta)[:8, :].astype(jnp.int32)

    rowbuf[:, :D_MODEL] = xf
    rowbuf[:, D_MODEL:] = jnp.where(lane == 0, w_lo, jnp.where(lane == 1, w_hi, 0.0))
    _to_row_linear(pay_ref, rowbuf, ROW_PITCH)


def _route_call(y_p, y_s, g, wr, br, tri):
    t = y_s.shape[0]
    npt = y_p.shape[0] // t
    n = y_p.shape[0] + t
    pidx = lambda i: (jnp.minimum(i, npt - 1), 0)
    return pl.pallas_call(
        functools.partial(_route_kernel, npt),
        grid=(npt + 1,),
        in_specs=[pl.BlockSpec((t, D_MODEL), pidx), _full(y_s.shape), _full(g.shape), _full(wr.shape),
                  _full(br.shape), _full(tri.shape)],
        out_specs=[pl.BlockSpec((t * ROW_PITCH, LANES), lambda i: (i, 0)),
                   pl.BlockSpec((8, t), lambda i: (0, i)), _full((1, ROUTER_LANES))],
        out_shape=[jax.ShapeDtypeStruct((n * ROW_PITCH, LANES), F32),
                   jax.ShapeDtypeStruct((8, n), jnp.int32), jax.ShapeDtypeStruct((1, ROUTER_LANES), F32)],
        scratch_shapes=[pltpu.VMEM((1, ROUTER_LANES), F32), pltpu.VMEM((t, PAY_WIDTH), F32)],
        compiler_params=_params("arbitrary"),
        name="route",
    )(y_p, y_s, g, wr, br, tri)


def _permute_kernel(pos_ref, zstart_ref, nz_ref, used_ref, src_ref, dst_ref, zbuf, ring, zsem, in_sems, out_sems):
    zbuf[...] = jnp.zeros_like(zbuf)

    def zero_tile(start):
        return pltpu.make_async_copy(zbuf, dst_ref.at[pl.ds(pl.multiple_of(start * ROW_PITCH, 8), TM * ROW_PITCH)], zsem)

    for b in range(N_BUCKETS):
        @pl.when(nz_ref[b] > 0)
        def _():
            zero_tile(zstart_ref[b]).start()
    n_tiles = dst_ref.shape[0] // (TM * ROW_PITCH)

    def start_unused(i, carry):
        zero_tile(i * TM).start()
        return carry

    lax.fori_loop(used_ref[0], n_tiles, start_unused, 0)
    for b in range(N_BUCKETS):
        @pl.when(nz_ref[b] > 0)
        def _():
            zero_tile(0).wait()

    def wait_unused(i, carry):
        zero_tile(0).wait()
        return carry

    lax.fori_loop(used_ref[0], n_tiles, wait_unused, 0)

    chunk_rows = DMA_CHUNK * ROW_PITCH
    n_chunks = src_ref.shape[0] // chunk_rows

    def fetch(c, slot):
        return pltpu.make_async_copy(src_ref.at[pl.ds(pl.multiple_of(c * chunk_rows, 8), chunk_rows)], ring.at[slot],
                                     in_sems.at[slot])

    def drain(slot):
        pltpu.make_async_copy(ring.at[slot], dst_ref.at[pl.ds(0, chunk_rows)], out_sems.at[slot]).wait()

    fetch(0, 0).start()

    def step(c, carry):
        slot = lax.rem(c, RING)
        nxt = lax.rem(c + 1, RING)
        fetch(c, slot).wait()

        @pl.when(c >= RING - 1)
        def _():
            drain(nxt)

        @pl.when(c + 1 < n_chunks)
        def _():
            fetch(c + 1, nxt).start()

        def send(jj, carry2):
            row = pos_ref[c * DMA_CHUNK + jj] * ROW_PITCH
            pltpu.make_async_copy(ring.at[slot, pl.ds(jj * ROW_PITCH, ROW_PITCH)], dst_ref.at[pl.ds(row, ROW_PITCH)],
                                  out_sems.at[slot]).start()
            return carry2

        lax.fori_loop(0, DMA_CHUNK, send, 0, unroll=8)
        return carry

    lax.fori_loop(0, n_chunks, step, 0)
    for c in range(n_chunks - (RING - 1), n_chunks):
        drain(c % RING)


def _permute(pos, zstart, nz, used, payload, n_rows_out):
    assert payload.shape[0] // (DMA_CHUNK * ROW_PITCH) >= RING
    return pl.pallas_call(
        _permute_kernel,
        grid_spec=pltpu.PrefetchScalarGridSpec(
            num_scalar_prefetch=4, grid=(1,),
            in_specs=[pl.BlockSpec(memory_space=pl.ANY)],
            out_specs=pl.BlockSpec(memory_space=pl.ANY),
            scratch_shapes=[pltpu.VMEM((TM * ROW_PITCH, LANES), payload.dtype),
                            pltpu.VMEM((RING, DMA_CHUNK * ROW_PITCH, LANES), payload.dtype),
                            pltpu.SemaphoreType.DMA(()), pltpu.SemaphoreType.DMA((RING,)),
                            pltpu.SemaphoreType.DMA((RING,))]),
        out_shape=jax.ShapeDtypeStruct((n_rows_out * ROW_PITCH, LANES), payload.dtype),
        compiler_params=_params("arbitrary"),
        name="permute",
    )(pos, zstart, nz, used, payload)


def _silu(x):
    return x * _sigmoid(x)


def _experts_kernel(tix_ref, lo_ref, hi_ref, valid_ref, fresh_ref, xs_ref, wgl_ref, wgh_ref, wul_ref, wuh_ref,
                    wdl_ref, wdh_ref, ys_ref, wup_s, wdn_s, xbuf, ybuf):
    i = pl.program_id(0)

    @pl.when(fresh_ref[i] > 0)
    def _():
        for s, ref in enumerate((wgl_ref, wgh_ref, wul_ref, wuh_ref)):
            wup_s[s] = ref[0, 0].astype(BF16)
        for s, ref in enumerate((wdl_ref, wdh_ref)):
            wdn_s[s] = ref[0, 0].astype(BF16)

    @pl.when(valid_ref[i] == 0)
    def _():
        ys_ref[...] = jnp.zeros_like(ys_ref)

    @pl.when(valid_ref[i] > 0)
    def _():
        _from_row_linear(xbuf, xs_ref, ROW_PITCH, ROW_PITCH)
        x = xbuf[:, :D_MODEL].astype(BF16)
        gates = xbuf[:, D_MODEL:]
        h_lo = _silu(_mm(x, wup_s[0])) * _mm(x, wup_s[2]) * gates[:, 0:1]
        h_hi = _silu(_mm(x, wup_s[1])) * _mm(x, wup_s[3]) * gates[:, 1:2]
        ybuf[:, :D_MODEL] = _mm(h_lo.astype(BF16), wdn_s[0]) + _mm(h_hi.astype(BF16), wdn_s[1])
        ybuf[:, D_MODEL:] = jnp.zeros((TM, LANES), F32)
        _to_row_linear(ys_ref, ybuf, ROW_PITCH)


def _experts(layer, tix, lo, hi, valid, fresh, xs, wg, wu, wd):
    n_tiles = xs.shape[0] // (TM * ROW_PITCH)
    row = lambda i, tix, lo, hi, valid, fresh: (tix[i], 0)
    row_out = lambda i, tix, lo, hi, valid, fresh: (i, 0)
    e_lo = lambda i, tix, lo, hi, valid, fresh: (layer, lo[i], 0, 0)
    e_hi = lambda i, tix, lo, hi, valid, fresh: (layer, hi[i], 0, 0)
    up = pl.BlockSpec((1, 1, D_MODEL, D_EXPERT), e_lo), pl.BlockSpec((1, 1, D_MODEL, D_EXPERT), e_hi)
    down = pl.BlockSpec((1, 1, D_EXPERT, D_MODEL), e_lo), pl.BlockSpec((1, 1, D_EXPERT, D_MODEL), e_hi)
    return pl.pallas_call(
        _experts_kernel,
        grid_spec=pltpu.PrefetchScalarGridSpec(
            num_scalar_prefetch=5, grid=(n_tiles,),
            in_specs=[pl.BlockSpec((TM * ROW_PITCH, LANES), row), *up, *up, *down],
            out_specs=pl.BlockSpec((TM * ROW_PITCH, LANES), row_out),
            scratch_shapes=[pltpu.VMEM((4, D_MODEL, D_EXPERT), BF16), pltpu.VMEM((2, D_EXPERT, D_MODEL), BF16),
                            pltpu.VMEM((TM, PAY_WIDTH), F32), pltpu.VMEM((TM, PAY_WIDTH), F32)]),
        out_shape=jax.ShapeDtypeStruct(xs.shape, F32),
        compiler_params=_params("arbitrary"),
        name="experts",
    )(tix, lo, hi, valid, fresh, xs, wg, wg, wu, wu, wd, wd)


def _ple_kernel(n_prompt_tiles, pos_ref, yp_ref, ys_ref, moe_ref, pp_ref, ps_ref, g_ref, wg_ref, wp_ref, op_ref, os_ref,
                mring, mbuf, sems):
    i = pl.program_id(0)
    t = mbuf.shape[0]
    w = D_MODEL // LANES

    def gather(tile, slot):
        def body(jj, carry):
            row = pos_ref[tile * t + jj] * ROW_PITCH
            pltpu.make_async_copy(moe_ref.at[pl.ds(row, w)], mring.at[slot, pl.ds(jj * w, w)], sems.at[slot]).start()
            return carry

        lax.fori_loop(0, t, body, 0, unroll=8)

    def wait_tile(slot):
        pltpu.make_async_copy(moe_ref.at[pl.ds(0, t * w)], mring.at[slot], sems.at[slot]).wait()

    @pl.when(i == 0)
    def _():
        gather(0, 0)
        gather(1, 1)

    slot = lax.rem(i, PLE_RING)
    nslot = lax.rem(i + 2, PLE_RING)
    wait_tile(slot)
    _from_row_linear(mbuf, mring.at[slot], w, w)
    y2 = _pick(i, n_prompt_tiles, yp_ref, ys_ref) + mbuf[...]
    hn = _rms(y2, g_ref[...]).astype(BF16)
    gt = _sigmoid(_mm(hn, wg_ref[...]))
    pr = _mm(jnp.where(i < n_prompt_tiles, pp_ref[0], ps_ref[...]).astype(BF16), wp_ref[...])
    y3 = y2 + gt * pr
    nxt = jnp.minimum(i + 2, n_prompt_tiles) * t
    for jj in range(t):
        row = pos_ref[nxt + jj] * ROW_PITCH
        pltpu.make_async_copy(moe_ref.at[pl.ds(row, w)], mring.at[nslot, pl.ds(jj * w, w)], sems.at[nslot]).start()

    @pl.when(i < n_prompt_tiles)
    def _():
        op_ref[...] = y3

    @pl.when(i >= n_prompt_tiles)
    def _():
        os_ref[...] = y3
        wait_tile(lax.rem(i + 1, PLE_RING))
        wait_tile(nslot)


def _ple(layer, pos, y_p, y_s, ys_sorted, p_p, p_s, g, wg, wp):
    t = y_s.shape[0]
    npt = y_p.shape[0] // t
    w = D_MODEL // LANES
    pidx = lambda i, pos: (jnp.minimum(i, npt - 1), 0)
    full = lambda shape: pl.BlockSpec(shape, lambda i, pos: (0,) * len(shape))
    return pl.pallas_call(
        functools.partial(_ple_kernel, npt),
        grid_spec=pltpu.PrefetchScalarGridSpec(
            num_scalar_prefetch=1, grid=(npt + 1,),
            in_specs=[pl.BlockSpec((t, D_MODEL), pidx), full(y_s.shape), pl.BlockSpec(memory_space=pl.ANY),
                      pl.BlockSpec((1, t, PLE_DIM), lambda i, pos: (layer, jnp.minimum(i, npt - 1), 0)),
                      full(p_s.shape), full(g.shape), full(wg.shape), full(wp.shape)],
            out_specs=[pl.BlockSpec((t, D_MODEL), pidx), full(y_s.shape)],
            scratch_shapes=[pltpu.VMEM((PLE_RING, t * w, LANES), F32), pltpu.VMEM((t, D_MODEL), F32),
                            pltpu.SemaphoreType.DMA((PLE_RING,))]),
        out_shape=[jax.ShapeDtypeStruct(y_p.shape, F32), jax.ShapeDtypeStruct(y_s.shape, F32)],
        compiler_params=_params("arbitrary"),
        name="ple",
    )(pos, y_p, y_s, ys_sorted, p_p, p_s, g, wg, wp)


def _bucket_experts():
    pairs = [(a, b) for a in range(EXPERTS_PER_GROUP) for b in range(a + 1, EXPERTS_PER_GROUP)]
    lo = [g * EXPERTS_PER_GROUP + a for g in range(N_EXPERT_GROUPS) for a, _ in pairs]
    hi = [g * EXPERTS_PER_GROUP + b for g in range(N_EXPERT_GROUPS) for _, b in pairs]
    return jnp.array(lo, jnp.int32), jnp.array(hi, jnp.int32)


def _moe_ple(layer, y_p, y_s, gffn, wr, br, wg, wu, wd, p_p, p_s, gple, pwg, pwp):
    t = y_s.shape[0]
    n = y_p.shape[0] + t
    tri = jnp.asarray(np.tri(t, t, -1, dtype=np.float32), BF16)
    payload, meta, counts = _route_call(y_p, y_s, gffn, wr, br, tri)

    bucket, rank = meta[0], meta[1]
    cnt = counts[0, :N_BUCKETS].astype(jnp.int32)
    padded = (cnt + TM - 1) // TM * TM
    ends = jnp.cumsum(padded)
    starts = ends - padded
    in_bucket = bucket[:, None] == jnp.arange(N_BUCKETS, dtype=jnp.int32)[None, :]
    pos = rank + jnp.sum(jnp.where(in_bucket, starts[None, :], 0), axis=1)
    n_rows = n + N_BUCKETS * TM
    n_tiles = n_rows // TM
    tile_start = jnp.arange(n_tiles, dtype=jnp.int32) * TM
    valid = tile_start < ends[-1]
    used = (ends[-1] // TM).astype(jnp.int32)
    tix = jnp.arange(n_tiles, dtype=jnp.int32)
    tile_bucket = jnp.sum((ends[None, :] <= (jnp.minimum(tix, used - 1) * TM)[:, None]).astype(jnp.int32), axis=1)
    prev_bucket = jnp.concatenate([jnp.full((1,), -1, jnp.int32), tile_bucket[:-1]])
    fresh = valid & (tile_bucket != prev_bucket)
    first_tile = jnp.where(fresh, tix, n_tiles)
    next_first = lax.cummin(jnp.concatenate([first_tile[1:], jnp.full((1,), n_tiles, jnp.int32)]), reverse=True)
    next_bucket = jnp.where(next_first < n_tiles, jnp.take(tile_bucket, jnp.minimum(next_first, n_tiles - 1)),
                            tile_bucket)
    resident = jnp.where(fresh, tile_bucket, next_bucket)
    lo_tab, hi_tab = _bucket_experts()
    lo = jnp.take(lo_tab, resident)
    hi = jnp.take(hi_tab, resident)

    xs = _permute(pos, (ends - TM).astype(jnp.int32), (cnt > 0).astype(jnp.int32), used.reshape(1), payload, n_rows)
    ys = _experts(layer, jnp.minimum(tix, used - 1), lo, hi, valid.astype(jnp.int32), fresh.astype(jnp.int32), xs, wg,
                  wu, wd)
    return _ple(layer, pos, y_p, y_s, ys, p_p, p_s, gple, pwg, pwp)


def _split_bf16(a):
    hi = a.astype(BF16)
    lo = (a - hi.astype(F32)).astype(BF16)
    return hi, lo


def _head_norm(a, ind_ref, indt_ref, gain):
    hi, lo = _split_bf16(a * a)
    ss = _mm(hi, ind_ref[...]) + _mm(lo, ind_ref[...])
    inv = lax.rsqrt(ss * (1.0 / HEAD_DIM) + EPS)
    ihi, ilo = _split_bf16(inv)
    invb = _mm(ihi, indt_ref[...]) + _mm(ilo, indt_ref[...])
    return a * invb * gain


def _qkv_kernel(x_ref, g_ref, w_ref, iq_ref, iqt_ref, ik_ref, ikt_ref, qg_ref, kg_ref, q_ref, k_ref, v_ref):
    h = _rms(x_ref[...], g_ref[...]).astype(BF16)
    qkv = _mm(h, w_ref[...])
    nq = N_HEADS * HEAD_DIM
    q = _head_norm(qkv[:, :nq], iq_ref, iqt_ref, qg_ref[...])
    k = _head_norm(qkv[:, nq:nq + KV_DIM], ik_ref, ikt_ref, kg_ref[...])
    q_ref[...] = (q * (HEAD_DIM ** -0.5)).astype(BF16)
    k_ref[...] = k
    v_ref[...] = qkv[:, nq + KV_DIM:]


def _qkv(x, g, w, iq, iqt, ik, ikt, qg, kg):
    n = x.shape[0]
    t = _row_tile(n)
    row = lambda i: (i, 0)
    return pl.pallas_call(
        _qkv_kernel,
        grid=(n // t,),
        in_specs=[pl.BlockSpec((t, D_MODEL), row)] + [_full(a.shape) for a in (g, w, iq, iqt, ik, ikt, qg, kg)],
        out_specs=[pl.BlockSpec((t, N_HEADS * HEAD_DIM), row), pl.BlockSpec((t, KV_DIM), row),
                   pl.BlockSpec((t, KV_DIM), row)],
        out_shape=[jax.ShapeDtypeStruct((n, N_HEADS * HEAD_DIM), BF16), jax.ShapeDtypeStruct((n, KV_DIM), F32),
                   jax.ShapeDtypeStruct((n, KV_DIM), F32)],
        compiler_params=_params("arbitrary"),
        name="qkv",
    )(x, g, w, iq, iqt, ik, ikt, qg, kg)


def _dup_heads(a):
    out = []
    for s in range(KV_DIM // LANES):
        sl = a[:, s * LANES:(s + 1) * LANES]
        sw = pltpu.roll(sl, HEAD_DIM, axis=1)
        low = lax.broadcasted_iota(jnp.int32, sl.shape, 1) < HEAD_DIM
        out.append(jnp.where(low, sl, sw))
        out.append(jnp.where(low, sw, sl))
    return jnp.concatenate(out, axis=1).astype(BF16)


def _attend(q_rows, k2, v2, bias_of, sink_of, extra_mask):
    m_rows = q_rows.shape[0]
    low_q = lax.broadcasted_iota(jnp.int32, (m_rows, LANES), 1) < HEAD_DIM
    low_k = lax.broadcasted_iota(jnp.int32, (k2.shape[0], LANES), 1) < HEAD_DIM
    zero_q = jnp.zeros((m_rows, LANES), BF16)
    zero_k = jnp.zeros((k2.shape[0], LANES), BF16)
    slabs = []
    for g in range(N_KV_HEADS):
        kg = k2[:, g * LANES:(g + 1) * LANES]
        vg = v2[:, g * LANES:(g + 1) * LANES]
        v_lo = jnp.where(low_k, vg, zero_k)
        v_hi = jnp.where(low_k, zero_k, vg)
        lhs = []
        for a in range(GROUP):
            h = g * GROUP + a
            qs = q_rows[:, (h // 2) * LANES:(h // 2 + 1) * LANES]
            lhs.append(jnp.where(low_q, qs, zero_q) if h % 2 == 0 else jnp.where(low_q, zero_q, qs))
        s = lax.dot_general(jnp.concatenate(lhs, axis=0), kg, (((1,), (1,)), ((), ())), preferred_element_type=F32)
        probs, rinv = [], []
        for a in range(GROUP):
            h = g * GROUP + a
            sa = s[a * m_rows:(a + 1) * m_rows] + bias_of(h)
            if extra_mask is not None:
                sa = jnp.where(extra_mask, NEG_INF, sa)
            sink = sink_of(h)
            m = jnp.maximum(jnp.max(sa, axis=1, keepdims=True), sink)
            p = jnp.exp(sa - m)
            den = jnp.sum(p, axis=1, keepdims=True) + jnp.exp(sink - m)
            probs.append(p.astype(BF16))
            rinv.append(1.0 / den)
        for sp in range(GROUP // 2):
            o = _mm(probs[2 * sp], v_lo) + _mm(probs[2 * sp + 1], v_hi)
            slabs.append(o * jnp.where(low_q, rinv[2 * sp], rinv[2 * sp + 1]))
    return jnp.concatenate(slabs, axis=1)


def _attn_p_kernel(sink_ref, q_ref, k_ref, v_ref, x_ref, bias_ref, wo_ref, y_ref, kbuf, vbuf, obuf):
    t = q_ref.shape[0]
    i = pl.program_id(0)

    @pl.when(i == 0)
    def _():
        kbuf[0:WINDOW, :] = jnp.zeros((WINDOW, 2 * KV_DIM), BF16)
        vbuf[0:WINDOW, :] = jnp.zeros((WINDOW, 2 * KV_DIM), BF16)

    @pl.when(i > 0)
    def _():
        kbuf[0:WINDOW, :] = kbuf[t:t + WINDOW, :]
        vbuf[0:WINDOW, :] = vbuf[t:t + WINDOW, :]

    kbuf[WINDOW:, :] = _dup_heads(k_ref[...])
    vbuf[WINDOW:, :] = _dup_heads(v_ref[...])
    col = lax.broadcasted_iota(jnp.int32, (WINDOW, 2 * WINDOW), 1)
    for j in range(t // WINDOW):
        rows = slice(j * WINDOW, (j + 1) * WINDOW)
        keys = slice(j * WINDOW, (j + 2) * WINDOW)
        extra = jnp.logical_and(i == 0, col < WINDOW) if j == 0 else None
        o = _attend(q_ref[rows, :], kbuf[keys, :], vbuf[keys, :], lambda h: bias_ref[h], lambda h: sink_ref[h], extra)
        obuf[rows, :] = o.astype(BF16)
    y_ref[...] = x_ref[...] + _mm(obuf[...], wo_ref[...])


def _attn_p(sinks, q, k, v, x, bias, wo):
    n = q.shape[0]
    t = _row_tile(n)
    row = lambda i: (i, 0)
    return pl.pallas_call(
        _attn_p_kernel,
        grid=(n // t,),
        in_specs=[pl.BlockSpec(memory_space=pltpu.SMEM),
                  pl.BlockSpec((t, N_HEADS * HEAD_DIM), row), pl.BlockSpec((t, KV_DIM), row),
                  pl.BlockSpec((t, KV_DIM), row), pl.BlockSpec((t, D_MODEL), row),
                  _full(bias.shape), _full(wo.shape)],
        out_specs=pl.BlockSpec((t, D_MODEL), row),
        out_shape=jax.ShapeDtypeStruct((n, D_MODEL), F32),
        scratch_shapes=[pltpu.VMEM((t + WINDOW, 2 * KV_DIM), BF16), pltpu.VMEM((t + WINDOW, 2 * KV_DIM), BF16),
                        pltpu.VMEM((t, N_HEADS * HEAD_DIM), BF16)],
        compiler_params=_params("arbitrary"),
        name="attn_prompt",
    )(sinks, q, k, v, x, bias, wo)


def _attn_s_kernel(sink_ref, q_ref, k_ref, v_ref, ck_ref, cv_ref, bias_ref, o_ref, nk_ref, nv_ref, shift_buf):
    pad = bias_ref.shape[2] - ck_ref.shape[0] - k_ref.shape[0]
    zpad = jnp.zeros((pad, KV_DIM), F32)
    k2 = _dup_heads(jnp.concatenate([ck_ref[...], k_ref[...], zpad], axis=0))
    v2 = _dup_heads(jnp.concatenate([cv_ref[...], v_ref[...], zpad], axis=0))
    o = _attend(q_ref[...], k2, v2, lambda h: bias_ref[h], lambda h: sink_ref[h], None)
    o_ref[...] = o.astype(BF16)
    t_new = k_ref.shape[0] // SAMPLE_SEQS
    for new_ref, old_ref, add_ref in ((nk_ref, ck_ref, k_ref), (nv_ref, cv_ref, v_ref)):
        for b in range(SAMPLE_SEQS):
            r = b * WINDOW
            shift_buf[0:WINDOW - t_new, :] = old_ref[r + t_new:r + WINDOW, :]
            shift_buf[WINDOW - t_new:WINDOW, :] = add_ref[b * t_new:(b + 1) * t_new, :]
            new_ref[b] = jnp.transpose(shift_buf[...])


def _attn_s(sinks, q, k, v, ck, cv, bias, t_new):
    n = q.shape[0]
    rows = SAMPLE_SEQS * t_new
    crow = SAMPLE_SEQS * WINDOW
    row = lambda i: (i, 0)
    return pl.pallas_call(
        _attn_s_kernel,
        grid=(n // rows,),
        in_specs=[pl.BlockSpec(memory_space=pltpu.SMEM),
                  pl.BlockSpec((rows, N_HEADS * HEAD_DIM), row), pl.BlockSpec((rows, KV_DIM), row),
                  pl.BlockSpec((rows, KV_DIM), row), pl.BlockSpec((crow, KV_DIM), row),
                  pl.BlockSpec((crow, KV_DIM), row), _full(bias.shape)],
        out_specs=[pl.BlockSpec((rows, N_HEADS * HEAD_DIM), row),
                   pl.BlockSpec((SAMPLE_SEQS, KV_DIM, WINDOW), lambda i: (i, 0, 0)),
                   pl.BlockSpec((SAMPLE_SEQS, KV_DIM, WINDOW), lambda i: (i, 0, 0))],
        out_shape=[jax.ShapeDtypeStruct((n, N_HEADS * HEAD_DIM), BF16),
                   jax.ShapeDtypeStruct((ck.shape[0] // WINDOW, KV_DIM, WINDOW), F32),
                   jax.ShapeDtypeStruct((cv.shape[0] // WINDOW, KV_DIM, WINDOW), F32)],
        scratch_shapes=[pltpu.VMEM((WINDOW, KV_DIM), F32)],
        compiler_params=_params("arbitrary"),
        name="attn_sample",
    )(sinks, q, k, v, ck, cv, bias)


def _proj_res_kernel(o_ref, x_ref, w_ref, y_ref):
    y_ref[...] = x_ref[...] + _mm(o_ref[...], w_ref[...])


def _proj_res(o, x, w):
    n = o.shape[0]
    t = _row_tile(n)
    row = lambda i: (i, 0)
    return pl.pallas_call(
        _proj_res_kernel,
        grid=(n // t,),
        in_specs=[pl.BlockSpec((t, o.shape[1]), row), pl.BlockSpec((t, D_MODEL), row), _full(w.shape)],
        out_specs=pl.BlockSpec((t, D_MODEL), row),
        out_shape=jax.ShapeDtypeStruct((n, D_MODEL), F32),
        compiler_params=_params("arbitrary"),
        name="proj_res",
    )(o, x, w)


def _alibi_slopes():
    return np.exp2(-8.0 * np.arange(1, N_HEADS + 1, dtype=np.float64) / N_HEADS).astype(np.float32)


def _band_bias(dist, allowed):
    b = -(_alibi_slopes()[:, None, None] * dist.astype(np.float32)[None])
    return jnp.asarray(np.where(allowed[None], b, np.float32(NEG_INF)).astype(np.float32))


def _prompt_bias():
    dist = WINDOW + np.arange(WINDOW)[:, None] - np.arange(2 * WINDOW)[None, :]
    return _band_bias(dist, (dist >= 0) & (dist <= WINDOW))


def _sample_bias(t_new, n_cols):
    c = np.arange(n_cols)
    n_cache = SAMPLE_SEQS * WINDOW
    n_new = SAMPLE_SEQS * t_new
    is_cache = c < n_cache
    is_new = (c >= n_cache) & (c < n_cache + n_new)
    seq_c = np.where(is_cache, c // WINDOW, (c - n_cache) // t_new)
    pos_c = np.where(is_cache, c % WINDOW, WINDOW + (c - n_cache) % t_new)
    r = np.arange(n_new)
    seq_r, tok_r = r // t_new, r % t_new
    dist = WINDOW + tok_r[:, None] - pos_c[None, :]
    allowed = (seq_r[:, None] == seq_c[None, :]) & (is_cache | is_new)[None, :] & (dist >= 0) & (dist <= WINDOW)
    return _band_bias(dist, allowed)


def _head_indicator(n_heads):
    ch = np.arange(n_heads * HEAD_DIM) // HEAD_DIM
    ind = (ch[:, None] == np.arange(LANES)[None, :]).astype(np.float32)
    return jnp.asarray(ind, BF16), jnp.asarray(ind.T, BF16)


def kernel(x_prompt, x_sample, state_conv, cache_k, cache_v, p_prompt, p_sample, norm_mix, norm_ffn, norm_ple,
           conv_w_in, conv_b_in, conv_w_dw, conv_b_dw, conv_ln_g, conv_ln_b, conv_w_out, conv_b_out, attn_w_qkv,
           attn_q_norm, attn_k_norm, attn_sinks, attn_w_o, moe_w_rg, moe_b_rg, moe_w_re, moe_b_re, moe_w_gate,
           moe_w_up, moe_w_down, ple_w_gate, ple_w_proj):
    bp, seq, d = x_prompt.shape
    bs, t_new, _ = x_sample.shape
    assert bp == 1 and d == D_MODEL and seq % WINDOW == 0 and bs % SAMPLE_SEQS == 0
    assert seq % (bs * t_new) == 0 and (bs * t_new) % DMA_CHUNK == 0
    depth = norm_mix.shape[0]
    row2 = lambda a: a.reshape(1, -1)

    y_p = x_prompt.reshape(seq, d)
    y_s = x_sample.reshape(bs * t_new, d)
    conv_p, conv_s, k_p, v_p, k_s, v_s = [], [], [], [], [], []

    for i in range(depth):
        j = i // 2
        g_mix = row2(norm_mix[i])
        if i % 2 == 0:
            w_in = conv_w_in[j].astype(BF16)
            b_in = row2(conv_b_in[j])
            tail = (conv_w_dw[j], row2(conv_b_dw[j]), row2(conv_ln_g[j]), row2(conv_ln_b[j]),
                    conv_w_out[j].astype(BF16), row2(conv_b_out[j]))
            u_p = _conv_in(y_p, g_mix, w_in, b_in)
            u_s = _conv_in(y_s, g_mix, w_in, b_in)
            conv_p.append(u_p[seq - CONV_STATE:].reshape(1, CONV_STATE, d))
            y_p = _conv_out_p(u_p, y_p, *tail)
            ys_t, new_state = _conv_out_s(state_conv[j], u_s.reshape(bs, t_new, d),
                                          y_s.reshape(bs, t_new, d).transpose(1, 0, 2), *tail)
            conv_s.append(new_state)
            y_s = ys_t.transpose(1, 0, 2).reshape(bs * t_new, d)
        else:
            w_qkv = attn_w_qkv[j].astype(BF16)
            w_o = attn_w_o[j].astype(BF16)
            iq, iqt = _head_indicator(N_HEADS)
            ik, ikt = _head_indicator(N_KV_HEADS)
            qg = row2(jnp.tile(attn_q_norm[j], N_HEADS))
            kg = row2(jnp.tile(attn_k_norm[j], N_KV_HEADS))
            sinks = attn_sinks[j]
            q1, k1, v1 = _qkv(y_p, g_mix, w_qkv, iq, iqt, ik, ikt, qg, kg)
            q2, k2, v2 = _qkv(y_s, g_mix, w_qkv, iq, iqt, ik, ikt, qg, kg)
            k_p.append(k1[seq - WINDOW:].reshape(1, WINDOW, N_KV_HEADS, HEAD_DIM))
            v_p.append(v1[seq - WINDOW:].reshape(1, WINDOW, N_KV_HEADS, HEAD_DIM))
            y_p = _attn_p(sinks, q1, k1, v1, y_p, _prompt_bias(), w_o)
            n_cols = -(-(SAMPLE_SEQS * (WINDOW + t_new)) // LANES) * LANES
            o_s, nk, nv = _attn_s(sinks, q2, k2, v2, cache_k[j].reshape(bs * WINDOW, KV_DIM),
                                  cache_v[j].reshape(bs * WINDOW, KV_DIM), _sample_bias(t_new, n_cols), t_new)
            k_s.append(nk.reshape(bs, N_KV_HEADS, HEAD_DIM, WINDOW).transpose(0, 3, 1, 2))
            v_s.append(nv.reshape(bs, N_KV_HEADS, HEAD_DIM, WINDOW).transpose(0, 3, 1, 2))
            y_s = _proj_res(o_s, y_s, w_o)

        w_r = jnp.zeros((d, ROUTER_LANES), F32)
        w_r = w_r.at[:, :N_EXPERT_GROUPS].set(moe_w_rg[i]).at[:, EXPERT_LANE0:EXPERT_LANE0 + N_EXPERTS].set(moe_w_re[i])
        b_r = jnp.zeros((1, ROUTER_LANES), F32)
        b_r = b_r.at[0, :N_EXPERT_GROUPS].set(moe_b_rg[i]).at[0, EXPERT_LANE0:EXPERT_LANE0 + N_EXPERTS].set(moe_b_re[i])
        moe = (row2(norm_ffn[i]), jnp.stack(_split_bf16(w_r)), b_r, moe_w_gate, moe_w_up, moe_w_down)
        ple = (row2(norm_ple[i]), ple_w_gate[i].astype(BF16), ple_w_proj[i].astype(BF16))
        y_p, y_s = _moe_ple(i, y_p, y_s, *moe, p_prompt.reshape(depth, seq, PLE_DIM),
                            p_sample[i].reshape(bs * t_new, PLE_DIM), *ple)

    return (y_p.reshape(1, seq, d), y_s.reshape(bs, t_new, d), jnp.stack(conv_p), jnp.stack(conv_s),
            jnp.stack(k_p), jnp.stack(v_p), jnp.stack(k_s), jnp.stack(v_s))
```

```python
import functools

import numpy as np

import jax
import jax.numpy as jnp
from jax import lax
from jax.experimental import pallas as pl
from jax.experimental.pallas import tpu as pltpu

F32 = jnp.float32
BF16 = jnp.bfloat16

D_MODEL = 1024
PLE_DIM = 256
CONV_WIDTH = 31
CONV_STATE = CONV_WIDTH - 1
N_HEADS = 16
N_KV_HEADS = 4
HEAD_DIM = 64
GROUP = N_HEADS // N_KV_HEADS
WINDOW = 128
KV_DIM = N_KV_HEADS * HEAD_DIM
N_EXPERT_GROUPS = 4
EXPERTS_PER_GROUP = 4
N_EXPERTS = 16
D_EXPERT = 256
EPS = 1e-6
NEG_INF = -1e30

LANES = 128
ROUTER_LANES = LANES
EXPERT_LANE0 = N_EXPERT_GROUPS
HALO = 32
CONV_ROWS = 32
NORM_ROWS = 128
CONV_PITCH = D_MODEL // LANES + 1
SAMPLE_SEQS = 8
PAIRS_PER_GROUP = EXPERTS_PER_GROUP * (EXPERTS_PER_GROUP - 1) // 2
N_BUCKETS = N_EXPERT_GROUPS * PAIRS_PER_GROUP
TM = 256
PAY_WIDTH = D_MODEL + LANES
ROW_PITCH = PAY_WIDTH // LANES
DMA_CHUNK = 512
RING = 3
PLE_RING = 3
VMEM_LIMIT = 48 * 1024 * 1024


def _row_tile(n):
    return 512 if n % 512 == 0 else n


def _params(*sem):
    return pltpu.CompilerParams(dimension_semantics=sem, vmem_limit_bytes=VMEM_LIMIT)


def _full(shape):
    nd = len(shape)
    return pl.BlockSpec(shape, lambda *_: (0,) * nd)


def _rms(x, g):
    ms = jnp.mean(x * x, axis=-1, keepdims=True)
    return x * lax.rsqrt(ms + EPS) * g


def _sigmoid(x):
    return 1.0 / (1.0 + jnp.exp(-x))


def _mm(a, b):
    return jnp.dot(a, b, preferred_element_type=F32)


def _conv_in_kernel(x_ref, g_ref, w_ref, b_ref, u_ref):
    h = _rms(x_ref[...], g_ref[...]).astype(BF16)
    z = _mm(h, w_ref[...]) + b_ref[...]
    u_ref[...] = z[:, :D_MODEL] * _sigmoid(z[:, D_MODEL:])


def _conv_in(x, g, w, b):
    n = x.shape[0]
    t = _row_tile(n)
    return pl.pallas_call(
        _conv_in_kernel,
        grid=(n // t,),
        in_specs=[pl.BlockSpec((t, D_MODEL), lambda i: (i, 0)), _full(g.shape), _full(w.shape), _full(b.shape)],
        out_specs=pl.BlockSpec((t, D_MODEL), lambda i: (i, 0)),
        out_shape=jax.ShapeDtypeStruct((n, D_MODEL), F32),
        compiler_params=_params("arbitrary"),
        name="conv_in",
    )(x, g, w, b)


def _ln_silu(c, g, b):
    mu = jnp.mean(c, axis=-1, keepdims=True)
    xc = c - mu
    var = jnp.mean(xc * xc, axis=-1, keepdims=True)
    cn = xc * lax.rsqrt(var + EPS) * g + b
    return cn * _sigmoid(cn)


def _conv_out_p_kernel(u_ref, halo_ref, x_ref, wdw_ref, bdw_ref, lng_ref, lnb_ref, wout_ref, bout_ref, y_ref,
                       ubuf, cbuf, hbuf):
    t = u_ref.shape[0]
    i = pl.program_id(0)
    nj = D_MODEL // LANES

    def put(r, j, val):
        ubuf[pl.ds(r * CONV_PITCH + j, 8, stride=CONV_PITCH), :] = val

    for r in range(0, HALO, 8):
        for j in range(nj):
            put(r, j, jnp.where(i > 0, halo_ref[r:r + 8, j * LANES:(j + 1) * LANES], 0.0))

    def fill(rr, carry):
        r = pl.multiple_of(rr * 8, 8)
        for j in range(nj):
            put(r + HALO, j, u_ref[pl.ds(r, 8), j * LANES:(j + 1) * LANES])
        return carry

    lax.fori_loop(0, t // 8, fill, 0)

    def conv_chunk(rr, carry):
        r0 = pl.multiple_of(rr * CONV_ROWS, CONV_ROWS)
        for j in range(nj):
            lanes = slice(j * LANES, (j + 1) * LANES)
            accs = [None] * (CONV_ROWS // 8)
            for k in range(CONV_WIDTH):
                wk = wdw_ref[k:k + 1, lanes]
                for q in range(CONV_ROWS // 8):
                    r = r0 + (HALO - CONV_STATE + k + 8 * q)
                    term = wk * ubuf[pl.ds(r * CONV_PITCH + j, 8, stride=CONV_PITCH), :]
                    accs[q] = term if k == 0 else accs[q] + term
            for q in range(CONV_ROWS // 8):
                cbuf[pl.ds(r0 + 8 * q, 8), lanes] = accs[q]
        return carry

    lax.fori_loop(0, t // CONV_ROWS, conv_chunk, 0)

    def norm_chunk(rr, carry):
        r0 = pl.multiple_of(rr * NORM_ROWS, NORM_ROWS)
        c = cbuf[pl.ds(r0, NORM_ROWS), :] + bdw_ref[...]
        hbuf[pl.ds(r0, NORM_ROWS), :] = _ln_silu(c, lng_ref[...], lnb_ref[...]).astype(BF16)
        return carry

    lax.fori_loop(0, t // NORM_ROWS, norm_chunk, 0)
    y_ref[...] = x_ref[...] + _mm(hbuf[...], wout_ref[...]) + bout_ref[...]


def _conv_out_p(u, x, wdw, bdw, lng, lnb, wout, bout):
    n = u.shape[0]
    t = _row_tile(n)
    hb = t // HALO
    row = lambda i: (i, 0)
    return pl.pallas_call(
        _conv_out_p_kernel,
        grid=(n // t,),
        in_specs=[pl.BlockSpec((t, D_MODEL), row),
                  pl.BlockSpec((HALO, D_MODEL), lambda i: (jnp.maximum(i * hb - 1, 0), 0)),
                  pl.BlockSpec((t, D_MODEL), row),
                  _full(wdw.shape), _full(bdw.shape), _full(lng.shape), _full(lnb.shape),
                  _full(wout.shape), _full(bout.shape)],
        out_specs=pl.BlockSpec((t, D_MODEL), row),
        out_shape=jax.ShapeDtypeStruct((n, D_MODEL), F32),
        scratch_shapes=[pltpu.VMEM(((t + HALO) * CONV_PITCH, LANES), F32), pltpu.VMEM((t, D_MODEL), F32),
                        pltpu.VMEM((t, D_MODEL), BF16)],
        compiler_params=_params("arbitrary"),
        name="conv_out_prompt",
    )(u, u, x, wdw, bdw, lng, lnb, wout, bout)


def _conv_out_s_kernel(st_ref, u_ref, x_ref, wdw_ref, bdw_ref, lng_ref, lnb_ref, wout_ref, bout_ref, y_ref, ns_ref,
                       wbuf, cbuf):
    bb, nt, _ = u_ref.shape
    win = CONV_STATE + nt
    nj = D_MODEL // LANES
    seq_pitch = win * CONV_PITCH

    def put(b, r0, rows, j, val):
        wbuf[pl.ds((b * win + r0) * CONV_PITCH + j, rows, stride=CONV_PITCH), :] = val

    for b in range(bb):
        for j in range(nj):
            lanes = slice(j * LANES, (j + 1) * LANES)
            for r0 in range(0, CONV_STATE, 8):
                rows = min(8, CONV_STATE - r0)
                put(b, r0, rows, j, st_ref[b, r0:r0 + rows, lanes])
            put(b, CONV_STATE, nt, j, u_ref[b, :, lanes])

    for b in range(bb):
        for r0 in range(0, CONV_STATE, 8):
            rows = min(8, CONV_STATE - r0)
            for j in range(nj):
                src = (b * win + nt + r0) * CONV_PITCH + j
                ns_ref[b, r0:r0 + rows, j * LANES:(j + 1) * LANES] = wbuf[pl.ds(src, rows, stride=CONV_PITCH), :]

    for t in range(nt):
        for b0 in range(0, bb, 8):
            for j in range(nj):
                lanes = slice(j * LANES, (j + 1) * LANES)
                acc = None
                for k in range(CONV_WIDTH):
                    start = (b0 * win + t + k) * CONV_PITCH + j
                    term = wdw_ref[k:k + 1, lanes] * wbuf[pl.ds(start, 8, stride=seq_pitch), :]
                    acc = term if acc is None else acc + term
                cbuf[t * bb + b0:t * bb + b0 + 8, lanes] = acc
    c = cbuf[...] + bdw_ref[...]
    h = _ln_silu(c, lng_ref[...], lnb_ref[...]).astype(BF16)
    y = _mm(h, wout_ref[...]) + bout_ref[...]
    for t in range(nt):
        y_ref[t] = x_ref[t] + y[t * bb:(t + 1) * bb]


def _conv_out_s(state, u, x, wdw, bdw, lng, lnb, wout, bout):
    b, nt, _ = u.shape
    bb = 16 if b % 16 == 0 else b
    assert bb % 8 == 0
    win = CONV_STATE + nt
    seq = lambda i: (i, 0, 0)
    tm = lambda i: (0, i, 0)
    return pl.pallas_call(
        _conv_out_s_kernel,
        grid=(b // bb,),
        in_specs=[pl.BlockSpec((bb, CONV_STATE, D_MODEL), seq), pl.BlockSpec((bb, nt, D_MODEL), seq),
                  pl.BlockSpec((nt, bb, D_MODEL), tm),
                  _full(wdw.shape), _full(bdw.shape), _full(lng.shape), _full(lnb.shape),
                  _full(wout.shape), _full(bout.shape)],
        out_specs=[pl.BlockSpec((nt, bb, D_MODEL), tm), pl.BlockSpec((bb, CONV_STATE, D_MODEL), seq)],
        out_shape=[jax.ShapeDtypeStruct((nt, b, D_MODEL), F32), jax.ShapeDtypeStruct((b, CONV_STATE, D_MODEL), F32)],
        scratch_shapes=[pltpu.VMEM((bb * win * CONV_PITCH, LANES), F32), pltpu.VMEM((nt * bb, D_MODEL), F32)],
        compiler_params=_params("arbitrary"),
        name="conv_out_sample",
    )(state, u, x, wdw, bdw, lng, lnb, wout, bout)


def _route(lg):
    big = 3.0e38
    lane = lax.broadcasted_iota(jnp.int32, lg.shape, 1)
    lanef = lane.astype(F32)
    is_g = lane < N_EXPERT_GROUPS
    gl = jnp.where(is_g, lg, -big)
    gmax = jnp.max(gl, axis=1, keepdims=True)
    gsum = jnp.sum(jnp.where(is_g, jnp.exp(gl - gmax), 0.0), axis=1, keepdims=True)
    g_w = 1.0 / gsum
    g_idx = jnp.min(jnp.where(gl == gmax, lanef, big), axis=1, keepdims=True)
    rel = lanef - float(EXPERT_LANE0) - g_idx * float(EXPERTS_PER_GROUP)
    in_grp = jnp.where(rel >= 0.0, jnp.where(rel < float(EXPERTS_PER_GROUP), 1.0, 0.0), 0.0) > 0.5
    el = jnp.where(in_grp, lg, -big)
    e1 = jnp.max(el, axis=1, keepdims=True)
    i1 = jnp.min(jnp.where(el == e1, lanef, big), axis=1, keepdims=True)
    el2 = jnp.where(lanef == i1, -big, el)
    e2 = jnp.max(el2, axis=1, keepdims=True)
    i2 = jnp.min(jnp.where(el2 == e2, lanef, big), axis=1, keepdims=True)
    tt = jnp.exp(e2 - e1)
    w1 = g_w / (1.0 + tt)
    w2 = g_w * tt / (1.0 + tt)
    base = float(EXPERT_LANE0) + g_idx * float(EXPERTS_PER_GROUP)
    a = jnp.minimum(i1, i2) - base
    b = jnp.maximum(i1, i2) - base
    pair = a * (7.0 - a) * 0.5 + (b - a - 1.0)
    first_is_lo = i1 < i2
    return (g_idx * float(PAIRS_PER_GROUP) + pair, jnp.where(first_is_lo, w1, w2), jnp.where(first_is_lo, w2, w1))


def _to_row_linear(dst_ref, src_ref, n_tiles):
    def body(g, carry):
        r = pl.multiple_of(g * 8, 8)
        for j in range(n_tiles):
            dst_ref[pl.ds(r * ROW_PITCH + j, 8, stride=ROW_PITCH), :] = src_ref[pl.ds(r, 8), j * LANES:(j + 1) * LANES]
        return carry

    lax.fori_loop(0, src_ref.shape[0] // 8, body, 0)


def _from_row_linear(dst_ref, src_ref, n_tiles, pitch):
    def body(g, carry):
        r = pl.multiple_of(g * 8, 8)
        for j in range(n_tiles):
            dst_ref[pl.ds(r, 8), j * LANES:(j + 1) * LANES] = src_ref[pl.ds(r * pitch + j, 8, stride=pitch), :]
        return carry

    lax.fori_loop(0, dst_ref.shape[0] // 8, body, 0)


def _pick(i, n_prompt_tiles, prompt_ref, sample_ref):
    return jnp.where(i < n_prompt_tiles, prompt_ref[...], sample_ref[...])


def _route_kernel(n_prompt_tiles, yp_ref, ys_ref, g_ref, wr_ref, br_ref, tri_ref, pay_ref, meta_ref, cnt_ref, carry):
    i = pl.program_id(0)

    @pl.when(i == 0)
    def _():
        carry[...] = jnp.zeros_like(carry)

    xf = _rms(_pick(i, n_prompt_tiles, yp_ref, ys_ref), g_ref[...])
    x_hi, x_lo = _split_bf16(xf)
    part = _mm(x_hi, wr_ref[...]) + _mm(x_lo, wr_ref[...])
    logits = part[:, :ROUTER_LANES] + part[:, ROUTER_LANES:] + br_ref[...]
    bucket, w_lo, w_hi = _route(logits)
    lane = lax.broadcasted_iota(jnp.int32, logits.shape, 1)
    onehot = jnp.where(lane.astype(F32) == bucket, 1.0, 0.0)
    before = _mm(tri_ref[...], onehot.astype(BF16)) + carry[...]
    rank = jnp.sum(onehot * before, axis=1, keepdims=True)
    carry[...] += jnp.sum(onehot, axis=0, keepdims=True)
    cnt_ref[...] = carry[...]
    meta = jnp.where(lane == 0, bucket, jnp.where(lane == 1, rank, 0.0))
    meta_ref[...] = jnp.transpose(meta)[:8, :].astype(jnp.int32)

    gates = jnp.where(lane == 0, w_lo, jnp.where(lane == 1, w_hi, 0.0))
    nj = D_MODEL // LANES
    for r in range(0, xf.shape[0], 8):
        for j in range(nj):
            pay_ref[pl.ds(r * ROW_PITCH + j, 8, stride=ROW_PITCH), :] = xf[r:r + 8, j * LANES:(j + 1) * LANES]
        pay_ref[pl.ds(r * ROW_PITCH + nj, 8, stride=ROW_PITCH), :] = gates[r:r + 8, :]


def _route_call(y_p, y_s, g, wr, br, tri):
    t = y_s.shape[0]
    npt = y_p.shape[0] // t
    n = y_p.shape[0] + t
    pidx = lambda i: (jnp.minimum(i, npt - 1), 0)
    return pl.pallas_call(
        functools.partial(_route_kernel, npt),
        grid=(npt + 1,),
        in_specs=[pl.BlockSpec((t, D_MODEL), pidx), _full(y_s.shape), _full(g.shape), _full(wr.shape),
                  _full(br.shape), _full(tri.shape)],
        out_specs=[pl.BlockSpec((t * ROW_PITCH, LANES), lambda i: (i, 0)),
                   pl.BlockSpec((8, t), lambda i: (0, i)), _full((1, ROUTER_LANES))],
        out_shape=[jax.ShapeDtypeStruct((n * ROW_PITCH, LANES), F32),
                   jax.ShapeDtypeStruct((8, n), jnp.int32), jax.ShapeDtypeStruct((1, ROUTER_LANES), F32)],
        scratch_shapes=[pltpu.VMEM((1, ROUTER_LANES), F32)],
        compiler_params=_params("arbitrary"),
        name="route",
    )(y_p, y_s, g, wr, br, tri)


def _permute_kernel(pos_ref, zstart_ref, nz_ref, used_ref, src_ref, dst_ref, zbuf, ring, zsem, in_sems, out_sems):
    zbuf[...] = jnp.zeros_like(zbuf)

    def zero_tile(start):
        return pltpu.make_async_copy(zbuf, dst_ref.at[pl.ds(pl.multiple_of(start * ROW_PITCH, 8), TM * ROW_PITCH)], zsem)

    for b in range(N_BUCKETS):
        @pl.when(nz_ref[b] > 0)
        def _():
            zero_tile(zstart_ref[b]).start()
    n_tiles = dst_ref.shape[0] // (TM * ROW_PITCH)

    def start_unused(i, carry):
        zero_tile(i * TM).start()
        return carry

    lax.fori_loop(used_ref[0], n_tiles, start_unused, 0)
    for b in range(N_BUCKETS):
        @pl.when(nz_ref[b] > 0)
        def _():
            zero_tile(0).wait()

    def wait_unused(i, carry):
        zero_tile(0).wait()
        return carry

    lax.fori_loop(used_ref[0], n_tiles, wait_unused, 0)

    chunk_rows = DMA_CHUNK * ROW_PITCH
    n_chunks = src_ref.shape[0] // chunk_rows

    def fetch(c, slot):
        return pltpu.make_async_copy(src_ref.at[pl.ds(pl.multiple_of(c * chunk_rows, 8), chunk_rows)], ring.at[slot],
                                     in_sems.at[slot])

    def drain(slot):
        pltpu.make_async_copy(ring.at[slot], dst_ref.at[pl.ds(0, chunk_rows)], out_sems.at[slot]).wait()

    fetch(0, 0).start()

    def step(c, carry):
        slot = lax.rem(c, RING)
        nxt = lax.rem(c + 1, RING)
        fetch(c, slot).wait()

        @pl.when(c >= RING - 1)
        def _():
            drain(nxt)

        @pl.when(c + 1 < n_chunks)
        def _():
            fetch(c + 1, nxt).start()

        def send(jj, carry2):
            row = pos_ref[c * DMA_CHUNK + jj] * ROW_PITCH
            pltpu.make_async_copy(ring.at[slot, pl.ds(jj * ROW_PITCH, ROW_PITCH)], dst_ref.at[pl.ds(row, ROW_PITCH)],
                                  out_sems.at[slot]).start()
            return carry2

        lax.fori_loop(0, DMA_CHUNK, send, 0, unroll=8)
        return carry

    lax.fori_loop(0, n_chunks, step, 0)
    for c in range(n_chunks - (RING - 1), n_chunks):
        drain(c % RING)


def _permute(pos, zstart, nz, used, payload, n_rows_out):
    assert payload.shape[0] // (DMA_CHUNK * ROW_PITCH) >= RING
    return pl.pallas_call(
        _permute_kernel,
        grid_spec=pltpu.PrefetchScalarGridSpec(
            num_scalar_prefetch=4, grid=(1,),
            in_specs=[pl.BlockSpec(memory_space=pl.ANY)],
            out_specs=pl.BlockSpec(memory_space=pl.ANY),
            scratch_shapes=[pltpu.VMEM((TM * ROW_PITCH, LANES), payload.dtype),
                            pltpu.VMEM((RING, DMA_CHUNK * ROW_PITCH, LANES), payload.dtype),
                            pltpu.SemaphoreType.DMA(()), pltpu.SemaphoreType.DMA((RING,)),
                            pltpu.SemaphoreType.DMA((RING,))]),
        out_shape=jax.ShapeDtypeStruct((n_rows_out * ROW_PITCH, LANES), payload.dtype),
        compiler_params=_params("arbitrary"),
        name="permute",
    )(pos, zstart, nz, used, payload)


def _silu(x):
    return x * _sigmoid(x)


def _experts_kernel(tix_ref, lo_ref, hi_ref, valid_ref, fresh_ref, xs_ref, wgl_ref, wgh_ref, wul_ref, wuh_ref,
                    wdl_ref, wdh_ref, ys_ref, wup_s, wdn_s, xbuf, ybuf):
    i = pl.program_id(0)

    @pl.when(fresh_ref[i] > 0)
    def _():
        for s, ref in enumerate((wgl_ref, wgh_ref, wul_ref, wuh_ref)):
            wup_s[s] = ref[0, 0].astype(BF16)
        for s, ref in enumerate((wdl_ref, wdh_ref)):
            wdn_s[s] = ref[0, 0].astype(BF16)

    @pl.when(valid_ref[i] == 0)
    def _():
        ys_ref[...] = jnp.zeros_like(ys_ref)

    @pl.when(valid_ref[i] > 0)
    def _():
        _from_row_linear(xbuf, xs_ref, ROW_PITCH, ROW_PITCH)
        x = xbuf[:, :D_MODEL].astype(BF16)
        gates = xbuf[:, D_MODEL:]
        h_lo = _silu(_mm(x, wup_s[0])) * _mm(x, wup_s[2]) * gates[:, 0:1]
        h_hi = _silu(_mm(x, wup_s[1])) * _mm(x, wup_s[3]) * gates[:, 1:2]
        ybuf[:, :D_MODEL] = _mm(h_lo.astype(BF16), wdn_s[0]) + _mm(h_hi.astype(BF16), wdn_s[1])
        ybuf[:, D_MODEL:] = jnp.zeros((TM, LANES), F32)
        _to_row_linear(ys_ref, ybuf, ROW_PITCH)


def _experts(layer, tix, lo, hi, valid, fresh, xs, wg, wu, wd):
    n_tiles = xs.shape[0] // (TM * ROW_PITCH)
    row = lambda i, tix, lo, hi, valid, fresh: (tix[i], 0)
    row_out = lambda i, tix, lo, hi, valid, fresh: (i, 0)
    e_lo = lambda i, tix, lo, hi, valid, fresh: (layer, lo[i], 0, 0)
    e_hi = lambda i, tix, lo, hi, valid, fresh: (layer, hi[i], 0, 0)
    up = pl.BlockSpec((1, 1, D_MODEL, D_EXPERT), e_lo), pl.BlockSpec((1, 1, D_MODEL, D_EXPERT), e_hi)
    down = pl.BlockSpec((1, 1, D_EXPERT, D_MODEL), e_lo), pl.BlockSpec((1, 1, D_EXPERT, D_MODEL), e_hi)
    return pl.pallas_call(
        _experts_kernel,
        grid_spec=pltpu.PrefetchScalarGridSpec(
            num_scalar_prefetch=5, grid=(n_tiles,),
            in_specs=[pl.BlockSpec((TM * ROW_PITCH, LANES), row), *up, *up, *down],
            out_specs=pl.BlockSpec((TM * ROW_PITCH, LANES), row_out),
            scratch_shapes=[pltpu.VMEM((4, D_MODEL, D_EXPERT), BF16), pltpu.VMEM((2, D_EXPERT, D_MODEL), BF16),
                            pltpu.VMEM((TM, PAY_WIDTH), F32), pltpu.VMEM((TM, PAY_WIDTH), F32)]),
        out_shape=jax.ShapeDtypeStruct(xs.shape, F32),
        compiler_params=_params("arbitrary"),
        name="experts",
    )(tix, lo, hi, valid, fresh, xs, wg, wg, wu, wu, wd, wd)


def _ple_kernel(n_prompt_tiles, pos_ref, yp_ref, ys_ref, moe_ref, pp_ref, ps_ref, g_ref, wg_ref, wp_ref, op_ref, os_ref,
                mring, mbuf, sems):
    i = pl.program_id(0)
    t = mbuf.shape[0]
    w = D_MODEL // LANES

    def gather(tile, slot):
        def body(jj, carry):
            row = pos_ref[tile * t + jj] * ROW_PITCH
            pltpu.make_async_copy(moe_ref.at[pl.ds(row, w)], mring.at[slot, pl.ds(jj * w, w)], sems.at[slot]).start()
            return carry

        lax.fori_loop(0, t, body, 0, unroll=8)

    def wait_tile(slot):
        pltpu.make_async_copy(moe_ref.at[pl.ds(0, t * w)], mring.at[slot], sems.at[slot]).wait()

    @pl.when(i == 0)
    def _():
        gather(0, 0)
        gather(1, 1)

    slot = lax.rem(i, PLE_RING)
    nslot = lax.rem(i + 2, PLE_RING)
    wait_tile(slot)
    def add_rows(g, carry):
        r = pl.multiple_of(g * 8, 8)
        for j in range(w):
            lanes = slice(j * LANES, (j + 1) * LANES)
            base = jnp.where(i < n_prompt_tiles, yp_ref[pl.ds(r, 8), lanes], ys_ref[pl.ds(r, 8), lanes])
            mbuf[pl.ds(r, 8), lanes] = base + mring[slot, pl.ds(r * w + j, 8, stride=w), :]
        return carry

    lax.fori_loop(0, t // 8, add_rows, 0)
    y2 = mbuf[...]
    hn = _rms(y2, g_ref[...]).astype(BF16)
    gt = _sigmoid(_mm(hn, wg_ref[...]))
    pr = _mm(jnp.where(i < n_prompt_tiles, pp_ref[0], ps_ref[...]).astype(BF16), wp_ref[...])
    y3 = y2 + gt * pr
    nxt = jnp.minimum(i + 2, n_prompt_tiles) * t
    for jj in range(t):
        row = pos_ref[nxt + jj] * ROW_PITCH
        pltpu.make_async_copy(moe_ref.at[pl.ds(row, w)], mring.at[nslot, pl.ds(jj * w, w)], sems.at[nslot]).start()

    @pl.when(i < n_prompt_tiles)
    def _():
        op_ref[...] = y3

    @pl.when(i >= n_prompt_tiles)
    def _():
        os_ref[...] = y3
        wait_tile(lax.rem(i + 1, PLE_RING))
        wait_tile(nslot)


def _ple(layer, pos, y_p, y_s, ys_sorted, p_p, p_s, g, wg, wp):
    t = y_s.shape[0]
    npt = y_p.shape[0] // t
    w = D_MODEL // LANES
    pidx = lambda i, pos: (jnp.minimum(i, npt - 1), 0)
    full = lambda shape: pl.BlockSpec(shape, lambda i, pos: (0,) * len(shape))
    return pl.pallas_call(
        functools.partial(_ple_kernel, npt),
        grid_spec=pltpu.PrefetchScalarGridSpec(
            num_scalar_prefetch=1, grid=(npt + 1,),
            in_specs=[pl.BlockSpec((t, D_MODEL), pidx), full(y_s.shape), pl.BlockSpec(memory_space=pl.ANY),
                      pl.BlockSpec((1, t, PLE_DIM), lambda i, pos: (layer, jnp.minimum(i, npt - 1), 0)),
                      full(p_s.shape), full(g.shape), full(wg.shape), full(wp.shape)],
            out_specs=[pl.BlockSpec((t, D_MODEL), pidx), full(y_s.shape)],
            scratch_shapes=[pltpu.VMEM((PLE_RING, t * w, LANES), F32), pltpu.VMEM((t, D_MODEL), F32),
                            pltpu.SemaphoreType.DMA((PLE_RING,))]),
        out_shape=[jax.ShapeDtypeStruct(y_p.shape, F32), jax.ShapeDtypeStruct(y_s.shape, F32)],
        compiler_params=_params("arbitrary"),
        name="ple",
    )(pos, y_p, y_s, ys_sorted, p_p, p_s, g, wg, wp)


def _bucket_experts():
    pairs = [(a, b) for a in range(EXPERTS_PER_GROUP) for b in range(a + 1, EXPERTS_PER_GROUP)]
    lo = [g * EXPERTS_PER_GROUP + a for g in range(N_EXPERT_GROUPS) for a, _ in pairs]
    hi = [g * EXPERTS_PER_GROUP + b for g in range(N_EXPERT_GROUPS) for _, b in pairs]
    return jnp.array(lo, jnp.int32), jnp.array(hi, jnp.int32)


def _moe_ple(layer, y_p, y_s, gffn, wr, br, wg, wu, wd, p_p, p_s, gple, pwg, pwp):
    t = y_s.shape[0]
    n = y_p.shape[0] + t
    tri = jnp.asarray(np.tri(t, t, -1, dtype=np.float32), BF16)
    payload, meta, counts = _route_call(y_p, y_s, gffn, wr, br, tri)

    bucket, rank = meta[0], meta[1]
    cnt = counts[0, :N_BUCKETS].astype(jnp.int32)
    padded = (cnt + TM - 1) // TM * TM
    ends = jnp.cumsum(padded)
    starts = ends - padded
    in_bucket = bucket[:, None] == jnp.arange(N_BUCKETS, dtype=jnp.int32)[None, :]
    pos = rank + jnp.sum(jnp.where(in_bucket, starts[None, :], 0), axis=1)
    n_rows = n + N_BUCKETS * TM
    n_tiles = n_rows // TM
    tile_start = jnp.arange(n_tiles, dtype=jnp.int32) * TM
    valid = tile_start < ends[-1]
    used = (ends[-1] // TM).astype(jnp.int32)
    tix = jnp.arange(n_tiles, dtype=jnp.int32)
    tile_bucket = jnp.sum((ends[None, :] <= (jnp.minimum(tix, used - 1) * TM)[:, None]).astype(jnp.int32), axis=1)
    prev_bucket = jnp.concatenate([jnp.full((1,), -1, jnp.int32), tile_bucket[:-1]])
    fresh = valid & (tile_bucket != prev_bucket)
    first_tile = jnp.where(fresh, tix, n_tiles)
    next_first = lax.cummin(jnp.concatenate([first_tile[1:], jnp.full((1,), n_tiles, jnp.int32)]), reverse=True)
    next_bucket = jnp.where(next_first < n_tiles, jnp.take(tile_bucket, jnp.minimum(next_first, n_tiles - 1)),
                            tile_bucket)
    resident = jnp.where(fresh, tile_bucket, next_bucket)
    lo_tab, hi_tab = _bucket_experts()
    lo = jnp.take(lo_tab, resident)
    hi = jnp.take(hi_tab, resident)

    xs = _permute(pos, (ends - TM).astype(jnp.int32), (cnt > 0).astype(jnp.int32), used.reshape(1), payload, n_rows)
    ys = _experts(layer, jnp.minimum(tix, used - 1), lo, hi, valid.astype(jnp.int32), fresh.astype(jnp.int32), xs, wg,
                  wu, wd)
    return _ple(layer, pos, y_p, y_s, ys, p_p, p_s, gple, pwg, pwp)


def _split_bf16(a):
    hi = a.astype(BF16)
    lo = (a - hi.astype(F32)).astype(BF16)
    return hi, lo


def _head_norm(a, ind_ref, indt_ref, gain):
    hi, lo = _split_bf16(a * a)
    ss = _mm(hi, ind_ref[...]) + _mm(lo, ind_ref[...])
    inv = lax.rsqrt(ss * (1.0 / HEAD_DIM) + EPS)
    ihi, ilo = _split_bf16(inv)
    invb = _mm(ihi, indt_ref[...]) + _mm(ilo, indt_ref[...])
    return a * invb * gain


def _qkv_kernel(x_ref, g_ref, w_ref, iq_ref, iqt_ref, ik_ref, ikt_ref, qg_ref, kg_ref, q_ref, k_ref, v_ref):
    h = _rms(x_ref[...], g_ref[...]).astype(BF16)
    qkv = _mm(h, w_ref[...])
    nq = N_HEADS * HEAD_DIM
    q = _head_norm(qkv[:, :nq], iq_ref, iqt_ref, qg_ref[...])
    k = _head_norm(qkv[:, nq:nq + KV_DIM], ik_ref, ikt_ref, kg_ref[...])
    q_ref[...] = (q * (HEAD_DIM ** -0.5)).astype(BF16)
    k_ref[...] = k
    v_ref[...] = qkv[:, nq + KV_DIM:]


def _qkv(x, g, w, iq, iqt, ik, ikt, qg, kg):
    n = x.shape[0]
    t = _row_tile(n)
    row = lambda i: (i, 0)
    return pl.pallas_call(
        _qkv_kernel,
        grid=(n // t,),
        in_specs=[pl.BlockSpec((t, D_MODEL), row)] + [_full(a.shape) for a in (g, w, iq, iqt, ik, ikt, qg, kg)],
        out_specs=[pl.BlockSpec((t, N_HEADS * HEAD_DIM), row), pl.BlockSpec((t, KV_DIM), row),
                   pl.BlockSpec((t, KV_DIM), row)],
        out_shape=[jax.ShapeDtypeStruct((n, N_HEADS * HEAD_DIM), BF16), jax.ShapeDtypeStruct((n, KV_DIM), F32),
                   jax.ShapeDtypeStruct((n, KV_DIM), F32)],
        compiler_params=_params("arbitrary"),
        name="qkv",
    )(x, g, w, iq, iqt, ik, ikt, qg, kg)


def _dup_heads(a):
    out = []
    for s in range(KV_DIM // LANES):
        sl = a[:, s * LANES:(s + 1) * LANES]
        sw = pltpu.roll(sl, HEAD_DIM, axis=1)
        low = lax.broadcasted_iota(jnp.int32, sl.shape, 1) < HEAD_DIM
        out.append(jnp.where(low, sl, sw))
        out.append(jnp.where(low, sw, sl))
    return jnp.concatenate(out, axis=1).astype(BF16)


def _attend(q_rows, k2, v2, bias_of, sink_of, extra_mask):
    m_rows = q_rows.shape[0]
    low_q = lax.broadcasted_iota(jnp.int32, (m_rows, LANES), 1) < HEAD_DIM
    low_k = lax.broadcasted_iota(jnp.int32, (k2.shape[0], LANES), 1) < HEAD_DIM
    zero_q = jnp.zeros((m_rows, LANES), BF16)
    zero_k = jnp.zeros((k2.shape[0], LANES), BF16)
    slabs = []
    for g in range(N_KV_HEADS):
        kg = k2[:, g * LANES:(g + 1) * LANES]
        vg = v2[:, g * LANES:(g + 1) * LANES]
        v_lo = jnp.where(low_k, vg, zero_k)
        v_hi = jnp.where(low_k, zero_k, vg)
        lhs = []
        for a in range(GROUP):
            h = g * GROUP + a
            qs = q_rows[:, (h // 2) * LANES:(h // 2 + 1) * LANES]
            lhs.append(jnp.where(low_q, qs, zero_q) if h % 2 == 0 else jnp.where(low_q, zero_q, qs))
        s = lax.dot_general(jnp.concatenate(lhs, axis=0), kg, (((1,), (1,)), ((), ())), preferred_element_type=F32)
        probs, rinv = [], []
        for a in range(GROUP):
            h = g * GROUP + a
            sa = s[a * m_rows:(a + 1) * m_rows] + bias_of(h)
            if extra_mask is not None:
                sa = jnp.where(extra_mask, NEG_INF, sa)
            sink = sink_of(h)
            m = jnp.maximum(jnp.max(sa, axis=1, keepdims=True), sink)
            p = jnp.exp(sa - m)
            den = jnp.sum(p, axis=1, keepdims=True) + jnp.exp(sink - m)
            probs.append(p.astype(BF16))
            rinv.append(1.0 / den)
        for sp in range(GROUP // 2):
            o = _mm(probs[2 * sp], v_lo) + _mm(probs[2 * sp + 1], v_hi)
            slabs.append(o * jnp.where(low_q, rinv[2 * sp], rinv[2 * sp + 1]))
    return jnp.concatenate(slabs, axis=1)


def _attn_p_kernel(sink_ref, q_ref, k_ref, v_ref, x_ref, bias_ref, wo_ref, y_ref, kbuf, vbuf, obuf):
    t = q_ref.shape[0]
    i = pl.program_id(0)

    @pl.when(i == 0)
    def _():
        kbuf[0:WINDOW, :] = jnp.zeros((WINDOW, 2 * KV_DIM), BF16)
        vbuf[0:WINDOW, :] = jnp.zeros((WINDOW, 2 * KV_DIM), BF16)

    @pl.when(i > 0)
    def _():
        kbuf[0:WINDOW, :] = kbuf[t:t + WINDOW, :]
        vbuf[0:WINDOW, :] = vbuf[t:t + WINDOW, :]

    kbuf[WINDOW:, :] = _dup_heads(k_ref[...])
    vbuf[WINDOW:, :] = _dup_heads(v_ref[...])
    col = lax.broadcasted_iota(jnp.int32, (WINDOW, 2 * WINDOW), 1)
    for j in range(t // WINDOW):
        rows = slice(j * WINDOW, (j + 1) * WINDOW)
        keys = slice(j * WINDOW, (j + 2) * WINDOW)
        extra = jnp.logical_and(i == 0, col < WINDOW) if j == 0 else None
        o = _attend(q_ref[rows, :], kbuf[keys, :], vbuf[keys, :], lambda h: bias_ref[h], lambda h: sink_ref[h], extra)
        obuf[rows, :] = o.astype(BF16)
    y_ref[...] = x_ref[...] + _mm(obuf[...], wo_ref[...])


def _attn_p(sinks, q, k, v, x, bias, wo):
    n = q.shape[0]
    t = _row_tile(n)
    row = lambda i: (i, 0)
    return pl.pallas_call(
        _attn_p_kernel,
        grid=(n // t,),
        in_specs=[pl.BlockSpec(memory_space=pltpu.SMEM),
                  pl.BlockSpec((t, N_HEADS * HEAD_DIM), row), pl.BlockSpec((t, KV_DIM), row),
                  pl.BlockSpec((t, KV_DIM), row), pl.BlockSpec((t, D_MODEL), row),
                  _full(bias.shape), _full(wo.shape)],
        out_specs=pl.BlockSpec((t, D_MODEL), row),
        out_shape=jax.ShapeDtypeStruct((n, D_MODEL), F32),
        scratch_shapes=[pltpu.VMEM((t + WINDOW, 2 * KV_DIM), BF16), pltpu.VMEM((t + WINDOW, 2 * KV_DIM), BF16),
                        pltpu.VMEM((t, N_HEADS * HEAD_DIM), BF16)],
        compiler_params=_params("arbitrary"),
        name="attn_prompt",
    )(sinks, q, k, v, x, bias, wo)


def _attn_s_kernel(sink_ref, q_ref, k_ref, v_ref, ck_ref, cv_ref, bias_ref, o_ref, nk_ref, nv_ref, shift_buf):
    pad = bias_ref.shape[2] - ck_ref.shape[0] - k_ref.shape[0]
    zpad = jnp.zeros((pad, KV_DIM), F32)
    k2 = _dup_heads(jnp.concatenate([ck_ref[...], k_ref[...], zpad], axis=0))
    v2 = _dup_heads(jnp.concatenate([cv_ref[...], v_ref[...], zpad], axis=0))
    o = _attend(q_ref[...], k2, v2, lambda h: bias_ref[h], lambda h: sink_ref[h], None)
    o_ref[...] = o.astype(BF16)
    t_new = k_ref.shape[0] // SAMPLE_SEQS
    for new_ref, old_ref, add_ref in ((nk_ref, ck_ref, k_ref), (nv_ref, cv_ref, v_ref)):
        for b in range(SAMPLE_SEQS):
            r = b * WINDOW
            shift_buf[0:WINDOW - t_new, :] = old_ref[r + t_new:r + WINDOW, :]
            shift_buf[WINDOW - t_new:WINDOW, :] = add_ref[b * t_new:(b + 1) * t_new, :]
            new_ref[b] = jnp.transpose(shift_buf[...])


def _attn_s(sinks, q, k, v, ck, cv, bias, t_new):
    n = q.shape[0]
    rows = SAMPLE_SEQS * t_new
    crow = SAMPLE_SEQS * WINDOW
    row = lambda i: (i, 0)
    return pl.pallas_call(
        _attn_s_kernel,
        grid=(n // rows,),
        in_specs=[pl.BlockSpec(memory_space=pltpu.SMEM),
                  pl.BlockSpec((rows, N_HEADS * HEAD_DIM), row), pl.BlockSpec((rows, KV_DIM), row),
                  pl.BlockSpec((rows, KV_DIM), row), pl.BlockSpec((crow, KV_DIM), row),
                  pl.BlockSpec((crow, KV_DIM), row), _full(bias.shape)],
        out_specs=[pl.BlockSpec((rows, N_HEADS * HEAD_DIM), row),
                   pl.BlockSpec((SAMPLE_SEQS, KV_DIM, WINDOW), lambda i: (i, 0, 0)),
                   pl.BlockSpec((SAMPLE_SEQS, KV_DIM, WINDOW), lambda i: (i, 0, 0))],
        out_shape=[jax.ShapeDtypeStruct((n, N_HEADS * HEAD_DIM), BF16),
                   jax.ShapeDtypeStruct((ck.shape[0] // WINDOW, KV_DIM, WINDOW), F32),
                   jax.ShapeDtypeStruct((cv.shape[0] // WINDOW, KV_DIM, WINDOW), F32)],
        scratch_shapes=[pltpu.VMEM((WINDOW, KV_DIM), F32)],
        compiler_params=_params("arbitrary"),
        name="attn_sample",
    )(sinks, q, k, v, ck, cv, bias)


def _proj_res_kernel(o_ref, x_ref, w_ref, y_ref):
    y_ref[...] = x_ref[...] + _mm(o_ref[...], w_ref[...])


def _proj_res(o, x, w):
    n = o.shape[0]
    t = _row_tile(n)
    row = lambda i: (i, 0)
    return pl.pallas_call(
        _proj_res_kernel,
        grid=(n // t,),
        in_specs=[pl.BlockSpec((t, o.shape[1]), row), pl.BlockSpec((t, D_MODEL), row), _full(w.shape)],
        out_specs=pl.BlockSpec((t, D_MODEL), row),
        out_shape=jax.ShapeDtypeStruct((n, D_MODEL), F32),
        compiler_params=_params("arbitrary"),
        name="proj_res",
    )(o, x, w)


def _alibi_slopes():
    return np.exp2(-8.0 * np.arange(1, N_HEADS + 1, dtype=np.float64) / N_HEADS).astype(np.float32)


def _band_bias(dist, allowed):
    b = -(_alibi_slopes()[:, None, None] * dist.astype(np.float32)[None])
    return jnp.asarray(np.where(allowed[None], b, np.float32(NEG_INF)).astype(np.float32))


def _prompt_bias():
    dist = WINDOW + np.arange(WINDOW)[:, None] - np.arange(2 * WINDOW)[None, :]
    return _band_bias(dist, (dist >= 0) & (dist <= WINDOW))


def _sample_bias(t_new, n_cols):
    c = np.arange(n_cols)
    n_cache = SAMPLE_SEQS * WINDOW
    n_new = SAMPLE_SEQS * t_new
    is_cache = c < n_cache
    is_new = (c >= n_cache) & (c < n_cache + n_new)
    seq_c = np.where(is_cache, c // WINDOW, (c - n_cache) // t_new)
    pos_c = np.where(is_cache, c % WINDOW, WINDOW + (c - n_cache) % t_new)
    r = np.arange(n_new)
    seq_r, tok_r = r // t_new, r % t_new
    dist = WINDOW + tok_r[:, None] - pos_c[None, :]
    allowed = (seq_r[:, None] == seq_c[None, :]) & (is_cache | is_new)[None, :] & (dist >= 0) & (dist <= WINDOW)
    return _band_bias(dist, allowed)


def _head_indicator(n_heads):
    ch = np.arange(n_heads * HEAD_DIM) // HEAD_DIM
    ind = (ch[:, None] == np.arange(LANES)[None, :]).astype(np.float32)
    return jnp.asarray(ind, BF16), jnp.asarray(ind.T, BF16)


def kernel(x_prompt, x_sample, state_conv, cache_k, cache_v, p_prompt, p_sample, norm_mix, norm_ffn, norm_ple,
           conv_w_in, conv_b_in, conv_w_dw, conv_b_dw, conv_ln_g, conv_ln_b, conv_w_out, conv_b_out, attn_w_qkv,
           attn_q_norm, attn_k_norm, attn_sinks, attn_w_o, moe_w_rg, moe_b_rg, moe_w_re, moe_b_re, moe_w_gate,
           moe_w_up, moe_w_down, ple_w_gate, ple_w_proj):
    bp, seq, d = x_prompt.shape
    bs, t_new, _ = x_sample.shape
    assert bp == 1 and d == D_MODEL and seq % WINDOW == 0 and bs % SAMPLE_SEQS == 0
    assert seq % (bs * t_new) == 0 and (bs * t_new) % DMA_CHUNK == 0
    depth = norm_mix.shape[0]
    row2 = lambda a: a.reshape(1, -1)

    y_p = x_prompt.reshape(seq, d)
    y_s = x_sample.reshape(bs * t_new, d)
    conv_p, conv_s, k_p, v_p, k_s, v_s = [], [], [], [], [], []

    for i in range(depth):
        j = i // 2
        g_mix = row2(norm_mix[i])
        if i % 2 == 0:
            w_in = conv_w_in[j].astype(BF16)
            b_in = row2(conv_b_in[j])
            tail = (conv_w_dw[j], row2(conv_b_dw[j]), row2(conv_ln_g[j]), row2(conv_ln_b[j]),
                    conv_w_out[j].astype(BF16), row2(conv_b_out[j]))
            u_p = _conv_in(y_p, g_mix, w_in, b_in)
            u_s = _conv_in(y_s, g_mix, w_in, b_in)
            conv_p.append(u_p[seq - CONV_STATE:].reshape(1, CONV_STATE, d))
            y_p = _conv_out_p(u_p, y_p, *tail)
            ys_t, new_state = _conv_out_s(state_conv[j], u_s.reshape(bs, t_new, d),
                                          y_s.reshape(bs, t_new, d).transpose(1, 0, 2), *tail)
            conv_s.append(new_state)
            y_s = ys_t.transpose(1, 0, 2).reshape(bs * t_new, d)
        else:
            w_qkv = attn_w_qkv[j].astype(BF16)
            w_o = attn_w_o[j].astype(BF16)
            iq, iqt = _head_indicator(N_HEADS)
            ik, ikt = _head_indicator(N_KV_HEADS)
            qg = row2(jnp.tile(attn_q_norm[j], N_HEADS))
            kg = row2(jnp.tile(attn_k_norm[j], N_KV_HEADS))
            sinks = attn_sinks[j]
            q1, k1, v1 = _qkv(y_p, g_mix, w_qkv, iq, iqt, ik, ikt, qg, kg)
            q2, k2, v2 = _qkv(y_s, g_mix, w_qkv, iq, iqt, ik, ikt, qg, kg)
            k_p.append(k1[seq - WINDOW:].reshape(1, WINDOW, N_KV_HEADS, HEAD_DIM))
            v_p.append(v1[seq - WINDOW:].reshape(1, WINDOW, N_KV_HEADS, HEAD_DIM))
            y_p = _attn_p(sinks, q1, k1, v1, y_p, _prompt_bias(), w_o)
            n_cols = -(-(SAMPLE_SEQS * (WINDOW + t_new)) // LANES) * LANES
            o_s, nk, nv = _attn_s(sinks, q2, k2, v2, cache_k[j].reshape(bs * WINDOW, KV_DIM),
                                  cache_v[j].reshape(bs * WINDOW, KV_DIM), _sample_bias(t_new, n_cols), t_new)
            k_s.append(nk.reshape(bs, N_KV_HEADS, HEAD_DIM, WINDOW).transpose(0, 3, 1, 2))
            v_s.append(nv.reshape(bs, N_KV_HEADS, HEAD_DIM, WINDOW).transpose(0, 3, 1, 2))
            y_s = _proj_res(o_s, y_s, w_o)

        w_r = jnp.zeros((d, ROUTER_LANES), F32)
        w_r = w_r.at[:, :N_EXPERT_GROUPS].set(moe_w_rg[i]).at[:, EXPERT_LANE0:EXPERT_LANE0 + N_EXPERTS].set(moe_w_re[i])
        b_r = jnp.zeros((1, ROUTER_LANES), F32)
        b_r = b_r.at[0, :N_EXPERT_GROUPS].set(moe_b_rg[i]).at[0, EXPERT_LANE0:EXPERT_LANE0 + N_EXPERTS].set(moe_b_re[i])
        moe = (row2(norm_ffn[i]), jnp.concatenate(_split_bf16(w_r), axis=1), b_r, moe_w_gate, moe_w_up, moe_w_down)
        ple = (row2(norm_ple[i]), ple_w_gate[i].astype(BF16), ple_w_proj[i].astype(BF16))
        y_p, y_s = _moe_ple(i, y_p, y_s, *moe, p_prompt.reshape(depth, seq, PLE_DIM),
                            p_sample[i].reshape(bs * t_new, PLE_DIM), *ple)

    return (y_p.reshape(1, seq, d), y_s.reshape(bs, t_new, d), jnp.stack(conv_p), jnp.stack(conv_s),
            jnp.stack(k_p), jnp.stack(v_p), jnp.stack(k_s), jnp.stack(v_s))
```

```python
import functools

import numpy as np

import jax
import jax.numpy as jnp
from jax import lax
from jax.experimental import pallas as pl
from jax.experimental.pallas import tpu as pltpu

F32 = jnp.float32
BF16 = jnp.bfloat16

D_MODEL = 1024
PLE_DIM = 256
CONV_WIDTH = 31
CONV_STATE = CONV_WIDTH - 1
N_HEADS = 16
N_KV_HEADS = 4
HEAD_DIM = 64
GROUP = N_HEADS // N_KV_HEADS
WINDOW = 128
KV_DIM = N_KV_HEADS * HEAD_DIM
N_EXPERT_GROUPS = 4
EXPERTS_PER_GROUP = 4
N_EXPERTS = 16
D_EXPERT = 256
EPS = 1e-6
NEG_INF = -1e30

LANES = 128
ROUTER_LANES = LANES
EXPERT_LANE0 = N_EXPERT_GROUPS
HALO = 32
CONV_ROWS = 32
NORM_ROWS = 256
CONV_PITCH = D_MODEL // LANES + 1
SAMPLE_SEQS = 8
PAIRS_PER_GROUP = EXPERTS_PER_GROUP * (EXPERTS_PER_GROUP - 1) // 2
N_BUCKETS = N_EXPERT_GROUPS * PAIRS_PER_GROUP
TM = 256
PAY_WIDTH = D_MODEL + LANES
ROW_PITCH = PAY_WIDTH // LANES
DMA_CHUNK = 512
RING = 3
PLE_RING = 3
VMEM_LIMIT = 48 * 1024 * 1024


def _row_tile(n):
    return 512 if n % 512 == 0 else n


def _params(*sem):
    return pltpu.CompilerParams(dimension_semantics=sem, vmem_limit_bytes=VMEM_LIMIT)


def _full(shape):
    nd = len(shape)
    return pl.BlockSpec(shape, lambda *_: (0,) * nd)


def _rms(x, g):
    ms = jnp.mean(x * x, axis=-1, keepdims=True)
    return x * lax.rsqrt(ms + EPS) * g


def _sigmoid(x):
    return 1.0 / (1.0 + jnp.exp(-x))


def _mm(a, b):
    return jnp.dot(a, b, preferred_element_type=F32)


def _conv_in_kernel(x_ref, g_ref, w_ref, b_ref, u_ref):
    h = _rms(x_ref[...], g_ref[...]).astype(BF16)
    z = _mm(h, w_ref[...]) + b_ref[...]
    u_ref[...] = z[:, :D_MODEL] * _sigmoid(z[:, D_MODEL:])


def _conv_in(x, g, w, b):
    n = x.shape[0]
    t = _row_tile(n)
    return pl.pallas_call(
        _conv_in_kernel,
        grid=(n // t,),
        in_specs=[pl.BlockSpec((t, D_MODEL), lambda i: (i, 0)), _full(g.shape), _full(w.shape), _full(b.shape)],
        out_specs=pl.BlockSpec((t, D_MODEL), lambda i: (i, 0)),
        out_shape=jax.ShapeDtypeStruct((n, D_MODEL), F32),
        compiler_params=_params("arbitrary"),
        name="conv_in",
    )(x, g, w, b)


def _ln_silu(c, g, b):
    mu = jnp.mean(c, axis=-1, keepdims=True)
    xc = c - mu
    var = jnp.mean(xc * xc, axis=-1, keepdims=True)
    cn = xc * lax.rsqrt(var + EPS) * g + b
    return cn * _sigmoid(cn)


def _conv_out_p_kernel(u_ref, halo_ref, x_ref, wdw_ref, bdw_ref, lng_ref, lnb_ref, wout_ref, bout_ref, y_ref,
                       ubuf, cbuf, hbuf):
    t = u_ref.shape[0]
    i = pl.program_id(0)
    nj = D_MODEL // LANES

    def put(r, j, val):
        ubuf[pl.ds(r * CONV_PITCH + j, 8, stride=CONV_PITCH), :] = val

    for r in range(0, HALO, 8):
        for j in range(nj):
            put(r, j, jnp.where(i > 0, halo_ref[r:r + 8, j * LANES:(j + 1) * LANES], 0.0))

    def fill(rr, carry):
        r = pl.multiple_of(rr * 8, 8)
        for j in range(nj):
            put(r + HALO, j, u_ref[pl.ds(r, 8), j * LANES:(j + 1) * LANES])
        return carry

    lax.fori_loop(0, t // 8, fill, 0)

    def conv_chunk(rr, carry):
        r0 = pl.multiple_of(rr * CONV_ROWS, CONV_ROWS)
        for j in range(nj):
            lanes = slice(j * LANES, (j + 1) * LANES)
            accs = [None] * (CONV_ROWS // 8)
            for k in range(CONV_WIDTH):
                wk = wdw_ref[k:k + 1, lanes]
                for q in range(CONV_ROWS // 8):
                    r = r0 + (HALO - CONV_STATE + k + 8 * q)
                    term = wk * ubuf[pl.ds(r * CONV_PITCH + j, 8, stride=CONV_PITCH), :]
                    accs[q] = term if k == 0 else accs[q] + term
            for q in range(CONV_ROWS // 8):
                cbuf[pl.ds(r0 + 8 * q, 8), lanes] = accs[q]
        return carry

    lax.fori_loop(0, t // CONV_ROWS, conv_chunk, 0)

    def norm_chunk(rr, carry):
        r0 = pl.multiple_of(rr * NORM_ROWS, NORM_ROWS)
        c = cbuf[pl.ds(r0, NORM_ROWS), :] + bdw_ref[...]
        hbuf[pl.ds(r0, NORM_ROWS), :] = _ln_silu(c, lng_ref[...], lnb_ref[...]).astype(BF16)
        return carry

    lax.fori_loop(0, t // NORM_ROWS, norm_chunk, 0)
    y_ref[...] = x_ref[...] + _mm(hbuf[...], wout_ref[...]) + bout_ref[...]


def _conv_out_p(u, x, wdw, bdw, lng, lnb, wout, bout):
    n = u.shape[0]
    t = _row_tile(n)
    hb = t // HALO
    row = lambda i: (i, 0)
    return pl.pallas_call(
        _conv_out_p_kernel,
        grid=(n // t,),
        in_specs=[pl.BlockSpec((t, D_MODEL), row),
                  pl.BlockSpec((HALO, D_MODEL), lambda i: (jnp.maximum(i * hb - 1, 0), 0)),
                  pl.BlockSpec((t, D_MODEL), row),
                  _full(wdw.shape), _full(bdw.shape), _full(lng.shape), _full(lnb.shape),
                  _full(wout.shape), _full(bout.shape)],
        out_specs=pl.BlockSpec((t, D_MODEL), row),
        out_shape=jax.ShapeDtypeStruct((n, D_MODEL), F32),
        scratch_shapes=[pltpu.VMEM(((t + HALO) * CONV_PITCH, LANES), F32), pltpu.VMEM((t, D_MODEL), F32),
                        pltpu.VMEM((t, D_MODEL), BF16)],
        compiler_params=_params("arbitrary"),
        name="conv_out_prompt",
    )(u, u, x, wdw, bdw, lng, lnb, wout, bout)


def _conv_out_s_kernel(st_ref, u_ref, x_ref, wdw_ref, bdw_ref, lng_ref, lnb_ref, wout_ref, bout_ref, y_ref, ns_ref,
                       wbuf, cbuf):
    bb, nt, _ = u_ref.shape
    win = CONV_STATE + nt
    nj = D_MODEL // LANES
    seq_pitch = win * CONV_PITCH

    def put(b, r0, rows, j, val):
        wbuf[pl.ds((b * win + r0) * CONV_PITCH + j, rows, stride=CONV_PITCH), :] = val

    for b in range(bb):
        for j in range(nj):
            lanes = slice(j * LANES, (j + 1) * LANES)
            for r0 in range(0, CONV_STATE, 8):
                rows = min(8, CONV_STATE - r0)
                put(b, r0, rows, j, st_ref[b, r0:r0 + rows, lanes])
            put(b, CONV_STATE, nt, j, u_ref[b, :, lanes])

    for b in range(bb):
        for r0 in range(0, CONV_STATE, 8):
            rows = min(8, CONV_STATE - r0)
            for j in range(nj):
                src = (b * win + nt + r0) * CONV_PITCH + j
                ns_ref[b, r0:r0 + rows, j * LANES:(j + 1) * LANES] = wbuf[pl.ds(src, rows, stride=CONV_PITCH), :]

    for t in range(nt):
        for b0 in range(0, bb, 8):
            for j in range(nj):
                lanes = slice(j * LANES, (j + 1) * LANES)
                acc = None
                for k in range(CONV_WIDTH):
                    start = (b0 * win + t + k) * CONV_PITCH + j
                    term = wdw_ref[k:k + 1, lanes] * wbuf[pl.ds(start, 8, stride=seq_pitch), :]
                    acc = term if acc is None else acc + term
                cbuf[t * bb + b0:t * bb + b0 + 8, lanes] = acc
    c = cbuf[...] + bdw_ref[...]
    h = _ln_silu(c, lng_ref[...], lnb_ref[...]).astype(BF16)
    y = _mm(h, wout_ref[...]) + bout_ref[...]
    for t in range(nt):
        y_ref[t] = x_ref[t] + y[t * bb:(t + 1) * bb]


def _conv_out_s(state, u, x, wdw, bdw, lng, lnb, wout, bout):
    b, nt, _ = u.shape
    bb = 16 if b % 16 == 0 else b
    assert bb % 8 == 0
    win = CONV_STATE + nt
    seq = lambda i: (i, 0, 0)
    tm = lambda i: (0, i, 0)
    return pl.pallas_call(
        _conv_out_s_kernel,
        grid=(b // bb,),
        in_specs=[pl.BlockSpec((bb, CONV_STATE, D_MODEL), seq), pl.BlockSpec((bb, nt, D_MODEL), seq),
                  pl.BlockSpec((nt, bb, D_MODEL), tm),
                  _full(wdw.shape), _full(bdw.shape), _full(lng.shape), _full(lnb.shape),
                  _full(wout.shape), _full(bout.shape)],
        out_specs=[pl.BlockSpec((nt, bb, D_MODEL), tm), pl.BlockSpec((bb, CONV_STATE, D_MODEL), seq)],
        out_shape=[jax.ShapeDtypeStruct((nt, b, D_MODEL), F32), jax.ShapeDtypeStruct((b, CONV_STATE, D_MODEL), F32)],
        scratch_shapes=[pltpu.VMEM((bb * win * CONV_PITCH, LANES), F32), pltpu.VMEM((nt * bb, D_MODEL), F32)],
        compiler_params=_params("arbitrary"),
        name="conv_out_sample",
    )(state, u, x, wdw, bdw, lng, lnb, wout, bout)


def _route(lg):
    big = 3.0e38
    lane = lax.broadcasted_iota(jnp.int32, lg.shape, 1)
    lanef = lane.astype(F32)
    is_g = lane < N_EXPERT_GROUPS
    gl = jnp.where(is_g, lg, -big)
    gmax = jnp.max(gl, axis=1, keepdims=True)
    gsum = jnp.sum(jnp.where(is_g, jnp.exp(gl - gmax), 0.0), axis=1, keepdims=True)
    g_w = 1.0 / gsum
    g_idx = jnp.min(jnp.where(gl == gmax, lanef, big), axis=1, keepdims=True)
    rel = lanef - float(EXPERT_LANE0) - g_idx * float(EXPERTS_PER_GROUP)
    in_grp = jnp.where(rel >= 0.0, jnp.where(rel < float(EXPERTS_PER_GROUP), 1.0, 0.0), 0.0) > 0.5
    el = jnp.where(in_grp, lg, -big)
    e1 = jnp.max(el, axis=1, keepdims=True)
    i1 = jnp.min(jnp.where(el == e1, lanef, big), axis=1, keepdims=True)
    el2 = jnp.where(lanef == i1, -big, el)
    e2 = jnp.max(el2, axis=1, keepdims=True)
    i2 = jnp.min(jnp.where(el2 == e2, lanef, big), axis=1, keepdims=True)
    tt = jnp.exp(e2 - e1)
    w1 = g_w / (1.0 + tt)
    w2 = g_w * tt / (1.0 + tt)
    base = float(EXPERT_LANE0) + g_idx * float(EXPERTS_PER_GROUP)
    a = jnp.minimum(i1, i2) - base
    b = jnp.maximum(i1, i2) - base
    pair = a * (7.0 - a) * 0.5 + (b - a - 1.0)
    first_is_lo = i1 < i2
    return (g_idx * float(PAIRS_PER_GROUP) + pair, jnp.where(first_is_lo, w1, w2), jnp.where(first_is_lo, w2, w1))


def _to_row_linear(dst_ref, src_ref, n_tiles):
    def body(g, carry):
        r = pl.multiple_of(g * 8, 8)
        for j in range(n_tiles):
            dst_ref[pl.ds(r * ROW_PITCH + j, 8, stride=ROW_PITCH), :] = src_ref[pl.ds(r, 8), j * LANES:(j + 1) * LANES]
        return carry

    lax.fori_loop(0, src_ref.shape[0] // 8, body, 0)


def _from_row_linear(dst_ref, src_ref, n_tiles, pitch):
    def body(g, carry):
        r = pl.multiple_of(g * 8, 8)
        for j in range(n_tiles):
            dst_ref[pl.ds(r, 8), j * LANES:(j + 1) * LANES] = src_ref[pl.ds(r * pitch + j, 8, stride=pitch), :]
        return carry

    lax.fori_loop(0, dst_ref.shape[0] // 8, body, 0)


def _pick(i, n_prompt_tiles, prompt_ref, sample_ref):
    return jnp.where(i < n_prompt_tiles, prompt_ref[...], sample_ref[...])


def _route_kernel(n_prompt_tiles, yp_ref, ys_ref, g_ref, wr_ref, br_ref, tri_ref, pay_ref, meta_ref, cnt_ref, carry):
    i = pl.program_id(0)

    @pl.when(i == 0)
    def _():
        carry[...] = jnp.zeros_like(carry)

    xf = _rms(_pick(i, n_prompt_tiles, yp_ref, ys_ref), g_ref[...])
    x_hi, x_lo = _split_bf16(xf)
    part = _mm(x_hi, wr_ref[...]) + _mm(x_lo, wr_ref[...])
    logits = part[:, :ROUTER_LANES] + part[:, ROUTER_LANES:] + br_ref[...]
    bucket, w_lo, w_hi = _route(logits)
    lane = lax.broadcasted_iota(jnp.int32, logits.shape, 1)
    onehot = jnp.where(lane.astype(F32) == bucket, 1.0, 0.0)
    before = _mm(tri_ref[...], onehot.astype(BF16)) + carry[...]
    rank = jnp.sum(onehot * before, axis=1, keepdims=True)
    carry[...] += jnp.sum(onehot, axis=0, keepdims=True)
    cnt_ref[...] = carry[...]
    meta = jnp.where(lane == 0, bucket, jnp.where(lane == 1, rank, 0.0))
    meta_ref[...] = jnp.transpose(meta)[:8, :].astype(jnp.int32)

    gates = jnp.where(lane == 0, w_lo, jnp.where(lane == 1, w_hi, 0.0))
    nj = D_MODEL // LANES
    for r in range(0, xf.shape[0], 8):
        for j in range(nj):
            pay_ref[pl.ds(r * ROW_PITCH + j, 8, stride=ROW_PITCH), :] = xf[r:r + 8, j * LANES:(j + 1) * LANES]
        pay_ref[pl.ds(r * ROW_PITCH + nj, 8, stride=ROW_PITCH), :] = gates[r:r + 8, :]


def _route_call(y_p, y_s, g, wr, br, tri):
    t = y_s.shape[0]
    npt = y_p.shape[0] // t
    n = y_p.shape[0] + t
    pidx = lambda i: (jnp.minimum(i, npt - 1), 0)
    return pl.pallas_call(
        functools.partial(_route_kernel, npt),
        grid=(npt + 1,),
        in_specs=[pl.BlockSpec((t, D_MODEL), pidx), _full(y_s.shape), _full(g.shape), _full(wr.shape),
                  _full(br.shape), _full(tri.shape)],
        out_specs=[pl.BlockSpec((t * ROW_PITCH, LANES), lambda i: (i, 0)),
                   pl.BlockSpec((8, t), lambda i: (0, i)), _full((1, ROUTER_LANES))],
        out_shape=[jax.ShapeDtypeStruct((n * ROW_PITCH, LANES), F32),
                   jax.ShapeDtypeStruct((8, n), jnp.int32), jax.ShapeDtypeStruct((1, ROUTER_LANES), F32)],
        scratch_shapes=[pltpu.VMEM((1, ROUTER_LANES), F32)],
        compiler_params=_params("arbitrary"),
        name="route",
    )(y_p, y_s, g, wr, br, tri)


def _permute_kernel(pos_ref, zstart_ref, nz_ref, used_ref, src_ref, dst_ref, zbuf, ring, zsem, in_sems, out_sems):
    zbuf[...] = jnp.zeros_like(zbuf)

    def zero_tile(start):
        return pltpu.make_async_copy(zbuf, dst_ref.at[pl.ds(pl.multiple_of(start * ROW_PITCH, 8), TM * ROW_PITCH)], zsem)

    for b in range(N_BUCKETS):
        @pl.when(nz_ref[b] > 0)
        def _():
            zero_tile(zstart_ref[b]).start()
    n_tiles = dst_ref.shape[0] // (TM * ROW_PITCH)

    def start_unused(i, carry):
        zero_tile(i * TM).start()
        return carry

    lax.fori_loop(used_ref[0], n_tiles, start_unused, 0)
    for b in range(N_BUCKETS):
        @pl.when(nz_ref[b] > 0)
        def _():
            zero_tile(0).wait()

    def wait_unused(i, carry):
        zero_tile(0).wait()
        return carry

    lax.fori_loop(used_ref[0], n_tiles, wait_unused, 0)

    chunk_rows = DMA_CHUNK * ROW_PITCH
    n_chunks = src_ref.shape[0] // chunk_rows

    def fetch(c, slot):
        return pltpu.make_async_copy(src_ref.at[pl.ds(pl.multiple_of(c * chunk_rows, 8), chunk_rows)], ring.at[slot],
                                     in_sems.at[slot])

    def drain(slot):
        pltpu.make_async_copy(ring.at[slot], dst_ref.at[pl.ds(0, chunk_rows)], out_sems.at[slot]).wait()

    fetch(0, 0).start()

    def step(c, carry):
        slot = lax.rem(c, RING)
        nxt = lax.rem(c + 1, RING)
        fetch(c, slot).wait()

        @pl.when(c >= RING - 1)
        def _():
            drain(nxt)

        @pl.when(c + 1 < n_chunks)
        def _():
            fetch(c + 1, nxt).start()

        def send(jj, carry2):
            row = pos_ref[c * DMA_CHUNK + jj] * ROW_PITCH
            pltpu.make_async_copy(ring.at[slot, pl.ds(jj * ROW_PITCH, ROW_PITCH)], dst_ref.at[pl.ds(row, ROW_PITCH)],
                                  out_sems.at[slot]).start()
            return carry2

        lax.fori_loop(0, DMA_CHUNK, send, 0, unroll=8)
        return carry

    lax.fori_loop(0, n_chunks, step, 0)
    for c in range(n_chunks - (RING - 1), n_chunks):
        drain(c % RING)


def _permute(pos, zstart, nz, used, payload, n_rows_out):
    assert payload.shape[0] // (DMA_CHUNK * ROW_PITCH) >= RING
    return pl.pallas_call(
        _permute_kernel,
        grid_spec=pltpu.PrefetchScalarGridSpec(
            num_scalar_prefetch=4, grid=(1,),
            in_specs=[pl.BlockSpec(memory_space=pl.ANY)],
            out_specs=pl.BlockSpec(memory_space=pl.ANY),
            scratch_shapes=[pltpu.VMEM((TM * ROW_PITCH, LANES), payload.dtype),
                            pltpu.VMEM((RING, DMA_CHUNK * ROW_PITCH, LANES), payload.dtype),
                            pltpu.SemaphoreType.DMA(()), pltpu.SemaphoreType.DMA((RING,)),
                            pltpu.SemaphoreType.DMA((RING,))]),
        out_shape=jax.ShapeDtypeStruct((n_rows_out * ROW_PITCH, LANES), payload.dtype),
        compiler_params=_params("arbitrary"),
        name="permute",
    )(pos, zstart, nz, used, payload)


def _silu(x):
    return x * _sigmoid(x)


def _experts_kernel(tix_ref, lo_ref, hi_ref, valid_ref, fresh_ref, xs_ref, wgl_ref, wgh_ref, wul_ref, wuh_ref,
                    wdl_ref, wdh_ref, ys_ref, wup_s, wdn_s, xbuf, ybuf):
    i = pl.program_id(0)

    @pl.when(fresh_ref[i] > 0)
    def _():
        for s, ref in enumerate((wgl_ref, wgh_ref, wul_ref, wuh_ref)):
            wup_s[s] = ref[0, 0].astype(BF16)
        for s, ref in enumerate((wdl_ref, wdh_ref)):
            wdn_s[s] = ref[0, 0].astype(BF16)

    @pl.when(valid_ref[i] == 0)
    def _():
        ys_ref[...] = jnp.zeros_like(ys_ref)

    @pl.when(valid_ref[i] > 0)
    def _():
        _from_row_linear(xbuf, xs_ref, ROW_PITCH, ROW_PITCH)
        x = xbuf[:, :D_MODEL].astype(BF16)
        gates = xbuf[:, D_MODEL:]
        h_lo = _silu(_mm(x, wup_s[0])) * _mm(x, wup_s[2]) * gates[:, 0:1]
        h_hi = _silu(_mm(x, wup_s[1])) * _mm(x, wup_s[3]) * gates[:, 1:2]
        ybuf[:, :D_MODEL] = _mm(h_lo.astype(BF16), wdn_s[0]) + _mm(h_hi.astype(BF16), wdn_s[1])
        ybuf[:, D_MODEL:] = jnp.zeros((TM, LANES), F32)
        _to_row_linear(ys_ref, ybuf, ROW_PITCH)


def _experts(layer, tix, lo, hi, valid, fresh, xs, wg, wu, wd):
    n_tiles = xs.shape[0] // (TM * ROW_PITCH)
    row = lambda i, tix, lo, hi, valid, fresh: (tix[i], 0)
    row_out = lambda i, tix, lo, hi, valid, fresh: (i, 0)
    e_lo = lambda i, tix, lo, hi, valid, fresh: (layer, lo[i], 0, 0)
    e_hi = lambda i, tix, lo, hi, valid, fresh: (layer, hi[i], 0, 0)
    up = pl.BlockSpec((1, 1, D_MODEL, D_EXPERT), e_lo), pl.BlockSpec((1, 1, D_MODEL, D_EXPERT), e_hi)
    down = pl.BlockSpec((1, 1, D_EXPERT, D_MODEL), e_lo), pl.BlockSpec((1, 1, D_EXPERT, D_MODEL), e_hi)
    return pl.pallas_call(
        _experts_kernel,
        grid_spec=pltpu.PrefetchScalarGridSpec(
            num_scalar_prefetch=5, grid=(n_tiles,),
            in_specs=[pl.BlockSpec((TM * ROW_PITCH, LANES), row), *up, *up, *down],
            out_specs=pl.BlockSpec((TM * ROW_PITCH, LANES), row_out),
            scratch_shapes=[pltpu.VMEM((4, D_MODEL, D_EXPERT), BF16), pltpu.VMEM((2, D_EXPERT, D_MODEL), BF16),
                            pltpu.VMEM((TM, PAY_WIDTH), F32), pltpu.VMEM((TM, PAY_WIDTH), F32)]),
        out_shape=jax.ShapeDtypeStruct(xs.shape, F32),
        compiler_params=_params("arbitrary"),
        name="experts",
    )(tix, lo, hi, valid, fresh, xs, wg, wg, wu, wu, wd, wd)


def _ple_kernel(n_prompt_tiles, pos_ref, yp_ref, ys_ref, moe_ref, pp_ref, ps_ref, g_ref, wg_ref, wp_ref, op_ref, os_ref,
                mring, mbuf, sems):
    i = pl.program_id(0)
    t = mbuf.shape[0]
    w = D_MODEL // LANES

    def gather(tile, slot):
        def body(jj, carry):
            row = pos_ref[tile * t + jj] * ROW_PITCH
            pltpu.make_async_copy(moe_ref.at[pl.ds(row, w)], mring.at[slot, pl.ds(jj * w, w)], sems.at[slot]).start()
            return carry

        lax.fori_loop(0, t, body, 0, unroll=8)

    def wait_tile(slot):
        pltpu.make_async_copy(moe_ref.at[pl.ds(0, t * w)], mring.at[slot], sems.at[slot]).wait()

    @pl.when(i == 0)
    def _():
        gather(0, 0)
        gather(1, 1)

    slot = lax.rem(i, PLE_RING)
    nslot = lax.rem(i + 2, PLE_RING)
    wait_tile(slot)
    _from_row_linear(mbuf, mring.at[slot], w, w)
    y2 = _pick(i, n_prompt_tiles, yp_ref, ys_ref) + mbuf[...]
    hn = _rms(y2, g_ref[...]).astype(BF16)
    gt = _sigmoid(_mm(hn, wg_ref[...]))
    pr = _mm(jnp.where(i < n_prompt_tiles, pp_ref[0], ps_ref[...]).astype(BF16), wp_ref[...])
    y3 = y2 + gt * pr
    nxt = jnp.minimum(i + 2, n_prompt_tiles) * t
    for jj in range(t):
        row = pos_ref[nxt + jj] * ROW_PITCH
        pltpu.make_async_copy(moe_ref.at[pl.ds(row, w)], mring.at[nslot, pl.ds(jj * w, w)], sems.at[nslot]).start()

    @pl.when(i < n_prompt_tiles)
    def _():
        op_ref[...] = y3

    @pl.when(i >= n_prompt_tiles)
    def _():
        os_ref[...] = y3
        wait_tile(lax.rem(i + 1, PLE_RING))
        wait_tile(nslot)


def _ple(layer, pos, y_p, y_s, ys_sorted, p_p, p_s, g, wg, wp):
    t = y_s.shape[0]
    npt = y_p.shape[0] // t
    w = D_MODEL // LANES
    pidx = lambda i, pos: (jnp.minimum(i, npt - 1), 0)
    full = lambda shape: pl.BlockSpec(shape, lambda i, pos: (0,) * len(shape))
    return pl.pallas_call(
        functools.partial(_ple_kernel, npt),
        grid_spec=pltpu.PrefetchScalarGridSpec(
            num_scalar_prefetch=1, grid=(npt + 1,),
            in_specs=[pl.BlockSpec((t, D_MODEL), pidx), full(y_s.shape), pl.BlockSpec(memory_space=pl.ANY),
                      pl.BlockSpec((1, t, PLE_DIM), lambda i, pos: (layer, jnp.minimum(i, npt - 1), 0)),
                      full(p_s.shape), full(g.shape), full(wg.shape), full(wp.shape)],
            out_specs=[pl.BlockSpec((t, D_MODEL), pidx), full(y_s.shape)],
            scratch_shapes=[pltpu.VMEM((PLE_RING, t * w, LANES), F32), pltpu.VMEM((t, D_MODEL), F32),
                            pltpu.SemaphoreType.DMA((PLE_RING,))]),
        out_shape=[jax.ShapeDtypeStruct(y_p.shape, F32), jax.ShapeDtypeStruct(y_s.shape, F32)],
        compiler_params=_params("arbitrary"),
        name="ple",
    )(pos, y_p, y_s, ys_sorted, p_p, p_s, g, wg, wp)


def _bucket_experts():
    pairs = [(a, b) for a in range(EXPERTS_PER_GROUP) for b in range(a + 1, EXPERTS_PER_GROUP)]
    lo = [g * EXPERTS_PER_GROUP + a for g in range(N_EXPERT_GROUPS) for a, _ in pairs]
    hi = [g * EXPERTS_PER_GROUP + b for g in range(N_EXPERT_GROUPS) for _, b in pairs]
    return jnp.array(lo, jnp.int32), jnp.array(hi, jnp.int32)


def _moe_ple(layer, y_p, y_s, gffn, wr, br, wg, wu, wd, p_p, p_s, gple, pwg, pwp):
    t = y_s.shape[0]
    n = y_p.shape[0] + t
    tri = jnp.asarray(np.tri(t, t, -1, dtype=np.float32), BF16)
    payload, meta, counts = _route_call(y_p, y_s, gffn, wr, br, tri)

    bucket, rank = meta[0], meta[1]
    cnt = counts[0, :N_BUCKETS].astype(jnp.int32)
    padded = (cnt + TM - 1) // TM * TM
    ends = jnp.cumsum(padded)
    starts = ends - padded
    in_bucket = bucket[:, None] == jnp.arange(N_BUCKETS, dtype=jnp.int32)[None, :]
    pos = rank + jnp.sum(jnp.where(in_bucket, starts[None, :], 0), axis=1)
    n_rows = n + N_BUCKETS * TM
    n_tiles = n_rows // TM
    tile_start = jnp.arange(n_tiles, dtype=jnp.int32) * TM
    valid = tile_start < ends[-1]
    used = (ends[-1] // TM).astype(jnp.int32)
    tix = jnp.arange(n_tiles, dtype=jnp.int32)
    tile_bucket = jnp.sum((ends[None, :] <= (jnp.minimum(tix, used - 1) * TM)[:, None]).astype(jnp.int32), axis=1)
    prev_bucket = jnp.concatenate([jnp.full((1,), -1, jnp.int32), tile_bucket[:-1]])
    fresh = valid & (tile_bucket != prev_bucket)
    first_tile = jnp.where(fresh, tix, n_tiles)
    next_first = lax.cummin(jnp.concatenate([first_tile[1:], jnp.full((1,), n_tiles, jnp.int32)]), reverse=True)
    next_bucket = jnp.where(next_first < n_tiles, jnp.take(tile_bucket, jnp.minimum(next_first, n_tiles - 1)),
                            tile_bucket)
    resident = jnp.where(fresh, tile_bucket, next_bucket)
    lo_tab, hi_tab = _bucket_experts()
    lo = jnp.take(lo_tab, resident)
    hi = jnp.take(hi_tab, resident)

    xs = _permute(pos, (ends - TM).astype(jnp.int32), (cnt > 0).astype(jnp.int32), used.reshape(1), payload, n_rows)
    ys = _experts(layer, jnp.minimum(tix, used - 1), lo, hi, valid.astype(jnp.int32), fresh.astype(jnp.int32), xs, wg,
                  wu, wd)
    return _ple(layer, pos, y_p, y_s, ys, p_p, p_s, gple, pwg, pwp)


def _split_bf16(a):
    hi = a.astype(BF16)
    lo = (a - hi.astype(F32)).astype(BF16)
    return hi, lo


def _head_norm(a, ind_ref, indt_ref, gain):
    hi, lo = _split_bf16(a * a)
    ss = _mm(hi, ind_ref[...]) + _mm(lo, ind_ref[...])
    inv = lax.rsqrt(ss * (1.0 / HEAD_DIM) + EPS)
    ihi, ilo = _split_bf16(inv)
    invb = _mm(ihi, indt_ref[...]) + _mm(ilo, indt_ref[...])
    return a * invb * gain


def _qkv_kernel(x_ref, g_ref, w_ref, iq_ref, iqt_ref, ik_ref, ikt_ref, qg_ref, kg_ref, q_ref, k_ref, v_ref):
    h = _rms(x_ref[...], g_ref[...]).astype(BF16)
    qkv = _mm(h, w_ref[...])
    nq = N_HEADS * HEAD_DIM
    q = _head_norm(qkv[:, :nq], iq_ref, iqt_ref, qg_ref[...])
    k = _head_norm(qkv[:, nq:nq + KV_DIM], ik_ref, ikt_ref, kg_ref[...])
    q_ref[...] = (q * (HEAD_DIM ** -0.5)).astype(BF16)
    k_ref[...] = k
    v_ref[...] = qkv[:, nq + KV_DIM:]


def _qkv(x, g, w, iq, iqt, ik, ikt, qg, kg):
    n = x.shape[0]
    t = _row_tile(n)
    row = lambda i: (i, 0)
    return pl.pallas_call(
        _qkv_kernel,
        grid=(n // t,),
        in_specs=[pl.BlockSpec((t, D_MODEL), row)] + [_full(a.shape) for a in (g, w, iq, iqt, ik, ikt, qg, kg)],
        out_specs=[pl.BlockSpec((t, N_HEADS * HEAD_DIM), row), pl.BlockSpec((t, KV_DIM), row),
                   pl.BlockSpec((t, KV_DIM), row)],
        out_shape=[jax.ShapeDtypeStruct((n, N_HEADS * HEAD_DIM), BF16), jax.ShapeDtypeStruct((n, KV_DIM), F32),
                   jax.ShapeDtypeStruct((n, KV_DIM), F32)],
        compiler_params=_params("arbitrary"),
        name="qkv",
    )(x, g, w, iq, iqt, ik, ikt, qg, kg)


def _dup_heads(a):
    out = []
    for s in range(KV_DIM // LANES):
        sl = a[:, s * LANES:(s + 1) * LANES]
        sw = pltpu.roll(sl, HEAD_DIM, axis=1)
        low = lax.broadcasted_iota(jnp.int32, sl.shape, 1) < HEAD_DIM
        out.append(jnp.where(low, sl, sw))
        out.append(jnp.where(low, sw, sl))
    return jnp.concatenate(out, axis=1).astype(BF16)


def _attend(q_rows, k2, v2, bias_of, sink_of, extra_mask):
    m_rows = q_rows.shape[0]
    low_q = lax.broadcasted_iota(jnp.int32, (m_rows, LANES), 1) < HEAD_DIM
    low_k = lax.broadcasted_iota(jnp.int32, (k2.shape[0], LANES), 1) < HEAD_DIM
    zero_q = jnp.zeros((m_rows, LANES), BF16)
    zero_k = jnp.zeros((k2.shape[0], LANES), BF16)
    slabs = []
    for g in range(N_KV_HEADS):
        kg = k2[:, g * LANES:(g + 1) * LANES]
        vg = v2[:, g * LANES:(g + 1) * LANES]
        v_lo = jnp.where(low_k, vg, zero_k)
        v_hi = jnp.where(low_k, zero_k, vg)
        lhs = []
        for a in range(GROUP):
            h = g * GROUP + a
            qs = q_rows[:, (h // 2) * LANES:(h // 2 + 1) * LANES]
            lhs.append(jnp.where(low_q, qs, zero_q) if h % 2 == 0 else jnp.where(low_q, zero_q, qs))
        s = lax.dot_general(jnp.concatenate(lhs, axis=0), kg, (((1,), (1,)), ((), ())), preferred_element_type=F32)
        probs, rinv = [], []
        for a in range(GROUP):
            h = g * GROUP + a
            sa = s[a * m_rows:(a + 1) * m_rows] + bias_of(h)
            if extra_mask is not None:
                sa = jnp.where(extra_mask, NEG_INF, sa)
            sink = sink_of(h)
            m = jnp.maximum(jnp.max(sa, axis=1, keepdims=True), sink)
            p = jnp.exp(sa - m)
            den = jnp.sum(p, axis=1, keepdims=True) + jnp.exp(sink - m)
            probs.append(p.astype(BF16))
            rinv.append(1.0 / den)
        for sp in range(GROUP // 2):
            o = _mm(probs[2 * sp], v_lo) + _mm(probs[2 * sp + 1], v_hi)
            slabs.append(o * jnp.where(low_q, rinv[2 * sp], rinv[2 * sp + 1]))
    return jnp.concatenate(slabs, axis=1)


def _attn_p_kernel(sink_ref, q_ref, k_ref, v_ref, x_ref, bias_ref, wo_ref, y_ref, kbuf, vbuf, obuf):
    t = q_ref.shape[0]
    i = pl.program_id(0)

    @pl.when(i == 0)
    def _():
        kbuf[0:WINDOW, :] = jnp.zeros((WINDOW, 2 * KV_DIM), BF16)
        vbuf[0:WINDOW, :] = jnp.zeros((WINDOW, 2 * KV_DIM), BF16)

    @pl.when(i > 0)
    def _():
        kbuf[0:WINDOW, :] = kbuf[t:t + WINDOW, :]
        vbuf[0:WINDOW, :] = vbuf[t:t + WINDOW, :]

    kbuf[WINDOW:, :] = _dup_heads(k_ref[...])
    vbuf[WINDOW:, :] = _dup_heads(v_ref[...])
    col = lax.broadcasted_iota(jnp.int32, (WINDOW, 2 * WINDOW), 1)
    for j in range(t // WINDOW):
        rows = slice(j * WINDOW, (j + 1) * WINDOW)
        keys = slice(j * WINDOW, (j + 2) * WINDOW)
        extra = jnp.logical_and(i == 0, col < WINDOW) if j == 0 else None
        o = _attend(q_ref[rows, :], kbuf[keys, :], vbuf[keys, :], lambda h: bias_ref[h], lambda h: sink_ref[h], extra)
        obuf[rows, :] = o.astype(BF16)
    y_ref[...] = x_ref[...] + _mm(obuf[...], wo_ref[...])


def _attn_p(sinks, q, k, v, x, bias, wo):
    n = q.shape[0]
    t = _row_tile(n)
    row = lambda i: (i, 0)
    return pl.pallas_call(
        _attn_p_kernel,
        grid=(n // t,),
        in_specs=[pl.BlockSpec(memory_space=pltpu.SMEM),
                  pl.BlockSpec((t, N_HEADS * HEAD_DIM), row), pl.BlockSpec((t, KV_DIM), row),
                  pl.BlockSpec((t, KV_DIM), row), pl.BlockSpec((t, D_MODEL), row),
                  _full(bias.shape), _full(wo.shape)],
        out_specs=pl.BlockSpec((t, D_MODEL), row),
        out_shape=jax.ShapeDtypeStruct((n, D_MODEL), F32),
        scratch_shapes=[pltpu.VMEM((t + WINDOW, 2 * KV_DIM), BF16), pltpu.VMEM((t + WINDOW, 2 * KV_DIM), BF16),
                        pltpu.VMEM((t, N_HEADS * HEAD_DIM), BF16)],
        compiler_params=_params("arbitrary"),
        name="attn_prompt",
    )(sinks, q, k, v, x, bias, wo)


def _attn_s_kernel(sink_ref, q_ref, k_ref, v_ref, ck_ref, cv_ref, bias_ref, o_ref, nk_ref, nv_ref, shift_buf):
    pad = bias_ref.shape[2] - ck_ref.shape[0] - k_ref.shape[0]
    zpad = jnp.zeros((pad, KV_DIM), F32)
    k2 = _dup_heads(jnp.concatenate([ck_ref[...], k_ref[...], zpad], axis=0))
    v2 = _dup_heads(jnp.concatenate([cv_ref[...], v_ref[...], zpad], axis=0))
    o = _attend(q_ref[...], k2, v2, lambda h: bias_ref[h], lambda h: sink_ref[h], None)
    o_ref[...] = o.astype(BF16)
    t_new = k_ref.shape[0] // SAMPLE_SEQS
    for new_ref, old_ref, add_ref in ((nk_ref, ck_ref, k_ref), (nv_ref, cv_ref, v_ref)):
        for b in range(SAMPLE_SEQS):
            r = b * WINDOW
            shift_buf[0:WINDOW - t_new, :] = old_ref[r + t_new:r + WINDOW, :]
            shift_buf[WINDOW - t_new:WINDOW, :] = add_ref[b * t_new:(b + 1) * t_new, :]
            new_ref[b] = jnp.transpose(shift_buf[...])


def _attn_s(sinks, q, k, v, ck, cv, bias, t_new):
    n = q.shape[0]
    rows = SAMPLE_SEQS * t_new
    crow = SAMPLE_SEQS * WINDOW
    row = lambda i: (i, 0)
    return pl.pallas_call(
        _attn_s_kernel,
        grid=(n // rows,),
        in_specs=[pl.BlockSpec(memory_space=pltpu.SMEM),
                  pl.BlockSpec((rows, N_HEADS * HEAD_DIM), row), pl.BlockSpec((rows, KV_DIM), row),
                  pl.BlockSpec((rows, KV_DIM), row), pl.BlockSpec((crow, KV_DIM), row),
                  pl.BlockSpec((crow, KV_DIM), row), _full(bias.shape)],
        out_specs=[pl.BlockSpec((rows, N_HEADS * HEAD_DIM), row),
                   pl.BlockSpec((SAMPLE_SEQS, KV_DIM, WINDOW), lambda i: (i, 0, 0)),
                   pl.BlockSpec((SAMPLE_SEQS, KV_DIM, WINDOW), lambda i: (i, 0, 0))],
        out_shape=[jax.ShapeDtypeStruct((n, N_HEADS * HEAD_DIM), BF16),
                   jax.ShapeDtypeStruct((ck.shape[0] // WINDOW, KV_DIM, WINDOW), F32),
                   jax.ShapeDtypeStruct((cv.shape[0] // WINDOW, KV_DIM, WINDOW), F32)],
        scratch_shapes=[pltpu.VMEM((WINDOW, KV_DIM), F32)],
        compiler_params=_params("arbitrary"),
        name="attn_sample",
    )(sinks, q, k, v, ck, cv, bias)


def _proj_res_kernel(o_ref, x_ref, w_ref, y_ref):
    y_ref[...] = x_ref[...] + _mm(o_ref[...], w_ref[...])


def _proj_res(o, x, w):
    n = o.shape[0]
    t = _row_tile(n)
    row = lambda i: (i, 0)
    return pl.pallas_call(
        _proj_res_kernel,
        grid=(n // t,),
        in_specs=[pl.BlockSpec((t, o.shape[1]), row), pl.BlockSpec((t, D_MODEL), row), _full(w.shape)],
        out_specs=pl.BlockSpec((t, D_MODEL), row),
        out_shape=jax.ShapeDtypeStruct((n, D_MODEL), F32),
        compiler_params=_params("arbitrary"),
        name="proj_res",
    )(o, x, w)


def _alibi_slopes():
    return np.exp2(-8.0 * np.arange(1, N_HEADS + 1, dtype=np.float64) / N_HEADS).astype(np.float32)


def _band_bias(dist, allowed):
    b = -(_alibi_slopes()[:, None, None] * dist.astype(np.float32)[None])
    return jnp.asarray(np.where(allowed[None], b, np.float32(NEG_INF)).astype(np.float32))


def _prompt_bias():
    dist = WINDOW + np.arange(WINDOW)[:, None] - np.arange(2 * WINDOW)[None, :]
    return _band_bias(dist, (dist >= 0) & (dist <= WINDOW))


def _sample_bias(t_new, n_cols):
    c = np.arange(n_cols)
    n_cache = SAMPLE_SEQS * WINDOW
    n_new = SAMPLE_SEQS * t_new
    is_cache = c < n_cache
    is_new = (c >= n_cache) & (c < n_cache + n_new)
    seq_c = np.where(is_cache, c // WINDOW, (c - n_cache) // t_new)
    pos_c = np.where(is_cache, c % WINDOW, WINDOW + (c - n_cache) % t_new)
    r = np.arange(n_new)
    seq_r, tok_r = r // t_new, r % t_new
    dist = WINDOW + tok_r[:, None] - pos_c[None, :]
    allowed = (seq_r[:, None] == seq_c[None, :]) & (is_cache | is_new)[None, :] & (dist >= 0) & (dist <= WINDOW)
    return _band_bias(dist, allowed)


def _head_indicator(n_heads):
    ch = np.arange(n_heads * HEAD_DIM) // HEAD_DIM
    ind = (ch[:, None] == np.arange(LANES)[None, :]).astype(np.float32)
    return jnp.asarray(ind, BF16), jnp.asarray(ind.T, BF16)


def kernel(x_prompt, x_sample, state_conv, cache_k, cache_v, p_prompt, p_sample, norm_mix, norm_ffn, norm_ple,
           conv_w_in, conv_b_in, conv_w_dw, conv_b_dw, conv_ln_g, conv_ln_b, conv_w_out, conv_b_out, attn_w_qkv,
           attn_q_norm, attn_k_norm, attn_sinks, attn_w_o, moe_w_rg, moe_b_rg, moe_w_re, moe_b_re, moe_w_gate,
           moe_w_up, moe_w_down, ple_w_gate, ple_w_proj):
    bp, seq, d = x_prompt.shape
    bs, t_new, _ = x_sample.shape
    assert bp == 1 and d == D_MODEL and seq % WINDOW == 0 and bs % SAMPLE_SEQS == 0
    assert seq % (bs * t_new) == 0 and (bs * t_new) % DMA_CHUNK == 0
    depth = norm_mix.shape[0]
    row2 = lambda a: a.reshape(1, -1)

    y_p = x_prompt.reshape(seq, d)
    y_s = x_sample.reshape(bs * t_new, d)
    conv_p, conv_s, k_p, v_p, k_s, v_s = [], [], [], [], [], []

    for i in range(depth):
        j = i // 2
        g_mix = row2(norm_mix[i])
        if i % 2 == 0:
            w_in = conv_w_in[j].astype(BF16)
            b_in = row2(conv_b_in[j])
            tail = (conv_w_dw[j], row2(conv_b_dw[j]), row2(conv_ln_g[j]), row2(conv_ln_b[j]),
                    conv_w_out[j].astype(BF16), row2(conv_b_out[j]))
            u_p = _conv_in(y_p, g_mix, w_in, b_in)
            u_s = _conv_in(y_s, g_mix, w_in, b_in)
            conv_p.append(u_p[seq - CONV_STATE:].reshape(1, CONV_STATE, d))
            y_p = _conv_out_p(u_p, y_p, *tail)
            ys_t, new_state = _conv_out_s(state_conv[j], u_s.reshape(bs, t_new, d),
                                          y_s.reshape(bs, t_new, d).transpose(1, 0, 2), *tail)
            conv_s.append(new_state)
            y_s = ys_t.transpose(1, 0, 2).reshape(bs * t_new, d)
        else:
            w_qkv = attn_w_qkv[j].astype(BF16)
            w_o = attn_w_o[j].astype(BF16)
            iq, iqt = _head_indicator(N_HEADS)
            ik, ikt = _head_indicator(N_KV_HEADS)
            qg = row2(jnp.tile(attn_q_norm[j], N_HEADS))
            kg = row2(jnp.tile(attn_k_norm[j], N_KV_HEADS))
            sinks = attn_sinks[j]
            q1, k1, v1 = _qkv(y_p, g_mix, w_qkv, iq, iqt, ik, ikt, qg, kg)
            q2, k2, v2 = _qkv(y_s, g_mix, w_qkv, iq, iqt, ik, ikt, qg, kg)
            k_p.append(k1[seq - WINDOW:].reshape(1, WINDOW, N_KV_HEADS, HEAD_DIM))
            v_p.append(v1[seq - WINDOW:].reshape(1, WINDOW, N_KV_HEADS, HEAD_DIM))
            y_p = _attn_p(sinks, q1, k1, v1, y_p, _prompt_bias(), w_o)
            n_cols = -(-(SAMPLE_SEQS * (WINDOW + t_new)) // LANES) * LANES
            o_s, nk, nv = _attn_s(sinks, q2, k2, v2, cache_k[j].reshape(bs * WINDOW, KV_DIM),
                                  cache_v[j].reshape(bs * WINDOW, KV_DIM), _sample_bias(t_new, n_cols), t_new)
            k_s.append(nk.reshape(bs, N_KV_HEADS, HEAD_DIM, WINDOW).transpose(0, 3, 1, 2))
            v_s.append(nv.reshape(bs, N_KV_HEADS, HEAD_DIM, WINDOW).transpose(0, 3, 1, 2))
            y_s = _proj_res(o_s, y_s, w_o)

        w_r = jnp.zeros((d, ROUTER_LANES), F32)
        w_r = w_r.at[:, :N_EXPERT_GROUPS].set(moe_w_rg[i]).at[:, EXPERT_LANE0:EXPERT_LANE0 + N_EXPERTS].set(moe_w_re[i])
        b_r = jnp.zeros((1, ROUTER_LANES), F32)
        b_r = b_r.at[0, :N_EXPERT_GROUPS].set(moe_b_rg[i]).at[0, EXPERT_LANE0:EXPERT_LANE0 + N_EXPERTS].set(moe_b_re[i])
        moe = (row2(norm_ffn[i]), jnp.concatenate(_split_bf16(w_r), axis=1), b_r, moe_w_gate, moe_w_up, moe_w_down)
        ple = (row2(norm_ple[i]), ple_w_gate[i].astype(BF16), ple_w_proj[i].astype(BF16))
        y_p, y_s = _moe_ple(i, y_p, y_s, *moe, p_prompt.reshape(depth, seq, PLE_DIM),
                            p_sample[i].reshape(bs * t_new, PLE_DIM), *ple)

    return (y_p.reshape(1, seq, d), y_s.reshape(bs, t_new, d), jnp.stack(conv_p), jnp.stack(conv_s),
            jnp.stack(k_p), jnp.stack(v_p), jnp.stack(k_s), jnp.stack(v_s))
```

```python
import functools

import numpy as np

import jax
import jax.numpy as jnp
from jax import lax
from jax.experimental import pallas as pl
from jax.experimental.pallas import tpu as pltpu

F32 = jnp.float32
BF16 = jnp.bfloat16

D_MODEL = 1024
PLE_DIM = 256
CONV_WIDTH = 31
CONV_STATE = CONV_WIDTH - 1
N_HEADS = 16
N_KV_HEADS = 4
HEAD_DIM = 64
GROUP = N_HEADS // N_KV_HEADS
WINDOW = 128
KV_DIM = N_KV_HEADS * HEAD_DIM
N_EXPERT_GROUPS = 4
EXPERTS_PER_GROUP = 4
N_EXPERTS = 16
D_EXPERT = 256
EPS = 1e-6
NEG_INF = -1e30

LANES = 128
ROUTER_LANES = LANES
EXPERT_LANE0 = N_EXPERT_GROUPS
HALO = 32
CONV_ROWS = 32
NORM_ROWS = 512
CONV_PITCH = D_MODEL // LANES + 1
SAMPLE_SEQS = 8
PAIRS_PER_GROUP = EXPERTS_PER_GROUP * (EXPERTS_PER_GROUP - 1) // 2
N_BUCKETS = N_EXPERT_GROUPS * PAIRS_PER_GROUP
TM = 256
PAY_WIDTH = D_MODEL + LANES
ROW_PITCH = PAY_WIDTH // LANES
DMA_CHUNK = 512
RING = 3
PLE_RING = 3
VMEM_LIMIT = 48 * 1024 * 1024


def _row_tile(n):
    return 512 if n % 512 == 0 else n


def _params(*sem):
    return pltpu.CompilerParams(dimension_semantics=sem, vmem_limit_bytes=VMEM_LIMIT)


def _full(shape):
    nd = len(shape)
    return pl.BlockSpec(shape, lambda *_: (0,) * nd)


def _rms(x, g):
    ms = jnp.mean(x * x, axis=-1, keepdims=True)
    return x * lax.rsqrt(ms + EPS) * g


def _sigmoid(x):
    return 1.0 / (1.0 + jnp.exp(-x))


def _mm(a, b):
    return jnp.dot(a, b, preferred_element_type=F32)


def _conv_in_kernel(x_ref, g_ref, w_ref, b_ref, u_ref):
    h = _rms(x_ref[...], g_ref[...]).astype(BF16)
    z = _mm(h, w_ref[...]) + b_ref[...]
    u_ref[...] = z[:, :D_MODEL] * _sigmoid(z[:, D_MODEL:])


def _conv_in(x, g, w, b):
    n = x.shape[0]
    t = _row_tile(n)
    return pl.pallas_call(
        _conv_in_kernel,
        grid=(n // t,),
        in_specs=[pl.BlockSpec((t, D_MODEL), lambda i: (i, 0)), _full(g.shape), _full(w.shape), _full(b.shape)],
        out_specs=pl.BlockSpec((t, D_MODEL), lambda i: (i, 0)),
        out_shape=jax.ShapeDtypeStruct((n, D_MODEL), F32),
        compiler_params=_params("arbitrary"),
        name="conv_in",
    )(x, g, w, b)


def _ln_silu(c, g, b):
    mu = jnp.mean(c, axis=-1, keepdims=True)
    xc = c - mu
    var = jnp.mean(xc * xc, axis=-1, keepdims=True)
    cn = xc * lax.rsqrt(var + EPS) * g + b
    return cn * _sigmoid(cn)


def _conv_out_p_kernel(u_ref, halo_ref, x_ref, wdw_ref, bdw_ref, lng_ref, lnb_ref, wout_ref, bout_ref, y_ref,
                       ubuf, cbuf, hbuf):
    t = u_ref.shape[0]
    i = pl.program_id(0)
    nj = D_MODEL // LANES

    def put(r, j, val):
        ubuf[pl.ds(r * CONV_PITCH + j, 8, stride=CONV_PITCH), :] = val

    for r in range(0, HALO, 8):
        for j in range(nj):
            put(r, j, jnp.where(i > 0, halo_ref[r:r + 8, j * LANES:(j + 1) * LANES], 0.0))

    def fill(rr, carry):
        r = pl.multiple_of(rr * 8, 8)
        for j in range(nj):
            put(r + HALO, j, u_ref[pl.ds(r, 8), j * LANES:(j + 1) * LANES])
        return carry

    lax.fori_loop(0, t // 8, fill, 0, unroll=4)

    def conv_chunk(rr, carry):
        r0 = pl.multiple_of(rr * CONV_ROWS, CONV_ROWS)
        for j in range(nj):
            lanes = slice(j * LANES, (j + 1) * LANES)
            accs = [None] * (CONV_ROWS // 8)
            for k in range(CONV_WIDTH):
                wk = wdw_ref[k:k + 1, lanes]
                for q in range(CONV_ROWS // 8):
                    r = r0 + (HALO - CONV_STATE + k + 8 * q)
                    term = wk * ubuf[pl.ds(r * CONV_PITCH + j, 8, stride=CONV_PITCH), :]
                    accs[q] = term if k == 0 else accs[q] + term
            for q in range(CONV_ROWS // 8):
                cbuf[pl.ds(r0 + 8 * q, 8), lanes] = accs[q]
        return carry

    lax.fori_loop(0, t // CONV_ROWS, conv_chunk, 0)

    def norm_chunk(rr, carry):
        r0 = pl.multiple_of(rr * NORM_ROWS, NORM_ROWS)
        c = cbuf[pl.ds(r0, NORM_ROWS), :] + bdw_ref[...]
        hbuf[pl.ds(r0, NORM_ROWS), :] = _ln_silu(c, lng_ref[...], lnb_ref[...]).astype(BF16)
        return carry

    lax.fori_loop(0, t // NORM_ROWS, norm_chunk, 0)
    y_ref[...] = x_ref[...] + _mm(hbuf[...], wout_ref[...]) + bout_ref[...]


def _conv_out_p(u, x, wdw, bdw, lng, lnb, wout, bout):
    n = u.shape[0]
    t = _row_tile(n)
    hb = t // HALO
    row = lambda i: (i, 0)
    return pl.pallas_call(
        _conv_out_p_kernel,
        grid=(n // t,),
        in_specs=[pl.BlockSpec((t, D_MODEL), row),
                  pl.BlockSpec((HALO, D_MODEL), lambda i: (jnp.maximum(i * hb - 1, 0), 0)),
                  pl.BlockSpec((t, D_MODEL), row),
                  _full(wdw.shape), _full(bdw.shape), _full(lng.shape), _full(lnb.shape),
                  _full(wout.shape), _full(bout.shape)],
        out_specs=pl.BlockSpec((t, D_MODEL), row),
        out_shape=jax.ShapeDtypeStruct((n, D_MODEL), F32),
        scratch_shapes=[pltpu.VMEM(((t + HALO) * CONV_PITCH, LANES), F32), pltpu.VMEM((t, D_MODEL), F32),
                        pltpu.VMEM((t, D_MODEL), BF16)],
        compiler_params=_params("arbitrary"),
        name="conv_out_prompt",
    )(u, u, x, wdw, bdw, lng, lnb, wout, bout)


def _conv_out_s_kernel(st_ref, u_ref, x_ref, wdw_ref, bdw_ref, lng_ref, lnb_ref, wout_ref, bout_ref, y_ref, ns_ref,
                       wbuf, cbuf):
    bb, nt, _ = u_ref.shape
    win = CONV_STATE + nt
    nj = D_MODEL // LANES
    seq_pitch = win * CONV_PITCH

    def put(b, r0, rows, j, val):
        wbuf[pl.ds((b * win + r0) * CONV_PITCH + j, rows, stride=CONV_PITCH), :] = val

    for b in range(bb):
        for j in range(nj):
            lanes = slice(j * LANES, (j + 1) * LANES)
            for r0 in range(0, CONV_STATE, 8):
                rows = min(8, CONV_STATE - r0)
                put(b, r0, rows, j, st_ref[b, r0:r0 + rows, lanes])
            put(b, CONV_STATE, nt, j, u_ref[b, :, lanes])

    for b in range(bb):
        for r0 in range(0, CONV_STATE, 8):
            rows = min(8, CONV_STATE - r0)
            for j in range(nj):
                src = (b * win + nt + r0) * CONV_PITCH + j
                ns_ref[b, r0:r0 + rows, j * LANES:(j + 1) * LANES] = wbuf[pl.ds(src, rows, stride=CONV_PITCH), :]

    for t in range(nt):
        for b0 in range(0, bb, 8):
            for j in range(nj):
                lanes = slice(j * LANES, (j + 1) * LANES)
                acc = None
                for k in range(CONV_WIDTH):
                    start = (b0 * win + t + k) * CONV_PITCH + j
                    term = wdw_ref[k:k + 1, lanes] * wbuf[pl.ds(start, 8, stride=seq_pitch), :]
                    acc = term if acc is None else acc + term
                cbuf[t * bb + b0:t * bb + b0 + 8, lanes] = acc
    c = cbuf[...] + bdw_ref[...]
    h = _ln_silu(c, lng_ref[...], lnb_ref[...]).astype(BF16)
    y = _mm(h, wout_ref[...]) + bout_ref[...]
    for t in range(nt):
        y_ref[t] = x_ref[t] + y[t * bb:(t + 1) * bb]


def _conv_out_s(state, u, x, wdw, bdw, lng, lnb, wout, bout):
    b, nt, _ = u.shape
    bb = 16 if b % 16 == 0 else b
    assert bb % 8 == 0
    win = CONV_STATE + nt
    seq = lambda i: (i, 0, 0)
    tm = lambda i: (0, i, 0)
    return pl.pallas_call(
        _conv_out_s_kernel,
        grid=(b // bb,),
        in_specs=[pl.BlockSpec((bb, CONV_STATE, D_MODEL), seq), pl.BlockSpec((bb, nt, D_MODEL), seq),
                  pl.BlockSpec((nt, bb, D_MODEL), tm),
                  _full(wdw.shape), _full(bdw.shape), _full(lng.shape), _full(lnb.shape),
                  _full(wout.shape), _full(bout.shape)],
        out_specs=[pl.BlockSpec((nt, bb, D_MODEL), tm), pl.BlockSpec((bb, CONV_STATE, D_MODEL), seq)],
        out_shape=[jax.ShapeDtypeStruct((nt, b, D_MODEL), F32), jax.ShapeDtypeStruct((b, CONV_STATE, D_MODEL), F32)],
        scratch_shapes=[pltpu.VMEM((bb * win * CONV_PITCH, LANES), F32), pltpu.VMEM((nt * bb, D_MODEL), F32)],
        compiler_params=_params("arbitrary"),
        name="conv_out_sample",
    )(state, u, x, wdw, bdw, lng, lnb, wout, bout)


def _route(lg):
    big = 3.0e38
    lane = lax.broadcasted_iota(jnp.int32, lg.shape, 1)
    lanef = lane.astype(F32)
    is_g = lane < N_EXPERT_GROUPS
    gl = jnp.where(is_g, lg, -big)
    gmax = jnp.max(gl, axis=1, keepdims=True)
    gsum = jnp.sum(jnp.where(is_g, jnp.exp(gl - gmax), 0.0), axis=1, keepdims=True)
    g_w = 1.0 / gsum
    g_idx = jnp.min(jnp.where(gl == gmax, lanef, big), axis=1, keepdims=True)
    rel = lanef - float(EXPERT_LANE0) - g_idx * float(EXPERTS_PER_GROUP)
    in_grp = jnp.where(rel >= 0.0, jnp.where(rel < float(EXPERTS_PER_GROUP), 1.0, 0.0), 0.0) > 0.5
    el = jnp.where(in_grp, lg, -big)
    e1 = jnp.max(el, axis=1, keepdims=True)
    i1 = jnp.min(jnp.where(el == e1, lanef, big), axis=1, keepdims=True)
    el2 = jnp.where(lanef == i1, -big, el)
    e2 = jnp.max(el2, axis=1, keepdims=True)
    i2 = jnp.min(jnp.where(el2 == e2, lanef, big), axis=1, keepdims=True)
    tt = jnp.exp(e2 - e1)
    w1 = g_w / (1.0 + tt)
    w2 = g_w * tt / (1.0 + tt)
    base = float(EXPERT_LANE0) + g_idx * float(EXPERTS_PER_GROUP)
    a = jnp.minimum(i1, i2) - base
    b = jnp.maximum(i1, i2) - base
    pair = a * (7.0 - a) * 0.5 + (b - a - 1.0)
    first_is_lo = i1 < i2
    return (g_idx * float(PAIRS_PER_GROUP) + pair, jnp.where(first_is_lo, w1, w2), jnp.where(first_is_lo, w2, w1))


def _to_row_linear(dst_ref, src_ref, n_tiles):
    def body(g, carry):
        r = pl.multiple_of(g * 8, 8)
        for j in range(n_tiles):
            dst_ref[pl.ds(r * ROW_PITCH + j, 8, stride=ROW_PITCH), :] = src_ref[pl.ds(r, 8), j * LANES:(j + 1) * LANES]
        return carry

    lax.fori_loop(0, src_ref.shape[0] // 8, body, 0, unroll=4)


def _from_row_linear(dst_ref, src_ref, n_tiles, pitch):
    def body(g, carry):
        r = pl.multiple_of(g * 8, 8)
        for j in range(n_tiles):
            dst_ref[pl.ds(r, 8), j * LANES:(j + 1) * LANES] = src_ref[pl.ds(r * pitch + j, 8, stride=pitch), :]
        return carry

    lax.fori_loop(0, dst_ref.shape[0] // 8, body, 0, unroll=4)


def _pick(i, n_prompt_tiles, prompt_ref, sample_ref):
    return jnp.where(i < n_prompt_tiles, prompt_ref[...], sample_ref[...])


def _route_kernel(n_prompt_tiles, yp_ref, ys_ref, g_ref, wr_ref, br_ref, tri_ref, pay_ref, meta_ref, cnt_ref, carry):
    i = pl.program_id(0)

    @pl.when(i == 0)
    def _():
        carry[...] = jnp.zeros_like(carry)

    xf = _rms(_pick(i, n_prompt_tiles, yp_ref, ys_ref), g_ref[...])
    x_hi, x_lo = _split_bf16(xf)
    part = _mm(x_hi, wr_ref[...]) + _mm(x_lo, wr_ref[...])
    logits = part[:, :ROUTER_LANES] + part[:, ROUTER_LANES:] + br_ref[...]
    bucket, w_lo, w_hi = _route(logits)
    lane = lax.broadcasted_iota(jnp.int32, logits.shape, 1)
    onehot = jnp.where(lane.astype(F32) == bucket, 1.0, 0.0)
    before = _mm(tri_ref[...], onehot.astype(BF16)) + carry[...]
    rank = jnp.sum(onehot * before, axis=1, keepdims=True)
    carry[...] += jnp.sum(onehot, axis=0, keepdims=True)
    cnt_ref[...] = carry[...]
    meta = jnp.where(lane == 0, bucket, jnp.where(lane == 1, rank, 0.0))
    meta_ref[...] = jnp.transpose(meta)[:8, :].astype(jnp.int32)

    gates = jnp.where(lane == 0, w_lo, jnp.where(lane == 1, w_hi, 0.0))
    nj = D_MODEL // LANES
    for r in range(0, xf.shape[0], 8):
        for j in range(nj):
            pay_ref[pl.ds(r * ROW_PITCH + j, 8, stride=ROW_PITCH), :] = xf[r:r + 8, j * LANES:(j + 1) * LANES]
        pay_ref[pl.ds(r * ROW_PITCH + nj, 8, stride=ROW_PITCH), :] = gates[r:r + 8, :]


def _route_call(y_p, y_s, g, wr, br, tri):
    t = y_s.shape[0]
    npt = y_p.shape[0] // t
    n = y_p.shape[0] + t
    pidx = lambda i: (jnp.minimum(i, npt - 1), 0)
    return pl.pallas_call(
        functools.partial(_route_kernel, npt),
        grid=(npt + 1,),
        in_specs=[pl.BlockSpec((t, D_MODEL), pidx), _full(y_s.shape), _full(g.shape), _full(wr.shape),
                  _full(br.shape), _full(tri.shape)],
        out_specs=[pl.BlockSpec((t * ROW_PITCH, LANES), lambda i: (i, 0)),
                   pl.BlockSpec((8, t), lambda i: (0, i)), _full((1, ROUTER_LANES))],
        out_shape=[jax.ShapeDtypeStruct((n * ROW_PITCH, LANES), F32),
                   jax.ShapeDtypeStruct((8, n), jnp.int32), jax.ShapeDtypeStruct((1, ROUTER_LANES), F32)],
        scratch_shapes=[pltpu.VMEM((1, ROUTER_LANES), F32)],
        compiler_params=_params("arbitrary"),
        name="route",
    )(y_p, y_s, g, wr, br, tri)


def _permute_kernel(pos_ref, zstart_ref, nz_ref, used_ref, src_ref, dst_ref, zbuf, ring, zsem, in_sems, out_sems):
    zbuf[...] = jnp.zeros_like(zbuf)

    def zero_tile(start):
        return pltpu.make_async_copy(zbuf, dst_ref.at[pl.ds(pl.multiple_of(start * ROW_PITCH, 8), TM * ROW_PITCH)], zsem)

    for b in range(N_BUCKETS):
        @pl.when(nz_ref[b] > 0)
        def _():
            zero_tile(zstart_ref[b]).start()
    n_tiles = dst_ref.shape[0] // (TM * ROW_PITCH)

    def start_unused(i, carry):
        zero_tile(i * TM).start()
        return carry

    lax.fori_loop(used_ref[0], n_tiles, start_unused, 0)
    for b in range(N_BUCKETS):
        @pl.when(nz_ref[b] > 0)
        def _():
            zero_tile(0).wait()

    def wait_unused(i, carry):
        zero_tile(0).wait()
        return carry

    lax.fori_loop(used_ref[0], n_tiles, wait_unused, 0)

    chunk_rows = DMA_CHUNK * ROW_PITCH
    n_chunks = src_ref.shape[0] // chunk_rows

    def fetch(c, slot):
        return pltpu.make_async_copy(src_ref.at[pl.ds(pl.multiple_of(c * chunk_rows, 8), chunk_rows)], ring.at[slot],
                                     in_sems.at[slot])

    def drain(slot):
        pltpu.make_async_copy(ring.at[slot], dst_ref.at[pl.ds(0, chunk_rows)], out_sems.at[slot]).wait()

    fetch(0, 0).start()

    def step(c, carry):
        slot = lax.rem(c, RING)
        nxt = lax.rem(c + 1, RING)
        fetch(c, slot).wait()

        @pl.when(c >= RING - 1)
        def _():
            drain(nxt)

        @pl.when(c + 1 < n_chunks)
        def _():
            fetch(c + 1, nxt).start()

        def send(jj, carry2):
            row = pos_ref[c * DMA_CHUNK + jj] * ROW_PITCH
            pltpu.make_async_copy(ring.at[slot, pl.ds(jj * ROW_PITCH, ROW_PITCH)], dst_ref.at[pl.ds(row, ROW_PITCH)],
                                  out_sems.at[slot]).start()
            return carry2

        lax.fori_loop(0, DMA_CHUNK, send, 0, unroll=8)
        return carry

    lax.fori_loop(0, n_chunks, step, 0)
    for c in range(n_chunks - (RING - 1), n_chunks):
        drain(c % RING)


def _permute(pos, zstart, nz, used, payload, n_rows_out):
    assert payload.shape[0] // (DMA_CHUNK * ROW_PITCH) >= RING
    return pl.pallas_call(
        _permute_kernel,
        grid_spec=pltpu.PrefetchScalarGridSpec(
            num_scalar_prefetch=4, grid=(1,),
            in_specs=[pl.BlockSpec(memory_space=pl.ANY)],
            out_specs=pl.BlockSpec(memory_space=pl.ANY),
            scratch_shapes=[pltpu.VMEM((TM * ROW_PITCH, LANES), payload.dtype),
                            pltpu.VMEM((RING, DMA_CHUNK * ROW_PITCH, LANES), payload.dtype),
                            pltpu.SemaphoreType.DMA(()), pltpu.SemaphoreType.DMA((RING,)),
                            pltpu.SemaphoreType.DMA((RING,))]),
        out_shape=jax.ShapeDtypeStruct((n_rows_out * ROW_PITCH, LANES), payload.dtype),
        compiler_params=_params("arbitrary"),
        name="permute",
    )(pos, zstart, nz, used, payload)


def _silu(x):
    return x * _sigmoid(x)


def _experts_kernel(tix_ref, lo_ref, hi_ref, valid_ref, fresh_ref, xs_ref, wgl_ref, wgh_ref, wul_ref, wuh_ref,
                    wdl_ref, wdh_ref, ys_ref, wup_s, wdn_s, xbuf, ybuf):
    i = pl.program_id(0)

    @pl.when(fresh_ref[i] > 0)
    def _():
        for s, ref in enumerate((wgl_ref, wgh_ref, wul_ref, wuh_ref)):
            wup_s[s] = ref[0, 0].astype(BF16)
        for s, ref in enumerate((wdl_ref, wdh_ref)):
            wdn_s[s] = ref[0, 0].astype(BF16)

    @pl.when(valid_ref[i] == 0)
    def _():
        ys_ref[...] = jnp.zeros_like(ys_ref)

    @pl.when(valid_ref[i] > 0)
    def _():
        _from_row_linear(xbuf, xs_ref, ROW_PITCH, ROW_PITCH)
        x = xbuf[:, :D_MODEL].astype(BF16)
        gates = xbuf[:, D_MODEL:]
        h_lo = _silu(_mm(x, wup_s[0])) * _mm(x, wup_s[2]) * gates[:, 0:1]
        h_hi = _silu(_mm(x, wup_s[1])) * _mm(x, wup_s[3]) * gates[:, 1:2]
        ybuf[:, :D_MODEL] = _mm(h_lo.astype(BF16), wdn_s[0]) + _mm(h_hi.astype(BF16), wdn_s[1])
        ybuf[:, D_MODEL:] = jnp.zeros((TM, LANES), F32)
        _to_row_linear(ys_ref, ybuf, ROW_PITCH)


def _experts(layer, tix, lo, hi, valid, fresh, xs, wg, wu, wd):
    n_tiles = xs.shape[0] // (TM * ROW_PITCH)
    row = lambda i, tix, lo, hi, valid, fresh: (tix[i], 0)
    row_out = lambda i, tix, lo, hi, valid, fresh: (i, 0)
    e_lo = lambda i, tix, lo, hi, valid, fresh: (layer, lo[i], 0, 0)
    e_hi = lambda i, tix, lo, hi, valid, fresh: (layer, hi[i], 0, 0)
    up = pl.BlockSpec((1, 1, D_MODEL, D_EXPERT), e_lo), pl.BlockSpec((1, 1, D_MODEL, D_EXPERT), e_hi)
    down = pl.BlockSpec((1, 1, D_EXPERT, D_MODEL), e_lo), pl.BlockSpec((1, 1, D_EXPERT, D_MODEL), e_hi)
    return pl.pallas_call(
        _experts_kernel,
        grid_spec=pltpu.PrefetchScalarGridSpec(
            num_scalar_prefetch=5, grid=(n_tiles,),
            in_specs=[pl.BlockSpec((TM * ROW_PITCH, LANES), row), *up, *up, *down],
            out_specs=pl.BlockSpec((TM * ROW_PITCH, LANES), row_out),
            scratch_shapes=[pltpu.VMEM((4, D_MODEL, D_EXPERT), BF16), pltpu.VMEM((2, D_EXPERT, D_MODEL), BF16),
                            pltpu.VMEM((TM, PAY_WIDTH), F32), pltpu.VMEM((TM, PAY_WIDTH), F32)]),
        out_shape=jax.ShapeDtypeStruct(xs.shape, F32),
        compiler_params=_params("arbitrary"),
        name="experts",
    )(tix, lo, hi, valid, fresh, xs, wg, wg, wu, wu, wd, wd)


def _ple_kernel(n_prompt_tiles, pos_ref, yp_ref, ys_ref, moe_ref, pp_ref, ps_ref, g_ref, wg_ref, wp_ref, op_ref, os_ref,
                mring, mbuf, sems):
    i = pl.program_id(0)
    t = mbuf.shape[0]
    w = D_MODEL // LANES

    def gather(tile, slot):
        def body(jj, carry):
            row = pos_ref[tile * t + jj] * ROW_PITCH
            pltpu.make_async_copy(moe_ref.at[pl.ds(row, w)], mring.at[slot, pl.ds(jj * w, w)], sems.at[slot]).start()
            return carry

        lax.fori_loop(0, t, body, 0, unroll=8)

    def wait_tile(slot):
        pltpu.make_async_copy(moe_ref.at[pl.ds(0, t * w)], mring.at[slot], sems.at[slot]).wait()

    @pl.when(i == 0)
    def _():
        gather(0, 0)
        gather(1, 1)

    slot = lax.rem(i, PLE_RING)
    nslot = lax.rem(i + 2, PLE_RING)
    wait_tile(slot)
    _from_row_linear(mbuf, mring.at[slot], w, w)
    y2 = _pick(i, n_prompt_tiles, yp_ref, ys_ref) + mbuf[...]
    hn = _rms(y2, g_ref[...]).astype(BF16)
    gt = _sigmoid(_mm(hn, wg_ref[...]))
    pr = _mm(jnp.where(i < n_prompt_tiles, pp_ref[0], ps_ref[...]).astype(BF16), wp_ref[...])
    y3 = y2 + gt * pr
    nxt = jnp.minimum(i + 2, n_prompt_tiles) * t
    for jj in range(t):
        row = pos_ref[nxt + jj] * ROW_PITCH
        pltpu.make_async_copy(moe_ref.at[pl.ds(row, w)], mring.at[nslot, pl.ds(jj * w, w)], sems.at[nslot]).start()

    @pl.when(i < n_prompt_tiles)
    def _():
        op_ref[...] = y3

    @pl.when(i >= n_prompt_tiles)
    def _():
        os_ref[...] = y3
        wait_tile(lax.rem(i + 1, PLE_RING))
        wait_tile(nslot)


def _ple(layer, pos, y_p, y_s, ys_sorted, p_p, p_s, g, wg, wp):
    t = y_s.shape[0]
    npt = y_p.shape[0] // t
    w = D_MODEL // LANES
    pidx = lambda i, pos: (jnp.minimum(i, npt - 1), 0)
    full = lambda shape: pl.BlockSpec(shape, lambda i, pos: (0,) * len(shape))
    return pl.pallas_call(
        functools.partial(_ple_kernel, npt),
        grid_spec=pltpu.PrefetchScalarGridSpec(
            num_scalar_prefetch=1, grid=(npt + 1,),
            in_specs=[pl.BlockSpec((t, D_MODEL), pidx), full(y_s.shape), pl.BlockSpec(memory_space=pl.ANY),
                      pl.BlockSpec((1, t, PLE_DIM), lambda i, pos: (layer, jnp.minimum(i, npt - 1), 0)),
                      full(p_s.shape), full(g.shape), full(wg.shape), full(wp.shape)],
            out_specs=[pl.BlockSpec((t, D_MODEL), pidx), full(y_s.shape)],
            scratch_shapes=[pltpu.VMEM((PLE_RING, t * w, LANES), F32), pltpu.VMEM((t, D_MODEL), F32),
                            pltpu.SemaphoreType.DMA((PLE_RING,))]),
        out_shape=[jax.ShapeDtypeStruct(y_p.shape, F32), jax.ShapeDtypeStruct(y_s.shape, F32)],
        compiler_params=_params("arbitrary"),
        name="ple",
    )(pos, y_p, y_s, ys_sorted, p_p, p_s, g, wg, wp)


def _bucket_experts():
    pairs = [(a, b) for a in range(EXPERTS_PER_GROUP) for b in range(a + 1, EXPERTS_PER_GROUP)]
    lo = [g * EXPERTS_PER_GROUP + a for g in range(N_EXPERT_GROUPS) for a, _ in pairs]
    hi = [g * EXPERTS_PER_GROUP + b for g in range(N_EXPERT_GROUPS) for _, b in pairs]
    return jnp.array(lo, jnp.int32), jnp.array(hi, jnp.int32)


def _moe_ple(layer, y_p, y_s, gffn, wr, br, wg, wu, wd, p_p, p_s, gple, pwg, pwp):
    t = y_s.shape[0]
    n = y_p.shape[0] + t
    tri = jnp.asarray(np.tri(t, t, -1, dtype=np.float32), BF16)
    payload, meta, counts = _route_call(y_p, y_s, gffn, wr, br, tri)

    bucket, rank = meta[0], meta[1]
    cnt = counts[0, :N_BUCKETS].astype(jnp.int32)
    padded = (cnt + TM - 1) // TM * TM
    ends = jnp.cumsum(padded)
    starts = ends - padded
    in_bucket = bucket[:, None] == jnp.arange(N_BUCKETS, dtype=jnp.int32)[None, :]
    pos = rank + jnp.sum(jnp.where(in_bucket, starts[None, :], 0), axis=1)
    n_rows = n + N_BUCKETS * TM
    n_tiles = n_rows // TM
    tile_start = jnp.arange(n_tiles, dtype=jnp.int32) * TM
    valid = tile_start < ends[-1]
    used = (ends[-1] // TM).astype(jnp.int32)
    tix = jnp.arange(n_tiles, dtype=jnp.int32)
    tile_bucket = jnp.sum((ends[None, :] <= (jnp.minimum(tix, used - 1) * TM)[:, None]).astype(jnp.int32), axis=1)
    prev_bucket = jnp.concatenate([jnp.full((1,), -1, jnp.int32), tile_bucket[:-1]])
    fresh = valid & (tile_bucket != prev_bucket)
    first_tile = jnp.where(fresh, tix, n_tiles)
    next_first = lax.cummin(jnp.concatenate([first_tile[1:], jnp.full((1,), n_tiles, jnp.int32)]), reverse=True)
    next_bucket = jnp.where(next_first < n_tiles, jnp.take(tile_bucket, jnp.minimum(next_first, n_tiles - 1)),
                            tile_bucket)
    resident = jnp.where(fresh, tile_bucket, next_bucket)
    lo_tab, hi_tab = _bucket_experts()
    lo = jnp.take(lo_tab, resident)
    hi = jnp.take(hi_tab, resident)

    xs = _permute(pos, (ends - TM).astype(jnp.int32), (cnt > 0).astype(jnp.int32), used.reshape(1), payload, n_rows)
    ys = _experts(layer, jnp.minimum(tix, used - 1), lo, hi, valid.astype(jnp.int32), fresh.astype(jnp.int32), xs, wg,
                  wu, wd)
    return _ple(layer, pos, y_p, y_s, ys, p_p, p_s, gple, pwg, pwp)


def _split_bf16(a):
    hi = a.astype(BF16)
    lo = (a - hi.astype(F32)).astype(BF16)
    return hi, lo


def _head_norm(a, ind_ref, indt_ref, gain):
    hi, lo = _split_bf16(a * a)
    ss = _mm(hi, ind_ref[...]) + _mm(lo, ind_ref[...])
    inv = lax.rsqrt(ss * (1.0 / HEAD_DIM) + EPS)
    ihi, ilo = _split_bf16(inv)
    invb = _mm(ihi, indt_ref[...]) + _mm(ilo, indt_ref[...])
    return a * invb * gain


def _qkv_kernel(x_ref, g_ref, w_ref, iq_ref, iqt_ref, ik_ref, ikt_ref, qg_ref, kg_ref, q_ref, k_ref, v_ref):
    h = _rms(x_ref[...], g_ref[...]).astype(BF16)
    qkv = _mm(h, w_ref[...])
    nq = N_HEADS * HEAD_DIM
    q = _head_norm(qkv[:, :nq], iq_ref, iqt_ref, qg_ref[...])
    k = _head_norm(qkv[:, nq:nq + KV_DIM], ik_ref, ikt_ref, kg_ref[...])
    q_ref[...] = (q * (HEAD_DIM ** -0.5)).astype(BF16)
    k_ref[...] = k
    v_ref[...] = qkv[:, nq + KV_DIM:]


def _qkv(x, g, w, iq, iqt, ik, ikt, qg, kg):
    n = x.shape[0]
    t = _row_tile(n)
    row = lambda i: (i, 0)
    return pl.pallas_call(
        _qkv_kernel,
        grid=(n // t,),
        in_specs=[pl.BlockSpec((t, D_MODEL), row)] + [_full(a.shape) for a in (g, w, iq, iqt, ik, ikt, qg, kg)],
        out_specs=[pl.BlockSpec((t, N_HEADS * HEAD_DIM), row), pl.BlockSpec((t, KV_DIM), row),
                   pl.BlockSpec((t, KV_DIM), row)],
        out_shape=[jax.ShapeDtypeStruct((n, N_HEADS * HEAD_DIM), BF16), jax.ShapeDtypeStruct((n, KV_DIM), F32),
                   jax.ShapeDtypeStruct((n, KV_DIM), F32)],
        compiler_params=_params("arbitrary"),
        name="qkv",
    )(x, g, w, iq, iqt, ik, ikt, qg, kg)


def _dup_heads(a):
    out = []
    for s in range(KV_DIM // LANES):
        sl = a[:, s * LANES:(s + 1) * LANES]
        sw = pltpu.roll(sl, HEAD_DIM, axis=1)
        low = lax.broadcasted_iota(jnp.int32, sl.shape, 1) < HEAD_DIM
        out.append(jnp.where(low, sl, sw))
        out.append(jnp.where(low, sw, sl))
    return jnp.concatenate(out, axis=1).astype(BF16)


def _attend(q_rows, k2, v2, bias_of, sink_of, extra_mask):
    m_rows = q_rows.shape[0]
    low_q = lax.broadcasted_iota(jnp.int32, (m_rows, LANES), 1) < HEAD_DIM
    low_k = lax.broadcasted_iota(jnp.int32, (k2.shape[0], LANES), 1) < HEAD_DIM
    zero_q = jnp.zeros((m_rows, LANES), BF16)
    zero_k = jnp.zeros((k2.shape[0], LANES), BF16)
    slabs = []
    for g in range(N_KV_HEADS):
        kg = k2[:, g * LANES:(g + 1) * LANES]
        vg = v2[:, g * LANES:(g + 1) * LANES]
        v_lo = jnp.where(low_k, vg, zero_k)
        v_hi = jnp.where(low_k, zero_k, vg)
        lhs = []
        for a in range(GROUP):
            h = g * GROUP + a
            qs = q_rows[:, (h // 2) * LANES:(h // 2 + 1) * LANES]
            lhs.append(jnp.where(low_q, qs, zero_q) if h % 2 == 0 else jnp.where(low_q, zero_q, qs))
        s = lax.dot_general(jnp.concatenate(lhs, axis=0), kg, (((1,), (1,)), ((), ())), preferred_element_type=F32)
        probs, rinv = [], []
        for a in range(GROUP):
            h = g * GROUP + a
            sa = s[a * m_rows:(a + 1) * m_rows] + bias_of(h)
            if extra_mask is not None:
                sa = jnp.where(extra_mask, NEG_INF, sa)
            sink = sink_of(h)
            m = jnp.maximum(jnp.max(sa, axis=1, keepdims=True), sink)
            p = jnp.exp(sa - m)
            den = jnp.sum(p, axis=1, keepdims=True) + jnp.exp(sink - m)
            probs.append(p.astype(BF16))
            rinv.append(1.0 / den)
        for sp in range(GROUP // 2):
            o = _mm(probs[2 * sp], v_lo) + _mm(probs[2 * sp + 1], v_hi)
            slabs.append(o * jnp.where(low_q, rinv[2 * sp], rinv[2 * sp + 1]))
    return jnp.concatenate(slabs, axis=1)


def _attn_p_kernel(sink_ref, q_ref, k_ref, v_ref, x_ref, bias_ref, wo_ref, y_ref, kbuf, vbuf, obuf):
    t = q_ref.shape[0]
    i = pl.program_id(0)

    @pl.when(i == 0)
    def _():
        kbuf[0:WINDOW, :] = jnp.zeros((WINDOW, 2 * KV_DIM), BF16)
        vbuf[0:WINDOW, :] = jnp.zeros((WINDOW, 2 * KV_DIM), BF16)

    @pl.when(i > 0)
    def _():
        kbuf[0:WINDOW, :] = kbuf[t:t + WINDOW, :]
        vbuf[0:WINDOW, :] = vbuf[t:t + WINDOW, :]

    kbuf[WINDOW:, :] = _dup_heads(k_ref[...])
    vbuf[WINDOW:, :] = _dup_heads(v_ref[...])
    col = lax.broadcasted_iota(jnp.int32, (WINDOW, 2 * WINDOW), 1)
    for j in range(t // WINDOW):
        rows = slice(j * WINDOW, (j + 1) * WINDOW)
        keys = slice(j * WINDOW, (j + 2) * WINDOW)
        extra = jnp.logical_and(i == 0, col < WINDOW) if j == 0 else None
        o = _attend(q_ref[rows, :], kbuf[keys, :], vbuf[keys, :], lambda h: bias_ref[h], lambda h: sink_ref[h], extra)
        obuf[rows, :] = o.astype(BF16)
    y_ref[...] = x_ref[...] + _mm(obuf[...], wo_ref[...])


def _attn_p(sinks, q, k, v, x, bias, wo):
    n = q.shape[0]
    t = _row_tile(n)
    row = lambda i: (i, 0)
    return pl.pallas_call(
        _attn_p_kernel,
        grid=(n // t,),
        in_specs=[pl.BlockSpec(memory_space=pltpu.SMEM),
                  pl.BlockSpec((t, N_HEADS * HEAD_DIM), row), pl.BlockSpec((t, KV_DIM), row),
                  pl.BlockSpec((t, KV_DIM), row), pl.BlockSpec((t, D_MODEL), row),
                  _full(bias.shape), _full(wo.shape)],
        out_specs=pl.BlockSpec((t, D_MODEL), row),
        out_shape=jax.ShapeDtypeStruct((n, D_MODEL), F32),
        scratch_shapes=[pltpu.VMEM((t + WINDOW, 2 * KV_DIM), BF16), pltpu.VMEM((t + WINDOW, 2 * KV_DIM), BF16),
                        pltpu.VMEM((t, N_HEADS * HEAD_DIM), BF16)],
        compiler_params=_params("arbitrary"),
        name="attn_prompt",
    )(sinks, q, k, v, x, bias, wo)


def _attn_s_kernel(sink_ref, q_ref, k_ref, v_ref, ck_ref, cv_ref, bias_ref, o_ref, nk_ref, nv_ref, shift_buf):
    pad = bias_ref.shape[2] - ck_ref.shape[0] - k_ref.shape[0]
    zpad = jnp.zeros((pad, KV_DIM), F32)
    k2 = _dup_heads(jnp.concatenate([ck_ref[...], k_ref[...], zpad], axis=0))
    v2 = _dup_heads(jnp.concatenate([cv_ref[...], v_ref[...], zpad], axis=0))
    o = _attend(q_ref[...], k2, v2, lambda h: bias_ref[h], lambda h: sink_ref[h], None)
    o_ref[...] = o.astype(BF16)
    t_new = k_ref.shape[0] // SAMPLE_SEQS
    for new_ref, old_ref, add_ref in ((nk_ref, ck_ref, k_ref), (nv_ref, cv_ref, v_ref)):
        for b in range(SAMPLE_SEQS):
            r = b * WINDOW
            shift_buf[0:WINDOW - t_new, :] = old_ref[r + t_new:r + WINDOW, :]
            shift_buf[WINDOW - t_new:WINDOW, :] = add_ref[b * t_new:(b + 1) * t_new, :]
            new_ref[b] = jnp.transpose(shift_buf[...])


def _attn_s(sinks, q, k, v, ck, cv, bias, t_new):
    n = q.shape[0]
    rows = SAMPLE_SEQS * t_new
    crow = SAMPLE_SEQS * WINDOW
    row = lambda i: (i, 0)
    return pl.pallas_call(
        _attn_s_kernel,
        grid=(n // rows,),
        in_specs=[pl.BlockSpec(memory_space=pltpu.SMEM),
                  pl.BlockSpec((rows, N_HEADS * HEAD_DIM), row), pl.BlockSpec((rows, KV_DIM), row),
                  pl.BlockSpec((rows, KV_DIM), row), pl.BlockSpec((crow, KV_DIM), row),
                  pl.BlockSpec((crow, KV_DIM), row), _full(bias.shape)],
        out_specs=[pl.BlockSpec((rows, N_HEADS * HEAD_DIM), row),
                   pl.BlockSpec((SAMPLE_SEQS, KV_DIM, WINDOW), lambda i: (i, 0, 0)),
                   pl.BlockSpec((SAMPLE_SEQS, KV_DIM, WINDOW), lambda i: (i, 0, 0))],
        out_shape=[jax.ShapeDtypeStruct((n, N_HEADS * HEAD_DIM), BF16),
                   jax.ShapeDtypeStruct((ck.shape[0] // WINDOW, KV_DIM, WINDOW), F32),
                   jax.ShapeDtypeStruct((cv.shape[0] // WINDOW, KV_DIM, WINDOW), F32)],
        scratch_shapes=[pltpu.VMEM((WINDOW, KV_DIM), F32)],
        compiler_params=_params("arbitrary"),
        name="attn_sample",
    )(sinks, q, k, v, ck, cv, bias)


def _proj_res_kernel(o_ref, x_ref, w_ref, y_ref):
    y_ref[...] = x_ref[...] + _mm(o_ref[...], w_ref[...])


def _proj_res(o, x, w):
    n = o.shape[0]
    t = _row_tile(n)
    row = lambda i: (i, 0)
    return pl.pallas_call(
        _proj_res_kernel,
        grid=(n // t,),
        in_specs=[pl.BlockSpec((t, o.shape[1]), row), pl.BlockSpec((t, D_MODEL), row), _full(w.shape)],
        out_specs=pl.BlockSpec((t, D_MODEL), row),
        out_shape=jax.ShapeDtypeStruct((n, D_MODEL), F32),
        compiler_params=_params("arbitrary"),
        name="proj_res",
    )(o, x, w)


def _alibi_slopes():
    return np.exp2(-8.0 * np.arange(1, N_HEADS + 1, dtype=np.float64) / N_HEADS).astype(np.float32)


def _band_bias(dist, allowed):
    b = -(_alibi_slopes()[:, None, None] * dist.astype(np.float32)[None])
    return jnp.asarray(np.where(allowed[None], b, np.float32(NEG_INF)).astype(np.float32))


def _prompt_bias():
    dist = WINDOW + np.arange(WINDOW)[:, None] - np.arange(2 * WINDOW)[None, :]
    return _band_bias(dist, (dist >= 0) & (dist <= WINDOW))


def _sample_bias(t_new, n_cols):
    c = np.arange(n_cols)
    n_cache = SAMPLE_SEQS * WINDOW
    n_new = SAMPLE_SEQS * t_new
    is_cache = c < n_cache
    is_new = (c >= n_cache) & (c < n_cache + n_new)
    seq_c = np.where(is_cache, c // WINDOW, (c - n_cache) // t_new)
    pos_c = np.where(is_cache, c % WINDOW, WINDOW + (c - n_cache) % t_new)
    r = np.arange(n_new)
    seq_r, tok_r = r // t_new, r % t_new
    dist = WINDOW + tok_r[:, None] - pos_c[None, :]
    allowed = (seq_r[:, None] == seq_c[None, :]) & (is_cache | is_new)[None, :] & (dist >= 0) & (dist <= WINDOW)
    return _band_bias(dist, allowed)


def _head_indicator(n_heads):
    ch = np.arange(n_heads * HEAD_DIM) // HEAD_DIM
    ind = (ch[:, None] == np.arange(LANES)[None, :]).astype(np.float32)
    return jnp.asarray(ind, BF16), jnp.asarray(ind.T, BF16)


def kernel(x_prompt, x_sample, state_conv, cache_k, cache_v, p_prompt, p_sample, norm_mix, norm_ffn, norm_ple,
           conv_w_in, conv_b_in, conv_w_dw, conv_b_dw, conv_ln_g, conv_ln_b, conv_w_out, conv_b_out, attn_w_qkv,
           attn_q_norm, attn_k_norm, attn_sinks, attn_w_o, moe_w_rg, moe_b_rg, moe_w_re, moe_b_re, moe_w_gate,
           moe_w_up, moe_w_down, ple_w_gate, ple_w_proj):
    bp, seq, d = x_prompt.shape
    bs, t_new, _ = x_sample.shape
    assert bp == 1 and d == D_MODEL and seq % WINDOW == 0 and bs % SAMPLE_SEQS == 0
    assert seq % (bs * t_new) == 0 and (bs * t_new) % DMA_CHUNK == 0
    depth = norm_mix.shape[0]
    row2 = lambda a: a.reshape(1, -1)

    y_p = x_prompt.reshape(seq, d)
    y_s = x_sample.reshape(bs * t_new, d)
    conv_p, conv_s, k_p, v_p, k_s, v_s = [], [], [], [], [], []

    for i in range(depth):
        j = i // 2
        g_mix = row2(norm_mix[i])
        if i % 2 == 0:
            w_in = conv_w_in[j].astype(BF16)
            b_in = row2(conv_b_in[j])
            tail = (conv_w_dw[j], row2(conv_b_dw[j]), row2(conv_ln_g[j]), row2(conv_ln_b[j]),
                    conv_w_out[j].astype(BF16), row2(conv_b_out[j]))
            u_p = _conv_in(y_p, g_mix, w_in, b_in)
            u_s = _conv_in(y_s, g_mix, w_in, b_in)
            conv_p.append(u_p[seq - CONV_STATE:].reshape(1, CONV_STATE, d))
            y_p = _conv_out_p(u_p, y_p, *tail)
            ys_t, new_state = _conv_out_s(state_conv[j], u_s.reshape(bs, t_new, d),
                                          y_s.reshape(bs, t_new, d).transpose(1, 0, 2), *tail)
            conv_s.append(new_state)
            y_s = ys_t.transpose(1, 0, 2).reshape(bs * t_new, d)
        else:
            w_qkv = attn_w_qkv[j].astype(BF16)
            w_o = attn_w_o[j].astype(BF16)
            iq, iqt = _head_indicator(N_HEADS)
            ik, ikt = _head_indicator(N_KV_HEADS)
            qg = row2(jnp.tile(attn_q_norm[j], N_HEADS))
            kg = row2(jnp.tile(attn_k_norm[j], N_KV_HEADS))
            sinks = attn_sinks[j]
            q1, k1, v1 = _qkv(y_p, g_mix, w_qkv, iq, iqt, ik, ikt, qg, kg)
            q2, k2, v2 = _qkv(y_s, g_mix, w_qkv, iq, iqt, ik, ikt, qg, kg)
            k_p.append(k1[seq - WINDOW:].reshape(1, WINDOW, N_KV_HEADS, HEAD_DIM))
            v_p.append(v1[seq - WINDOW:].reshape(1, WINDOW, N_KV_HEADS, HEAD_DIM))
            y_p = _attn_p(sinks, q1, k1, v1, y_p, _prompt_bias(), w_o)
            n_cols = -(-(SAMPLE_SEQS * (WINDOW + t_new)) // LANES) * LANES
            o_s, nk, nv = _attn_s(sinks, q2, k2, v2, cache_k[j].reshape(bs * WINDOW, KV_DIM),
                                  cache_v[j].reshape(bs * WINDOW, KV_DIM), _sample_bias(t_new, n_cols), t_new)
            k_s.append(nk.reshape(bs, N_KV_HEADS, HEAD_DIM, WINDOW).transpose(0, 3, 1, 2))
            v_s.append(nv.reshape(bs, N_KV_HEADS, HEAD_DIM, WINDOW).transpose(0, 3, 1, 2))
            y_s = _proj_res(o_s, y_s, w_o)

        w_r = jnp.zeros((d, ROUTER_LANES), F32)
        w_r = w_r.at[:, :N_EXPERT_GROUPS].set(moe_w_rg[i]).at[:, EXPERT_LANE0:EXPERT_LANE0 + N_EXPERTS].set(moe_w_re[i])
        b_r = jnp.zeros((1, ROUTER_LANES), F32)
        b_r = b_r.at[0, :N_EXPERT_GROUPS].set(moe_b_rg[i]).at[0, EXPERT_LANE0:EXPERT_LANE0 + N_EXPERTS].set(moe_b_re[i])
        moe = (row2(norm_ffn[i]), jnp.concatenate(_split_bf16(w_r), axis=1), b_r, moe_w_gate, moe_w_up, moe_w_down)
        ple = (row2(norm_ple[i]), ple_w_gate[i].astype(BF16), ple_w_proj[i].astype(BF16))
        y_p, y_s = _moe_ple(i, y_p, y_s, *moe, p_prompt.reshape(depth, seq, PLE_DIM),
                            p_sample[i].reshape(bs * t_new, PLE_DIM), *ple)

    return (y_p.reshape(1, seq, d), y_s.reshape(bs, t_new, d), jnp.stack(conv_p), jnp.stack(conv_s),
            jnp.stack(k_p), jnp.stack(v_p), jnp.stack(k_s), jnp.stack(v_s))
```

```python
import functools

import numpy as np

import jax
import jax.numpy as jnp
from jax import lax
from jax.experimental import pallas as pl
from jax.experimental.pallas import tpu as pltpu

F32 = jnp.float32
BF16 = jnp.bfloat16

D_MODEL = 1024
PLE_DIM = 256
CONV_WIDTH = 31
CONV_STATE = CONV_WIDTH - 1
N_HEADS = 16
N_KV_HEADS = 4
HEAD_DIM = 64
GROUP = N_HEADS // N_KV_HEADS
WINDOW = 128
KV_DIM = N_KV_HEADS * HEAD_DIM
N_EXPERT_GROUPS = 4
EXPERTS_PER_GROUP = 4
N_EXPERTS = 16
D_EXPERT = 256
EPS = 1e-6
NEG_INF = -1e30

LANES = 128
ROUTER_LANES = LANES
EXPERT_LANE0 = N_EXPERT_GROUPS
HALO = 32
CONV_ROWS = 32
NORM_ROWS = 512
LOOP_UNROLL = 8
CONV_PITCH = D_MODEL // LANES + 1
SAMPLE_SEQS = 8
PAIRS_PER_GROUP = EXPERTS_PER_GROUP * (EXPERTS_PER_GROUP - 1) // 2
N_BUCKETS = N_EXPERT_GROUPS * PAIRS_PER_GROUP
TM = 256
PAY_WIDTH = D_MODEL + LANES
ROW_PITCH = PAY_WIDTH // LANES
DMA_CHUNK = 512
RING = 3
PLE_RING = 3
VMEM_LIMIT = 48 * 1024 * 1024


def _row_tile(n):
    return 512 if n % 512 == 0 else n


def _params(*sem):
    return pltpu.CompilerParams(dimension_semantics=sem, vmem_limit_bytes=VMEM_LIMIT)


def _full(shape):
    nd = len(shape)
    return pl.BlockSpec(shape, lambda *_: (0,) * nd)


def _rms(x, g):
    ms = jnp.mean(x * x, axis=-1, keepdims=True)
    return x * lax.rsqrt(ms + EPS) * g


def _sigmoid(x):
    return 1.0 / (1.0 + jnp.exp(-x))


def _mm(a, b):
    return jnp.dot(a, b, preferred_element_type=F32)


def _conv_in_kernel(x_ref, g_ref, w_ref, b_ref, u_ref):
    h = _rms(x_ref[...], g_ref[...]).astype(BF16)
    z = _mm(h, w_ref[...]) + b_ref[...]
    u_ref[...] = z[:, :D_MODEL] * _sigmoid(z[:, D_MODEL:])


def _conv_in(x, g, w, b):
    n = x.shape[0]
    t = _row_tile(n)
    return pl.pallas_call(
        _conv_in_kernel,
        grid=(n // t,),
        in_specs=[pl.BlockSpec((t, D_MODEL), lambda i: (i, 0)), _full(g.shape), _full(w.shape), _full(b.shape)],
        out_specs=pl.BlockSpec((t, D_MODEL), lambda i: (i, 0)),
        out_shape=jax.ShapeDtypeStruct((n, D_MODEL), F32),
        compiler_params=_params("arbitrary"),
        name="conv_in",
    )(x, g, w, b)


def _ln_silu(c, g, b):
    mu = jnp.mean(c, axis=-1, keepdims=True)
    xc = c - mu
    var = jnp.mean(xc * xc, axis=-1, keepdims=True)
    cn = xc * lax.rsqrt(var + EPS) * g + b
    return cn * _sigmoid(cn)


def _conv_out_p_kernel(u_ref, halo_ref, x_ref, wdw_ref, bdw_ref, lng_ref, lnb_ref, wout_ref, bout_ref, y_ref,
                       ubuf, cbuf, hbuf):
    t = u_ref.shape[0]
    i = pl.program_id(0)
    nj = D_MODEL // LANES

    def put(r, j, val):
        ubuf[pl.ds(r * CONV_PITCH + j, 8, stride=CONV_PITCH), :] = val

    for r in range(0, HALO, 8):
        for j in range(nj):
            put(r, j, jnp.where(i > 0, halo_ref[r:r + 8, j * LANES:(j + 1) * LANES], 0.0))

    def fill(rr, carry):
        r = pl.multiple_of(rr * 8, 8)
        for j in range(nj):
            put(r + HALO, j, u_ref[pl.ds(r, 8), j * LANES:(j + 1) * LANES])
        return carry

    lax.fori_loop(0, t // 8, fill, 0, unroll=LOOP_UNROLL)

    def conv_chunk(rr, carry):
        r0 = pl.multiple_of(rr * CONV_ROWS, CONV_ROWS)
        for j in range(nj):
            lanes = slice(j * LANES, (j + 1) * LANES)
            accs = [None] * (CONV_ROWS // 8)
            for k in range(CONV_WIDTH):
                wk = wdw_ref[k:k + 1, lanes]
                for q in range(CONV_ROWS // 8):
                    r = r0 + (HALO - CONV_STATE + k + 8 * q)
                    term = wk * ubuf[pl.ds(r * CONV_PITCH + j, 8, stride=CONV_PITCH), :]
                    accs[q] = term if k == 0 else accs[q] + term
            for q in range(CONV_ROWS // 8):
                cbuf[pl.ds(r0 + 8 * q, 8), lanes] = accs[q]
        return carry

    lax.fori_loop(0, t // CONV_ROWS, conv_chunk, 0)

    def norm_chunk(rr, carry):
        r0 = pl.multiple_of(rr * NORM_ROWS, NORM_ROWS)
        c = cbuf[pl.ds(r0, NORM_ROWS), :] + bdw_ref[...]
        hbuf[pl.ds(r0, NORM_ROWS), :] = _ln_silu(c, lng_ref[...], lnb_ref[...]).astype(BF16)
        return carry

    lax.fori_loop(0, t // NORM_ROWS, norm_chunk, 0)
    y_ref[...] = x_ref[...] + _mm(hbuf[...], wout_ref[...]) + bout_ref[...]


def _conv_out_p(u, x, wdw, bdw, lng, lnb, wout, bout):
    n = u.shape[0]
    t = _row_tile(n)
    hb = t // HALO
    row = lambda i: (i, 0)
    return pl.pallas_call(
        _conv_out_p_kernel,
        grid=(n // t,),
        in_specs=[pl.BlockSpec((t, D_MODEL), row),
                  pl.BlockSpec((HALO, D_MODEL), lambda i: (jnp.maximum(i * hb - 1, 0), 0)),
                  pl.BlockSpec((t, D_MODEL), row),
                  _full(wdw.shape), _full(bdw.shape), _full(lng.shape), _full(lnb.shape),
                  _full(wout.shape), _full(bout.shape)],
        out_specs=pl.BlockSpec((t, D_MODEL), row),
        out_shape=jax.ShapeDtypeStruct((n, D_MODEL), F32),
        scratch_shapes=[pltpu.VMEM(((t + HALO) * CONV_PITCH, LANES), F32), pltpu.VMEM((t, D_MODEL), F32),
                        pltpu.VMEM((t, D_MODEL), BF16)],
        compiler_params=_params("arbitrary"),
        name="conv_out_prompt",
    )(u, u, x, wdw, bdw, lng, lnb, wout, bout)


def _conv_out_s_kernel(st_ref, u_ref, x_ref, wdw_ref, bdw_ref, lng_ref, lnb_ref, wout_ref, bout_ref, y_ref, ns_ref,
                       wbuf, cbuf):
    bb, nt, _ = u_ref.shape
    win = CONV_STATE + nt
    nj = D_MODEL // LANES
    seq_pitch = win * CONV_PITCH

    def put(b, r0, rows, j, val):
        wbuf[pl.ds((b * win + r0) * CONV_PITCH + j, rows, stride=CONV_PITCH), :] = val

    for b in range(bb):
        for j in range(nj):
            lanes = slice(j * LANES, (j + 1) * LANES)
            for r0 in range(0, CONV_STATE, 8):
                rows = min(8, CONV_STATE - r0)
                put(b, r0, rows, j, st_ref[b, r0:r0 + rows, lanes])
            put(b, CONV_STATE, nt, j, u_ref[b, :, lanes])

    for b in range(bb):
        for r0 in range(0, CONV_STATE, 8):
            rows = min(8, CONV_STATE - r0)
            for j in range(nj):
                src = (b * win + nt + r0) * CONV_PITCH + j
                ns_ref[b, r0:r0 + rows, j * LANES:(j + 1) * LANES] = wbuf[pl.ds(src, rows, stride=CONV_PITCH), :]

    for t in range(nt):
        for b0 in range(0, bb, 8):
            for j in range(nj):
                lanes = slice(j * LANES, (j + 1) * LANES)
                acc = None
                for k in range(CONV_WIDTH):
                    start = (b0 * win + t + k) * CONV_PITCH + j
                    term = wdw_ref[k:k + 1, lanes] * wbuf[pl.ds(start, 8, stride=seq_pitch), :]
                    acc = term if acc is None else acc + term
                cbuf[t * bb + b0:t * bb + b0 + 8, lanes] = acc
    c = cbuf[...] + bdw_ref[...]
    h = _ln_silu(c, lng_ref[...], lnb_ref[...]).astype(BF16)
    y = _mm(h, wout_ref[...]) + bout_ref[...]
    for t in range(nt):
        y_ref[t] = x_ref[t] + y[t * bb:(t + 1) * bb]


def _conv_out_s(state, u, x, wdw, bdw, lng, lnb, wout, bout):
    b, nt, _ = u.shape
    bb = 16 if b % 16 == 0 else b
    assert bb % 8 == 0
    win = CONV_STATE + nt
    seq = lambda i: (i, 0, 0)
    tm = lambda i: (0, i, 0)
    return pl.pallas_call(
        _conv_out_s_kernel,
        grid=(b // bb,),
        in_specs=[pl.BlockSpec((bb, CONV_STATE, D_MODEL), seq), pl.BlockSpec((bb, nt, D_MODEL), seq),
                  pl.BlockSpec((nt, bb, D_MODEL), tm),
                  _full(wdw.shape), _full(bdw.shape), _full(lng.shape), _full(lnb.shape),
                  _full(wout.shape), _full(bout.shape)],
        out_specs=[pl.BlockSpec((nt, bb, D_MODEL), tm), pl.BlockSpec((bb, CONV_STATE, D_MODEL), seq)],
        out_shape=[jax.ShapeDtypeStruct((nt, b, D_MODEL), F32), jax.ShapeDtypeStruct((b, CONV_STATE, D_MODEL), F32)],
        scratch_shapes=[pltpu.VMEM((bb * win * CONV_PITCH, LANES), F32), pltpu.VMEM((nt * bb, D_MODEL), F32)],
        compiler_params=_params("arbitrary"),
        name="conv_out_sample",
    )(state, u, x, wdw, bdw, lng, lnb, wout, bout)


def _route(lg):
    big = 3.0e38
    lane = lax.broadcasted_iota(jnp.int32, lg.shape, 1)
    lanef = lane.astype(F32)
    is_g = lane < N_EXPERT_GROUPS
    gl = jnp.where(is_g, lg, -big)
    gmax = jnp.max(gl, axis=1, keepdims=True)
    gsum = jnp.sum(jnp.where(is_g, jnp.exp(gl - gmax), 0.0), axis=1, keepdims=True)
    g_w = 1.0 / gsum
    g_idx = jnp.min(jnp.where(gl == gmax, lanef, big), axis=1, keepdims=True)
    rel = lanef - float(EXPERT_LANE0) - g_idx * float(EXPERTS_PER_GROUP)
    in_grp = jnp.where(rel >= 0.0, jnp.where(rel < float(EXPERTS_PER_GROUP), 1.0, 0.0), 0.0) > 0.5
    el = jnp.where(in_grp, lg, -big)
    e1 = jnp.max(el, axis=1, keepdims=True)
    i1 = jnp.min(jnp.where(el == e1, lanef, big), axis=1, keepdims=True)
    el2 = jnp.where(lanef == i1, -big, el)
    e2 = jnp.max(el2, axis=1, keepdims=True)
    i2 = jnp.min(jnp.where(el2 == e2, lanef, big), axis=1, keepdims=True)
    tt = jnp.exp(e2 - e1)
    w1 = g_w / (1.0 + tt)
    w2 = g_w * tt / (1.0 + tt)
    base = float(EXPERT_LANE0) + g_idx * float(EXPERTS_PER_GROUP)
    a = jnp.minimum(i1, i2) - base
    b = jnp.maximum(i1, i2) - base
    pair = a * (7.0 - a) * 0.5 + (b - a - 1.0)
    first_is_lo = i1 < i2
    return (g_idx * float(PAIRS_PER_GROUP) + pair, jnp.where(first_is_lo, w1, w2), jnp.where(first_is_lo, w2, w1))


def _to_row_linear(dst_ref, src_ref, n_tiles):
    def body(g, carry):
        r = pl.multiple_of(g * 8, 8)
        for j in range(n_tiles):
            dst_ref[pl.ds(r * ROW_PITCH + j, 8, stride=ROW_PITCH), :] = src_ref[pl.ds(r, 8), j * LANES:(j + 1) * LANES]
        return carry

    lax.fori_loop(0, src_ref.shape[0] // 8, body, 0, unroll=LOOP_UNROLL)


def _from_row_linear(dst_ref, src_ref, n_tiles, pitch):
    def body(g, carry):
        r = pl.multiple_of(g * 8, 8)
        for j in range(n_tiles):
            dst_ref[pl.ds(r, 8), j * LANES:(j + 1) * LANES] = src_ref[pl.ds(r * pitch + j, 8, stride=pitch), :]
        return carry

    lax.fori_loop(0, dst_ref.shape[0] // 8, body, 0, unroll=LOOP_UNROLL)


def _pick(i, n_prompt_tiles, prompt_ref, sample_ref):
    return jnp.where(i < n_prompt_tiles, prompt_ref[...], sample_ref[...])


def _route_kernel(n_prompt_tiles, yp_ref, ys_ref, g_ref, wr_ref, br_ref, tri_ref, pay_ref, meta_ref, cnt_ref, carry):
    i = pl.program_id(0)

    @pl.when(i == 0)
    def _():
        carry[...] = jnp.zeros_like(carry)

    xf = _rms(_pick(i, n_prompt_tiles, yp_ref, ys_ref), g_ref[...])
    x_hi, x_lo = _split_bf16(xf)
    part = _mm(x_hi, wr_ref[...]) + _mm(x_lo, wr_ref[...])
    logits = part[:, :ROUTER_LANES] + part[:, ROUTER_LANES:] + br_ref[...]
    bucket, w_lo, w_hi = _route(logits)
    lane = lax.broadcasted_iota(jnp.int32, logits.shape, 1)
    onehot = jnp.where(lane.astype(F32) == bucket, 1.0, 0.0)
    before = _mm(tri_ref[...], onehot.astype(BF16)) + carry[...]
    rank = jnp.sum(onehot * before, axis=1, keepdims=True)
    carry[...] += jnp.sum(onehot, axis=0, keepdims=True)
    cnt_ref[...] = carry[...]
    meta = jnp.where(lane == 0, bucket, jnp.where(lane == 1, rank, 0.0))
    meta_ref[...] = jnp.transpose(meta)[:8, :].astype(jnp.int32)

    gates = jnp.where(lane == 0, w_lo, jnp.where(lane == 1, w_hi, 0.0))
    nj = D_MODEL // LANES
    for r in range(0, xf.shape[0], 8):
        for j in range(nj):
            pay_ref[pl.ds(r * ROW_PITCH + j, 8, stride=ROW_PITCH), :] = xf[r:r + 8, j * LANES:(j + 1) * LANES]
        pay_ref[pl.ds(r * ROW_PITCH + nj, 8, stride=ROW_PITCH), :] = gates[r:r + 8, :]


def _route_call(y_p, y_s, g, wr, br, tri):
    t = y_s.shape[0]
    npt = y_p.shape[0] // t
    n = y_p.shape[0] + t
    pidx = lambda i: (jnp.minimum(i, npt - 1), 0)
    return pl.pallas_call(
        functools.partial(_route_kernel, npt),
        grid=(npt + 1,),
        in_specs=[pl.BlockSpec((t, D_MODEL), pidx), _full(y_s.shape), _full(g.shape), _full(wr.shape),
                  _full(br.shape), _full(tri.shape)],
        out_specs=[pl.BlockSpec((t * ROW_PITCH, LANES), lambda i: (i, 0)),
                   pl.BlockSpec((8, t), lambda i: (0, i)), _full((1, ROUTER_LANES))],
        out_shape=[jax.ShapeDtypeStruct((n * ROW_PITCH, LANES), F32),
                   jax.ShapeDtypeStruct((8, n), jnp.int32), jax.ShapeDtypeStruct((1, ROUTER_LANES), F32)],
        scratch_shapes=[pltpu.VMEM((1, ROUTER_LANES), F32)],
        compiler_params=_params("arbitrary"),
        name="route",
    )(y_p, y_s, g, wr, br, tri)


def _permute_kernel(pos_ref, zstart_ref, nz_ref, used_ref, src_ref, dst_ref, zbuf, ring, zsem, in_sems, out_sems):
    zbuf[...] = jnp.zeros_like(zbuf)

    def zero_tile(start):
        return pltpu.make_async_copy(zbuf, dst_ref.at[pl.ds(pl.multiple_of(start * ROW_PITCH, 8), TM * ROW_PITCH)], zsem)

    for b in range(N_BUCKETS):
        @pl.when(nz_ref[b] > 0)
        def _():
            zero_tile(zstart_ref[b]).start()
    n_tiles = dst_ref.shape[0] // (TM * ROW_PITCH)

    def start_unused(i, carry):
        zero_tile(i * TM).start()
        return carry

    lax.fori_loop(used_ref[0], n_tiles, start_unused, 0)
    for b in range(N_BUCKETS):
        @pl.when(nz_ref[b] > 0)
        def _():
            zero_tile(0).wait()

    def wait_unused(i, carry):
        zero_tile(0).wait()
        return carry

    lax.fori_loop(used_ref[0], n_tiles, wait_unused, 0)

    chunk_rows = DMA_CHUNK * ROW_PITCH
    n_chunks = src_ref.shape[0] // chunk_rows

    def fetch(c, slot):
        return pltpu.make_async_copy(src_ref.at[pl.ds(pl.multiple_of(c * chunk_rows, 8), chunk_rows)], ring.at[slot],
                                     in_sems.at[slot])

    def drain(slot):
        pltpu.make_async_copy(ring.at[slot], dst_ref.at[pl.ds(0, chunk_rows)], out_sems.at[slot]).wait()

    fetch(0, 0).start()

    def step(c, carry):
        slot = lax.rem(c, RING)
        nxt = lax.rem(c + 1, RING)
        fetch(c, slot).wait()

        @pl.when(c >= RING - 1)
        def _():
            drain(nxt)

        @pl.when(c + 1 < n_chunks)
        def _():
            fetch(c + 1, nxt).start()

        def send(jj, carry2):
            row = pos_ref[c * DMA_CHUNK + jj] * ROW_PITCH
            pltpu.make_async_copy(ring.at[slot, pl.ds(jj * ROW_PITCH, ROW_PITCH)], dst_ref.at[pl.ds(row, ROW_PITCH)],
                                  out_sems.at[slot]).start()
            return carry2

        lax.fori_loop(0, DMA_CHUNK, send, 0, unroll=8)
        return carry

    lax.fori_loop(0, n_chunks, step, 0)
    for c in range(n_chunks - (RING - 1), n_chunks):
        drain(c % RING)


def _permute(pos, zstart, nz, used, payload, n_rows_out):
    assert payload.shape[0] // (DMA_CHUNK * ROW_PITCH) >= RING
    return pl.pallas_call(
        _permute_kernel,
        grid_spec=pltpu.PrefetchScalarGridSpec(
            num_scalar_prefetch=4, grid=(1,),
            in_specs=[pl.BlockSpec(memory_space=pl.ANY)],
            out_specs=pl.BlockSpec(memory_space=pl.ANY),
            scratch_shapes=[pltpu.VMEM((TM * ROW_PITCH, LANES), payload.dtype),
                            pltpu.VMEM((RING, DMA_CHUNK * ROW_PITCH, LANES), payload.dtype),
                            pltpu.SemaphoreType.DMA(()), pltpu.SemaphoreType.DMA((RING,)),
                            pltpu.SemaphoreType.DMA((RING,))]),
        out_shape=jax.ShapeDtypeStruct((n_rows_out * ROW_PITCH, LANES), payload.dtype),
        compiler_params=_params("arbitrary"),
        name="permute",
    )(pos, zstart, nz, used, payload)


def _silu(x):
    return x * _sigmoid(x)


def _experts_kernel(tix_ref, lo_ref, hi_ref, valid_ref, fresh_ref, xs_ref, wgl_ref, wgh_ref, wul_ref, wuh_ref,
                    wdl_ref, wdh_ref, ys_ref, wup_s, wdn_s, xbuf, ybuf):
    i = pl.program_id(0)

    @pl.when(fresh_ref[i] > 0)
    def _():
        for s, ref in enumerate((wgl_ref, wgh_ref, wul_ref, wuh_ref)):
            wup_s[s] = ref[0, 0].astype(BF16)
        for s, ref in enumerate((wdl_ref, wdh_ref)):
            wdn_s[s] = ref[0, 0].astype(BF16)

    @pl.when(valid_ref[i] == 0)
    def _():
        ys_ref[...] = jnp.zeros_like(ys_ref)

    @pl.when(valid_ref[i] > 0)
    def _():
        _from_row_linear(xbuf, xs_ref, ROW_PITCH, ROW_PITCH)
        x = xbuf[:, :D_MODEL].astype(BF16)
        gates = xbuf[:, D_MODEL:]
        h_lo = _silu(_mm(x, wup_s[0])) * _mm(x, wup_s[2]) * gates[:, 0:1]
        h_hi = _silu(_mm(x, wup_s[1])) * _mm(x, wup_s[3]) * gates[:, 1:2]
        ybuf[:, :D_MODEL] = _mm(h_lo.astype(BF16), wdn_s[0]) + _mm(h_hi.astype(BF16), wdn_s[1])
        ybuf[:, D_MODEL:] = jnp.zeros((TM, LANES), F32)
        _to_row_linear(ys_ref, ybuf, ROW_PITCH)


def _experts(layer, tix, lo, hi, valid, fresh, xs, wg, wu, wd):
    n_tiles = xs.shape[0] // (TM * ROW_PITCH)
    row = lambda i, tix, lo, hi, valid, fresh: (tix[i], 0)
    row_out = lambda i, tix, lo, hi, valid, fresh: (i, 0)
    e_lo = lambda i, tix, lo, hi, valid, fresh: (layer, lo[i], 0, 0)
    e_hi = lambda i, tix, lo, hi, valid, fresh: (layer, hi[i], 0, 0)
    up = pl.BlockSpec((1, 1, D_MODEL, D_EXPERT), e_lo), pl.BlockSpec((1, 1, D_MODEL, D_EXPERT), e_hi)
    down = pl.BlockSpec((1, 1, D_EXPERT, D_MODEL), e_lo), pl.BlockSpec((1, 1, D_EXPERT, D_MODEL), e_hi)
    return pl.pallas_call(
        _experts_kernel,
        grid_spec=pltpu.PrefetchScalarGridSpec(
            num_scalar_prefetch=5, grid=(n_tiles,),
            in_specs=[pl.BlockSpec((TM * ROW_PITCH, LANES), row), *up, *up, *down],
            out_specs=pl.BlockSpec((TM * ROW_PITCH, LANES), row_out),
            scratch_shapes=[pltpu.VMEM((4, D_MODEL, D_EXPERT), BF16), pltpu.VMEM((2, D_EXPERT, D_MODEL), BF16),
                            pltpu.VMEM((TM, PAY_WIDTH), F32), pltpu.VMEM((TM, PAY_WIDTH), F32)]),
        out_shape=jax.ShapeDtypeStruct(xs.shape, F32),
        compiler_params=_params("arbitrary"),
        name="experts",
    )(tix, lo, hi, valid, fresh, xs, wg, wg, wu, wu, wd, wd)


def _ple_kernel(n_prompt_tiles, pos_ref, yp_ref, ys_ref, moe_ref, pp_ref, ps_ref, g_ref, wg_ref, wp_ref, op_ref, os_ref,
                mring, mbuf, sems):
    i = pl.program_id(0)
    t = mbuf.shape[0]
    w = D_MODEL // LANES

    def gather(tile, slot):
        def body(jj, carry):
            row = pos_ref[tile * t + jj] * ROW_PITCH
            pltpu.make_async_copy(moe_ref.at[pl.ds(row, w)], mring.at[slot, pl.ds(jj * w, w)], sems.at[slot]).start()
            return carry

        lax.fori_loop(0, t, body, 0, unroll=8)

    def wait_tile(slot):
        pltpu.make_async_copy(moe_ref.at[pl.ds(0, t * w)], mring.at[slot], sems.at[slot]).wait()

    @pl.when(i == 0)
    def _():
        gather(0, 0)
        gather(1, 1)

    slot = lax.rem(i, PLE_RING)
    nslot = lax.rem(i + 2, PLE_RING)
    wait_tile(slot)
    _from_row_linear(mbuf, mring.at[slot], w, w)
    y2 = _pick(i, n_prompt_tiles, yp_ref, ys_ref) + mbuf[...]
    hn = _rms(y2, g_ref[...]).astype(BF16)
    gt = _sigmoid(_mm(hn, wg_ref[...]))
    pr = _mm(jnp.where(i < n_prompt_tiles, pp_ref[0], ps_ref[...]).astype(BF16), wp_ref[...])
    y3 = y2 + gt * pr
    nxt = jnp.minimum(i + 2, n_prompt_tiles) * t
    for jj in range(t):
        row = pos_ref[nxt + jj] * ROW_PITCH
        pltpu.make_async_copy(moe_ref.at[pl.ds(row, w)], mring.at[nslot, pl.ds(jj * w, w)], sems.at[nslot]).start()

    @pl.when(i < n_prompt_tiles)
    def _():
        op_ref[...] = y3

    @pl.when(i >= n_prompt_tiles)
    def _():
        os_ref[...] = y3
        wait_tile(lax.rem(i + 1, PLE_RING))
        wait_tile(nslot)


def _ple(layer, pos, y_p, y_s, ys_sorted, p_p, p_s, g, wg, wp):
    t = y_s.shape[0]
    npt = y_p.shape[0] // t
    w = D_MODEL // LANES
    pidx = lambda i, pos: (jnp.minimum(i, npt - 1), 0)
    full = lambda shape: pl.BlockSpec(shape, lambda i, pos: (0,) * len(shape))
    return pl.pallas_call(
        functools.partial(_ple_kernel, npt),
        grid_spec=pltpu.PrefetchScalarGridSpec(
            num_scalar_prefetch=1, grid=(npt + 1,),
            in_specs=[pl.BlockSpec((t, D_MODEL), pidx), full(y_s.shape), pl.BlockSpec(memory_space=pl.ANY),
                      pl.BlockSpec((1, t, PLE_DIM), lambda i, pos: (layer, jnp.minimum(i, npt - 1), 0)),
                      full(p_s.shape), full(g.shape), full(wg.shape), full(wp.shape)],
            out_specs=[pl.BlockSpec((t, D_MODEL), pidx), full(y_s.shape)],
            scratch_shapes=[pltpu.VMEM((PLE_RING, t * w, LANES), F32), pltpu.VMEM((t, D_MODEL), F32),
                            pltpu.SemaphoreType.DMA((PLE_RING,))]),
        out_shape=[jax.ShapeDtypeStruct(y_p.shape, F32), jax.ShapeDtypeStruct(y_s.shape, F32)],
        compiler_params=_params("arbitrary"),
        name="ple",
    )(pos, y_p, y_s, ys_sorted, p_p, p_s, g, wg, wp)


def _bucket_experts():
    pairs = [(a, b) for a in range(EXPERTS_PER_GROUP) for b in range(a + 1, EXPERTS_PER_GROUP)]
    lo = [g * EXPERTS_PER_GROUP + a for g in range(N_EXPERT_GROUPS) for a, _ in pairs]
    hi = [g * EXPERTS_PER_GROUP + b for g in range(N_EXPERT_GROUPS) for _, b in pairs]
    return jnp.array(lo, jnp.int32), jnp.array(hi, jnp.int32)


def _moe_ple(layer, y_p, y_s, gffn, wr, br, wg, wu, wd, p_p, p_s, gple, pwg, pwp):
    t = y_s.shape[0]
    n = y_p.shape[0] + t
    tri = jnp.asarray(np.tri(t, t, -1, dtype=np.float32), BF16)
    payload, meta, counts = _route_call(y_p, y_s, gffn, wr, br, tri)

    bucket, rank = meta[0], meta[1]
    cnt = counts[0, :N_BUCKETS].astype(jnp.int32)
    padded = (cnt + TM - 1) // TM * TM
    ends = jnp.cumsum(padded)
    starts = ends - padded
    in_bucket = bucket[:, None] == jnp.arange(N_BUCKETS, dtype=jnp.int32)[None, :]
    pos = rank + jnp.sum(jnp.where(in_bucket, starts[None, :], 0), axis=1)
    n_rows = n + N_BUCKETS * TM
    n_tiles = n_rows // TM
    tile_start = jnp.arange(n_tiles, dtype=jnp.int32) * TM
    valid = tile_start < ends[-1]
    used = (ends[-1] // TM).astype(jnp.int32)
    tix = jnp.arange(n_tiles, dtype=jnp.int32)
    tile_bucket = jnp.sum((ends[None, :] <= (jnp.minimum(tix, used - 1) * TM)[:, None]).astype(jnp.int32), axis=1)
    prev_bucket = jnp.concatenate([jnp.full((1,), -1, jnp.int32), tile_bucket[:-1]])
    fresh = valid & (tile_bucket != prev_bucket)
    first_tile = jnp.where(fresh, tix, n_tiles)
    next_first = lax.cummin(jnp.concatenate([first_tile[1:], jnp.full((1,), n_tiles, jnp.int32)]), reverse=True)
    next_bucket = jnp.where(next_first < n_tiles, jnp.take(tile_bucket, jnp.minimum(next_first, n_tiles - 1)),
                            tile_bucket)
    resident = jnp.where(fresh, tile_bucket, next_bucket)
    lo_tab, hi_tab = _bucket_experts()
    lo = jnp.take(lo_tab, resident)
    hi = jnp.take(hi_tab, resident)

    xs = _permute(pos, (ends - TM).astype(jnp.int32), (cnt > 0).astype(jnp.int32), used.reshape(1), payload, n_rows)
    ys = _experts(layer, jnp.minimum(tix, used - 1), lo, hi, valid.astype(jnp.int32), fresh.astype(jnp.int32), xs, wg,
                  wu, wd)
    return _ple(layer, pos, y_p, y_s, ys, p_p, p_s, gple, pwg, pwp)


def _split_bf16(a):
    hi = a.astype(BF16)
    lo = (a - hi.astype(F32)).astype(BF16)
    return hi, lo


def _head_norm(a, ind_ref, indt_ref, gain):
    hi, lo = _split_bf16(a * a)
    ss = _mm(hi, ind_ref[...]) + _mm(lo, ind_ref[...])
    inv = lax.rsqrt(ss * (1.0 / HEAD_DIM) + EPS)
    ihi, ilo = _split_bf16(inv)
    invb = _mm(ihi, indt_ref[...]) + _mm(ilo, indt_ref[...])
    return a * invb * gain


def _qkv_kernel(x_ref, g_ref, w_ref, iq_ref, iqt_ref, ik_ref, ikt_ref, qg_ref, kg_ref, q_ref, k_ref, v_ref):
    h = _rms(x_ref[...], g_ref[...]).astype(BF16)
    qkv = _mm(h, w_ref[...])
    nq = N_HEADS * HEAD_DIM
    q = _head_norm(qkv[:, :nq], iq_ref, iqt_ref, qg_ref[...])
    k = _head_norm(qkv[:, nq:nq + KV_DIM], ik_ref, ikt_ref, kg_ref[...])
    q_ref[...] = (q * (HEAD_DIM ** -0.5)).astype(BF16)
    k_ref[...] = k
    v_ref[...] = qkv[:, nq + KV_DIM:]


def _qkv(x, g, w, iq, iqt, ik, ikt, qg, kg):
    n = x.shape[0]
    t = _row_tile(n)
    row = lambda i: (i, 0)
    return pl.pallas_call(
        _qkv_kernel,
        grid=(n // t,),
        in_specs=[pl.BlockSpec((t, D_MODEL), row)] + [_full(a.shape) for a in (g, w, iq, iqt, ik, ikt, qg, kg)],
        out_specs=[pl.BlockSpec((t, N_HEADS * HEAD_DIM), row), pl.BlockSpec((t, KV_DIM), row),
                   pl.BlockSpec((t, KV_DIM), row)],
        out_shape=[jax.ShapeDtypeStruct((n, N_HEADS * HEAD_DIM), BF16), jax.ShapeDtypeStruct((n, KV_DIM), F32),
                   jax.ShapeDtypeStruct((n, KV_DIM), F32)],
        compiler_params=_params("arbitrary"),
        name="qkv",
    )(x, g, w, iq, iqt, ik, ikt, qg, kg)


def _dup_heads(a):
    out = []
    for s in range(KV_DIM // LANES):
        sl = a[:, s * LANES:(s + 1) * LANES]
        sw = pltpu.roll(sl, HEAD_DIM, axis=1)
        low = lax.broadcasted_iota(jnp.int32, sl.shape, 1) < HEAD_DIM
        out.append(jnp.where(low, sl, sw))
        out.append(jnp.where(low, sw, sl))
    return jnp.concatenate(out, axis=1).astype(BF16)


def _attend(q_rows, k2, v2, bias_of, sink_of, extra_mask):
    m_rows = q_rows.shape[0]
    low_q = lax.broadcasted_iota(jnp.int32, (m_rows, LANES), 1) < HEAD_DIM
    low_k = lax.broadcasted_iota(jnp.int32, (k2.shape[0], LANES), 1) < HEAD_DIM
    zero_q = jnp.zeros((m_rows, LANES), BF16)
    zero_k = jnp.zeros((k2.shape[0], LANES), BF16)
    slabs = []
    for g in range(N_KV_HEADS):
        kg = k2[:, g * LANES:(g + 1) * LANES]
        vg = v2[:, g * LANES:(g + 1) * LANES]
        v_lo = jnp.where(low_k, vg, zero_k)
        v_hi = jnp.where(low_k, zero_k, vg)
        lhs = []
        for a in range(GROUP):
            h = g * GROUP + a
            qs = q_rows[:, (h // 2) * LANES:(h // 2 + 1) * LANES]
            lhs.append(jnp.where(low_q, qs, zero_q) if h % 2 == 0 else jnp.where(low_q, zero_q, qs))
        s = lax.dot_general(jnp.concatenate(lhs, axis=0), kg, (((1,), (1,)), ((), ())), preferred_element_type=F32)
        probs, rinv = [], []
        for a in range(GROUP):
            h = g * GROUP + a
            sa = s[a * m_rows:(a + 1) * m_rows] + bias_of(h)
            if extra_mask is not None:
                sa = jnp.where(extra_mask, NEG_INF, sa)
            sink = sink_of(h)
            m = jnp.maximum(jnp.max(sa, axis=1, keepdims=True), sink)
            p = jnp.exp(sa - m)
            den = jnp.sum(p, axis=1, keepdims=True) + jnp.exp(sink - m)
            probs.append(p.astype(BF16))
            rinv.append(1.0 / den)
        for sp in range(GROUP // 2):
            o = _mm(probs[2 * sp], v_lo) + _mm(probs[2 * sp + 1], v_hi)
            slabs.append(o * jnp.where(low_q, rinv[2 * sp], rinv[2 * sp + 1]))
    return jnp.concatenate(slabs, axis=1)


def _attn_p_kernel(sink_ref, q_ref, k_ref, v_ref, x_ref, bias_ref, wo_ref, y_ref, kbuf, vbuf, obuf):
    t = q_ref.shape[0]
    i = pl.program_id(0)

    @pl.when(i == 0)
    def _():
        kbuf[0:WINDOW, :] = jnp.zeros((WINDOW, 2 * KV_DIM), BF16)
        vbuf[0:WINDOW, :] = jnp.zeros((WINDOW, 2 * KV_DIM), BF16)

    @pl.when(i > 0)
    def _():
        kbuf[0:WINDOW, :] = kbuf[t:t + WINDOW, :]
        vbuf[0:WINDOW, :] = vbuf[t:t + WINDOW, :]

    kbuf[WINDOW:, :] = _dup_heads(k_ref[...])
    vbuf[WINDOW:, :] = _dup_heads(v_ref[...])
    col = lax.broadcasted_iota(jnp.int32, (WINDOW, 2 * WINDOW), 1)
    for j in range(t // WINDOW):
        rows = slice(j * WINDOW, (j + 1) * WINDOW)
        keys = slice(j * WINDOW, (j + 2) * WINDOW)
        extra = jnp.logical_and(i == 0, col < WINDOW) if j == 0 else None
        o = _attend(q_ref[rows, :], kbuf[keys, :], vbuf[keys, :], lambda h: bias_ref[h], lambda h: sink_ref[h], extra)
        obuf[rows, :] = o.astype(BF16)
    y_ref[...] = x_ref[...] + _mm(obuf[...], wo_ref[...])


def _attn_p(sinks, q, k, v, x, bias, wo):
    n = q.shape[0]
    t = _row_tile(n)
    row = lambda i: (i, 0)
    return pl.pallas_call(
        _attn_p_kernel,
        grid=(n // t,),
        in_specs=[pl.BlockSpec(memory_space=pltpu.SMEM),
                  pl.BlockSpec((t, N_HEADS * HEAD_DIM), row), pl.BlockSpec((t, KV_DIM), row),
                  pl.BlockSpec((t, KV_DIM), row), pl.BlockSpec((t, D_MODEL), row),
                  _full(bias.shape), _full(wo.shape)],
        out_specs=pl.BlockSpec((t, D_MODEL), row),
        out_shape=jax.ShapeDtypeStruct((n, D_MODEL), F32),
        scratch_shapes=[pltpu.VMEM((t + WINDOW, 2 * KV_DIM), BF16), pltpu.VMEM((t + WINDOW, 2 * KV_DIM), BF16),
                        pltpu.VMEM((t, N_HEADS * HEAD_DIM), BF16)],
        compiler_params=_params("arbitrary"),
        name="attn_prompt",
    )(sinks, q, k, v, x, bias, wo)


def _attn_s_kernel(sink_ref, q_ref, k_ref, v_ref, ck_ref, cv_ref, bias_ref, o_ref, nk_ref, nv_ref, shift_buf):
    pad = bias_ref.shape[2] - ck_ref.shape[0] - k_ref.shape[0]
    zpad = jnp.zeros((pad, KV_DIM), F32)
    k2 = _dup_heads(jnp.concatenate([ck_ref[...], k_ref[...], zpad], axis=0))
    v2 = _dup_heads(jnp.concatenate([cv_ref[...], v_ref[...], zpad], axis=0))
    o = _attend(q_ref[...], k2, v2, lambda h: bias_ref[h], lambda h: sink_ref[h], None)
    o_ref[...] = o.astype(BF16)
    t_new = k_ref.shape[0] // SAMPLE_SEQS
    for new_ref, old_ref, add_ref in ((nk_ref, ck_ref, k_ref), (nv_ref, cv_ref, v_ref)):
        for b in range(SAMPLE_SEQS):
            r = b * WINDOW
            shift_buf[0:WINDOW - t_new, :] = old_ref[r + t_new:r + WINDOW, :]
            shift_buf[WINDOW - t_new:WINDOW, :] = add_ref[b * t_new:(b + 1) * t_new, :]
            new_ref[b] = jnp.transpose(shift_buf[...])


def _attn_s(sinks, q, k, v, ck, cv, bias, t_new):
    n = q.shape[0]
    rows = SAMPLE_SEQS * t_new
    crow = SAMPLE_SEQS * WINDOW
    row = lambda i: (i, 0)
    return pl.pallas_call(
        _attn_s_kernel,
        grid=(n // rows,),
        in_specs=[pl.BlockSpec(memory_space=pltpu.SMEM),
                  pl.BlockSpec((rows, N_HEADS * HEAD_DIM), row), pl.BlockSpec((rows, KV_DIM), row),
                  pl.BlockSpec((rows, KV_DIM), row), pl.BlockSpec((crow, KV_DIM), row),
                  pl.BlockSpec((crow, KV_DIM), row), _full(bias.shape)],
        out_specs=[pl.BlockSpec((rows, N_HEADS * HEAD_DIM), row),
                   pl.BlockSpec((SAMPLE_SEQS, KV_DIM, WINDOW), lambda i: (i, 0, 0)),
                   pl.BlockSpec((SAMPLE_SEQS, KV_DIM, WINDOW), lambda i: (i, 0, 0))],
        out_shape=[jax.ShapeDtypeStruct((n, N_HEADS * HEAD_DIM), BF16),
                   jax.ShapeDtypeStruct((ck.shape[0] // WINDOW, KV_DIM, WINDOW), F32),
                   jax.ShapeDtypeStruct((cv.shape[0] // WINDOW, KV_DIM, WINDOW), F32)],
        scratch_shapes=[pltpu.VMEM((WINDOW, KV_DIM), F32)],
        compiler_params=_params("arbitrary"),
        name="attn_sample",
    )(sinks, q, k, v, ck, cv, bias)


def _proj_res_kernel(o_ref, x_ref, w_ref, y_ref):
    y_ref[...] = x_ref[...] + _mm(o_ref[...], w_ref[...])


def _proj_res(o, x, w):
    n = o.shape[0]
    t = _row_tile(n)
    row = lambda i: (i, 0)
    return pl.pallas_call(
        _proj_res_kernel,
        grid=(n // t,),
        in_specs=[pl.BlockSpec((t, o.shape[1]), row), pl.BlockSpec((t, D_MODEL), row), _full(w.shape)],
        out_specs=pl.BlockSpec((t, D_MODEL), row),
        out_shape=jax.ShapeDtypeStruct((n, D_MODEL), F32),
        compiler_params=_params("arbitrary"),
        name="proj_res",
    )(o, x, w)


def _alibi_slopes():
    return np.exp2(-8.0 * np.arange(1, N_HEADS + 1, dtype=np.float64) / N_HEADS).astype(np.float32)


def _band_bias(dist, allowed):
    b = -(_alibi_slopes()[:, None, None] * dist.astype(np.float32)[None])
    return jnp.asarray(np.where(allowed[None], b, np.float32(NEG_INF)).astype(np.float32))


def _prompt_bias():
    dist = WINDOW + np.arange(WINDOW)[:, None] - np.arange(2 * WINDOW)[None, :]
    return _band_bias(dist, (dist >= 0) & (dist <= WINDOW))


def _sample_bias(t_new, n_cols):
    c = np.arange(n_cols)
    n_cache = SAMPLE_SEQS * WINDOW
    n_new = SAMPLE_SEQS * t_new
    is_cache = c < n_cache
    is_new = (c >= n_cache) & (c < n_cache + n_new)
    seq_c = np.where(is_cache, c // WINDOW, (c - n_cache) // t_new)
    pos_c = np.where(is_cache, c % WINDOW, WINDOW + (c - n_cache) % t_new)
    r = np.arange(n_new)
    seq_r, tok_r = r // t_new, r % t_new
    dist = WINDOW + tok_r[:, None] - pos_c[None, :]
    allowed = (seq_r[:, None] == seq_c[None, :]) & (is_cache | is_new)[None, :] & (dist >= 0) & (dist <= WINDOW)
    return _band_bias(dist, allowed)


def _head_indicator(n_heads):
    ch = np.arange(n_heads * HEAD_DIM) // HEAD_DIM
    ind = (ch[:, None] == np.arange(LANES)[None, :]).astype(np.float32)
    return jnp.asarray(ind, BF16), jnp.asarray(ind.T, BF16)


def kernel(x_prompt, x_sample, state_conv, cache_k, cache_v, p_prompt, p_sample, norm_mix, norm_ffn, norm_ple,
           conv_w_in, conv_b_in, conv_w_dw, conv_b_dw, conv_ln_g, conv_ln_b, conv_w_out, conv_b_out, attn_w_qkv,
           attn_q_norm, attn_k_norm, attn_sinks, attn_w_o, moe_w_rg, moe_b_rg, moe_w_re, moe_b_re, moe_w_gate,
           moe_w_up, moe_w_down, ple_w_gate, ple_w_proj):
    bp, seq, d = x_prompt.shape
    bs, t_new, _ = x_sample.shape
    assert bp == 1 and d == D_MODEL and seq % WINDOW == 0 and bs % SAMPLE_SEQS == 0
    assert seq % (bs * t_new) == 0 and (bs * t_new) % DMA_CHUNK == 0
    depth = norm_mix.shape[0]
    row2 = lambda a: a.reshape(1, -1)

    y_p = x_prompt.reshape(seq, d)
    y_s = x_sample.reshape(bs * t_new, d)
    conv_p, conv_s, k_p, v_p, k_s, v_s = [], [], [], [], [], []

    for i in range(depth):
        j = i // 2
        g_mix = row2(norm_mix[i])
        if i % 2 == 0:
            w_in = conv_w_in[j].astype(BF16)
            b_in = row2(conv_b_in[j])
            tail = (conv_w_dw[j], row2(conv_b_dw[j]), row2(conv_ln_g[j]), row2(conv_ln_b[j]),
                    conv_w_out[j].astype(BF16), row2(conv_b_out[j]))
            u_p = _conv_in(y_p, g_mix, w_in, b_in)
            u_s = _conv_in(y_s, g_mix, w_in, b_in)
            conv_p.append(u_p[seq - CONV_STATE:].reshape(1, CONV_STATE, d))
            y_p = _conv_out_p(u_p, y_p, *tail)
            ys_t, new_state = _conv_out_s(state_conv[j], u_s.reshape(bs, t_new, d),
                                          y_s.reshape(bs, t_new, d).transpose(1, 0, 2), *tail)
            conv_s.append(new_state)
            y_s = ys_t.transpose(1, 0, 2).reshape(bs * t_new, d)
        else:
            w_qkv = attn_w_qkv[j].astype(BF16)
            w_o = attn_w_o[j].astype(BF16)
            iq, iqt = _head_indicator(N_HEADS)
            ik, ikt = _head_indicator(N_KV_HEADS)
            qg = row2(jnp.tile(attn_q_norm[j], N_HEADS))
            kg = row2(jnp.tile(attn_k_norm[j], N_KV_HEADS))
            sinks = attn_sinks[j]
            q1, k1, v1 = _qkv(y_p, g_mix, w_qkv, iq, iqt, ik, ikt, qg, kg)
            q2, k2, v2 = _qkv(y_s, g_mix, w_qkv, iq, iqt, ik, ikt, qg, kg)
            k_p.append(k1[seq - WINDOW:].reshape(1, WINDOW, N_KV_HEADS, HEAD_DIM))
            v_p.append(v1[seq - WINDOW:].reshape(1, WINDOW, N_KV_HEADS, HEAD_DIM))
            y_p = _attn_p(sinks, q1, k1, v1, y_p, _prompt_bias(), w_o)
            n_cols = -(-(SAMPLE_SEQS * (WINDOW + t_new)) // LANES) * LANES
            o_s, nk, nv = _attn_s(sinks, q2, k2, v2, cache_k[j].reshape(bs * WINDOW, KV_DIM),
                                  cache_v[j].reshape(bs * WINDOW, KV_DIM), _sample_bias(t_new, n_cols), t_new)
            k_s.append(nk.reshape(bs, N_KV_HEADS, HEAD_DIM, WINDOW).transpose(0, 3, 1, 2))
            v_s.append(nv.reshape(bs, N_KV_HEADS, HEAD_DIM, WINDOW).transpose(0, 3, 1, 2))
            y_s = _proj_res(o_s, y_s, w_o)

        w_r = jnp.zeros((d, ROUTER_LANES), F32)
        w_r = w_r.at[:, :N_EXPERT_GROUPS].set(moe_w_rg[i]).at[:, EXPERT_LANE0:EXPERT_LANE0 + N_EXPERTS].set(moe_w_re[i])
        b_r = jnp.zeros((1, ROUTER_LANES), F32)
        b_r = b_r.at[0, :N_EXPERT_GROUPS].set(moe_b_rg[i]).at[0, EXPERT_LANE0:EXPERT_LANE0 + N_EXPERTS].set(moe_b_re[i])
        moe = (row2(norm_ffn[i]), jnp.concatenate(_split_bf16(w_r), axis=1), b_r, moe_w_gate, moe_w_up, moe_w_down)
        ple = (row2(norm_ple[i]), ple_w_gate[i].astype(BF16), ple_w_proj[i].astype(BF16))
        y_p, y_s = _moe_ple(i, y_p, y_s, *moe, p_prompt.reshape(depth, seq, PLE_DIM),
                            p_sample[i].reshape(bs * t_new, PLE_DIM), *ple)

    return (y_p.reshape(1, seq, d), y_s.reshape(bs, t_new, d), jnp.stack(conv_p), jnp.stack(conv_s),
            jnp.stack(k_p), jnp.stack(v_p), jnp.stack(k_s), jnp.stack(v_s))
```

```python
import functools

import numpy as np

import jax
import jax.numpy as jnp
from jax import lax
from jax.experimental import pallas as pl
from jax.experimental.pallas import tpu as pltpu

F32 = jnp.float32
BF16 = jnp.bfloat16

D_MODEL = 1024
PLE_DIM = 256
CONV_WIDTH = 31
CONV_STATE = CONV_WIDTH - 1
N_HEADS = 16
N_KV_HEADS = 4
HEAD_DIM = 64
GROUP = N_HEADS // N_KV_HEADS
WINDOW = 128
KV_DIM = N_KV_HEADS * HEAD_DIM
N_EXPERT_GROUPS = 4
EXPERTS_PER_GROUP = 4
N_EXPERTS = 16
D_EXPERT = 256
EPS = 1e-6
NEG_INF = -1e30

LANES = 128
ROUTER_LANES = LANES
EXPERT_LANE0 = N_EXPERT_GROUPS
HALO = 32
CONV_ROWS = 32
NORM_ROWS = 512
CONV_PITCH = D_MODEL // LANES + 1
SAMPLE_SEQS = 8
PAIRS_PER_GROUP = EXPERTS_PER_GROUP * (EXPERTS_PER_GROUP - 1) // 2
N_BUCKETS = N_EXPERT_GROUPS * PAIRS_PER_GROUP
TM = 256
PAY_WIDTH = D_MODEL + LANES
ROW_PITCH = PAY_WIDTH // LANES
DMA_CHUNK = 512
RING = 3
PLE_RING = 3
VMEM_LIMIT = 48 * 1024 * 1024


def _row_tile(n, big=False):
    if big and n % 1024 == 0:
        return 1024
    return 512 if n % 512 == 0 else n


def _params(*sem):
    return pltpu.CompilerParams(dimension_semantics=sem, vmem_limit_bytes=VMEM_LIMIT)


def _full(shape):
    nd = len(shape)
    return pl.BlockSpec(shape, lambda *_: (0,) * nd)


def _rms(x, g):
    ms = jnp.mean(x * x, axis=-1, keepdims=True)
    return x * lax.rsqrt(ms + EPS) * g


def _sigmoid(x):
    return 1.0 / (1.0 + jnp.exp(-x))


def _mm(a, b):
    return jnp.dot(a, b, preferred_element_type=F32)


def _conv_in_kernel(x_ref, g_ref, w_ref, b_ref, u_ref):
    h = _rms(x_ref[...], g_ref[...]).astype(BF16)
    z = _mm(h, w_ref[...]) + b_ref[...]
    u_ref[...] = z[:, :D_MODEL] * _sigmoid(z[:, D_MODEL:])


def _conv_in(x, g, w, b):
    n = x.shape[0]
    t = _row_tile(n, big=True)
    return pl.pallas_call(
        _conv_in_kernel,
        grid=(n // t,),
        in_specs=[pl.BlockSpec((t, D_MODEL), lambda i: (i, 0)), _full(g.shape), _full(w.shape), _full(b.shape)],
        out_specs=pl.BlockSpec((t, D_MODEL), lambda i: (i, 0)),
        out_shape=jax.ShapeDtypeStruct((n, D_MODEL), F32),
        compiler_params=_params("arbitrary"),
        name="conv_in",
    )(x, g, w, b)


def _ln_silu(c, g, b):
    mu = jnp.mean(c, axis=-1, keepdims=True)
    xc = c - mu
    var = jnp.mean(xc * xc, axis=-1, keepdims=True)
    cn = xc * lax.rsqrt(var + EPS) * g + b
    return cn * _sigmoid(cn)


def _conv_out_p_kernel(u_ref, halo_ref, x_ref, wdw_ref, bdw_ref, lng_ref, lnb_ref, wout_ref, bout_ref, y_ref,
                       ubuf, cbuf, hbuf):
    t = u_ref.shape[0]
    i = pl.program_id(0)
    nj = D_MODEL // LANES

    def put(r, j, val):
        ubuf[pl.ds(r * CONV_PITCH + j, 8, stride=CONV_PITCH), :] = val

    for r in range(0, HALO, 8):
        for j in range(nj):
            put(r, j, jnp.where(i > 0, halo_ref[r:r + 8, j * LANES:(j + 1) * LANES], 0.0))

    def fill(rr, carry):
        r = pl.multiple_of(rr * 8, 8)
        for j in range(nj):
            put(r + HALO, j, u_ref[pl.ds(r, 8), j * LANES:(j + 1) * LANES])
        return carry

    lax.fori_loop(0, t // 8, fill, 0, unroll=4)

    def conv_chunk(rr, carry):
        r0 = pl.multiple_of(rr * CONV_ROWS, CONV_ROWS)
        for j in range(nj):
            lanes = slice(j * LANES, (j + 1) * LANES)
            accs = [None] * (CONV_ROWS // 8)
            for k in range(CONV_WIDTH):
                wk = wdw_ref[k:k + 1, lanes]
                for q in range(CONV_ROWS // 8):
                    r = r0 + (HALO - CONV_STATE + k + 8 * q)
                    term = wk * ubuf[pl.ds(r * CONV_PITCH + j, 8, stride=CONV_PITCH), :]
                    accs[q] = term if k == 0 else accs[q] + term
            for q in range(CONV_ROWS // 8):
                cbuf[pl.ds(r0 + 8 * q, 8), lanes] = accs[q]
        return carry

    lax.fori_loop(0, t // CONV_ROWS, conv_chunk, 0)

    def norm_chunk(rr, carry):
        r0 = pl.multiple_of(rr * NORM_ROWS, NORM_ROWS)
        c = cbuf[pl.ds(r0, NORM_ROWS), :] + bdw_ref[...]
        hbuf[pl.ds(r0, NORM_ROWS), :] = _ln_silu(c, lng_ref[...], lnb_ref[...]).astype(BF16)
        return carry

    lax.fori_loop(0, t // NORM_ROWS, norm_chunk, 0)
    y_ref[...] = x_ref[...] + _mm(hbuf[...], wout_ref[...]) + bout_ref[...]


def _conv_out_p(u, x, wdw, bdw, lng, lnb, wout, bout):
    n = u.shape[0]
    t = _row_tile(n)
    hb = t // HALO
    row = lambda i: (i, 0)
    return pl.pallas_call(
        _conv_out_p_kernel,
        grid=(n // t,),
        in_specs=[pl.BlockSpec((t, D_MODEL), row),
                  pl.BlockSpec((HALO, D_MODEL), lambda i: (jnp.maximum(i * hb - 1, 0), 0)),
                  pl.BlockSpec((t, D_MODEL), row),
                  _full(wdw.shape), _full(bdw.shape), _full(lng.shape), _full(lnb.shape),
                  _full(wout.shape), _full(bout.shape)],
        out_specs=pl.BlockSpec((t, D_MODEL), row),
        out_shape=jax.ShapeDtypeStruct((n, D_MODEL), F32),
        scratch_shapes=[pltpu.VMEM(((t + HALO) * CONV_PITCH, LANES), F32), pltpu.VMEM((t, D_MODEL), F32),
                        pltpu.VMEM((t, D_MODEL), BF16)],
        compiler_params=_params("arbitrary"),
        name="conv_out_prompt",
    )(u, u, x, wdw, bdw, lng, lnb, wout, bout)


def _conv_out_s_kernel(st_ref, u_ref, x_ref, wdw_ref, bdw_ref, lng_ref, lnb_ref, wout_ref, bout_ref, y_ref, ns_ref,
                       wbuf, cbuf):
    bb, nt, _ = u_ref.shape
    win = CONV_STATE + nt
    nj = D_MODEL // LANES
    seq_pitch = win * CONV_PITCH

    def put(b, r0, rows, j, val):
        wbuf[pl.ds((b * win + r0) * CONV_PITCH + j, rows, stride=CONV_PITCH), :] = val

    for b in range(bb):
        for j in range(nj):
            lanes = slice(j * LANES, (j + 1) * LANES)
            for r0 in range(0, CONV_STATE, 8):
                rows = min(8, CONV_STATE - r0)
                put(b, r0, rows, j, st_ref[b, r0:r0 + rows, lanes])
            put(b, CONV_STATE, nt, j, u_ref[b, :, lanes])

    for b in range(bb):
        for r0 in range(0, CONV_STATE, 8):
            rows = min(8, CONV_STATE - r0)
            for j in range(nj):
                src = (b * win + nt + r0) * CONV_PITCH + j
                ns_ref[b, r0:r0 + rows, j * LANES:(j + 1) * LANES] = wbuf[pl.ds(src, rows, stride=CONV_PITCH), :]

    for t in range(nt):
        for b0 in range(0, bb, 8):
            for j in range(nj):
                lanes = slice(j * LANES, (j + 1) * LANES)
                acc = None
                for k in range(CONV_WIDTH):
                    start = (b0 * win + t + k) * CONV_PITCH + j
                    term = wdw_ref[k:k + 1, lanes] * wbuf[pl.ds(start, 8, stride=seq_pitch), :]
                    acc = term if acc is None else acc + term
                cbuf[t * bb + b0:t * bb + b0 + 8, lanes] = acc
    c = cbuf[...] + bdw_ref[...]
    h = _ln_silu(c, lng_ref[...], lnb_ref[...]).astype(BF16)
    y = _mm(h, wout_ref[...]) + bout_ref[...]
    for t in range(nt):
        y_ref[t] = x_ref[t] + y[t * bb:(t + 1) * bb]


def _conv_out_s(state, u, x, wdw, bdw, lng, lnb, wout, bout):
    b, nt, _ = u.shape
    bb = 16 if b % 16 == 0 else b
    assert bb % 8 == 0
    win = CONV_STATE + nt
    seq = lambda i: (i, 0, 0)
    tm = lambda i: (0, i, 0)
    return pl.pallas_call(
        _conv_out_s_kernel,
        grid=(b // bb,),
        in_specs=[pl.BlockSpec((bb, CONV_STATE, D_MODEL), seq), pl.BlockSpec((bb, nt, D_MODEL), seq),
                  pl.BlockSpec((nt, bb, D_MODEL), tm),
                  _full(wdw.shape), _full(bdw.shape), _full(lng.shape), _full(lnb.shape),
                  _full(wout.shape), _full(bout.shape)],
        out_specs=[pl.BlockSpec((nt, bb, D_MODEL), tm), pl.BlockSpec((bb, CONV_STATE, D_MODEL), seq)],
        out_shape=[jax.ShapeDtypeStruct((nt, b, D_MODEL), F32), jax.ShapeDtypeStruct((b, CONV_STATE, D_MODEL), F32)],
        scratch_shapes=[pltpu.VMEM((bb * win * CONV_PITCH, LANES), F32), pltpu.VMEM((nt * bb, D_MODEL), F32)],
        compiler_params=_params("arbitrary"),
        name="conv_out_sample",
    )(state, u, x, wdw, bdw, lng, lnb, wout, bout)


def _route(lg):
    big = 3.0e38
    lane = lax.broadcasted_iota(jnp.int32, lg.shape, 1)
    lanef = lane.astype(F32)
    is_g = lane < N_EXPERT_GROUPS
    gl = jnp.where(is_g, lg, -big)
    gmax = jnp.max(gl, axis=1, keepdims=True)
    gsum = jnp.sum(jnp.where(is_g, jnp.exp(gl - gmax), 0.0), axis=1, keepdims=True)
    g_w = 1.0 / gsum
    g_idx = jnp.min(jnp.where(gl == gmax, lanef, big), axis=1, keepdims=True)
    rel = lanef - float(EXPERT_LANE0) - g_idx * float(EXPERTS_PER_GROUP)
    in_grp = jnp.where(rel >= 0.0, jnp.where(rel < float(EXPERTS_PER_GROUP), 1.0, 0.0), 0.0) > 0.5
    el = jnp.where(in_grp, lg, -big)
    e1 = jnp.max(el, axis=1, keepdims=True)
    i1 = jnp.min(jnp.where(el == e1, lanef, big), axis=1, keepdims=True)
    el2 = jnp.where(lanef == i1, -big, el)
    e2 = jnp.max(el2, axis=1, keepdims=True)
    i2 = jnp.min(jnp.where(el2 == e2, lanef, big), axis=1, keepdims=True)
    tt = jnp.exp(e2 - e1)
    w1 = g_w / (1.0 + tt)
    w2 = g_w * tt / (1.0 + tt)
    base = float(EXPERT_LANE0) + g_idx * float(EXPERTS_PER_GROUP)
    a = jnp.minimum(i1, i2) - base
    b = jnp.maximum(i1, i2) - base
    pair = a * (7.0 - a) * 0.5 + (b - a - 1.0)
    first_is_lo = i1 < i2
    return (g_idx * float(PAIRS_PER_GROUP) + pair, jnp.where(first_is_lo, w1, w2), jnp.where(first_is_lo, w2, w1))


def _to_row_linear(dst_ref, src_ref, n_tiles):
    def body(g, carry):
        r = pl.multiple_of(g * 8, 8)
        for j in range(n_tiles):
            dst_ref[pl.ds(r * ROW_PITCH + j, 8, stride=ROW_PITCH), :] = src_ref[pl.ds(r, 8), j * LANES:(j + 1) * LANES]
        return carry

    lax.fori_loop(0, src_ref.shape[0] // 8, body, 0, unroll=4)


def _from_row_linear(dst_ref, src_ref, n_tiles, pitch):
    def body(g, carry):
        r = pl.multiple_of(g * 8, 8)
        for j in range(n_tiles):
            dst_ref[pl.ds(r, 8), j * LANES:(j + 1) * LANES] = src_ref[pl.ds(r * pitch + j, 8, stride=pitch), :]
        return carry

    lax.fori_loop(0, dst_ref.shape[0] // 8, body, 0, unroll=4)


def _pick(i, n_prompt_tiles, prompt_ref, sample_ref):
    return jnp.where(i < n_prompt_tiles, prompt_ref[...], sample_ref[...])


def _route_kernel(n_prompt_tiles, yp_ref, ys_ref, g_ref, wr_ref, br_ref, tri_ref, pay_ref, meta_ref, cnt_ref, carry):
    i = pl.program_id(0)

    @pl.when(i == 0)
    def _():
        carry[...] = jnp.zeros_like(carry)

    xf = _rms(_pick(i, n_prompt_tiles, yp_ref, ys_ref), g_ref[...])
    x_hi, x_lo = _split_bf16(xf)
    part = _mm(x_hi, wr_ref[...]) + _mm(x_lo, wr_ref[...])
    logits = part[:, :ROUTER_LANES] + part[:, ROUTER_LANES:] + br_ref[...]
    bucket, w_lo, w_hi = _route(logits)
    lane = lax.broadcasted_iota(jnp.int32, logits.shape, 1)
    onehot = jnp.where(lane.astype(F32) == bucket, 1.0, 0.0)
    before = _mm(tri_ref[...], onehot.astype(BF16)) + carry[...]
    rank = jnp.sum(onehot * before, axis=1, keepdims=True)
    carry[...] += jnp.sum(onehot, axis=0, keepdims=True)
    cnt_ref[...] = carry[...]
    meta = jnp.where(lane == 0, bucket, jnp.where(lane == 1, rank, 0.0))
    meta_ref[...] = jnp.transpose(meta)[:8, :].astype(jnp.int32)

    gates = jnp.where(lane == 0, w_lo, jnp.where(lane == 1, w_hi, 0.0))
    nj = D_MODEL // LANES
    for r in range(0, xf.shape[0], 8):
        for j in range(nj):
            pay_ref[pl.ds(r * ROW_PITCH + j, 8, stride=ROW_PITCH), :] = xf[r:r + 8, j * LANES:(j + 1) * LANES]
        pay_ref[pl.ds(r * ROW_PITCH + nj, 8, stride=ROW_PITCH), :] = gates[r:r + 8, :]


def _route_call(y_p, y_s, g, wr, br, tri):
    t = y_s.shape[0]
    npt = y_p.shape[0] // t
    n = y_p.shape[0] + t
    pidx = lambda i: (jnp.minimum(i, npt - 1), 0)
    return pl.pallas_call(
        functools.partial(_route_kernel, npt),
        grid=(npt + 1,),
        in_specs=[pl.BlockSpec((t, D_MODEL), pidx), _full(y_s.shape), _full(g.shape), _full(wr.shape),
                  _full(br.shape), _full(tri.shape)],
        out_specs=[pl.BlockSpec((t * ROW_PITCH, LANES), lambda i: (i, 0)),
                   pl.BlockSpec((8, t), lambda i: (0, i)), _full((1, ROUTER_LANES))],
        out_shape=[jax.ShapeDtypeStruct((n * ROW_PITCH, LANES), F32),
                   jax.ShapeDtypeStruct((8, n), jnp.int32), jax.ShapeDtypeStruct((1, ROUTER_LANES), F32)],
        scratch_shapes=[pltpu.VMEM((1, ROUTER_LANES), F32)],
        compiler_params=_params("arbitrary"),
        name="route",
    )(y_p, y_s, g, wr, br, tri)


def _permute_kernel(pos_ref, zstart_ref, nz_ref, used_ref, src_ref, dst_ref, zbuf, ring, zsem, in_sems, out_sems):
    zbuf[...] = jnp.zeros_like(zbuf)

    def zero_tile(start):
        return pltpu.make_async_copy(zbuf, dst_ref.at[pl.ds(pl.multiple_of(start * ROW_PITCH, 8), TM * ROW_PITCH)], zsem)

    for b in range(N_BUCKETS):
        @pl.when(nz_ref[b] > 0)
        def _():
            zero_tile(zstart_ref[b]).start()
    n_tiles = dst_ref.shape[0] // (TM * ROW_PITCH)

    def start_unused(i, carry):
        zero_tile(i * TM).start()
        return carry

    lax.fori_loop(used_ref[0], n_tiles, start_unused, 0)
    for b in range(N_BUCKETS):
        @pl.when(nz_ref[b] > 0)
        def _():
            zero_tile(0).wait()

    def wait_unused(i, carry):
        zero_tile(0).wait()
        return carry

    lax.fori_loop(used_ref[0], n_tiles, wait_unused, 0)

    chunk_rows = DMA_CHUNK * ROW_PITCH
    n_chunks = src_ref.shape[0] // chunk_rows

    def fetch(c, slot):
        return pltpu.make_async_copy(src_ref.at[pl.ds(pl.multiple_of(c * chunk_rows, 8), chunk_rows)], ring.at[slot],
                                     in_sems.at[slot])

    def drain(slot):
        pltpu.make_async_copy(ring.at[slot], dst_ref.at[pl.ds(0, chunk_rows)], out_sems.at[slot]).wait()

    fetch(0, 0).start()

    def step(c, carry):
        slot = lax.rem(c, RING)
        nxt = lax.rem(c + 1, RING)
        fetch(c, slot).wait()

        @pl.when(c >= RING - 1)
        def _():
            drain(nxt)

        @pl.when(c + 1 < n_chunks)
        def _():
            fetch(c + 1, nxt).start()

        def send(jj, carry2):
            row = pos_ref[c * DMA_CHUNK + jj] * ROW_PITCH
            pltpu.make_async_copy(ring.at[slot, pl.ds(jj * ROW_PITCH, ROW_PITCH)], dst_ref.at[pl.ds(row, ROW_PITCH)],
                                  out_sems.at[slot]).start()
            return carry2

        lax.fori_loop(0, DMA_CHUNK, send, 0, unroll=8)
        return carry

    lax.fori_loop(0, n_chunks, step, 0)
    for c in range(n_chunks - (RING - 1), n_chunks):
        drain(c % RING)


def _permute(pos, zstart, nz, used, payload, n_rows_out):
    assert payload.shape[0] // (DMA_CHUNK * ROW_PITCH) >= RING
    return pl.pallas_call(
        _permute_kernel,
        grid_spec=pltpu.PrefetchScalarGridSpec(
            num_scalar_prefetch=4, grid=(1,),
            in_specs=[pl.BlockSpec(memory_space=pl.ANY)],
            out_specs=pl.BlockSpec(memory_space=pl.ANY),
            scratch_shapes=[pltpu.VMEM((TM * ROW_PITCH, LANES), payload.dtype),
                            pltpu.VMEM((RING, DMA_CHUNK * ROW_PITCH, LANES), payload.dtype),
                            pltpu.SemaphoreType.DMA(()), pltpu.SemaphoreType.DMA((RING,)),
                            pltpu.SemaphoreType.DMA((RING,))]),
        out_shape=jax.ShapeDtypeStruct((n_rows_out * ROW_PITCH, LANES), payload.dtype),
        compiler_params=_params("arbitrary"),
        name="permute",
    )(pos, zstart, nz, used, payload)


def _silu(x):
    return x * _sigmoid(x)


def _experts_kernel(tix_ref, lo_ref, hi_ref, valid_ref, fresh_ref, xs_ref, wgl_ref, wgh_ref, wul_ref, wuh_ref,
                    wdl_ref, wdh_ref, ys_ref, wup_s, wdn_s, xbuf, ybuf):
    i = pl.program_id(0)

    @pl.when(fresh_ref[i] > 0)
    def _():
        for s, ref in enumerate((wgl_ref, wgh_ref, wul_ref, wuh_ref)):
            wup_s[s] = ref[0, 0].astype(BF16)
        for s, ref in enumerate((wdl_ref, wdh_ref)):
            wdn_s[s] = ref[0, 0].astype(BF16)

    @pl.when(valid_ref[i] == 0)
    def _():
        ys_ref[...] = jnp.zeros_like(ys_ref)

    @pl.when(valid_ref[i] > 0)
    def _():
        _from_row_linear(xbuf, xs_ref, ROW_PITCH, ROW_PITCH)
        x = xbuf[:, :D_MODEL].astype(BF16)
        gates = xbuf[:, D_MODEL:]
        h_lo = _silu(_mm(x, wup_s[0])) * _mm(x, wup_s[2]) * gates[:, 0:1]
        h_hi = _silu(_mm(x, wup_s[1])) * _mm(x, wup_s[3]) * gates[:, 1:2]
        ybuf[:, :D_MODEL] = _mm(h_lo.astype(BF16), wdn_s[0]) + _mm(h_hi.astype(BF16), wdn_s[1])
        ybuf[:, D_MODEL:] = jnp.zeros((TM, LANES), F32)
        _to_row_linear(ys_ref, ybuf, ROW_PITCH)


def _experts(layer, tix, lo, hi, valid, fresh, xs, wg, wu, wd):
    n_tiles = xs.shape[0] // (TM * ROW_PITCH)
    row = lambda i, tix, lo, hi, valid, fresh: (tix[i], 0)
    row_out = lambda i, tix, lo, hi, valid, fresh: (i, 0)
    e_lo = lambda i, tix, lo, hi, valid, fresh: (layer, lo[i], 0, 0)
    e_hi = lambda i, tix, lo, hi, valid, fresh: (layer, hi[i], 0, 0)
    up = pl.BlockSpec((1, 1, D_MODEL, D_EXPERT), e_lo), pl.BlockSpec((1, 1, D_MODEL, D_EXPERT), e_hi)
    down = pl.BlockSpec((1, 1, D_EXPERT, D_MODEL), e_lo), pl.BlockSpec((1, 1, D_EXPERT, D_MODEL), e_hi)
    return pl.pallas_call(
        _experts_kernel,
        grid_spec=pltpu.PrefetchScalarGridSpec(
            num_scalar_prefetch=5, grid=(n_tiles,),
            in_specs=[pl.BlockSpec((TM * ROW_PITCH, LANES), row), *up, *up, *down],
            out_specs=pl.BlockSpec((TM * ROW_PITCH, LANES), row_out),
            scratch_shapes=[pltpu.VMEM((4, D_MODEL, D_EXPERT), BF16), pltpu.VMEM((2, D_EXPERT, D_MODEL), BF16),
                            pltpu.VMEM((TM, PAY_WIDTH), F32), pltpu.VMEM((TM, PAY_WIDTH), F32)]),
        out_shape=jax.ShapeDtypeStruct(xs.shape, F32),
        compiler_params=_params("arbitrary"),
        name="experts",
    )(tix, lo, hi, valid, fresh, xs, wg, wg, wu, wu, wd, wd)


def _ple_kernel(n_prompt_tiles, pos_ref, yp_ref, ys_ref, moe_ref, pp_ref, ps_ref, g_ref, wg_ref, wp_ref, op_ref, os_ref,
                mring, mbuf, sems):
    i = pl.program_id(0)
    t = mbuf.shape[0]
    w = D_MODEL // LANES

    def gather(tile, slot):
        def body(jj, carry):
            row = pos_ref[tile * t + jj] * ROW_PITCH
            pltpu.make_async_copy(moe_ref.at[pl.ds(row, w)], mring.at[slot, pl.ds(jj * w, w)], sems.at[slot]).start()
            return carry

        lax.fori_loop(0, t, body, 0, unroll=8)

    def wait_tile(slot):
        pltpu.make_async_copy(moe_ref.at[pl.ds(0, t * w)], mring.at[slot], sems.at[slot]).wait()

    @pl.when(i == 0)
    def _():
        gather(0, 0)
        gather(1, 1)

    slot = lax.rem(i, PLE_RING)
    nslot = lax.rem(i + 2, PLE_RING)
    wait_tile(slot)
    _from_row_linear(mbuf, mring.at[slot], w, w)
    y2 = _pick(i, n_prompt_tiles, yp_ref, ys_ref) + mbuf[...]
    hn = _rms(y2, g_ref[...]).astype(BF16)
    gt = _sigmoid(_mm(hn, wg_ref[...]))
    pr = _mm(jnp.where(i < n_prompt_tiles, pp_ref[0], ps_ref[...]).astype(BF16), wp_ref[...])
    y3 = y2 + gt * pr
    nxt = jnp.minimum(i + 2, n_prompt_tiles) * t
    for jj in range(t):
        row = pos_ref[nxt + jj] * ROW_PITCH
        pltpu.make_async_copy(moe_ref.at[pl.ds(row, w)], mring.at[nslot, pl.ds(jj * w, w)], sems.at[nslot]).start()

    @pl.when(i < n_prompt_tiles)
    def _():
        op_ref[...] = y3

    @pl.when(i >= n_prompt_tiles)
    def _():
        os_ref[...] = y3
        wait_tile(lax.rem(i + 1, PLE_RING))
        wait_tile(nslot)


def _ple(layer, pos, y_p, y_s, ys_sorted, p_p, p_s, g, wg, wp):
    t = y_s.shape[0]
    npt = y_p.shape[0] // t
    w = D_MODEL // LANES
    pidx = lambda i, pos: (jnp.minimum(i, npt - 1), 0)
    full = lambda shape: pl.BlockSpec(shape, lambda i, pos: (0,) * len(shape))
    return pl.pallas_call(
        functools.partial(_ple_kernel, npt),
        grid_spec=pltpu.PrefetchScalarGridSpec(
            num_scalar_prefetch=1, grid=(npt + 1,),
            in_specs=[pl.BlockSpec((t, D_MODEL), pidx), full(y_s.shape), pl.BlockSpec(memory_space=pl.ANY),
                      pl.BlockSpec((1, t, PLE_DIM), lambda i, pos: (layer, jnp.minimum(i, npt - 1), 0)),
                      full(p_s.shape), full(g.shape), full(wg.shape), full(wp.shape)],
            out_specs=[pl.BlockSpec((t, D_MODEL), pidx), full(y_s.shape)],
            scratch_shapes=[pltpu.VMEM((PLE_RING, t * w, LANES), F32), pltpu.VMEM((t, D_MODEL), F32),
                            pltpu.SemaphoreType.DMA((PLE_RING,))]),
        out_shape=[jax.ShapeDtypeStruct(y_p.shape, F32), jax.ShapeDtypeStruct(y_s.shape, F32)],
        compiler_params=_params("arbitrary"),
        name="ple",
    )(pos, y_p, y_s, ys_sorted, p_p, p_s, g, wg, wp)


def _bucket_experts():
    pairs = [(a, b) for a in range(EXPERTS_PER_GROUP) for b in range(a + 1, EXPERTS_PER_GROUP)]
    lo = [g * EXPERTS_PER_GROUP + a for g in range(N_EXPERT_GROUPS) for a, _ in pairs]
    hi = [g * EXPERTS_PER_GROUP + b for g in range(N_EXPERT_GROUPS) for _, b in pairs]
    return jnp.array(lo, jnp.int32), jnp.array(hi, jnp.int32)


def _moe_ple(layer, y_p, y_s, gffn, wr, br, wg, wu, wd, p_p, p_s, gple, pwg, pwp):
    t = y_s.shape[0]
    n = y_p.shape[0] + t
    tri = jnp.asarray(np.tri(t, t, -1, dtype=np.float32), BF16)
    payload, meta, counts = _route_call(y_p, y_s, gffn, wr, br, tri)

    bucket, rank = meta[0], meta[1]
    cnt = counts[0, :N_BUCKETS].astype(jnp.int32)
    padded = (cnt + TM - 1) // TM * TM
    ends = jnp.cumsum(padded)
    starts = ends - padded
    in_bucket = bucket[:, None] == jnp.arange(N_BUCKETS, dtype=jnp.int32)[None, :]
    pos = rank + jnp.sum(jnp.where(in_bucket, starts[None, :], 0), axis=1)
    n_rows = n + N_BUCKETS * TM
    n_tiles = n_rows // TM
    tile_start = jnp.arange(n_tiles, dtype=jnp.int32) * TM
    valid = tile_start < ends[-1]
    used = (ends[-1] // TM).astype(jnp.int32)
    tix = jnp.arange(n_tiles, dtype=jnp.int32)
    tile_bucket = jnp.sum((ends[None, :] <= (jnp.minimum(tix, used - 1) * TM)[:, None]).astype(jnp.int32), axis=1)
    prev_bucket = jnp.concatenate([jnp.full((1,), -1, jnp.int32), tile_bucket[:-1]])
    fresh = valid & (tile_bucket != prev_bucket)
    first_tile = jnp.where(fresh, tix, n_tiles)
    next_first = lax.cummin(jnp.concatenate([first_tile[1:], jnp.full((1,), n_tiles, jnp.int32)]), reverse=True)
    next_bucket = jnp.where(next_first < n_tiles, jnp.take(tile_bucket, jnp.minimum(next_first, n_tiles - 1)),
                            tile_bucket)
    resident = jnp.where(fresh, tile_bucket, next_bucket)
    lo_tab, hi_tab = _bucket_experts()
    lo = jnp.take(lo_tab, resident)
    hi = jnp.take(hi_tab, resident)

    xs = _permute(pos, (ends - TM).astype(jnp.int32), (cnt > 0).astype(jnp.int32), used.reshape(1), payload, n_rows)
    ys = _experts(layer, jnp.minimum(tix, used - 1), lo, hi, valid.astype(jnp.int32), fresh.astype(jnp.int32), xs, wg,
                  wu, wd)
    return _ple(layer, pos, y_p, y_s, ys, p_p, p_s, gple, pwg, pwp)


def _split_bf16(a):
    hi = a.astype(BF16)
    lo = (a - hi.astype(F32)).astype(BF16)
    return hi, lo


def _head_norm(a, ind_ref, indt_ref, gain):
    hi, lo = _split_bf16(a * a)
    ss = _mm(hi, ind_ref[...]) + _mm(lo, ind_ref[...])
    inv = lax.rsqrt(ss * (1.0 / HEAD_DIM) + EPS)
    ihi, ilo = _split_bf16(inv)
    invb = _mm(ihi, indt_ref[...]) + _mm(ilo, indt_ref[...])
    return a * invb * gain


def _qkv_kernel(x_ref, g_ref, w_ref, iq_ref, iqt_ref, ik_ref, ikt_ref, qg_ref, kg_ref, q_ref, k_ref, v_ref):
    h = _rms(x_ref[...], g_ref[...]).astype(BF16)
    qkv = _mm(h, w_ref[...])
    nq = N_HEADS * HEAD_DIM
    q = _head_norm(qkv[:, :nq], iq_ref, iqt_ref, qg_ref[...])
    k = _head_norm(qkv[:, nq:nq + KV_DIM], ik_ref, ikt_ref, kg_ref[...])
    q_ref[...] = (q * (HEAD_DIM ** -0.5)).astype(BF16)
    k_ref[...] = k
    v_ref[...] = qkv[:, nq + KV_DIM:]


def _qkv(x, g, w, iq, iqt, ik, ikt, qg, kg):
    n = x.shape[0]
    t = _row_tile(n, big=True)
    row = lambda i: (i, 0)
    return pl.pallas_call(
        _qkv_kernel,
        grid=(n // t,),
        in_specs=[pl.BlockSpec((t, D_MODEL), row)] + [_full(a.shape) for a in (g, w, iq, iqt, ik, ikt, qg, kg)],
        out_specs=[pl.BlockSpec((t, N_HEADS * HEAD_DIM), row), pl.BlockSpec((t, KV_DIM), row),
                   pl.BlockSpec((t, KV_DIM), row)],
        out_shape=[jax.ShapeDtypeStruct((n, N_HEADS * HEAD_DIM), BF16), jax.ShapeDtypeStruct((n, KV_DIM), F32),
                   jax.ShapeDtypeStruct((n, KV_DIM), F32)],
        compiler_params=_params("arbitrary"),
        name="qkv",
    )(x, g, w, iq, iqt, ik, ikt, qg, kg)


def _dup_heads(a):
    out = []
    for s in range(KV_DIM // LANES):
        sl = a[:, s * LANES:(s + 1) * LANES]
        sw = pltpu.roll(sl, HEAD_DIM, axis=1)
        low = lax.broadcasted_iota(jnp.int32, sl.shape, 1) < HEAD_DIM
        out.append(jnp.where(low, sl, sw))
        out.append(jnp.where(low, sw, sl))
    return jnp.concatenate(out, axis=1).astype(BF16)


def _attend(q_rows, k2, v2, bias_of, sink_of, extra_mask):
    m_rows = q_rows.shape[0]
    low_q = lax.broadcasted_iota(jnp.int32, (m_rows, LANES), 1) < HEAD_DIM
    low_k = lax.broadcasted_iota(jnp.int32, (k2.shape[0], LANES), 1) < HEAD_DIM
    zero_q = jnp.zeros((m_rows, LANES), BF16)
    zero_k = jnp.zeros((k2.shape[0], LANES), BF16)
    slabs = []
    for g in range(N_KV_HEADS):
        kg = k2[:, g * LANES:(g + 1) * LANES]
        vg = v2[:, g * LANES:(g + 1) * LANES]
        v_lo = jnp.where(low_k, vg, zero_k)
        v_hi = jnp.where(low_k, zero_k, vg)
        lhs = []
        for a in range(GROUP):
            h = g * GROUP + a
            qs = q_rows[:, (h // 2) * LANES:(h // 2 + 1) * LANES]
            lhs.append(jnp.where(low_q, qs, zero_q) if h % 2 == 0 else jnp.where(low_q, zero_q, qs))
        s = lax.dot_general(jnp.concatenate(lhs, axis=0), kg, (((1,), (1,)), ((), ())), preferred_element_type=F32)
        probs, rinv = [], []
        for a in range(GROUP):
            h = g * GROUP + a
            sa = s[a * m_rows:(a + 1) * m_rows] + bias_of(h)
            if extra_mask is not None:
                sa = jnp.where(extra_mask, NEG_INF, sa)
            sink = sink_of(h)
            m = jnp.maximum(jnp.max(sa, axis=1, keepdims=True), sink)
            p = jnp.exp(sa - m)
            den = jnp.sum(p, axis=1, keepdims=True) + jnp.exp(sink - m)
            probs.append(p.astype(BF16))
            rinv.append(1.0 / den)
        for sp in range(GROUP // 2):
            o = _mm(probs[2 * sp], v_lo) + _mm(probs[2 * sp + 1], v_hi)
            slabs.append(o * jnp.where(low_q, rinv[2 * sp], rinv[2 * sp + 1]))
    return jnp.concatenate(slabs, axis=1)


def _attn_p_kernel(sink_ref, q_ref, k_ref, v_ref, x_ref, bias_ref, wo_ref, y_ref, kbuf, vbuf, obuf):
    t = q_ref.shape[0]
    i = pl.program_id(0)

    @pl.when(i == 0)
    def _():
        kbuf[0:WINDOW, :] = jnp.zeros((WINDOW, 2 * KV_DIM), BF16)
        vbuf[0:WINDOW, :] = jnp.zeros((WINDOW, 2 * KV_DIM), BF16)

    @pl.when(i > 0)
    def _():
        kbuf[0:WINDOW, :] = kbuf[t:t + WINDOW, :]
        vbuf[0:WINDOW, :] = vbuf[t:t + WINDOW, :]

    kbuf[WINDOW:, :] = _dup_heads(k_ref[...])
    vbuf[WINDOW:, :] = _dup_heads(v_ref[...])
    col = lax.broadcasted_iota(jnp.int32, (WINDOW, 2 * WINDOW), 1)
    for j in range(t // WINDOW):
        rows = slice(j * WINDOW, (j + 1) * WINDOW)
        keys = slice(j * WINDOW, (j + 2) * WINDOW)
        extra = jnp.logical_and(i == 0, col < WINDOW) if j == 0 else None
        o = _attend(q_ref[rows, :], kbuf[keys, :], vbuf[keys, :], lambda h: bias_ref[h], lambda h: sink_ref[h], extra)
        obuf[rows, :] = o.astype(BF16)
    y_ref[...] = x_ref[...] + _mm(obuf[...], wo_ref[...])


def _attn_p(sinks, q, k, v, x, bias, wo):
    n = q.shape[0]
    t = _row_tile(n)
    row = lambda i: (i, 0)
    return pl.pallas_call(
        _attn_p_kernel,
        grid=(n // t,),
        in_specs=[pl.BlockSpec(memory_space=pltpu.SMEM),
                  pl.BlockSpec((t, N_HEADS * HEAD_DIM), row), pl.BlockSpec((t, KV_DIM), row),
                  pl.BlockSpec((t, KV_DIM), row), pl.BlockSpec((t, D_MODEL), row),
                  _full(bias.shape), _full(wo.shape)],
        out_specs=pl.BlockSpec((t, D_MODEL), row),
        out_shape=jax.ShapeDtypeStruct((n, D_MODEL), F32),
        scratch_shapes=[pltpu.VMEM((t + WINDOW, 2 * KV_DIM), BF16), pltpu.VMEM((t + WINDOW, 2 * KV_DIM), BF16),
                        pltpu.VMEM((t, N_HEADS * HEAD_DIM), BF16)],
        compiler_params=_params("arbitrary"),
        name="attn_prompt",
    )(sinks, q, k, v, x, bias, wo)


def _attn_s_kernel(sink_ref, q_ref, k_ref, v_ref, ck_ref, cv_ref, bias_ref, o_ref, nk_ref, nv_ref, shift_buf):
    pad = bias_ref.shape[2] - ck_ref.shape[0] - k_ref.shape[0]
    zpad = jnp.zeros((pad, KV_DIM), F32)
    k2 = _dup_heads(jnp.concatenate([ck_ref[...], k_ref[...], zpad], axis=0))
    v2 = _dup_heads(jnp.concatenate([cv_ref[...], v_ref[...], zpad], axis=0))
    o = _attend(q_ref[...], k2, v2, lambda h: bias_ref[h], lambda h: sink_ref[h], None)
    o_ref[...] = o.astype(BF16)
    t_new = k_ref.shape[0] // SAMPLE_SEQS
    for new_ref, old_ref, add_ref in ((nk_ref, ck_ref, k_ref), (nv_ref, cv_ref, v_ref)):
        for b in range(SAMPLE_SEQS):
            r = b * WINDOW
            shift_buf[0:WINDOW - t_new, :] = old_ref[r + t_new:r + WINDOW, :]
            shift_buf[WINDOW - t_new:WINDOW, :] = add_ref[b * t_new:(b + 1) * t_new, :]
            new_ref[b] = jnp.transpose(shift_buf[...])


def _attn_s(sinks, q, k, v, ck, cv, bias, t_new):
    n = q.shape[0]
    rows = SAMPLE_SEQS * t_new
    crow = SAMPLE_SEQS * WINDOW
    row = lambda i: (i, 0)
    return pl.pallas_call(
        _attn_s_kernel,
        grid=(n // rows,),
        in_specs=[pl.BlockSpec(memory_space=pltpu.SMEM),
                  pl.BlockSpec((rows, N_HEADS * HEAD_DIM), row), pl.BlockSpec((rows, KV_DIM), row),
                  pl.BlockSpec((rows, KV_DIM), row), pl.BlockSpec((crow, KV_DIM), row),
                  pl.BlockSpec((crow, KV_DIM), row), _full(bias.shape)],
        out_specs=[pl.BlockSpec((rows, N_HEADS * HEAD_DIM), row),
                   pl.BlockSpec((SAMPLE_SEQS, KV_DIM, WINDOW), lambda i: (i, 0, 0)),
                   pl.BlockSpec((SAMPLE_SEQS, KV_DIM, WINDOW), lambda i: (i, 0, 0))],
        out_shape=[jax.ShapeDtypeStruct((n, N_HEADS * HEAD_DIM), BF16),
                   jax.ShapeDtypeStruct((ck.shape[0] // WINDOW, KV_DIM, WINDOW), F32),
                   jax.ShapeDtypeStruct((cv.shape[0] // WINDOW, KV_DIM, WINDOW), F32)],
        scratch_shapes=[pltpu.VMEM((WINDOW, KV_DIM), F32)],
        compiler_params=_params("arbitrary"),
        name="attn_sample",
    )(sinks, q, k, v, ck, cv, bias)


def _proj_res_kernel(o_ref, x_ref, w_ref, y_ref):
    y_ref[...] = x_ref[...] + _mm(o_ref[...], w_ref[...])


def _proj_res(o, x, w):
    n = o.shape[0]
    t = _row_tile(n)
    row = lambda i: (i, 0)
    return pl.pallas_call(
        _proj_res_kernel,
        grid=(n // t,),
        in_specs=[pl.BlockSpec((t, o.shape[1]), row), pl.BlockSpec((t, D_MODEL), row), _full(w.shape)],
        out_specs=pl.BlockSpec((t, D_MODEL), row),
        out_shape=jax.ShapeDtypeStruct((n, D_MODEL), F32),
        compiler_params=_params("arbitrary"),
        name="proj_res",
    )(o, x, w)


def _alibi_slopes():
    return np.exp2(-8.0 * np.arange(1, N_HEADS + 1, dtype=np.float64) / N_HEADS).astype(np.float32)


def _band_bias(dist, allowed):
    b = -(_alibi_slopes()[:, None, None] * dist.astype(np.float32)[None])
    return jnp.asarray(np.where(allowed[None], b, np.float32(NEG_INF)).astype(np.float32))


def _prompt_bias():
    dist = WINDOW + np.arange(WINDOW)[:, None] - np.arange(2 * WINDOW)[None, :]
    return _band_bias(dist, (dist >= 0) & (dist <= WINDOW))


def _sample_bias(t_new, n_cols):
    c = np.arange(n_cols)
    n_cache = SAMPLE_SEQS * WINDOW
    n_new = SAMPLE_SEQS * t_new
    is_cache = c < n_cache
    is_new = (c >= n_cache) & (c < n_cache + n_new)
    seq_c = np.where(is_cache, c // WINDOW, (c - n_cache) // t_new)
    pos_c = np.where(is_cache, c % WINDOW, WINDOW + (c - n_cache) % t_new)
    r = np.arange(n_new)
    seq_r, tok_r = r // t_new, r % t_new
    dist = WINDOW + tok_r[:, None] - pos_c[None, :]
    allowed = (seq_r[:, None] == seq_c[None, :]) & (is_cache | is_new)[None, :] & (dist >= 0) & (dist <= WINDOW)
    return _band_bias(dist, allowed)


def _head_indicator(n_heads):
    ch = np.arange(n_heads * HEAD_DIM) // HEAD_DIM
    ind = (ch[:, None] == np.arange(LANES)[None, :]).astype(np.float32)
    return jnp.asarray(ind, BF16), jnp.asarray(ind.T, BF16)


def kernel(x_prompt, x_sample, state_conv, cache_k, cache_v, p_prompt, p_sample, norm_mix, norm_ffn, norm_ple,
           conv_w_in, conv_b_in, conv_w_dw, conv_b_dw, conv_ln_g, conv_ln_b, conv_w_out, conv_b_out, attn_w_qkv,
           attn_q_norm, attn_k_norm, attn_sinks, attn_w_o, moe_w_rg, moe_b_rg, moe_w_re, moe_b_re, moe_w_gate,
           moe_w_up, moe_w_down, ple_w_gate, ple_w_proj):
    bp, seq, d = x_prompt.shape
    bs, t_new, _ = x_sample.shape
    assert bp == 1 and d == D_MODEL and seq % WINDOW == 0 and bs % SAMPLE_SEQS == 0
    assert seq % (bs * t_new) == 0 and (bs * t_new) % DMA_CHUNK == 0
    depth = norm_mix.shape[0]
    row2 = lambda a: a.reshape(1, -1)

    y_p = x_prompt.reshape(seq, d)
    y_s = x_sample.reshape(bs * t_new, d)
    conv_p, conv_s, k_p, v_p, k_s, v_s = [], [], [], [], [], []

    for i in range(depth):
        j = i // 2
        g_mix = row2(norm_mix[i])
        if i % 2 == 0:
            w_in = conv_w_in[j].astype(BF16)
            b_in = row2(conv_b_in[j])
            tail = (conv_w_dw[j], row2(conv_b_dw[j]), row2(conv_ln_g[j]), row2(conv_ln_b[j]),
                    conv_w_out[j].astype(BF16), row2(conv_b_out[j]))
            u_p = _conv_in(y_p, g_mix, w_in, b_in)
            u_s = _conv_in(y_s, g_mix, w_in, b_in)
            conv_p.append(u_p[seq - CONV_STATE:].reshape(1, CONV_STATE, d))
            y_p = _conv_out_p(u_p, y_p, *tail)
            ys_t, new_state = _conv_out_s(state_conv[j], u_s.reshape(bs, t_new, d),
                                          y_s.reshape(bs, t_new, d).transpose(1, 0, 2), *tail)
            conv_s.append(new_state)
            y_s = ys_t.transpose(1, 0, 2).reshape(bs * t_new, d)
        else:
            w_qkv = attn_w_qkv[j].astype(BF16)
            w_o = attn_w_o[j].astype(BF16)
            iq, iqt = _head_indicator(N_HEADS)
            ik, ikt = _head_indicator(N_KV_HEADS)
            qg = row2(jnp.tile(attn_q_norm[j], N_HEADS))
            kg = row2(jnp.tile(attn_k_norm[j], N_KV_HEADS))
            sinks = attn_sinks[j]
            q1, k1, v1 = _qkv(y_p, g_mix, w_qkv, iq, iqt, ik, ikt, qg, kg)
            q2, k2, v2 = _qkv(y_s, g_mix, w_qkv, iq, iqt, ik, ikt, qg, kg)
            k_p.append(k1[seq - WINDOW:].reshape(1, WINDOW, N_KV_HEADS, HEAD_DIM))
            v_p.append(v1[seq - WINDOW:].reshape(1, WINDOW, N_KV_HEADS, HEAD_DIM))
            y_p = _attn_p(sinks, q1, k1, v1, y_p, _prompt_bias(), w_o)
            n_cols = -(-(SAMPLE_SEQS * (WINDOW + t_new)) // LANES) * LANES
            o_s, nk, nv = _attn_s(sinks, q2, k2, v2, cache_k[j].reshape(bs * WINDOW, KV_DIM),
                                  cache_v[j].reshape(bs * WINDOW, KV_DIM), _sample_bias(t_new, n_cols), t_new)
            k_s.append(nk.reshape(bs, N_KV_HEADS, HEAD_DIM, WINDOW).transpose(0, 3, 1, 2))
            v_s.append(nv.reshape(bs, N_KV_HEADS, HEAD_DIM, WINDOW).transpose(0, 3, 1, 2))
            y_s = _proj_res(o_s, y_s, w_o)

        w_r = jnp.zeros((d, ROUTER_LANES), F32)
        w_r = w_r.at[:, :N_EXPERT_GROUPS].set(moe_w_rg[i]).at[:, EXPERT_LANE0:EXPERT_LANE0 + N_EXPERTS].set(moe_w_re[i])
        b_r = jnp.zeros((1, ROUTER_LANES), F32)
        b_r = b_r.at[0, :N_EXPERT_GROUPS].set(moe_b_rg[i]).at[0, EXPERT_LANE0:EXPERT_LANE0 + N_EXPERTS].set(moe_b_re[i])
        moe = (row2(norm_ffn[i]), jnp.concatenate(_split_bf16(w_r), axis=1), b_r, moe_w_gate, moe_w_up, moe_w_down)
        ple = (row2(norm_ple[i]), ple_w_gate[i].astype(BF16), ple_w_proj[i].astype(BF16))
        y_p, y_s = _moe_ple(i, y_p, y_s, *moe, p_prompt.reshape(depth, seq, PLE_DIM),
                            p_sample[i].reshape(bs * t_new, PLE_DIM), *ple)

    return (y_p.reshape(1, seq, d), y_s.reshape(bs, t_new, d), jnp.stack(conv_p), jnp.stack(conv_s),
            jnp.stack(k_p), jnp.stack(v_p), jnp.stack(k_s), jnp.stack(v_s))
```

```python
import functools

import numpy as np

import jax
import jax.numpy as jnp
from jax import lax
from jax.experimental import pallas as pl
from jax.experimental.pallas import tpu as pltpu

F32 = jnp.float32
BF16 = jnp.bfloat16

D_MODEL = 1024
PLE_DIM = 256
CONV_WIDTH = 31
CONV_STATE = CONV_WIDTH - 1
N_HEADS = 16
N_KV_HEADS = 4
HEAD_DIM = 64
GROUP = N_HEADS // N_KV_HEADS
WINDOW = 128
KV_DIM = N_KV_HEADS * HEAD_DIM
N_EXPERT_GROUPS = 4
EXPERTS_PER_GROUP = 4
N_EXPERTS = 16
D_EXPERT = 256
EPS = 1e-6
NEG_INF = -1e30

LANES = 128
ROUTER_LANES = LANES
EXPERT_LANE0 = N_EXPERT_GROUPS
HALO = 32
CONV_ROWS = 32
NORM_ROWS = 512
CONV_PITCH = D_MODEL // LANES + 1
SAMPLE_SEQS = 8
PAIRS_PER_GROUP = EXPERTS_PER_GROUP * (EXPERTS_PER_GROUP - 1) // 2
N_BUCKETS = N_EXPERT_GROUPS * PAIRS_PER_GROUP
TM = 256
PAY_WIDTH = D_MODEL + LANES
ROW_PITCH = PAY_WIDTH // LANES
DMA_CHUNK = 512
RING = 3
PLE_RING = 3
VMEM_LIMIT = 48 * 1024 * 1024


def _row_tile(n, big=False):
    if big and n % 1024 == 0:
        return 1024
    return 512 if n % 512 == 0 else n


def _params(*sem):
    return pltpu.CompilerParams(dimension_semantics=sem, vmem_limit_bytes=VMEM_LIMIT)


def _full(shape):
    nd = len(shape)
    return pl.BlockSpec(shape, lambda *_: (0,) * nd)


def _rms(x, g):
    ms = jnp.mean(x * x, axis=-1, keepdims=True)
    return x * lax.rsqrt(ms + EPS) * g


def _sigmoid(x):
    return 1.0 / (1.0 + jnp.exp(-x))


def _mm(a, b):
    return jnp.dot(a, b, preferred_element_type=F32)


def _conv_in_kernel(x_ref, g_ref, w_ref, b_ref, u_ref):
    h = _rms(x_ref[...], g_ref[...]).astype(BF16)
    z = _mm(h, w_ref[...]) + b_ref[...]
    u_ref[...] = z[:, :D_MODEL] * _sigmoid(z[:, D_MODEL:])


def _conv_in(x, g, w, b):
    n = x.shape[0]
    t = _row_tile(n, big=True)
    return pl.pallas_call(
        _conv_in_kernel,
        grid=(n // t,),
        in_specs=[pl.BlockSpec((t, D_MODEL), lambda i: (i, 0)), _full(g.shape), _full(w.shape), _full(b.shape)],
        out_specs=pl.BlockSpec((t, D_MODEL), lambda i: (i, 0)),
        out_shape=jax.ShapeDtypeStruct((n, D_MODEL), F32),
        compiler_params=_params("arbitrary"),
        name="conv_in",
    )(x, g, w, b)


def _ln_silu(c, g, b):
    mu = jnp.mean(c, axis=-1, keepdims=True)
    xc = c - mu
    var = jnp.mean(xc * xc, axis=-1, keepdims=True)
    cn = xc * lax.rsqrt(var + EPS) * g + b
    return cn * _sigmoid(cn)


def _conv_out_p_kernel(u_ref, halo_ref, x_ref, wdw_ref, bdw_ref, lng_ref, lnb_ref, wout_ref, bout_ref, y_ref,
                       ubuf, cbuf, hbuf):
    t = u_ref.shape[0]
    i = pl.program_id(0)
    nj = D_MODEL // LANES

    def put(r, j, val):
        ubuf[pl.ds(r * CONV_PITCH + j, 8, stride=CONV_PITCH), :] = val

    for r in range(0, HALO, 8):
        for j in range(nj):
            put(r, j, jnp.where(i > 0, halo_ref[r:r + 8, j * LANES:(j + 1) * LANES], 0.0))

    def fill(rr, carry):
        r = pl.multiple_of(rr * 8, 8)
        for j in range(nj):
            put(r + HALO, j, u_ref[pl.ds(r, 8), j * LANES:(j + 1) * LANES])
        return carry

    lax.fori_loop(0, t // 8, fill, 0, unroll=4)

    def conv_chunk(rr, carry):
        r0 = pl.multiple_of(rr * CONV_ROWS, CONV_ROWS)
        for j in range(nj):
            lanes = slice(j * LANES, (j + 1) * LANES)
            accs = [None] * (CONV_ROWS // 8)
            for k in range(CONV_WIDTH):
                wk = wdw_ref[k:k + 1, lanes]
                for q in range(CONV_ROWS // 8):
                    r = r0 + (HALO - CONV_STATE + k + 8 * q)
                    term = wk * ubuf[pl.ds(r * CONV_PITCH + j, 8, stride=CONV_PITCH), :]
                    accs[q] = term if k == 0 else accs[q] + term
            for q in range(CONV_ROWS // 8):
                cbuf[pl.ds(r0 + 8 * q, 8), lanes] = accs[q]
        return carry

    lax.fori_loop(0, t // CONV_ROWS, conv_chunk, 0)

    def norm_chunk(rr, carry):
        r0 = pl.multiple_of(rr * NORM_ROWS, NORM_ROWS)
        c = cbuf[pl.ds(r0, NORM_ROWS), :] + bdw_ref[...]
        hbuf[pl.ds(r0, NORM_ROWS), :] = _ln_silu(c, lng_ref[...], lnb_ref[...]).astype(BF16)
        return carry

    lax.fori_loop(0, t // NORM_ROWS, norm_chunk, 0)
    y_ref[...] = x_ref[...] + _mm(hbuf[...], wout_ref[...]) + bout_ref[...]


def _conv_out_p(u, x, wdw, bdw, lng, lnb, wout, bout):
    n = u.shape[0]
    t = _row_tile(n)
    hb = t // HALO
    row = lambda i: (i, 0)
    return pl.pallas_call(
        _conv_out_p_kernel,
        grid=(n // t,),
        in_specs=[pl.BlockSpec((t, D_MODEL), row),
                  pl.BlockSpec((HALO, D_MODEL), lambda i: (jnp.maximum(i * hb - 1, 0), 0)),
                  pl.BlockSpec((t, D_MODEL), row),
                  _full(wdw.shape), _full(bdw.shape), _full(lng.shape), _full(lnb.shape),
                  _full(wout.shape), _full(bout.shape)],
        out_specs=pl.BlockSpec((t, D_MODEL), row),
        out_shape=jax.ShapeDtypeStruct((n, D_MODEL), F32),
        scratch_shapes=[pltpu.VMEM(((t + HALO) * CONV_PITCH, LANES), F32), pltpu.VMEM((t, D_MODEL), F32),
                        pltpu.VMEM((t, D_MODEL), BF16)],
        compiler_params=_params("arbitrary"),
        name="conv_out_prompt",
    )(u, u, x, wdw, bdw, lng, lnb, wout, bout)


def _conv_out_s_kernel(st_ref, u_ref, x_ref, wdw_ref, bdw_ref, lng_ref, lnb_ref, wout_ref, bout_ref, y_ref, ns_ref,
                       wbuf, cbuf):
    bb, nt, _ = u_ref.shape
    win = CONV_STATE + nt
    nj = D_MODEL // LANES
    seq_pitch = win * CONV_PITCH

    def put(b, r0, rows, j, val):
        wbuf[pl.ds((b * win + r0) * CONV_PITCH + j, rows, stride=CONV_PITCH), :] = val

    for b in range(bb):
        for j in range(nj):
            lanes = slice(j * LANES, (j + 1) * LANES)
            for r0 in range(0, CONV_STATE, 8):
                rows = min(8, CONV_STATE - r0)
                put(b, r0, rows, j, st_ref[b, r0:r0 + rows, lanes])
            put(b, CONV_STATE, nt, j, u_ref[b, :, lanes])

    for b in range(bb):
        for r0 in range(0, CONV_STATE, 8):
            rows = min(8, CONV_STATE - r0)
            for j in range(nj):
                src = (b * win + nt + r0) * CONV_PITCH + j
                ns_ref[b, r0:r0 + rows, j * LANES:(j + 1) * LANES] = wbuf[pl.ds(src, rows, stride=CONV_PITCH), :]

    for t in range(nt):
        for b0 in range(0, bb, 8):
            for j in range(nj):
                lanes = slice(j * LANES, (j + 1) * LANES)
                acc = None
                for k in range(CONV_WIDTH):
                    start = (b0 * win + t + k) * CONV_PITCH + j
                    term = wdw_ref[k:k + 1, lanes] * wbuf[pl.ds(start, 8, stride=seq_pitch), :]
                    acc = term if acc is None else acc + term
                cbuf[t * bb + b0:t * bb + b0 + 8, lanes] = acc
    c = cbuf[...] + bdw_ref[...]
    h = _ln_silu(c, lng_ref[...], lnb_ref[...]).astype(BF16)
    y = _mm(h, wout_ref[...]) + bout_ref[...]
    for t in range(nt):
        y_ref[t] = x_ref[t] + y[t * bb:(t + 1) * bb]


def _conv_out_s(state, u, x, wdw, bdw, lng, lnb, wout, bout):
    b, nt, _ = u.shape
    bb = 16 if b % 16 == 0 else b
    assert bb % 8 == 0
    win = CONV_STATE + nt
    seq = lambda i: (i, 0, 0)
    tm = lambda i: (0, i, 0)
    return pl.pallas_call(
        _conv_out_s_kernel,
        grid=(b // bb,),
        in_specs=[pl.BlockSpec((bb, CONV_STATE, D_MODEL), seq), pl.BlockSpec((bb, nt, D_MODEL), seq),
                  pl.BlockSpec((nt, bb, D_MODEL), tm),
                  _full(wdw.shape), _full(bdw.shape), _full(lng.shape), _full(lnb.shape),
                  _full(wout.shape), _full(bout.shape)],
        out_specs=[pl.BlockSpec((nt, bb, D_MODEL), tm), pl.BlockSpec((bb, CONV_STATE, D_MODEL), seq)],
        out_shape=[jax.ShapeDtypeStruct((nt, b, D_MODEL), F32), jax.ShapeDtypeStruct((b, CONV_STATE, D_MODEL), F32)],
        scratch_shapes=[pltpu.VMEM((bb * win * CONV_PITCH, LANES), F32), pltpu.VMEM((nt * bb, D_MODEL), F32)],
        compiler_params=_params("arbitrary"),
        name="conv_out_sample",
    )(state, u, x, wdw, bdw, lng, lnb, wout, bout)


def _route(lg):
    big = 3.0e38
    lane = lax.broadcasted_iota(jnp.int32, lg.shape, 1)
    lanef = lane.astype(F32)
    is_g = lane < N_EXPERT_GROUPS
    gl = jnp.where(is_g, lg, -big)
    gmax = jnp.max(gl, axis=1, keepdims=True)
    gsum = jnp.sum(jnp.where(is_g, jnp.exp(gl - gmax), 0.0), axis=1, keepdims=True)
    g_w = 1.0 / gsum
    g_idx = jnp.min(jnp.where(gl == gmax, lanef, big), axis=1, keepdims=True)
    rel = lanef - float(EXPERT_LANE0) - g_idx * float(EXPERTS_PER_GROUP)
    in_grp = jnp.where(rel >= 0.0, jnp.where(rel < float(EXPERTS_PER_GROUP), 1.0, 0.0), 0.0) > 0.5
    el = jnp.where(in_grp, lg, -big)
    e1 = jnp.max(el, axis=1, keepdims=True)
    i1 = jnp.min(jnp.where(el == e1, lanef, big), axis=1, keepdims=True)
    el2 = jnp.where(lanef == i1, -big, el)
    e2 = jnp.max(el2, axis=1, keepdims=True)
    i2 = jnp.min(jnp.where(el2 == e2, lanef, big), axis=1, keepdims=True)
    tt = jnp.exp(e2 - e1)
    w1 = g_w / (1.0 + tt)
    w2 = g_w * tt / (1.0 + tt)
    base = float(EXPERT_LANE0) + g_idx * float(EXPERTS_PER_GROUP)
    a = jnp.minimum(i1, i2) - base
    b = jnp.maximum(i1, i2) - base
    pair = a * (7.0 - a) * 0.5 + (b - a - 1.0)
    first_is_lo = i1 < i2
    return (g_idx * float(PAIRS_PER_GROUP) + pair, jnp.where(first_is_lo, w1, w2), jnp.where(first_is_lo, w2, w1))


def _to_row_linear(dst_ref, src_ref, n_tiles):
    def body(g, carry):
        r = pl.multiple_of(g * 8, 8)
        for j in range(n_tiles):
            dst_ref[pl.ds(r * ROW_PITCH + j, 8, stride=ROW_PITCH), :] = src_ref[pl.ds(r, 8), j * LANES:(j + 1) * LANES]
        return carry

    lax.fori_loop(0, src_ref.shape[0] // 8, body, 0, unroll=4)


def _from_row_linear(dst_ref, src_ref, n_tiles, pitch):
    def body(g, carry):
        r = pl.multiple_of(g * 8, 8)
        for j in range(n_tiles):
            dst_ref[pl.ds(r, 8), j * LANES:(j + 1) * LANES] = src_ref[pl.ds(r * pitch + j, 8, stride=pitch), :]
        return carry

    lax.fori_loop(0, dst_ref.shape[0] // 8, body, 0, unroll=4)


def _pick(i, n_prompt_tiles, prompt_ref, sample_ref):
    return jnp.where(i < n_prompt_tiles, prompt_ref[...], sample_ref[...])


def _route_kernel(n_prompt_tiles, yp_ref, ys_ref, g_ref, wr_ref, br_ref, tri_ref, pay_ref, meta_ref, cnt_ref, carry):
    i = pl.program_id(0)

    @pl.when(i == 0)
    def _():
        carry[...] = jnp.zeros_like(carry)

    xf = _rms(_pick(i, n_prompt_tiles, yp_ref, ys_ref), g_ref[...])
    x_hi, x_lo = _split_bf16(xf)
    part = _mm(x_hi, wr_ref[...]) + _mm(x_lo, wr_ref[...])
    logits = part[:, :ROUTER_LANES] + part[:, ROUTER_LANES:] + br_ref[...]
    bucket, w_lo, w_hi = _route(logits)
    lane = lax.broadcasted_iota(jnp.int32, logits.shape, 1)
    onehot = jnp.where(lane.astype(F32) == bucket, 1.0, 0.0)
    before = _mm(tri_ref[...], onehot.astype(BF16)) + carry[...]
    rank = jnp.sum(onehot * before, axis=1, keepdims=True)
    carry[...] += jnp.sum(onehot, axis=0, keepdims=True)
    cnt_ref[...] = carry[...]
    meta = jnp.where(lane == 0, bucket, jnp.where(lane == 1, rank, 0.0))
    meta_ref[...] = jnp.transpose(meta)[:8, :].astype(jnp.int32)

    gates = jnp.where(lane == 0, w_lo, jnp.where(lane == 1, w_hi, 0.0))
    nj = D_MODEL // LANES
    for r in range(0, xf.shape[0], 8):
        for j in range(nj):
            pay_ref[pl.ds(r * ROW_PITCH + j, 8, stride=ROW_PITCH), :] = xf[r:r + 8, j * LANES:(j + 1) * LANES]
        pay_ref[pl.ds(r * ROW_PITCH + nj, 8, stride=ROW_PITCH), :] = gates[r:r + 8, :]


def _route_call(y_p, y_s, g, wr, br, tri):
    t = y_s.shape[0]
    npt = y_p.shape[0] // t
    n = y_p.shape[0] + t
    pidx = lambda i: (jnp.minimum(i, npt - 1), 0)
    return pl.pallas_call(
        functools.partial(_route_kernel, npt),
        grid=(npt + 1,),
        in_specs=[pl.BlockSpec((t, D_MODEL), pidx), _full(y_s.shape), _full(g.shape), _full(wr.shape),
                  _full(br.shape), _full(tri.shape)],
        out_specs=[pl.BlockSpec((t * ROW_PITCH, LANES), lambda i: (i, 0)),
                   pl.BlockSpec((8, t), lambda i: (0, i)), _full((1, ROUTER_LANES))],
        out_shape=[jax.ShapeDtypeStruct((n * ROW_PITCH, LANES), F32),
                   jax.ShapeDtypeStruct((8, n), jnp.int32), jax.ShapeDtypeStruct((1, ROUTER_LANES), F32)],
        scratch_shapes=[pltpu.VMEM((1, ROUTER_LANES), F32)],
        compiler_params=_params("arbitrary"),
        name="route",
    )(y_p, y_s, g, wr, br, tri)


def _permute_kernel(pos_ref, zstart_ref, nz_ref, used_ref, src_ref, dst_ref, zbuf, ring, zsem, in_sems, out_sems):
    zbuf[...] = jnp.zeros_like(zbuf)

    def zero_tile(start):
        return pltpu.make_async_copy(zbuf, dst_ref.at[pl.ds(pl.multiple_of(start * ROW_PITCH, 8), TM * ROW_PITCH)], zsem)

    for b in range(N_BUCKETS):
        @pl.when(nz_ref[b] > 0)
        def _():
            zero_tile(zstart_ref[b]).start()
    n_tiles = dst_ref.shape[0] // (TM * ROW_PITCH)

    def start_unused(i, carry):
        zero_tile(i * TM).start()
        return carry

    lax.fori_loop(used_ref[0], n_tiles, start_unused, 0)
    for b in range(N_BUCKETS):
        @pl.when(nz_ref[b] > 0)
        def _():
            zero_tile(0).wait()

    def wait_unused(i, carry):
        zero_tile(0).wait()
        return carry

    lax.fori_loop(used_ref[0], n_tiles, wait_unused, 0)

    chunk_rows = DMA_CHUNK * ROW_PITCH
    n_chunks = src_ref.shape[0] // chunk_rows

    def fetch(c, slot):
        return pltpu.make_async_copy(src_ref.at[pl.ds(pl.multiple_of(c * chunk_rows, 8), chunk_rows)], ring.at[slot],
                                     in_sems.at[slot])

    def drain(slot):
        pltpu.make_async_copy(ring.at[slot], dst_ref.at[pl.ds(0, chunk_rows)], out_sems.at[slot]).wait()

    fetch(0, 0).start()

    def step(c, carry):
        slot = lax.rem(c, RING)
        nxt = lax.rem(c + 1, RING)
        fetch(c, slot).wait()

        @pl.when(c >= RING - 1)
        def _():
            drain(nxt)

        @pl.when(c + 1 < n_chunks)
        def _():
            fetch(c + 1, nxt).start()

        def send(jj, carry2):
            row = pos_ref[c * DMA_CHUNK + jj] * ROW_PITCH
            pltpu.make_async_copy(ring.at[slot, pl.ds(jj * ROW_PITCH, ROW_PITCH)], dst_ref.at[pl.ds(row, ROW_PITCH)],
                                  out_sems.at[slot]).start()
            return carry2

        lax.fori_loop(0, DMA_CHUNK, send, 0, unroll=8)
        return carry

    lax.fori_loop(0, n_chunks, step, 0)
    for c in range(n_chunks - (RING - 1), n_chunks):
        drain(c % RING)


def _permute(pos, zstart, nz, used, payload, n_rows_out):
    assert payload.shape[0] // (DMA_CHUNK * ROW_PITCH) >= RING
    return pl.pallas_call(
        _permute_kernel,
        grid_spec=pltpu.PrefetchScalarGridSpec(
            num_scalar_prefetch=4, grid=(1,),
            in_specs=[pl.BlockSpec(memory_space=pl.ANY)],
            out_specs=pl.BlockSpec(memory_space=pl.ANY),
            scratch_shapes=[pltpu.VMEM((TM * ROW_PITCH, LANES), payload.dtype),
                            pltpu.VMEM((RING, DMA_CHUNK * ROW_PITCH, LANES), payload.dtype),
                            pltpu.SemaphoreType.DMA(()), pltpu.SemaphoreType.DMA((RING,)),
                            pltpu.SemaphoreType.DMA((RING,))]),
        out_shape=jax.ShapeDtypeStruct((n_rows_out * ROW_PITCH, LANES), payload.dtype),
        compiler_params=_params("arbitrary"),
        name="permute",
    )(pos, zstart, nz, used, payload)


def _silu(x):
    return x * _sigmoid(x)


def _experts_kernel(tix_ref, lo_ref, hi_ref, valid_ref, fresh_ref, xs_ref, wgl_ref, wgh_ref, wul_ref, wuh_ref,
                    wdl_ref, wdh_ref, ys_ref, wup_s, wdn_s, xbuf, ybuf):
    i = pl.program_id(0)

    @pl.when(fresh_ref[i] > 0)
    def _():
        for s, ref in enumerate((wgl_ref, wgh_ref, wul_ref, wuh_ref)):
            wup_s[s] = ref[0, 0].astype(BF16)
        for s, ref in enumerate((wdl_ref, wdh_ref)):
            wdn_s[s] = ref[0, 0].astype(BF16)

    @pl.when(valid_ref[i] == 0)
    def _():
        ys_ref[...] = jnp.zeros_like(ys_ref)

    @pl.when(valid_ref[i] > 0)
    def _():
        _from_row_linear(xbuf, xs_ref, ROW_PITCH, ROW_PITCH)
        x = xbuf[:, :D_MODEL].astype(BF16)
        gates = xbuf[:, D_MODEL:]
        h_lo = _silu(_mm(x, wup_s[0])) * _mm(x, wup_s[2]) * gates[:, 0:1]
        h_hi = _silu(_mm(x, wup_s[1])) * _mm(x, wup_s[3]) * gates[:, 1:2]
        ybuf[:, :D_MODEL] = _mm(h_lo.astype(BF16), wdn_s[0]) + _mm(h_hi.astype(BF16), wdn_s[1])
        ybuf[:, D_MODEL:] = jnp.zeros((TM, LANES), F32)
        _to_row_linear(ys_ref, ybuf, ROW_PITCH)


def _experts(layer, tix, lo, hi, valid, fresh, xs, wg, wu, wd):
    n_tiles = xs.shape[0] // (TM * ROW_PITCH)
    row = lambda i, tix, lo, hi, valid, fresh: (tix[i], 0)
    row_out = lambda i, tix, lo, hi, valid, fresh: (i, 0)
    e_lo = lambda i, tix, lo, hi, valid, fresh: (layer, lo[i], 0, 0)
    e_hi = lambda i, tix, lo, hi, valid, fresh: (layer, hi[i], 0, 0)
    up = pl.BlockSpec((1, 1, D_MODEL, D_EXPERT), e_lo), pl.BlockSpec((1, 1, D_MODEL, D_EXPERT), e_hi)
    down = pl.BlockSpec((1, 1, D_EXPERT, D_MODEL), e_lo), pl.BlockSpec((1, 1, D_EXPERT, D_MODEL), e_hi)
    return pl.pallas_call(
        _experts_kernel,
        grid_spec=pltpu.PrefetchScalarGridSpec(
            num_scalar_prefetch=5, grid=(n_tiles,),
            in_specs=[pl.BlockSpec((TM * ROW_PITCH, LANES), row), *up, *up, *down],
            out_specs=pl.BlockSpec((TM * ROW_PITCH, LANES), row_out),
            scratch_shapes=[pltpu.VMEM((4, D_MODEL, D_EXPERT), BF16), pltpu.VMEM((2, D_EXPERT, D_MODEL), BF16),
                            pltpu.VMEM((TM, PAY_WIDTH), F32), pltpu.VMEM((TM, PAY_WIDTH), F32)]),
        out_shape=jax.ShapeDtypeStruct(xs.shape, F32),
        compiler_params=_params("arbitrary"),
        name="experts",
    )(tix, lo, hi, valid, fresh, xs, wg, wg, wu, wu, wd, wd)


def _ple_kernel(n_prompt_tiles, pos_ref, yp_ref, ys_ref, moe_ref, pp_ref, ps_ref, g_ref, wg_ref, wp_ref, op_ref, os_ref,
                mring, mbuf, sems):
    i = pl.program_id(0)
    t = mbuf.shape[0]
    w = D_MODEL // LANES

    def gather(tile, slot):
        def body(jj, carry):
            row = pos_ref[tile * t + jj] * ROW_PITCH
            pltpu.make_async_copy(moe_ref.at[pl.ds(row, w)], mring.at[slot, pl.ds(jj * w, w)], sems.at[slot]).start()
            return carry

        lax.fori_loop(0, t, body, 0, unroll=8)

    def wait_tile(slot):
        pltpu.make_async_copy(moe_ref.at[pl.ds(0, t * w)], mring.at[slot], sems.at[slot]).wait()

    @pl.when(i == 0)
    def _():
        gather(0, 0)
        gather(1, 1)

    slot = lax.rem(i, PLE_RING)
    nslot = lax.rem(i + 2, PLE_RING)
    wait_tile(slot)
    _from_row_linear(mbuf, mring.at[slot], w, w)
    y2 = _pick(i, n_prompt_tiles, yp_ref, ys_ref) + mbuf[...]
    hn = _rms(y2, g_ref[...]).astype(BF16)
    gt = _sigmoid(_mm(hn, wg_ref[...]))
    pr = _mm(jnp.where(i < n_prompt_tiles, pp_ref[0], ps_ref[...]).astype(BF16), wp_ref[...])
    y3 = y2 + gt * pr
    nxt = jnp.minimum(i + 2, n_prompt_tiles) * t
    for jj in range(t):
        row = pos_ref[nxt + jj] * ROW_PITCH
        pltpu.make_async_copy(moe_ref.at[pl.ds(row, w)], mring.at[nslot, pl.ds(jj * w, w)], sems.at[nslot]).start()

    @pl.when(i < n_prompt_tiles)
    def _():
        op_ref[...] = y3

    @pl.when(i >= n_prompt_tiles)
    def _():
        os_ref[...] = y3
        wait_tile(lax.rem(i + 1, PLE_RING))
        wait_tile(nslot)


def _ple(layer, pos, y_p, y_s, ys_sorted, p_p, p_s, g, wg, wp):
    t = y_s.shape[0]
    npt = y_p.shape[0] // t
    w = D_MODEL // LANES
    pidx = lambda i, pos: (jnp.minimum(i, npt - 1), 0)
    full = lambda shape: pl.BlockSpec(shape, lambda i, pos: (0,) * len(shape))
    return pl.pallas_call(
        functools.partial(_ple_kernel, npt),
        grid_spec=pltpu.PrefetchScalarGridSpec(
            num_scalar_prefetch=1, grid=(npt + 1,),
            in_specs=[pl.BlockSpec((t, D_MODEL), pidx), full(y_s.shape), pl.BlockSpec(memory_space=pl.ANY),
                      pl.BlockSpec((1, t, PLE_DIM), lambda i, pos: (layer, jnp.minimum(i, npt - 1), 0)),
                      full(p_s.shape), full(g.shape), full(wg.shape), full(wp.shape)],
            out_specs=[pl.BlockSpec((t, D_MODEL), pidx), full(y_s.shape)],
            scratch_shapes=[pltpu.VMEM((PLE_RING, t * w, LANES), F32), pltpu.VMEM((t, D_MODEL), F32),
                            pltpu.SemaphoreType.DMA((PLE_RING,))]),
        out_shape=[jax.ShapeDtypeStruct(y_p.shape, F32), jax.ShapeDtypeStruct(y_s.shape, F32)],
        compiler_params=_params("arbitrary"),
        name="ple",
    )(pos, y_p, y_s, ys_sorted, p_p, p_s, g, wg, wp)


def _bucket_experts():
    pairs = [(a, b) for a in range(EXPERTS_PER_GROUP) for b in range(a + 1, EXPERTS_PER_GROUP)]
    lo = [g * EXPERTS_PER_GROUP + a for g in range(N_EXPERT_GROUPS) for a, _ in pairs]
    hi = [g * EXPERTS_PER_GROUP + b for g in range(N_EXPERT_GROUPS) for _, b in pairs]
    return jnp.array(lo, jnp.int32), jnp.array(hi, jnp.int32)


def _moe_ple(layer, y_p, y_s, gffn, wr, br, wg, wu, wd, p_p, p_s, gple, pwg, pwp):
    t = y_s.shape[0]
    n = y_p.shape[0] + t
    tri = jnp.asarray(np.tri(t, t, -1, dtype=np.float32), BF16)
    payload, meta, counts = _route_call(y_p, y_s, gffn, wr, br, tri)

    bucket, rank = meta[0], meta[1]
    cnt = counts[0, :N_BUCKETS].astype(jnp.int32)
    padded = (cnt + TM - 1) // TM * TM
    ends = jnp.cumsum(padded)
    starts = ends - padded
    in_bucket = bucket[:, None] == jnp.arange(N_BUCKETS, dtype=jnp.int32)[None, :]
    pos = rank + jnp.sum(jnp.where(in_bucket, starts[None, :], 0), axis=1)
    n_rows = n + N_BUCKETS * TM
    n_tiles = n_rows // TM
    tile_start = jnp.arange(n_tiles, dtype=jnp.int32) * TM
    valid = tile_start < ends[-1]
    used = (ends[-1] // TM).astype(jnp.int32)
    tix = jnp.arange(n_tiles, dtype=jnp.int32)
    tile_bucket = jnp.sum((ends[None, :] <= (jnp.minimum(tix, used - 1) * TM)[:, None]).astype(jnp.int32), axis=1)
    prev_bucket = jnp.concatenate([jnp.full((1,), -1, jnp.int32), tile_bucket[:-1]])
    fresh = valid & (tile_bucket != prev_bucket)
    first_tile = jnp.where(fresh, tix, n_tiles)
    next_first = lax.cummin(jnp.concatenate([first_tile[1:], jnp.full((1,), n_tiles, jnp.int32)]), reverse=True)
    next_bucket = jnp.where(next_first < n_tiles, jnp.take(tile_bucket, jnp.minimum(next_first, n_tiles - 1)),
                            tile_bucket)
    resident = jnp.where(fresh, tile_bucket, next_bucket)
    lo_tab, hi_tab = _bucket_experts()
    lo = jnp.take(lo_tab, resident)
    hi = jnp.take(hi_tab, resident)

    xs = _permute(pos, (ends - TM).astype(jnp.int32), (cnt > 0).astype(jnp.int32), used.reshape(1), payload, n_rows)
    ys = _experts(layer, jnp.minimum(tix, used - 1), lo, hi, valid.astype(jnp.int32), fresh.astype(jnp.int32), xs, wg,
                  wu, wd)
    return _ple(layer, pos, y_p, y_s, ys, p_p, p_s, gple, pwg, pwp)


def _split_bf16(a):
    hi = a.astype(BF16)
    lo = (a - hi.astype(F32)).astype(BF16)
    return hi, lo


def _head_norm(a, ind_ref, indt_ref, gain):
    hi, lo = _split_bf16(a * a)
    ss = _mm(hi, ind_ref[...]) + _mm(lo, ind_ref[...])
    inv = lax.rsqrt(ss * (1.0 / HEAD_DIM) + EPS)
    invb = _mm(jnp.concatenate(_split_bf16(inv), axis=1), indt_ref[...])
    return a * invb * gain


def _qkv_kernel(x_ref, g_ref, w_ref, iq_ref, iqt_ref, ik_ref, ikt_ref, qg_ref, kg_ref, q_ref, k_ref, v_ref):
    h = _rms(x_ref[...], g_ref[...]).astype(BF16)
    qkv = _mm(h, w_ref[...])
    nq = N_HEADS * HEAD_DIM
    q = _head_norm(qkv[:, :nq], iq_ref, iqt_ref, qg_ref[...])
    k = _head_norm(qkv[:, nq:nq + KV_DIM], ik_ref, ikt_ref, kg_ref[...])
    q_ref[...] = (q * (HEAD_DIM ** -0.5)).astype(BF16)
    k_ref[...] = k
    v_ref[...] = qkv[:, nq + KV_DIM:]


def _qkv(x, g, w, iq, iqt, ik, ikt, qg, kg):
    n = x.shape[0]
    t = _row_tile(n, big=True)
    row = lambda i: (i, 0)
    return pl.pallas_call(
        _qkv_kernel,
        grid=(n // t,),
        in_specs=[pl.BlockSpec((t, D_MODEL), row)] + [_full(a.shape) for a in (g, w, iq, iqt, ik, ikt, qg, kg)],
        out_specs=[pl.BlockSpec((t, N_HEADS * HEAD_DIM), row), pl.BlockSpec((t, KV_DIM), row),
                   pl.BlockSpec((t, KV_DIM), row)],
        out_shape=[jax.ShapeDtypeStruct((n, N_HEADS * HEAD_DIM), BF16), jax.ShapeDtypeStruct((n, KV_DIM), F32),
                   jax.ShapeDtypeStruct((n, KV_DIM), F32)],
        compiler_params=_params("arbitrary"),
        name="qkv",
    )(x, g, w, iq, iqt, ik, ikt, qg, kg)


def _dup_heads(a):
    out = []
    for s in range(KV_DIM // LANES):
        sl = a[:, s * LANES:(s + 1) * LANES]
        sw = pltpu.roll(sl, HEAD_DIM, axis=1)
        low = lax.broadcasted_iota(jnp.int32, sl.shape, 1) < HEAD_DIM
        out.append(jnp.where(low, sl, sw))
        out.append(jnp.where(low, sw, sl))
    return jnp.concatenate(out, axis=1).astype(BF16)


def _attend(q_rows, k2, v2, bias_of, sink_of, extra_mask):
    m_rows = q_rows.shape[0]
    low_q = lax.broadcasted_iota(jnp.int32, (m_rows, LANES), 1) < HEAD_DIM
    low_k = lax.broadcasted_iota(jnp.int32, (k2.shape[0], LANES), 1) < HEAD_DIM
    zero_q = jnp.zeros((m_rows, LANES), BF16)
    zero_k = jnp.zeros((k2.shape[0], LANES), BF16)
    slabs = []
    for g in range(N_KV_HEADS):
        kg = k2[:, g * LANES:(g + 1) * LANES]
        vg = v2[:, g * LANES:(g + 1) * LANES]
        v_lo = jnp.where(low_k, vg, zero_k)
        v_hi = jnp.where(low_k, zero_k, vg)
        lhs = []
        for a in range(GROUP):
            h = g * GROUP + a
            qs = q_rows[:, (h // 2) * LANES:(h // 2 + 1) * LANES]
            lhs.append(jnp.where(low_q, qs, zero_q) if h % 2 == 0 else jnp.where(low_q, zero_q, qs))
        s = lax.dot_general(jnp.concatenate(lhs, axis=0), kg, (((1,), (1,)), ((), ())), preferred_element_type=F32)
        probs, rinv = [], []
        for a in range(GROUP):
            h = g * GROUP + a
            sa = s[a * m_rows:(a + 1) * m_rows] + bias_of(h)
            if extra_mask is not None:
                sa = jnp.where(extra_mask, NEG_INF, sa)
            sink = sink_of(h)
            m = jnp.maximum(jnp.max(sa, axis=1, keepdims=True), sink)
            p = jnp.exp(sa - m)
            den = jnp.sum(p, axis=1, keepdims=True) + jnp.exp(sink - m)
            probs.append(p.astype(BF16))
            rinv.append(1.0 / den)
        for sp in range(GROUP // 2):
            o = _mm(probs[2 * sp], v_lo) + _mm(probs[2 * sp + 1], v_hi)
            slabs.append(o * jnp.where(low_q, rinv[2 * sp], rinv[2 * sp + 1]))
    return jnp.concatenate(slabs, axis=1)


def _attn_p_kernel(sink_ref, q_ref, k_ref, v_ref, x_ref, bias_ref, wo_ref, y_ref, kbuf, vbuf, obuf):
    t = q_ref.shape[0]
    i = pl.program_id(0)

    @pl.when(i == 0)
    def _():
        kbuf[0:WINDOW, :] = jnp.zeros((WINDOW, 2 * KV_DIM), BF16)
        vbuf[0:WINDOW, :] = jnp.zeros((WINDOW, 2 * KV_DIM), BF16)

    @pl.when(i > 0)
    def _():
        kbuf[0:WINDOW, :] = kbuf[t:t + WINDOW, :]
        vbuf[0:WINDOW, :] = vbuf[t:t + WINDOW, :]

    kbuf[WINDOW:, :] = _dup_heads(k_ref[...])
    vbuf[WINDOW:, :] = _dup_heads(v_ref[...])
    col = lax.broadcasted_iota(jnp.int32, (WINDOW, 2 * WINDOW), 1)
    for j in range(t // WINDOW):
        rows = slice(j * WINDOW, (j + 1) * WINDOW)
        keys = slice(j * WINDOW, (j + 2) * WINDOW)
        extra = jnp.logical_and(i == 0, col < WINDOW) if j == 0 else None
        o = _attend(q_ref[rows, :], kbuf[keys, :], vbuf[keys, :], lambda h: bias_ref[h], lambda h: sink_ref[h], extra)
        obuf[rows, :] = o.astype(BF16)
    y_ref[...] = x_ref[...] + _mm(obuf[...], wo_ref[...])


def _attn_p(sinks, q, k, v, x, bias, wo):
    n = q.shape[0]
    t = _row_tile(n)
    row = lambda i: (i, 0)
    return pl.pallas_call(
        _attn_p_kernel,
        grid=(n // t,),
        in_specs=[pl.BlockSpec(memory_space=pltpu.SMEM),
                  pl.BlockSpec((t, N_HEADS * HEAD_DIM), row), pl.BlockSpec((t, KV_DIM), row),
                  pl.BlockSpec((t, KV_DIM), row), pl.BlockSpec((t, D_MODEL), row),
                  _full(bias.shape), _full(wo.shape)],
        out_specs=pl.BlockSpec((t, D_MODEL), row),
        out_shape=jax.ShapeDtypeStruct((n, D_MODEL), F32),
        scratch_shapes=[pltpu.VMEM((t + WINDOW, 2 * KV_DIM), BF16), pltpu.VMEM((t + WINDOW, 2 * KV_DIM), BF16),
                        pltpu.VMEM((t, N_HEADS * HEAD_DIM), BF16)],
        compiler_params=_params("arbitrary"),
        name="attn_prompt",
    )(sinks, q, k, v, x, bias, wo)


def _attn_s_kernel(sink_ref, q_ref, k_ref, v_ref, ck_ref, cv_ref, bias_ref, o_ref, nk_ref, nv_ref, shift_buf):
    pad = bias_ref.shape[2] - ck_ref.shape[0] - k_ref.shape[0]
    zpad = jnp.zeros((pad, KV_DIM), F32)
    k2 = _dup_heads(jnp.concatenate([ck_ref[...], k_ref[...], zpad], axis=0))
    v2 = _dup_heads(jnp.concatenate([cv_ref[...], v_ref[...], zpad], axis=0))
    o = _attend(q_ref[...], k2, v2, lambda h: bias_ref[h], lambda h: sink_ref[h], None)
    o_ref[...] = o.astype(BF16)
    t_new = k_ref.shape[0] // SAMPLE_SEQS
    for new_ref, old_ref, add_ref in ((nk_ref, ck_ref, k_ref), (nv_ref, cv_ref, v_ref)):
        for b in range(SAMPLE_SEQS):
            r = b * WINDOW
            shift_buf[0:WINDOW - t_new, :] = old_ref[r + t_new:r + WINDOW, :]
            shift_buf[WINDOW - t_new:WINDOW, :] = add_ref[b * t_new:(b + 1) * t_new, :]
            new_ref[b] = jnp.transpose(shift_buf[...])


def _attn_s(sinks, q, k, v, ck, cv, bias, t_new):
    n = q.shape[0]
    rows = SAMPLE_SEQS * t_new
    crow = SAMPLE_SEQS * WINDOW
    row = lambda i: (i, 0)
    return pl.pallas_call(
        _attn_s_kernel,
        grid=(n // rows,),
        in_specs=[pl.BlockSpec(memory_space=pltpu.SMEM),
                  pl.BlockSpec((rows, N_HEADS * HEAD_DIM), row), pl.BlockSpec((rows, KV_DIM), row),
                  pl.BlockSpec((rows, KV_DIM), row), pl.BlockSpec((crow, KV_DIM), row),
                  pl.BlockSpec((crow, KV_DIM), row), _full(bias.shape)],
        out_specs=[pl.BlockSpec((rows, N_HEADS * HEAD_DIM), row),
                   pl.BlockSpec((SAMPLE_SEQS, KV_DIM, WINDOW), lambda i: (i, 0, 0)),
                   pl.BlockSpec((SAMPLE_SEQS, KV_DIM, WINDOW), lambda i: (i, 0, 0))],
        out_shape=[jax.ShapeDtypeStruct((n, N_HEADS * HEAD_DIM), BF16),
                   jax.ShapeDtypeStruct((ck.shape[0] // WINDOW, KV_DIM, WINDOW), F32),
                   jax.ShapeDtypeStruct((cv.shape[0] // WINDOW, KV_DIM, WINDOW), F32)],
        scratch_shapes=[pltpu.VMEM((WINDOW, KV_DIM), F32)],
        compiler_params=_params("arbitrary"),
        name="attn_sample",
    )(sinks, q, k, v, ck, cv, bias)


def _proj_res_kernel(o_ref, x_ref, w_ref, y_ref):
    y_ref[...] = x_ref[...] + _mm(o_ref[...], w_ref[...])


def _proj_res(o, x, w):
    n = o.shape[0]
    t = _row_tile(n)
    row = lambda i: (i, 0)
    return pl.pallas_call(
        _proj_res_kernel,
        grid=(n // t,),
        in_specs=[pl.BlockSpec((t, o.shape[1]), row), pl.BlockSpec((t, D_MODEL), row), _full(w.shape)],
        out_specs=pl.BlockSpec((t, D_MODEL), row),
        out_shape=jax.ShapeDtypeStruct((n, D_MODEL), F32),
        compiler_params=_params("arbitrary"),
        name="proj_res",
    )(o, x, w)


def _alibi_slopes():
    return np.exp2(-8.0 * np.arange(1, N_HEADS + 1, dtype=np.float64) / N_HEADS).astype(np.float32)


def _band_bias(dist, allowed):
    b = -(_alibi_slopes()[:, None, None] * dist.astype(np.float32)[None])
    return jnp.asarray(np.where(allowed[None], b, np.float32(NEG_INF)).astype(np.float32))


def _prompt_bias():
    dist = WINDOW + np.arange(WINDOW)[:, None] - np.arange(2 * WINDOW)[None, :]
    return _band_bias(dist, (dist >= 0) & (dist <= WINDOW))


def _sample_bias(t_new, n_cols):
    c = np.arange(n_cols)
    n_cache = SAMPLE_SEQS * WINDOW
    n_new = SAMPLE_SEQS * t_new
    is_cache = c < n_cache
    is_new = (c >= n_cache) & (c < n_cache + n_new)
    seq_c = np.where(is_cache, c // WINDOW, (c - n_cache) // t_new)
    pos_c = np.where(is_cache, c % WINDOW, WINDOW + (c - n_cache) % t_new)
    r = np.arange(n_new)
    seq_r, tok_r = r // t_new, r % t_new
    dist = WINDOW + tok_r[:, None] - pos_c[None, :]
    allowed = (seq_r[:, None] == seq_c[None, :]) & (is_cache | is_new)[None, :] & (dist >= 0) & (dist <= WINDOW)
    return _band_bias(dist, allowed)


def _head_indicator(n_heads):
    ch = np.arange(n_heads * HEAD_DIM) // HEAD_DIM
    ind = (ch[:, None] == np.arange(LANES)[None, :]).astype(np.float32)
    return jnp.asarray(ind, BF16), jnp.asarray(np.concatenate([ind.T, ind.T], axis=0), BF16)


def kernel(x_prompt, x_sample, state_conv, cache_k, cache_v, p_prompt, p_sample, norm_mix, norm_ffn, norm_ple,
           conv_w_in, conv_b_in, conv_w_dw, conv_b_dw, conv_ln_g, conv_ln_b, conv_w_out, conv_b_out, attn_w_qkv,
           attn_q_norm, attn_k_norm, attn_sinks, attn_w_o, moe_w_rg, moe_b_rg, moe_w_re, moe_b_re, moe_w_gate,
           moe_w_up, moe_w_down, ple_w_gate, ple_w_proj):
    bp, seq, d = x_prompt.shape
    bs, t_new, _ = x_sample.shape
    assert bp == 1 and d == D_MODEL and seq % WINDOW == 0 and bs % SAMPLE_SEQS == 0
    assert seq % (bs * t_new) == 0 and (bs * t_new) % DMA_CHUNK == 0
    depth = norm_mix.shape[0]
    row2 = lambda a: a.reshape(1, -1)

    y_p = x_prompt.reshape(seq, d)
    y_s = x_sample.reshape(bs * t_new, d)
    conv_p, conv_s, k_p, v_p, k_s, v_s = [], [], [], [], [], []

    for i in range(depth):
        j = i // 2
        g_mix = row2(norm_mix[i])
        if i % 2 == 0:
            w_in = conv_w_in[j].astype(BF16)
            b_in = row2(conv_b_in[j])
            tail = (conv_w_dw[j], row2(conv_b_dw[j]), row2(conv_ln_g[j]), row2(conv_ln_b[j]),
                    conv_w_out[j].astype(BF16), row2(conv_b_out[j]))
            u_p = _conv_in(y_p, g_mix, w_in, b_in)
            u_s = _conv_in(y_s, g_mix, w_in, b_in)
            conv_p.append(u_p[seq - CONV_STATE:].reshape(1, CONV_STATE, d))
            y_p = _conv_out_p(u_p, y_p, *tail)
            ys_t, new_state = _conv_out_s(state_conv[j], u_s.reshape(bs, t_new, d),
                                          y_s.reshape(bs, t_new, d).transpose(1, 0, 2), *tail)
            conv_s.append(new_state)
            y_s = ys_t.transpose(1, 0, 2).reshape(bs * t_new, d)
        else:
            w_qkv = attn_w_qkv[j].astype(BF16)
            w_o = attn_w_o[j].astype(BF16)
            iq, iqt = _head_indicator(N_HEADS)
            ik, ikt = _head_indicator(N_KV_HEADS)
            qg = row2(jnp.tile(attn_q_norm[j], N_HEADS))
            kg = row2(jnp.tile(attn_k_norm[j], N_KV_HEADS))
            sinks = attn_sinks[j]
            q1, k1, v1 = _qkv(y_p, g_mix, w_qkv, iq, iqt, ik, ikt, qg, kg)
            q2, k2, v2 = _qkv(y_s, g_mix, w_qkv, iq, iqt, ik, ikt, qg, kg)
            k_p.append(k1[seq - WINDOW:].reshape(1, WINDOW, N_KV_HEADS, HEAD_DIM))
            v_p.append(v1[seq - WINDOW:].reshape(1, WINDOW, N_KV_HEADS, HEAD_DIM))
            y_p = _attn_p(sinks, q1, k1, v1, y_p, _prompt_bias(), w_o)
            n_cols = -(-(SAMPLE_SEQS * (WINDOW + t_new)) // LANES) * LANES
            o_s, nk, nv = _attn_s(sinks, q2, k2, v2, cache_k[j].reshape(bs * WINDOW, KV_DIM),
                                  cache_v[j].reshape(bs * WINDOW, KV_DIM), _sample_bias(t_new, n_cols), t_new)
            k_s.append(nk.reshape(bs, N_KV_HEADS, HEAD_DIM, WINDOW).transpose(0, 3, 1, 2))
            v_s.append(nv.reshape(bs, N_KV_HEADS, HEAD_DIM, WINDOW).transpose(0, 3, 1, 2))
            y_s = _proj_res(o_s, y_s, w_o)

        w_r = jnp.zeros((d, ROUTER_LANES), F32)
        w_r = w_r.at[:, :N_EXPERT_GROUPS].set(moe_w_rg[i]).at[:, EXPERT_LANE0:EXPERT_LANE0 + N_EXPERTS].set(moe_w_re[i])
        b_r = jnp.zeros((1, ROUTER_LANES), F32)
        b_r = b_r.at[0, :N_EXPERT_GROUPS].set(moe_b_rg[i]).at[0, EXPERT_LANE0:EXPERT_LANE0 + N_EXPERTS].set(moe_b_re[i])
        moe = (row2(norm_ffn[i]), jnp.concatenate(_split_bf16(w_r), axis=1), b_r, moe_w_gate, moe_w_up, moe_w_down)
        ple = (row2(norm_ple[i]), ple_w_gate[i].astype(BF16), ple_w_proj[i].astype(BF16))
        y_p, y_s = _moe_ple(i, y_p, y_s, *moe, p_prompt.reshape(depth, seq, PLE_DIM),
                            p_sample[i].reshape(bs * t_new, PLE_DIM), *ple)

    return (y_p.reshape(1, seq, d), y_s.reshape(bs, t_new, d), jnp.stack(conv_p), jnp.stack(conv_s),
            jnp.stack(k_p), jnp.stack(v_p), jnp.stack(k_s), jnp.stack(v_s))
```

```python
import functools

import numpy as np

import jax
import jax.numpy as jnp
from jax import lax
from jax.experimental import pallas as pl
from jax.experimental.pallas import tpu as pltpu

F32 = jnp.float32
BF16 = jnp.bfloat16

D_MODEL = 1024
PLE_DIM = 256
CONV_WIDTH = 31
CONV_STATE = CONV_WIDTH - 1
N_HEADS = 16
N_KV_HEADS = 4
HEAD_DIM = 64
GROUP = N_HEADS // N_KV_HEADS
WINDOW = 128
KV_DIM = N_KV_HEADS * HEAD_DIM
N_EXPERT_GROUPS = 4
EXPERTS_PER_GROUP = 4
N_EXPERTS = 16
D_EXPERT = 256
EPS = 1e-6
NEG_INF = -1e30

LANES = 128
ROUTER_LANES = LANES
EXPERT_LANE0 = N_EXPERT_GROUPS
HALO = 32
CONV_ROWS = 32
NORM_ROWS = 512
CONV_PITCH = D_MODEL // LANES + 1
SAMPLE_SEQS = 8
PAIRS_PER_GROUP = EXPERTS_PER_GROUP * (EXPERTS_PER_GROUP - 1) // 2
N_BUCKETS = N_EXPERT_GROUPS * PAIRS_PER_GROUP
TM = 256
PAY_WIDTH = D_MODEL + LANES
ROW_PITCH = PAY_WIDTH // LANES
DMA_CHUNK = 512
RING = 3
PLE_RING = 3
VMEM_LIMIT = 48 * 1024 * 1024


def _row_tile(n, big=False):
    if big and n % 1024 == 0:
        return 1024
    return 512 if n % 512 == 0 else n


def _params(*sem):
    return pltpu.CompilerParams(dimension_semantics=sem, vmem_limit_bytes=VMEM_LIMIT)


def _full(shape):
    nd = len(shape)
    return pl.BlockSpec(shape, lambda *_: (0,) * nd)


def _rms(x, g):
    ms = jnp.mean(x * x, axis=-1, keepdims=True)
    return x * lax.rsqrt(ms + EPS) * g


def _sigmoid(x):
    return 1.0 / (1.0 + jnp.exp(-x))


def _mm(a, b):
    return jnp.dot(a, b, preferred_element_type=F32)


def _conv_in_kernel(x_ref, g_ref, w_ref, b_ref, u_ref):
    h = _rms(x_ref[...], g_ref[...]).astype(BF16)
    z = _mm(h, w_ref[...]) + b_ref[...]
    u_ref[...] = z[:, :D_MODEL] * _sigmoid(z[:, D_MODEL:])


def _conv_in(x, g, w, b):
    n = x.shape[0]
    t = _row_tile(n, big=True)
    return pl.pallas_call(
        _conv_in_kernel,
        grid=(n // t,),
        in_specs=[pl.BlockSpec((t, D_MODEL), lambda i: (i, 0)), _full(g.shape), _full(w.shape), _full(b.shape)],
        out_specs=pl.BlockSpec((t, D_MODEL), lambda i: (i, 0)),
        out_shape=jax.ShapeDtypeStruct((n, D_MODEL), F32),
        compiler_params=_params("arbitrary"),
        name="conv_in",
    )(x, g, w, b)


def _ln_silu(c, g, b):
    mu = jnp.mean(c, axis=-1, keepdims=True)
    xc = c - mu
    var = jnp.mean(xc * xc, axis=-1, keepdims=True)
    cn = xc * lax.rsqrt(var + EPS) * g + b
    return cn * _sigmoid(cn)


def _conv_out_p_kernel(u_ref, halo_ref, x_ref, wdw_ref, bdw_ref, lng_ref, lnb_ref, wout_ref, bout_ref, y_ref,
                       ubuf, cbuf, hbuf):
    t = u_ref.shape[0]
    i = pl.program_id(0)
    nj = D_MODEL // LANES

    def put(r, j, val):
        ubuf[pl.ds(r * CONV_PITCH + j, 8, stride=CONV_PITCH), :] = val

    for r in range(0, HALO, 8):
        for j in range(nj):
            put(r, j, jnp.where(i > 0, halo_ref[r:r + 8, j * LANES:(j + 1) * LANES], 0.0))

    def fill(rr, carry):
        r = pl.multiple_of(rr * 8, 8)
        for j in range(nj):
            put(r + HALO, j, u_ref[pl.ds(r, 8), j * LANES:(j + 1) * LANES])
        return carry

    lax.fori_loop(0, t // 8, fill, 0, unroll=4)

    def conv_chunk(rr, carry):
        r0 = pl.multiple_of(rr * CONV_ROWS, CONV_ROWS)
        for j in range(nj):
            lanes = slice(j * LANES, (j + 1) * LANES)
            accs = [None] * (CONV_ROWS // 8)
            for k in range(CONV_WIDTH):
                wk = wdw_ref[k:k + 1, lanes]
                for q in range(CONV_ROWS // 8):
                    r = r0 + (HALO - CONV_STATE + k + 8 * q)
                    term = wk * ubuf[pl.ds(r * CONV_PITCH + j, 8, stride=CONV_PITCH), :]
                    accs[q] = term if k == 0 else accs[q] + term
            for q in range(CONV_ROWS // 8):
                cbuf[pl.ds(r0 + 8 * q, 8), lanes] = accs[q]
        return carry

    lax.fori_loop(0, t // CONV_ROWS, conv_chunk, 0)

    def norm_chunk(rr, carry):
        r0 = pl.multiple_of(rr * NORM_ROWS, NORM_ROWS)
        c = cbuf[pl.ds(r0, NORM_ROWS), :] + bdw_ref[...]
        hbuf[pl.ds(r0, NORM_ROWS), :] = _ln_silu(c, lng_ref[...], lnb_ref[...]).astype(BF16)
        return carry

    lax.fori_loop(0, t // NORM_ROWS, norm_chunk, 0)
    y_ref[...] = x_ref[...] + _mm(hbuf[...], wout_ref[...]) + bout_ref[...]


def _conv_out_p(u, x, wdw, bdw, lng, lnb, wout, bout):
    n = u.shape[0]
    t = _row_tile(n)
    hb = t // HALO
    row = lambda i: (i, 0)
    return pl.pallas_call(
        _conv_out_p_kernel,
        grid=(n // t,),
        in_specs=[pl.BlockSpec((t, D_MODEL), row),
                  pl.BlockSpec((HALO, D_MODEL), lambda i: (jnp.maximum(i * hb - 1, 0), 0)),
                  pl.BlockSpec((t, D_MODEL), row),
                  _full(wdw.shape), _full(bdw.shape), _full(lng.shape), _full(lnb.shape),
                  _full(wout.shape), _full(bout.shape)],
        out_specs=pl.BlockSpec((t, D_MODEL), row),
        out_shape=jax.ShapeDtypeStruct((n, D_MODEL), F32),
        scratch_shapes=[pltpu.VMEM(((t + HALO) * CONV_PITCH, LANES), F32), pltpu.VMEM((t, D_MODEL), F32),
                        pltpu.VMEM((t, D_MODEL), BF16)],
        compiler_params=_params("arbitrary"),
        name="conv_out_prompt",
    )(u, u, x, wdw, bdw, lng, lnb, wout, bout)


def _conv_out_s_kernel(st_ref, u_ref, x_ref, wdw_ref, bdw_ref, lng_ref, lnb_ref, wout_ref, bout_ref, y_ref, ns_ref,
                       wbuf, cbuf):
    bb, nt, _ = u_ref.shape
    win = CONV_STATE + nt
    nj = D_MODEL // LANES
    seq_pitch = win * CONV_PITCH

    def put(b, r0, rows, j, val):
        wbuf[pl.ds((b * win + r0) * CONV_PITCH + j, rows, stride=CONV_PITCH), :] = val

    for b in range(bb):
        for j in range(nj):
            lanes = slice(j * LANES, (j + 1) * LANES)
            for r0 in range(0, CONV_STATE, 8):
                rows = min(8, CONV_STATE - r0)
                put(b, r0, rows, j, st_ref[b, r0:r0 + rows, lanes])
            put(b, CONV_STATE, nt, j, u_ref[b, :, lanes])

    for b in range(bb):
        for r0 in range(0, CONV_STATE, 8):
            rows = min(8, CONV_STATE - r0)
            for j in range(nj):
                src = (b * win + nt + r0) * CONV_PITCH + j
                ns_ref[b, r0:r0 + rows, j * LANES:(j + 1) * LANES] = wbuf[pl.ds(src, rows, stride=CONV_PITCH), :]

    for t in range(nt):
        for b0 in range(0, bb, 8):
            for j in range(nj):
                lanes = slice(j * LANES, (j + 1) * LANES)
                acc = None
                for k in range(CONV_WIDTH):
                    start = (b0 * win + t + k) * CONV_PITCH + j
                    term = wdw_ref[k:k + 1, lanes] * wbuf[pl.ds(start, 8, stride=seq_pitch), :]
                    acc = term if acc is None else acc + term
                cbuf[t * bb + b0:t * bb + b0 + 8, lanes] = acc
    c = cbuf[...] + bdw_ref[...]
    h = _ln_silu(c, lng_ref[...], lnb_ref[...]).astype(BF16)
    y = _mm(h, wout_ref[...]) + bout_ref[...]
    for t in range(nt):
        y_ref[t] = x_ref[t] + y[t * bb:(t + 1) * bb]


def _conv_out_s(state, u, x, wdw, bdw, lng, lnb, wout, bout):
    b, nt, _ = u.shape
    bb = 16 if b % 16 == 0 else b
    assert bb % 8 == 0
    win = CONV_STATE + nt
    seq = lambda i: (i, 0, 0)
    tm = lambda i: (0, i, 0)
    return pl.pallas_call(
        _conv_out_s_kernel,
        grid=(b // bb,),
        in_specs=[pl.BlockSpec((bb, CONV_STATE, D_MODEL), seq), pl.BlockSpec((bb, nt, D_MODEL), seq),
                  pl.BlockSpec((nt, bb, D_MODEL), tm),
                  _full(wdw.shape), _full(bdw.shape), _full(lng.shape), _full(lnb.shape),
                  _full(wout.shape), _full(bout.shape)],
        out_specs=[pl.BlockSpec((nt, bb, D_MODEL), tm), pl.BlockSpec((bb, CONV_STATE, D_MODEL), seq)],
        out_shape=[jax.ShapeDtypeStruct((nt, b, D_MODEL), F32), jax.ShapeDtypeStruct((b, CONV_STATE, D_MODEL), F32)],
        scratch_shapes=[pltpu.VMEM((bb * win * CONV_PITCH, LANES), F32), pltpu.VMEM((nt * bb, D_MODEL), F32)],
        compiler_params=_params("arbitrary"),
        name="conv_out_sample",
    )(state, u, x, wdw, bdw, lng, lnb, wout, bout)


def _route(lg):
    big = 3.0e38
    lane = lax.broadcasted_iota(jnp.int32, lg.shape, 1)
    lanef = lane.astype(F32)
    is_g = lane < N_EXPERT_GROUPS
    gl = jnp.where(is_g, lg, -big)
    gmax = jnp.max(gl, axis=1, keepdims=True)
    gsum = jnp.sum(jnp.where(is_g, jnp.exp(gl - gmax), 0.0), axis=1, keepdims=True)
    g_w = 1.0 / gsum
    g_idx = jnp.min(jnp.where(gl == gmax, lanef, big), axis=1, keepdims=True)
    rel = lanef - float(EXPERT_LANE0) - g_idx * float(EXPERTS_PER_GROUP)
    in_grp = jnp.where(rel >= 0.0, jnp.where(rel < float(EXPERTS_PER_GROUP), 1.0, 0.0), 0.0) > 0.5
    el = jnp.where(in_grp, lg, -big)
    e1 = jnp.max(el, axis=1, keepdims=True)
    i1 = jnp.min(jnp.where(el == e1, lanef, big), axis=1, keepdims=True)
    el2 = jnp.where(lanef == i1, -big, el)
    e2 = jnp.max(el2, axis=1, keepdims=True)
    i2 = jnp.min(jnp.where(el2 == e2, lanef, big), axis=1, keepdims=True)
    tt = jnp.exp(e2 - e1)
    w1 = g_w / (1.0 + tt)
    w2 = g_w * tt / (1.0 + tt)
    base = float(EXPERT_LANE0) + g_idx * float(EXPERTS_PER_GROUP)
    a = jnp.minimum(i1, i2) - base
    b = jnp.maximum(i1, i2) - base
    pair = a * (7.0 - a) * 0.5 + (b - a - 1.0)
    first_is_lo = i1 < i2
    return (g_idx * float(PAIRS_PER_GROUP) + pair, jnp.where(first_is_lo, w1, w2), jnp.where(first_is_lo, w2, w1))


def _to_row_linear(dst_ref, src_ref, n_tiles):
    def body(g, carry):
        r = pl.multiple_of(g * 8, 8)
        for j in range(n_tiles):
            dst_ref[pl.ds(r * ROW_PITCH + j, 8, stride=ROW_PITCH), :] = src_ref[pl.ds(r, 8), j * LANES:(j + 1) * LANES]
        return carry

    lax.fori_loop(0, src_ref.shape[0] // 8, body, 0, unroll=4)


def _from_row_linear(dst_ref, src_ref, n_tiles, pitch):
    def body(g, carry):
        r = pl.multiple_of(g * 8, 8)
        for j in range(n_tiles):
            dst_ref[pl.ds(r, 8), j * LANES:(j + 1) * LANES] = src_ref[pl.ds(r * pitch + j, 8, stride=pitch), :]
        return carry

    lax.fori_loop(0, dst_ref.shape[0] // 8, body, 0, unroll=4)


def _pick(i, n_prompt_tiles, prompt_ref, sample_ref):
    return jnp.where(i < n_prompt_tiles, prompt_ref[...], sample_ref[...])


def _route_kernel(n_prompt_tiles, yp_ref, ys_ref, g_ref, wr_ref, br_ref, tri_ref, pay_ref, meta_ref, cnt_ref, carry):
    i = pl.program_id(0)

    @pl.when(i == 0)
    def _():
        carry[...] = jnp.zeros_like(carry)

    xf = _rms(_pick(i, n_prompt_tiles, yp_ref, ys_ref), g_ref[...])
    x_hi, x_lo = _split_bf16(xf)
    part = _mm(x_hi, wr_ref[...]) + _mm(x_lo, wr_ref[...])
    logits = part[:, :ROUTER_LANES] + part[:, ROUTER_LANES:] + br_ref[...]
    bucket, w_lo, w_hi = _route(logits)
    lane = lax.broadcasted_iota(jnp.int32, logits.shape, 1)
    onehot = jnp.where(lane.astype(F32) == bucket, 1.0, 0.0)
    before = _mm(tri_ref[...], onehot.astype(BF16)) + carry[...]
    rank = jnp.sum(onehot * before, axis=1, keepdims=True)
    carry[...] += jnp.sum(onehot, axis=0, keepdims=True)
    cnt_ref[...] = carry[...]
    meta = jnp.where(lane == 0, bucket, jnp.where(lane == 1, rank, 0.0))
    meta_ref[...] = jnp.transpose(meta)[:8, :].astype(jnp.int32)

    gates = jnp.where(lane == 0, w_lo, jnp.where(lane == 1, w_hi, 0.0))
    nj = D_MODEL // LANES
    for r in range(0, xf.shape[0], 8):
        for j in range(nj):
            pay_ref[pl.ds(r * ROW_PITCH + j, 8, stride=ROW_PITCH), :] = xf[r:r + 8, j * LANES:(j + 1) * LANES]
        pay_ref[pl.ds(r * ROW_PITCH + nj, 8, stride=ROW_PITCH), :] = gates[r:r + 8, :]


def _route_call(y_p, y_s, g, wr, br, tri):
    t = y_s.shape[0]
    npt = y_p.shape[0] // t
    n = y_p.shape[0] + t
    pidx = lambda i: (jnp.minimum(i, npt - 1), 0)
    return pl.pallas_call(
        functools.partial(_route_kernel, npt),
        grid=(npt + 1,),
        in_specs=[pl.BlockSpec((t, D_MODEL), pidx), _full(y_s.shape), _full(g.shape), _full(wr.shape),
                  _full(br.shape), _full(tri.shape)],
        out_specs=[pl.BlockSpec((t * ROW_PITCH, LANES), lambda i: (i, 0)),
                   pl.BlockSpec((8, t), lambda i: (0, i)), _full((1, ROUTER_LANES))],
        out_shape=[jax.ShapeDtypeStruct((n * ROW_PITCH, LANES), F32),
                   jax.ShapeDtypeStruct((8, n), jnp.int32), jax.ShapeDtypeStruct((1, ROUTER_LANES), F32)],
        scratch_shapes=[pltpu.VMEM((1, ROUTER_LANES), F32)],
        compiler_params=_params("arbitrary"),
        name="route",
    )(y_p, y_s, g, wr, br, tri)


def _permute_kernel(pos_ref, zstart_ref, nz_ref, used_ref, src_ref, dst_ref, zbuf, ring, zsem, in_sems, out_sems):
    zbuf[...] = jnp.zeros_like(zbuf)

    def zero_tile(start):
        return pltpu.make_async_copy(zbuf, dst_ref.at[pl.ds(pl.multiple_of(start * ROW_PITCH, 8), TM * ROW_PITCH)], zsem)

    for b in range(N_BUCKETS):
        @pl.when(nz_ref[b] > 0)
        def _():
            zero_tile(zstart_ref[b]).start()
    n_tiles = dst_ref.shape[0] // (TM * ROW_PITCH)

    def start_unused(i, carry):
        zero_tile(i * TM).start()
        return carry

    lax.fori_loop(used_ref[0], n_tiles, start_unused, 0)
    for b in range(N_BUCKETS):
        @pl.when(nz_ref[b] > 0)
        def _():
            zero_tile(0).wait()

    def wait_unused(i, carry):
        zero_tile(0).wait()
        return carry

    lax.fori_loop(used_ref[0], n_tiles, wait_unused, 0)

    chunk_rows = DMA_CHUNK * ROW_PITCH
    n_chunks = src_ref.shape[0] // chunk_rows

    def fetch(c, slot):
        return pltpu.make_async_copy(src_ref.at[pl.ds(pl.multiple_of(c * chunk_rows, 8), chunk_rows)], ring.at[slot],
                                     in_sems.at[slot])

    def drain(slot):
        pltpu.make_async_copy(ring.at[slot], dst_ref.at[pl.ds(0, chunk_rows)], out_sems.at[slot]).wait()

    fetch(0, 0).start()

    def step(c, carry):
        slot = lax.rem(c, RING)
        nxt = lax.rem(c + 1, RING)
        fetch(c, slot).wait()

        @pl.when(c >= RING - 1)
        def _():
            drain(nxt)

        @pl.when(c + 1 < n_chunks)
        def _():
            fetch(c + 1, nxt).start()

        def send(jj, carry2):
            row = pos_ref[c * DMA_CHUNK + jj] * ROW_PITCH
            pltpu.make_async_copy(ring.at[slot, pl.ds(jj * ROW_PITCH, ROW_PITCH)], dst_ref.at[pl.ds(row, ROW_PITCH)],
                                  out_sems.at[slot]).start()
            return carry2

        lax.fori_loop(0, DMA_CHUNK, send, 0, unroll=8)
        return carry

    lax.fori_loop(0, n_chunks, step, 0)
    for c in range(n_chunks - (RING - 1), n_chunks):
        drain(c % RING)


def _permute(pos, zstart, nz, used, payload, n_rows_out):
    assert payload.shape[0] // (DMA_CHUNK * ROW_PITCH) >= RING
    return pl.pallas_call(
        _permute_kernel,
        grid_spec=pltpu.PrefetchScalarGridSpec(
            num_scalar_prefetch=4, grid=(1,),
            in_specs=[pl.BlockSpec(memory_space=pl.ANY)],
            out_specs=pl.BlockSpec(memory_space=pl.ANY),
            scratch_shapes=[pltpu.VMEM((TM * ROW_PITCH, LANES), payload.dtype),
                            pltpu.VMEM((RING, DMA_CHUNK * ROW_PITCH, LANES), payload.dtype),
                            pltpu.SemaphoreType.DMA(()), pltpu.SemaphoreType.DMA((RING,)),
                            pltpu.SemaphoreType.DMA((RING,))]),
        out_shape=jax.ShapeDtypeStruct((n_rows_out * ROW_PITCH, LANES), payload.dtype),
        compiler_params=_params("arbitrary"),
        name="permute",
    )(pos, zstart, nz, used, payload)


def _silu(x):
    return x * _sigmoid(x)


def _experts_kernel(tix_ref, lo_ref, hi_ref, valid_ref, fresh_ref, xs_ref, wgl_ref, wgh_ref, wul_ref, wuh_ref,
                    wdl_ref, wdh_ref, ys_ref, wup_s, wdn_s, xbuf, ybuf):
    i = pl.program_id(0)

    @pl.when(fresh_ref[i] > 0)
    def _():
        for s, ref in enumerate((wgl_ref, wul_ref, wgh_ref, wuh_ref)):
            wup_s[:, s * D_EXPERT:(s + 1) * D_EXPERT] = ref[0, 0].astype(BF16)
        for s, ref in enumerate((wdl_ref, wdh_ref)):
            wdn_s[s * D_EXPERT:(s + 1) * D_EXPERT, :] = ref[0, 0].astype(BF16)

    @pl.when(valid_ref[i] == 0)
    def _():
        ys_ref[...] = jnp.zeros_like(ys_ref)

    @pl.when(valid_ref[i] > 0)
    def _():
        _from_row_linear(xbuf, xs_ref, ROW_PITCH, ROW_PITCH)
        x = xbuf[:, :D_MODEL].astype(BF16)
        gates = xbuf[:, D_MODEL:]
        a = _mm(x, wup_s[...])
        f = D_EXPERT
        h_lo = _silu(a[:, 0:f]) * a[:, f:2 * f] * gates[:, 0:1]
        h_hi = _silu(a[:, 2 * f:3 * f]) * a[:, 3 * f:4 * f] * gates[:, 1:2]
        ybuf[:, :D_MODEL] = _mm(jnp.concatenate([h_lo, h_hi], axis=1).astype(BF16), wdn_s[...])
        ybuf[:, D_MODEL:] = jnp.zeros((TM, LANES), F32)
        _to_row_linear(ys_ref, ybuf, ROW_PITCH)


def _experts(layer, tix, lo, hi, valid, fresh, xs, wg, wu, wd):
    n_tiles = xs.shape[0] // (TM * ROW_PITCH)
    row = lambda i, tix, lo, hi, valid, fresh: (tix[i], 0)
    row_out = lambda i, tix, lo, hi, valid, fresh: (i, 0)
    e_lo = lambda i, tix, lo, hi, valid, fresh: (layer, lo[i], 0, 0)
    e_hi = lambda i, tix, lo, hi, valid, fresh: (layer, hi[i], 0, 0)
    up = pl.BlockSpec((1, 1, D_MODEL, D_EXPERT), e_lo), pl.BlockSpec((1, 1, D_MODEL, D_EXPERT), e_hi)
    down = pl.BlockSpec((1, 1, D_EXPERT, D_MODEL), e_lo), pl.BlockSpec((1, 1, D_EXPERT, D_MODEL), e_hi)
    return pl.pallas_call(
        _experts_kernel,
        grid_spec=pltpu.PrefetchScalarGridSpec(
            num_scalar_prefetch=5, grid=(n_tiles,),
            in_specs=[pl.BlockSpec((TM * ROW_PITCH, LANES), row), *up, *up, *down],
            out_specs=pl.BlockSpec((TM * ROW_PITCH, LANES), row_out),
            scratch_shapes=[pltpu.VMEM((D_MODEL, 4 * D_EXPERT), BF16), pltpu.VMEM((2 * D_EXPERT, D_MODEL), BF16),
                            pltpu.VMEM((TM, PAY_WIDTH), F32), pltpu.VMEM((TM, PAY_WIDTH), F32)]),
        out_shape=jax.ShapeDtypeStruct(xs.shape, F32),
        compiler_params=_params("arbitrary"),
        name="experts",
    )(tix, lo, hi, valid, fresh, xs, wg, wg, wu, wu, wd, wd)


def _ple_kernel(n_prompt_tiles, pos_ref, yp_ref, ys_ref, moe_ref, pp_ref, ps_ref, g_ref, wg_ref, wp_ref, op_ref, os_ref,
                mring, mbuf, sems):
    i = pl.program_id(0)
    t = mbuf.shape[0]
    w = D_MODEL // LANES

    def gather(tile, slot):
        def body(jj, carry):
            row = pos_ref[tile * t + jj] * ROW_PITCH
            pltpu.make_async_copy(moe_ref.at[pl.ds(row, w)], mring.at[slot, pl.ds(jj * w, w)], sems.at[slot]).start()
            return carry

        lax.fori_loop(0, t, body, 0, unroll=8)

    def wait_tile(slot):
        pltpu.make_async_copy(moe_ref.at[pl.ds(0, t * w)], mring.at[slot], sems.at[slot]).wait()

    @pl.when(i == 0)
    def _():
        gather(0, 0)
        gather(1, 1)

    slot = lax.rem(i, PLE_RING)
    nslot = lax.rem(i + 2, PLE_RING)
    wait_tile(slot)
    _from_row_linear(mbuf, mring.at[slot], w, w)
    y2 = _pick(i, n_prompt_tiles, yp_ref, ys_ref) + mbuf[...]
    hn = _rms(y2, g_ref[...]).astype(BF16)
    gt = _sigmoid(_mm(hn, wg_ref[...]))
    pr = _mm(jnp.where(i < n_prompt_tiles, pp_ref[0], ps_ref[...]).astype(BF16), wp_ref[...])
    y3 = y2 + gt * pr
    nxt = jnp.minimum(i + 2, n_prompt_tiles) * t
    for jj in range(t):
        row = pos_ref[nxt + jj] * ROW_PITCH
        pltpu.make_async_copy(moe_ref.at[pl.ds(row, w)], mring.at[nslot, pl.ds(jj * w, w)], sems.at[nslot]).start()

    @pl.when(i < n_prompt_tiles)
    def _():
        op_ref[...] = y3

    @pl.when(i >= n_prompt_tiles)
    def _():
        os_ref[...] = y3
        wait_tile(lax.rem(i + 1, PLE_RING))
        wait_tile(nslot)


def _ple(layer, pos, y_p, y_s, ys_sorted, p_p, p_s, g, wg, wp):
    t = y_s.shape[0]
    npt = y_p.shape[0] // t
    w = D_MODEL // LANES
    pidx = lambda i, pos: (jnp.minimum(i, npt - 1), 0)
    full = lambda shape: pl.BlockSpec(shape, lambda i, pos: (0,) * len(shape))
    return pl.pallas_call(
        functools.partial(_ple_kernel, npt),
        grid_spec=pltpu.PrefetchScalarGridSpec(
            num_scalar_prefetch=1, grid=(npt + 1,),
            in_specs=[pl.BlockSpec((t, D_MODEL), pidx), full(y_s.shape), pl.BlockSpec(memory_space=pl.ANY),
                      pl.BlockSpec((1, t, PLE_DIM), lambda i, pos: (layer, jnp.minimum(i, npt - 1), 0)),
                      full(p_s.shape), full(g.shape), full(wg.shape), full(wp.shape)],
            out_specs=[pl.BlockSpec((t, D_MODEL), pidx), full(y_s.shape)],
            scratch_shapes=[pltpu.VMEM((PLE_RING, t * w, LANES), F32), pltpu.VMEM((t, D_MODEL), F32),
                            pltpu.SemaphoreType.DMA((PLE_RING,))]),
        out_shape=[jax.ShapeDtypeStruct(y_p.shape, F32), jax.ShapeDtypeStruct(y_s.shape, F32)],
        compiler_params=_params("arbitrary"),
        name="ple",
    )(pos, y_p, y_s, ys_sorted, p_p, p_s, g, wg, wp)


def _bucket_experts():
    pairs = [(a, b) for a in range(EXPERTS_PER_GROUP) for b in range(a + 1, EXPERTS_PER_GROUP)]
    lo = [g * EXPERTS_PER_GROUP + a for g in range(N_EXPERT_GROUPS) for a, _ in pairs]
    hi = [g * EXPERTS_PER_GROUP + b for g in range(N_EXPERT_GROUPS) for _, b in pairs]
    return jnp.array(lo, jnp.int32), jnp.array(hi, jnp.int32)


def _moe_ple(layer, y_p, y_s, gffn, wr, br, wg, wu, wd, p_p, p_s, gple, pwg, pwp):
    t = y_s.shape[0]
    n = y_p.shape[0] + t
    tri = jnp.asarray(np.tri(t, t, -1, dtype=np.float32), BF16)
    payload, meta, counts = _route_call(y_p, y_s, gffn, wr, br, tri)

    bucket, rank = meta[0], meta[1]
    cnt = counts[0, :N_BUCKETS].astype(jnp.int32)
    padded = (cnt + TM - 1) // TM * TM
    ends = jnp.cumsum(padded)
    starts = ends - padded
    in_bucket = bucket[:, None] == jnp.arange(N_BUCKETS, dtype=jnp.int32)[None, :]
    pos = rank + jnp.sum(jnp.where(in_bucket, starts[None, :], 0), axis=1)
    n_rows = n + N_BUCKETS * TM
    n_tiles = n_rows // TM
    tile_start = jnp.arange(n_tiles, dtype=jnp.int32) * TM
    valid = tile_start < ends[-1]
    used = (ends[-1] // TM).astype(jnp.int32)
    tix = jnp.arange(n_tiles, dtype=jnp.int32)
    tile_bucket = jnp.sum((ends[None, :] <= (jnp.minimum(tix, used - 1) * TM)[:, None]).astype(jnp.int32), axis=1)
    prev_bucket = jnp.concatenate([jnp.full((1,), -1, jnp.int32), tile_bucket[:-1]])
    fresh = valid & (tile_bucket != prev_bucket)
    first_tile = jnp.where(fresh, tix, n_tiles)
    next_first = lax.cummin(jnp.concatenate([first_tile[1:], jnp.full((1,), n_tiles, jnp.int32)]), reverse=True)
    next_bucket = jnp.where(next_first < n_tiles, jnp.take(tile_bucket, jnp.minimum(next_first, n_tiles - 1)),
                            tile_bucket)
    resident = jnp.where(fresh, tile_bucket, next_bucket)
    lo_tab, hi_tab = _bucket_experts()
    lo = jnp.take(lo_tab, resident)
    hi = jnp.take(hi_tab, resident)

    xs = _permute(pos, (ends - TM).astype(jnp.int32), (cnt > 0).astype(jnp.int32), used.reshape(1), payload, n_rows)
    ys = _experts(layer, jnp.minimum(tix, used - 1), lo, hi, valid.astype(jnp.int32), fresh.astype(jnp.int32), xs, wg,
                  wu, wd)
    return _ple(layer, pos, y_p, y_s, ys, p_p, p_s, gple, pwg, pwp)


def _split_bf16(a):
    hi = a.astype(BF16)
    lo = (a - hi.astype(F32)).astype(BF16)
    return hi, lo


def _head_norm(a, ind_ref, indt_ref, gain):
    hi, lo = _split_bf16(a * a)
    ss = _mm(hi, ind_ref[...]) + _mm(lo, ind_ref[...])
    inv = lax.rsqrt(ss * (1.0 / HEAD_DIM) + EPS)
    invb = _mm(jnp.concatenate(_split_bf16(inv), axis=1), indt_ref[...])
    return a * invb * gain


def _qkv_kernel(x_ref, g_ref, w_ref, iq_ref, iqt_ref, ik_ref, ikt_ref, qg_ref, kg_ref, q_ref, k_ref, v_ref):
    h = _rms(x_ref[...], g_ref[...]).astype(BF16)
    qkv = _mm(h, w_ref[...])
    nq = N_HEADS * HEAD_DIM
    q = _head_norm(qkv[:, :nq], iq_ref, iqt_ref, qg_ref[...])
    k = _head_norm(qkv[:, nq:nq + KV_DIM], ik_ref, ikt_ref, kg_ref[...])
    q_ref[...] = (q * (HEAD_DIM ** -0.5)).astype(BF16)
    k_ref[...] = k
    v_ref[...] = qkv[:, nq + KV_DIM:]


def _qkv(x, g, w, iq, iqt, ik, ikt, qg, kg):
    n = x.shape[0]
    t = _row_tile(n, big=True)
    row = lambda i: (i, 0)
    return pl.pallas_call(
        _qkv_kernel,
        grid=(n // t,),
        in_specs=[pl.BlockSpec((t, D_MODEL), row)] + [_full(a.shape) for a in (g, w, iq, iqt, ik, ikt, qg, kg)],
        out_specs=[pl.BlockSpec((t, N_HEADS * HEAD_DIM), row), pl.BlockSpec((t, KV_DIM), row),
                   pl.BlockSpec((t, KV_DIM), row)],
        out_shape=[jax.ShapeDtypeStruct((n, N_HEADS * HEAD_DIM), BF16), jax.ShapeDtypeStruct((n, KV_DIM), F32),
                   jax.ShapeDtypeStruct((n, KV_DIM), F32)],
        compiler_params=_params("arbitrary"),
        name="qkv",
    )(x, g, w, iq, iqt, ik, ikt, qg, kg)


def _dup_heads(a):
    out = []
    for s in range(KV_DIM // LANES):
        sl = a[:, s * LANES:(s + 1) * LANES]
        sw = pltpu.roll(sl, HEAD_DIM, axis=1)
        low = lax.broadcasted_iota(jnp.int32, sl.shape, 1) < HEAD_DIM
        out.append(jnp.where(low, sl, sw))
        out.append(jnp.where(low, sw, sl))
    return jnp.concatenate(out, axis=1).astype(BF16)


def _attend(q_rows, k2, v2, bias_of, sink_of, extra_mask):
    m_rows = q_rows.shape[0]
    low_q = lax.broadcasted_iota(jnp.int32, (m_rows, LANES), 1) < HEAD_DIM
    low_k = lax.broadcasted_iota(jnp.int32, (k2.shape[0], LANES), 1) < HEAD_DIM
    zero_q = jnp.zeros((m_rows, LANES), BF16)
    zero_k = jnp.zeros((k2.shape[0], LANES), BF16)
    slabs = []
    for g in range(N_KV_HEADS):
        kg = k2[:, g * LANES:(g + 1) * LANES]
        vg = v2[:, g * LANES:(g + 1) * LANES]
        v_lo = jnp.where(low_k, vg, zero_k)
        v_hi = jnp.where(low_k, zero_k, vg)
        lhs = []
        for a in range(GROUP):
            h = g * GROUP + a
            qs = q_rows[:, (h // 2) * LANES:(h // 2 + 1) * LANES]
            lhs.append(jnp.where(low_q, qs, zero_q) if h % 2 == 0 else jnp.where(low_q, zero_q, qs))
        s = lax.dot_general(jnp.concatenate(lhs, axis=0), kg, (((1,), (1,)), ((), ())), preferred_element_type=F32)
        probs, rinv = [], []
        for a in range(GROUP):
            h = g * GROUP + a
            sa = s[a * m_rows:(a + 1) * m_rows] + bias_of(h)
            if extra_mask is not None:
                sa = jnp.where(extra_mask, NEG_INF, sa)
            sink = sink_of(h)
            m = jnp.maximum(jnp.max(sa, axis=1, keepdims=True), sink)
            p = jnp.exp(sa - m)
            den = jnp.sum(p, axis=1, keepdims=True) + jnp.exp(sink - m)
            probs.append(p.astype(BF16))
            rinv.append(1.0 / den)
        for sp in range(GROUP // 2):
            o = _mm(probs[2 * sp], v_lo) + _mm(probs[2 * sp + 1], v_hi)
            slabs.append(o * jnp.where(low_q, rinv[2 * sp], rinv[2 * sp + 1]))
    return jnp.concatenate(slabs, axis=1)


def _attn_p_kernel(sink_ref, q_ref, k_ref, v_ref, x_ref, bias_ref, wo_ref, y_ref, kbuf, vbuf, obuf):
    t = q_ref.shape[0]
    i = pl.program_id(0)

    @pl.when(i == 0)
    def _():
        kbuf[0:WINDOW, :] = jnp.zeros((WINDOW, 2 * KV_DIM), BF16)
        vbuf[0:WINDOW, :] = jnp.zeros((WINDOW, 2 * KV_DIM), BF16)

    @pl.when(i > 0)
    def _():
        kbuf[0:WINDOW, :] = kbuf[t:t + WINDOW, :]
        vbuf[0:WINDOW, :] = vbuf[t:t + WINDOW, :]

    kbuf[WINDOW:, :] = _dup_heads(k_ref[...])
    vbuf[WINDOW:, :] = _dup_heads(v_ref[...])
    col = lax.broadcasted_iota(jnp.int32, (WINDOW, 2 * WINDOW), 1)
    for j in range(t // WINDOW):
        rows = slice(j * WINDOW, (j + 1) * WINDOW)
        keys = slice(j * WINDOW, (j + 2) * WINDOW)
        extra = jnp.logical_and(i == 0, col < WINDOW) if j == 0 else None
        o = _attend(q_ref[rows, :], kbuf[keys, :], vbuf[keys, :], lambda h: bias_ref[h], lambda h: sink_ref[h], extra)
        obuf[rows, :] = o.astype(BF16)
    y_ref[...] = x_ref[...] + _mm(obuf[...], wo_ref[...])


def _attn_p(sinks, q, k, v, x, bias, wo):
    n = q.shape[0]
    t = _row_tile(n)
    row = lambda i: (i, 0)
    return pl.pallas_call(
        _attn_p_kernel,
        grid=(n // t,),
        in_specs=[pl.BlockSpec(memory_space=pltpu.SMEM),
                  pl.BlockSpec((t, N_HEADS * HEAD_DIM), row), pl.BlockSpec((t, KV_DIM), row),
                  pl.BlockSpec((t, KV_DIM), row), pl.BlockSpec((t, D_MODEL), row),
                  _full(bias.shape), _full(wo.shape)],
        out_specs=pl.BlockSpec((t, D_MODEL), row),
        out_shape=jax.ShapeDtypeStruct((n, D_MODEL), F32),
        scratch_shapes=[pltpu.VMEM((t + WINDOW, 2 * KV_DIM), BF16), pltpu.VMEM((t + WINDOW, 2 * KV_DIM), BF16),
                        pltpu.VMEM((t, N_HEADS * HEAD_DIM), BF16)],
        compiler_params=_params("arbitrary"),
        name="attn_prompt",
    )(sinks, q, k, v, x, bias, wo)


def _attn_s_kernel(sink_ref, q_ref, k_ref, v_ref, ck_ref, cv_ref, bias_ref, o_ref, nk_ref, nv_ref, shift_buf):
    pad = bias_ref.shape[2] - ck_ref.shape[0] - k_ref.shape[0]
    zpad = jnp.zeros((pad, KV_DIM), F32)
    k2 = _dup_heads(jnp.concatenate([ck_ref[...], k_ref[...], zpad], axis=0))
    v2 = _dup_heads(jnp.concatenate([cv_ref[...], v_ref[...], zpad], axis=0))
    o = _attend(q_ref[...], k2, v2, lambda h: bias_ref[h], lambda h: sink_ref[h], None)
    o_ref[...] = o.astype(BF16)
    t_new = k_ref.shape[0] // SAMPLE_SEQS
    for new_ref, old_ref, add_ref in ((nk_ref, ck_ref, k_ref), (nv_ref, cv_ref, v_ref)):
        for b in range(SAMPLE_SEQS):
            r = b * WINDOW
            shift_buf[0:WINDOW - t_new, :] = old_ref[r + t_new:r + WINDOW, :]
            shift_buf[WINDOW - t_new:WINDOW, :] = add_ref[b * t_new:(b + 1) * t_new, :]
            new_ref[b] = jnp.transpose(shift_buf[...])


def _attn_s(sinks, q, k, v, ck, cv, bias, t_new):
    n = q.shape[0]
    rows = SAMPLE_SEQS * t_new
    crow = SAMPLE_SEQS * WINDOW
    row = lambda i: (i, 0)
    return pl.pallas_call(
        _attn_s_kernel,
        grid=(n // rows,),
        in_specs=[pl.BlockSpec(memory_space=pltpu.SMEM),
                  pl.BlockSpec((rows, N_HEADS * HEAD_DIM), row), pl.BlockSpec((rows, KV_DIM), row),
                  pl.BlockSpec((rows, KV_DIM), row), pl.BlockSpec((crow, KV_DIM), row),
                  pl.BlockSpec((crow, KV_DIM), row), _full(bias.shape)],
        out_specs=[pl.BlockSpec((rows, N_HEADS * HEAD_DIM), row),
                   pl.BlockSpec((SAMPLE_SEQS, KV_DIM, WINDOW), lambda i: (i, 0, 0)),
                   pl.BlockSpec((SAMPLE_SEQS, KV_DIM, WINDOW), lambda i: (i, 0, 0))],
        out_shape=[jax.ShapeDtypeStruct((n, N_HEADS * HEAD_DIM), BF16),
                   jax.ShapeDtypeStruct((ck.shape[0] // WINDOW, KV_DIM, WINDOW), F32),
                   jax.ShapeDtypeStruct((cv.shape[0] // WINDOW, KV_DIM, WINDOW), F32)],
        scratch_shapes=[pltpu.VMEM((WINDOW, KV_DIM), F32)],
        compiler_params=_params("arbitrary"),
        name="attn_sample",
    )(sinks, q, k, v, ck, cv, bias)


def _proj_res_kernel(o_ref, x_ref, w_ref, y_ref):
    y_ref[...] = x_ref[...] + _mm(o_ref[...], w_ref[...])


def _proj_res(o, x, w):
    n = o.shape[0]
    t = _row_tile(n)
    row = lambda i: (i, 0)
    return pl.pallas_call(
        _proj_res_kernel,
        grid=(n // t,),
        in_specs=[pl.BlockSpec((t, o.shape[1]), row), pl.BlockSpec((t, D_MODEL), row), _full(w.shape)],
        out_specs=pl.BlockSpec((t, D_MODEL), row),
        out_shape=jax.ShapeDtypeStruct((n, D_MODEL), F32),
        compiler_params=_params("arbitrary"),
        name="proj_res",
    )(o, x, w)


def _alibi_slopes():
    return np.exp2(-8.0 * np.arange(1, N_HEADS + 1, dtype=np.float64) / N_HEADS).astype(np.float32)


def _band_bias(dist, allowed):
    b = -(_alibi_slopes()[:, None, None] * dist.astype(np.float32)[None])
    return jnp.asarray(np.where(allowed[None], b, np.float32(NEG_INF)).astype(np.float32))


def _prompt_bias():
    dist = WINDOW + np.arange(WINDOW)[:, None] - np.arange(2 * WINDOW)[None, :]
    return _band_bias(dist, (dist >= 0) & (dist <= WINDOW))


def _sample_bias(t_new, n_cols):
    c = np.arange(n_cols)
    n_cache = SAMPLE_SEQS * WINDOW
    n_new = SAMPLE_SEQS * t_new
    is_cache = c < n_cache
    is_new = (c >= n_cache) & (c < n_cache + n_new)
    seq_c = np.where(is_cache, c // WINDOW, (c - n_cache) // t_new)
    pos_c = np.where(is_cache, c % WINDOW, WINDOW + (c - n_cache) % t_new)
    r = np.arange(n_new)
    seq_r, tok_r = r // t_new, r % t_new
    dist = WINDOW + tok_r[:, None] - pos_c[None, :]
    allowed = (seq_r[:, None] == seq_c[None, :]) & (is_cache | is_new)[None, :] & (dist >= 0) & (dist <= WINDOW)
    return _band_bias(dist, allowed)


def _head_indicator(n_heads):
    ch = np.arange(n_heads * HEAD_DIM) // HEAD_DIM
    ind = (ch[:, None] == np.arange(LANES)[None, :]).astype(np.float32)
    return jnp.asarray(ind, BF16), jnp.asarray(np.concatenate([ind.T, ind.T], axis=0), BF16)


def kernel(x_prompt, x_sample, state_conv, cache_k, cache_v, p_prompt, p_sample, norm_mix, norm_ffn, norm_ple,
           conv_w_in, conv_b_in, conv_w_dw, conv_b_dw, conv_ln_g, conv_ln_b, conv_w_out, conv_b_out, attn_w_qkv,
           attn_q_norm, attn_k_norm, attn_sinks, attn_w_o, moe_w_rg, moe_b_rg, moe_w_re, moe_b_re, moe_w_gate,
           moe_w_up, moe_w_down, ple_w_gate, ple_w_proj):
    bp, seq, d = x_prompt.shape
    bs, t_new, _ = x_sample.shape
    assert bp == 1 and d == D_MODEL and seq % WINDOW == 0 and bs % SAMPLE_SEQS == 0
    assert seq % (bs * t_new) == 0 and (bs * t_new) % DMA_CHUNK == 0
    depth = norm_mix.shape[0]
    row2 = lambda a: a.reshape(1, -1)

    y_p = x_prompt.reshape(seq, d)
    y_s = x_sample.reshape(bs * t_new, d)
    conv_p, conv_s, k_p, v_p, k_s, v_s = [], [], [], [], [], []

    for i in range(depth):
        j = i // 2
        g_mix = row2(norm_mix[i])
        if i % 2 == 0:
            w_in = conv_w_in[j].astype(BF16)
            b_in = row2(conv_b_in[j])
            tail = (conv_w_dw[j], row2(conv_b_dw[j]), row2(conv_ln_g[j]), row2(conv_ln_b[j]),
                    conv_w_out[j].astype(BF16), row2(conv_b_out[j]))
            u_p = _conv_in(y_p, g_mix, w_in, b_in)
            u_s = _conv_in(y_s, g_mix, w_in, b_in)
            conv_p.append(u_p[seq - CONV_STATE:].reshape(1, CONV_STATE, d))
            y_p = _conv_out_p(u_p, y_p, *tail)
            ys_t, new_state = _conv_out_s(state_conv[j], u_s.reshape(bs, t_new, d),
                                          y_s.reshape(bs, t_new, d).transpose(1, 0, 2), *tail)
            conv_s.append(new_state)
            y_s = ys_t.transpose(1, 0, 2).reshape(bs * t_new, d)
        else:
            w_qkv = attn_w_qkv[j].astype(BF16)
            w_o = attn_w_o[j].astype(BF16)
            iq, iqt = _head_indicator(N_HEADS)
            ik, ikt = _head_indicator(N_KV_HEADS)
            qg = row2(jnp.tile(attn_q_norm[j], N_HEADS))
            kg = row2(jnp.tile(attn_k_norm[j], N_KV_HEADS))
            sinks = attn_sinks[j]
            q1, k1, v1 = _qkv(y_p, g_mix, w_qkv, iq, iqt, ik, ikt, qg, kg)
            q2, k2, v2 = _qkv(y_s, g_mix, w_qkv, iq, iqt, ik, ikt, qg, kg)
            k_p.append(k1[seq - WINDOW:].reshape(1, WINDOW, N_KV_HEADS, HEAD_DIM))
            v_p.append(v1[seq - WINDOW:].reshape(1, WINDOW, N_KV_HEADS, HEAD_DIM))
            y_p = _attn_p(sinks, q1, k1, v1, y_p, _prompt_bias(), w_o)
            n_cols = -(-(SAMPLE_SEQS * (WINDOW + t_new)) // LANES) * LANES
            o_s, nk, nv = _attn_s(sinks, q2, k2, v2, cache_k[j].reshape(bs * WINDOW, KV_DIM),
                                  cache_v[j].reshape(bs * WINDOW, KV_DIM), _sample_bias(t_new, n_cols), t_new)
            k_s.append(nk.reshape(bs, N_KV_HEADS, HEAD_DIM, WINDOW).transpose(0, 3, 1, 2))
            v_s.append(nv.reshape(bs, N_KV_HEADS, HEAD_DIM, WINDOW).transpose(0, 3, 1, 2))
            y_s = _proj_res(o_s, y_s, w_o)

        w_r = jnp.zeros((d, ROUTER_LANES), F32)
        w_r = w_r.at[:, :N_EXPERT_GROUPS].set(moe_w_rg[i]).at[:, EXPERT_LANE0:EXPERT_LANE0 + N_EXPERTS].set(moe_w_re[i])
        b_r = jnp.zeros((1, ROUTER_LANES), F32)
        b_r = b_r.at[0, :N_EXPERT_GROUPS].set(moe_b_rg[i]).at[0, EXPERT_LANE0:EXPERT_LANE0 + N_EXPERTS].set(moe_b_re[i])
        moe = (row2(norm_ffn[i]), jnp.concatenate(_split_bf16(w_r), axis=1), b_r, moe_w_gate, moe_w_up, moe_w_down)
        ple = (row2(norm_ple[i]), ple_w_gate[i].astype(BF16), ple_w_proj[i].astype(BF16))
        y_p, y_s = _moe_ple(i, y_p, y_s, *moe, p_prompt.reshape(depth, seq, PLE_DIM),
                            p_sample[i].reshape(bs * t_new, PLE_DIM), *ple)

    return (y_p.reshape(1, seq, d), y_s.reshape(bs, t_new, d), jnp.stack(conv_p), jnp.stack(conv_s),
            jnp.stack(k_p), jnp.stack(v_p), jnp.stack(k_s), jnp.stack(v_s))
```

```python
import functools

import numpy as np

import jax
import jax.numpy as jnp
from jax import lax
from jax.experimental import pallas as pl
from jax.experimental.pallas import tpu as pltpu

F32 = jnp.float32
BF16 = jnp.bfloat16

D_MODEL = 1024
PLE_DIM = 256
CONV_WIDTH = 31
CONV_STATE = CONV_WIDTH - 1
N_HEADS = 16
N_KV_HEADS = 4
HEAD_DIM = 64
GROUP = N_HEADS // N_KV_HEADS
WINDOW = 128
KV_DIM = N_KV_HEADS * HEAD_DIM
N_EXPERT_GROUPS = 4
EXPERTS_PER_GROUP = 4
N_EXPERTS = 16
D_EXPERT = 256
EPS = 1e-6
NEG_INF = -1e30

LANES = 128
ROUTER_LANES = LANES
EXPERT_LANE0 = N_EXPERT_GROUPS
HALO = 32
CONV_ROWS = 32
NORM_ROWS = 512
CONV_PITCH = D_MODEL // LANES + 1
SAMPLE_SEQS = 8
PAIRS_PER_GROUP = EXPERTS_PER_GROUP * (EXPERTS_PER_GROUP - 1) // 2
N_BUCKETS = N_EXPERT_GROUPS * PAIRS_PER_GROUP
TM = 256
PAY_WIDTH = D_MODEL + LANES
ROW_PITCH = PAY_WIDTH // LANES
DMA_CHUNK = 512
RING = 3
PLE_RING = 3
VMEM_LIMIT = 48 * 1024 * 1024


def _row_tile(n, big=False):
    if big and n % 1024 == 0:
        return 1024
    return 512 if n % 512 == 0 else n


def _params(*sem):
    return pltpu.CompilerParams(dimension_semantics=sem, vmem_limit_bytes=VMEM_LIMIT)


def _full(shape):
    nd = len(shape)
    return pl.BlockSpec(shape, lambda *_: (0,) * nd)


def _rms(x, g):
    ms = jnp.mean(x * x, axis=-1, keepdims=True)
    return x * lax.rsqrt(ms + EPS) * g


def _sigmoid(x):
    return 1.0 / (1.0 + jnp.exp(-x))


def _mm(a, b):
    return jnp.dot(a, b, preferred_element_type=F32)


def _conv_in_kernel(x_ref, g_ref, w_ref, b_ref, u_ref):
    h = _rms(x_ref[...], g_ref[...]).astype(BF16)
    z = _mm(h, w_ref[...]) + b_ref[...]
    u_ref[...] = z[:, :D_MODEL] * _sigmoid(z[:, D_MODEL:])


def _conv_in(x, g, w, b):
    n = x.shape[0]
    t = _row_tile(n, big=True)
    return pl.pallas_call(
        _conv_in_kernel,
        grid=(n // t,),
        in_specs=[pl.BlockSpec((t, D_MODEL), lambda i: (i, 0)), _full(g.shape), _full(w.shape), _full(b.shape)],
        out_specs=pl.BlockSpec((t, D_MODEL), lambda i: (i, 0)),
        out_shape=jax.ShapeDtypeStruct((n, D_MODEL), F32),
        compiler_params=_params("arbitrary"),
        name="conv_in",
    )(x, g, w, b)


def _ln_silu(c, g, b):
    mu = jnp.mean(c, axis=-1, keepdims=True)
    xc = c - mu
    var = jnp.mean(xc * xc, axis=-1, keepdims=True)
    cn = xc * lax.rsqrt(var + EPS) * g + b
    return cn * _sigmoid(cn)


def _conv_out_p_kernel(u_ref, halo_ref, x_ref, wdw_ref, bdw_ref, lng_ref, lnb_ref, wout_ref, bout_ref, y_ref,
                       ubuf, cbuf, hbuf):
    t = u_ref.shape[0]
    i = pl.program_id(0)
    nj = D_MODEL // LANES

    def put(r, j, val):
        ubuf[pl.ds(r * CONV_PITCH + j, 8, stride=CONV_PITCH), :] = val

    for r in range(0, HALO, 8):
        for j in range(nj):
            put(r, j, jnp.where(i > 0, halo_ref[r:r + 8, j * LANES:(j + 1) * LANES], 0.0))

    def fill(rr, carry):
        r = pl.multiple_of(rr * 8, 8)
        for j in range(nj):
            put(r + HALO, j, u_ref[pl.ds(r, 8), j * LANES:(j + 1) * LANES])
        return carry

    lax.fori_loop(0, t // 8, fill, 0, unroll=4)

    def conv_chunk(rr, carry):
        r0 = pl.multiple_of(rr * CONV_ROWS, CONV_ROWS)
        for j in range(nj):
            lanes = slice(j * LANES, (j + 1) * LANES)
            accs = [None] * (CONV_ROWS // 8)
            for k in range(CONV_WIDTH):
                wk = wdw_ref[k:k + 1, lanes]
                for q in range(CONV_ROWS // 8):
                    r = r0 + (HALO - CONV_STATE + k + 8 * q)
                    term = wk * ubuf[pl.ds(r * CONV_PITCH + j, 8, stride=CONV_PITCH), :]
                    accs[q] = term if k == 0 else accs[q] + term
            for q in range(CONV_ROWS // 8):
                cbuf[pl.ds(r0 + 8 * q, 8), lanes] = accs[q]
        return carry

    lax.fori_loop(0, t // CONV_ROWS, conv_chunk, 0)

    def norm_chunk(rr, carry):
        r0 = pl.multiple_of(rr * NORM_ROWS, NORM_ROWS)
        c = cbuf[pl.ds(r0, NORM_ROWS), :] + bdw_ref[...]
        hbuf[pl.ds(r0, NORM_ROWS), :] = _ln_silu(c, lng_ref[...], lnb_ref[...]).astype(BF16)
        return carry

    lax.fori_loop(0, t // NORM_ROWS, norm_chunk, 0)
    y_ref[...] = x_ref[...] + _mm(hbuf[...], wout_ref[...]) + bout_ref[...]


def _conv_out_p(u, x, wdw, bdw, lng, lnb, wout, bout):
    n = u.shape[0]
    t = _row_tile(n)
    hb = t // HALO
    row = lambda i: (i, 0)
    return pl.pallas_call(
        _conv_out_p_kernel,
        grid=(n // t,),
        in_specs=[pl.BlockSpec((t, D_MODEL), row),
                  pl.BlockSpec((HALO, D_MODEL), lambda i: (jnp.maximum(i * hb - 1, 0), 0)),
                  pl.BlockSpec((t, D_MODEL), row),
                  _full(wdw.shape), _full(bdw.shape), _full(lng.shape), _full(lnb.shape),
                  _full(wout.shape), _full(bout.shape)],
        out_specs=pl.BlockSpec((t, D_MODEL), row),
        out_shape=jax.ShapeDtypeStruct((n, D_MODEL), F32),
        scratch_shapes=[pltpu.VMEM(((t + HALO) * CONV_PITCH, LANES), F32), pltpu.VMEM((t, D_MODEL), F32),
                        pltpu.VMEM((t, D_MODEL), BF16)],
        compiler_params=_params("arbitrary"),
        name="conv_out_prompt",
    )(u, u, x, wdw, bdw, lng, lnb, wout, bout)


def _conv_out_s_kernel(st_ref, u_ref, x_ref, wdw_ref, bdw_ref, lng_ref, lnb_ref, wout_ref, bout_ref, y_ref, ns_ref,
                       wbuf, cbuf):
    bb, nt, _ = u_ref.shape
    win = CONV_STATE + nt
    nj = D_MODEL // LANES
    seq_pitch = win * CONV_PITCH

    def put(b, r0, rows, j, val):
        wbuf[pl.ds((b * win + r0) * CONV_PITCH + j, rows, stride=CONV_PITCH), :] = val

    for b in range(bb):
        for j in range(nj):
            lanes = slice(j * LANES, (j + 1) * LANES)
            for r0 in range(0, CONV_STATE, 8):
                rows = min(8, CONV_STATE - r0)
                put(b, r0, rows, j, st_ref[b, r0:r0 + rows, lanes])
            put(b, CONV_STATE, nt, j, u_ref[b, :, lanes])

    for b in range(bb):
        for r0 in range(0, CONV_STATE, 8):
            rows = min(8, CONV_STATE - r0)
            for j in range(nj):
                src = (b * win + nt + r0) * CONV_PITCH + j
                ns_ref[b, r0:r0 + rows, j * LANES:(j + 1) * LANES] = wbuf[pl.ds(src, rows, stride=CONV_PITCH), :]

    for t in range(nt):
        for b0 in range(0, bb, 8):
            for j in range(nj):
                lanes = slice(j * LANES, (j + 1) * LANES)
                acc = None
                for k in range(CONV_WIDTH):
                    start = (b0 * win + t + k) * CONV_PITCH + j
                    term = wdw_ref[k:k + 1, lanes] * wbuf[pl.ds(start, 8, stride=seq_pitch), :]
                    acc = term if acc is None else acc + term
                cbuf[t * bb + b0:t * bb + b0 + 8, lanes] = acc
    c = cbuf[...] + bdw_ref[...]
    h = _ln_silu(c, lng_ref[...], lnb_ref[...]).astype(BF16)
    y = _mm(h, wout_ref[...]) + bout_ref[...]
    for t in range(nt):
        y_ref[t] = x_ref[t] + y[t * bb:(t + 1) * bb]


def _conv_out_s(state, u, x, wdw, bdw, lng, lnb, wout, bout):
    b, nt, _ = u.shape
    bb = 16 if b % 16 == 0 else b
    assert bb % 8 == 0
    win = CONV_STATE + nt
    seq = lambda i: (i, 0, 0)
    tm = lambda i: (0, i, 0)
    return pl.pallas_call(
        _conv_out_s_kernel,
        grid=(b // bb,),
        in_specs=[pl.BlockSpec((bb, CONV_STATE, D_MODEL), seq), pl.BlockSpec((bb, nt, D_MODEL), seq),
                  pl.BlockSpec((nt, bb, D_MODEL), tm),
                  _full(wdw.shape), _full(bdw.shape), _full(lng.shape), _full(lnb.shape),
                  _full(wout.shape), _full(bout.shape)],
        out_specs=[pl.BlockSpec((nt, bb, D_MODEL), tm), pl.BlockSpec((bb, CONV_STATE, D_MODEL), seq)],
        out_shape=[jax.ShapeDtypeStruct((nt, b, D_MODEL), F32), jax.ShapeDtypeStruct((b, CONV_STATE, D_MODEL), F32)],
        scratch_shapes=[pltpu.VMEM((bb * win * CONV_PITCH, LANES), F32), pltpu.VMEM((nt * bb, D_MODEL), F32)],
        compiler_params=_params("arbitrary"),
        name="conv_out_sample",
    )(state, u, x, wdw, bdw, lng, lnb, wout, bout)


def _route(lg):
    big = 3.0e38
    lane = lax.broadcasted_iota(jnp.int32, lg.shape, 1)
    lanef = lane.astype(F32)
    is_g = lane < N_EXPERT_GROUPS
    gl = jnp.where(is_g, lg, -big)
    gmax = jnp.max(gl, axis=1, keepdims=True)
    gsum = jnp.sum(jnp.where(is_g, jnp.exp(gl - gmax), 0.0), axis=1, keepdims=True)
    g_w = 1.0 / gsum
    g_idx = jnp.min(jnp.where(gl == gmax, lanef, big), axis=1, keepdims=True)
    rel = lanef - float(EXPERT_LANE0) - g_idx * float(EXPERTS_PER_GROUP)
    in_grp = jnp.where(rel >= 0.0, jnp.where(rel < float(EXPERTS_PER_GROUP), 1.0, 0.0), 0.0) > 0.5
    el = jnp.where(in_grp, lg, -big)
    e1 = jnp.max(el, axis=1, keepdims=True)
    i1 = jnp.min(jnp.where(el == e1, lanef, big), axis=1, keepdims=True)
    el2 = jnp.where(lanef == i1, -big, el)
    e2 = jnp.max(el2, axis=1, keepdims=True)
    i2 = jnp.min(jnp.where(el2 == e2, lanef, big), axis=1, keepdims=True)
    tt = jnp.exp(e2 - e1)
    w1 = g_w / (1.0 + tt)
    w2 = g_w * tt / (1.0 + tt)
    base = float(EXPERT_LANE0) + g_idx * float(EXPERTS_PER_GROUP)
    a = jnp.minimum(i1, i2) - base
    b = jnp.maximum(i1, i2) - base
    pair = a * (7.0 - a) * 0.5 + (b - a - 1.0)
    first_is_lo = i1 < i2
    return (g_idx * float(PAIRS_PER_GROUP) + pair, jnp.where(first_is_lo, w1, w2), jnp.where(first_is_lo, w2, w1))


def _to_row_linear(dst_ref, src_ref, n_tiles):
    def body(g, carry):
        r = pl.multiple_of(g * 8, 8)
        for j in range(n_tiles):
            dst_ref[pl.ds(r * ROW_PITCH + j, 8, stride=ROW_PITCH), :] = src_ref[pl.ds(r, 8), j * LANES:(j + 1) * LANES]
        return carry

    lax.fori_loop(0, src_ref.shape[0] // 8, body, 0, unroll=4)


def _from_row_linear(dst_ref, src_ref, n_tiles, pitch):
    def body(g, carry):
        r = pl.multiple_of(g * 8, 8)
        for j in range(n_tiles):
            dst_ref[pl.ds(r, 8), j * LANES:(j + 1) * LANES] = src_ref[pl.ds(r * pitch + j, 8, stride=pitch), :]
        return carry

    lax.fori_loop(0, dst_ref.shape[0] // 8, body, 0, unroll=4)


def _pick(i, n_prompt_tiles, prompt_ref, sample_ref):
    return jnp.where(i < n_prompt_tiles, prompt_ref[...], sample_ref[...])


def _route_kernel(n_prompt_tiles, yp_ref, ys_ref, g_ref, wr_ref, br_ref, tri_ref, pay_ref, meta_ref, cnt_ref, carry):
    i = pl.program_id(0)

    @pl.when(i == 0)
    def _():
        carry[...] = jnp.zeros_like(carry)

    xf = _rms(_pick(i, n_prompt_tiles, yp_ref, ys_ref), g_ref[...])
    x_hi, x_lo = _split_bf16(xf)
    part = _mm(x_hi, wr_ref[...]) + _mm(x_lo, wr_ref[...])
    logits = part[:, :ROUTER_LANES] + part[:, ROUTER_LANES:] + br_ref[...]
    bucket, w_lo, w_hi = _route(logits)
    lane = lax.broadcasted_iota(jnp.int32, logits.shape, 1)
    onehot = jnp.where(lane.astype(F32) == bucket, 1.0, 0.0)
    before = _mm(tri_ref[...], onehot.astype(BF16)) + carry[...]
    rank = jnp.sum(onehot * before, axis=1, keepdims=True)
    carry[...] += jnp.sum(onehot, axis=0, keepdims=True)
    cnt_ref[...] = carry[...]
    meta = jnp.where(lane == 0, bucket, jnp.where(lane == 1, rank, 0.0))
    meta_ref[...] = jnp.transpose(meta)[:8, :].astype(jnp.int32)

    gates = jnp.where(lane == 0, w_lo, jnp.where(lane == 1, w_hi, 0.0))
    nj = D_MODEL // LANES
    for r in range(0, xf.shape[0], 8):
        for j in range(nj):
            pay_ref[pl.ds(r * ROW_PITCH + j, 8, stride=ROW_PITCH), :] = xf[r:r + 8, j * LANES:(j + 1) * LANES]
        pay_ref[pl.ds(r * ROW_PITCH + nj, 8, stride=ROW_PITCH), :] = gates[r:r + 8, :]


def _route_call(y_p, y_s, g, wr, br, tri):
    t = y_s.shape[0]
    npt = y_p.shape[0] // t
    n = y_p.shape[0] + t
    pidx = lambda i: (jnp.minimum(i, npt - 1), 0)
    return pl.pallas_call(
        functools.partial(_route_kernel, npt),
        grid=(npt + 1,),
        in_specs=[pl.BlockSpec((t, D_MODEL), pidx), _full(y_s.shape), _full(g.shape), _full(wr.shape),
                  _full(br.shape), _full(tri.shape)],
        out_specs=[pl.BlockSpec((t * ROW_PITCH, LANES), lambda i: (i, 0)),
                   pl.BlockSpec((8, t), lambda i: (0, i)), _full((1, ROUTER_LANES))],
        out_shape=[jax.ShapeDtypeStruct((n * ROW_PITCH, LANES), F32),
                   jax.ShapeDtypeStruct((8, n), jnp.int32), jax.ShapeDtypeStruct((1, ROUTER_LANES), F32)],
        scratch_shapes=[pltpu.VMEM((1, ROUTER_LANES), F32)],
        compiler_params=_params("arbitrary"),
        name="route",
    )(y_p, y_s, g, wr, br, tri)


def _permute_kernel(pos_ref, zstart_ref, nz_ref, used_ref, src_ref, dst_ref, zbuf, ring, zsem, in_sems, out_sems):
    zbuf[...] = jnp.zeros_like(zbuf)

    def zero_tile(start):
        return pltpu.make_async_copy(zbuf, dst_ref.at[pl.ds(pl.multiple_of(start * ROW_PITCH, 8), TM * ROW_PITCH)], zsem)

    for b in range(N_BUCKETS):
        @pl.when(nz_ref[b] > 0)
        def _():
            zero_tile(zstart_ref[b]).start()
    n_tiles = dst_ref.shape[0] // (TM * ROW_PITCH)

    def start_unused(i, carry):
        zero_tile(i * TM).start()
        return carry

    lax.fori_loop(used_ref[0], n_tiles, start_unused, 0)
    for b in range(N_BUCKETS):
        @pl.when(nz_ref[b] > 0)
        def _():
            zero_tile(0).wait()

    def wait_unused(i, carry):
        zero_tile(0).wait()
        return carry

    lax.fori_loop(used_ref[0], n_tiles, wait_unused, 0)

    chunk_rows = DMA_CHUNK * ROW_PITCH
    n_chunks = src_ref.shape[0] // chunk_rows

    def fetch(c, slot):
        return pltpu.make_async_copy(src_ref.at[pl.ds(pl.multiple_of(c * chunk_rows, 8), chunk_rows)], ring.at[slot],
                                     in_sems.at[slot])

    def drain(slot):
        pltpu.make_async_copy(ring.at[slot], dst_ref.at[pl.ds(0, chunk_rows)], out_sems.at[slot]).wait()

    fetch(0, 0).start()

    def step(c, carry):
        slot = lax.rem(c, RING)
        nxt = lax.rem(c + 1, RING)
        fetch(c, slot).wait()

        @pl.when(c >= RING - 1)
        def _():
            drain(nxt)

        @pl.when(c + 1 < n_chunks)
        def _():
            fetch(c + 1, nxt).start()

        def send(jj, carry2):
            row = pos_ref[c * DMA_CHUNK + jj] * ROW_PITCH
            pltpu.make_async_copy(ring.at[slot, pl.ds(jj * ROW_PITCH, ROW_PITCH)], dst_ref.at[pl.ds(row, ROW_PITCH)],
                                  out_sems.at[slot]).start()
            return carry2

        lax.fori_loop(0, DMA_CHUNK, send, 0, unroll=8)
        return carry

    lax.fori_loop(0, n_chunks, step, 0)
    for c in range(n_chunks - (RING - 1), n_chunks):
        drain(c % RING)


def _permute(pos, zstart, nz, used, payload, n_rows_out):
    assert payload.shape[0] // (DMA_CHUNK * ROW_PITCH) >= RING
    return pl.pallas_call(
        _permute_kernel,
        grid_spec=pltpu.PrefetchScalarGridSpec(
            num_scalar_prefetch=4, grid=(1,),
            in_specs=[pl.BlockSpec(memory_space=pl.ANY)],
            out_specs=pl.BlockSpec(memory_space=pl.ANY),
            scratch_shapes=[pltpu.VMEM((TM * ROW_PITCH, LANES), payload.dtype),
                            pltpu.VMEM((RING, DMA_CHUNK * ROW_PITCH, LANES), payload.dtype),
                            pltpu.SemaphoreType.DMA(()), pltpu.SemaphoreType.DMA((RING,)),
                            pltpu.SemaphoreType.DMA((RING,))]),
        out_shape=jax.ShapeDtypeStruct((n_rows_out * ROW_PITCH, LANES), payload.dtype),
        compiler_params=_params("arbitrary"),
        name="permute",
    )(pos, zstart, nz, used, payload)


def _silu(x):
    return x * _sigmoid(x)


def _experts_kernel(tix_ref, lo_ref, hi_ref, valid_ref, fresh_ref, xs_ref, wgl_ref, wgh_ref, wul_ref, wuh_ref,
                    wdl_ref, wdh_ref, ys_ref, wup_s, wdn_s, xbuf, ybuf):
    i = pl.program_id(0)

    @pl.when(fresh_ref[i] > 0)
    def _():
        for s, ref in enumerate((wgl_ref, wul_ref, wgh_ref, wuh_ref)):
            wup_s[:, s * D_EXPERT:(s + 1) * D_EXPERT] = ref[0, 0].astype(BF16)
        for s, ref in enumerate((wdl_ref, wdh_ref)):
            wdn_s[s * D_EXPERT:(s + 1) * D_EXPERT, :] = ref[0, 0].astype(BF16)

    @pl.when(valid_ref[i] == 0)
    def _():
        ys_ref[...] = jnp.zeros_like(ys_ref)

    @pl.when(valid_ref[i] > 0)
    def _():
        _from_row_linear(xbuf, xs_ref, ROW_PITCH, ROW_PITCH)
        x = xbuf[:, :D_MODEL].astype(BF16)
        gates = xbuf[:, D_MODEL:]
        a = _mm(x, wup_s[...])
        f = D_EXPERT
        h_lo = _silu(a[:, 0:f]) * a[:, f:2 * f] * gates[:, 0:1]
        h_hi = _silu(a[:, 2 * f:3 * f]) * a[:, 3 * f:4 * f] * gates[:, 1:2]
        ybuf[:, :D_MODEL] = _mm(jnp.concatenate([h_lo, h_hi], axis=1).astype(BF16), wdn_s[...])
        ybuf[:, D_MODEL:] = jnp.zeros((TM, LANES), F32)
        _to_row_linear(ys_ref, ybuf, ROW_PITCH)


def _experts(layer, tix, lo, hi, valid, fresh, xs, wg, wu, wd):
    n_tiles = xs.shape[0] // (TM * ROW_PITCH)
    row = lambda i, tix, lo, hi, valid, fresh: (tix[i], 0)
    row_out = lambda i, tix, lo, hi, valid, fresh: (i, 0)
    e_lo = lambda i, tix, lo, hi, valid, fresh: (layer, lo[i], 0, 0)
    e_hi = lambda i, tix, lo, hi, valid, fresh: (layer, hi[i], 0, 0)
    up = pl.BlockSpec((1, 1, D_MODEL, D_EXPERT), e_lo), pl.BlockSpec((1, 1, D_MODEL, D_EXPERT), e_hi)
    down = pl.BlockSpec((1, 1, D_EXPERT, D_MODEL), e_lo), pl.BlockSpec((1, 1, D_EXPERT, D_MODEL), e_hi)
    return pl.pallas_call(
        _experts_kernel,
        grid_spec=pltpu.PrefetchScalarGridSpec(
            num_scalar_prefetch=5, grid=(n_tiles,),
            in_specs=[pl.BlockSpec((TM * ROW_PITCH, LANES), row), *up, *up, *down],
            out_specs=pl.BlockSpec((TM * ROW_PITCH, LANES), row_out),
            scratch_shapes=[pltpu.VMEM((D_MODEL, 4 * D_EXPERT), BF16), pltpu.VMEM((2 * D_EXPERT, D_MODEL), BF16),
                            pltpu.VMEM((TM, PAY_WIDTH), F32), pltpu.VMEM((TM, PAY_WIDTH), F32)]),
        out_shape=jax.ShapeDtypeStruct(xs.shape, F32),
        compiler_params=_params("arbitrary"),
        name="experts",
    )(tix, lo, hi, valid, fresh, xs, wg, wg, wu, wu, wd, wd)


def _ple_kernel(n_prompt_tiles, pos_ref, yp_ref, ys_ref, moe_ref, pp_ref, ps_ref, g_ref, wg_ref, wp_ref, op_ref, os_ref,
                mring, mbuf, sems):
    i = pl.program_id(0)
    t = mbuf.shape[0]
    w = D_MODEL // LANES

    def gather(tile, slot):
        def body(jj, carry):
            row = pos_ref[tile * t + jj] * ROW_PITCH
            pltpu.make_async_copy(moe_ref.at[pl.ds(row, w)], mring.at[slot, pl.ds(jj * w, w)], sems.at[slot]).start()
            return carry

        lax.fori_loop(0, t, body, 0, unroll=8)

    def wait_tile(slot):
        pltpu.make_async_copy(moe_ref.at[pl.ds(0, t * w)], mring.at[slot], sems.at[slot]).wait()

    @pl.when(i == 0)
    def _():
        gather(0, 0)
        gather(1, 1)

    slot = lax.rem(i, PLE_RING)
    nslot = lax.rem(i + 2, PLE_RING)
    wait_tile(slot)
    _from_row_linear(mbuf, mring.at[slot], w, w)
    y2 = _pick(i, n_prompt_tiles, yp_ref, ys_ref) + mbuf[...]
    hn = _rms(y2, g_ref[...]).astype(BF16)
    gt = _sigmoid(_mm(hn, wg_ref[...]))
    pr = _mm(jnp.where(i < n_prompt_tiles, pp_ref[0], ps_ref[...]).astype(BF16), wp_ref[...])
    y3 = y2 + gt * pr
    nxt = jnp.minimum(i + 2, n_prompt_tiles) * t
    for jj in range(t):
        row = pos_ref[nxt + jj] * ROW_PITCH
        pltpu.make_async_copy(moe_ref.at[pl.ds(row, w)], mring.at[nslot, pl.ds(jj * w, w)], sems.at[nslot]).start()

    @pl.when(i < n_prompt_tiles)
    def _():
        op_ref[...] = y3

    @pl.when(i >= n_prompt_tiles)
    def _():
        os_ref[...] = y3
        wait_tile(lax.rem(i + 1, PLE_RING))
        wait_tile(nslot)


def _ple(layer, pos, y_p, y_s, ys_sorted, p_p, p_s, g, wg, wp):
    t = y_s.shape[0]
    npt = y_p.shape[0] // t
    w = D_MODEL // LANES
    pidx = lambda i, pos: (jnp.minimum(i, npt - 1), 0)
    full = lambda shape: pl.BlockSpec(shape, lambda i, pos: (0,) * len(shape))
    return pl.pallas_call(
        functools.partial(_ple_kernel, npt),
        grid_spec=pltpu.PrefetchScalarGridSpec(
            num_scalar_prefetch=1, grid=(npt + 1,),
            in_specs=[pl.BlockSpec((t, D_MODEL), pidx), full(y_s.shape), pl.BlockSpec(memory_space=pl.ANY),
                      pl.BlockSpec((1, t, PLE_DIM), lambda i, pos: (layer, jnp.minimum(i, npt - 1), 0)),
                      full(p_s.shape), full(g.shape), full(wg.shape), full(wp.shape)],
            out_specs=[pl.BlockSpec((t, D_MODEL), pidx), full(y_s.shape)],
            scratch_shapes=[pltpu.VMEM((PLE_RING, t * w, LANES), F32), pltpu.VMEM((t, D_MODEL), F32),
                            pltpu.SemaphoreType.DMA((PLE_RING,))]),
        out_shape=[jax.ShapeDtypeStruct(y_p.shape, F32), jax.ShapeDtypeStruct(y_s.shape, F32)],
        compiler_params=_params("arbitrary"),
        name="ple",
    )(pos, y_p, y_s, ys_sorted, p_p, p_s, g, wg, wp)


def _bucket_experts():
    pairs = [(a, b) for a in range(EXPERTS_PER_GROUP) for b in range(a + 1, EXPERTS_PER_GROUP)]
    lo = [g * EXPERTS_PER_GROUP + a for g in range(N_EXPERT_GROUPS) for a, _ in pairs]
    hi = [g * EXPERTS_PER_GROUP + b for g in range(N_EXPERT_GROUPS) for _, b in pairs]
    return jnp.array(lo, jnp.int32), jnp.array(hi, jnp.int32)


def _moe_ple(layer, y_p, y_s, gffn, wr, br, wg, wu, wd, p_p, p_s, gple, pwg, pwp):
    t = y_s.shape[0]
    n = y_p.shape[0] + t
    tri = jnp.asarray(np.tri(t, t, -1, dtype=np.float32), BF16)
    payload, meta, counts = _route_call(y_p, y_s, gffn, wr, br, tri)

    bucket, rank = meta[0], meta[1]
    cnt = counts[0, :N_BUCKETS].astype(jnp.int32)
    padded = (cnt + TM - 1) // TM * TM
    ends = jnp.cumsum(padded)
    starts = ends - padded
    in_bucket = bucket[:, None] == jnp.arange(N_BUCKETS, dtype=jnp.int32)[None, :]
    pos = rank + jnp.sum(jnp.where(in_bucket, starts[None, :], 0), axis=1)
    n_rows = n + N_BUCKETS * TM
    n_tiles = n_rows // TM
    tile_start = jnp.arange(n_tiles, dtype=jnp.int32) * TM
    valid = tile_start < ends[-1]
    used = (ends[-1] // TM).astype(jnp.int32)
    tix = jnp.arange(n_tiles, dtype=jnp.int32)
    tile_bucket = jnp.sum((ends[None, :] <= (jnp.minimum(tix, used - 1) * TM)[:, None]).astype(jnp.int32), axis=1)
    prev_bucket = jnp.concatenate([jnp.full((1,), -1, jnp.int32), tile_bucket[:-1]])
    fresh = valid & (tile_bucket != prev_bucket)
    first_tile = jnp.where(fresh, tix, n_tiles)
    next_first = lax.cummin(jnp.concatenate([first_tile[1:], jnp.full((1,), n_tiles, jnp.int32)]), reverse=True)
    next_bucket = jnp.where(next_first < n_tiles, jnp.take(tile_bucket, jnp.minimum(next_first, n_tiles - 1)),
                            tile_bucket)
    resident = jnp.where(fresh, tile_bucket, next_bucket)
    lo_tab, hi_tab = _bucket_experts()
    lo = jnp.take(lo_tab, resident)
    hi = jnp.take(hi_tab, resident)

    xs = _permute(pos, (ends - TM).astype(jnp.int32), (cnt > 0).astype(jnp.int32), used.reshape(1), payload, n_rows)
    ys = _experts(layer, jnp.minimum(tix, used - 1), lo, hi, valid.astype(jnp.int32), fresh.astype(jnp.int32), xs, wg,
                  wu, wd)
    return _ple(layer, pos, y_p, y_s, ys, p_p, p_s, gple, pwg, pwp)


def _split_bf16(a):
    hi = a.astype(BF16)
    lo = (a - hi.astype(F32)).astype(BF16)
    return hi, lo


def _head_norm(a, ind_ref, indt_ref, gain):
    hi, lo = _split_bf16(a * a)
    ss = _mm(hi, ind_ref[...]) + _mm(lo, ind_ref[...])
    inv = lax.rsqrt(ss * (1.0 / HEAD_DIM) + EPS)
    invb = _mm(jnp.concatenate(_split_bf16(inv), axis=1), indt_ref[...])
    return a * invb * gain


def _qkv_kernel(x_ref, g_ref, w_ref, iq_ref, iqt_ref, ik_ref, ikt_ref, qg_ref, kg_ref, q_ref, k_ref, v_ref):
    h = _rms(x_ref[...], g_ref[...]).astype(BF16)
    qkv = _mm(h, w_ref[...])
    nq = N_HEADS * HEAD_DIM
    q = _head_norm(qkv[:, :nq], iq_ref, iqt_ref, qg_ref[...])
    k = _head_norm(qkv[:, nq:nq + KV_DIM], ik_ref, ikt_ref, kg_ref[...])
    q_ref[...] = (q * (HEAD_DIM ** -0.5)).astype(BF16)
    k_ref[...] = k
    v_ref[...] = qkv[:, nq + KV_DIM:]


def _qkv(x, g, w, iq, iqt, ik, ikt, qg, kg):
    n = x.shape[0]
    t = _row_tile(n, big=True)
    row = lambda i: (i, 0)
    return pl.pallas_call(
        _qkv_kernel,
        grid=(n // t,),
        in_specs=[pl.BlockSpec((t, D_MODEL), row)] + [_full(a.shape) for a in (g, w, iq, iqt, ik, ikt, qg, kg)],
        out_specs=[pl.BlockSpec((t, N_HEADS * HEAD_DIM), row), pl.BlockSpec((t, KV_DIM), row),
                   pl.BlockSpec((t, KV_DIM), row)],
        out_shape=[jax.ShapeDtypeStruct((n, N_HEADS * HEAD_DIM), BF16), jax.ShapeDtypeStruct((n, KV_DIM), F32),
                   jax.ShapeDtypeStruct((n, KV_DIM), F32)],
        compiler_params=_params("arbitrary"),
        name="qkv",
    )(x, g, w, iq, iqt, ik, ikt, qg, kg)


def _dup_heads(a):
    out = []
    for s in range(KV_DIM // LANES):
        sl = a[:, s * LANES:(s + 1) * LANES]
        sw = pltpu.roll(sl, HEAD_DIM, axis=1)
        low = lax.broadcasted_iota(jnp.int32, sl.shape, 1) < HEAD_DIM
        out.append(jnp.where(low, sl, sw))
        out.append(jnp.where(low, sw, sl))
    return jnp.concatenate(out, axis=1).astype(BF16)


def _attend(q_rows, k2, v2, bias_of, sink_of, extra_mask):
    m_rows = q_rows.shape[0]
    low_q = lax.broadcasted_iota(jnp.int32, (m_rows, LANES), 1) < HEAD_DIM
    low_k = lax.broadcasted_iota(jnp.int32, (k2.shape[0], LANES), 1) < HEAD_DIM
    zero_q = jnp.zeros((m_rows, LANES), BF16)
    zero_k = jnp.zeros((k2.shape[0], LANES), BF16)
    slabs = []
    for g in range(N_KV_HEADS):
        kg = k2[:, g * LANES:(g + 1) * LANES]
        vg = v2[:, g * LANES:(g + 1) * LANES]
        v_lo = jnp.where(low_k, vg, zero_k)
        v_hi = jnp.where(low_k, zero_k, vg)
        lhs = []
        for a in range(GROUP):
            h = g * GROUP + a
            qs = q_rows[:, (h // 2) * LANES:(h // 2 + 1) * LANES]
            lhs.append(jnp.where(low_q, qs, zero_q) if h % 2 == 0 else jnp.where(low_q, zero_q, qs))
        s = lax.dot_general(jnp.concatenate(lhs, axis=0), kg, (((1,), (1,)), ((), ())), preferred_element_type=F32)
        probs, rinv = [], []
        for a in range(GROUP):
            h = g * GROUP + a
            sa = s[a * m_rows:(a + 1) * m_rows] + bias_of(h)
            if extra_mask is not None:
                sa = jnp.where(extra_mask, NEG_INF, sa)
            sink = sink_of(h)
            m = jnp.maximum(jnp.max(sa, axis=1, keepdims=True), sink)
            p = jnp.exp(sa - m)
            den = jnp.sum(p, axis=1, keepdims=True) + jnp.exp(sink - m)
            probs.append(p.astype(BF16))
            rinv.append(1.0 / den)
        v_pair = jnp.concatenate([v_lo, v_hi], axis=0)
        for sp in range(GROUP // 2):
            o = _mm(jnp.concatenate([probs[2 * sp], probs[2 * sp + 1]], axis=1), v_pair)
            slabs.append(o * jnp.where(low_q, rinv[2 * sp], rinv[2 * sp + 1]))
    return jnp.concatenate(slabs, axis=1)


def _attn_p_kernel(sink_ref, q_ref, k_ref, v_ref, x_ref, bias_ref, wo_ref, y_ref, kbuf, vbuf, obuf):
    t = q_ref.shape[0]
    i = pl.program_id(0)

    @pl.when(i == 0)
    def _():
        kbuf[0:WINDOW, :] = jnp.zeros((WINDOW, 2 * KV_DIM), BF16)
        vbuf[0:WINDOW, :] = jnp.zeros((WINDOW, 2 * KV_DIM), BF16)

    @pl.when(i > 0)
    def _():
        kbuf[0:WINDOW, :] = kbuf[t:t + WINDOW, :]
        vbuf[0:WINDOW, :] = vbuf[t:t + WINDOW, :]

    kbuf[WINDOW:, :] = _dup_heads(k_ref[...])
    vbuf[WINDOW:, :] = _dup_heads(v_ref[...])
    col = lax.broadcasted_iota(jnp.int32, (WINDOW, 2 * WINDOW), 1)
    for j in range(t // WINDOW):
        rows = slice(j * WINDOW, (j + 1) * WINDOW)
        keys = slice(j * WINDOW, (j + 2) * WINDOW)
        extra = jnp.logical_and(i == 0, col < WINDOW) if j == 0 else None
        o = _attend(q_ref[rows, :], kbuf[keys, :], vbuf[keys, :], lambda h: bias_ref[h], lambda h: sink_ref[h], extra)
        obuf[rows, :] = o.astype(BF16)
    y_ref[...] = x_ref[...] + _mm(obuf[...], wo_ref[...])


def _attn_p(sinks, q, k, v, x, bias, wo):
    n = q.shape[0]
    t = _row_tile(n)
    row = lambda i: (i, 0)
    return pl.pallas_call(
        _attn_p_kernel,
        grid=(n // t,),
        in_specs=[pl.BlockSpec(memory_space=pltpu.SMEM),
                  pl.BlockSpec((t, N_HEADS * HEAD_DIM), row), pl.BlockSpec((t, KV_DIM), row),
                  pl.BlockSpec((t, KV_DIM), row), pl.BlockSpec((t, D_MODEL), row),
                  _full(bias.shape), _full(wo.shape)],
        out_specs=pl.BlockSpec((t, D_MODEL), row),
        out_shape=jax.ShapeDtypeStruct((n, D_MODEL), F32),
        scratch_shapes=[pltpu.VMEM((t + WINDOW, 2 * KV_DIM), BF16), pltpu.VMEM((t + WINDOW, 2 * KV_DIM), BF16),
                        pltpu.VMEM((t, N_HEADS * HEAD_DIM), BF16)],
        compiler_params=_params("arbitrary"),
        name="attn_prompt",
    )(sinks, q, k, v, x, bias, wo)


def _attn_s_kernel(sink_ref, q_ref, k_ref, v_ref, ck_ref, cv_ref, bias_ref, o_ref, nk_ref, nv_ref, shift_buf):
    pad = bias_ref.shape[2] - ck_ref.shape[0] - k_ref.shape[0]
    zpad = jnp.zeros((pad, KV_DIM), F32)
    k2 = _dup_heads(jnp.concatenate([ck_ref[...], k_ref[...], zpad], axis=0))
    v2 = _dup_heads(jnp.concatenate([cv_ref[...], v_ref[...], zpad], axis=0))
    o = _attend(q_ref[...], k2, v2, lambda h: bias_ref[h], lambda h: sink_ref[h], None)
    o_ref[...] = o.astype(BF16)
    t_new = k_ref.shape[0] // SAMPLE_SEQS
    for new_ref, old_ref, add_ref in ((nk_ref, ck_ref, k_ref), (nv_ref, cv_ref, v_ref)):
        for b in range(SAMPLE_SEQS):
            r = b * WINDOW
            shift_buf[0:WINDOW - t_new, :] = old_ref[r + t_new:r + WINDOW, :]
            shift_buf[WINDOW - t_new:WINDOW, :] = add_ref[b * t_new:(b + 1) * t_new, :]
            new_ref[b] = jnp.transpose(shift_buf[...])


def _attn_s(sinks, q, k, v, ck, cv, bias, t_new):
    n = q.shape[0]
    rows = SAMPLE_SEQS * t_new
    crow = SAMPLE_SEQS * WINDOW
    row = lambda i: (i, 0)
    return pl.pallas_call(
        _attn_s_kernel,
        grid=(n // rows,),
        in_specs=[pl.BlockSpec(memory_space=pltpu.SMEM),
                  pl.BlockSpec((rows, N_HEADS * HEAD_DIM), row), pl.BlockSpec((rows, KV_DIM), row),
                  pl.BlockSpec((rows, KV_DIM), row), pl.BlockSpec((crow, KV_DIM), row),
                  pl.BlockSpec((crow, KV_DIM), row), _full(bias.shape)],
        out_specs=[pl.BlockSpec((rows, N_HEADS * HEAD_DIM), row),
                   pl.BlockSpec((SAMPLE_SEQS, KV_DIM, WINDOW), lambda i: (i, 0, 0)),
                   pl.BlockSpec((SAMPLE_SEQS, KV_DIM, WINDOW), lambda i: (i, 0, 0))],
        out_shape=[jax.ShapeDtypeStruct((n, N_HEADS * HEAD_DIM), BF16),
                   jax.ShapeDtypeStruct((ck.shape[0] // WINDOW, KV_DIM, WINDOW), F32),
                   jax.ShapeDtypeStruct((cv.shape[0] // WINDOW, KV_DIM, WINDOW), F32)],
        scratch_shapes=[pltpu.VMEM((WINDOW, KV_DIM), F32)],
        compiler_params=_params("arbitrary"),
        name="attn_sample",
    )(sinks, q, k, v, ck, cv, bias)


def _proj_res_kernel(o_ref, x_ref, w_ref, y_ref):
    y_ref[...] = x_ref[...] + _mm(o_ref[...], w_ref[...])


def _proj_res(o, x, w):
    n = o.shape[0]
    t = _row_tile(n)
    row = lambda i: (i, 0)
    return pl.pallas_call(
        _proj_res_kernel,
        grid=(n // t,),
        in_specs=[pl.BlockSpec((t, o.shape[1]), row), pl.BlockSpec((t, D_MODEL), row), _full(w.shape)],
        out_specs=pl.BlockSpec((t, D_MODEL), row),
        out_shape=jax.ShapeDtypeStruct((n, D_MODEL), F32),
        compiler_params=_params("arbitrary"),
        name="proj_res",
    )(o, x, w)


def _alibi_slopes():
    return np.exp2(-8.0 * np.arange(1, N_HEADS + 1, dtype=np.float64) / N_HEADS).astype(np.float32)


def _band_bias(dist, allowed):
    b = -(_alibi_slopes()[:, None, None] * dist.astype(np.float32)[None])
    return jnp.asarray(np.where(allowed[None], b, np.float32(NEG_INF)).astype(np.float32))


def _prompt_bias():
    dist = WINDOW + np.arange(WINDOW)[:, None] - np.arange(2 * WINDOW)[None, :]
    return _band_bias(dist, (dist >= 0) & (dist <= WINDOW))


def _sample_bias(t_new, n_cols):
    c = np.arange(n_cols)
    n_cache = SAMPLE_SEQS * WINDOW
    n_new = SAMPLE_SEQS * t_new
    is_cache = c < n_cache
    is_new = (c >= n_cache) & (c < n_cache + n_new)
    seq_c = np.where(is_cache, c // WINDOW, (c - n_cache) // t_new)
    pos_c = np.where(is_cache, c % WINDOW, WINDOW + (c - n_cache) % t_new)
    r = np.arange(n_new)
    seq_r, tok_r = r // t_new, r % t_new
    dist = WINDOW + tok_r[:, None] - pos_c[None, :]
    allowed = (seq_r[:, None] == seq_c[None, :]) & (is_cache | is_new)[None, :] & (dist >= 0) & (dist <= WINDOW)
    return _band_bias(dist, allowed)


def _head_indicator(n_heads):
    ch = np.arange(n_heads * HEAD_DIM) // HEAD_DIM
    ind = (ch[:, None] == np.arange(LANES)[None, :]).astype(np.float32)
    return jnp.asarray(ind, BF16), jnp.asarray(np.concatenate([ind.T, ind.T], axis=0), BF16)


def kernel(x_prompt, x_sample, state_conv, cache_k, cache_v, p_prompt, p_sample, norm_mix, norm_ffn, norm_ple,
           conv_w_in, conv_b_in, conv_w_dw, conv_b_dw, conv_ln_g, conv_ln_b, conv_w_out, conv_b_out, attn_w_qkv,
           attn_q_norm, attn_k_norm, attn_sinks, attn_w_o, moe_w_rg, moe_b_rg, moe_w_re, moe_b_re, moe_w_gate,
           moe_w_up, moe_w_down, ple_w_gate, ple_w_proj):
    bp, seq, d = x_prompt.shape
    bs, t_new, _ = x_sample.shape
    assert bp == 1 and d == D_MODEL and seq % WINDOW == 0 and bs % SAMPLE_SEQS == 0
    assert seq % (bs * t_new) == 0 and (bs * t_new) % DMA_CHUNK == 0
    depth = norm_mix.shape[0]
    row2 = lambda a: a.reshape(1, -1)

    y_p = x_prompt.reshape(seq, d)
    y_s = x_sample.reshape(bs * t_new, d)
    conv_p, conv_s, k_p, v_p, k_s, v_s = [], [], [], [], [], []

    for i in range(depth):
        j = i // 2
        g_mix = row2(norm_mix[i])
        if i % 2 == 0:
            w_in = conv_w_in[j].astype(BF16)
            b_in = row2(conv_b_in[j])
            tail = (conv_w_dw[j], row2(conv_b_dw[j]), row2(conv_ln_g[j]), row2(conv_ln_b[j]),
                    conv_w_out[j].astype(BF16), row2(conv_b_out[j]))
            u_p = _conv_in(y_p, g_mix, w_in, b_in)
            u_s = _conv_in(y_s, g_mix, w_in, b_in)
            conv_p.append(u_p[seq - CONV_STATE:].reshape(1, CONV_STATE, d))
            y_p = _conv_out_p(u_p, y_p, *tail)
            ys_t, new_state = _conv_out_s(state_conv[j], u_s.reshape(bs, t_new, d),
                                          y_s.reshape(bs, t_new, d).transpose(1, 0, 2), *tail)
            conv_s.append(new_state)
            y_s = ys_t.transpose(1, 0, 2).reshape(bs * t_new, d)
        else:
            w_qkv = attn_w_qkv[j].astype(BF16)
            w_o = attn_w_o[j].astype(BF16)
            iq, iqt = _head_indicator(N_HEADS)
            ik, ikt = _head_indicator(N_KV_HEADS)
            qg = row2(jnp.tile(attn_q_norm[j], N_HEADS))
            kg = row2(jnp.tile(attn_k_norm[j], N_KV_HEADS))
            sinks = attn_sinks[j]
            q1, k1, v1 = _qkv(y_p, g_mix, w_qkv, iq, iqt, ik, ikt, qg, kg)
            q2, k2, v2 = _qkv(y_s, g_mix, w_qkv, iq, iqt, ik, ikt, qg, kg)
            k_p.append(k1[seq - WINDOW:].reshape(1, WINDOW, N_KV_HEADS, HEAD_DIM))
            v_p.append(v1[seq - WINDOW:].reshape(1, WINDOW, N_KV_HEADS, HEAD_DIM))
            y_p = _attn_p(sinks, q1, k1, v1, y_p, _prompt_bias(), w_o)
            n_cols = -(-(SAMPLE_SEQS * (WINDOW + t_new)) // LANES) * LANES
            o_s, nk, nv = _attn_s(sinks, q2, k2, v2, cache_k[j].reshape(bs * WINDOW, KV_DIM),
                                  cache_v[j].reshape(bs * WINDOW, KV_DIM), _sample_bias(t_new, n_cols), t_new)
            k_s.append(nk.reshape(bs, N_KV_HEADS, HEAD_DIM, WINDOW).transpose(0, 3, 1, 2))
            v_s.append(nv.reshape(bs, N_KV_HEADS, HEAD_DIM, WINDOW).transpose(0, 3, 1, 2))
            y_s = _proj_res(o_s, y_s, w_o)

        pad = ROUTER_LANES - N_EXPERT_GROUPS - N_EXPERTS
        w_r = jnp.concatenate([moe_w_rg[i], moe_w_re[i], jnp.zeros((d, pad), F32)], axis=1)
        b_r = jnp.concatenate([moe_b_rg[i], moe_b_re[i], jnp.zeros((pad,), F32)]).reshape(1, ROUTER_LANES)
        moe = (row2(norm_ffn[i]), jnp.concatenate(_split_bf16(w_r), axis=1), b_r, moe_w_gate, moe_w_up, moe_w_down)
        ple = (row2(norm_ple[i]), ple_w_gate[i].astype(BF16), ple_w_proj[i].astype(BF16))
        y_p, y_s = _moe_ple(i, y_p, y_s, *moe, p_prompt.reshape(depth, seq, PLE_DIM),
                            p_sample[i].reshape(bs * t_new, PLE_DIM), *ple)

    return (y_p.reshape(1, seq, d), y_s.reshape(bs, t_new, d), jnp.stack(conv_p), jnp.stack(conv_s),
            jnp.stack(k_p), jnp.stack(v_p), jnp.stack(k_s), jnp.stack(v_s))
```

```python
import functools

import numpy as np

import jax
import jax.numpy as jnp
from jax import lax
from jax.experimental import pallas as pl
from jax.experimental.pallas import tpu as pltpu

F32 = jnp.float32
BF16 = jnp.bfloat16

D_MODEL = 1024
PLE_DIM = 256
CONV_WIDTH = 31
CONV_STATE = CONV_WIDTH - 1
N_HEADS = 16
N_KV_HEADS = 4
HEAD_DIM = 64
GROUP = N_HEADS // N_KV_HEADS
WINDOW = 128
KV_DIM = N_KV_HEADS * HEAD_DIM
N_EXPERT_GROUPS = 4
EXPERTS_PER_GROUP = 4
N_EXPERTS = 16
D_EXPERT = 256
EPS = 1e-6
NEG_INF = -1e30

LANES = 128
ROUTER_LANES = LANES
EXPERT_LANE0 = N_EXPERT_GROUPS
HALO = 32
CONV_ROWS = 32
NORM_ROWS = 512
CONV_PITCH = D_MODEL // LANES + 1
SAMPLE_SEQS = 8
PAIRS_PER_GROUP = EXPERTS_PER_GROUP * (EXPERTS_PER_GROUP - 1) // 2
N_BUCKETS = N_EXPERT_GROUPS * PAIRS_PER_GROUP
TM = 256
PAY_WIDTH = D_MODEL + LANES
ROW_PITCH = PAY_WIDTH // LANES
DMA_CHUNK = 512
RING = 3
PLE_RING = 3
VMEM_LIMIT = 48 * 1024 * 1024


def _row_tile(n, big=False):
    if big and n % 1024 == 0:
        return 1024
    return 512 if n % 512 == 0 else n


def _params(*sem):
    return pltpu.CompilerParams(dimension_semantics=sem, vmem_limit_bytes=VMEM_LIMIT)


def _full(shape):
    nd = len(shape)
    return pl.BlockSpec(shape, lambda *_: (0,) * nd)


def _rms(x, g):
    ms = jnp.mean(x * x, axis=-1, keepdims=True)
    return x * lax.rsqrt(ms + EPS) * g


def _sigmoid(x):
    return 1.0 / (1.0 + jnp.exp(-x))


def _mm(a, b):
    return jnp.dot(a, b, preferred_element_type=F32)


def _conv_in_kernel(x_ref, g_ref, w_ref, b_ref, u_ref):
    h = _rms(x_ref[...], g_ref[...]).astype(BF16)
    z = _mm(h, w_ref[...]) + b_ref[...]
    u_ref[...] = z[:, :D_MODEL] * _sigmoid(z[:, D_MODEL:])


def _conv_in(x, g, w, b):
    n = x.shape[0]
    t = _row_tile(n, big=True)
    return pl.pallas_call(
        _conv_in_kernel,
        grid=(n // t,),
        in_specs=[pl.BlockSpec((t, D_MODEL), lambda i: (i, 0)), _full(g.shape), _full(w.shape), _full(b.shape)],
        out_specs=pl.BlockSpec((t, D_MODEL), lambda i: (i, 0)),
        out_shape=jax.ShapeDtypeStruct((n, D_MODEL), F32),
        compiler_params=_params("arbitrary"),
        name="conv_in",
    )(x, g, w, b)


def _ln_silu(c, g, b):
    mu = jnp.mean(c, axis=-1, keepdims=True)
    xc = c - mu
    var = jnp.mean(xc * xc, axis=-1, keepdims=True)
    cn = xc * lax.rsqrt(var + EPS) * g + b
    return cn * _sigmoid(cn)


def _conv_out_p_kernel(u_ref, halo_ref, x_ref, wdw_ref, bdw_ref, lng_ref, lnb_ref, wout_ref, bout_ref, y_ref,
                       ubuf, cbuf, hbuf):
    t = u_ref.shape[0]
    i = pl.program_id(0)
    nj = D_MODEL // LANES

    def put(r, j, val):
        ubuf[pl.ds(r * CONV_PITCH + j, 8, stride=CONV_PITCH), :] = val

    for r in range(0, HALO, 8):
        for j in range(nj):
            put(r, j, jnp.where(i > 0, halo_ref[r:r + 8, j * LANES:(j + 1) * LANES], 0.0))

    def fill(rr, carry):
        r = pl.multiple_of(rr * 8, 8)
        for j in range(nj):
            put(r + HALO, j, u_ref[pl.ds(r, 8), j * LANES:(j + 1) * LANES])
        return carry

    lax.fori_loop(0, t // 8, fill, 0, unroll=4)

    def conv_chunk(rr, carry):
        r0 = pl.multiple_of(rr * CONV_ROWS, CONV_ROWS)
        for j in range(nj):
            lanes = slice(j * LANES, (j + 1) * LANES)
            accs = [None] * (CONV_ROWS // 8)
            for k in range(CONV_WIDTH):
                wk = wdw_ref[k:k + 1, lanes]
                for q in range(CONV_ROWS // 8):
                    r = r0 + (HALO - CONV_STATE + k + 8 * q)
                    term = wk * ubuf[pl.ds(r * CONV_PITCH + j, 8, stride=CONV_PITCH), :]
                    accs[q] = term if k == 0 else accs[q] + term
            for q in range(CONV_ROWS // 8):
                cbuf[pl.ds(r0 + 8 * q, 8), lanes] = accs[q]
        return carry

    lax.fori_loop(0, t // CONV_ROWS, conv_chunk, 0)

    def norm_chunk(rr, carry):
        r0 = pl.multiple_of(rr * NORM_ROWS, NORM_ROWS)
        c = cbuf[pl.ds(r0, NORM_ROWS), :] + bdw_ref[...]
        hbuf[pl.ds(r0, NORM_ROWS), :] = _ln_silu(c, lng_ref[...], lnb_ref[...]).astype(BF16)
        return carry

    lax.fori_loop(0, t // NORM_ROWS, norm_chunk, 0)
    y_ref[...] = x_ref[...] + _mm(hbuf[...], wout_ref[...]) + bout_ref[...]


def _conv_out_p(u, x, wdw, bdw, lng, lnb, wout, bout):
    n = u.shape[0]
    t = _row_tile(n)
    hb = t // HALO
    row = lambda i: (i, 0)
    return pl.pallas_call(
        _conv_out_p_kernel,
        grid=(n // t,),
        in_specs=[pl.BlockSpec((t, D_MODEL), row),
                  pl.BlockSpec((HALO, D_MODEL), lambda i: (jnp.maximum(i * hb - 1, 0), 0)),
                  pl.BlockSpec((t, D_MODEL), row),
                  _full(wdw.shape), _full(bdw.shape), _full(lng.shape), _full(lnb.shape),
                  _full(wout.shape), _full(bout.shape)],
        out_specs=pl.BlockSpec((t, D_MODEL), row),
        out_shape=jax.ShapeDtypeStruct((n, D_MODEL), F32),
        scratch_shapes=[pltpu.VMEM(((t + HALO) * CONV_PITCH, LANES), F32), pltpu.VMEM((t, D_MODEL), F32),
                        pltpu.VMEM((t, D_MODEL), BF16)],
        compiler_params=_params("arbitrary"),
        name="conv_out_prompt",
    )(u, u, x, wdw, bdw, lng, lnb, wout, bout)


def _conv_out_s_kernel(st_ref, u_ref, x_ref, wdw_ref, bdw_ref, lng_ref, lnb_ref, wout_ref, bout_ref, y_ref, ns_ref,
                       wbuf, cbuf):
    bb, nt, _ = u_ref.shape
    win = CONV_STATE + nt
    nj = D_MODEL // LANES
    seq_pitch = win * CONV_PITCH

    def put(b, r0, rows, j, val):
        wbuf[pl.ds((b * win + r0) * CONV_PITCH + j, rows, stride=CONV_PITCH), :] = val

    for b in range(bb):
        for j in range(nj):
            lanes = slice(j * LANES, (j + 1) * LANES)
            for r0 in range(0, CONV_STATE, 8):
                rows = min(8, CONV_STATE - r0)
                put(b, r0, rows, j, st_ref[b, r0:r0 + rows, lanes])
            put(b, CONV_STATE, nt, j, u_ref[b, :, lanes])

    for b in range(bb):
        for r0 in range(0, CONV_STATE, 8):
            rows = min(8, CONV_STATE - r0)
            for j in range(nj):
                src = (b * win + nt + r0) * CONV_PITCH + j
                ns_ref[b, r0:r0 + rows, j * LANES:(j + 1) * LANES] = wbuf[pl.ds(src, rows, stride=CONV_PITCH), :]

    for t in range(nt):
        for b0 in range(0, bb, 8):
            for j in range(nj):
                lanes = slice(j * LANES, (j + 1) * LANES)
                acc = None
                for k in range(CONV_WIDTH):
                    start = (b0 * win + t + k) * CONV_PITCH + j
                    term = wdw_ref[k:k + 1, lanes] * wbuf[pl.ds(start, 8, stride=seq_pitch), :]
                    acc = term if acc is None else acc + term
                cbuf[t * bb + b0:t * bb + b0 + 8, lanes] = acc
    c = cbuf[...] + bdw_ref[...]
    h = _ln_silu(c, lng_ref[...], lnb_ref[...]).astype(BF16)
    y = _mm(h, wout_ref[...]) + bout_ref[...]
    for t in range(nt):
        y_ref[t] = x_ref[t] + y[t * bb:(t + 1) * bb]


def _conv_out_s(state, u, x, wdw, bdw, lng, lnb, wout, bout):
    b, nt, _ = u.shape
    bb = 16 if b % 16 == 0 else b
    assert bb % 8 == 0
    win = CONV_STATE + nt
    seq = lambda i: (i, 0, 0)
    tm = lambda i: (0, i, 0)
    return pl.pallas_call(
        _conv_out_s_kernel,
        grid=(b // bb,),
        in_specs=[pl.BlockSpec((bb, CONV_STATE, D_MODEL), seq), pl.BlockSpec((bb, nt, D_MODEL), seq),
                  pl.BlockSpec((nt, bb, D_MODEL), tm),
                  _full(wdw.shape), _full(bdw.shape), _full(lng.shape), _full(lnb.shape),
                  _full(wout.shape), _full(bout.shape)],
        out_specs=[pl.BlockSpec((nt, bb, D_MODEL), tm), pl.BlockSpec((bb, CONV_STATE, D_MODEL), seq)],
        out_shape=[jax.ShapeDtypeStruct((nt, b, D_MODEL), F32), jax.ShapeDtypeStruct((b, CONV_STATE, D_MODEL), F32)],
        scratch_shapes=[pltpu.VMEM((bb * win * CONV_PITCH, LANES), F32), pltpu.VMEM((nt * bb, D_MODEL), F32)],
        compiler_params=_params("arbitrary"),
        name="conv_out_sample",
    )(state, u, x, wdw, bdw, lng, lnb, wout, bout)


def _route(lg):
    big = 3.0e38
    lane = lax.broadcasted_iota(jnp.int32, lg.shape, 1)
    lanef = lane.astype(F32)
    is_g = lane < N_EXPERT_GROUPS
    gl = jnp.where(is_g, lg, -big)
    gmax = jnp.max(gl, axis=1, keepdims=True)
    gsum = jnp.sum(jnp.where(is_g, jnp.exp(gl - gmax), 0.0), axis=1, keepdims=True)
    g_w = 1.0 / gsum
    g_idx = jnp.min(jnp.where(gl == gmax, lanef, big), axis=1, keepdims=True)
    rel = lanef - float(EXPERT_LANE0) - g_idx * float(EXPERTS_PER_GROUP)
    in_grp = jnp.where(rel >= 0.0, jnp.where(rel < float(EXPERTS_PER_GROUP), 1.0, 0.0), 0.0) > 0.5
    el = jnp.where(in_grp, lg, -big)
    e1 = jnp.max(el, axis=1, keepdims=True)
    i1 = jnp.min(jnp.where(el == e1, lanef, big), axis=1, keepdims=True)
    el2 = jnp.where(lanef == i1, -big, el)
    e2 = jnp.max(el2, axis=1, keepdims=True)
    i2 = jnp.min(jnp.where(el2 == e2, lanef, big), axis=1, keepdims=True)
    tt = jnp.exp(e2 - e1)
    w1 = g_w / (1.0 + tt)
    w2 = g_w * tt / (1.0 + tt)
    base = float(EXPERT_LANE0) + g_idx * float(EXPERTS_PER_GROUP)
    a = jnp.minimum(i1, i2) - base
    b = jnp.maximum(i1, i2) - base
    pair = a * (7.0 - a) * 0.5 + (b - a - 1.0)
    first_is_lo = i1 < i2
    return (g_idx * float(PAIRS_PER_GROUP) + pair, jnp.where(first_is_lo, w1, w2), jnp.where(first_is_lo, w2, w1))


def _to_row_linear(dst_ref, src_ref, n_tiles):
    def body(g, carry):
        r = pl.multiple_of(g * 8, 8)
        for j in range(n_tiles):
            dst_ref[pl.ds(r * ROW_PITCH + j, 8, stride=ROW_PITCH), :] = src_ref[pl.ds(r, 8), j * LANES:(j + 1) * LANES]
        return carry

    lax.fori_loop(0, src_ref.shape[0] // 8, body, 0, unroll=4)


def _from_row_linear(dst_ref, src_ref, n_tiles, pitch):
    def body(g, carry):
        r = pl.multiple_of(g * 8, 8)
        for j in range(n_tiles):
            dst_ref[pl.ds(r, 8), j * LANES:(j + 1) * LANES] = src_ref[pl.ds(r * pitch + j, 8, stride=pitch), :]
        return carry

    lax.fori_loop(0, dst_ref.shape[0] // 8, body, 0, unroll=4)


def _pick(i, n_prompt_tiles, prompt_ref, sample_ref):
    return jnp.where(i < n_prompt_tiles, prompt_ref[...], sample_ref[...])


def _route_kernel(n_prompt_tiles, yp_ref, ys_ref, g_ref, wr_ref, br_ref, tri_ref, pay_ref, meta_ref, cnt_ref, carry):
    i = pl.program_id(0)

    @pl.when(i == 0)
    def _():
        carry[...] = jnp.zeros_like(carry)

    xf = _rms(_pick(i, n_prompt_tiles, yp_ref, ys_ref), g_ref[...])
    x_hi, x_lo = _split_bf16(xf)
    part = _mm(x_hi, wr_ref[...]) + _mm(x_lo, wr_ref[...])
    logits = part[:, :ROUTER_LANES] + part[:, ROUTER_LANES:] + br_ref[...]
    bucket, w_lo, w_hi = _route(logits)
    lane = lax.broadcasted_iota(jnp.int32, logits.shape, 1)
    onehot = jnp.where(lane.astype(F32) == bucket, 1.0, 0.0)
    before = _mm(tri_ref[...], onehot.astype(BF16)) + carry[...]
    rank = jnp.sum(onehot * before, axis=1, keepdims=True)
    carry[...] += jnp.sum(onehot, axis=0, keepdims=True)
    cnt_ref[...] = carry[...]
    meta = jnp.where(lane == 0, bucket, jnp.where(lane == 1, rank, 0.0))
    meta_ref[...] = jnp.transpose(meta)[:8, :].astype(jnp.int32)

    gates = jnp.where(lane == 0, w_lo, jnp.where(lane == 1, w_hi, 0.0))
    nj = D_MODEL // LANES
    for r in range(0, xf.shape[0], 8):
        for j in range(nj):
            pay_ref[pl.ds(r * ROW_PITCH + j, 8, stride=ROW_PITCH), :] = xf[r:r + 8, j * LANES:(j + 1) * LANES]
        pay_ref[pl.ds(r * ROW_PITCH + nj, 8, stride=ROW_PITCH), :] = gates[r:r + 8, :]


def _route_call(y_p, y_s, g, wr, br, tri):
    t = y_s.shape[0]
    npt = y_p.shape[0] // t
    n = y_p.shape[0] + t
    pidx = lambda i: (jnp.minimum(i, npt - 1), 0)
    return pl.pallas_call(
        functools.partial(_route_kernel, npt),
        grid=(npt + 1,),
        in_specs=[pl.BlockSpec((t, D_MODEL), pidx), _full(y_s.shape), _full(g.shape), _full(wr.shape),
                  _full(br.shape), _full(tri.shape)],
        out_specs=[pl.BlockSpec((t * ROW_PITCH, LANES), lambda i: (i, 0)),
                   pl.BlockSpec((8, t), lambda i: (0, i)), _full((1, ROUTER_LANES))],
        out_shape=[jax.ShapeDtypeStruct((n * ROW_PITCH, LANES), F32),
                   jax.ShapeDtypeStruct((8, n), jnp.int32), jax.ShapeDtypeStruct((1, ROUTER_LANES), F32)],
        scratch_shapes=[pltpu.VMEM((1, ROUTER_LANES), F32)],
        compiler_params=_params("arbitrary"),
        name="route",
    )(y_p, y_s, g, wr, br, tri)


def _permute_kernel(pos_ref, zstart_ref, nz_ref, used_ref, src_ref, dst_ref, zbuf, ring, zsem, in_sems, out_sems):
    zbuf[...] = jnp.zeros_like(zbuf)

    def zero_tile(start):
        return pltpu.make_async_copy(zbuf, dst_ref.at[pl.ds(pl.multiple_of(start * ROW_PITCH, 8), TM * ROW_PITCH)], zsem)

    for b in range(N_BUCKETS):
        @pl.when(nz_ref[b] > 0)
        def _():
            zero_tile(zstart_ref[b]).start()
    n_tiles = dst_ref.shape[0] // (TM * ROW_PITCH)

    def start_unused(i, carry):
        zero_tile(i * TM).start()
        return carry

    lax.fori_loop(used_ref[0], n_tiles, start_unused, 0)
    for b in range(N_BUCKETS):
        @pl.when(nz_ref[b] > 0)
        def _():
            zero_tile(0).wait()

    def wait_unused(i, carry):
        zero_tile(0).wait()
        return carry

    lax.fori_loop(used_ref[0], n_tiles, wait_unused, 0)

    chunk_rows = DMA_CHUNK * ROW_PITCH
    n_chunks = src_ref.shape[0] // chunk_rows

    def fetch(c, slot):
        return pltpu.make_async_copy(src_ref.at[pl.ds(pl.multiple_of(c * chunk_rows, 8), chunk_rows)], ring.at[slot],
                                     in_sems.at[slot])

    def drain(slot):
        pltpu.make_async_copy(ring.at[slot], dst_ref.at[pl.ds(0, chunk_rows)], out_sems.at[slot]).wait()

    fetch(0, 0).start()

    def step(c, carry):
        slot = lax.rem(c, RING)
        nxt = lax.rem(c + 1, RING)
        fetch(c, slot).wait()

        @pl.when(c >= RING - 1)
        def _():
            drain(nxt)

        @pl.when(c + 1 < n_chunks)
        def _():
            fetch(c + 1, nxt).start()

        def send(jj, carry2):
            row = pos_ref[c * DMA_CHUNK + jj] * ROW_PITCH
            pltpu.make_async_copy(ring.at[slot, pl.ds(jj * ROW_PITCH, ROW_PITCH)], dst_ref.at[pl.ds(row, ROW_PITCH)],
                                  out_sems.at[slot]).start()
            return carry2

        lax.fori_loop(0, DMA_CHUNK, send, 0, unroll=8)
        return carry

    lax.fori_loop(0, n_chunks, step, 0)
    for c in range(n_chunks - (RING - 1), n_chunks):
        drain(c % RING)


def _permute(pos, zstart, nz, used, payload, n_rows_out):
    assert payload.shape[0] // (DMA_CHUNK * ROW_PITCH) >= RING
    return pl.pallas_call(
        _permute_kernel,
        grid_spec=pltpu.PrefetchScalarGridSpec(
            num_scalar_prefetch=4, grid=(1,),
            in_specs=[pl.BlockSpec(memory_space=pl.ANY)],
            out_specs=pl.BlockSpec(memory_space=pl.ANY),
            scratch_shapes=[pltpu.VMEM((TM * ROW_PITCH, LANES), payload.dtype),
                            pltpu.VMEM((RING, DMA_CHUNK * ROW_PITCH, LANES), payload.dtype),
                            pltpu.SemaphoreType.DMA(()), pltpu.SemaphoreType.DMA((RING,)),
                            pltpu.SemaphoreType.DMA((RING,))]),
        out_shape=jax.ShapeDtypeStruct((n_rows_out * ROW_PITCH, LANES), payload.dtype),
        compiler_params=_params("arbitrary"),
        name="permute",
    )(pos, zstart, nz, used, payload)


def _silu(x):
    return x * _sigmoid(x)


def _experts_kernel(tix_ref, lo_ref, hi_ref, valid_ref, fresh_ref, xs_ref, wgl_ref, wgh_ref, wul_ref, wuh_ref,
                    wdl_ref, wdh_ref, ys_ref, wup_s, wdn_s, xbuf, ybuf):
    i = pl.program_id(0)

    @pl.when(fresh_ref[i] > 0)
    def _():
        for s, ref in enumerate((wgl_ref, wul_ref, wgh_ref, wuh_ref)):
            wup_s[:, s * D_EXPERT:(s + 1) * D_EXPERT] = ref[0, 0].astype(BF16)
        for s, ref in enumerate((wdl_ref, wdh_ref)):
            wdn_s[s * D_EXPERT:(s + 1) * D_EXPERT, :] = ref[0, 0].astype(BF16)

    @pl.when(valid_ref[i] == 0)
    def _():
        ys_ref[...] = jnp.zeros_like(ys_ref)

    @pl.when(valid_ref[i] > 0)
    def _():
        _from_row_linear(xbuf, xs_ref, ROW_PITCH, ROW_PITCH)
        x = xbuf[:, :D_MODEL].astype(BF16)
        gates = xbuf[:, D_MODEL:]
        a = _mm(x, wup_s[...])
        f = D_EXPERT
        h_lo = _silu(a[:, 0:f]) * a[:, f:2 * f] * gates[:, 0:1]
        h_hi = _silu(a[:, 2 * f:3 * f]) * a[:, 3 * f:4 * f] * gates[:, 1:2]
        ybuf[:, :D_MODEL] = _mm(jnp.concatenate([h_lo, h_hi], axis=1).astype(BF16), wdn_s[...])
        ybuf[:, D_MODEL:] = jnp.zeros((TM, LANES), F32)
        _to_row_linear(ys_ref, ybuf, ROW_PITCH)


def _experts(layer, tix, lo, hi, valid, fresh, xs, wg, wu, wd):
    n_tiles = xs.shape[0] // (TM * ROW_PITCH)
    row = lambda i, tix, lo, hi, valid, fresh: (tix[i], 0)
    row_out = lambda i, tix, lo, hi, valid, fresh: (i, 0)
    e_lo = lambda i, tix, lo, hi, valid, fresh: (layer, lo[i], 0, 0)
    e_hi = lambda i, tix, lo, hi, valid, fresh: (layer, hi[i], 0, 0)
    up = pl.BlockSpec((1, 1, D_MODEL, D_EXPERT), e_lo), pl.BlockSpec((1, 1, D_MODEL, D_EXPERT), e_hi)
    down = pl.BlockSpec((1, 1, D_EXPERT, D_MODEL), e_lo), pl.BlockSpec((1, 1, D_EXPERT, D_MODEL), e_hi)
    return pl.pallas_call(
        _experts_kernel,
        grid_spec=pltpu.PrefetchScalarGridSpec(
            num_scalar_prefetch=5, grid=(n_tiles,),
            in_specs=[pl.BlockSpec((TM * ROW_PITCH, LANES), row), *up, *up, *down],
            out_specs=pl.BlockSpec((TM * ROW_PITCH, LANES), row_out),
            scratch_shapes=[pltpu.VMEM((D_MODEL, 4 * D_EXPERT), BF16), pltpu.VMEM((2 * D_EXPERT, D_MODEL), BF16),
                            pltpu.VMEM((TM, PAY_WIDTH), F32), pltpu.VMEM((TM, PAY_WIDTH), F32)]),
        out_shape=jax.ShapeDtypeStruct(xs.shape, F32),
        compiler_params=_params("arbitrary"),
        name="experts",
    )(tix, lo, hi, valid, fresh, xs, wg, wg, wu, wu, wd, wd)


def _ple_kernel(n_prompt_tiles, pos_ref, yp_ref, ys_ref, moe_ref, pp_ref, ps_ref, g_ref, wg_ref, wp_ref, op_ref, os_ref,
                mring, mbuf, sems):
    i = pl.program_id(0)
    t = mbuf.shape[0]
    w = D_MODEL // LANES

    def gather(tile, slot):
        def body(jj, carry):
            row = pos_ref[tile * t + jj] * ROW_PITCH
            pltpu.make_async_copy(moe_ref.at[pl.ds(row, w)], mring.at[slot, pl.ds(jj * w, w)], sems.at[slot]).start()
            return carry

        lax.fori_loop(0, t, body, 0, unroll=8)

    def wait_tile(slot):
        pltpu.make_async_copy(moe_ref.at[pl.ds(0, t * w)], mring.at[slot], sems.at[slot]).wait()

    @pl.when(i == 0)
    def _():
        gather(0, 0)
        gather(1, 1)

    slot = lax.rem(i, PLE_RING)
    nslot = lax.rem(i + 2, PLE_RING)
    wait_tile(slot)
    _from_row_linear(mbuf, mring.at[slot], w, w)
    y2 = _pick(i, n_prompt_tiles, yp_ref, ys_ref) + mbuf[...]
    hn = _rms(y2, g_ref[...]).astype(BF16)
    gt = _sigmoid(_mm(hn, wg_ref[...]))
    pr = _mm(jnp.where(i < n_prompt_tiles, pp_ref[0], ps_ref[...]).astype(BF16), wp_ref[...])
    y3 = y2 + gt * pr
    nxt = jnp.minimum(i + 2, n_prompt_tiles) * t
    for jj in range(t):
        row = pos_ref[nxt + jj] * ROW_PITCH
        pltpu.make_async_copy(moe_ref.at[pl.ds(row, w)], mring.at[nslot, pl.ds(jj * w, w)], sems.at[nslot]).start()

    @pl.when(i < n_prompt_tiles)
    def _():
        op_ref[...] = y3

    @pl.when(i >= n_prompt_tiles)
    def _():
        os_ref[...] = y3
        wait_tile(lax.rem(i + 1, PLE_RING))
        wait_tile(nslot)


def _ple(layer, pos, y_p, y_s, ys_sorted, p_p, p_s, g, wg, wp):
    t = y_s.shape[0]
    npt = y_p.shape[0] // t
    w = D_MODEL // LANES
    pidx = lambda i, pos: (jnp.minimum(i, npt - 1), 0)
    full = lambda shape: pl.BlockSpec(shape, lambda i, pos: (0,) * len(shape))
    return pl.pallas_call(
        functools.partial(_ple_kernel, npt),
        grid_spec=pltpu.PrefetchScalarGridSpec(
            num_scalar_prefetch=1, grid=(npt + 1,),
            in_specs=[pl.BlockSpec((t, D_MODEL), pidx), full(y_s.shape), pl.BlockSpec(memory_space=pl.ANY),
                      pl.BlockSpec((1, t, PLE_DIM), lambda i, pos: (layer, jnp.minimum(i, npt - 1), 0)),
                      full(p_s.shape), full(g.shape), full(wg.shape), full(wp.shape)],
            out_specs=[pl.BlockSpec((t, D_MODEL), pidx), full(y_s.shape)],
            scratch_shapes=[pltpu.VMEM((PLE_RING, t * w, LANES), F32), pltpu.VMEM((t, D_MODEL), F32),
                            pltpu.SemaphoreType.DMA((PLE_RING,))]),
        out_shape=[jax.ShapeDtypeStruct(y_p.shape, F32), jax.ShapeDtypeStruct(y_s.shape, F32)],
        compiler_params=_params("arbitrary"),
        name="ple",
    )(pos, y_p, y_s, ys_sorted, p_p, p_s, g, wg, wp)


def _bucket_experts():
    pairs = [(a, b) for a in range(EXPERTS_PER_GROUP) for b in range(a + 1, EXPERTS_PER_GROUP)]
    lo = [g * EXPERTS_PER_GROUP + a for g in range(N_EXPERT_GROUPS) for a, _ in pairs]
    hi = [g * EXPERTS_PER_GROUP + b for g in range(N_EXPERT_GROUPS) for _, b in pairs]
    return jnp.array(lo, jnp.int32), jnp.array(hi, jnp.int32)


def _moe_ple(layer, y_p, y_s, gffn, wr, br, wg, wu, wd, p_p, p_s, gple, pwg, pwp):
    t = y_s.shape[0]
    n = y_p.shape[0] + t
    tri = jnp.asarray(np.tri(t, t, -1, dtype=np.float32), BF16)
    payload, meta, counts = _route_call(y_p, y_s, gffn, wr, br, tri)

    bid = jnp.arange(N_BUCKETS, dtype=jnp.int32)

    def lookup(table, idx):
        return jnp.sum(jnp.where(idx[:, None] == bid[None, :], table[None, :], 0), axis=1)

    bucket, rank = meta[0], meta[1]
    cnt = counts[0, :N_BUCKETS].astype(jnp.int32)
    padded = (cnt + TM - 1) // TM * TM
    ends = jnp.sum(jnp.where(bid[None, :] <= bid[:, None], padded[None, :], 0), axis=1)
    starts = ends - padded
    total = jnp.sum(padded)
    pos = rank + lookup(starts, bucket)
    n_rows = n + N_BUCKETS * TM
    n_tiles = n_rows // TM
    tix = jnp.arange(n_tiles, dtype=jnp.int32)
    valid = tix * TM < total
    used = total // TM
    tin = jnp.minimum(tix, used - 1)
    tile_bucket = jnp.sum((ends[None, :] <= (tin * TM)[:, None]).astype(jnp.int32), axis=1)
    fresh = valid & (tix * TM == lookup(starts, tile_bucket))
    later = (bid[None, :] > bid[:, None]) & (cnt[None, :] > 0)
    next_bucket = jnp.min(jnp.where(later, bid[None, :], N_BUCKETS), axis=1)
    next_bucket = jnp.where(next_bucket < N_BUCKETS, next_bucket, bid)
    resident = jnp.where(fresh, tile_bucket, lookup(next_bucket, tile_bucket))
    lo_tab, hi_tab = _bucket_experts()
    lo = lookup(lo_tab, resident)
    hi = lookup(hi_tab, resident)

    xs = _permute(pos, (ends - TM).astype(jnp.int32), (cnt > 0).astype(jnp.int32), used.reshape(1), payload, n_rows)
    ys = _experts(layer, tin, lo, hi, valid.astype(jnp.int32), fresh.astype(jnp.int32), xs, wg, wu, wd)
    return _ple(layer, pos, y_p, y_s, ys, p_p, p_s, gple, pwg, pwp)


def _split_bf16(a):
    hi = a.astype(BF16)
    lo = (a - hi.astype(F32)).astype(BF16)
    return hi, lo


def _head_norm(a, ind_ref, indt_ref, gain):
    hi, lo = _split_bf16(a * a)
    ss = _mm(hi, ind_ref[...]) + _mm(lo, ind_ref[...])
    inv = lax.rsqrt(ss * (1.0 / HEAD_DIM) + EPS)
    invb = _mm(jnp.concatenate(_split_bf16(inv), axis=1), indt_ref[...])
    return a * invb * gain


def _qkv_kernel(x_ref, g_ref, w_ref, iq_ref, iqt_ref, ik_ref, ikt_ref, qg_ref, kg_ref, q_ref, k_ref, v_ref):
    h = _rms(x_ref[...], g_ref[...]).astype(BF16)
    qkv = _mm(h, w_ref[...])
    nq = N_HEADS * HEAD_DIM
    q = _head_norm(qkv[:, :nq], iq_ref, iqt_ref, qg_ref[...])
    k = _head_norm(qkv[:, nq:nq + KV_DIM], ik_ref, ikt_ref, kg_ref[...])
    q_ref[...] = (q * (HEAD_DIM ** -0.5)).astype(BF16)
    k_ref[...] = k
    v_ref[...] = qkv[:, nq + KV_DIM:]


def _qkv(x, g, w, iq, iqt, ik, ikt, qg, kg):
    n = x.shape[0]
    t = _row_tile(n, big=True)
    row = lambda i: (i, 0)
    return pl.pallas_call(
        _qkv_kernel,
        grid=(n // t,),
        in_specs=[pl.BlockSpec((t, D_MODEL), row)] + [_full(a.shape) for a in (g, w, iq, iqt, ik, ikt, qg, kg)],
        out_specs=[pl.BlockSpec((t, N_HEADS * HEAD_DIM), row), pl.BlockSpec((t, KV_DIM), row),
                   pl.BlockSpec((t, KV_DIM), row)],
        out_shape=[jax.ShapeDtypeStruct((n, N_HEADS * HEAD_DIM), BF16), jax.ShapeDtypeStruct((n, KV_DIM), F32),
                   jax.ShapeDtypeStruct((n, KV_DIM), F32)],
        compiler_params=_params("arbitrary"),
        name="qkv",
    )(x, g, w, iq, iqt, ik, ikt, qg, kg)


def _dup_heads(a):
    out = []
    for s in range(KV_DIM // LANES):
        sl = a[:, s * LANES:(s + 1) * LANES]
        sw = pltpu.roll(sl, HEAD_DIM, axis=1)
        low = lax.broadcasted_iota(jnp.int32, sl.shape, 1) < HEAD_DIM
        out.append(jnp.where(low, sl, sw))
        out.append(jnp.where(low, sw, sl))
    return jnp.concatenate(out, axis=1).astype(BF16)


def _attend(q_rows, k2, v2, bias_of, sink_of, extra_mask):
    m_rows = q_rows.shape[0]
    low_q = lax.broadcasted_iota(jnp.int32, (m_rows, LANES), 1) < HEAD_DIM
    low_k = lax.broadcasted_iota(jnp.int32, (k2.shape[0], LANES), 1) < HEAD_DIM
    zero_q = jnp.zeros((m_rows, LANES), BF16)
    zero_k = jnp.zeros((k2.shape[0], LANES), BF16)
    slabs = []
    for g in range(N_KV_HEADS):
        kg = k2[:, g * LANES:(g + 1) * LANES]
        vg = v2[:, g * LANES:(g + 1) * LANES]
        v_lo = jnp.where(low_k, vg, zero_k)
        v_hi = jnp.where(low_k, zero_k, vg)
        lhs = []
        for a in range(GROUP):
            h = g * GROUP + a
            qs = q_rows[:, (h // 2) * LANES:(h // 2 + 1) * LANES]
            lhs.append(jnp.where(low_q, qs, zero_q) if h % 2 == 0 else jnp.where(low_q, zero_q, qs))
        s = lax.dot_general(jnp.concatenate(lhs, axis=0), kg, (((1,), (1,)), ((), ())), preferred_element_type=F32)
        probs, rinv = [], []
        for a in range(GROUP):
            h = g * GROUP + a
            sa = s[a * m_rows:(a + 1) * m_rows] + bias_of(h)
            if extra_mask is not None:
                sa = jnp.where(extra_mask, NEG_INF, sa)
            sink = sink_of(h)
            m = jnp.maximum(jnp.max(sa, axis=1, keepdims=True), sink)
            p = jnp.exp(sa - m)
            den = jnp.sum(p, axis=1, keepdims=True) + jnp.exp(sink - m)
            probs.append(p.astype(BF16))
            rinv.append(1.0 / den)
        v_pair = jnp.concatenate([v_lo, v_hi], axis=0)
        for sp in range(GROUP // 2):
            o = _mm(jnp.concatenate([probs[2 * sp], probs[2 * sp + 1]], axis=1), v_pair)
            slabs.append(o * jnp.where(low_q, rinv[2 * sp], rinv[2 * sp + 1]))
    return jnp.concatenate(slabs, axis=1)


def _attn_p_kernel(sink_ref, q_ref, k_ref, v_ref, x_ref, bias_ref, wo_ref, y_ref, kbuf, vbuf, obuf):
    t = q_ref.shape[0]
    i = pl.program_id(0)

    @pl.when(i == 0)
    def _():
        kbuf[0:WINDOW, :] = jnp.zeros((WINDOW, 2 * KV_DIM), BF16)
        vbuf[0:WINDOW, :] = jnp.zeros((WINDOW, 2 * KV_DIM), BF16)

    @pl.when(i > 0)
    def _():
        kbuf[0:WINDOW, :] = kbuf[t:t + WINDOW, :]
        vbuf[0:WINDOW, :] = vbuf[t:t + WINDOW, :]

    kbuf[WINDOW:, :] = _dup_heads(k_ref[...])
    vbuf[WINDOW:, :] = _dup_heads(v_ref[...])
    col = lax.broadcasted_iota(jnp.int32, (WINDOW, 2 * WINDOW), 1)
    for j in range(t // WINDOW):
        rows = slice(j * WINDOW, (j + 1) * WINDOW)
        keys = slice(j * WINDOW, (j + 2) * WINDOW)
        extra = jnp.logical_and(i == 0, col < WINDOW) if j == 0 else None
        o = _attend(q_ref[rows, :], kbuf[keys, :], vbuf[keys, :], lambda h: bias_ref[h], lambda h: sink_ref[h], extra)
        obuf[rows, :] = o.astype(BF16)
    y_ref[...] = x_ref[...] + _mm(obuf[...], wo_ref[...])


def _attn_p(sinks, q, k, v, x, bias, wo):
    n = q.shape[0]
    t = _row_tile(n)
    row = lambda i: (i, 0)
    return pl.pallas_call(
        _attn_p_kernel,
        grid=(n // t,),
        in_specs=[pl.BlockSpec(memory_space=pltpu.SMEM),
                  pl.BlockSpec((t, N_HEADS * HEAD_DIM), row), pl.BlockSpec((t, KV_DIM), row),
                  pl.BlockSpec((t, KV_DIM), row), pl.BlockSpec((t, D_MODEL), row),
                  _full(bias.shape), _full(wo.shape)],
        out_specs=pl.BlockSpec((t, D_MODEL), row),
        out_shape=jax.ShapeDtypeStruct((n, D_MODEL), F32),
        scratch_shapes=[pltpu.VMEM((t + WINDOW, 2 * KV_DIM), BF16), pltpu.VMEM((t + WINDOW, 2 * KV_DIM), BF16),
                        pltpu.VMEM((t, N_HEADS * HEAD_DIM), BF16)],
        compiler_params=_params("arbitrary"),
        name="attn_prompt",
    )(sinks, q, k, v, x, bias, wo)


def _attn_s_kernel(sink_ref, q_ref, k_ref, v_ref, ck_ref, cv_ref, bias_ref, o_ref, nk_ref, nv_ref, shift_buf):
    pad = bias_ref.shape[2] - ck_ref.shape[0] - k_ref.shape[0]
    zpad = jnp.zeros((pad, KV_DIM), F32)
    k2 = _dup_heads(jnp.concatenate([ck_ref[...], k_ref[...], zpad], axis=0))
    v2 = _dup_heads(jnp.concatenate([cv_ref[...], v_ref[...], zpad], axis=0))
    o = _attend(q_ref[...], k2, v2, lambda h: bias_ref[h], lambda h: sink_ref[h], None)
    o_ref[...] = o.astype(BF16)
    t_new = k_ref.shape[0] // SAMPLE_SEQS
    for new_ref, old_ref, add_ref in ((nk_ref, ck_ref, k_ref), (nv_ref, cv_ref, v_ref)):
        for b in range(SAMPLE_SEQS):
            r = b * WINDOW
            shift_buf[0:WINDOW - t_new, :] = old_ref[r + t_new:r + WINDOW, :]
            shift_buf[WINDOW - t_new:WINDOW, :] = add_ref[b * t_new:(b + 1) * t_new, :]
            new_ref[b] = jnp.transpose(shift_buf[...])


def _attn_s(sinks, q, k, v, ck, cv, bias, t_new):
    n = q.shape[0]
    rows = SAMPLE_SEQS * t_new
    crow = SAMPLE_SEQS * WINDOW
    row = lambda i: (i, 0)
    return pl.pallas_call(
        _attn_s_kernel,
        grid=(n // rows,),
        in_specs=[pl.BlockSpec(memory_space=pltpu.SMEM),
                  pl.BlockSpec((rows, N_HEADS * HEAD_DIM), row), pl.BlockSpec((rows, KV_DIM), row),
                  pl.BlockSpec((rows, KV_DIM), row), pl.BlockSpec((crow, KV_DIM), row),
                  pl.BlockSpec((crow, KV_DIM), row), _full(bias.shape)],
        out_specs=[pl.BlockSpec((rows, N_HEADS * HEAD_DIM), row),
                   pl.BlockSpec((SAMPLE_SEQS, KV_DIM, WINDOW), lambda i: (i, 0, 0)),
                   pl.BlockSpec((SAMPLE_SEQS, KV_DIM, WINDOW), lambda i: (i, 0, 0))],
        out_shape=[jax.ShapeDtypeStruct((n, N_HEADS * HEAD_DIM), BF16),
                   jax.ShapeDtypeStruct((ck.shape[0] // WINDOW, KV_DIM, WINDOW), F32),
                   jax.ShapeDtypeStruct((cv.shape[0] // WINDOW, KV_DIM, WINDOW), F32)],
        scratch_shapes=[pltpu.VMEM((WINDOW, KV_DIM), F32)],
        compiler_params=_params("arbitrary"),
        name="attn_sample",
    )(sinks, q, k, v, ck, cv, bias)


def _proj_res_kernel(o_ref, x_ref, w_ref, y_ref):
    y_ref[...] = x_ref[...] + _mm(o_ref[...], w_ref[...])


def _proj_res(o, x, w):
    n = o.shape[0]
    t = _row_tile(n)
    row = lambda i: (i, 0)
    return pl.pallas_call(
        _proj_res_kernel,
        grid=(n // t,),
        in_specs=[pl.BlockSpec((t, o.shape[1]), row), pl.BlockSpec((t, D_MODEL), row), _full(w.shape)],
        out_specs=pl.BlockSpec((t, D_MODEL), row),
        out_shape=jax.ShapeDtypeStruct((n, D_MODEL), F32),
        compiler_params=_params("arbitrary"),
        name="proj_res",
    )(o, x, w)


def _alibi_slopes():
    return np.exp2(-8.0 * np.arange(1, N_HEADS + 1, dtype=np.float64) / N_HEADS).astype(np.float32)


def _band_bias(dist, allowed):
    b = -(_alibi_slopes()[:, None, None] * dist.astype(np.float32)[None])
    return jnp.asarray(np.where(allowed[None], b, np.float32(NEG_INF)).astype(np.float32))


def _prompt_bias():
    dist = WINDOW + np.arange(WINDOW)[:, None] - np.arange(2 * WINDOW)[None, :]
    return _band_bias(dist, (dist >= 0) & (dist <= WINDOW))


def _sample_bias(t_new, n_cols):
    c = np.arange(n_cols)
    n_cache = SAMPLE_SEQS * WINDOW
    n_new = SAMPLE_SEQS * t_new
    is_cache = c < n_cache
    is_new = (c >= n_cache) & (c < n_cache + n_new)
    seq_c = np.where(is_cache, c // WINDOW, (c - n_cache) // t_new)
    pos_c = np.where(is_cache, c % WINDOW, WINDOW + (c - n_cache) % t_new)
    r = np.arange(n_new)
    seq_r, tok_r = r // t_new, r % t_new
    dist = WINDOW + tok_r[:, None] - pos_c[None, :]
    allowed = (seq_r[:, None] == seq_c[None, :]) & (is_cache | is_new)[None, :] & (dist >= 0) & (dist <= WINDOW)
    return _band_bias(dist, allowed)


def _head_indicator(n_heads):
    ch = np.arange(n_heads * HEAD_DIM) // HEAD_DIM
    ind = (ch[:, None] == np.arange(LANES)[None, :]).astype(np.float32)
    return jnp.asarray(ind, BF16), jnp.asarray(np.concatenate([ind.T, ind.T], axis=0), BF16)


def kernel(x_prompt, x_sample, state_conv, cache_k, cache_v, p_prompt, p_sample, norm_mix, norm_ffn, norm_ple,
           conv_w_in, conv_b_in, conv_w_dw, conv_b_dw, conv_ln_g, conv_ln_b, conv_w_out, conv_b_out, attn_w_qkv,
           attn_q_norm, attn_k_norm, attn_sinks, attn_w_o, moe_w_rg, moe_b_rg, moe_w_re, moe_b_re, moe_w_gate,
           moe_w_up, moe_w_down, ple_w_gate, ple_w_proj):
    bp, seq, d = x_prompt.shape
    bs, t_new, _ = x_sample.shape
    assert bp == 1 and d == D_MODEL and seq % WINDOW == 0 and bs % SAMPLE_SEQS == 0
    assert seq % (bs * t_new) == 0 and (bs * t_new) % DMA_CHUNK == 0
    depth = norm_mix.shape[0]
    row2 = lambda a: a.reshape(1, -1)

    y_p = x_prompt.reshape(seq, d)
    y_s = x_sample.reshape(bs * t_new, d)
    conv_p, conv_s, k_p, v_p, k_s, v_s = [], [], [], [], [], []

    for i in range(depth):
        j = i // 2
        g_mix = row2(norm_mix[i])
        if i % 2 == 0:
            w_in = conv_w_in[j].astype(BF16)
            b_in = row2(conv_b_in[j])
            tail = (conv_w_dw[j], row2(conv_b_dw[j]), row2(conv_ln_g[j]), row2(conv_ln_b[j]),
                    conv_w_out[j].astype(BF16), row2(conv_b_out[j]))
            u_p = _conv_in(y_p, g_mix, w_in, b_in)
            u_s = _conv_in(y_s, g_mix, w_in, b_in)
            conv_p.append(u_p[seq - CONV_STATE:].reshape(1, CONV_STATE, d))
            y_p = _conv_out_p(u_p, y_p, *tail)
            ys_t, new_state = _conv_out_s(state_conv[j], u_s.reshape(bs, t_new, d),
                                          y_s.reshape(bs, t_new, d).transpose(1, 0, 2), *tail)
            conv_s.append(new_state)
            y_s = ys_t.transpose(1, 0, 2).reshape(bs * t_new, d)
        else:
            w_qkv = attn_w_qkv[j].astype(BF16)
            w_o = attn_w_o[j].astype(BF16)
            iq, iqt = _head_indicator(N_HEADS)
            ik, ikt = _head_indicator(N_KV_HEADS)
            qg = row2(jnp.tile(attn_q_norm[j], N_HEADS))
            kg = row2(jnp.tile(attn_k_norm[j], N_KV_HEADS))
            sinks = attn_sinks[j]
            q1, k1, v1 = _qkv(y_p, g_mix, w_qkv, iq, iqt, ik, ikt, qg, kg)
            q2, k2, v2 = _qkv(y_s, g_mix, w_qkv, iq, iqt, ik, ikt, qg, kg)
            k_p.append(k1[seq - WINDOW:].reshape(1, WINDOW, N_KV_HEADS, HEAD_DIM))
            v_p.append(v1[seq - WINDOW:].reshape(1, WINDOW, N_KV_HEADS, HEAD_DIM))
            y_p = _attn_p(sinks, q1, k1, v1, y_p, _prompt_bias(), w_o)
            n_cols = -(-(SAMPLE_SEQS * (WINDOW + t_new)) // LANES) * LANES
            o_s, nk, nv = _attn_s(sinks, q2, k2, v2, cache_k[j].reshape(bs * WINDOW, KV_DIM),
                                  cache_v[j].reshape(bs * WINDOW, KV_DIM), _sample_bias(t_new, n_cols), t_new)
            k_s.append(nk.reshape(bs, N_KV_HEADS, HEAD_DIM, WINDOW).transpose(0, 3, 1, 2))
            v_s.append(nv.reshape(bs, N_KV_HEADS, HEAD_DIM, WINDOW).transpose(0, 3, 1, 2))
            y_s = _proj_res(o_s, y_s, w_o)

        pad = ROUTER_LANES - N_EXPERT_GROUPS - N_EXPERTS
        w_r = jnp.concatenate([moe_w_rg[i], moe_w_re[i], jnp.zeros((d, pad), F32)], axis=1)
        b_r = jnp.concatenate([moe_b_rg[i], moe_b_re[i], jnp.zeros((pad,), F32)]).reshape(1, ROUTER_LANES)
        moe = (row2(norm_ffn[i]), jnp.concatenate(_split_bf16(w_r), axis=1), b_r, moe_w_gate, moe_w_up, moe_w_down)
        ple = (row2(norm_ple[i]), ple_w_gate[i].astype(BF16), ple_w_proj[i].astype(BF16))
        y_p, y_s = _moe_ple(i, y_p, y_s, *moe, p_prompt.reshape(depth, seq, PLE_DIM),
                            p_sample[i].reshape(bs * t_new, PLE_DIM), *ple)

    return (y_p.reshape(1, seq, d), y_s.reshape(bs, t_new, d), jnp.stack(conv_p), jnp.stack(conv_s),
            jnp.stack(k_p), jnp.stack(v_p), jnp.stack(k_s), jnp.stack(v_s))
```

```python
import functools

import numpy as np

import jax
import jax.numpy as jnp
from jax import lax
from jax.experimental import pallas as pl
from jax.experimental.pallas import tpu as pltpu

F32 = jnp.float32
BF16 = jnp.bfloat16

D_MODEL = 1024
PLE_DIM = 256
CONV_WIDTH = 31
CONV_STATE = CONV_WIDTH - 1
N_HEADS = 16
N_KV_HEADS = 4
HEAD_DIM = 64
GROUP = N_HEADS // N_KV_HEADS
WINDOW = 128
KV_DIM = N_KV_HEADS * HEAD_DIM
N_EXPERT_GROUPS = 4
EXPERTS_PER_GROUP = 4
N_EXPERTS = 16
D_EXPERT = 256
EPS = 1e-6
NEG_INF = -1e30

LANES = 128
ROUTER_LANES = LANES
EXPERT_LANE0 = N_EXPERT_GROUPS
HALO = 32
CONV_ROWS = 32
NORM_ROWS = 512
CONV_PITCH = D_MODEL // LANES + 1
SAMPLE_SEQS = 8
PAIRS_PER_GROUP = EXPERTS_PER_GROUP * (EXPERTS_PER_GROUP - 1) // 2
N_BUCKETS = N_EXPERT_GROUPS * PAIRS_PER_GROUP
TM = 256
PAY_WIDTH = D_MODEL + LANES
ROW_PITCH = PAY_WIDTH // LANES
DMA_CHUNK = 512
RING = 3
PLE_RING = 3
VMEM_LIMIT = 48 * 1024 * 1024


def _row_tile(n, big=False):
    if big and n % 1024 == 0:
        return 1024
    return 512 if n % 512 == 0 else n


def _params(*sem):
    return pltpu.CompilerParams(dimension_semantics=sem, vmem_limit_bytes=VMEM_LIMIT)


def _full(shape):
    nd = len(shape)
    return pl.BlockSpec(shape, lambda *_: (0,) * nd)


def _rms(x, g):
    ms = jnp.mean(x * x, axis=-1, keepdims=True)
    return x * lax.rsqrt(ms + EPS) * g


def _sigmoid(x):
    return 1.0 / (1.0 + jnp.exp(-x))


def _mm(a, b):
    return jnp.dot(a, b, preferred_element_type=F32)


def _conv_in_kernel(x_ref, g_ref, w_ref, b_ref, u_ref):
    h = _rms(x_ref[...], g_ref[...]).astype(BF16)
    z = _mm(h, w_ref[...]) + b_ref[...]
    u_ref[...] = z[:, :D_MODEL] * _sigmoid(z[:, D_MODEL:])


def _conv_in(x, g, w, b):
    n = x.shape[0]
    t = _row_tile(n, big=True)
    return pl.pallas_call(
        _conv_in_kernel,
        grid=(n // t,),
        in_specs=[pl.BlockSpec((t, D_MODEL), lambda i: (i, 0)), _full(g.shape), _full(w.shape), _full(b.shape)],
        out_specs=pl.BlockSpec((t, D_MODEL), lambda i: (i, 0)),
        out_shape=jax.ShapeDtypeStruct((n, D_MODEL), F32),
        compiler_params=_params("arbitrary"),
        name="conv_in",
    )(x, g, w, b)


def _ln_silu(c, g, b):
    mu = jnp.mean(c, axis=-1, keepdims=True)
    xc = c - mu
    var = jnp.mean(xc * xc, axis=-1, keepdims=True)
    cn = xc * lax.rsqrt(var + EPS) * g + b
    return cn * _sigmoid(cn)


def _conv_out_p_kernel(u_ref, halo_ref, x_ref, wdw_ref, bdw_ref, lng_ref, lnb_ref, wout_ref, bout_ref, y_ref,
                       ubuf, cbuf, hbuf):
    t = u_ref.shape[0]
    i = pl.program_id(0)
    nj = D_MODEL // LANES

    def put(r, j, val):
        ubuf[pl.ds(r * CONV_PITCH + j, 8, stride=CONV_PITCH), :] = val

    for r in range(0, HALO, 8):
        for j in range(nj):
            put(r, j, jnp.where(i > 0, halo_ref[r:r + 8, j * LANES:(j + 1) * LANES], 0.0))

    def fill(rr, carry):
        r = pl.multiple_of(rr * 8, 8)
        for j in range(nj):
            put(r + HALO, j, u_ref[pl.ds(r, 8), j * LANES:(j + 1) * LANES])
        return carry

    lax.fori_loop(0, t // 8, fill, 0, unroll=4)

    def conv_chunk(rr, carry):
        r0 = pl.multiple_of(rr * CONV_ROWS, CONV_ROWS)
        for j in range(nj):
            lanes = slice(j * LANES, (j + 1) * LANES)
            accs = [None] * (CONV_ROWS // 8)
            for k in range(CONV_WIDTH):
                wk = wdw_ref[k:k + 1, lanes]
                for q in range(CONV_ROWS // 8):
                    r = r0 + (HALO - CONV_STATE + k + 8 * q)
                    term = wk * ubuf[pl.ds(r * CONV_PITCH + j, 8, stride=CONV_PITCH), :]
                    accs[q] = term if k == 0 else accs[q] + term
            for q in range(CONV_ROWS // 8):
                cbuf[pl.ds(r0 + 8 * q, 8), lanes] = accs[q]
        return carry

    lax.fori_loop(0, t // CONV_ROWS, conv_chunk, 0)

    def norm_chunk(rr, carry):
        r0 = pl.multiple_of(rr * NORM_ROWS, NORM_ROWS)
        c = cbuf[pl.ds(r0, NORM_ROWS), :] + bdw_ref[...]
        hbuf[pl.ds(r0, NORM_ROWS), :] = _ln_silu(c, lng_ref[...], lnb_ref[...]).astype(BF16)
        return carry

    lax.fori_loop(0, t // NORM_ROWS, norm_chunk, 0)
    y_ref[...] = x_ref[...] + _mm(hbuf[...], wout_ref[...]) + bout_ref[...]


def _conv_out_p(u, x, wdw, bdw, lng, lnb, wout, bout):
    n = u.shape[0]
    t = _row_tile(n)
    hb = t // HALO
    row = lambda i: (i, 0)
    return pl.pallas_call(
        _conv_out_p_kernel,
        grid=(n // t,),
        in_specs=[pl.BlockSpec((t, D_MODEL), row),
                  pl.BlockSpec((HALO, D_MODEL), lambda i: (jnp.maximum(i * hb - 1, 0), 0)),
                  pl.BlockSpec((t, D_MODEL), row),
                  _full(wdw.shape), _full(bdw.shape), _full(lng.shape), _full(lnb.shape),
                  _full(wout.shape), _full(bout.shape)],
        out_specs=pl.BlockSpec((t, D_MODEL), row),
        out_shape=jax.ShapeDtypeStruct((n, D_MODEL), F32),
        scratch_shapes=[pltpu.VMEM(((t + HALO) * CONV_PITCH, LANES), F32), pltpu.VMEM((t, D_MODEL), F32),
                        pltpu.VMEM((t, D_MODEL), BF16)],
        compiler_params=_params("arbitrary"),
        name="conv_out_prompt",
    )(u, u, x, wdw, bdw, lng, lnb, wout, bout)


def _conv_out_s_kernel(st_ref, u_ref, x_ref, wdw_ref, bdw_ref, lng_ref, lnb_ref, wout_ref, bout_ref, y_ref, ns_ref,
                       wbuf, cbuf):
    bb, nt, _ = u_ref.shape
    win = CONV_STATE + nt
    nj = D_MODEL // LANES
    seq_pitch = win * CONV_PITCH

    def put(b, r0, rows, j, val):
        wbuf[pl.ds((b * win + r0) * CONV_PITCH + j, rows, stride=CONV_PITCH), :] = val

    for b in range(bb):
        for j in range(nj):
            lanes = slice(j * LANES, (j + 1) * LANES)
            for r0 in range(0, CONV_STATE, 8):
                rows = min(8, CONV_STATE - r0)
                put(b, r0, rows, j, st_ref[b, r0:r0 + rows, lanes])
            put(b, CONV_STATE, nt, j, u_ref[b, :, lanes])

    for b in range(bb):
        for r0 in range(0, CONV_STATE, 8):
            rows = min(8, CONV_STATE - r0)
            for j in range(nj):
                src = (b * win + nt + r0) * CONV_PITCH + j
                ns_ref[b, r0:r0 + rows, j * LANES:(j + 1) * LANES] = wbuf[pl.ds(src, rows, stride=CONV_PITCH), :]

    for t in range(nt):
        for b0 in range(0, bb, 8):
            for j in range(nj):
                lanes = slice(j * LANES, (j + 1) * LANES)
                acc = None
                for k in range(CONV_WIDTH):
                    start = (b0 * win + t + k) * CONV_PITCH + j
                    term = wdw_ref[k:k + 1, lanes] * wbuf[pl.ds(start, 8, stride=seq_pitch), :]
                    acc = term if acc is None else acc + term
                cbuf[t * bb + b0:t * bb + b0 + 8, lanes] = acc
    c = cbuf[...] + bdw_ref[...]
    h = _ln_silu(c, lng_ref[...], lnb_ref[...]).astype(BF16)
    y = _mm(h, wout_ref[...]) + bout_ref[...]
    for t in range(nt):
        y_ref[t] = x_ref[t] + y[t * bb:(t + 1) * bb]


def _conv_out_s(state, u, x, wdw, bdw, lng, lnb, wout, bout):
    b, nt, _ = u.shape
    bb = 16 if b % 16 == 0 else b
    assert bb % 8 == 0
    win = CONV_STATE + nt
    seq = lambda i: (i, 0, 0)
    tm = lambda i: (0, i, 0)
    return pl.pallas_call(
        _conv_out_s_kernel,
        grid=(b // bb,),
        in_specs=[pl.BlockSpec((bb, CONV_STATE, D_MODEL), seq), pl.BlockSpec((bb, nt, D_MODEL), seq),
                  pl.BlockSpec((nt, bb, D_MODEL), tm),
                  _full(wdw.shape), _full(bdw.shape), _full(lng.shape), _full(lnb.shape),
                  _full(wout.shape), _full(bout.shape)],
        out_specs=[pl.BlockSpec((nt, bb, D_MODEL), tm), pl.BlockSpec((bb, CONV_STATE, D_MODEL), seq)],
        out_shape=[jax.ShapeDtypeStruct((nt, b, D_MODEL), F32), jax.ShapeDtypeStruct((b, CONV_STATE, D_MODEL), F32)],
        scratch_shapes=[pltpu.VMEM((bb * win * CONV_PITCH, LANES), F32), pltpu.VMEM((nt * bb, D_MODEL), F32)],
        compiler_params=_params("arbitrary"),
        name="conv_out_sample",
    )(state, u, x, wdw, bdw, lng, lnb, wout, bout)


def _route(lg):
    big = 3.0e38
    lane = lax.broadcasted_iota(jnp.int32, lg.shape, 1)
    lanef = lane.astype(F32)
    is_g = lane < N_EXPERT_GROUPS
    gl = jnp.where(is_g, lg, -big)
    gmax = jnp.max(gl, axis=1, keepdims=True)
    gsum = jnp.sum(jnp.where(is_g, jnp.exp(gl - gmax), 0.0), axis=1, keepdims=True)
    g_w = 1.0 / gsum
    g_idx = jnp.min(jnp.where(gl == gmax, lanef, big), axis=1, keepdims=True)
    rel = lanef - float(EXPERT_LANE0) - g_idx * float(EXPERTS_PER_GROUP)
    in_grp = jnp.where(rel >= 0.0, jnp.where(rel < float(EXPERTS_PER_GROUP), 1.0, 0.0), 0.0) > 0.5
    el = jnp.where(in_grp, lg, -big)
    e1 = jnp.max(el, axis=1, keepdims=True)
    i1 = jnp.min(jnp.where(el == e1, lanef, big), axis=1, keepdims=True)
    el2 = jnp.where(lanef == i1, -big, el)
    e2 = jnp.max(el2, axis=1, keepdims=True)
    i2 = jnp.min(jnp.where(el2 == e2, lanef, big), axis=1, keepdims=True)
    tt = jnp.exp(e2 - e1)
    w1 = g_w / (1.0 + tt)
    w2 = g_w * tt / (1.0 + tt)
    base = float(EXPERT_LANE0) + g_idx * float(EXPERTS_PER_GROUP)
    a = jnp.minimum(i1, i2) - base
    b = jnp.maximum(i1, i2) - base
    pair = a * (7.0 - a) * 0.5 + (b - a - 1.0)
    first_is_lo = i1 < i2
    return (g_idx * float(PAIRS_PER_GROUP) + pair, jnp.where(first_is_lo, w1, w2), jnp.where(first_is_lo, w2, w1))


def _to_row_linear(dst_ref, src_ref, n_tiles):
    def body(g, carry):
        r = pl.multiple_of(g * 8, 8)
        for j in range(n_tiles):
            dst_ref[pl.ds(r * ROW_PITCH + j, 8, stride=ROW_PITCH), :] = src_ref[pl.ds(r, 8), j * LANES:(j + 1) * LANES]
        return carry

    lax.fori_loop(0, src_ref.shape[0] // 8, body, 0, unroll=4)


def _from_row_linear(dst_ref, src_ref, n_tiles, pitch):
    def body(g, carry):
        r = pl.multiple_of(g * 8, 8)
        for j in range(n_tiles):
            dst_ref[pl.ds(r, 8), j * LANES:(j + 1) * LANES] = src_ref[pl.ds(r * pitch + j, 8, stride=pitch), :]
        return carry

    lax.fori_loop(0, dst_ref.shape[0] // 8, body, 0, unroll=4)


def _pick(i, n_prompt_tiles, prompt_ref, sample_ref):
    return jnp.where(i < n_prompt_tiles, prompt_ref[...], sample_ref[...])


def _route_kernel(n_prompt_tiles, yp_ref, ys_ref, g_ref, wr_ref, br_ref, tri_ref, pay_ref, meta_ref, cnt_ref, carry):
    i = pl.program_id(0)

    @pl.when(i == 0)
    def _():
        carry[...] = jnp.zeros_like(carry)

    xf = _rms(_pick(i, n_prompt_tiles, yp_ref, ys_ref), g_ref[...])
    x_hi, x_lo = _split_bf16(xf)
    part = _mm(x_hi, wr_ref[...]) + _mm(x_lo, wr_ref[...])
    logits = part[:, :ROUTER_LANES] + part[:, ROUTER_LANES:] + br_ref[...]
    bucket, w_lo, w_hi = _route(logits)
    lane = lax.broadcasted_iota(jnp.int32, logits.shape, 1)
    onehot = jnp.where(lane.astype(F32) == bucket, 1.0, 0.0)
    before = _mm(tri_ref[...], onehot.astype(BF16)) + carry[...]
    rank = jnp.sum(onehot * before, axis=1, keepdims=True)
    carry[...] += jnp.sum(onehot, axis=0, keepdims=True)
    cnt_ref[...] = carry[...]
    meta = jnp.where(lane == 0, bucket, jnp.where(lane == 1, rank, 0.0))
    meta_ref[...] = jnp.transpose(meta)[:8, :].astype(jnp.int32)

    gates = jnp.where(lane == 0, w_lo, jnp.where(lane == 1, w_hi, 0.0))
    nj = D_MODEL // LANES
    for r in range(0, xf.shape[0], 8):
        for j in range(nj):
            pay_ref[pl.ds(r * ROW_PITCH + j, 8, stride=ROW_PITCH), :] = xf[r:r + 8, j * LANES:(j + 1) * LANES]
        pay_ref[pl.ds(r * ROW_PITCH + nj, 8, stride=ROW_PITCH), :] = gates[r:r + 8, :]


def _route_call(y_p, y_s, g, wr, br, tri):
    t = y_s.shape[0]
    npt = y_p.shape[0] // t
    n = y_p.shape[0] + t
    pidx = lambda i: (jnp.minimum(i, npt - 1), 0)
    return pl.pallas_call(
        functools.partial(_route_kernel, npt),
        grid=(npt + 1,),
        in_specs=[pl.BlockSpec((t, D_MODEL), pidx), _full(y_s.shape), _full(g.shape), _full(wr.shape),
                  _full(br.shape), _full(tri.shape)],
        out_specs=[pl.BlockSpec((t * ROW_PITCH, LANES), lambda i: (i, 0)),
                   pl.BlockSpec((8, t), lambda i: (0, i)), _full((1, ROUTER_LANES))],
        out_shape=[jax.ShapeDtypeStruct((n * ROW_PITCH, LANES), F32),
                   jax.ShapeDtypeStruct((8, n), jnp.int32), jax.ShapeDtypeStruct((1, ROUTER_LANES), F32)],
        scratch_shapes=[pltpu.VMEM((1, ROUTER_LANES), F32)],
        compiler_params=_params("arbitrary"),
        name="route",
    )(y_p, y_s, g, wr, br, tri)


def _permute_kernel(pos_ref, zstart_ref, nz_ref, used_ref, src_ref, dst_ref, zbuf, ring, zsem, in_sems, out_sems):
    zbuf[...] = jnp.zeros_like(zbuf)

    def zero_tile(start):
        return pltpu.make_async_copy(zbuf, dst_ref.at[pl.ds(pl.multiple_of(start * ROW_PITCH, 8), TM * ROW_PITCH)], zsem)

    for b in range(N_BUCKETS):
        @pl.when(nz_ref[b] > 0)
        def _():
            zero_tile(zstart_ref[b]).start()
    n_tiles = dst_ref.shape[0] // (TM * ROW_PITCH)

    def start_unused(i, carry):
        zero_tile(i * TM).start()
        return carry

    lax.fori_loop(used_ref[0], n_tiles, start_unused, 0)
    for b in range(N_BUCKETS):
        @pl.when(nz_ref[b] > 0)
        def _():
            zero_tile(0).wait()

    def wait_unused(i, carry):
        zero_tile(0).wait()
        return carry

    lax.fori_loop(used_ref[0], n_tiles, wait_unused, 0)

    chunk_rows = DMA_CHUNK * ROW_PITCH
    n_chunks = src_ref.shape[0] // chunk_rows

    def fetch(c, slot):
        return pltpu.make_async_copy(src_ref.at[pl.ds(pl.multiple_of(c * chunk_rows, 8), chunk_rows)], ring.at[slot],
                                     in_sems.at[slot])

    def drain(slot):
        pltpu.make_async_copy(ring.at[slot], dst_ref.at[pl.ds(0, chunk_rows)], out_sems.at[slot]).wait()

    fetch(0, 0).start()

    def step(c, carry):
        slot = lax.rem(c, RING)
        nxt = lax.rem(c + 1, RING)
        fetch(c, slot).wait()

        @pl.when(c >= RING - 1)
        def _():
            drain(nxt)

        @pl.when(c + 1 < n_chunks)
        def _():
            fetch(c + 1, nxt).start()

        def send(jj, carry2):
            row = pos_ref[c * DMA_CHUNK + jj] * ROW_PITCH
            pltpu.make_async_copy(ring.at[slot, pl.ds(jj * ROW_PITCH, ROW_PITCH)], dst_ref.at[pl.ds(row, ROW_PITCH)],
                                  out_sems.at[slot]).start()
            return carry2

        lax.fori_loop(0, DMA_CHUNK, send, 0, unroll=8)
        return carry

    lax.fori_loop(0, n_chunks, step, 0)
    for c in range(n_chunks - (RING - 1), n_chunks):
        drain(c % RING)


def _permute(pos, zstart, nz, used, payload, n_rows_out):
    assert payload.shape[0] // (DMA_CHUNK * ROW_PITCH) >= RING
    return pl.pallas_call(
        _permute_kernel,
        grid_spec=pltpu.PrefetchScalarGridSpec(
            num_scalar_prefetch=4, grid=(1,),
            in_specs=[pl.BlockSpec(memory_space=pl.ANY)],
            out_specs=pl.BlockSpec(memory_space=pl.ANY),
            scratch_shapes=[pltpu.VMEM((TM * ROW_PITCH, LANES), payload.dtype),
                            pltpu.VMEM((RING, DMA_CHUNK * ROW_PITCH, LANES), payload.dtype),
                            pltpu.SemaphoreType.DMA(()), pltpu.SemaphoreType.DMA((RING,)),
                            pltpu.SemaphoreType.DMA((RING,))]),
        out_shape=jax.ShapeDtypeStruct((n_rows_out * ROW_PITCH, LANES), payload.dtype),
        compiler_params=_params("arbitrary"),
        name="permute",
    )(pos, zstart, nz, used, payload)


def _silu(x):
    return x * _sigmoid(x)


def _experts_kernel(tix_ref, lo_ref, hi_ref, valid_ref, fresh_ref, xs_ref, wgl_ref, wgh_ref, wul_ref, wuh_ref,
                    wdl_ref, wdh_ref, ys_ref, wup_s, wdn_s, xbuf, ybuf, xring, xsems):
    i = pl.program_id(0)
    tile_rows = TM * ROW_PITCH

    def fetch(step, slot):
        start = pl.multiple_of(tix_ref[step] * tile_rows, 8)
        return pltpu.make_async_copy(xs_ref.at[pl.ds(start, tile_rows)], xring.at[slot], xsems.at[slot])

    @pl.when(i == 0)
    def _():
        fetch(0, 0).start()
        fetch(1, 1).start()

    @pl.when(i + 2 < pl.num_programs(0))
    def _():
        fetch(i + 2, lax.rem(i + 2, RING)).start()

    slot = lax.rem(i, RING)
    fetch(i, slot).wait()

    @pl.when(fresh_ref[i] > 0)
    def _():
        for s, ref in enumerate((wgl_ref, wul_ref, wgh_ref, wuh_ref)):
            wup_s[:, s * D_EXPERT:(s + 1) * D_EXPERT] = ref[0, 0].astype(BF16)
        for s, ref in enumerate((wdl_ref, wdh_ref)):
            wdn_s[s * D_EXPERT:(s + 1) * D_EXPERT, :] = ref[0, 0].astype(BF16)

    @pl.when(valid_ref[i] == 0)
    def _():
        ys_ref[...] = jnp.zeros_like(ys_ref)

    @pl.when(valid_ref[i] > 0)
    def _():
        _from_row_linear(xbuf, xring.at[slot], ROW_PITCH, ROW_PITCH)
        x = xbuf[:, :D_MODEL].astype(BF16)
        gates = xbuf[:, D_MODEL:]
        a = _mm(x, wup_s[...])
        f = D_EXPERT
        h_lo = _silu(a[:, 0:f]) * a[:, f:2 * f] * gates[:, 0:1]
        h_hi = _silu(a[:, 2 * f:3 * f]) * a[:, 3 * f:4 * f] * gates[:, 1:2]
        ybuf[:, :D_MODEL] = _mm(jnp.concatenate([h_lo, h_hi], axis=1).astype(BF16), wdn_s[...])
        ybuf[:, D_MODEL:] = jnp.zeros((TM, LANES), F32)
        _to_row_linear(ys_ref, ybuf, ROW_PITCH)


def _experts(layer, tix, lo, hi, valid, fresh, xs, wg, wu, wd):
    n_tiles = xs.shape[0] // (TM * ROW_PITCH)
    row = lambda i, tix, lo, hi, valid, fresh: (tix[i], 0)
    row_out = lambda i, tix, lo, hi, valid, fresh: (i, 0)
    e_lo = lambda i, tix, lo, hi, valid, fresh: (layer, lo[i], 0, 0)
    e_hi = lambda i, tix, lo, hi, valid, fresh: (layer, hi[i], 0, 0)
    up = pl.BlockSpec((1, 1, D_MODEL, D_EXPERT), e_lo), pl.BlockSpec((1, 1, D_MODEL, D_EXPERT), e_hi)
    down = pl.BlockSpec((1, 1, D_EXPERT, D_MODEL), e_lo), pl.BlockSpec((1, 1, D_EXPERT, D_MODEL), e_hi)
    return pl.pallas_call(
        _experts_kernel,
        grid_spec=pltpu.PrefetchScalarGridSpec(
            num_scalar_prefetch=5, grid=(n_tiles,),
            in_specs=[pl.BlockSpec(memory_space=pl.ANY), *up, *up, *down],
            out_specs=pl.BlockSpec((TM * ROW_PITCH, LANES), row_out),
            scratch_shapes=[pltpu.VMEM((D_MODEL, 4 * D_EXPERT), BF16), pltpu.VMEM((2 * D_EXPERT, D_MODEL), BF16),
                            pltpu.VMEM((TM, PAY_WIDTH), F32), pltpu.VMEM((TM, PAY_WIDTH), F32),
                            pltpu.VMEM((RING, TM * ROW_PITCH, LANES), F32), pltpu.SemaphoreType.DMA((RING,))]),
        out_shape=jax.ShapeDtypeStruct(xs.shape, F32),
        compiler_params=_params("arbitrary"),
        name="experts",
    )(tix, lo, hi, valid, fresh, xs, wg, wg, wu, wu, wd, wd)


def _ple_kernel(n_prompt_tiles, pos_ref, yp_ref, ys_ref, moe_ref, pp_ref, ps_ref, g_ref, wg_ref, wp_ref, op_ref, os_ref,
                mring, mbuf, sems):
    i = pl.program_id(0)
    t = mbuf.shape[0]
    w = D_MODEL // LANES

    def gather(tile, slot):
        def body(jj, carry):
            row = pos_ref[tile * t + jj] * ROW_PITCH
            pltpu.make_async_copy(moe_ref.at[pl.ds(row, w)], mring.at[slot, pl.ds(jj * w, w)], sems.at[slot]).start()
            return carry

        lax.fori_loop(0, t, body, 0, unroll=8)

    def wait_tile(slot):
        pltpu.make_async_copy(moe_ref.at[pl.ds(0, t * w)], mring.at[slot], sems.at[slot]).wait()

    @pl.when(i == 0)
    def _():
        gather(0, 0)
        gather(1, 1)

    slot = lax.rem(i, PLE_RING)
    nslot = lax.rem(i + 2, PLE_RING)
    wait_tile(slot)
    _from_row_linear(mbuf, mring.at[slot], w, w)
    y2 = _pick(i, n_prompt_tiles, yp_ref, ys_ref) + mbuf[...]
    hn = _rms(y2, g_ref[...]).astype(BF16)
    gt = _sigmoid(_mm(hn, wg_ref[...]))
    pr = _mm(jnp.where(i < n_prompt_tiles, pp_ref[0], ps_ref[...]).astype(BF16), wp_ref[...])
    y3 = y2 + gt * pr
    nxt = jnp.minimum(i + 2, n_prompt_tiles) * t
    for jj in range(t):
        row = pos_ref[nxt + jj] * ROW_PITCH
        pltpu.make_async_copy(moe_ref.at[pl.ds(row, w)], mring.at[nslot, pl.ds(jj * w, w)], sems.at[nslot]).start()

    @pl.when(i < n_prompt_tiles)
    def _():
        op_ref[...] = y3

    @pl.when(i >= n_prompt_tiles)
    def _():
        os_ref[...] = y3
        wait_tile(lax.rem(i + 1, PLE_RING))
        wait_tile(nslot)


def _ple(layer, pos, y_p, y_s, ys_sorted, p_p, p_s, g, wg, wp):
    t = y_s.shape[0]
    npt = y_p.shape[0] // t
    w = D_MODEL // LANES
    pidx = lambda i, pos: (jnp.minimum(i, npt - 1), 0)
    full = lambda shape: pl.BlockSpec(shape, lambda i, pos: (0,) * len(shape))
    return pl.pallas_call(
        functools.partial(_ple_kernel, npt),
        grid_spec=pltpu.PrefetchScalarGridSpec(
            num_scalar_prefetch=1, grid=(npt + 1,),
            in_specs=[pl.BlockSpec((t, D_MODEL), pidx), full(y_s.shape), pl.BlockSpec(memory_space=pl.ANY),
                      pl.BlockSpec((1, t, PLE_DIM), lambda i, pos: (layer, jnp.minimum(i, npt - 1), 0)),
                      full(p_s.shape), full(g.shape), full(wg.shape), full(wp.shape)],
            out_specs=[pl.BlockSpec((t, D_MODEL), pidx), full(y_s.shape)],
            scratch_shapes=[pltpu.VMEM((PLE_RING, t * w, LANES), F32), pltpu.VMEM((t, D_MODEL), F32),
                            pltpu.SemaphoreType.DMA((PLE_RING,))]),
        out_shape=[jax.ShapeDtypeStruct(y_p.shape, F32), jax.ShapeDtypeStruct(y_s.shape, F32)],
        compiler_params=_params("arbitrary"),
        name="ple",
    )(pos, y_p, y_s, ys_sorted, p_p, p_s, g, wg, wp)


def _bucket_experts():
    pairs = [(a, b) for a in range(EXPERTS_PER_GROUP) for b in range(a + 1, EXPERTS_PER_GROUP)]
    lo = [g * EXPERTS_PER_GROUP + a for g in range(N_EXPERT_GROUPS) for a, _ in pairs]
    hi = [g * EXPERTS_PER_GROUP + b for g in range(N_EXPERT_GROUPS) for _, b in pairs]
    return jnp.array(lo, jnp.int32), jnp.array(hi, jnp.int32)


def _moe_ple(layer, y_p, y_s, gffn, wr, br, wg, wu, wd, p_p, p_s, gple, pwg, pwp):
    t = y_s.shape[0]
    n = y_p.shape[0] + t
    tri = jnp.asarray(np.tri(t, t, -1, dtype=np.float32), BF16)
    payload, meta, counts = _route_call(y_p, y_s, gffn, wr, br, tri)

    bid = jnp.arange(N_BUCKETS, dtype=jnp.int32)

    def lookup(table, idx):
        return jnp.sum(jnp.where(idx[:, None] == bid[None, :], table[None, :], 0), axis=1)

    bucket, rank = meta[0], meta[1]
    cnt = counts[0, :N_BUCKETS].astype(jnp.int32)
    padded = (cnt + TM - 1) // TM * TM
    ends = jnp.sum(jnp.where(bid[None, :] <= bid[:, None], padded[None, :], 0), axis=1)
    starts = ends - padded
    total = jnp.sum(padded)
    pos = rank + lookup(starts, bucket)
    n_rows = n + N_BUCKETS * TM
    n_tiles = n_rows // TM
    tix = jnp.arange(n_tiles, dtype=jnp.int32)
    valid = tix * TM < total
    used = total // TM
    tin = jnp.minimum(tix, used - 1)
    tile_bucket = jnp.sum((ends[None, :] <= (tin * TM)[:, None]).astype(jnp.int32), axis=1)
    fresh = valid & (tix * TM == lookup(starts, tile_bucket))
    later = (bid[None, :] > bid[:, None]) & (cnt[None, :] > 0)
    next_bucket = jnp.min(jnp.where(later, bid[None, :], N_BUCKETS), axis=1)
    next_bucket = jnp.where(next_bucket < N_BUCKETS, next_bucket, bid)
    resident = jnp.where(fresh, tile_bucket, lookup(next_bucket, tile_bucket))
    lo_tab, hi_tab = _bucket_experts()
    lo = lookup(lo_tab, resident)
    hi = lookup(hi_tab, resident)

    xs = _permute(pos, (ends - TM).astype(jnp.int32), (cnt > 0).astype(jnp.int32), used.reshape(1), payload, n_rows)
    ys = _experts(layer, tin, lo, hi, valid.astype(jnp.int32), fresh.astype(jnp.int32), xs, wg, wu, wd)
    return _ple(layer, pos, y_p, y_s, ys, p_p, p_s, gple, pwg, pwp)


def _split_bf16(a):
    hi = a.astype(BF16)
    lo = (a - hi.astype(F32)).astype(BF16)
    return hi, lo


def _head_norm(a, ind_ref, indt_ref, gain):
    hi, lo = _split_bf16(a * a)
    ss = _mm(hi, ind_ref[...]) + _mm(lo, ind_ref[...])
    inv = lax.rsqrt(ss * (1.0 / HEAD_DIM) + EPS)
    invb = _mm(jnp.concatenate(_split_bf16(inv), axis=1), indt_ref[...])
    return a * invb * gain


def _qkv_kernel(x_ref, g_ref, w_ref, iq_ref, iqt_ref, ik_ref, ikt_ref, qg_ref, kg_ref, q_ref, k_ref, v_ref):
    h = _rms(x_ref[...], g_ref[...]).astype(BF16)
    qkv = _mm(h, w_ref[...])
    nq = N_HEADS * HEAD_DIM
    q = _head_norm(qkv[:, :nq], iq_ref, iqt_ref, qg_ref[...])
    k = _head_norm(qkv[:, nq:nq + KV_DIM], ik_ref, ikt_ref, kg_ref[...])
    q_ref[...] = (q * (HEAD_DIM ** -0.5)).astype(BF16)
    k_ref[...] = k
    v_ref[...] = qkv[:, nq + KV_DIM:]


def _qkv(x, g, w, iq, iqt, ik, ikt, qg, kg):
    n = x.shape[0]
    t = _row_tile(n, big=True)
    row = lambda i: (i, 0)
    return pl.pallas_call(
        _qkv_kernel,
        grid=(n // t,),
        in_specs=[pl.BlockSpec((t, D_MODEL), row)] + [_full(a.shape) for a in (g, w, iq, iqt, ik, ikt, qg, kg)],
        out_specs=[pl.BlockSpec((t, N_HEADS * HEAD_DIM), row), pl.BlockSpec((t, KV_DIM), row),
                   pl.BlockSpec((t, KV_DIM), row)],
        out_shape=[jax.ShapeDtypeStruct((n, N_HEADS * HEAD_DIM), BF16), jax.ShapeDtypeStruct((n, KV_DIM), F32),
                   jax.ShapeDtypeStruct((n, KV_DIM), F32)],
        compiler_params=_params("arbitrary"),
        name="qkv",
    )(x, g, w, iq, iqt, ik, ikt, qg, kg)


def _dup_heads(a):
    out = []
    for s in range(KV_DIM // LANES):
        sl = a[:, s * LANES:(s + 1) * LANES]
        sw = pltpu.roll(sl, HEAD_DIM, axis=1)
        low = lax.broadcasted_iota(jnp.int32, sl.shape, 1) < HEAD_DIM
        out.append(jnp.where(low, sl, sw))
        out.append(jnp.where(low, sw, sl))
    return jnp.concatenate(out, axis=1).astype(BF16)


def _attend(q_rows, k2, v2, bias_of, sink_of, extra_mask):
    m_rows = q_rows.shape[0]
    low_q = lax.broadcasted_iota(jnp.int32, (m_rows, LANES), 1) < HEAD_DIM
    low_k = lax.broadcasted_iota(jnp.int32, (k2.shape[0], LANES), 1) < HEAD_DIM
    zero_q = jnp.zeros((m_rows, LANES), BF16)
    zero_k = jnp.zeros((k2.shape[0], LANES), BF16)
    slabs = []
    for g in range(N_KV_HEADS):
        kg = k2[:, g * LANES:(g + 1) * LANES]
        vg = v2[:, g * LANES:(g + 1) * LANES]
        v_lo = jnp.where(low_k, vg, zero_k)
        v_hi = jnp.where(low_k, zero_k, vg)
        lhs = []
        for a in range(GROUP):
            h = g * GROUP + a
            qs = q_rows[:, (h // 2) * LANES:(h // 2 + 1) * LANES]
            lhs.append(jnp.where(low_q, qs, zero_q) if h % 2 == 0 else jnp.where(low_q, zero_q, qs))
        s = lax.dot_general(jnp.concatenate(lhs, axis=0), kg, (((1,), (1,)), ((), ())), preferred_element_type=F32)
        probs, rinv = [], []
        for a in range(GROUP):
            h = g * GROUP + a
            sa = s[a * m_rows:(a + 1) * m_rows] + bias_of(h)
            if extra_mask is not None:
                sa = jnp.where(extra_mask, NEG_INF, sa)
            sink = sink_of(h)
            m = jnp.maximum(jnp.max(sa, axis=1, keepdims=True), sink)
            p = jnp.exp(sa - m)
            den = jnp.sum(p, axis=1, keepdims=True) + jnp.exp(sink - m)
            probs.append(p.astype(BF16))
            rinv.append(1.0 / den)
        v_pair = jnp.concatenate([v_lo, v_hi], axis=0)
        for sp in range(GROUP // 2):
            o = _mm(jnp.concatenate([probs[2 * sp], probs[2 * sp + 1]], axis=1), v_pair)
            slabs.append(o * jnp.where(low_q, rinv[2 * sp], rinv[2 * sp + 1]))
    return jnp.concatenate(slabs, axis=1)


def _attn_p_kernel(sink_ref, q_ref, k_ref, v_ref, x_ref, bias_ref, wo_ref, y_ref, kbuf, vbuf, obuf):
    t = q_ref.shape[0]
    i = pl.program_id(0)

    @pl.when(i == 0)
    def _():
        kbuf[0:WINDOW, :] = jnp.zeros((WINDOW, 2 * KV_DIM), BF16)
        vbuf[0:WINDOW, :] = jnp.zeros((WINDOW, 2 * KV_DIM), BF16)

    @pl.when(i > 0)
    def _():
        kbuf[0:WINDOW, :] = kbuf[t:t + WINDOW, :]
        vbuf[0:WINDOW, :] = vbuf[t:t + WINDOW, :]

    kbuf[WINDOW:, :] = _dup_heads(k_ref[...])
    vbuf[WINDOW:, :] = _dup_heads(v_ref[...])
    col = lax.broadcasted_iota(jnp.int32, (WINDOW, 2 * WINDOW), 1)
    for j in range(t // WINDOW):
        rows = slice(j * WINDOW, (j + 1) * WINDOW)
        keys = slice(j * WINDOW, (j + 2) * WINDOW)
        extra = jnp.logical_and(i == 0, col < WINDOW) if j == 0 else None
        o = _attend(q_ref[rows, :], kbuf[keys, :], vbuf[keys, :], lambda h: bias_ref[h], lambda h: sink_ref[h], extra)
        obuf[rows, :] = o.astype(BF16)
    y_ref[...] = x_ref[...] + _mm(obuf[...], wo_ref[...])


def _attn_p(sinks, q, k, v, x, bias, wo):
    n = q.shape[0]
    t = _row_tile(n)
    row = lambda i: (i, 0)
    return pl.pallas_call(
        _attn_p_kernel,
        grid=(n // t,),
        in_specs=[pl.BlockSpec(memory_space=pltpu.SMEM),
                  pl.BlockSpec((t, N_HEADS * HEAD_DIM), row), pl.BlockSpec((t, KV_DIM), row),
                  pl.BlockSpec((t, KV_DIM), row), pl.BlockSpec((t, D_MODEL), row),
                  _full(bias.shape), _full(wo.shape)],
        out_specs=pl.BlockSpec((t, D_MODEL), row),
        out_shape=jax.ShapeDtypeStruct((n, D_MODEL), F32),
        scratch_shapes=[pltpu.VMEM((t + WINDOW, 2 * KV_DIM), BF16), pltpu.VMEM((t + WINDOW, 2 * KV_DIM), BF16),
                        pltpu.VMEM((t, N_HEADS * HEAD_DIM), BF16)],
        compiler_params=_params("arbitrary"),
        name="attn_prompt",
    )(sinks, q, k, v, x, bias, wo)


def _attn_s_kernel(sink_ref, q_ref, k_ref, v_ref, ck_ref, cv_ref, bias_ref, o_ref, nk_ref, nv_ref, shift_buf):
    pad = bias_ref.shape[2] - ck_ref.shape[0] - k_ref.shape[0]
    zpad = jnp.zeros((pad, KV_DIM), F32)
    k2 = _dup_heads(jnp.concatenate([ck_ref[...], k_ref[...], zpad], axis=0))
    v2 = _dup_heads(jnp.concatenate([cv_ref[...], v_ref[...], zpad], axis=0))
    o = _attend(q_ref[...], k2, v2, lambda h: bias_ref[h], lambda h: sink_ref[h], None)
    o_ref[...] = o.astype(BF16)
    t_new = k_ref.shape[0] // SAMPLE_SEQS
    for new_ref, old_ref, add_ref in ((nk_ref, ck_ref, k_ref), (nv_ref, cv_ref, v_ref)):
        for b in range(SAMPLE_SEQS):
            r = b * WINDOW
            shift_buf[0:WINDOW - t_new, :] = old_ref[r + t_new:r + WINDOW, :]
            shift_buf[WINDOW - t_new:WINDOW, :] = add_ref[b * t_new:(b + 1) * t_new, :]
            new_ref[b] = jnp.transpose(shift_buf[...])


def _attn_s(sinks, q, k, v, ck, cv, bias, t_new):
    n = q.shape[0]
    rows = SAMPLE_SEQS * t_new
    crow = SAMPLE_SEQS * WINDOW
    row = lambda i: (i, 0)
    return pl.pallas_call(
        _attn_s_kernel,
        grid=(n // rows,),
        in_specs=[pl.BlockSpec(memory_space=pltpu.SMEM),
                  pl.BlockSpec((rows, N_HEADS * HEAD_DIM), row), pl.BlockSpec((rows, KV_DIM), row),
                  pl.BlockSpec((rows, KV_DIM), row), pl.BlockSpec((crow, KV_DIM), row),
                  pl.BlockSpec((crow, KV_DIM), row), _full(bias.shape)],
        out_specs=[pl.BlockSpec((rows, N_HEADS * HEAD_DIM), row),
                   pl.BlockSpec((SAMPLE_SEQS, KV_DIM, WINDOW), lambda i: (i, 0, 0)),
                   pl.BlockSpec((SAMPLE_SEQS, KV_DIM, WINDOW), lambda i: (i, 0, 0))],
        out_shape=[jax.ShapeDtypeStruct((n, N_HEADS * HEAD_DIM), BF16),
                   jax.ShapeDtypeStruct((ck.shape[0] // WINDOW, KV_DIM, WINDOW), F32),
                   jax.ShapeDtypeStruct((cv.shape[0] // WINDOW, KV_DIM, WINDOW), F32)],
        scratch_shapes=[pltpu.VMEM((WINDOW, KV_DIM), F32)],
        compiler_params=_params("arbitrary"),
        name="attn_sample",
    )(sinks, q, k, v, ck, cv, bias)


def _proj_res_kernel(o_ref, x_ref, w_ref, y_ref):
    y_ref[...] = x_ref[...] + _mm(o_ref[...], w_ref[...])


def _proj_res(o, x, w):
    n = o.shape[0]
    t = _row_tile(n)
    row = lambda i: (i, 0)
    return pl.pallas_call(
        _proj_res_kernel,
        grid=(n // t,),
        in_specs=[pl.BlockSpec((t, o.shape[1]), row), pl.BlockSpec((t, D_MODEL), row), _full(w.shape)],
        out_specs=pl.BlockSpec((t, D_MODEL), row),
        out_shape=jax.ShapeDtypeStruct((n, D_MODEL), F32),
        compiler_params=_params("arbitrary"),
        name="proj_res",
    )(o, x, w)


def _alibi_slopes():
    return np.exp2(-8.0 * np.arange(1, N_HEADS + 1, dtype=np.float64) / N_HEADS).astype(np.float32)


def _band_bias(dist, allowed):
    b = -(_alibi_slopes()[:, None, None] * dist.astype(np.float32)[None])
    return jnp.asarray(np.where(allowed[None], b, np.float32(NEG_INF)).astype(np.float32))


def _prompt_bias():
    dist = WINDOW + np.arange(WINDOW)[:, None] - np.arange(2 * WINDOW)[None, :]
    return _band_bias(dist, (dist >= 0) & (dist <= WINDOW))


def _sample_bias(t_new, n_cols):
    c = np.arange(n_cols)
    n_cache = SAMPLE_SEQS * WINDOW
    n_new = SAMPLE_SEQS * t_new
    is_cache = c < n_cache
    is_new = (c >= n_cache) & (c < n_cache + n_new)
    seq_c = np.where(is_cache, c // WINDOW, (c - n_cache) // t_new)
    pos_c = np.where(is_cache, c % WINDOW, WINDOW + (c - n_cache) % t_new)
    r = np.arange(n_new)
    seq_r, tok_r = r // t_new, r % t_new
    dist = WINDOW + tok_r[:, None] - pos_c[None, :]
    allowed = (seq_r[:, None] == seq_c[None, :]) & (is_cache | is_new)[None, :] & (dist >= 0) & (dist <= WINDOW)
    return _band_bias(dist, allowed)


def _head_indicator(n_heads):
    ch = np.arange(n_heads * HEAD_DIM) // HEAD_DIM
    ind = (ch[:, None] == np.arange(LANES)[None, :]).astype(np.float32)
    return jnp.asarray(ind, BF16), jnp.asarray(np.concatenate([ind.T, ind.T], axis=0), BF16)


def kernel(x_prompt, x_sample, state_conv, cache_k, cache_v, p_prompt, p_sample, norm_mix, norm_ffn, norm_ple,
           conv_w_in, conv_b_in, conv_w_dw, conv_b_dw, conv_ln_g, conv_ln_b, conv_w_out, conv_b_out, attn_w_qkv,
           attn_q_norm, attn_k_norm, attn_sinks, attn_w_o, moe_w_rg, moe_b_rg, moe_w_re, moe_b_re, moe_w_gate,
           moe_w_up, moe_w_down, ple_w_gate, ple_w_proj):
    bp, seq, d = x_prompt.shape
    bs, t_new, _ = x_sample.shape
    assert bp == 1 and d == D_MODEL and seq % WINDOW == 0 and bs % SAMPLE_SEQS == 0
    assert seq % (bs * t_new) == 0 and (bs * t_new) % DMA_CHUNK == 0
    depth = norm_mix.shape[0]
    row2 = lambda a: a.reshape(1, -1)

    y_p = x_prompt.reshape(seq, d)
    y_s = x_sample.reshape(bs * t_new, d)
    conv_p, conv_s, k_p, v_p, k_s, v_s = [], [], [], [], [], []

    for i in range(depth):
        j = i // 2
        g_mix = row2(norm_mix[i])
        if i % 2 == 0:
            w_in = conv_w_in[j].astype(BF16)
            b_in = row2(conv_b_in[j])
            tail = (conv_w_dw[j], row2(conv_b_dw[j]), row2(conv_ln_g[j]), row2(conv_ln_b[j]),
                    conv_w_out[j].astype(BF16), row2(conv_b_out[j]))
            u_p = _conv_in(y_p, g_mix, w_in, b_in)
            u_s = _conv_in(y_s, g_mix, w_in, b_in)
            conv_p.append(u_p[seq - CONV_STATE:].reshape(1, CONV_STATE, d))
            y_p = _conv_out_p(u_p, y_p, *tail)
            ys_t, new_state = _conv_out_s(state_conv[j], u_s.reshape(bs, t_new, d),
                                          y_s.reshape(bs, t_new, d).transpose(1, 0, 2), *tail)
            conv_s.append(new_state)
            y_s = ys_t.transpose(1, 0, 2).reshape(bs * t_new, d)
        else:
            w_qkv = attn_w_qkv[j].astype(BF16)
            w_o = attn_w_o[j].astype(BF16)
            iq, iqt = _head_indicator(N_HEADS)
            ik, ikt = _head_indicator(N_KV_HEADS)
            qg = row2(jnp.tile(attn_q_norm[j], N_HEADS))
            kg = row2(jnp.tile(attn_k_norm[j], N_KV_HEADS))
            sinks = attn_sinks[j]
            q1, k1, v1 = _qkv(y_p, g_mix, w_qkv, iq, iqt, ik, ikt, qg, kg)
            q2, k2, v2 = _qkv(y_s, g_mix, w_qkv, iq, iqt, ik, ikt, qg, kg)
            k_p.append(k1[seq - WINDOW:].reshape(1, WINDOW, N_KV_HEADS, HEAD_DIM))
            v_p.append(v1[seq - WINDOW:].reshape(1, WINDOW, N_KV_HEADS, HEAD_DIM))
            y_p = _attn_p(sinks, q1, k1, v1, y_p, _prompt_bias(), w_o)
            n_cols = -(-(SAMPLE_SEQS * (WINDOW + t_new)) // LANES) * LANES
            o_s, nk, nv = _attn_s(sinks, q2, k2, v2, cache_k[j].reshape(bs * WINDOW, KV_DIM),
                                  cache_v[j].reshape(bs * WINDOW, KV_DIM), _sample_bias(t_new, n_cols), t_new)
            k_s.append(nk.reshape(bs, N_KV_HEADS, HEAD_DIM, WINDOW).transpose(0, 3, 1, 2))
            v_s.append(nv.reshape(bs, N_KV_HEADS, HEAD_DIM, WINDOW).transpose(0, 3, 1, 2))
            y_s = _proj_res(o_s, y_s, w_o)

        pad = ROUTER_LANES - N_EXPERT_GROUPS - N_EXPERTS
        w_r = jnp.concatenate([moe_w_rg[i], moe_w_re[i], jnp.zeros((d, pad), F32)], axis=1)
        b_r = jnp.concatenate([moe_b_rg[i], moe_b_re[i], jnp.zeros((pad,), F32)]).reshape(1, ROUTER_LANES)
        moe = (row2(norm_ffn[i]), jnp.concatenate(_split_bf16(w_r), axis=1), b_r, moe_w_gate, moe_w_up, moe_w_down)
        ple = (row2(norm_ple[i]), ple_w_gate[i].astype(BF16), ple_w_proj[i].astype(BF16))
        y_p, y_s = _moe_ple(i, y_p, y_s, *moe, p_prompt.reshape(depth, seq, PLE_DIM),
                            p_sample[i].reshape(bs * t_new, PLE_DIM), *ple)

    return (y_p.reshape(1, seq, d), y_s.reshape(bs, t_new, d), jnp.stack(conv_p), jnp.stack(conv_s),
            jnp.stack(k_p), jnp.stack(v_p), jnp.stack(k_s), jnp.stack(v_s))
```
